```python
import math
import jax, jax.numpy as jnp
from jax import lax
import numpy as np

D_MODEL = 1024
BATCH = 8
SEQ = 2048
DEPTH = 1
DEC_BATCH = 128
DEC_SEQ = 4
PAST_LEN = 16384
PAGE_SIZE = 128

MIX_WIDTH = D_MODEL
SSM_WIDTH = MIX_WIDTH // 2
SSM_GROUP = 16
SSM_GROUPS = SSM_WIDTH // SSM_GROUP
SSM_STATE = 64
RET_WIDTH = MIX_WIDTH - SSM_WIDTH
RET_HEADS = 4
RET_HEAD_DIM = RET_WIDTH // RET_HEADS
RET_CHUNK = 128
ROPE_BASE = 10000.0
MEM_LEN = 256
MEM_HEADS = 4
MEM_HEAD_DIM = D_MODEL // MEM_HEADS
D_FF = 4 * D_MODEL
PROJ_WIDTH = SSM_WIDTH + 4 * RET_WIDTH
EPS = 1e-6
DT_MIN = 1e-3
DT_MAX = 1e-1

kernel_name = "hymba_s5_retnet_memxattn_step"

F32 = jnp.float32


def rmsnorm(x, g):
    xf = x.astype(F32)
    y = xf * lax.rsqrt(jnp.mean(xf * xf, axis=-1, keepdims=True) + EPS)
    return (y * g.astype(F32)).astype(x.dtype)


def s5_discretize(lam_re, lam_im, log_dt, b_re, b_im):
    lr = lam_re.astype(F32)
    li = lam_im.astype(F32)
    dt = jnp.exp(log_dt.astype(F32))[:, None]
    mag = jnp.exp(lr * dt)
    ab_re = mag * jnp.cos(li * dt)
    ab_im = mag * jnp.sin(li * dt)
    den = lr * lr + li * li
    f_re = ((ab_re - 1.0) * lr + ab_im * li) / den
    f_im = (ab_im * lr - (ab_re - 1.0) * li) / den
    br = b_re.astype(F32)
    bi = b_im.astype(F32)
    bb_re = f_re[..., None] * br - f_im[..., None] * bi
    bb_im = f_re[..., None] * bi + f_im[..., None] * br
    return ab_re, ab_im, bb_re, bb_im


def _cplx_scan_op(e1, e2):
    a1r, a1i, b1r, b1i = e1
    a2r, a2i, b2r, b2i = e2
    ar = a2r * a1r - a2i * a1i
    ai = a2r * a1i + a2i * a1r
    br = a2r * b1r - a2i * b1i + b2r
    bi = a2r * b1i + a2i * b1r + b2i
    return ar, ai, br, bi


def s5_mixer(u, h0_re, h0_im, lam_re, lam_im, log_dt, b_re, b_im, c_re, c_im, d_skip, w_glu):
    bsz, L, _ = u.shape
    uf = u.astype(F32)
    ug = uf.reshape(bsz, L, SSM_GROUPS, SSM_GROUP)
    ab_re, ab_im, bb_re, bb_im = s5_discretize(lam_re, lam_im, log_dt, b_re, b_im)
    bu_re = jnp.einsum('blgh,gph->blgp', ug, bb_re)
    bu_im = jnp.einsum('blgh,gph->blgp', ug, bb_im)
    a_re = jnp.broadcast_to(ab_re, bu_re.shape)
    a_im = jnp.broadcast_to(ab_im, bu_im.shape)
    ac_re, ac_im, s_re, s_im = lax.associative_scan(
        _cplx_scan_op, (a_re, a_im, bu_re, bu_im), axis=1)
    h0r = h0_re.astype(F32)[:, None]
    h0i = h0_im.astype(F32)[:, None]
    x_re = s_re + ac_re * h0r - ac_im * h0i
    x_im = s_im + ac_re * h0i + ac_im * h0r
    y = (jnp.einsum('blgp,ghp->blgh', x_re, c_re.astype(F32))
         - jnp.einsum('blgp,ghp->blgh', x_im, c_im.astype(F32)))
    y = y.reshape(bsz, L, SSM_WIDTH) + d_skip.astype(F32) * uf
    z = jax.nn.gelu(y)
    out = z * jax.nn.sigmoid(z @ w_glu.astype(F32))
    return out.astype(u.dtype), x_re[:, -1], x_im[:, -1]


def rope(x, pos):
    half = x.shape[-1] // 2
    inv = ROPE_BASE ** (-jnp.arange(half, dtype=F32) / half)
    ang = pos[:, None] * inv[None, :]
    cos = jnp.cos(ang)[None, :, None, :]
    sin = jnp.sin(ang)[None, :, None, :]
    x1 = x[..., :half]
    x2 = x[..., half:]
    return jnp.concatenate([x1 * cos - x2 * sin, x1 * sin + x2 * cos], axis=-1)


def retention_mixer(q, k, v, g, s0, pos0, gn_gain):
    bsz, L, _ = q.shape
    H, dh = RET_HEADS, RET_HEAD_DIM
    C = RET_CHUNK if L % RET_CHUNK == 0 else L
    n = L // C
    pos = pos0 + jnp.arange(L, dtype=F32)
    qh = rope(q.astype(F32).reshape(bsz, L, H, dh), pos) * (dh ** -0.5)
    kh = rope(k.astype(F32).reshape(bsz, L, H, dh), pos)
    vh = v.astype(F32).reshape(bsz, L, H, dh)
    qc = qh.reshape(bsz, n, C, H, dh)
    kc = kh.reshape(bsz, n, C, H, dh)
    vc = vh.reshape(bsz, n, C, H, dh)
    log_gamma = jnp.log(1.0 - 2.0 ** (-5.0 - jnp.arange(H, dtype=F32)))
    idx = jnp.arange(C, dtype=F32)
    diff = idx[:, None] - idx[None, :]
    decay_mask = jnp.where(diff[None] >= 0,
                           jnp.exp(jnp.maximum(diff, 0.0)[None] * log_gamma[:, None, None]),
                           0.0)
    scores = jnp.einsum('bnqhd,bnkhd->bnhqk', qc, kc) * decay_mask[None, None]
    inner = jnp.einsum('bnhqk,bnkhe->bnqhe', scores, vc)
    zeta = jnp.exp((C - 1.0 - idx)[:, None] * log_gamma[None, :])
    kv = jnp.einsum('bnkhd,bnkhe->nbhde', kc * zeta[None, None, :, :, None], vc)
    gamma_c = jnp.exp(C * log_gamma)[None, :, None, None]

    def step(s, kv_i):
        return s * gamma_c + kv_i, s

    s_final, s_prev = lax.scan(step, s0.astype(F32), kv)
    xi = jnp.exp((idx + 1.0)[:, None] * log_gamma[None, :])
    cross = jnp.einsum('bnqhd,nbhde->bnqhe', qc, s_prev) * xi[None, None, :, :, None]
    o = (inner + cross).reshape(bsz, L, H, dh)
    mu = jnp.mean(o, axis=-1, keepdims=True)
    var = jnp.mean(jnp.square(o - mu), axis=-1, keepdims=True)
    o = ((o - mu) * lax.rsqrt(var + EPS)).reshape(bsz, L, RET_WIDTH) * gn_gain.astype(F32)
    out = jax.nn.silu(g.astype(F32)) * o
    return out.astype(q.dtype), s_final


def memory_kv(mem, g_mem, w_mk, w_mv):
    bsz = mem.shape[0]
    m = rmsnorm(mem, g_mem)
    mk = (m @ w_mk).reshape(bsz, MEM_LEN, MEM_HEADS, MEM_HEAD_DIM)
    mv = (m @ w_mv).reshape(bsz, MEM_LEN, MEM_HEADS, MEM_HEAD_DIM)
    return mk, mv


def memory_attend(h, mk, mv, w_mq, w_mo):
    bsz, L, _ = h.shape
    q = (h @ w_mq).reshape(bsz, L, MEM_HEADS, MEM_HEAD_DIM).astype(F32)
    s = jnp.einsum('blhd,bmhd->bhlm', q, mk.astype(F32)) * (MEM_HEAD_DIM ** -0.5)
    p = jax.nn.softmax(s, axis=-1)
    o = jnp.einsum('bhlm,bmhd->blhd', p, mv.astype(F32)).reshape(bsz, L, D_MODEL)
    return o.astype(h.dtype) @ w_mo


def decoder_layer(x, mk, mv, s5_re, s5_im, ret_s, pos0, p):
    h = rmsnorm(x, p['g_mix'])
    proj = h @ p['w_in']
    u = proj[..., :SSM_WIDTH]
    q = proj[..., SSM_WIDTH:SSM_WIDTH + RET_WIDTH]
    k = proj[..., SSM_WIDTH + RET_WIDTH:SSM_WIDTH + 2 * RET_WIDTH]
    v = proj[..., SSM_WIDTH + 2 * RET_WIDTH:SSM_WIDTH + 3 * RET_WIDTH]
    g = proj[..., SSM_WIDTH + 3 * RET_WIDTH:]
    ssm_out, s5_re_new, s5_im_new = s5_mixer(
        u, s5_re, s5_im, p['lam_re'], p['lam_im'], p['log_dt'], p['b_re'], p['b_im'],
        p['c_re'], p['c_im'], p['d_skip'], p['w_glu'])
    ret_out, ret_new = retention_mixer(q, k, v, g, ret_s, pos0, p['ret_gn'])
    x = x + jnp.concatenate([ssm_out, ret_out], axis=-1) @ p['w_out']
    x = x + memory_attend(rmsnorm(x, p['g_xattn']), mk, mv, p['w_mq'], p['w_mo'])
    h = rmsnorm(x, p['g_mlp'])
    x = x + jnp.square(jax.nn.relu(h @ p['w_up'])) @ p['w_down']
    return x, s5_re_new, s5_im_new, ret_new


def setup_inputs(seed: int = 0) -> dict:
    key = jax.random.key(seed)
    ks = jax.random.split(key, 32)
    nrm = jax.random.normal
    G, P, Hg = SSM_GROUPS, SSM_STATE, SSM_GROUP
    lam_im_base = jnp.pi * jnp.arange(P, dtype=F32)
    return {
        'x_prompt': nrm(ks[0], (BATCH, SEQ, D_MODEL), F32),
        'x_sample': nrm(ks[1], (DEC_BATCH, DEC_SEQ, D_MODEL), F32),
        'mem_prompt': nrm(ks[2], (BATCH, MEM_LEN, D_MODEL), F32),
        'state_s5_re': 0.3 * nrm(ks[3], (DEPTH, DEC_BATCH, G, P), F32),
        'state_s5_im': 0.3 * nrm(ks[4], (DEPTH, DEC_BATCH, G, P), F32),
        'state_ret': 3.0 * nrm(ks[5], (DEPTH, DEC_BATCH, RET_HEADS, RET_HEAD_DIM, RET_HEAD_DIM), F32),
        'cache_mem_k': nrm(ks[6], (DEPTH, DEC_BATCH, MEM_LEN, MEM_HEADS, MEM_HEAD_DIM), F32),
        'cache_mem_v': nrm(ks[7], (DEPTH, DEC_BATCH, MEM_LEN, MEM_HEADS, MEM_HEAD_DIM), F32),
        'g_mix': 1.0 + 0.02 * nrm(ks[8], (DEPTH, D_MODEL), F32),
        'w_in': nrm(ks[9], (DEPTH, D_MODEL, PROJ_WIDTH), F32) * D_MODEL ** -0.5,
        'lam_re': -0.5 + 0.01 * nrm(ks[10], (DEPTH, G, P), F32),
        'lam_im': lam_im_base + 0.01 * nrm(ks[11], (DEPTH, G, P), F32),
        'log_dt': jax.random.uniform(ks[12], (DEPTH, G), F32, math.log(DT_MIN), math.log(DT_MAX)),
        'b_re': nrm(ks[13], (DEPTH, G, P, Hg), F32) * (2 * Hg) ** -0.5,
        'b_im': nrm(ks[14], (DEPTH, G, P, Hg), F32) * (2 * Hg) ** -0.5,
        'c_re': nrm(ks[15], (DEPTH, G, Hg, P), F32) * P ** -0.5,
        'c_im': nrm(ks[16], (DEPTH, G, Hg, P), F32) * P ** -0.5,
        'd_skip': nrm(ks[17], (DEPTH, SSM_WIDTH), F32),
        'w_glu': nrm(ks[18], (DEPTH, SSM_WIDTH, SSM_WIDTH), F32) * SSM_WIDTH ** -0.5,
        'ret_gn': 1.0 + 0.02 * nrm(ks[19], (DEPTH, RET_WIDTH), F32),
        'w_out': nrm(ks[20], (DEPTH, MIX_WIDTH, D_MODEL), F32) * MIX_WIDTH ** -0.5,
        'g_xattn': 1.0 + 0.02 * nrm(ks[21], (DEPTH, D_MODEL), F32),
        'g_mem': 1.0 + 0.02 * nrm(ks[22], (DEPTH, D_MODEL), F32),
        'w_mq': nrm(ks[23], (DEPTH, D_MODEL, D_MODEL), F32) * D_MODEL ** -0.5,
        'w_mk': nrm(ks[24], (DEPTH, D_MODEL, D_MODEL), F32) * D_MODEL ** -0.5,
        'w_mv': nrm(ks[25], (DEPTH, D_MODEL, D_MODEL), F32) * D_MODEL ** -0.5,
        'w_mo': nrm(ks[26], (DEPTH, D_MODEL, D_MODEL), F32) * D_MODEL ** -0.5,
        'g_mlp': 1.0 + 0.02 * nrm(ks[27], (DEPTH, D_MODEL), F32),
        'w_up': nrm(ks[28], (DEPTH, D_MODEL, D_FF), F32) * D_MODEL ** -0.5,
        'w_down': nrm(ks[29], (DEPTH, D_FF, D_MODEL), F32) * D_FF ** -0.5,
        'g_final': 1.0 + 0.02 * nrm(ks[30], (D_MODEL,), F32),
    }


def reference(x_prompt, x_sample, mem_prompt, state_s5_re, state_s5_im, state_ret,
              cache_mem_k, cache_mem_v, g_mix, w_in, lam_re, lam_im, log_dt, b_re, b_im,
              c_re, c_im, d_skip, w_glu, ret_gn, w_out, g_xattn, g_mem, w_mq, w_mk, w_mv,
              w_mo, g_mlp, w_up, w_down, g_final):
    bp = x_prompt.shape[0]
    bs = x_sample.shape[0]
    xp = x_prompt
    xs = x_sample
    s5r_p_l, s5i_p_l, ret_p_l, mk_p_l, mv_p_l = [], [], [], [], []
    s5r_s_l, s5i_s_l, ret_s_l = [], [], []
    for l in range(DEPTH):
        p = dict(g_mix=g_mix[l], w_in=w_in[l], lam_re=lam_re[l], lam_im=lam_im[l],
                 log_dt=log_dt[l], b_re=b_re[l], b_im=b_im[l], c_re=c_re[l], c_im=c_im[l],
                 d_skip=d_skip[l], w_glu=w_glu[l], ret_gn=ret_gn[l], w_out=w_out[l],
                 g_xattn=g_xattn[l], w_mq=w_mq[l], w_mo=w_mo[l], g_mlp=g_mlp[l],
                 w_up=w_up[l], w_down=w_down[l])
        mk_p, mv_p = memory_kv(mem_prompt, g_mem[l], w_mk[l], w_mv[l])
        zs = jnp.zeros((bp, SSM_GROUPS, SSM_STATE), F32)
        zr = jnp.zeros((bp, RET_HEADS, RET_HEAD_DIM, RET_HEAD_DIM), F32)
        xp, s5r_p, s5i_p, ret_p = decoder_layer(xp, mk_p, mv_p, zs, zs, zr, 0.0, p)
        xs, s5r_s, s5i_s, ret_s = decoder_layer(
            xs, cache_mem_k[l], cache_mem_v[l], state_s5_re[l], state_s5_im[l],
            state_ret[l], float(PAST_LEN), p)
        s5r_p_l.append(s5r_p)
        s5i_p_l.append(s5i_p)
        ret_p_l.append(ret_p)
        mk_p_l.append(mk_p)
        mv_p_l.append(mv_p)
        s5r_s_l.append(s5r_s)
        s5i_s_l.append(s5i_s)
        ret_s_l.append(ret_s)
    y_prompt = rmsnorm(xp, g_final)
    y_sample = rmsnorm(xs, g_final)
    new_s5_re_prompt = jnp.stack(s5r_p_l, axis=0)
    new_s5_im_prompt = jnp.stack(s5i_p_l, axis=0)
    new_ret_prompt = jnp.stack(ret_p_l, axis=0)
    new_mem_k_prompt = jnp.stack(mk_p_l, axis=0)
    new_mem_v_prompt = jnp.stack(mv_p_l, axis=0)
    new_s5_re_sample = jnp.stack(s5r_s_l, axis=0)
    new_s5_im_sample = jnp.stack(s5i_s_l, axis=0)
    new_ret_sample = jnp.stack(ret_s_l, axis=0)
    return (y_prompt, y_sample, new_s5_re_prompt, new_s5_im_prompt, new_ret_prompt,
            new_mem_k_prompt, new_mem_v_prompt, new_s5_re_sample, new_s5_im_sample,
            new_ret_sample)
```

```python
import functools
import math

import numpy as np
import jax
import jax.numpy as jnp
from jax import lax
from jax.experimental import pallas as pl
from jax.experimental.pallas import tpu as pltpu

F32 = jnp.float32
BF16 = jnp.bfloat16

D_MODEL = 1024
SSM_WIDTH = 512
SSM_GROUP = 16
SSM_GROUPS = 32
SSM_STATE = 64
SSM_LANES = SSM_GROUPS * SSM_STATE
RET_WIDTH = 512
RET_HEADS = 4
RET_HEAD_DIM = 128
RET_CHUNK = 128
ROPE_BASE = 10000.0
MEM_LEN = 256
MEM_HEADS = 4
MEM_HEAD_DIM = 256
D_FF = 4096
PROJ_WIDTH = SSM_WIDTH + 4 * RET_WIDTH
EPS = 1e-6
PAST_LEN = 16384

BF16_ROWS = 16
VMEM_LIMIT = 56 * 1024 * 1024


def _dot(a, b):
    return jnp.dot(a, b, preferred_element_type=F32)


def _dot_nt(a, b):
    return lax.dot_general(a, b, (((1,), (1,)), ((), ())), preferred_element_type=F32)


def _dot_tn(a, b):
    return lax.dot_general(a, b, (((0,), (0,)), ((), ())), preferred_element_type=F32)


def _rmsnorm(x, g):
    return x * lax.rsqrt(jnp.mean(x * x, axis=-1, keepdims=True) + EPS) * g


def _sigmoid(x):
    return 1.0 / (1.0 + jnp.exp(-x))


def _gelu_tanh(x):
    c = math.sqrt(2.0 / math.pi)
    return x * (0.5 * (1.0 + jnp.tanh(c * (x + 0.044715 * (x * x * x)))))


def _params(*sem):
    return pltpu.CompilerParams(dimension_semantics=sem, vmem_limit_bytes=VMEM_LIMIT)


def _full(shape):
    return pl.BlockSpec(shape, lambda *_: (0,) * len(shape))


def _disc_kernel(lr_ref, li_ref, ldt_ref, br_ref, bi_ref, abr_ref, abi_ref, bbr_ref, bbi_ref):
    lr = lr_ref[...]
    li = li_ref[...]
    dt = jnp.exp(ldt_ref[...])
    mag = jnp.exp(lr * dt)
    ab_re = mag * jnp.cos(li * dt)
    ab_im = mag * jnp.sin(li * dt)
    den = lr * lr + li * li
    f_re = ((ab_re - 1.0) * lr + ab_im * li) / den
    f_im = (ab_im * lr - (ab_re - 1.0) * li) / den
    br = br_ref[...]
    bi = bi_ref[...]
    abr_ref[...] = ab_re
    abi_ref[...] = ab_im
    bbr_ref[...] = f_re * br - f_im * bi
    bbi_ref[...] = f_re * bi + f_im * br


def _discretize(lam_re, lam_im, log_dt, b_re, b_im):
    rows = SSM_GROUPS * SSM_GROUP
    rep = lambda a: jnp.repeat(a, SSM_GROUP, axis=0)
    ldt = jnp.broadcast_to(log_dt[:, None], (SSM_GROUPS, SSM_STATE))
    bt = lambda b: b.transpose(0, 2, 1).reshape(rows, SSM_STATE)
    shp = jax.ShapeDtypeStruct((rows, SSM_STATE), F32)
    abr, abi, bbr, bbi = pl.pallas_call(
        _disc_kernel, out_shape=(shp, shp, shp, shp), name="s5_discretize",
    )(rep(lam_re), rep(lam_im), rep(ldt), bt(b_re), bt(b_im))
    ab_re = abr[::SSM_GROUP].reshape(1, SSM_LANES)
    ab_im = abi[::SSM_GROUP].reshape(1, SSM_LANES)
    return ab_re, ab_im, bbr, bbi


def _block_diag_in(bbt):
    row_g = np.arange(SSM_WIDTH)[:, None] // SSM_GROUP
    col_g = np.arange(SSM_LANES)[None, :] // SSM_STATE
    return jnp.where(row_g == col_g, jnp.tile(bbt, (1, SSM_GROUPS)), 0.0)


def _block_diag_out(c):
    ct = c.transpose(0, 2, 1).reshape(SSM_LANES, SSM_GROUP)
    row_g = np.arange(SSM_LANES)[:, None] // SSM_STATE
    col_g = np.arange(SSM_WIDTH)[None, :] // SSM_GROUP
    return jnp.where(row_g == col_g, jnp.tile(ct, (1, SSM_GROUPS)), 0.0)


def _proj_kernel(x_ref, g_ref, w_ref, u_ref, q_ref, k_ref, v_ref, gate_ref):
    h = _rmsnorm(x_ref[...], g_ref[...]).astype(BF16)
    proj = _dot(h, w_ref[...])
    u_ref[...] = proj[:, :SSM_WIDTH]
    q_ref[...] = proj[:, SSM_WIDTH:SSM_WIDTH + RET_WIDTH]
    k_ref[...] = proj[:, SSM_WIDTH + RET_WIDTH:SSM_WIDTH + 2 * RET_WIDTH]
    v_ref[...] = proj[:, SSM_WIDTH + 2 * RET_WIDTH:SSM_WIDTH + 3 * RET_WIDTH].astype(BF16)
    gate_ref[...] = proj[:, SSM_WIDTH + 3 * RET_WIDTH:]


def _project(x, g_mix, w_in):
    t = x.shape[0]
    tm = min(t, 512)
    row = lambda w: pl.BlockSpec((tm, w), lambda i: (i, 0))
    f = jax.ShapeDtypeStruct((t, 512), F32)
    return pl.pallas_call(
        _proj_kernel,
        grid=(t // tm,),
        in_specs=[row(D_MODEL), _full((1, D_MODEL)), _full((D_MODEL, PROJ_WIDTH))],
        out_specs=[row(512)] * 5,
        out_shape=(f, f, f, jax.ShapeDtypeStruct((t, 512), BF16), f),
        compiler_params=_params("parallel"),
        name="in_proj",
    )(x, g_mix, w_in)


def _s5_kernel(u_ref, h0r_ref, h0i_ref, abr_ref, abi_ref, bre_ref, bim_ref, cre_ref, ncim_ref,
               d_ref, wglu_ref, out_ref, hr_ref, hi_ref, bur_scr, bui_scr, xr_scr, xi_scr,
               *, batch, steps, unroll, width):
    half = SSM_LANES // 2
    kin = SSM_WIDTH // 2
    kout = SSM_WIDTH // 2

    @pl.when(pl.program_id(0) == 0)
    def _():
        hr_ref[...] = h0r_ref[...]
        hi_ref[...] = h0i_ref[...]

    u = u_ref[...]
    ub = u.astype(BF16)
    ys = []
    for p in range(2):
        lanes = slice(half * p, half * (p + 1))
        uk = ub[:, kin * p:kin * (p + 1)]
        bur_scr[...] = _dot(uk, bre_ref[kin * p:kin * (p + 1), lanes])
        bui_scr[...] = _dot(uk, bim_ref[kin * p:kin * (p + 1), lanes])
        for j in range(half // width):
            loc = slice(j * width, (j + 1) * width)
            glob = slice(half * p + j * width, half * p + (j + 1) * width)
            ar = jnp.broadcast_to(abr_ref[:, glob], (batch, width))
            ai = jnp.broadcast_to(abi_ref[:, glob], (batch, width))

            def body(i, carry, loc=loc, ar=ar, ai=ai):
                xr, xi = carry
                rows = unroll * batch
                base = pl.multiple_of(i * rows, rows)
                out_r, out_i = [], []
                for s in range(unroll):
                    br = bur_scr[pl.ds(base + s * batch, batch), loc]
                    bi = bui_scr[pl.ds(base + s * batch, batch), loc]
                    xr, xi = ar * xr - ai * xi + br, ar * xi + ai * xr + bi
                    out_r.append(xr)
                    out_i.append(xi)
                xr_scr[pl.ds(base, rows), loc] = jnp.concatenate(out_r, axis=0).astype(BF16)
                xi_scr[pl.ds(base, rows), loc] = jnp.concatenate(out_i, axis=0).astype(BF16)
                return xr, xi

            xr, xi = lax.fori_loop(0, steps // unroll, body, (hr_ref[:, glob], hi_ref[:, glob]))
            hr_ref[:, glob] = xr
            hi_ref[:, glob] = xi
        cols = slice(kout * p, kout * (p + 1))
        ys.append(_dot(xr_scr[...], cre_ref[lanes, cols]) + _dot(xi_scr[...], ncim_ref[lanes, cols]))
    y = jnp.concatenate(ys, axis=1) + d_ref[...] * u
    z = _gelu_tanh(y)
    out_ref[...] = (z * _sigmoid(_dot(z.astype(BF16), wglu_ref[...]))).astype(BF16)


def _s5_mixer(u_tb, h0_re, h0_im, ab_re, ab_im, b_re, b_im, c_re, nc_im, d_skip, w_glu, *, batch, length):
    steps = min(length, 128)
    unroll = max(1, min(steps, 4 * 8 // batch))
    if (unroll * batch) % BF16_ROWS:
        raise ValueError("S5 scan stores need whole bf16 tiles")
    width = 512 if batch <= 8 else 128
    rows = steps * batch
    half = SSM_LANES // 2
    kern = functools.partial(_s5_kernel, batch=batch, steps=steps, unroll=unroll, width=width)
    st = jax.ShapeDtypeStruct((batch, SSM_LANES), F32)
    return pl.pallas_call(
        kern,
        grid=(length // steps,),
        in_specs=[pl.BlockSpec((rows, SSM_WIDTH), lambda i: (i, 0)),
                  _full((batch, SSM_LANES)), _full((batch, SSM_LANES)),
                  _full((1, SSM_LANES)), _full((1, SSM_LANES)),
                  _full((SSM_WIDTH, SSM_LANES)), _full((SSM_WIDTH, SSM_LANES)),
                  _full((SSM_LANES, SSM_WIDTH)), _full((SSM_LANES, SSM_WIDTH)),
                  _full((1, SSM_WIDTH)), _full((SSM_WIDTH, SSM_WIDTH))],
        out_specs=[pl.BlockSpec((rows, SSM_WIDTH), lambda i: (i, 0)),
                   _full((batch, SSM_LANES)), _full((batch, SSM_LANES))],
        out_shape=(jax.ShapeDtypeStruct((length * batch, SSM_WIDTH), BF16), st, st),
        scratch_shapes=[pltpu.VMEM((rows, half), F32), pltpu.VMEM((rows, half), F32),
                        pltpu.VMEM((rows, half), BF16), pltpu.VMEM((rows, half), BF16)],
        compiler_params=_params("arbitrary"),
        name="s5_mixer",
    )(u_tb, h0_re, h0_im, ab_re, ab_im, b_re, b_im, c_re, nc_im, d_skip, w_glu)


def _ret_tables(length, padded, pos0):
    chunk = RET_CHUNK if length % RET_CHUNK == 0 else length
    cpad = chunk if padded == length else padded
    lg = np.log(1.0 - 2.0 ** (-5.0 - np.arange(RET_HEADS, dtype=np.float64)))
    idx = np.arange(chunk, dtype=np.float64)
    diff = idx[:, None] - idx[None, :]
    mask = np.where(diff[None] >= 0, np.exp(np.maximum(diff, 0.0)[None] * lg[:, None, None]), 0.0)
    zeta = np.exp((chunk - 1.0 - idx)[None, :] * lg[:, None])
    xi = np.exp((idx + 1.0)[None, :] * lg[:, None])
    gamma_c = tuple(float(v) for v in np.exp(chunk * lg))
    pad2 = lambda a: np.pad(a, ((0, 0), (0, cpad - chunk), (0, cpad - chunk)))
    lane = lambda a: np.broadcast_to(np.pad(a, ((0, 0), (0, cpad - chunk)))[:, :, None],
                                     (RET_HEADS, cpad, RET_HEAD_DIM))
    half = RET_HEAD_DIM // 2
    inv = ROPE_BASE ** (-np.arange(half, dtype=np.float64) / half)
    ang = (pos0 + np.arange(length, dtype=np.float64))[:, None] * inv[None, :]
    cc = np.concatenate([np.cos(ang), np.cos(ang)], axis=1)
    ss = np.concatenate([-np.sin(ang), np.sin(ang)], axis=1)
    rows = lambda a: np.pad(a, ((0, padded - length), (0, 0)))
    f = lambda a: jnp.asarray(np.ascontiguousarray(a), dtype=F32)
    return cpad, gamma_c, f(pad2(mask)), f(lane(zeta)), f(lane(xi)), f(rows(cc)), f(rows(ss))


def _ret_kernel(q_ref, k_ref, v_ref, g_ref, cc_ref, ss_ref, mask_ref, zeta_ref, xi_ref, gain_ref,
                s0_ref, o_ref, s_ref, *, bb, gamma_c):
    @pl.when(pl.program_id(1) == 0)
    def _():
        s_ref[...] = s0_ref[...]

    cc = cc_ref[...]
    ss = ss_ref[...]
    half = RET_HEAD_DIM // 2

    def rope(x):
        return x * cc + pltpu.roll(x, half, axis=1) * ss

    def per_batch(b, carry):
        outs = []
        for h in range(RET_HEADS):
            cols = slice(h * RET_HEAD_DIM, (h + 1) * RET_HEAD_DIM)
            qb = (rope(q_ref[b, :, cols]) * (RET_HEAD_DIM ** -0.5)).astype(BF16)
            kr = rope(k_ref[b, :, cols])
            kb = kr.astype(BF16)
            kz = (kr * zeta_ref[h]).astype(BF16)
            vb = v_ref[b, :, cols]
            scores = (_dot_nt(qb, kb) * mask_ref[h]).astype(BF16)
            inner = _dot(scores, vb)
            s_prev = s_ref[b, h]
            cross = _dot(qb, s_prev.astype(BF16)) * xi_ref[h]
            s_ref[b, h] = s_prev * gamma_c[h] + _dot_tn(kz, vb)
            o = inner + cross
            mu = jnp.mean(o, axis=-1, keepdims=True)
            oc = o - mu
            var = jnp.mean(oc * oc, axis=-1, keepdims=True)
            on = oc * lax.rsqrt(var + EPS) * gain_ref[:, cols]
            gate = g_ref[b, :, cols]
            outs.append((gate * _sigmoid(gate) * on).astype(BF16))
        o_ref[b] = jnp.concatenate(outs, axis=1)
        return carry

    lax.fori_loop(0, bb, per_batch, 0)


def _retention(q, k, v, g, s0, gn_gain, *, pos0):
    batch, length, _ = q.shape
    padded = length if length % BF16_ROWS == 0 else pl.cdiv(length, BF16_ROWS) * BF16_ROWS
    chunk, gamma_c, mask, zeta, xi, cc, ss = _ret_tables(length, padded, pos0)
    if padded != length:
        pad = lambda a: jnp.pad(a, ((0, 0), (0, padded - length), (0, 0)))
        q, k, v, g = pad(q), pad(k), pad(v), pad(g)
    bb = 8
    tok = pl.BlockSpec((bb, chunk, RET_WIDTH), lambda i, c: (i, c, 0))
    tab = pl.BlockSpec((chunk, RET_HEAD_DIM), lambda i, c: (c, 0))
    state = pl.BlockSpec((bb, RET_HEADS, RET_HEAD_DIM, RET_HEAD_DIM), lambda i, c: (i, 0, 0, 0))
    out, s_new = pl.pallas_call(
        functools.partial(_ret_kernel, bb=bb, gamma_c=gamma_c),
        grid=(batch // bb, padded // chunk),
        in_specs=[tok, tok, tok, tok, tab, tab,
                  _full((RET_HEADS, chunk, chunk)), _full((RET_HEADS, chunk, RET_HEAD_DIM)),
                  _full((RET_HEADS, chunk, RET_HEAD_DIM)), _full((1, RET_WIDTH)), state],
        out_specs=[tok, state],
        out_shape=(jax.ShapeDtypeStruct((batch, padded, RET_WIDTH), BF16),
                   jax.ShapeDtypeStruct(s0.shape, F32)),
        compiler_params=_params("parallel", "arbitrary"),
        name="retention",
    )(q, k, v, g, cc, ss, mask, zeta, xi, gn_gain, s0)
    return out[:, :length], s_new


def _outq_kernel(x_ref, ssm_ref, ret_ref, wout_ref, g_ref, wq_ref, x1_ref, q_ref):
    x1 = (x_ref[...] + _dot(ssm_ref[...], wout_ref[:SSM_WIDTH, :])
          + _dot(ret_ref[...], wout_ref[SSM_WIDTH:, :]))
    x1_ref[...] = x1
    q_ref[...] = _dot(_rmsnorm(x1, g_ref[...]).astype(BF16), wq_ref[...]).astype(BF16)


def _out_and_query(x, ssm, ret, w_out, g_xattn, w_mq):
    t = x.shape[0]
    tm = min(t, 512)
    row = lambda w: pl.BlockSpec((tm, w), lambda i: (i, 0))
    return pl.pallas_call(
        _outq_kernel,
        grid=(t // tm,),
        in_specs=[row(D_MODEL), row(SSM_WIDTH), row(RET_WIDTH), _full((D_MODEL, D_MODEL)),
                  _full((1, D_MODEL)), _full((D_MODEL, D_MODEL))],
        out_specs=[row(D_MODEL), row(D_MODEL)],
        out_shape=(jax.ShapeDtypeStruct((t, D_MODEL), F32), jax.ShapeDtypeStruct((t, D_MODEL), BF16)),
        compiler_params=_params("parallel"),
        name="out_proj_query",
    )(x, ssm, ret, w_out, g_xattn, w_mq)


def _memkv_kernel(m_ref, g_ref, wk_ref, wv_ref, k_ref, v_ref):
    m = _rmsnorm(m_ref[...], g_ref[...]).astype(BF16)
    k_ref[...] = _dot(m, wk_ref[...])
    v_ref[...] = _dot(m, wv_ref[...])


def _memory_kv(mem, g_mem, w_mk, w_mv):
    t = mem.shape[0]
    tm = min(t, 512)
    row = pl.BlockSpec((tm, D_MODEL), lambda i: (i, 0))
    f = jax.ShapeDtypeStruct((t, D_MODEL), F32)
    return pl.pallas_call(
        _memkv_kernel,
        grid=(t // tm,),
        in_specs=[row, _full((1, D_MODEL)), _full((D_MODEL, D_MODEL)), _full((D_MODEL, D_MODEL))],
        out_specs=[row, row],
        out_shape=(f, f),
        compiler_params=_params("parallel"),
        name="memory_kv",
    )(mem, g_mem, w_mk, w_mv)


def _attn_kernel(q_ref, k_ref, v_ref, o_ref, *, bb):
    def per_batch(b, carry):
        outs = []
        for h in range(MEM_HEADS):
            cols = slice(h * MEM_HEAD_DIM, (h + 1) * MEM_HEAD_DIM)
            s = _dot_nt(q_ref[b, :, cols], k_ref[b, :, cols].astype(BF16)) * (MEM_HEAD_DIM ** -0.5)
            e = jnp.exp(s - jnp.max(s, axis=-1, keepdims=True))
            p = e / jnp.sum(e, axis=-1, keepdims=True)
            outs.append(_dot(p.astype(BF16), v_ref[b, :, cols].astype(BF16)).astype(BF16))
        o_ref[b] = jnp.concatenate(outs, axis=1)
        return carry

    lax.fori_loop(0, bb, per_batch, 0)


def _memory_attention(q, mk, mv):
    batch, length, _ = q.shape
    padded = length if length % BF16_ROWS == 0 else pl.cdiv(length, BF16_ROWS) * BF16_ROWS
    if padded != length:
        q = jnp.pad(q, ((0, 0), (0, padded - length), (0, 0)))
    tq = min(padded, 512)
    bb = 1 if tq >= 128 else 4
    out = pl.pallas_call(
        functools.partial(_attn_kernel, bb=bb),
        grid=(batch // bb, padded // tq),
        in_specs=[pl.BlockSpec((bb, tq, D_MODEL), lambda i, j: (i, j, 0)),
                  pl.BlockSpec((bb, MEM_LEN, D_MODEL), lambda i, j: (i, 0, 0)),
                  pl.BlockSpec((bb, MEM_LEN, D_MODEL), lambda i, j: (i, 0, 0))],
        out_specs=pl.BlockSpec((bb, tq, D_MODEL), lambda i, j: (i, j, 0)),
        out_shape=jax.ShapeDtypeStruct((batch, padded, D_MODEL), BF16),
        compiler_params=_params("parallel", "arbitrary"),
        name="memory_attention",
    )(q, mk, mv)
    return out[:, :length]


def _post_kernel(x1_ref, o_ref, wo_ref, gm_ref, wup_ref, wdn_ref, gf_ref, y_ref, x2_scr, h_scr, acc_scr):
    f = pl.program_id(1)

    @pl.when(f == 0)
    def _():
        x2 = x1_ref[...] + _dot(o_ref[...], wo_ref[...])
        x2_scr[...] = x2
        h_scr[...] = _rmsnorm(x2, gm_ref[...]).astype(BF16)
        acc_scr[...] = jnp.zeros_like(acc_scr)

    up = jnp.maximum(_dot(h_scr[...], wup_ref[...]), 0.0)
    acc_scr[...] += _dot((up * up).astype(BF16), wdn_ref[...])

    @pl.when(f == pl.num_programs(1) - 1)
    def _():
        y_ref[...] = _rmsnorm(x2_scr[...] + acc_scr[...], gf_ref[...])


def _attn_out_mlp(x1, o, w_mo, g_mlp, w_up, w_down, g_final):
    t = x1.shape[0]
    tm = min(t, 1024)
    tf = 1024
    row = lambda w: pl.BlockSpec((tm, w), lambda i, f: (i, 0))
    vec = pl.BlockSpec((1, D_MODEL), lambda i, f: (0, 0))
    return pl.pallas_call(
        _post_kernel,
        grid=(t // tm, D_FF // tf),
        in_specs=[row(D_MODEL), row(D_MODEL), pl.BlockSpec((D_MODEL, D_MODEL), lambda i, f: (0, 0)), vec,
                  pl.BlockSpec((D_MODEL, tf), lambda i, f: (0, f)),
                  pl.BlockSpec((tf, D_MODEL), lambda i, f: (f, 0)), vec],
        out_specs=row(D_MODEL),
        out_shape=jax.ShapeDtypeStruct((t, D_MODEL), F32),
        scratch_shapes=[pltpu.VMEM((tm, D_MODEL), F32), pltpu.VMEM((tm, D_MODEL), BF16),
                        pltpu.VMEM((tm, D_MODEL), F32)],
        compiler_params=_params("parallel", "arbitrary"),
        name="attn_out_mlp",
    )(x1, o, w_mo, g_mlp, w_up, w_down, g_final)


def _decoder_layer(x, mk, mv, s5_re, s5_im, ret_s, pos0, p):
    batch, length, _ = x.shape
    tokens = batch * length
    xf = x.reshape(tokens, D_MODEL)
    u, q, k, v, g = _project(xf, p["g_mix"], p["w_in"])
    u_tb = u.reshape(batch, length, SSM_WIDTH).transpose(1, 0, 2).reshape(tokens, SSM_WIDTH)
    ssm_tb, s5_re_new, s5_im_new = _s5_mixer(
        u_tb, s5_re.reshape(batch, SSM_LANES), s5_im.reshape(batch, SSM_LANES),
        p["ab_re"], p["ab_im"], p["b_blk_re"], p["b_blk_im"], p["c_blk_re"], p["c_blk_nim"],
        p["d_skip"], p["w_glu"], batch=batch, length=length)
    ssm = ssm_tb.reshape(length, batch, SSM_WIDTH).transpose(1, 0, 2).reshape(tokens, SSM_WIDTH)
    tok3 = lambda a: a.reshape(batch, length, RET_WIDTH)
    ret, ret_new = _retention(tok3(q), tok3(k), tok3(v), tok3(g), ret_s, p["ret_gn"], pos0=pos0)
    x1, qm = _out_and_query(xf, ssm, ret.reshape(tokens, RET_WIDTH), p["w_out"], p["g_xattn"], p["w_mq"])
    o = _memory_attention(qm.reshape(batch, length, D_MODEL), mk, mv)
    y = _attn_out_mlp(x1, o.reshape(tokens, D_MODEL), p["w_mo"], p["g_mlp"], p["w_up"], p["w_down"],
                      p["g_final"])
    return (y.reshape(batch, length, D_MODEL),
            s5_re_new.reshape(batch, SSM_GROUPS, SSM_STATE),
            s5_im_new.reshape(batch, SSM_GROUPS, SSM_STATE), ret_new)


def kernel(x_prompt, x_sample, mem_prompt, state_s5_re, state_s5_im, state_ret, cache_mem_k, cache_mem_v, g_mix, w_in, lam_re, lam_im, log_dt, b_re, b_im, c_re, c_im, d_skip, w_glu, ret_gn, w_out, g_xattn, g_mem, w_mq, w_mk, w_mv, w_mo, g_mlp, w_up, w_down, g_final):
    if g_mix.shape[0] != 1:
        raise ValueError("this kernel implements the single-layer configuration")
    bp = x_prompt.shape[0]
    bs = x_sample.shape[0]
    vec = lambda a: a.reshape(1, -1).astype(F32)
    ab_re, ab_im, bb_re, bb_im = _discretize(lam_re[0], lam_im[0], log_dt[0], b_re[0], b_im[0])
    p = dict(
        g_mix=vec(g_mix[0]), w_in=w_in[0].astype(BF16), ab_re=ab_re, ab_im=ab_im,
        b_blk_re=_block_diag_in(bb_re).astype(BF16), b_blk_im=_block_diag_in(bb_im).astype(BF16),
        c_blk_re=_block_diag_out(c_re[0]).astype(BF16), c_blk_nim=_block_diag_out(-c_im[0]).astype(BF16),
        d_skip=vec(d_skip[0]), w_glu=w_glu[0].astype(BF16), ret_gn=vec(ret_gn[0]),
        w_out=w_out[0].astype(BF16), g_xattn=vec(g_xattn[0]), w_mq=w_mq[0].astype(BF16),
        w_mo=w_mo[0].astype(BF16), g_mlp=vec(g_mlp[0]), w_up=w_up[0].astype(BF16),
        w_down=w_down[0].astype(BF16), g_final=vec(g_final))

    mk_p, mv_p = _memory_kv(mem_prompt.reshape(bp * MEM_LEN, D_MODEL), vec(g_mem[0]),
                            w_mk[0].astype(BF16), w_mv[0].astype(BF16))
    mk_p = mk_p.reshape(bp, MEM_LEN, D_MODEL)
    mv_p = mv_p.reshape(bp, MEM_LEN, D_MODEL)
    zs = jnp.zeros((bp, SSM_GROUPS, SSM_STATE), F32)
    zr = jnp.zeros((bp, RET_HEADS, RET_HEAD_DIM, RET_HEAD_DIM), F32)
    y_p, s5r_p, s5i_p, ret_p = _decoder_layer(x_prompt, mk_p, mv_p, zs, zs, zr, 0.0, p)

    y_s, s5r_s, s5i_s, ret_s = _decoder_layer(
        x_sample, cache_mem_k[0].reshape(bs, MEM_LEN, D_MODEL), cache_mem_v[0].reshape(bs, MEM_LEN, D_MODEL),
        state_s5_re[0], state_s5_im[0], state_ret[0], float(PAST_LEN), p)

    kv5 = lambda a: a.reshape(1, bp, MEM_LEN, MEM_HEADS, MEM_HEAD_DIM)
    return (y_p, y_s, s5r_p[None], s5i_p[None], ret_p[None], kv5(mk_p), kv5(mv_p),
            s5r_s[None], s5i_s[None], ret_s[None])
```

```python
import functools
import math

import numpy as np
import jax
import jax.numpy as jnp
from jax import lax
from jax.experimental import pallas as pl
from jax.experimental.pallas import tpu as pltpu

F32 = jnp.float32
BF16 = jnp.bfloat16

D_MODEL = 1024
SSM_WIDTH = 512
SSM_GROUP = 16
SSM_GROUPS = 32
SSM_STATE = 64
SSM_LANES = SSM_GROUPS * SSM_STATE
RET_WIDTH = 512
RET_HEADS = 4
RET_HEAD_DIM = 128
RET_CHUNK = 128
ROPE_BASE = 10000.0
MEM_LEN = 256
MEM_HEADS = 4
MEM_HEAD_DIM = 256
D_FF = 4096
PROJ_WIDTH = SSM_WIDTH + 4 * RET_WIDTH
EPS = 1e-6
PAST_LEN = 16384

BF16_ROWS = 16
VMEM_LIMIT = 56 * 1024 * 1024


def _dot(a, b):
    return jnp.dot(a, b, preferred_element_type=F32)


def _dot_nt(a, b):
    return lax.dot_general(a, b, (((1,), (1,)), ((), ())), preferred_element_type=F32)


def _dot_tn(a, b):
    return lax.dot_general(a, b, (((0,), (0,)), ((), ())), preferred_element_type=F32)


def _rmsnorm(x, g):
    return x * lax.rsqrt(jnp.mean(x * x, axis=-1, keepdims=True) + EPS) * g


def _sigmoid(x):
    return 1.0 / (1.0 + jnp.exp(-x))


def _gelu_tanh(x):
    c = math.sqrt(2.0 / math.pi)
    return x * (0.5 * (1.0 + jnp.tanh(c * (x + 0.044715 * (x * x * x)))))


def _params(*sem):
    return pltpu.CompilerParams(dimension_semantics=sem, vmem_limit_bytes=VMEM_LIMIT)


def _full(shape):
    return pl.BlockSpec(shape, lambda *_: (0,) * len(shape))


def _disc_kernel(lr_ref, li_ref, ldt_ref, br_ref, bi_ref, abr_ref, abi_ref, bbr_ref, bbi_ref):
    lr = lr_ref[...]
    li = li_ref[...]
    dt = jnp.exp(ldt_ref[...])
    mag = jnp.exp(lr * dt)
    ab_re = mag * jnp.cos(li * dt)
    ab_im = mag * jnp.sin(li * dt)
    den = lr * lr + li * li
    f_re = ((ab_re - 1.0) * lr + ab_im * li) / den
    f_im = (ab_im * lr - (ab_re - 1.0) * li) / den
    br = br_ref[...]
    bi = bi_ref[...]
    abr_ref[...] = ab_re
    abi_ref[...] = ab_im
    bbr_ref[...] = f_re * br - f_im * bi
    bbi_ref[...] = f_re * bi + f_im * br


def _discretize(lam_re, lam_im, log_dt, b_re, b_im):
    rows = SSM_GROUPS * SSM_GROUP
    rep = lambda a: jnp.repeat(a, SSM_GROUP, axis=0)
    ldt = jnp.broadcast_to(log_dt[:, None], (SSM_GROUPS, SSM_STATE))
    bt = lambda b: b.transpose(0, 2, 1).reshape(rows, SSM_STATE)
    shp = jax.ShapeDtypeStruct((rows, SSM_STATE), F32)
    abr, abi, bbr, bbi = pl.pallas_call(
        _disc_kernel, out_shape=(shp, shp, shp, shp), name="s5_discretize",
    )(rep(lam_re), rep(lam_im), rep(ldt), bt(b_re), bt(b_im))
    ab_re = abr[::SSM_GROUP].reshape(1, SSM_LANES)
    ab_im = abi[::SSM_GROUP].reshape(1, SSM_LANES)
    return ab_re, ab_im, bbr, bbi


def _block_diag_in(bbt):
    row_g = np.arange(SSM_WIDTH)[:, None] // SSM_GROUP
    col_g = np.arange(SSM_LANES)[None, :] // SSM_STATE
    return jnp.where(row_g == col_g, jnp.tile(bbt, (1, SSM_GROUPS)), 0.0)


def _block_diag_out(c):
    ct = c.transpose(0, 2, 1).reshape(SSM_LANES, SSM_GROUP)
    row_g = np.arange(SSM_LANES)[:, None] // SSM_STATE
    col_g = np.arange(SSM_WIDTH)[None, :] // SSM_GROUP
    return jnp.where(row_g == col_g, jnp.tile(ct, (1, SSM_GROUPS)), 0.0)


def _proj_kernel(x_ref, g_ref, w_ref, u_ref, q_ref, k_ref, v_ref, gate_ref):
    h = _rmsnorm(x_ref[...], g_ref[...]).astype(BF16)
    proj = _dot(h, w_ref[...])
    u_ref[...] = proj[:, :SSM_WIDTH]
    q_ref[...] = proj[:, SSM_WIDTH:SSM_WIDTH + RET_WIDTH]
    k_ref[...] = proj[:, SSM_WIDTH + RET_WIDTH:SSM_WIDTH + 2 * RET_WIDTH]
    v_ref[...] = proj[:, SSM_WIDTH + 2 * RET_WIDTH:SSM_WIDTH + 3 * RET_WIDTH].astype(BF16)
    gate_ref[...] = proj[:, SSM_WIDTH + 3 * RET_WIDTH:]


def _project(x, g_mix, w_in):
    t = x.shape[0]
    tm = min(t, 512)
    row = lambda w: pl.BlockSpec((tm, w), lambda i: (i, 0))
    f = jax.ShapeDtypeStruct((t, 512), F32)
    return pl.pallas_call(
        _proj_kernel,
        grid=(t // tm,),
        in_specs=[row(D_MODEL), _full((1, D_MODEL)), _full((D_MODEL, PROJ_WIDTH))],
        out_specs=[row(512)] * 5,
        out_shape=(f, f, f, jax.ShapeDtypeStruct((t, 512), BF16), f),
        compiler_params=_params("parallel"),
        name="in_proj",
    )(x, g_mix, w_in)


def _s5_kernel(u_ref, h0r_ref, h0i_ref, abr_ref, abi_ref, bre_ref, bim_ref, cre_ref, ncim_ref,
               d_ref, wglu_ref, out_ref, hr_ref, hi_ref, bur_scr, bui_scr, xr_scr, xi_scr,
               *, batch, steps, unroll, width):
    half = SSM_LANES // 2
    kin = SSM_WIDTH // 2
    kout = SSM_WIDTH // 2

    @pl.when(pl.program_id(0) == 0)
    def _():
        hr_ref[...] = h0r_ref[...]
        hi_ref[...] = h0i_ref[...]

    u = u_ref[...]
    ub = u.astype(BF16)
    ys = []
    for p in range(2):
        lanes = slice(half * p, half * (p + 1))
        uk = ub[:, kin * p:kin * (p + 1)]
        bur_scr[...] = _dot(uk, bre_ref[kin * p:kin * (p + 1), lanes])
        bui_scr[...] = _dot(uk, bim_ref[kin * p:kin * (p + 1), lanes])
        for j in range(half // width):
            loc = slice(j * width, (j + 1) * width)
            glob = slice(half * p + j * width, half * p + (j + 1) * width)
            ar = jnp.broadcast_to(abr_ref[:, glob], (batch, width))
            ai = jnp.broadcast_to(abi_ref[:, glob], (batch, width))

            def body(i, carry, loc=loc, ar=ar, ai=ai):
                xr, xi = carry
                rows = unroll * batch
                base = pl.multiple_of(i * rows, rows)
                out_r, out_i = [], []
                for s in range(unroll):
                    br = bur_scr[pl.ds(base + s * batch, batch), loc]
                    bi = bui_scr[pl.ds(base + s * batch, batch), loc]
                    xr, xi = ar * xr - ai * xi + br, ar * xi + ai * xr + bi
                    out_r.append(xr)
                    out_i.append(xi)
                xr_scr[pl.ds(base, rows), loc] = jnp.concatenate(out_r, axis=0).astype(BF16)
                xi_scr[pl.ds(base, rows), loc] = jnp.concatenate(out_i, axis=0).astype(BF16)
                return xr, xi

            xr, xi = lax.fori_loop(0, steps // unroll, body, (hr_ref[:, glob], hi_ref[:, glob]))
            hr_ref[:, glob] = xr
            hi_ref[:, glob] = xi
        cols = slice(kout * p, kout * (p + 1))
        ys.append(_dot(xr_scr[...], cre_ref[lanes, cols]) + _dot(xi_scr[...], ncim_ref[lanes, cols]))
    y = jnp.concatenate(ys, axis=1) + d_ref[...] * u
    z = _gelu_tanh(y)
    out_ref[...] = (z * _sigmoid(_dot(z.astype(BF16), wglu_ref[...]))).astype(BF16)


def _s5_mixer(u_tb, h0_re, h0_im, ab_re, ab_im, b_re, b_im, c_re, nc_im, d_skip, w_glu, *, batch, length):
    steps = min(length, 128)
    unroll = max(1, min(steps, 4 * 8 // batch))
    if (unroll * batch) % BF16_ROWS:
        raise ValueError("S5 scan stores need whole bf16 tiles")
    width = 512 if batch <= 8 else 128
    rows = steps * batch
    half = SSM_LANES // 2
    kern = functools.partial(_s5_kernel, batch=batch, steps=steps, unroll=unroll, width=width)
    st = jax.ShapeDtypeStruct((batch, SSM_LANES), F32)
    return pl.pallas_call(
        kern,
        grid=(length // steps,),
        in_specs=[pl.BlockSpec((rows, SSM_WIDTH), lambda i: (i, 0)),
                  _full((batch, SSM_LANES)), _full((batch, SSM_LANES)),
                  _full((1, SSM_LANES)), _full((1, SSM_LANES)),
                  _full((SSM_WIDTH, SSM_LANES)), _full((SSM_WIDTH, SSM_LANES)),
                  _full((SSM_LANES, SSM_WIDTH)), _full((SSM_LANES, SSM_WIDTH)),
                  _full((1, SSM_WIDTH)), _full((SSM_WIDTH, SSM_WIDTH))],
        out_specs=[pl.BlockSpec((rows, SSM_WIDTH), lambda i: (i, 0)),
                   _full((batch, SSM_LANES)), _full((batch, SSM_LANES))],
        out_shape=(jax.ShapeDtypeStruct((length * batch, SSM_WIDTH), BF16), st, st),
        scratch_shapes=[pltpu.VMEM((rows, half), F32), pltpu.VMEM((rows, half), F32),
                        pltpu.VMEM((rows, half), BF16), pltpu.VMEM((rows, half), BF16)],
        compiler_params=_params("arbitrary"),
        name="s5_mixer",
    )(u_tb, h0_re, h0_im, ab_re, ab_im, b_re, b_im, c_re, nc_im, d_skip, w_glu)


def _ret_tables(length, padded, pos0):
    chunk = RET_CHUNK if length % RET_CHUNK == 0 else length
    cpad = chunk if padded == length else padded
    lg = np.log(1.0 - 2.0 ** (-5.0 - np.arange(RET_HEADS, dtype=np.float64)))
    idx = np.arange(chunk, dtype=np.float64)
    diff = idx[:, None] - idx[None, :]
    mask = np.where(diff[None] >= 0, np.exp(np.maximum(diff, 0.0)[None] * lg[:, None, None]), 0.0)
    zeta = np.exp((chunk - 1.0 - idx)[None, :] * lg[:, None])
    xi = np.exp((idx + 1.0)[None, :] * lg[:, None])
    gamma_c = tuple(float(v) for v in np.exp(chunk * lg))
    pad2 = lambda a: np.pad(a, ((0, 0), (0, cpad - chunk), (0, cpad - chunk)))
    lane = lambda a: np.broadcast_to(np.pad(a, ((0, 0), (0, cpad - chunk)))[:, :, None],
                                     (RET_HEADS, cpad, RET_HEAD_DIM))
    half = RET_HEAD_DIM // 2
    inv = ROPE_BASE ** (-np.arange(half, dtype=np.float64) / half)
    ang = (pos0 + np.arange(length, dtype=np.float64))[:, None] * inv[None, :]
    cc = np.concatenate([np.cos(ang), np.cos(ang)], axis=1)
    ss = np.concatenate([-np.sin(ang), np.sin(ang)], axis=1)
    rows = lambda a: np.pad(a, ((0, padded - length), (0, 0)))
    f = lambda a: jnp.asarray(np.ascontiguousarray(a), dtype=F32)
    return cpad, gamma_c, f(pad2(mask)), f(lane(zeta)), f(lane(xi)), f(rows(cc)), f(rows(ss))


def _ret_kernel(q_ref, k_ref, v_ref, g_ref, cc_ref, ss_ref, mask_ref, zeta_ref, xi_ref, gain_ref,
                s0_ref, o_ref, s_ref, *, bb, gamma_c):
    @pl.when(pl.program_id(1) == 0)
    def _():
        s_ref[...] = s0_ref[...]

    cc = cc_ref[...]
    ss = ss_ref[...]
    half = RET_HEAD_DIM // 2

    def rope(x):
        return x * cc + pltpu.roll(x, half, axis=1) * ss

    def per_batch(b, carry):
        outs = []
        for h in range(RET_HEADS):
            cols = slice(h * RET_HEAD_DIM, (h + 1) * RET_HEAD_DIM)
            qb = (rope(q_ref[b, :, cols]) * (RET_HEAD_DIM ** -0.5)).astype(BF16)
            kr = rope(k_ref[b, :, cols])
            kb = kr.astype(BF16)
            kz = (kr * zeta_ref[h]).astype(BF16)
            vb = v_ref[b, :, cols]
            scores = (_dot_nt(qb, kb) * mask_ref[h]).astype(BF16)
            inner = _dot(scores, vb)
            s_prev = s_ref[b, h]
            cross = _dot(qb, s_prev.astype(BF16)) * xi_ref[h]
            s_ref[b, h] = s_prev * gamma_c[h] + _dot_tn(kz, vb)
            o = inner + cross
            mu = jnp.mean(o, axis=-1, keepdims=True)
            oc = o - mu
            var = jnp.mean(oc * oc, axis=-1, keepdims=True)
            on = oc * lax.rsqrt(var + EPS) * gain_ref[:, cols]
            gate = g_ref[b, :, cols]
            outs.append((gate * _sigmoid(gate) * on).astype(BF16))
        o_ref[b] = jnp.concatenate(outs, axis=1)
        return carry

    lax.fori_loop(0, bb, per_batch, 0)


def _retention(q, k, v, g, s0, gn_gain, *, pos0):
    batch, length, _ = q.shape
    padded = length if length % BF16_ROWS == 0 else pl.cdiv(length, BF16_ROWS) * BF16_ROWS
    chunk, gamma_c, mask, zeta, xi, cc, ss = _ret_tables(length, padded, pos0)
    if padded != length:
        pad = lambda a: jnp.pad(a, ((0, 0), (0, padded - length), (0, 0)))
        q, k, v, g = pad(q), pad(k), pad(v), pad(g)
    bb = 8
    tok = pl.BlockSpec((bb, chunk, RET_WIDTH), lambda i, c: (i, c, 0))
    tab = pl.BlockSpec((chunk, RET_HEAD_DIM), lambda i, c: (c, 0))
    state = pl.BlockSpec((bb, RET_HEADS, RET_HEAD_DIM, RET_HEAD_DIM), lambda i, c: (i, 0, 0, 0))
    out, s_new = pl.pallas_call(
        functools.partial(_ret_kernel, bb=bb, gamma_c=gamma_c),
        grid=(batch // bb, padded // chunk),
        in_specs=[tok, tok, tok, tok, tab, tab,
                  _full((RET_HEADS, chunk, chunk)), _full((RET_HEADS, chunk, RET_HEAD_DIM)),
                  _full((RET_HEADS, chunk, RET_HEAD_DIM)), _full((1, RET_WIDTH)), state],
        out_specs=[tok, state],
        out_shape=(jax.ShapeDtypeStruct((batch, padded, RET_WIDTH), BF16),
                   jax.ShapeDtypeStruct(s0.shape, F32)),
        compiler_params=_params("parallel", "arbitrary"),
        name="retention",
    )(q, k, v, g, cc, ss, mask, zeta, xi, gn_gain, s0)
    return out[:, :length], s_new


def _outq_kernel(x_ref, ssm_ref, ret_ref, wout_ref, g_ref, wq_ref, x1_ref, q_ref):
    x1 = (x_ref[...] + _dot(ssm_ref[...], wout_ref[:SSM_WIDTH, :])
          + _dot(ret_ref[...], wout_ref[SSM_WIDTH:, :]))
    x1_ref[...] = x1
    q_ref[...] = _dot(_rmsnorm(x1, g_ref[...]).astype(BF16), wq_ref[...]).astype(BF16)


def _out_and_query(x, ssm, ret, w_out, g_xattn, w_mq):
    t = x.shape[0]
    tm = min(t, 512)
    row = lambda w: pl.BlockSpec((tm, w), lambda i: (i, 0))
    return pl.pallas_call(
        _outq_kernel,
        grid=(t // tm,),
        in_specs=[row(D_MODEL), row(SSM_WIDTH), row(RET_WIDTH), _full((D_MODEL, D_MODEL)),
                  _full((1, D_MODEL)), _full((D_MODEL, D_MODEL))],
        out_specs=[row(D_MODEL), row(D_MODEL)],
        out_shape=(jax.ShapeDtypeStruct((t, D_MODEL), F32), jax.ShapeDtypeStruct((t, D_MODEL), BF16)),
        compiler_params=_params("parallel"),
        name="out_proj_query",
    )(x, ssm, ret, w_out, g_xattn, w_mq)


def _memkv_kernel(m_ref, g_ref, wk_ref, wv_ref, k_ref, v_ref):
    m = _rmsnorm(m_ref[...], g_ref[...]).astype(BF16)
    k_ref[...] = _dot(m, wk_ref[...])
    v_ref[...] = _dot(m, wv_ref[...])


def _memory_kv(mem, g_mem, w_mk, w_mv):
    t = mem.shape[0]
    tm = min(t, 512)
    row = pl.BlockSpec((tm, D_MODEL), lambda i: (i, 0))
    f = jax.ShapeDtypeStruct((t, D_MODEL), F32)
    return pl.pallas_call(
        _memkv_kernel,
        grid=(t // tm,),
        in_specs=[row, _full((1, D_MODEL)), _full((D_MODEL, D_MODEL)), _full((D_MODEL, D_MODEL))],
        out_specs=[row, row],
        out_shape=(f, f),
        compiler_params=_params("parallel"),
        name="memory_kv",
    )(mem, g_mem, w_mk, w_mv)


def _attn_kernel(q_ref, k_ref, v_ref, o_ref, *, bb):
    def per_batch(b, carry):
        outs = []
        for h in range(MEM_HEADS):
            cols = slice(h * MEM_HEAD_DIM, (h + 1) * MEM_HEAD_DIM)
            s = _dot_nt(q_ref[b, :, cols], k_ref[b, :, cols].astype(BF16)) * (MEM_HEAD_DIM ** -0.5)
            e = jnp.exp(s - jnp.max(s, axis=-1, keepdims=True))
            p = e / jnp.sum(e, axis=-1, keepdims=True)
            outs.append(_dot(p.astype(BF16), v_ref[b, :, cols].astype(BF16)).astype(BF16))
        o_ref[b] = jnp.concatenate(outs, axis=1)
        return carry

    lax.fori_loop(0, bb, per_batch, 0)


def _memory_attention(q, mk, mv):
    batch, length, _ = q.shape
    padded = length if length % BF16_ROWS == 0 else pl.cdiv(length, BF16_ROWS) * BF16_ROWS
    if padded != length:
        q = jnp.pad(q, ((0, 0), (0, padded - length), (0, 0)))
    tq = min(padded, 512)
    bb = 1 if tq >= 128 else 4
    out = pl.pallas_call(
        functools.partial(_attn_kernel, bb=bb),
        grid=(batch // bb, padded // tq),
        in_specs=[pl.BlockSpec((bb, tq, D_MODEL), lambda i, j: (i, j, 0)),
                  pl.BlockSpec((bb, MEM_LEN, D_MODEL), lambda i, j: (i, 0, 0)),
                  pl.BlockSpec((bb, MEM_LEN, D_MODEL), lambda i, j: (i, 0, 0))],
        out_specs=pl.BlockSpec((bb, tq, D_MODEL), lambda i, j: (i, j, 0)),
        out_shape=jax.ShapeDtypeStruct((batch, padded, D_MODEL), BF16),
        compiler_params=_params("parallel", "arbitrary"),
        name="memory_attention",
    )(q, mk, mv)
    return out[:, :length]


CACHE_ROWS = MEM_LEN * MEM_HEADS * MEM_HEAD_DIM // 128
CACHE_SUB = MEM_HEADS * MEM_HEAD_DIM // 128


def _cached_attn_kernel(q_ref, k_ref, v_ref, o_ref, *, bb, length):
    rows = length * CACHE_SUB
    hit = (lax.broadcasted_iota(jnp.int32, (CACHE_SUB, CACHE_ROWS), 0)
           == lax.broadcasted_iota(jnp.int32, (CACHE_SUB, CACHE_ROWS), 1) % CACHE_SUB)
    first_half = lax.broadcasted_iota(jnp.int32, (length, 128), 1) % CACHE_SUB < MEM_HEADS
    tiles = CACHE_ROWS // 128

    def class_reduce(x, op):
        for shift in (8, 16, 32, 64):
            x = op(x, pltpu.roll(x, shift, axis=1))
        return x

    def softmax_rows(r):
        z = jnp.concatenate(
            [jnp.sum(jnp.where(hit, r[CACHE_SUB * t:CACHE_SUB * (t + 1)], 0.0), axis=0, keepdims=True)
             for t in range(length)], axis=0)
        parts = []
        for i in range(tiles):
            zi = z[:, 128 * i:128 * (i + 1)]
            parts.append(zi + pltpu.roll(zi, 128 - MEM_HEADS, axis=1))
        mx = parts[0]
        for pi in parts[1:]:
            mx = jnp.maximum(mx, pi)
        mx = class_reduce(mx, jnp.maximum)
        es = [jnp.exp(pi - mx) for pi in parts]
        tot = es[0]
        for ei in es[1:]:
            tot = tot + ei
        tot = class_reduce(tot, jnp.add)
        ps = []
        for ei in es:
            pi = ei / tot
            ps.append(jnp.where(first_half, pi, pltpu.roll(pi, MEM_HEADS, axis=1)))
        p = jnp.concatenate(ps, axis=1)
        return jnp.concatenate(
            [jnp.where(hit, jnp.broadcast_to(p[t:t + 1], (CACHE_SUB, CACHE_ROWS)), 0.0)
             for t in range(length)], axis=0).astype(BF16)

    scores = [_dot_nt(q_ref[b], k_ref[b].astype(BF16)) * (MEM_HEAD_DIM ** -0.5) for b in range(bb)]
    probs = [softmax_rows(r) for r in scores]
    for b in range(bb):
        o_ref[b] = _dot(probs[b], v_ref[b].astype(BF16)).astype(BF16)


def _cached_memory_attention(q, cache_k, cache_v):
    batch, length, _ = q.shape
    halves = MEM_HEAD_DIM // 128
    rows = length * CACHE_SUB
    if rows % BF16_ROWS:
        raise ValueError("cached attention needs whole bf16 tiles of query rows")

    def cache_rows(c):
        c = c.reshape(batch, MEM_LEN, MEM_HEADS, halves, 128).transpose(0, 1, 3, 2, 4)
        return c.reshape(batch, CACHE_ROWS, 128)

    qr = q.reshape(batch, length, MEM_HEADS, halves, 128).transpose(0, 1, 3, 2, 4).reshape(batch, rows, 128)
    bb = 4
    tok = pl.BlockSpec((bb, rows, 128), lambda i: (i, 0, 0))
    kv = pl.BlockSpec((bb, CACHE_ROWS, 128), lambda i: (i, 0, 0))
    out = pl.pallas_call(
        functools.partial(_cached_attn_kernel, bb=bb, length=length),
        grid=(batch // bb,),
        in_specs=[tok, kv, kv],
        out_specs=tok,
        out_shape=jax.ShapeDtypeStruct((batch, rows, 128), BF16),
        compiler_params=_params("parallel"),
        name="cached_memory_attention",
    )(qr, cache_rows(cache_k), cache_rows(cache_v))
    out = out.reshape(batch, length, halves, MEM_HEADS, 128).transpose(0, 1, 3, 2, 4)
    return out.reshape(batch, length, D_MODEL)


def _post_kernel(x1_ref, o_ref, wo_ref, gm_ref, wup_ref, wdn_ref, gf_ref, y_ref, x2_scr, h_scr, acc_scr):
    f = pl.program_id(1)

    @pl.when(f == 0)
    def _():
        x2 = x1_ref[...] + _dot(o_ref[...], wo_ref[...])
        x2_scr[...] = x2
        h_scr[...] = _rmsnorm(x2, gm_ref[...]).astype(BF16)
        acc_scr[...] = jnp.zeros_like(acc_scr)

    up = jnp.maximum(_dot(h_scr[...], wup_ref[...]), 0.0)
    acc_scr[...] += _dot((up * up).astype(BF16), wdn_ref[...])

    @pl.when(f == pl.num_programs(1) - 1)
    def _():
        y_ref[...] = _rmsnorm(x2_scr[...] + acc_scr[...], gf_ref[...])


def _attn_out_mlp(x1, o, w_mo, g_mlp, w_up, w_down, g_final):
    t = x1.shape[0]
    tm = min(t, 1024)
    tf = 1024
    row = lambda w: pl.BlockSpec((tm, w), lambda i, f: (i, 0))
    vec = pl.BlockSpec((1, D_MODEL), lambda i, f: (0, 0))
    return pl.pallas_call(
        _post_kernel,
        grid=(t // tm, D_FF // tf),
        in_specs=[row(D_MODEL), row(D_MODEL), pl.BlockSpec((D_MODEL, D_MODEL), lambda i, f: (0, 0)), vec,
                  pl.BlockSpec((D_MODEL, tf), lambda i, f: (0, f)),
                  pl.BlockSpec((tf, D_MODEL), lambda i, f: (f, 0)), vec],
        out_specs=row(D_MODEL),
        out_shape=jax.ShapeDtypeStruct((t, D_MODEL), F32),
        scratch_shapes=[pltpu.VMEM((tm, D_MODEL), F32), pltpu.VMEM((tm, D_MODEL), BF16),
                        pltpu.VMEM((tm, D_MODEL), F32)],
        compiler_params=_params("parallel", "arbitrary"),
        name="attn_out_mlp",
    )(x1, o, w_mo, g_mlp, w_up, w_down, g_final)


def _decoder_layer(x, attend, s5_re, s5_im, ret_s, pos0, p):
    batch, length, _ = x.shape
    tokens = batch * length
    xf = x.reshape(tokens, D_MODEL)
    u, q, k, v, g = _project(xf, p["g_mix"], p["w_in"])
    u_tb = u.reshape(batch, length, SSM_WIDTH).transpose(1, 0, 2).reshape(tokens, SSM_WIDTH)
    ssm_tb, s5_re_new, s5_im_new = _s5_mixer(
        u_tb, s5_re.reshape(batch, SSM_LANES), s5_im.reshape(batch, SSM_LANES),
        p["ab_re"], p["ab_im"], p["b_blk_re"], p["b_blk_im"], p["c_blk_re"], p["c_blk_nim"],
        p["d_skip"], p["w_glu"], batch=batch, length=length)
    ssm = ssm_tb.reshape(length, batch, SSM_WIDTH).transpose(1, 0, 2).reshape(tokens, SSM_WIDTH)
    tok3 = lambda a: a.reshape(batch, length, RET_WIDTH)
    ret, ret_new = _retention(tok3(q), tok3(k), tok3(v), tok3(g), ret_s, p["ret_gn"], pos0=pos0)
    x1, qm = _out_and_query(xf, ssm, ret.reshape(tokens, RET_WIDTH), p["w_out"], p["g_xattn"], p["w_mq"])
    o = attend(qm.reshape(batch, length, D_MODEL))
    y = _attn_out_mlp(x1, o.reshape(tokens, D_MODEL), p["w_mo"], p["g_mlp"], p["w_up"], p["w_down"],
                      p["g_final"])
    return (y.reshape(batch, length, D_MODEL),
            s5_re_new.reshape(batch, SSM_GROUPS, SSM_STATE),
            s5_im_new.reshape(batch, SSM_GROUPS, SSM_STATE), ret_new)


def kernel(x_prompt, x_sample, mem_prompt, state_s5_re, state_s5_im, state_ret, cache_mem_k, cache_mem_v, g_mix, w_in, lam_re, lam_im, log_dt, b_re, b_im, c_re, c_im, d_skip, w_glu, ret_gn, w_out, g_xattn, g_mem, w_mq, w_mk, w_mv, w_mo, g_mlp, w_up, w_down, g_final):
    if g_mix.shape[0] != 1:
        raise ValueError("this kernel implements the single-layer configuration")
    bp = x_prompt.shape[0]
    vec = lambda a: a.reshape(1, -1).astype(F32)
    ab_re, ab_im, bb_re, bb_im = _discretize(lam_re[0], lam_im[0], log_dt[0], b_re[0], b_im[0])
    p = dict(
        g_mix=vec(g_mix[0]), w_in=w_in[0].astype(BF16), ab_re=ab_re, ab_im=ab_im,
        b_blk_re=_block_diag_in(bb_re).astype(BF16), b_blk_im=_block_diag_in(bb_im).astype(BF16),
        c_blk_re=_block_diag_out(c_re[0]).astype(BF16), c_blk_nim=_block_diag_out(-c_im[0]).astype(BF16),
        d_skip=vec(d_skip[0]), w_glu=w_glu[0].astype(BF16), ret_gn=vec(ret_gn[0]),
        w_out=w_out[0].astype(BF16), g_xattn=vec(g_xattn[0]), w_mq=w_mq[0].astype(BF16),
        w_mo=w_mo[0].astype(BF16), g_mlp=vec(g_mlp[0]), w_up=w_up[0].astype(BF16),
        w_down=w_down[0].astype(BF16), g_final=vec(g_final))

    mk_p, mv_p = _memory_kv(mem_prompt.reshape(bp * MEM_LEN, D_MODEL), vec(g_mem[0]),
                            w_mk[0].astype(BF16), w_mv[0].astype(BF16))
    mk_p = mk_p.reshape(bp, MEM_LEN, D_MODEL)
    mv_p = mv_p.reshape(bp, MEM_LEN, D_MODEL)
    zs = jnp.zeros((bp, SSM_GROUPS, SSM_STATE), F32)
    zr = jnp.zeros((bp, RET_HEADS, RET_HEAD_DIM, RET_HEAD_DIM), F32)
    y_p, s5r_p, s5i_p, ret_p = _decoder_layer(
        x_prompt, functools.partial(_memory_attention, mk=mk_p, mv=mv_p), zs, zs, zr, 0.0, p)

    y_s, s5r_s, s5i_s, ret_s = _decoder_layer(
        x_sample, functools.partial(_cached_memory_attention, cache_k=cache_mem_k, cache_v=cache_mem_v),
        state_s5_re[0], state_s5_im[0], state_ret[0], float(PAST_LEN), p)

    kv5 = lambda a: a.reshape(1, bp, MEM_LEN, MEM_HEADS, MEM_HEAD_DIM)
    return (y_p, y_s, s5r_p[None], s5i_p[None], ret_p[None], kv5(mk_p), kv5(mv_p),
            s5r_s[None], s5i_s[None], ret_s[None])
```

```python
import functools
import math

import numpy as np
import jax
import jax.numpy as jnp
from jax import lax
from jax.experimental import pallas as pl
from jax.experimental.pallas import tpu as pltpu

F32 = jnp.float32
BF16 = jnp.bfloat16

D_MODEL = 1024
SSM_WIDTH = 512
SSM_GROUP = 16
SSM_GROUPS = 32
SSM_STATE = 64
SSM_LANES = SSM_GROUPS * SSM_STATE
RET_WIDTH = 512
RET_HEADS = 4
RET_HEAD_DIM = 128
RET_CHUNK = 128
ROPE_BASE = 10000.0
MEM_LEN = 256
MEM_HEADS = 4
MEM_HEAD_DIM = 256
D_FF = 4096
PROJ_WIDTH = SSM_WIDTH + 4 * RET_WIDTH
EPS = 1e-6
PAST_LEN = 16384

CACHE_ROWS = MEM_LEN * MEM_HEADS * MEM_HEAD_DIM // 128
CACHE_SUB = MEM_HEADS * MEM_HEAD_DIM // 128

BF16_ROWS = 16
VMEM_LIMIT = 56 * 1024 * 1024


def _dot(a, b):
    return jnp.dot(a, b, preferred_element_type=F32)


def _dot_nt(a, b):
    return lax.dot_general(a, b, (((1,), (1,)), ((), ())), preferred_element_type=F32)


def _dot_tn(a, b):
    return lax.dot_general(a, b, (((0,), (0,)), ((), ())), preferred_element_type=F32)


def _rmsnorm(x, g):
    return x * lax.rsqrt(jnp.mean(x * x, axis=-1, keepdims=True) + EPS) * g


def _sigmoid(x):
    return 1.0 / (1.0 + jnp.exp(-x))


def _gelu_tanh(x):
    c = math.sqrt(2.0 / math.pi)
    return x * (0.5 * (1.0 + jnp.tanh(c * (x + 0.044715 * (x * x * x)))))


def _params(*sem):
    return pltpu.CompilerParams(dimension_semantics=sem, vmem_limit_bytes=VMEM_LIMIT)


def _full(shape):
    return pl.BlockSpec(shape, lambda *_: (0,) * len(shape))


def _disc_kernel(lr_ref, li_ref, ldt_ref, br_ref, bi_ref, abr_ref, abi_ref, bbr_ref, bbi_ref):
    lr = lr_ref[...]
    li = li_ref[...]
    dt = jnp.exp(ldt_ref[...])
    mag = jnp.exp(lr * dt)
    ab_re = mag * jnp.cos(li * dt)
    ab_im = mag * jnp.sin(li * dt)
    den = lr * lr + li * li
    f_re = ((ab_re - 1.0) * lr + ab_im * li) / den
    f_im = (ab_im * lr - (ab_re - 1.0) * li) / den
    br = br_ref[...]
    bi = bi_ref[...]
    abr_ref[...] = ab_re
    abi_ref[...] = ab_im
    bbr_ref[...] = f_re * br - f_im * bi
    bbi_ref[...] = f_re * bi + f_im * br


def _discretize(lam_re, lam_im, log_dt, b_re, b_im):
    rows = SSM_GROUPS * SSM_GROUP
    rep = lambda a: jnp.repeat(a, SSM_GROUP, axis=0)
    ldt = jnp.broadcast_to(log_dt[:, None], (SSM_GROUPS, SSM_STATE))
    bt = lambda b: b.transpose(0, 2, 1).reshape(rows, SSM_STATE)
    shp = jax.ShapeDtypeStruct((rows, SSM_STATE), F32)
    abr, abi, bbr, bbi = pl.pallas_call(
        _disc_kernel, out_shape=(shp, shp, shp, shp), name="s5_discretize",
    )(rep(lam_re), rep(lam_im), rep(ldt), bt(b_re), bt(b_im))
    ab_re = abr[::SSM_GROUP].reshape(1, SSM_LANES)
    ab_im = abi[::SSM_GROUP].reshape(1, SSM_LANES)
    return ab_re, ab_im, bbr, bbi


def _block_diag_in(bbt):
    row_g = np.arange(SSM_WIDTH)[:, None] // SSM_GROUP
    col_g = np.arange(SSM_LANES)[None, :] // SSM_STATE
    return jnp.where(row_g == col_g, jnp.tile(bbt, (1, SSM_GROUPS)), 0.0)


def _block_diag_out(c):
    ct = c.transpose(0, 2, 1).reshape(SSM_LANES, SSM_GROUP)
    row_g = np.arange(SSM_LANES)[:, None] // SSM_STATE
    col_g = np.arange(SSM_WIDTH)[None, :] // SSM_GROUP
    return jnp.where(row_g == col_g, jnp.tile(ct, (1, SSM_GROUPS)), 0.0)


def _proj_kernel(x_ref, g_ref, w_ref, u_ref, q_ref, k_ref, v_ref, gate_ref):
    h = _rmsnorm(x_ref[...], g_ref[...]).astype(BF16)
    proj = _dot(h, w_ref[...])
    u_ref[...] = proj[:, :SSM_WIDTH]
    q_ref[...] = proj[:, SSM_WIDTH:SSM_WIDTH + RET_WIDTH]
    k_ref[...] = proj[:, SSM_WIDTH + RET_WIDTH:SSM_WIDTH + 2 * RET_WIDTH]
    v_ref[...] = proj[:, SSM_WIDTH + 2 * RET_WIDTH:SSM_WIDTH + 3 * RET_WIDTH].astype(BF16)
    gate_ref[...] = proj[:, SSM_WIDTH + 3 * RET_WIDTH:]


def _project(x, g_mix, w_in):
    t = x.shape[0]
    tm = min(t, 512)
    row = lambda w: pl.BlockSpec((tm, w), lambda i: (i, 0))
    f = jax.ShapeDtypeStruct((t, 512), F32)
    return pl.pallas_call(
        _proj_kernel,
        grid=(t // tm,),
        in_specs=[row(D_MODEL), _full((1, D_MODEL)), _full((D_MODEL, PROJ_WIDTH))],
        out_specs=[row(512)] * 5,
        out_shape=(f, f, f, jax.ShapeDtypeStruct((t, 512), BF16), f),
        compiler_params=_params("parallel"),
        name="in_proj",
    )(x, g_mix, w_in)


def _s5_kernel(u_ref, h0r_ref, h0i_ref, abr_ref, abi_ref, bre_ref, bim_ref, cre_ref, ncim_ref,
               d_ref, wglu_ref, out_ref, hr_ref, hi_ref, bur_scr, bui_scr, xr_scr, xi_scr,
               *, batch, steps, unroll, width):
    half = SSM_LANES // 2
    kin = SSM_WIDTH // 2
    kout = SSM_WIDTH // 2

    @pl.when(pl.program_id(0) == 0)
    def _():
        hr_ref[...] = h0r_ref[...]
        hi_ref[...] = h0i_ref[...]

    rows = steps * batch
    u = jnp.swapaxes(u_ref[...], 0, 1).reshape(rows, SSM_WIDTH)
    ub = u.astype(BF16)
    ys = []
    for p in range(2):
        lanes = slice(half * p, half * (p + 1))
        uk = ub[:, kin * p:kin * (p + 1)]
        bur_scr[...] = _dot(uk, bre_ref[kin * p:kin * (p + 1), lanes])
        bui_scr[...] = _dot(uk, bim_ref[kin * p:kin * (p + 1), lanes])
        for j in range(half // width):
            loc = slice(j * width, (j + 1) * width)
            glob = slice(half * p + j * width, half * p + (j + 1) * width)
            ar = jnp.broadcast_to(abr_ref[:, glob], (batch, width))
            ai = jnp.broadcast_to(abi_ref[:, glob], (batch, width))

            def body(i, carry, loc=loc, ar=ar, ai=ai):
                xr, xi = carry
                rows = unroll * batch
                base = pl.multiple_of(i * rows, rows)
                out_r, out_i = [], []
                for s in range(unroll):
                    br = bur_scr[pl.ds(base + s * batch, batch), loc]
                    bi = bui_scr[pl.ds(base + s * batch, batch), loc]
                    xr, xi = ar * xr - ai * xi + br, ar * xi + ai * xr + bi
                    out_r.append(xr)
                    out_i.append(xi)
                xr_scr[pl.ds(base, rows), loc] = jnp.concatenate(out_r, axis=0).astype(BF16)
                xi_scr[pl.ds(base, rows), loc] = jnp.concatenate(out_i, axis=0).astype(BF16)
                return xr, xi

            xr, xi = lax.fori_loop(0, steps // unroll, body, (hr_ref[:, glob], hi_ref[:, glob]))
            hr_ref[:, glob] = xr
            hi_ref[:, glob] = xi
        cols = slice(kout * p, kout * (p + 1))
        ys.append(_dot(xr_scr[...], cre_ref[lanes, cols]) + _dot(xi_scr[...], ncim_ref[lanes, cols]))
    y = jnp.concatenate(ys, axis=1) + d_ref[...] * u
    z = _gelu_tanh(y)
    out = z * _sigmoid(_dot(z.astype(BF16), wglu_ref[...]))
    out_ref[...] = jnp.swapaxes(out.reshape(steps, batch, SSM_WIDTH), 0, 1).astype(BF16)


def _s5_mixer(u, h0_re, h0_im, ab_re, ab_im, b_re, b_im, c_re, nc_im, d_skip, w_glu, *, batch, length):
    steps = min(length, 128)
    unroll = max(1, min(steps, 4 * 8 // batch))
    if (unroll * batch) % BF16_ROWS:
        raise ValueError("S5 scan stores need whole bf16 tiles")
    width = 512 if batch <= 8 else 128
    rows = steps * batch
    half = SSM_LANES // 2
    kern = functools.partial(_s5_kernel, batch=batch, steps=steps, unroll=unroll, width=width)
    st = jax.ShapeDtypeStruct((batch, SSM_LANES), F32)
    return pl.pallas_call(
        kern,
        grid=(length // steps,),
        in_specs=[pl.BlockSpec((batch, steps, SSM_WIDTH), lambda i: (0, i, 0)),
                  _full((batch, SSM_LANES)), _full((batch, SSM_LANES)),
                  _full((1, SSM_LANES)), _full((1, SSM_LANES)),
                  _full((SSM_WIDTH, SSM_LANES)), _full((SSM_WIDTH, SSM_LANES)),
                  _full((SSM_LANES, SSM_WIDTH)), _full((SSM_LANES, SSM_WIDTH)),
                  _full((1, SSM_WIDTH)), _full((SSM_WIDTH, SSM_WIDTH))],
        out_specs=[pl.BlockSpec((batch, steps, SSM_WIDTH), lambda i: (0, i, 0)),
                   _full((batch, SSM_LANES)), _full((batch, SSM_LANES))],
        out_shape=(jax.ShapeDtypeStruct((batch, length, SSM_WIDTH), BF16), st, st),
        scratch_shapes=[pltpu.VMEM((rows, half), F32), pltpu.VMEM((rows, half), F32),
                        pltpu.VMEM((rows, half), BF16), pltpu.VMEM((rows, half), BF16)],
        compiler_params=_params("arbitrary"),
        name="s5_mixer",
    )(u, h0_re, h0_im, ab_re, ab_im, b_re, b_im, c_re, nc_im, d_skip, w_glu)


def _ret_tables(length, padded, pos0):
    chunk = RET_CHUNK if length % RET_CHUNK == 0 else length
    cpad = chunk if padded == length else padded
    lg = np.log(1.0 - 2.0 ** (-5.0 - np.arange(RET_HEADS, dtype=np.float64)))
    idx = np.arange(chunk, dtype=np.float64)
    diff = idx[:, None] - idx[None, :]
    mask = np.where(diff[None] >= 0, np.exp(np.maximum(diff, 0.0)[None] * lg[:, None, None]), 0.0)
    zeta = np.exp((chunk - 1.0 - idx)[None, :] * lg[:, None])
    xi = np.exp((idx + 1.0)[None, :] * lg[:, None])
    gamma_c = tuple(float(v) for v in np.exp(chunk * lg))
    pad2 = lambda a: np.pad(a, ((0, 0), (0, cpad - chunk), (0, cpad - chunk)))
    lane = lambda a: np.broadcast_to(np.pad(a, ((0, 0), (0, cpad - chunk)))[:, :, None],
                                     (RET_HEADS, cpad, RET_HEAD_DIM))
    half = RET_HEAD_DIM // 2
    inv = ROPE_BASE ** (-np.arange(half, dtype=np.float64) / half)
    ang = (pos0 + np.arange(length, dtype=np.float64))[:, None] * inv[None, :]
    cc = np.concatenate([np.cos(ang), np.cos(ang)], axis=1)
    ss = np.concatenate([-np.sin(ang), np.sin(ang)], axis=1)
    rows = lambda a: np.pad(a, ((0, padded - length), (0, 0)))
    f = lambda a: jnp.asarray(np.ascontiguousarray(a), dtype=F32)
    return cpad, gamma_c, f(pad2(mask)), f(lane(zeta)), f(lane(xi)), f(rows(cc)), f(rows(ss))


def _ret_kernel(q_ref, k_ref, v_ref, g_ref, cc_ref, ss_ref, mask_ref, zeta_ref, xi_ref, gain_ref,
                s0_ref, o_ref, s_ref, *, bb, gamma_c):
    @pl.when(pl.program_id(1) == 0)
    def _():
        s_ref[...] = s0_ref[...]

    cc = cc_ref[...]
    ss = ss_ref[...]
    half = RET_HEAD_DIM // 2

    def rope(x):
        return x * cc + pltpu.roll(x, half, axis=1) * ss

    heads = range(RET_HEADS)
    cols = [slice(h * RET_HEAD_DIM, (h + 1) * RET_HEAD_DIM) for h in heads]

    def per_batch(b, carry):
        qb = [(rope(q_ref[b, :, c]) * (RET_HEAD_DIM ** -0.5)).astype(BF16) for c in cols]
        kr = [rope(k_ref[b, :, c]) for c in cols]
        kb = [x.astype(BF16) for x in kr]
        kz = [(kr[h] * zeta_ref[h]).astype(BF16) for h in heads]
        vb = [v_ref[b, :, c] for c in cols]
        s_prev = [s_ref[b, h] for h in heads]
        scores = [(_dot_nt(qb[h], kb[h]) * mask_ref[h]).astype(BF16) for h in heads]
        cross = [_dot(qb[h], s_prev[h].astype(BF16)) * xi_ref[h] for h in heads]
        kv = [_dot_tn(kz[h], vb[h]) for h in heads]
        inner = [_dot(scores[h], vb[h]) for h in heads]
        outs = []
        for h in heads:
            s_ref[b, h] = s_prev[h] * gamma_c[h] + kv[h]
            o = inner[h] + cross[h]
            mu = jnp.mean(o, axis=-1, keepdims=True)
            oc = o - mu
            var = jnp.mean(oc * oc, axis=-1, keepdims=True)
            on = oc * lax.rsqrt(var + EPS) * gain_ref[:, cols[h]]
            gate = g_ref[b, :, cols[h]]
            outs.append((gate * _sigmoid(gate) * on).astype(BF16))
        o_ref[b] = jnp.concatenate(outs, axis=1)
        return carry

    lax.fori_loop(0, bb, per_batch, 0, unroll=2)


def _retention(q, k, v, g, s0, gn_gain, *, pos0):
    batch, length, _ = q.shape
    padded = length if length % BF16_ROWS == 0 else pl.cdiv(length, BF16_ROWS) * BF16_ROWS
    chunk, gamma_c, mask, zeta, xi, cc, ss = _ret_tables(length, padded, pos0)
    if padded != length:
        pad = lambda a: jnp.pad(a, ((0, 0), (0, padded - length), (0, 0)))
        q, k, v, g = pad(q), pad(k), pad(v), pad(g)
    bb = 8
    tok = pl.BlockSpec((bb, chunk, RET_WIDTH), lambda i, c: (i, c, 0))
    tab = pl.BlockSpec((chunk, RET_HEAD_DIM), lambda i, c: (c, 0))
    state = pl.BlockSpec((bb, RET_HEADS, RET_HEAD_DIM, RET_HEAD_DIM), lambda i, c: (i, 0, 0, 0))
    out, s_new = pl.pallas_call(
        functools.partial(_ret_kernel, bb=bb, gamma_c=gamma_c),
        grid=(batch // bb, padded // chunk),
        in_specs=[tok, tok, tok, tok, tab, tab,
                  _full((RET_HEADS, chunk, chunk)), _full((RET_HEADS, chunk, RET_HEAD_DIM)),
                  _full((RET_HEADS, chunk, RET_HEAD_DIM)), _full((1, RET_WIDTH)), state],
        out_specs=[tok, state],
        out_shape=(jax.ShapeDtypeStruct((batch, padded, RET_WIDTH), BF16),
                   jax.ShapeDtypeStruct(s0.shape, F32)),
        compiler_params=_params("parallel", "arbitrary"),
        name="retention",
    )(q, k, v, g, cc, ss, mask, zeta, xi, gn_gain, s0)
    return out[:, :length], s_new


def _outq_kernel(x_ref, ssm_ref, ret_ref, wout_ref, g_ref, wq_ref, x1_ref, q_ref):
    x1 = (x_ref[...] + _dot(ssm_ref[...], wout_ref[:SSM_WIDTH, :])
          + _dot(ret_ref[...], wout_ref[SSM_WIDTH:, :]))
    x1_ref[...] = x1
    q_ref[...] = _dot(_rmsnorm(x1, g_ref[...]).astype(BF16), wq_ref[...]).astype(BF16)


def _out_and_query(x, ssm, ret, w_out, g_xattn, w_mq):
    t = x.shape[0]
    tm = min(t, 512)
    row = lambda w: pl.BlockSpec((tm, w), lambda i: (i, 0))
    return pl.pallas_call(
        _outq_kernel,
        grid=(t // tm,),
        in_specs=[row(D_MODEL), row(SSM_WIDTH), row(RET_WIDTH), _full((D_MODEL, D_MODEL)),
                  _full((1, D_MODEL)), _full((D_MODEL, D_MODEL))],
        out_specs=[row(D_MODEL), row(D_MODEL)],
        out_shape=(jax.ShapeDtypeStruct((t, D_MODEL), F32), jax.ShapeDtypeStruct((t, D_MODEL), BF16)),
        compiler_params=_params("parallel"),
        name="out_proj_query",
    )(x, ssm, ret, w_out, g_xattn, w_mq)


def _cache_row_order(x):
    halves = MEM_HEAD_DIM // 128
    tiles = [x[:, (h * halves + d) * 128:(h * halves + d + 1) * 128]
             for d in range(halves) for h in range(MEM_HEADS)]
    return jnp.swapaxes(jnp.stack(tiles, axis=0), 0, 1).reshape(x.shape[0] * CACHE_SUB, 128)


def _memkv_kernel(m_ref, g_ref, wk_ref, wv_ref, k_ref, v_ref, kb_ref, vb_ref):
    m = _rmsnorm(m_ref[0], g_ref[...]).astype(BF16)
    k = _dot(m, wk_ref[...])
    v = _dot(m, wv_ref[...])
    k_ref[0] = _cache_row_order(k)
    v_ref[0] = _cache_row_order(v)
    kb_ref[0] = k.astype(BF16)
    vb_ref[0] = v.astype(BF16)


def _memory_kv(mem, g_mem, w_mk, w_mv):
    batch = mem.shape[0]
    tok = pl.BlockSpec((1, MEM_LEN, D_MODEL), lambda i: (i, 0, 0))
    cache = pl.BlockSpec((1, CACHE_ROWS, 128), lambda i: (i, 0, 0))
    f = jax.ShapeDtypeStruct((batch, CACHE_ROWS, 128), F32)
    h = jax.ShapeDtypeStruct((batch, MEM_LEN, D_MODEL), BF16)
    return pl.pallas_call(
        _memkv_kernel,
        grid=(batch,),
        in_specs=[tok, _full((1, D_MODEL)), _full((D_MODEL, D_MODEL)), _full((D_MODEL, D_MODEL))],
        out_specs=[cache, cache, tok, tok],
        out_shape=(f, f, h, h),
        compiler_params=_params("parallel"),
        name="memory_kv",
    )(mem, g_mem, w_mk, w_mv)


def _attn_kernel(q_ref, k_ref, v_ref, o_ref, *, bb):
    def per_batch(b, carry):
        outs = []
        for h in range(MEM_HEADS):
            cols = slice(h * MEM_HEAD_DIM, (h + 1) * MEM_HEAD_DIM)
            s = _dot_nt(q_ref[b, :, cols], k_ref[b, :, cols]) * (MEM_HEAD_DIM ** -0.5)
            e = jnp.exp(s - jnp.max(s, axis=-1, keepdims=True))
            p = e / jnp.sum(e, axis=-1, keepdims=True)
            outs.append(_dot(p.astype(BF16), v_ref[b, :, cols]).astype(BF16))
        o_ref[b] = jnp.concatenate(outs, axis=1)
        return carry

    lax.fori_loop(0, bb, per_batch, 0)


def _memory_attention(q, mk, mv):
    batch, length, _ = q.shape
    padded = length if length % BF16_ROWS == 0 else pl.cdiv(length, BF16_ROWS) * BF16_ROWS
    if padded != length:
        q = jnp.pad(q, ((0, 0), (0, padded - length), (0, 0)))
    tq = min(padded, 512)
    bb = 1 if tq >= 128 else 4
    out = pl.pallas_call(
        functools.partial(_attn_kernel, bb=bb),
        grid=(batch // bb, padded // tq),
        in_specs=[pl.BlockSpec((bb, tq, D_MODEL), lambda i, j: (i, j, 0)),
                  pl.BlockSpec((bb, MEM_LEN, D_MODEL), lambda i, j: (i, 0, 0)),
                  pl.BlockSpec((bb, MEM_LEN, D_MODEL), lambda i, j: (i, 0, 0))],
        out_specs=pl.BlockSpec((bb, tq, D_MODEL), lambda i, j: (i, j, 0)),
        out_shape=jax.ShapeDtypeStruct((batch, padded, D_MODEL), BF16),
        compiler_params=_params("parallel", "arbitrary"),
        name="memory_attention",
    )(q, mk, mv)
    return out[:, :length]


def _cached_attn_kernel(q_ref, k_ref, v_ref, o_ref, *, bb, length):
    rows = length * CACHE_SUB
    hit = (lax.broadcasted_iota(jnp.int32, (CACHE_SUB, CACHE_ROWS), 0)
           == lax.broadcasted_iota(jnp.int32, (CACHE_SUB, CACHE_ROWS), 1) % CACHE_SUB)
    first_half = lax.broadcasted_iota(jnp.int32, (length, 128), 1) % CACHE_SUB < MEM_HEADS
    tiles = CACHE_ROWS // 128

    def class_reduce(x, op):
        for shift in (8, 16, 32, 64):
            x = op(x, pltpu.roll(x, shift, axis=1))
        return x

    def softmax_rows(r):
        z = jnp.concatenate(
            [jnp.sum(jnp.where(hit, r[CACHE_SUB * t:CACHE_SUB * (t + 1)], 0.0), axis=0, keepdims=True)
             for t in range(length)], axis=0)
        parts = []
        for i in range(tiles):
            zi = z[:, 128 * i:128 * (i + 1)]
            parts.append(zi + pltpu.roll(zi, 128 - MEM_HEADS, axis=1))
        mx = parts[0]
        for pi in parts[1:]:
            mx = jnp.maximum(mx, pi)
        mx = class_reduce(mx, jnp.maximum)
        es = [jnp.exp(pi - mx) for pi in parts]
        tot = es[0]
        for ei in es[1:]:
            tot = tot + ei
        tot = class_reduce(tot, jnp.add)
        ps = []
        for ei in es:
            pi = ei / tot
            ps.append(jnp.where(first_half, pi, pltpu.roll(pi, MEM_HEADS, axis=1)))
        p = jnp.concatenate(ps, axis=1)
        return jnp.concatenate(
            [jnp.where(hit, jnp.broadcast_to(p[t:t + 1], (CACHE_SUB, CACHE_ROWS)), 0.0)
             for t in range(length)], axis=0).astype(BF16)

    scores = [_dot_nt(q_ref[b], k_ref[b].astype(BF16)) * (MEM_HEAD_DIM ** -0.5) for b in range(bb)]
    probs = [softmax_rows(r) for r in scores]
    for b in range(bb):
        o_ref[b] = _dot(probs[b], v_ref[b].astype(BF16)).astype(BF16)


def _cached_memory_attention(q, cache_k, cache_v):
    batch, length, _ = q.shape
    halves = MEM_HEAD_DIM // 128
    rows = length * CACHE_SUB
    if rows % BF16_ROWS:
        raise ValueError("cached attention needs whole bf16 tiles of query rows")

    def cache_rows(c):
        c = c.reshape(batch, MEM_LEN, MEM_HEADS, halves, 128).transpose(0, 1, 3, 2, 4)
        return c.reshape(batch, CACHE_ROWS, 128)

    qr = q.reshape(batch, length, MEM_HEADS, halves, 128).transpose(0, 1, 3, 2, 4).reshape(batch, rows, 128)
    bb = 4
    tok = pl.BlockSpec((bb, rows, 128), lambda i: (i, 0, 0))
    kv = pl.BlockSpec((bb, CACHE_ROWS, 128), lambda i: (i, 0, 0))
    out = pl.pallas_call(
        functools.partial(_cached_attn_kernel, bb=bb, length=length),
        grid=(batch // bb,),
        in_specs=[tok, kv, kv],
        out_specs=tok,
        out_shape=jax.ShapeDtypeStruct((batch, rows, 128), BF16),
        compiler_params=_params("parallel"),
        name="cached_memory_attention",
    )(qr, cache_rows(cache_k), cache_rows(cache_v))
    out = out.reshape(batch, length, halves, MEM_HEADS, 128).transpose(0, 1, 3, 2, 4)
    return out.reshape(batch, length, D_MODEL)


def _post_kernel(x1_ref, o_ref, wo_ref, gm_ref, wup_ref, wdn_ref, gf_ref, y_ref, x2_scr, h_scr, acc_scr):
    f = pl.program_id(1)

    @pl.when(f == 0)
    def _():
        x2 = x1_ref[...] + _dot(o_ref[...], wo_ref[...])
        x2_scr[...] = x2
        h_scr[...] = _rmsnorm(x2, gm_ref[...]).astype(BF16)
        acc_scr[...] = jnp.zeros_like(acc_scr)

    up = jnp.maximum(_dot(h_scr[...], wup_ref[...]), 0.0)
    acc_scr[...] += _dot((up * up).astype(BF16), wdn_ref[...])

    @pl.when(f == pl.num_programs(1) - 1)
    def _():
        y_ref[...] = _rmsnorm(x2_scr[...] + acc_scr[...], gf_ref[...])


def _attn_out_mlp(x1, o, w_mo, g_mlp, w_up, w_down, g_final):
    t = x1.shape[0]
    tm = min(t, 1024)
    tf = 1024
    row = lambda w: pl.BlockSpec((tm, w), lambda i, f: (i, 0))
    vec = pl.BlockSpec((1, D_MODEL), lambda i, f: (0, 0))
    return pl.pallas_call(
        _post_kernel,
        grid=(t // tm, D_FF // tf),
        in_specs=[row(D_MODEL), row(D_MODEL), pl.BlockSpec((D_MODEL, D_MODEL), lambda i, f: (0, 0)), vec,
                  pl.BlockSpec((D_MODEL, tf), lambda i, f: (0, f)),
                  pl.BlockSpec((tf, D_MODEL), lambda i, f: (f, 0)), vec],
        out_specs=row(D_MODEL),
        out_shape=jax.ShapeDtypeStruct((t, D_MODEL), F32),
        scratch_shapes=[pltpu.VMEM((tm, D_MODEL), F32), pltpu.VMEM((tm, D_MODEL), BF16),
                        pltpu.VMEM((tm, D_MODEL), F32)],
        compiler_params=_params("parallel", "arbitrary"),
        name="attn_out_mlp",
    )(x1, o, w_mo, g_mlp, w_up, w_down, g_final)


def _decoder_layer(x, attend, s5_re, s5_im, ret_s, pos0, p):
    batch, length, _ = x.shape
    tokens = batch * length
    xf = x.reshape(tokens, D_MODEL)
    u, q, k, v, g = _project(xf, p["g_mix"], p["w_in"])
    ssm, s5_re_new, s5_im_new = _s5_mixer(
        u.reshape(batch, length, SSM_WIDTH), s5_re.reshape(batch, SSM_LANES), s5_im.reshape(batch, SSM_LANES),
        p["ab_re"], p["ab_im"], p["b_blk_re"], p["b_blk_im"], p["c_blk_re"], p["c_blk_nim"],
        p["d_skip"], p["w_glu"], batch=batch, length=length)
    ssm = ssm.reshape(tokens, SSM_WIDTH)
    tok3 = lambda a: a.reshape(batch, length, RET_WIDTH)
    ret, ret_new = _retention(tok3(q), tok3(k), tok3(v), tok3(g), ret_s, p["ret_gn"], pos0=pos0)
    x1, qm = _out_and_query(xf, ssm, ret.reshape(tokens, RET_WIDTH), p["w_out"], p["g_xattn"], p["w_mq"])
    o = attend(qm.reshape(batch, length, D_MODEL))
    y = _attn_out_mlp(x1, o.reshape(tokens, D_MODEL), p["w_mo"], p["g_mlp"], p["w_up"], p["w_down"],
                      p["g_final"])
    return (y.reshape(batch, length, D_MODEL),
            s5_re_new.reshape(batch, SSM_GROUPS, SSM_STATE),
            s5_im_new.reshape(batch, SSM_GROUPS, SSM_STATE), ret_new)


def kernel(x_prompt, x_sample, mem_prompt, state_s5_re, state_s5_im, state_ret, cache_mem_k, cache_mem_v, g_mix, w_in, lam_re, lam_im, log_dt, b_re, b_im, c_re, c_im, d_skip, w_glu, ret_gn, w_out, g_xattn, g_mem, w_mq, w_mk, w_mv, w_mo, g_mlp, w_up, w_down, g_final):
    if g_mix.shape[0] != 1:
        raise ValueError("this kernel implements the single-layer configuration")
    bp = x_prompt.shape[0]
    vec = lambda a: a.reshape(1, -1).astype(F32)
    ab_re, ab_im, bb_re, bb_im = _discretize(lam_re[0], lam_im[0], log_dt[0], b_re[0], b_im[0])
    p = dict(
        g_mix=vec(g_mix[0]), w_in=w_in[0].astype(BF16), ab_re=ab_re, ab_im=ab_im,
        b_blk_re=_block_diag_in(bb_re).astype(BF16), b_blk_im=_block_diag_in(bb_im).astype(BF16),
        c_blk_re=_block_diag_out(c_re[0]).astype(BF16), c_blk_nim=_block_diag_out(-c_im[0]).astype(BF16),
        d_skip=vec(d_skip[0]), w_glu=w_glu[0].astype(BF16), ret_gn=vec(ret_gn[0]),
        w_out=w_out[0].astype(BF16), g_xattn=vec(g_xattn[0]), w_mq=w_mq[0].astype(BF16),
        w_mo=w_mo[0].astype(BF16), g_mlp=vec(g_mlp[0]), w_up=w_up[0].astype(BF16),
        w_down=w_down[0].astype(BF16), g_final=vec(g_final))

    mk_rows, mv_rows, mk_p, mv_p = _memory_kv(mem_prompt, vec(g_mem[0]),
                                              w_mk[0].astype(BF16), w_mv[0].astype(BF16))
    zs = jnp.zeros((bp, SSM_GROUPS, SSM_STATE), F32)
    zr = jnp.zeros((bp, RET_HEADS, RET_HEAD_DIM, RET_HEAD_DIM), F32)
    y_p, s5r_p, s5i_p, ret_p = _decoder_layer(
        x_prompt, functools.partial(_memory_attention, mk=mk_p, mv=mv_p), zs, zs, zr, 0.0, p)

    y_s, s5r_s, s5i_s, ret_s = _decoder_layer(
        x_sample, functools.partial(_cached_memory_attention, cache_k=cache_mem_k, cache_v=cache_mem_v),
        state_s5_re[0], state_s5_im[0], state_ret[0], float(PAST_LEN), p)

    halves = MEM_HEAD_DIM // 128
    kv5 = lambda a: (a.reshape(bp, MEM_LEN, halves, MEM_HEADS, 128).transpose(0, 1, 3, 2, 4)
                     .reshape(1, bp, MEM_LEN, MEM_HEADS, MEM_HEAD_DIM))
    return (y_p, y_s, s5r_p[None], s5i_p[None], ret_p[None], kv5(mk_rows), kv5(mv_rows),
            s5r_s[None], s5i_s[None], ret_s[None])
```

```python
import functools
import math

import numpy as np
import jax
import jax.numpy as jnp
from jax import lax
from jax.experimental import pallas as pl
from jax.experimental.pallas import tpu as pltpu

F32 = jnp.float32
BF16 = jnp.bfloat16

D_MODEL = 1024
SSM_WIDTH = 512
SSM_GROUP = 16
SSM_GROUPS = 32
SSM_STATE = 64
SSM_LANES = SSM_GROUPS * SSM_STATE
RET_WIDTH = 512
RET_HEADS = 4
RET_HEAD_DIM = 128
RET_CHUNK = 128
ROPE_BASE = 10000.0
MEM_LEN = 256
MEM_HEADS = 4
MEM_HEAD_DIM = 256
D_FF = 4096
PROJ_WIDTH = SSM_WIDTH + 4 * RET_WIDTH
EPS = 1e-6
PAST_LEN = 16384

CACHE_ROWS = MEM_LEN * MEM_HEADS * MEM_HEAD_DIM // 128
CACHE_SUB = MEM_HEADS * MEM_HEAD_DIM // 128

BF16_ROWS = 16
VMEM_LIMIT = 56 * 1024 * 1024


def _dot(a, b):
    return jnp.dot(a, b, preferred_element_type=F32)


def _dot_nt(a, b):
    return lax.dot_general(a, b, (((1,), (1,)), ((), ())), preferred_element_type=F32)


def _dot_tn(a, b):
    return lax.dot_general(a, b, (((0,), (0,)), ((), ())), preferred_element_type=F32)


def _rmsnorm(x, g):
    return x * lax.rsqrt(jnp.mean(x * x, axis=-1, keepdims=True) + EPS) * g


def _sigmoid(x):
    return 1.0 / (1.0 + jnp.exp(-x))


def _gelu_tanh(x):
    c = math.sqrt(2.0 / math.pi)
    return x * (0.5 * (1.0 + jnp.tanh(c * (x + 0.044715 * (x * x * x)))))


def _params(*sem):
    return pltpu.CompilerParams(dimension_semantics=sem, vmem_limit_bytes=VMEM_LIMIT)


def _full(shape):
    return pl.BlockSpec(shape, lambda *_: (0,) * len(shape))


def _disc_kernel(lr_ref, li_ref, ldt_ref, br_ref, bi_ref, abr_ref, abi_ref, bbr_ref, bbi_ref):
    lr = lr_ref[...]
    li = li_ref[...]
    dt = jnp.exp(ldt_ref[...])
    mag = jnp.exp(lr * dt)
    ab_re = mag * jnp.cos(li * dt)
    ab_im = mag * jnp.sin(li * dt)
    den = lr * lr + li * li
    f_re = ((ab_re - 1.0) * lr + ab_im * li) / den
    f_im = (ab_im * lr - (ab_re - 1.0) * li) / den
    br = br_ref[...]
    bi = bi_ref[...]
    abr_ref[...] = ab_re
    abi_ref[...] = ab_im
    bbr_ref[...] = f_re * br - f_im * bi
    bbi_ref[...] = f_re * bi + f_im * br


def _discretize(lam_re, lam_im, log_dt, b_re, b_im):
    rows = SSM_GROUPS * SSM_GROUP
    rep = lambda a: jnp.repeat(a, SSM_GROUP, axis=0)
    ldt = jnp.broadcast_to(log_dt[:, None], (SSM_GROUPS, SSM_STATE))
    bt = lambda b: b.transpose(0, 2, 1).reshape(rows, SSM_STATE)
    shp = jax.ShapeDtypeStruct((rows, SSM_STATE), F32)
    abr, abi, bbr, bbi = pl.pallas_call(
        _disc_kernel, out_shape=(shp, shp, shp, shp), name="s5_discretize",
    )(rep(lam_re), rep(lam_im), rep(ldt), bt(b_re), bt(b_im))
    ab_re = abr[::SSM_GROUP].reshape(1, SSM_LANES)
    ab_im = abi[::SSM_GROUP].reshape(1, SSM_LANES)
    return ab_re, ab_im, bbr, bbi


def _block_diag_in(bbt):
    row_g = np.arange(SSM_WIDTH)[:, None] // SSM_GROUP
    col_g = np.arange(SSM_LANES)[None, :] // SSM_STATE
    return jnp.where(row_g == col_g, jnp.tile(bbt, (1, SSM_GROUPS)), 0.0)


def _block_diag_out(c):
    ct = c.transpose(0, 2, 1).reshape(SSM_LANES, SSM_GROUP)
    row_g = np.arange(SSM_LANES)[:, None] // SSM_STATE
    col_g = np.arange(SSM_WIDTH)[None, :] // SSM_GROUP
    return jnp.where(row_g == col_g, jnp.tile(ct, (1, SSM_GROUPS)), 0.0)


def _proj_kernel(x_ref, g_ref, w_ref, u_ref, q_ref, k_ref, v_ref, gate_ref):
    h = _rmsnorm(x_ref[...], g_ref[...]).astype(BF16)
    proj = _dot(h, w_ref[...])
    u_ref[...] = proj[:, :SSM_WIDTH]
    q_ref[...] = proj[:, SSM_WIDTH:SSM_WIDTH + RET_WIDTH]
    k_ref[...] = proj[:, SSM_WIDTH + RET_WIDTH:SSM_WIDTH + 2 * RET_WIDTH]
    v_ref[...] = proj[:, SSM_WIDTH + 2 * RET_WIDTH:SSM_WIDTH + 3 * RET_WIDTH].astype(BF16)
    gate_ref[...] = proj[:, SSM_WIDTH + 3 * RET_WIDTH:]


def _project(x, g_mix, w_in):
    t = x.shape[0]
    tm = min(t, 512)
    row = lambda w: pl.BlockSpec((tm, w), lambda i: (i, 0))
    f = jax.ShapeDtypeStruct((t, 512), F32)
    return pl.pallas_call(
        _proj_kernel,
        grid=(t // tm,),
        in_specs=[row(D_MODEL), _full((1, D_MODEL)), _full((D_MODEL, PROJ_WIDTH))],
        out_specs=[row(512)] * 5,
        out_shape=(f, f, f, jax.ShapeDtypeStruct((t, 512), BF16), f),
        compiler_params=_params("parallel"),
        name="in_proj",
    )(x, g_mix, w_in)


def _s5_kernel(u_ref, h0r_ref, h0i_ref, abr_ref, abi_ref, bre_ref, bim_ref, cre_ref, ncim_ref,
               d_ref, wglu_ref, out_ref, hr_ref, hi_ref, bur_scr, bui_scr, xr_scr, xi_scr,
               *, batch, steps, unroll, width):
    half = SSM_LANES // 2
    kin = SSM_WIDTH // 2
    kout = SSM_WIDTH // 2

    @pl.when(pl.program_id(0) == 0)
    def _():
        hr_ref[...] = h0r_ref[...]
        hi_ref[...] = h0i_ref[...]

    rows = steps * batch
    u = jnp.swapaxes(u_ref[...], 0, 1).reshape(rows, SSM_WIDTH)
    ub = u.astype(BF16)
    ys = []
    for p in range(2):
        lanes = slice(half * p, half * (p + 1))
        uk = ub[:, kin * p:kin * (p + 1)]
        bur_scr[...] = _dot(uk, bre_ref[kin * p:kin * (p + 1), lanes])
        bui_scr[...] = _dot(uk, bim_ref[kin * p:kin * (p + 1), lanes])
        for j in range(half // width):
            loc = slice(j * width, (j + 1) * width)
            glob = slice(half * p + j * width, half * p + (j + 1) * width)
            ar = jnp.broadcast_to(abr_ref[:, glob], (batch, width))
            ai = jnp.broadcast_to(abi_ref[:, glob], (batch, width))

            def body(i, carry, loc=loc, ar=ar, ai=ai):
                xr, xi = carry
                rows = unroll * batch
                base = pl.multiple_of(i * rows, rows)
                out_r, out_i = [], []
                for s in range(unroll):
                    br = bur_scr[pl.ds(base + s * batch, batch), loc]
                    bi = bui_scr[pl.ds(base + s * batch, batch), loc]
                    xr, xi = ar * xr - ai * xi + br, ar * xi + ai * xr + bi
                    out_r.append(xr)
                    out_i.append(xi)
                xr_scr[pl.ds(base, rows), loc] = jnp.concatenate(out_r, axis=0).astype(BF16)
                xi_scr[pl.ds(base, rows), loc] = jnp.concatenate(out_i, axis=0).astype(BF16)
                return xr, xi

            xr, xi = lax.fori_loop(0, steps // unroll, body, (hr_ref[:, glob], hi_ref[:, glob]))
            hr_ref[:, glob] = xr
            hi_ref[:, glob] = xi
        cols = slice(kout * p, kout * (p + 1))
        ys.append(_dot(xr_scr[...], cre_ref[lanes, cols]) + _dot(xi_scr[...], ncim_ref[lanes, cols]))
    y = jnp.concatenate(ys, axis=1) + d_ref[...] * u
    z = _gelu_tanh(y)
    out = z * _sigmoid(_dot(z.astype(BF16), wglu_ref[...]))
    out_ref[...] = jnp.swapaxes(out.reshape(steps, batch, SSM_WIDTH), 0, 1).astype(BF16)


def _s5_mixer(u, h0_re, h0_im, ab_re, ab_im, b_re, b_im, c_re, nc_im, d_skip, w_glu, *, batch, length):
    steps = min(length, 128)
    unroll = max(1, min(steps, 4 * 8 // batch))
    if (unroll * batch) % BF16_ROWS:
        raise ValueError("S5 scan stores need whole bf16 tiles")
    width = 512 if batch <= 8 else 128
    rows = steps * batch
    half = SSM_LANES // 2
    kern = functools.partial(_s5_kernel, batch=batch, steps=steps, unroll=unroll, width=width)
    st = jax.ShapeDtypeStruct((batch, SSM_LANES), F32)
    return pl.pallas_call(
        kern,
        grid=(length // steps,),
        in_specs=[pl.BlockSpec((batch, steps, SSM_WIDTH), lambda i: (0, i, 0)),
                  _full((batch, SSM_LANES)), _full((batch, SSM_LANES)),
                  _full((1, SSM_LANES)), _full((1, SSM_LANES)),
                  _full((SSM_WIDTH, SSM_LANES)), _full((SSM_WIDTH, SSM_LANES)),
                  _full((SSM_LANES, SSM_WIDTH)), _full((SSM_LANES, SSM_WIDTH)),
                  _full((1, SSM_WIDTH)), _full((SSM_WIDTH, SSM_WIDTH))],
        out_specs=[pl.BlockSpec((batch, steps, SSM_WIDTH), lambda i: (0, i, 0)),
                   _full((batch, SSM_LANES)), _full((batch, SSM_LANES))],
        out_shape=(jax.ShapeDtypeStruct((batch, length, SSM_WIDTH), BF16), st, st),
        scratch_shapes=[pltpu.VMEM((rows, half), F32), pltpu.VMEM((rows, half), F32),
                        pltpu.VMEM((rows, half), BF16), pltpu.VMEM((rows, half), BF16)],
        compiler_params=_params("arbitrary"),
        name="s5_mixer",
    )(u, h0_re, h0_im, ab_re, ab_im, b_re, b_im, c_re, nc_im, d_skip, w_glu)


def _ret_tables(length, padded, pos0):
    chunk = RET_CHUNK if length % RET_CHUNK == 0 else length
    cpad = chunk if padded == length else padded
    lg = np.log(1.0 - 2.0 ** (-5.0 - np.arange(RET_HEADS, dtype=np.float64)))
    idx = np.arange(chunk, dtype=np.float64)
    diff = idx[:, None] - idx[None, :]
    mask = np.where(diff[None] >= 0, np.exp(np.maximum(diff, 0.0)[None] * lg[:, None, None]), 0.0)
    zeta = np.exp((chunk - 1.0 - idx)[None, :] * lg[:, None])
    xi = np.exp((idx + 1.0)[None, :] * lg[:, None])
    gamma_c = tuple(float(v) for v in np.exp(chunk * lg))
    pad2 = lambda a: np.pad(a, ((0, 0), (0, cpad - chunk), (0, cpad - chunk)))
    lane = lambda a: np.broadcast_to(np.pad(a, ((0, 0), (0, cpad - chunk)))[:, :, None],
                                     (RET_HEADS, cpad, RET_HEAD_DIM))
    half = RET_HEAD_DIM // 2
    inv = ROPE_BASE ** (-np.arange(half, dtype=np.float64) / half)
    ang = (pos0 + np.arange(length, dtype=np.float64))[:, None] * inv[None, :]
    cc = np.concatenate([np.cos(ang), np.cos(ang)], axis=1)
    ss = np.concatenate([-np.sin(ang), np.sin(ang)], axis=1)
    rows = lambda a: np.pad(a, ((0, padded - length), (0, 0)))
    f = lambda a: jnp.asarray(np.ascontiguousarray(a), dtype=F32)
    return cpad, gamma_c, f(pad2(mask)), f(lane(zeta)), f(lane(xi)), f(rows(cc)), f(rows(ss))


def _ret_kernel(q_ref, k_ref, v_ref, g_ref, cc_ref, ss_ref, mask_ref, zeta_ref, xi_ref, gain_ref,
                s0_ref, o_ref, s_ref, *, bb, gamma_c):
    @pl.when(pl.program_id(1) == 0)
    def _():
        s_ref[...] = s0_ref[...]

    cc = cc_ref[...]
    ss = ss_ref[...]
    half = RET_HEAD_DIM // 2

    def rope(x):
        return x * cc + pltpu.roll(x, half, axis=1) * ss

    heads = range(RET_HEADS)
    cols = [slice(h * RET_HEAD_DIM, (h + 1) * RET_HEAD_DIM) for h in heads]

    def per_batch(b, carry):
        qb = [(rope(q_ref[b, :, c]) * (RET_HEAD_DIM ** -0.5)).astype(BF16) for c in cols]
        kr = [rope(k_ref[b, :, c]) for c in cols]
        kb = [x.astype(BF16) for x in kr]
        kz = [(kr[h] * zeta_ref[h]).astype(BF16) for h in heads]
        vb = [v_ref[b, :, c] for c in cols]
        s_prev = [s_ref[b, h] for h in heads]
        scores = [(_dot_nt(qb[h], kb[h]) * mask_ref[h]).astype(BF16) for h in heads]
        cross = [_dot(qb[h], s_prev[h].astype(BF16)) * xi_ref[h] for h in heads]
        kv = [_dot_tn(kz[h], vb[h]) for h in heads]
        inner = [_dot(scores[h], vb[h]) for h in heads]
        outs = []
        for h in heads:
            s_ref[b, h] = s_prev[h] * gamma_c[h] + kv[h]
            o = inner[h] + cross[h]
            mu = jnp.mean(o, axis=-1, keepdims=True)
            oc = o - mu
            var = jnp.mean(oc * oc, axis=-1, keepdims=True)
            on = oc * lax.rsqrt(var + EPS) * gain_ref[:, cols[h]]
            gate = g_ref[b, :, cols[h]]
            outs.append((gate * _sigmoid(gate) * on).astype(BF16))
        o_ref[b] = jnp.concatenate(outs, axis=1)
        return carry

    lax.fori_loop(0, bb, per_batch, 0, unroll=2)


def _retention(q, k, v, g, s0, gn_gain, *, pos0):
    batch, length, _ = q.shape
    padded = length if length % BF16_ROWS == 0 else pl.cdiv(length, BF16_ROWS) * BF16_ROWS
    chunk, gamma_c, mask, zeta, xi, cc, ss = _ret_tables(length, padded, pos0)
    if padded != length:
        pad = lambda a: jnp.pad(a, ((0, 0), (0, padded - length), (0, 0)))
        q, k, v, g = pad(q), pad(k), pad(v), pad(g)
    bb = 8
    tok = pl.BlockSpec((bb, chunk, RET_WIDTH), lambda i, c: (i, c, 0))
    tab = pl.BlockSpec((chunk, RET_HEAD_DIM), lambda i, c: (c, 0))
    state = pl.BlockSpec((bb, RET_HEADS, RET_HEAD_DIM, RET_HEAD_DIM), lambda i, c: (i, 0, 0, 0))
    out, s_new = pl.pallas_call(
        functools.partial(_ret_kernel, bb=bb, gamma_c=gamma_c),
        grid=(batch // bb, padded // chunk),
        in_specs=[tok, tok, tok, tok, tab, tab,
                  _full((RET_HEADS, chunk, chunk)), _full((RET_HEADS, chunk, RET_HEAD_DIM)),
                  _full((RET_HEADS, chunk, RET_HEAD_DIM)), _full((1, RET_WIDTH)), state],
        out_specs=[tok, state],
        out_shape=(jax.ShapeDtypeStruct((batch, padded, RET_WIDTH), BF16),
                   jax.ShapeDtypeStruct(s0.shape, F32)),
        compiler_params=_params("parallel", "arbitrary"),
        name="retention",
    )(q, k, v, g, cc, ss, mask, zeta, xi, gn_gain, s0)
    return out[:, :length], s_new


def _outq_kernel(x_ref, ssm_ref, ret_ref, wout_ref, g_ref, wq_ref, x1_ref, q_ref):
    x1 = (x_ref[...] + _dot(ssm_ref[...], wout_ref[:SSM_WIDTH, :])
          + _dot(ret_ref[...], wout_ref[SSM_WIDTH:, :]))
    x1_ref[...] = x1
    q_ref[...] = _dot(_rmsnorm(x1, g_ref[...]).astype(BF16), wq_ref[...]).astype(BF16)


def _out_and_query(x, ssm, ret, w_out, g_xattn, w_mq):
    t = x.shape[0]
    tm = min(t, 512)
    row = lambda w: pl.BlockSpec((tm, w), lambda i: (i, 0))
    return pl.pallas_call(
        _outq_kernel,
        grid=(t // tm,),
        in_specs=[row(D_MODEL), row(SSM_WIDTH), row(RET_WIDTH), _full((D_MODEL, D_MODEL)),
                  _full((1, D_MODEL)), _full((D_MODEL, D_MODEL))],
        out_specs=[row(D_MODEL), row(D_MODEL)],
        out_shape=(jax.ShapeDtypeStruct((t, D_MODEL), F32), jax.ShapeDtypeStruct((t, D_MODEL), BF16)),
        compiler_params=_params("parallel"),
        name="out_proj_query",
    )(x, ssm, ret, w_out, g_xattn, w_mq)


def _cache_row_order(x):
    halves = MEM_HEAD_DIM // 128
    tiles = [x[:, (h * halves + d) * 128:(h * halves + d + 1) * 128]
             for d in range(halves) for h in range(MEM_HEADS)]
    return jnp.swapaxes(jnp.stack(tiles, axis=0), 0, 1).reshape(x.shape[0] * CACHE_SUB, 128)


def _memkv_kernel(m_ref, g_ref, wk_ref, wv_ref, k_ref, v_ref, kb_ref, vb_ref):
    m = _rmsnorm(m_ref[0], g_ref[...]).astype(BF16)
    k = _dot(m, wk_ref[...])
    v = _dot(m, wv_ref[...])
    k_ref[0] = _cache_row_order(k)
    v_ref[0] = _cache_row_order(v)
    kb_ref[0] = k.astype(BF16)
    vb_ref[0] = v.astype(BF16)


def _memory_kv(mem, g_mem, w_mk, w_mv):
    batch = mem.shape[0]
    tok = pl.BlockSpec((1, MEM_LEN, D_MODEL), lambda i: (i, 0, 0))
    cache = pl.BlockSpec((1, CACHE_ROWS, 128), lambda i: (i, 0, 0))
    f = jax.ShapeDtypeStruct((batch, CACHE_ROWS, 128), F32)
    h = jax.ShapeDtypeStruct((batch, MEM_LEN, D_MODEL), BF16)
    return pl.pallas_call(
        _memkv_kernel,
        grid=(batch,),
        in_specs=[tok, _full((1, D_MODEL)), _full((D_MODEL, D_MODEL)), _full((D_MODEL, D_MODEL))],
        out_specs=[cache, cache, tok, tok],
        out_shape=(f, f, h, h),
        compiler_params=_params("parallel"),
        name="memory_kv",
    )(mem, g_mem, w_mk, w_mv)


def _outattn_kernel(x_ref, ssm_ref, ret_ref, wout_ref, g_ref, wq_ref, k_ref, v_ref, x1_ref, o_ref):
    x1 = (x_ref[...] + _dot(ssm_ref[...], wout_ref[:SSM_WIDTH, :])
          + _dot(ret_ref[...], wout_ref[SSM_WIDTH:, :]))
    x1_ref[...] = x1
    q = _dot(_rmsnorm(x1, g_ref[...]).astype(BF16), wq_ref[...]).astype(BF16)
    cols = [slice(h * MEM_HEAD_DIM, (h + 1) * MEM_HEAD_DIM) for h in range(MEM_HEADS)]
    scores = [_dot_nt(q[:, c], k_ref[0, :, c]) * (MEM_HEAD_DIM ** -0.5) for c in cols]
    probs = []
    for s in scores:
        e = jnp.exp(s - jnp.max(s, axis=-1, keepdims=True))
        probs.append((e / jnp.sum(e, axis=-1, keepdims=True)).astype(BF16))
    o_ref[...] = jnp.concatenate([_dot(p, v_ref[0, :, c]).astype(BF16) for p, c in zip(probs, cols)], axis=1)


def _out_and_attention(x, ssm, ret, w_out, g_xattn, w_mq, mk, mv, *, length):
    t = x.shape[0]
    tm = min(length, 512)
    per_seq = length // tm
    row = lambda w: pl.BlockSpec((tm, w), lambda i: (i, 0))
    mem = pl.BlockSpec((1, MEM_LEN, D_MODEL), lambda i: (i // per_seq, 0, 0))
    return pl.pallas_call(
        _outattn_kernel,
        grid=(t // tm,),
        in_specs=[row(D_MODEL), row(SSM_WIDTH), row(RET_WIDTH), _full((D_MODEL, D_MODEL)),
                  _full((1, D_MODEL)), _full((D_MODEL, D_MODEL)), mem, mem],
        out_specs=[row(D_MODEL), row(D_MODEL)],
        out_shape=(jax.ShapeDtypeStruct((t, D_MODEL), F32), jax.ShapeDtypeStruct((t, D_MODEL), BF16)),
        compiler_params=_params("parallel"),
        name="out_proj_attention",
    )(x, ssm, ret, w_out, g_xattn, w_mq, mk, mv)


def _cached_attn_kernel(q_ref, k_ref, v_ref, o_ref, *, bb, length, between=None):
    rows = length * CACHE_SUB
    hit = (lax.broadcasted_iota(jnp.int32, (CACHE_SUB, CACHE_ROWS), 0)
           == lax.broadcasted_iota(jnp.int32, (CACHE_SUB, CACHE_ROWS), 1) % CACHE_SUB)
    first_half = lax.broadcasted_iota(jnp.int32, (length, 128), 1) % CACHE_SUB < MEM_HEADS
    tiles = CACHE_ROWS // 128

    def class_reduce(x, op):
        for shift in (8, 16, 32, 64):
            x = op(x, pltpu.roll(x, shift, axis=1))
        return x

    def softmax_rows(r):
        z = jnp.concatenate(
            [jnp.sum(jnp.where(hit, r[CACHE_SUB * t:CACHE_SUB * (t + 1)], 0.0), axis=0, keepdims=True)
             for t in range(length)], axis=0)
        parts = []
        for i in range(tiles):
            zi = z[:, 128 * i:128 * (i + 1)]
            parts.append(zi + pltpu.roll(zi, 128 - MEM_HEADS, axis=1))
        mx = parts[0]
        for pi in parts[1:]:
            mx = jnp.maximum(mx, pi)
        mx = class_reduce(mx, jnp.maximum)
        es = [jnp.exp(pi - mx) for pi in parts]
        tot = es[0]
        for ei in es[1:]:
            tot = tot + ei
        tot = class_reduce(tot, jnp.add)
        ps = []
        for ei in es:
            pi = ei / tot
            ps.append(jnp.where(first_half, pi, pltpu.roll(pi, MEM_HEADS, axis=1)))
        p = jnp.concatenate(ps, axis=1)
        return jnp.concatenate(
            [jnp.where(hit, jnp.broadcast_to(p[t:t + 1], (CACHE_SUB, CACHE_ROWS)), 0.0)
             for t in range(length)], axis=0).astype(BF16)

    scores = [_dot_nt(q_ref[b], k_ref[b].astype(BF16)) * (MEM_HEAD_DIM ** -0.5) for b in range(bb)]
    if between is not None:
        between()
    probs = [softmax_rows(r) for r in scores]
    for b in range(bb):
        o_ref[b] = _dot(probs[b], v_ref[b].astype(BF16)).astype(BF16)


def _to_cache_rows(a, lead):
    halves = MEM_HEAD_DIM // 128
    batch = a.size // (lead * D_MODEL)
    a = a.reshape(batch, lead, MEM_HEADS, halves, 128).transpose(0, 1, 3, 2, 4)
    return a.reshape(batch, lead * CACHE_SUB, 128)


def _from_cache_rows(a, lead):
    halves = MEM_HEAD_DIM // 128
    batch = a.shape[0]
    a = a.reshape(batch, lead, halves, MEM_HEADS, 128).transpose(0, 1, 3, 2, 4)
    return a.reshape(batch, lead, D_MODEL)


def _post_kernel(x1_ref, o_ref, wo_ref, gm_ref, wup_ref, wdn_ref, gf_ref, *rest, side_bb, side_len):
    if side_bb:
        qs_ref, ck_ref, cv_ref, y_ref, os_ref, x2_scr, h_scr, acc_scr = rest
    else:
        y_ref, x2_scr, h_scr, acc_scr = rest
    f = pl.program_id(1)

    @pl.when(f == 0)
    def _():
        x2 = x1_ref[...] + _dot(o_ref[...], wo_ref[...])
        x2_scr[...] = x2
        h_scr[...] = _rmsnorm(x2, gm_ref[...]).astype(BF16)
        acc_scr[...] = jnp.zeros_like(acc_scr)

    def mlp_chunk():
        up = jnp.maximum(_dot(h_scr[...], wup_ref[...]), 0.0)
        acc_scr[...] += _dot((up * up).astype(BF16), wdn_ref[...])

    if side_bb:
        _cached_attn_kernel(qs_ref, ck_ref, cv_ref, os_ref, bb=side_bb, length=side_len, between=mlp_chunk)
    else:
        mlp_chunk()

    @pl.when(f == pl.num_programs(1) - 1)
    def _():
        y_ref[...] = _rmsnorm(x2_scr[...] + acc_scr[...], gf_ref[...])


def _attn_out_mlp(x1, o, w_mo, g_mlp, w_up, w_down, g_final, side=None):
    t = x1.shape[0]
    tm = min(t, 1024)
    tf = 1024
    nf = D_FF // tf
    row = lambda w: pl.BlockSpec((tm, w), lambda i, f: (i, 0))
    vec = pl.BlockSpec((1, D_MODEL), lambda i, f: (0, 0))
    in_specs = [row(D_MODEL), row(D_MODEL), pl.BlockSpec((D_MODEL, D_MODEL), lambda i, f: (0, 0)), vec,
                pl.BlockSpec((D_MODEL, tf), lambda i, f: (0, f)),
                pl.BlockSpec((tf, D_MODEL), lambda i, f: (f, 0)), vec]
    out_specs = [row(D_MODEL)]
    out_shape = [jax.ShapeDtypeStruct((t, D_MODEL), F32)]
    args = [x1, o, w_mo, g_mlp, w_up, w_down, g_final]
    side_bb = side_len = 0
    if side is not None:
        q_rows = side[0]
        steps = (t // tm) * nf
        side_bb, rem = divmod(q_rows.shape[0], steps)
        if rem or not side_bb:
            raise ValueError("side attention sequences must spread evenly over the grid steps")
        side_len = q_rows.shape[1] // CACHE_SUB
        blk = lambda r: pl.BlockSpec((side_bb, r, 128), lambda i, f: (i * nf + f, 0, 0))
        in_specs += [blk(q_rows.shape[1]), blk(CACHE_ROWS), blk(CACHE_ROWS)]
        out_specs.append(blk(q_rows.shape[1]))
        out_shape.append(jax.ShapeDtypeStruct(q_rows.shape, BF16))
        args += list(side)
    outs = pl.pallas_call(
        functools.partial(_post_kernel, side_bb=side_bb, side_len=side_len),
        grid=(t // tm, nf),
        in_specs=in_specs,
        out_specs=out_specs,
        out_shape=out_shape,
        scratch_shapes=[pltpu.VMEM((tm, D_MODEL), F32), pltpu.VMEM((tm, D_MODEL), BF16),
                        pltpu.VMEM((tm, D_MODEL), F32)],
        compiler_params=_params("parallel", "arbitrary"),
        name="attn_out_mlp",
    )(*args)
    return outs if side is not None else outs[0]


def _mixers(x, s5_re, s5_im, ret_s, pos0, p):
    batch, length, _ = x.shape
    tokens = batch * length
    xf = x.reshape(tokens, D_MODEL)
    u, q, k, v, g = _project(xf, p["g_mix"], p["w_in"])
    ssm, s5_re_new, s5_im_new = _s5_mixer(
        u.reshape(batch, length, SSM_WIDTH), s5_re.reshape(batch, SSM_LANES), s5_im.reshape(batch, SSM_LANES),
        p["ab_re"], p["ab_im"], p["b_blk_re"], p["b_blk_im"], p["c_blk_re"], p["c_blk_nim"],
        p["d_skip"], p["w_glu"], batch=batch, length=length)
    tok3 = lambda a: a.reshape(batch, length, RET_WIDTH)
    ret, ret_new = _retention(tok3(q), tok3(k), tok3(v), tok3(g), ret_s, p["ret_gn"], pos0=pos0)
    states = (s5_re_new.reshape(1, batch, SSM_GROUPS, SSM_STATE),
              s5_im_new.reshape(1, batch, SSM_GROUPS, SSM_STATE), ret_new[None])
    return xf, ssm.reshape(tokens, SSM_WIDTH), ret.reshape(tokens, RET_WIDTH), states


def kernel(x_prompt, x_sample, mem_prompt, state_s5_re, state_s5_im, state_ret, cache_mem_k, cache_mem_v, g_mix, w_in, lam_re, lam_im, log_dt, b_re, b_im, c_re, c_im, d_skip, w_glu, ret_gn, w_out, g_xattn, g_mem, w_mq, w_mk, w_mv, w_mo, g_mlp, w_up, w_down, g_final):
    if g_mix.shape[0] != 1:
        raise ValueError("this kernel implements the single-layer configuration")
    bp, lp, _ = x_prompt.shape
    bs, ls, _ = x_sample.shape
    vec = lambda a: a.reshape(1, -1).astype(F32)
    ab_re, ab_im, bb_re, bb_im = _discretize(lam_re[0], lam_im[0], log_dt[0], b_re[0], b_im[0])
    p = dict(
        g_mix=vec(g_mix[0]), w_in=w_in[0].astype(BF16), ab_re=ab_re, ab_im=ab_im,
        b_blk_re=_block_diag_in(bb_re).astype(BF16), b_blk_im=_block_diag_in(bb_im).astype(BF16),
        c_blk_re=_block_diag_out(c_re[0]).astype(BF16), c_blk_nim=_block_diag_out(-c_im[0]).astype(BF16),
        d_skip=vec(d_skip[0]), w_glu=w_glu[0].astype(BF16), ret_gn=vec(ret_gn[0]))
    w_out_b, w_mq_b, g_x = w_out[0].astype(BF16), w_mq[0].astype(BF16), vec(g_xattn[0])
    mlp = (w_mo[0].astype(BF16), vec(g_mlp[0]), w_up[0].astype(BF16), w_down[0].astype(BF16), vec(g_final))

    xs, ssm_s, ret_s, states_s = _mixers(x_sample, state_s5_re[0], state_s5_im[0], state_ret[0],
                                         float(PAST_LEN), p)
    x1_s, q_s = _out_and_query(xs, ssm_s, ret_s, w_out_b, g_x, w_mq_b)
    if (ls * CACHE_SUB) % BF16_ROWS:
        raise ValueError("cached attention needs whole bf16 tiles of query rows")
    side = (_to_cache_rows(q_s, ls), _to_cache_rows(cache_mem_k, MEM_LEN), _to_cache_rows(cache_mem_v, MEM_LEN))

    mk_rows, mv_rows, mk_p, mv_p = _memory_kv(mem_prompt, vec(g_mem[0]),
                                              w_mk[0].astype(BF16), w_mv[0].astype(BF16))
    zs = jnp.zeros((bp, SSM_GROUPS, SSM_STATE), F32)
    zr = jnp.zeros((bp, RET_HEADS, RET_HEAD_DIM, RET_HEAD_DIM), F32)
    xp, ssm_p, ret_p, states_p = _mixers(x_prompt, zs, zs, zr, 0.0, p)
    x1_p, o_p = _out_and_attention(xp, ssm_p, ret_p, w_out_b, g_x, w_mq_b, mk_p, mv_p, length=lp)
    y_p, o_s = _attn_out_mlp(x1_p, o_p, *mlp, side=side)
    y_s = _attn_out_mlp(x1_s, _from_cache_rows(o_s, ls).reshape(bs * ls, D_MODEL), *mlp)

    kv5 = lambda a: _from_cache_rows(a, MEM_LEN).reshape(1, bp, MEM_LEN, MEM_HEADS, MEM_HEAD_DIM)
    return (y_p.reshape(bp, lp, D_MODEL), y_s.reshape(bs, ls, D_MODEL), *states_p, kv5(mk_rows), kv5(mv_rows),
            *states_s)
```

```python
import functools
import math

import numpy as np
import jax
import jax.numpy as jnp
from jax import lax
from jax.experimental import pallas as pl
from jax.experimental.pallas import tpu as pltpu

F32 = jnp.float32
BF16 = jnp.bfloat16

D_MODEL = 1024
SSM_WIDTH = 512
SSM_GROUP = 16
SSM_GROUPS = 32
SSM_STATE = 64
SSM_LANES = SSM_GROUPS * SSM_STATE
RET_WIDTH = 512
RET_HEADS = 4
RET_HEAD_DIM = 128
RET_CHUNK = 128
ROPE_BASE = 10000.0
MEM_LEN = 256
MEM_HEADS = 4
MEM_HEAD_DIM = 256
D_FF = 4096
PROJ_WIDTH = SSM_WIDTH + 4 * RET_WIDTH
EPS = 1e-6
PAST_LEN = 16384

CACHE_ROWS = MEM_LEN * MEM_HEADS * MEM_HEAD_DIM // 128
CACHE_SUB = MEM_HEADS * MEM_HEAD_DIM // 128

BF16_ROWS = 16
VMEM_LIMIT = 56 * 1024 * 1024


def _dot(a, b):
    return jnp.dot(a, b, preferred_element_type=F32)


def _dot_nt(a, b):
    return lax.dot_general(a, b, (((1,), (1,)), ((), ())), preferred_element_type=F32)


def _dot_tn(a, b):
    return lax.dot_general(a, b, (((0,), (0,)), ((), ())), preferred_element_type=F32)


def _rmsnorm(x, g):
    return x * lax.rsqrt(jnp.mean(x * x, axis=-1, keepdims=True) + EPS) * g


def _sigmoid(x):
    return 1.0 / (1.0 + jnp.exp(-x))


def _gelu_tanh(x):
    c = math.sqrt(2.0 / math.pi)
    return x * (0.5 * (1.0 + jnp.tanh(c * (x + 0.044715 * (x * x * x)))))


def _params(*sem):
    return pltpu.CompilerParams(dimension_semantics=sem, vmem_limit_bytes=VMEM_LIMIT)


def _full(shape):
    return pl.BlockSpec(shape, lambda *_: (0,) * len(shape))


def _disc_kernel(lr_ref, li_ref, ldt_ref, br_ref, bi_ref, abr_ref, abi_ref, bbr_ref, bbi_ref):
    lr = lr_ref[...]
    li = li_ref[...]
    dt = jnp.exp(ldt_ref[...])
    mag = jnp.exp(lr * dt)
    ab_re = mag * jnp.cos(li * dt)
    ab_im = mag * jnp.sin(li * dt)
    den = lr * lr + li * li
    f_re = ((ab_re - 1.0) * lr + ab_im * li) / den
    f_im = (ab_im * lr - (ab_re - 1.0) * li) / den
    br = br_ref[...]
    bi = bi_ref[...]
    abr_ref[...] = ab_re
    abi_ref[...] = ab_im
    bbr_ref[...] = f_re * br - f_im * bi
    bbi_ref[...] = f_re * bi + f_im * br


def _discretize(lam_re, lam_im, log_dt, b_re, b_im):
    rows = SSM_GROUPS * SSM_GROUP
    rep = lambda a: jnp.repeat(a, SSM_GROUP, axis=0)
    ldt = jnp.broadcast_to(log_dt[:, None], (SSM_GROUPS, SSM_STATE))
    bt = lambda b: b.transpose(0, 2, 1).reshape(rows, SSM_STATE)
    shp = jax.ShapeDtypeStruct((rows, SSM_STATE), F32)
    abr, abi, bbr, bbi = pl.pallas_call(
        _disc_kernel, out_shape=(shp, shp, shp, shp), name="s5_discretize",
    )(rep(lam_re), rep(lam_im), rep(ldt), bt(b_re), bt(b_im))
    ab_re = abr[::SSM_GROUP].reshape(1, SSM_LANES)
    ab_im = abi[::SSM_GROUP].reshape(1, SSM_LANES)
    return ab_re, ab_im, bbr, bbi


def _block_diag_in(bbt):
    row_g = np.arange(SSM_WIDTH)[:, None] // SSM_GROUP
    col_g = np.arange(SSM_LANES)[None, :] // SSM_STATE
    return jnp.where(row_g == col_g, jnp.tile(bbt, (1, SSM_GROUPS)), 0.0)


def _block_diag_out(c):
    ct = c.transpose(0, 2, 1).reshape(SSM_LANES, SSM_GROUP)
    row_g = np.arange(SSM_LANES)[:, None] // SSM_STATE
    col_g = np.arange(SSM_WIDTH)[None, :] // SSM_GROUP
    return jnp.where(row_g == col_g, jnp.tile(ct, (1, SSM_GROUPS)), 0.0)


def _proj_kernel(x_ref, g_ref, w_ref, u_ref, q_ref, k_ref, v_ref, gate_ref):
    h = _rmsnorm(x_ref[...], g_ref[...]).astype(BF16)
    proj = _dot(h, w_ref[...])
    u_ref[...] = proj[:, :SSM_WIDTH]
    q_ref[...] = proj[:, SSM_WIDTH:SSM_WIDTH + RET_WIDTH]
    k_ref[...] = proj[:, SSM_WIDTH + RET_WIDTH:SSM_WIDTH + 2 * RET_WIDTH]
    v_ref[...] = proj[:, SSM_WIDTH + 2 * RET_WIDTH:SSM_WIDTH + 3 * RET_WIDTH].astype(BF16)
    gate_ref[...] = proj[:, SSM_WIDTH + 3 * RET_WIDTH:]


def _project(x, g_mix, w_in):
    t = x.shape[0]
    tm = min(t, 512)
    row = lambda w: pl.BlockSpec((tm, w), lambda i: (i, 0))
    f = jax.ShapeDtypeStruct((t, 512), F32)
    return pl.pallas_call(
        _proj_kernel,
        grid=(t // tm,),
        in_specs=[row(D_MODEL), _full((1, D_MODEL)), _full((D_MODEL, PROJ_WIDTH))],
        out_specs=[row(512)] * 5,
        out_shape=(f, f, f, jax.ShapeDtypeStruct((t, 512), BF16), f),
        compiler_params=_params("parallel"),
        name="in_proj",
    )(x, g_mix, w_in)


def _s5_kernel(u_ref, h0r_ref, h0i_ref, abr_ref, abi_ref, bre_ref, bim_ref, cre_ref, ncim_ref,
               d_ref, wglu_ref, out_ref, hr_ref, hi_ref, bur_scr, bui_scr, xr_scr, xi_scr,
               *, batch, steps, unroll, width):
    half = SSM_LANES // 2
    kin = SSM_WIDTH // 2
    kout = SSM_WIDTH // 2

    @pl.when(pl.program_id(0) == 0)
    def _():
        hr_ref[...] = h0r_ref[...]
        hi_ref[...] = h0i_ref[...]

    rows = steps * batch
    u = jnp.swapaxes(u_ref[...], 0, 1).reshape(rows, SSM_WIDTH)
    ub = u.astype(BF16)
    for p in range(2):
        lanes = slice(half * p, half * (p + 1))
        uk = ub[:, kin * p:kin * (p + 1)]
        bur_scr[:, lanes] = _dot(uk, bre_ref[kin * p:kin * (p + 1), lanes])
        bui_scr[:, lanes] = _dot(uk, bim_ref[kin * p:kin * (p + 1), lanes])
    ys = []
    for p in range(2):
        lanes = slice(half * p, half * (p + 1))
        for j in range(half // width):
            glob = slice(half * p + j * width, half * p + (j + 1) * width)
            ar = jnp.broadcast_to(abr_ref[:, glob], (batch, width))
            ai = jnp.broadcast_to(abi_ref[:, glob], (batch, width))
            xr = hr_ref[:, glob]
            xi = hi_ref[:, glob]
            for i in range(steps // unroll):
                base = i * unroll * batch
                out_r, out_i = [], []
                for s in range(unroll):
                    at = slice(base + s * batch, base + (s + 1) * batch)
                    xr, xi = ar * xr - ai * xi + bur_scr[at, glob], ar * xi + ai * xr + bui_scr[at, glob]
                    out_r.append(xr)
                    out_i.append(xi)
                blk = slice(base, base + unroll * batch)
                xr_scr[blk, glob] = jnp.concatenate(out_r, axis=0).astype(BF16)
                xi_scr[blk, glob] = jnp.concatenate(out_i, axis=0).astype(BF16)
            hr_ref[:, glob] = xr
            hi_ref[:, glob] = xi
        cols = slice(kout * p, kout * (p + 1))
        ys.append(_dot(xr_scr[:, lanes], cre_ref[lanes, cols]) + _dot(xi_scr[:, lanes], ncim_ref[lanes, cols]))
    y = jnp.concatenate(ys, axis=1) + d_ref[...] * u
    z = _gelu_tanh(y)
    out = z * _sigmoid(_dot(z.astype(BF16), wglu_ref[...]))
    out_ref[...] = jnp.swapaxes(out.reshape(steps, batch, SSM_WIDTH), 0, 1).astype(BF16)


def _s5_mixer(u, h0_re, h0_im, ab_re, ab_im, b_re, b_im, c_re, nc_im, d_skip, w_glu, *, batch, length):
    steps = min(length, 128)
    unroll = max(1, min(steps, BF16_ROWS // batch))
    if (unroll * batch) % BF16_ROWS:
        raise ValueError("S5 scan stores need whole bf16 tiles")
    width = SSM_LANES // 2 if batch <= 8 else 128
    rows = steps * batch
    kern = functools.partial(_s5_kernel, batch=batch, steps=steps, unroll=unroll, width=width)
    st = jax.ShapeDtypeStruct((batch, SSM_LANES), F32)
    return pl.pallas_call(
        kern,
        grid=(length // steps,),
        in_specs=[pl.BlockSpec((batch, steps, SSM_WIDTH), lambda i: (0, i, 0)),
                  _full((batch, SSM_LANES)), _full((batch, SSM_LANES)),
                  _full((1, SSM_LANES)), _full((1, SSM_LANES)),
                  _full((SSM_WIDTH, SSM_LANES)), _full((SSM_WIDTH, SSM_LANES)),
                  _full((SSM_LANES, SSM_WIDTH)), _full((SSM_LANES, SSM_WIDTH)),
                  _full((1, SSM_WIDTH)), _full((SSM_WIDTH, SSM_WIDTH))],
        out_specs=[pl.BlockSpec((batch, steps, SSM_WIDTH), lambda i: (0, i, 0)),
                   _full((batch, SSM_LANES)), _full((batch, SSM_LANES))],
        out_shape=(jax.ShapeDtypeStruct((batch, length, SSM_WIDTH), BF16), st, st),
        scratch_shapes=[pltpu.VMEM((rows, SSM_LANES), F32), pltpu.VMEM((rows, SSM_LANES), F32),
                        pltpu.VMEM((rows, SSM_LANES), BF16), pltpu.VMEM((rows, SSM_LANES), BF16)],
        compiler_params=_params("arbitrary"),
        name="s5_mixer",
    )(u, h0_re, h0_im, ab_re, ab_im, b_re, b_im, c_re, nc_im, d_skip, w_glu)


def _ret_tables(length, padded, pos0):
    chunk = RET_CHUNK if length % RET_CHUNK == 0 else length
    cpad = chunk if padded == length else padded
    lg = np.log(1.0 - 2.0 ** (-5.0 - np.arange(RET_HEADS, dtype=np.float64)))
    idx = np.arange(chunk, dtype=np.float64)
    diff = idx[:, None] - idx[None, :]
    mask = np.where(diff[None] >= 0, np.exp(np.maximum(diff, 0.0)[None] * lg[:, None, None]), 0.0)
    zeta = np.exp((chunk - 1.0 - idx)[None, :] * lg[:, None])
    xi = np.exp((idx + 1.0)[None, :] * lg[:, None])
    gamma_c = tuple(float(v) for v in np.exp(chunk * lg))
    pad2 = lambda a: np.pad(a, ((0, 0), (0, cpad - chunk), (0, cpad - chunk)))
    lane = lambda a: np.broadcast_to(np.pad(a, ((0, 0), (0, cpad - chunk)))[:, :, None],
                                     (RET_HEADS, cpad, RET_HEAD_DIM))
    half = RET_HEAD_DIM // 2
    inv = ROPE_BASE ** (-np.arange(half, dtype=np.float64) / half)
    ang = (pos0 + np.arange(length, dtype=np.float64))[:, None] * inv[None, :]
    cc = np.concatenate([np.cos(ang), np.cos(ang)], axis=1)
    ss = np.concatenate([-np.sin(ang), np.sin(ang)], axis=1)
    rows = lambda a: np.pad(a, ((0, padded - length), (0, 0)))
    f = lambda a: jnp.asarray(np.ascontiguousarray(a), dtype=F32)
    return cpad, gamma_c, f(pad2(mask)), f(lane(zeta)), f(lane(xi)), f(rows(cc)), f(rows(ss))


def _ret_kernel(q_ref, k_ref, v_ref, g_ref, cc_ref, ss_ref, mask_ref, zeta_ref, xi_ref, gain_ref,
                s0_ref, o_ref, s_ref, *, bb, gamma_c):
    @pl.when(pl.program_id(1) == 0)
    def _():
        s_ref[...] = s0_ref[...]

    cc = cc_ref[...]
    ss = ss_ref[...]
    half = RET_HEAD_DIM // 2

    def rope(x):
        return x * cc + pltpu.roll(x, half, axis=1) * ss

    heads = range(RET_HEADS)
    cols = [slice(h * RET_HEAD_DIM, (h + 1) * RET_HEAD_DIM) for h in heads]

    def per_batch(b, carry):
        qb = [(rope(q_ref[b, :, c]) * (RET_HEAD_DIM ** -0.5)).astype(BF16) for c in cols]
        kr = [rope(k_ref[b, :, c]) for c in cols]
        kb = [x.astype(BF16) for x in kr]
        kz = [(kr[h] * zeta_ref[h]).astype(BF16) for h in heads]
        vb = [v_ref[b, :, c] for c in cols]
        s_prev = [s_ref[b, h] for h in heads]
        scores = [(_dot_nt(qb[h], kb[h]) * mask_ref[h]).astype(BF16) for h in heads]
        cross = [_dot(qb[h], s_prev[h].astype(BF16)) * xi_ref[h] for h in heads]
        kv = [_dot_tn(kz[h], vb[h]) for h in heads]
        inner = [_dot(scores[h], vb[h]) for h in heads]
        outs = []
        for h in heads:
            s_ref[b, h] = s_prev[h] * gamma_c[h] + kv[h]
            o = inner[h] + cross[h]
            mu = jnp.mean(o, axis=-1, keepdims=True)
            oc = o - mu
            var = jnp.mean(oc * oc, axis=-1, keepdims=True)
            on = oc * lax.rsqrt(var + EPS) * gain_ref[:, cols[h]]
            gate = g_ref[b, :, cols[h]]
            outs.append((gate * _sigmoid(gate) * on).astype(BF16))
        o_ref[b] = jnp.concatenate(outs, axis=1)
        return carry

    lax.fori_loop(0, bb, per_batch, 0, unroll=2)


def _retention(q, k, v, g, s0, gn_gain, *, pos0):
    batch, length, _ = q.shape
    padded = length if length % BF16_ROWS == 0 else pl.cdiv(length, BF16_ROWS) * BF16_ROWS
    chunk, gamma_c, mask, zeta, xi, cc, ss = _ret_tables(length, padded, pos0)
    if padded != length:
        pad = lambda a: jnp.pad(a, ((0, 0), (0, padded - length), (0, 0)))
        q, k, v, g = pad(q), pad(k), pad(v), pad(g)
    bb = 8
    tok = pl.BlockSpec((bb, chunk, RET_WIDTH), lambda i, c: (i, c, 0))
    tab = pl.BlockSpec((chunk, RET_HEAD_DIM), lambda i, c: (c, 0))
    state = pl.BlockSpec((bb, RET_HEADS, RET_HEAD_DIM, RET_HEAD_DIM), lambda i, c: (i, 0, 0, 0))
    out, s_new = pl.pallas_call(
        functools.partial(_ret_kernel, bb=bb, gamma_c=gamma_c),
        grid=(batch // bb, padded // chunk),
        in_specs=[tok, tok, tok, tok, tab, tab,
                  _full((RET_HEADS, chunk, chunk)), _full((RET_HEADS, chunk, RET_HEAD_DIM)),
                  _full((RET_HEADS, chunk, RET_HEAD_DIM)), _full((1, RET_WIDTH)), state],
        out_specs=[tok, state],
        out_shape=(jax.ShapeDtypeStruct((batch, padded, RET_WIDTH), BF16),
                   jax.ShapeDtypeStruct(s0.shape, F32)),
        compiler_params=_params("parallel", "arbitrary"),
        name="retention",
    )(q, k, v, g, cc, ss, mask, zeta, xi, gn_gain, s0)
    return out[:, :length], s_new


def _outq_kernel(x_ref, ssm_ref, ret_ref, wout_ref, g_ref, wq_ref, x1_ref, q_ref):
    x1 = (x_ref[...] + _dot(ssm_ref[...], wout_ref[:SSM_WIDTH, :])
          + _dot(ret_ref[...], wout_ref[SSM_WIDTH:, :]))
    x1_ref[...] = x1
    q_ref[...] = _dot(_rmsnorm(x1, g_ref[...]).astype(BF16), wq_ref[...]).astype(BF16)


def _out_and_query(x, ssm, ret, w_out, g_xattn, w_mq):
    t = x.shape[0]
    tm = min(t, 512)
    row = lambda w: pl.BlockSpec((tm, w), lambda i: (i, 0))
    return pl.pallas_call(
        _outq_kernel,
        grid=(t // tm,),
        in_specs=[row(D_MODEL), row(SSM_WIDTH), row(RET_WIDTH), _full((D_MODEL, D_MODEL)),
                  _full((1, D_MODEL)), _full((D_MODEL, D_MODEL))],
        out_specs=[row(D_MODEL), row(D_MODEL)],
        out_shape=(jax.ShapeDtypeStruct((t, D_MODEL), F32), jax.ShapeDtypeStruct((t, D_MODEL), BF16)),
        compiler_params=_params("parallel"),
        name="out_proj_query",
    )(x, ssm, ret, w_out, g_xattn, w_mq)


def _cache_row_order(x):
    halves = MEM_HEAD_DIM // 128
    tiles = [x[:, (h * halves + d) * 128:(h * halves + d + 1) * 128]
             for d in range(halves) for h in range(MEM_HEADS)]
    return jnp.swapaxes(jnp.stack(tiles, axis=0), 0, 1).reshape(x.shape[0] * CACHE_SUB, 128)


def _memkv_kernel(m_ref, g_ref, wk_ref, wv_ref, k_ref, v_ref, kb_ref, vb_ref):
    m = _rmsnorm(m_ref[0], g_ref[...]).astype(BF16)
    k = _dot(m, wk_ref[...])
    v = _dot(m, wv_ref[...])
    k_ref[0] = _cache_row_order(k)
    v_ref[0] = _cache_row_order(v)
    kb_ref[0] = k.astype(BF16)
    vb_ref[0] = v.astype(BF16)


def _memory_kv(mem, g_mem, w_mk, w_mv):
    batch = mem.shape[0]
    tok = pl.BlockSpec((1, MEM_LEN, D_MODEL), lambda i: (i, 0, 0))
    cache = pl.BlockSpec((1, CACHE_ROWS, 128), lambda i: (i, 0, 0))
    f = jax.ShapeDtypeStruct((batch, CACHE_ROWS, 128), F32)
    h = jax.ShapeDtypeStruct((batch, MEM_LEN, D_MODEL), BF16)
    return pl.pallas_call(
        _memkv_kernel,
        grid=(batch,),
        in_specs=[tok, _full((1, D_MODEL)), _full((D_MODEL, D_MODEL)), _full((D_MODEL, D_MODEL))],
        out_specs=[cache, cache, tok, tok],
        out_shape=(f, f, h, h),
        compiler_params=_params("parallel"),
        name="memory_kv",
    )(mem, g_mem, w_mk, w_mv)


def _outattn_kernel(x_ref, ssm_ref, ret_ref, wout_ref, g_ref, wq_ref, k_ref, v_ref, x1_ref, o_ref):
    x1 = (x_ref[...] + _dot(ssm_ref[...], wout_ref[:SSM_WIDTH, :])
          + _dot(ret_ref[...], wout_ref[SSM_WIDTH:, :]))
    x1_ref[...] = x1
    q = _dot(_rmsnorm(x1, g_ref[...]).astype(BF16), wq_ref[...]).astype(BF16)
    cols = [slice(h * MEM_HEAD_DIM, (h + 1) * MEM_HEAD_DIM) for h in range(MEM_HEADS)]
    scores = [_dot_nt(q[:, c], k_ref[0, :, c]) * (MEM_HEAD_DIM ** -0.5) for c in cols]
    probs = []
    for s in scores:
        e = jnp.exp(s - jnp.max(s, axis=-1, keepdims=True))
        probs.append((e / jnp.sum(e, axis=-1, keepdims=True)).astype(BF16))
    o_ref[...] = jnp.concatenate([_dot(p, v_ref[0, :, c]).astype(BF16) for p, c in zip(probs, cols)], axis=1)


def _out_and_attention(x, ssm, ret, w_out, g_xattn, w_mq, mk, mv, *, length):
    t = x.shape[0]
    tm = min(length, 512)
    per_seq = length // tm
    row = lambda w: pl.BlockSpec((tm, w), lambda i: (i, 0))
    mem = pl.BlockSpec((1, MEM_LEN, D_MODEL), lambda i: (i // per_seq, 0, 0))
    return pl.pallas_call(
        _outattn_kernel,
        grid=(t // tm,),
        in_specs=[row(D_MODEL), row(SSM_WIDTH), row(RET_WIDTH), _full((D_MODEL, D_MODEL)),
                  _full((1, D_MODEL)), _full((D_MODEL, D_MODEL)), mem, mem],
        out_specs=[row(D_MODEL), row(D_MODEL)],
        out_shape=(jax.ShapeDtypeStruct((t, D_MODEL), F32), jax.ShapeDtypeStruct((t, D_MODEL), BF16)),
        compiler_params=_params("parallel"),
        name="out_proj_attention",
    )(x, ssm, ret, w_out, g_xattn, w_mq, mk, mv)


def _cached_attn_kernel(q_ref, k_ref, v_ref, o_ref, *, bb, length, between=None):
    rows = length * CACHE_SUB
    hit = (lax.broadcasted_iota(jnp.int32, (CACHE_SUB, CACHE_ROWS), 0)
           == lax.broadcasted_iota(jnp.int32, (CACHE_SUB, CACHE_ROWS), 1) % CACHE_SUB)
    first_half = lax.broadcasted_iota(jnp.int32, (length, 128), 1) % CACHE_SUB < MEM_HEADS
    tiles = CACHE_ROWS // 128

    def class_reduce(x, op):
        for shift in (8, 16, 32, 64):
            x = op(x, pltpu.roll(x, shift, axis=1))
        return x

    def softmax_rows(r):
        z = jnp.concatenate(
            [jnp.sum(jnp.where(hit, r[CACHE_SUB * t:CACHE_SUB * (t + 1)], 0.0), axis=0, keepdims=True)
             for t in range(length)], axis=0)
        parts = []
        for i in range(tiles):
            zi = z[:, 128 * i:128 * (i + 1)]
            parts.append(zi + pltpu.roll(zi, 128 - MEM_HEADS, axis=1))
        mx = parts[0]
        for pi in parts[1:]:
            mx = jnp.maximum(mx, pi)
        mx = class_reduce(mx, jnp.maximum)
        es = [jnp.exp(pi - mx) for pi in parts]
        tot = es[0]
        for ei in es[1:]:
            tot = tot + ei
        tot = class_reduce(tot, jnp.add)
        ps = []
        for ei in es:
            pi = ei / tot
            ps.append(jnp.where(first_half, pi, pltpu.roll(pi, MEM_HEADS, axis=1)))
        p = jnp.concatenate(ps, axis=1)
        return jnp.concatenate(
            [jnp.where(hit, jnp.broadcast_to(p[t:t + 1], (CACHE_SUB, CACHE_ROWS)), 0.0)
             for t in range(length)], axis=0).astype(BF16)

    scores = [_dot_nt(q_ref[b], k_ref[b].astype(BF16)) * (MEM_HEAD_DIM ** -0.5) for b in range(bb)]
    if between is not None:
        between()
    probs = [softmax_rows(r) for r in scores]
    for b in range(bb):
        o_ref[b] = _dot(probs[b], v_ref[b].astype(BF16)).astype(BF16)


def _to_cache_rows(a, lead):
    halves = MEM_HEAD_DIM // 128
    batch = a.size // (lead * D_MODEL)
    a = a.reshape(batch, lead, MEM_HEADS, halves, 128).transpose(0, 1, 3, 2, 4)
    return a.reshape(batch, lead * CACHE_SUB, 128)


def _from_cache_rows(a, lead):
    halves = MEM_HEAD_DIM // 128
    batch = a.shape[0]
    a = a.reshape(batch, lead, halves, MEM_HEADS, 128).transpose(0, 1, 3, 2, 4)
    return a.reshape(batch, lead, D_MODEL)


def _post_kernel(x1_ref, o_ref, wo_ref, gm_ref, wup_ref, wdn_ref, gf_ref, *rest, side_bb, side_len):
    if side_bb:
        qs_ref, ck_ref, cv_ref, y_ref, os_ref, x2_scr, h_scr, acc_scr = rest
    else:
        y_ref, x2_scr, h_scr, acc_scr = rest
    f = pl.program_id(1)

    @pl.when(f == 0)
    def _():
        x2 = x1_ref[...] + _dot(o_ref[...], wo_ref[...])
        x2_scr[...] = x2
        h_scr[...] = _rmsnorm(x2, gm_ref[...]).astype(BF16)
        acc_scr[...] = jnp.zeros_like(acc_scr)

    def mlp_chunk():
        up = jnp.maximum(_dot(h_scr[...], wup_ref[...]), 0.0)
        acc_scr[...] += _dot((up * up).astype(BF16), wdn_ref[...])

    if side_bb:
        _cached_attn_kernel(qs_ref, ck_ref, cv_ref, os_ref, bb=side_bb, length=side_len, between=mlp_chunk)
    else:
        mlp_chunk()

    @pl.when(f == pl.num_programs(1) - 1)
    def _():
        y_ref[...] = _rmsnorm(x2_scr[...] + acc_scr[...], gf_ref[...])


def _attn_out_mlp(x1, o, w_mo, g_mlp, w_up, w_down, g_final, side=None):
    t = x1.shape[0]
    tm = min(t, 1024)
    tf = 1024
    nf = D_FF // tf
    row = lambda w: pl.BlockSpec((tm, w), lambda i, f: (i, 0))
    vec = pl.BlockSpec((1, D_MODEL), lambda i, f: (0, 0))
    in_specs = [row(D_MODEL), row(D_MODEL), pl.BlockSpec((D_MODEL, D_MODEL), lambda i, f: (0, 0)), vec,
                pl.BlockSpec((D_MODEL, tf), lambda i, f: (0, f)),
                pl.BlockSpec((tf, D_MODEL), lambda i, f: (f, 0)), vec]
    out_specs = [row(D_MODEL)]
    out_shape = [jax.ShapeDtypeStruct((t, D_MODEL), F32)]
    args = [x1, o, w_mo, g_mlp, w_up, w_down, g_final]
    side_bb = side_len = 0
    if side is not None:
        q_rows = side[0]
        steps = (t // tm) * nf
        side_bb, rem = divmod(q_rows.shape[0], steps)
        if rem or not side_bb:
            raise ValueError("side attention sequences must spread evenly over the grid steps")
        side_len = q_rows.shape[1] // CACHE_SUB
        blk = lambda r: pl.BlockSpec((side_bb, r, 128), lambda i, f: (i * nf + f, 0, 0))
        in_specs += [blk(q_rows.shape[1]), blk(CACHE_ROWS), blk(CACHE_ROWS)]
        out_specs.append(blk(q_rows.shape[1]))
        out_shape.append(jax.ShapeDtypeStruct(q_rows.shape, BF16))
        args += list(side)
    outs = pl.pallas_call(
        functools.partial(_post_kernel, side_bb=side_bb, side_len=side_len),
        grid=(t // tm, nf),
        in_specs=in_specs,
        out_specs=out_specs,
        out_shape=out_shape,
        scratch_shapes=[pltpu.VMEM((tm, D_MODEL), F32), pltpu.VMEM((tm, D_MODEL), BF16),
                        pltpu.VMEM((tm, D_MODEL), F32)],
        compiler_params=_params("parallel", "arbitrary"),
        name="attn_out_mlp",
    )(*args)
    return outs if side is not None else outs[0]


def _mixers(x, s5_re, s5_im, ret_s, pos0, p):
    batch, length, _ = x.shape
    tokens = batch * length
    xf = x.reshape(tokens, D_MODEL)
    u, q, k, v, g = _project(xf, p["g_mix"], p["w_in"])
    ssm, s5_re_new, s5_im_new = _s5_mixer(
        u.reshape(batch, length, SSM_WIDTH), s5_re.reshape(batch, SSM_LANES), s5_im.reshape(batch, SSM_LANES),
        p["ab_re"], p["ab_im"], p["b_blk_re"], p["b_blk_im"], p["c_blk_re"], p["c_blk_nim"],
        p["d_skip"], p["w_glu"], batch=batch, length=length)
    tok3 = lambda a: a.reshape(batch, length, RET_WIDTH)
    ret, ret_new = _retention(tok3(q), tok3(k), tok3(v), tok3(g), ret_s, p["ret_gn"], pos0=pos0)
    states = (s5_re_new.reshape(1, batch, SSM_GROUPS, SSM_STATE),
              s5_im_new.reshape(1, batch, SSM_GROUPS, SSM_STATE), ret_new[None])
    return xf, ssm.reshape(tokens, SSM_WIDTH), ret.reshape(tokens, RET_WIDTH), states


def kernel(x_prompt, x_sample, mem_prompt, state_s5_re, state_s5_im, state_ret, cache_mem_k, cache_mem_v, g_mix, w_in, lam_re, lam_im, log_dt, b_re, b_im, c_re, c_im, d_skip, w_glu, ret_gn, w_out, g_xattn, g_mem, w_mq, w_mk, w_mv, w_mo, g_mlp, w_up, w_down, g_final):
    if g_mix.shape[0] != 1:
        raise ValueError("this kernel implements the single-layer configuration")
    bp, lp, _ = x_prompt.shape
    bs, ls, _ = x_sample.shape
    vec = lambda a: a.reshape(1, -1).astype(F32)
    ab_re, ab_im, bb_re, bb_im = _discretize(lam_re[0], lam_im[0], log_dt[0], b_re[0], b_im[0])
    p = dict(
        g_mix=vec(g_mix[0]), w_in=w_in[0].astype(BF16), ab_re=ab_re, ab_im=ab_im,
        b_blk_re=_block_diag_in(bb_re).astype(BF16), b_blk_im=_block_diag_in(bb_im).astype(BF16),
        c_blk_re=_block_diag_out(c_re[0]).astype(BF16), c_blk_nim=_block_diag_out(-c_im[0]).astype(BF16),
        d_skip=vec(d_skip[0]), w_glu=w_glu[0].astype(BF16), ret_gn=vec(ret_gn[0]))
    w_out_b, w_mq_b, g_x = w_out[0].astype(BF16), w_mq[0].astype(BF16), vec(g_xattn[0])
    mlp = (w_mo[0].astype(BF16), vec(g_mlp[0]), w_up[0].astype(BF16), w_down[0].astype(BF16), vec(g_final))

    xs, ssm_s, ret_s, states_s = _mixers(x_sample, state_s5_re[0], state_s5_im[0], state_ret[0],
                                         float(PAST_LEN), p)
    x1_s, q_s = _out_and_query(xs, ssm_s, ret_s, w_out_b, g_x, w_mq_b)
    if (ls * CACHE_SUB) % BF16_ROWS:
        raise ValueError("cached attention needs whole bf16 tiles of query rows")
    side = (_to_cache_rows(q_s, ls), _to_cache_rows(cache_mem_k, MEM_LEN), _to_cache_rows(cache_mem_v, MEM_LEN))

    mk_rows, mv_rows, mk_p, mv_p = _memory_kv(mem_prompt, vec(g_mem[0]),
                                              w_mk[0].astype(BF16), w_mv[0].astype(BF16))
    zs = jnp.zeros((bp, SSM_GROUPS, SSM_STATE), F32)
    zr = jnp.zeros((bp, RET_HEADS, RET_HEAD_DIM, RET_HEAD_DIM), F32)
    xp, ssm_p, ret_p, states_p = _mixers(x_prompt, zs, zs, zr, 0.0, p)
    x1_p, o_p = _out_and_attention(xp, ssm_p, ret_p, w_out_b, g_x, w_mq_b, mk_p, mv_p, length=lp)
    y_p, o_s = _attn_out_mlp(x1_p, o_p, *mlp, side=side)
    y_s = _attn_out_mlp(x1_s, _from_cache_rows(o_s, ls).reshape(bs * ls, D_MODEL), *mlp)

    kv5 = lambda a: _from_cache_rows(a, MEM_LEN).reshape(1, bp, MEM_LEN, MEM_HEADS, MEM_HEAD_DIM)
    return (y_p.reshape(bp, lp, D_MODEL), y_s.reshape(bs, ls, D_MODEL), *states_p, kv5(mk_rows), kv5(mv_rows),
            *states_s)
```

```python
import functools
import math

import numpy as np
import jax
import jax.numpy as jnp
from jax import lax
from jax.experimental import pallas as pl
from jax.experimental.pallas import tpu as pltpu

F32 = jnp.float32
BF16 = jnp.bfloat16

D_MODEL = 1024
SSM_WIDTH = 512
SSM_GROUP = 16
SSM_GROUPS = 32
SSM_STATE = 64
SSM_LANES = SSM_GROUPS * SSM_STATE
RET_WIDTH = 512
RET_HEADS = 4
RET_HEAD_DIM = 128
RET_CHUNK = 128
ROPE_BASE = 10000.0
MEM_LEN = 256
MEM_HEADS = 4
MEM_HEAD_DIM = 256
D_FF = 4096
PROJ_WIDTH = SSM_WIDTH + 4 * RET_WIDTH
EPS = 1e-6
PAST_LEN = 16384

CACHE_ROWS = MEM_LEN * MEM_HEADS * MEM_HEAD_DIM // 128
CACHE_SUB = MEM_HEADS * MEM_HEAD_DIM // 128

BF16_ROWS = 16
VMEM_LIMIT = 56 * 1024 * 1024


def _dot(a, b):
    return jnp.dot(a, b, preferred_element_type=F32)


def _dot_nt(a, b):
    return lax.dot_general(a, b, (((1,), (1,)), ((), ())), preferred_element_type=F32)


def _dot_tn(a, b):
    return lax.dot_general(a, b, (((0,), (0,)), ((), ())), preferred_element_type=F32)


def _rmsnorm(x, g):
    return x * lax.rsqrt(jnp.mean(x * x, axis=-1, keepdims=True) + EPS) * g


def _sigmoid(x):
    return 1.0 / (1.0 + jnp.exp(-x))


def _gelu_tanh(x):
    c = math.sqrt(2.0 / math.pi)
    return x * (0.5 * (1.0 + jnp.tanh(c * (x + 0.044715 * (x * x * x)))))


def _params(*sem):
    return pltpu.CompilerParams(dimension_semantics=sem, vmem_limit_bytes=VMEM_LIMIT)


def _full(shape):
    return pl.BlockSpec(shape, lambda *_: (0,) * len(shape))


SSM_BLOCK = 4
SSM_TILE_GROUPS = 4
SSM_TILES = SSM_GROUPS // SSM_TILE_GROUPS


def _s5_prep_kernel(lr_ref, li_ref, ldt_ref, br_ref, bi_ref, cr_ref, ci_ref,
                    ar_ref, ai_ref, winr_ref, wini_ref, woutr_ref, wouti_ref, k_ref):
    lr = lr_ref[...]
    li = li_ref[...]
    dt = jnp.exp(ldt_ref[...])
    mag = jnp.exp(lr * dt)
    ab_re = mag * jnp.cos(li * dt)
    ab_im = mag * jnp.sin(li * dt)
    den = lr * lr + li * li
    f_re = ((ab_re - 1.0) * lr + ab_im * li) / den
    f_im = (ab_im * lr - (ab_re - 1.0) * li) / den
    bb_re = f_re * br_ref[...] - f_im * bi_ref[...]
    bb_im = f_re * bi_ref[...] + f_im * br_ref[...]
    cr = cr_ref[...]
    ci = ci_ref[...]
    pows = [(jnp.ones_like(ab_re), jnp.zeros_like(ab_re))]
    for _ in range(SSM_BLOCK):
        pr, pi = pows[-1]
        pows.append((pr * ab_re - pi * ab_im, pr * ab_im + pi * ab_re))
    ar_ref[...], ai_ref[...] = pows[SSM_BLOCK]
    grouped = lambda a: a.reshape(SSM_GROUPS, SSM_GROUP, SSM_STATE)
    for s in range(SSM_BLOCK):
        pr, pi = pows[SSM_BLOCK - 1 - s]
        winr_ref[s] = pr * bb_re - pi * bb_im
        wini_ref[s] = pr * bb_im + pi * bb_re
        pr, pi = pows[s + 1]
        woutr_ref[s] = cr * pr - ci * pi
        wouti_ref[s] = -(cr * pi + ci * pr)
        pr, pi = pows[s]
        er = grouped(pr * bb_re - pi * bb_im)
        ei = grouped(pr * bb_im + pi * bb_re)
        contract = lambda a, b: lax.dot_general(a, b, (((2,), (2,)), ((0,), (0,))),
                                                precision=lax.Precision.HIGHEST, preferred_element_type=F32)
        k_ref[s] = contract(grouped(cr), er) - contract(grouped(ci), ei)


def _s5_block_maps(lam_re, lam_im, log_dt, b_re, b_im, c_re, c_im):
    rows = SSM_GROUPS * SSM_GROUP
    R, T, TG = SSM_BLOCK, SSM_TILES, SSM_TILE_GROUPS
    rep = lambda a: jnp.repeat(a, SSM_GROUP, axis=0)
    ldt = jnp.broadcast_to(log_dt[:, None], (SSM_GROUPS, SSM_STATE))
    bt = lambda b: b.transpose(0, 2, 1).reshape(rows, SSM_STATE)
    ct = lambda c: c.reshape(rows, SSM_STATE)
    one = jax.ShapeDtypeStruct((rows, SSM_STATE), F32)
    lagged = jax.ShapeDtypeStruct((R, rows, SSM_STATE), F32)
    ar, ai, winr, wini, woutr, wouti, kern = pl.pallas_call(
        _s5_prep_kernel,
        out_shape=(one, one, lagged, lagged, lagged, lagged,
                   jax.ShapeDtypeStruct((R, SSM_GROUPS, SSM_GROUP, SSM_GROUP), F32)),
        name="s5_block_maps",
    )(rep(lam_re), rep(lam_im), rep(ldt), bt(b_re), bt(b_im), ct(c_re), ct(c_im))
    a_re = ar[::SSM_GROUP].reshape(1, SSM_LANES)
    a_im = ai[::SSM_GROUP].reshape(1, SSM_LANES)
    eye = jnp.eye(TG, dtype=F32)
    split = lambda w: w.reshape(R, T, TG, SSM_GROUP, SSM_STATE)
    w_in = lambda w: jnp.einsum("stghp,gk->tsghkp", split(w), eye).reshape(T, 256, 256).astype(BF16)
    w_out = lambda w: jnp.einsum("jtghp,gk->tgpjkh", split(w), eye).reshape(T, 256, 256).astype(BF16)
    zero = jnp.zeros_like(kern[0])
    lag_pairs = jnp.stack([jnp.stack([kern[j - s] if j >= s else zero for j in range(R)]) for s in range(R)])
    lag_pairs = lag_pairs.reshape(R, R, T, TG, SSM_GROUP, SSM_GROUP)
    direct = jnp.einsum("sjtgoi,gk->tsgijko", lag_pairs, eye).reshape(T, 256, 256).astype(BF16)
    return a_re, a_im, w_in(winr), w_in(wini), w_out(woutr), w_out(wouti), direct


def _proj_kernel(x_ref, g_ref, w_ref, u_ref, q_ref, k_ref, v_ref, gate_ref):
    h = _rmsnorm(x_ref[...], g_ref[...]).astype(BF16)
    proj = _dot(h, w_ref[...])
    u_ref[...] = proj[:, :SSM_WIDTH]
    q_ref[...] = proj[:, SSM_WIDTH:SSM_WIDTH + RET_WIDTH]
    k_ref[...] = proj[:, SSM_WIDTH + RET_WIDTH:SSM_WIDTH + 2 * RET_WIDTH]
    v_ref[...] = proj[:, SSM_WIDTH + 2 * RET_WIDTH:SSM_WIDTH + 3 * RET_WIDTH].astype(BF16)
    gate_ref[...] = proj[:, SSM_WIDTH + 3 * RET_WIDTH:]


def _project(x, g_mix, w_in):
    t = x.shape[0]
    tm = min(t, 1024)
    row = lambda w: pl.BlockSpec((tm, w), lambda i: (i, 0))
    f = jax.ShapeDtypeStruct((t, 512), F32)
    return pl.pallas_call(
        _proj_kernel,
        grid=(t // tm,),
        in_specs=[row(D_MODEL), _full((1, D_MODEL)), _full((D_MODEL, PROJ_WIDTH))],
        out_specs=[row(512)] * 5,
        out_shape=(f, f, f, jax.ShapeDtypeStruct((t, 512), BF16), f),
        compiler_params=_params("parallel"),
        name="in_proj",
    )(x, g_mix, w_in)


def _s5_kernel(u_ref, h0r_ref, h0i_ref, abr_ref, abi_ref, bre_ref, bim_ref, cre_ref, ncim_ref,
               d_ref, wglu_ref, out_ref, hr_ref, hi_ref, bur_scr, bui_scr, xr_scr, xi_scr,
               *, batch, steps, unroll, width):
    half = SSM_LANES // 2
    kin = SSM_WIDTH // 2
    kout = SSM_WIDTH // 2

    @pl.when(pl.program_id(0) == 0)
    def _():
        hr_ref[...] = h0r_ref[...]
        hi_ref[...] = h0i_ref[...]

    rows = steps * batch
    u = jnp.swapaxes(u_ref[...], 0, 1).reshape(rows, SSM_WIDTH)
    ub = u.astype(BF16)
    for p in range(2):
        lanes = slice(half * p, half * (p + 1))
        uk = ub[:, kin * p:kin * (p + 1)]
        bur_scr[:, lanes] = _dot(uk, bre_ref[kin * p:kin * (p + 1), lanes])
        bui_scr[:, lanes] = _dot(uk, bim_ref[kin * p:kin * (p + 1), lanes])
    ys = []
    for p in range(2):
        lanes = slice(half * p, half * (p + 1))
        for j in range(half // width):
            glob = slice(half * p + j * width, half * p + (j + 1) * width)
            ar = jnp.broadcast_to(abr_ref[:, glob], (batch, width))
            ai = jnp.broadcast_to(abi_ref[:, glob], (batch, width))
            xr = hr_ref[:, glob]
            xi = hi_ref[:, glob]
            for i in range(steps // unroll):
                base = i * unroll * batch
                out_r, out_i = [], []
                for s in range(unroll):
                    at = slice(base + s * batch, base + (s + 1) * batch)
                    xr, xi = ar * xr - ai * xi + bur_scr[at, glob], ar * xi + ai * xr + bui_scr[at, glob]
                    out_r.append(xr)
                    out_i.append(xi)
                blk = slice(base, base + unroll * batch)
                xr_scr[blk, glob] = jnp.concatenate(out_r, axis=0).astype(BF16)
                xi_scr[blk, glob] = jnp.concatenate(out_i, axis=0).astype(BF16)
            hr_ref[:, glob] = xr
            hi_ref[:, glob] = xi
        cols = slice(kout * p, kout * (p + 1))
        ys.append(_dot(xr_scr[:, lanes], cre_ref[lanes, cols]) + _dot(xi_scr[:, lanes], ncim_ref[lanes, cols]))
    y = jnp.concatenate(ys, axis=1) + d_ref[...] * u
    z = _gelu_tanh(y)
    out = z * _sigmoid(_dot(z.astype(BF16), wglu_ref[...]))
    out_ref[...] = jnp.swapaxes(out.reshape(steps, batch, SSM_WIDTH), 0, 1).astype(BF16)


def _s5_mixer(u, h0_re, h0_im, ab_re, ab_im, b_re, b_im, c_re, nc_im, d_skip, w_glu, *, batch, length):
    steps = min(length, 128)
    unroll = max(1, min(steps, BF16_ROWS // batch))
    if (unroll * batch) % BF16_ROWS:
        raise ValueError("S5 scan stores need whole bf16 tiles")
    width = SSM_LANES // 2 if batch <= 8 else 128
    rows = steps * batch
    kern = functools.partial(_s5_kernel, batch=batch, steps=steps, unroll=unroll, width=width)
    st = jax.ShapeDtypeStruct((batch, SSM_LANES), F32)
    return pl.pallas_call(
        kern,
        grid=(length // steps,),
        in_specs=[pl.BlockSpec((batch, steps, SSM_WIDTH), lambda i: (0, i, 0)),
                  _full((batch, SSM_LANES)), _full((batch, SSM_LANES)),
                  _full((1, SSM_LANES)), _full((1, SSM_LANES)),
                  _full((SSM_WIDTH, SSM_LANES)), _full((SSM_WIDTH, SSM_LANES)),
                  _full((SSM_LANES, SSM_WIDTH)), _full((SSM_LANES, SSM_WIDTH)),
                  _full((1, SSM_WIDTH)), _full((SSM_WIDTH, SSM_WIDTH))],
        out_specs=[pl.BlockSpec((batch, steps, SSM_WIDTH), lambda i: (0, i, 0)),
                   _full((batch, SSM_LANES)), _full((batch, SSM_LANES))],
        out_shape=(jax.ShapeDtypeStruct((batch, length, SSM_WIDTH), BF16), st, st),
        scratch_shapes=[pltpu.VMEM((rows, SSM_LANES), F32), pltpu.VMEM((rows, SSM_LANES), F32),
                        pltpu.VMEM((rows, SSM_LANES), BF16), pltpu.VMEM((rows, SSM_LANES), BF16)],
        compiler_params=_params("arbitrary"),
        name="s5_mixer",
    )(u, h0_re, h0_im, ab_re, ab_im, b_re, b_im, c_re, nc_im, d_skip, w_glu)


def _s5_block_kernel(u_ref, h0r_ref, h0i_ref, ar_ref, ai_ref, winr_ref, wini_ref, woutr_ref, wouti_ref,
                     kdir_ref, d_ref, wglu_ref, out_ref, hr_ref, hi_ref, vr_scr, vi_scr, xr_scr, xi_scr,
                     *, batch, steps, pair, width):
    R, T = SSM_BLOCK, SSM_TILES
    chan = SSM_WIDTH // T
    blocks = steps // R
    rows, brow = steps * batch, blocks * batch

    @pl.when(pl.program_id(0) == 0)
    def _():
        hr_ref[...] = h0r_ref[...]
        hi_ref[...] = h0i_ref[...]

    u = jnp.swapaxes(u_ref[...], 0, 1).reshape(rows, SSM_WIDTH)
    u4 = u.reshape(blocks, R, batch, SSM_WIDTH)
    lag = [u4[:, s].reshape(brow, SSM_WIDTH) for s in range(R)]
    direct = []
    for n in range(T):
        un = jnp.concatenate([lag[s][:, chan * n:chan * (n + 1)] for s in range(R)], axis=1).astype(BF16)
        lanes = slice(256 * n, 256 * (n + 1))
        vr_scr[:, lanes] = _dot(un, winr_ref[n])
        vi_scr[:, lanes] = _dot(un, wini_ref[n])
        direct.append(_dot(un, kdir_ref[n]))

    for j in range(SSM_LANES // width):
        glob = slice(j * width, (j + 1) * width)
        ar = jnp.broadcast_to(ar_ref[:, glob], (batch, width))
        ai = jnp.broadcast_to(ai_ref[:, glob], (batch, width))
        xr = hr_ref[:, glob]
        xi = hi_ref[:, glob]
        for i in range(blocks // pair):
            prev_r, prev_i = [], []
            for s in range(pair):
                at = slice((i * pair + s) * batch, (i * pair + s + 1) * batch)
                prev_r.append(xr)
                prev_i.append(xi)
                xr, xi = ar * xr - ai * xi + vr_scr[at, glob], ar * xi + ai * xr + vi_scr[at, glob]
            blk = slice(i * pair * batch, (i + 1) * pair * batch)
            xr_scr[blk, glob] = jnp.concatenate(prev_r, axis=0).astype(BF16)
            xi_scr[blk, glob] = jnp.concatenate(prev_i, axis=0).astype(BF16)
        hr_ref[:, glob] = xr
        hi_ref[:, glob] = xi

    y4 = []
    for n in range(T):
        lanes = slice(256 * n, 256 * (n + 1))
        y4.append(_dot(xr_scr[:, lanes], woutr_ref[n]) + _dot(xi_scr[:, lanes], wouti_ref[n]) + direct[n])
    per_lag = [jnp.concatenate([y4[n][:, chan * j:chan * (j + 1)] for n in range(T)], axis=1)
               .reshape(blocks, batch, SSM_WIDTH) for j in range(R)]
    y = jnp.stack(per_lag, axis=1).reshape(rows, SSM_WIDTH) + d_ref[...] * u
    z = _gelu_tanh(y)
    out = z * _sigmoid(_dot(z.astype(BF16), wglu_ref[...]))
    out_ref[...] = jnp.swapaxes(out.reshape(steps, batch, SSM_WIDTH), 0, 1).astype(BF16)


def _s5_mixer_blocked(u, h0_re, h0_im, maps, d_skip, w_glu, *, batch, length):
    a_re, a_im, winr, wini, woutr, wouti, direct = maps
    steps = min(length, 128)
    if steps % SSM_BLOCK or length % steps:
        raise ValueError("sequence length must be a multiple of the S5 block")
    blocks = steps // SSM_BLOCK
    pair = max(1, min(blocks, BF16_ROWS // batch))
    if (pair * batch) % BF16_ROWS or blocks % pair:
        raise ValueError("S5 state stores need whole bf16 tiles")
    width = SSM_LANES // 2 if batch <= 8 else 128
    brow = blocks * batch
    kern = functools.partial(_s5_block_kernel, batch=batch, steps=steps, pair=pair, width=width)
    st = jax.ShapeDtypeStruct((batch, SSM_LANES), F32)
    tile = _full((SSM_TILES, 256, 256))
    return pl.pallas_call(
        kern,
        grid=(length // steps,),
        in_specs=[pl.BlockSpec((batch, steps, SSM_WIDTH), lambda i: (0, i, 0)),
                  _full((batch, SSM_LANES)), _full((batch, SSM_LANES)),
                  _full((1, SSM_LANES)), _full((1, SSM_LANES)),
                  tile, tile, tile, tile, tile,
                  _full((1, SSM_WIDTH)), _full((SSM_WIDTH, SSM_WIDTH))],
        out_specs=[pl.BlockSpec((batch, steps, SSM_WIDTH), lambda i: (0, i, 0)),
                   _full((batch, SSM_LANES)), _full((batch, SSM_LANES))],
        out_shape=(jax.ShapeDtypeStruct((batch, length, SSM_WIDTH), BF16), st, st),
        scratch_shapes=[pltpu.VMEM((brow, SSM_LANES), F32), pltpu.VMEM((brow, SSM_LANES), F32),
                        pltpu.VMEM((brow, SSM_LANES), BF16), pltpu.VMEM((brow, SSM_LANES), BF16)],
        compiler_params=_params("arbitrary"),
        name="s5_mixer",
    )(u, h0_re, h0_im, a_re, a_im, winr, wini, woutr, wouti, direct, d_skip, w_glu)


def _ret_tables(length, padded, pos0):
    chunk = RET_CHUNK if length % RET_CHUNK == 0 else length
    cpad = chunk if padded == length else padded
    lg = np.log(1.0 - 2.0 ** (-5.0 - np.arange(RET_HEADS, dtype=np.float64)))
    idx = np.arange(chunk, dtype=np.float64)
    diff = idx[:, None] - idx[None, :]
    mask = np.where(diff[None] >= 0, np.exp(np.maximum(diff, 0.0)[None] * lg[:, None, None]), 0.0)
    zeta = np.exp((chunk - 1.0 - idx)[None, :] * lg[:, None])
    xi = np.exp((idx + 1.0)[None, :] * lg[:, None])
    gamma_c = tuple(float(v) for v in np.exp(chunk * lg))
    pad2 = lambda a: np.pad(a, ((0, 0), (0, cpad - chunk), (0, cpad - chunk)))
    lane = lambda a: np.broadcast_to(np.pad(a, ((0, 0), (0, cpad - chunk)))[:, :, None],
                                     (RET_HEADS, cpad, RET_HEAD_DIM))
    half = RET_HEAD_DIM // 2
    inv = ROPE_BASE ** (-np.arange(half, dtype=np.float64) / half)
    ang = (pos0 + np.arange(length, dtype=np.float64))[:, None] * inv[None, :]
    cc = np.concatenate([np.cos(ang), np.cos(ang)], axis=1)
    ss = np.concatenate([-np.sin(ang), np.sin(ang)], axis=1)
    rows = lambda a: np.pad(a, ((0, padded - length), (0, 0)))
    f = lambda a: jnp.asarray(np.ascontiguousarray(a), dtype=F32)
    return cpad, gamma_c, f(pad2(mask)), f(lane(zeta)), f(lane(xi)), f(rows(cc)), f(rows(ss))


def _ret_kernel(q_ref, k_ref, v_ref, g_ref, cc_ref, ss_ref, mask_ref, zeta_ref, xi_ref, gain_ref,
                s0_ref, o_ref, s_ref, *, bb, gamma_c):
    @pl.when(pl.program_id(1) == 0)
    def _():
        s_ref[...] = s0_ref[...]

    cc = cc_ref[...]
    ss = ss_ref[...]
    half = RET_HEAD_DIM // 2

    def rope(x):
        return x * cc + pltpu.roll(x, half, axis=1) * ss

    heads = range(RET_HEADS)
    cols = [slice(h * RET_HEAD_DIM, (h + 1) * RET_HEAD_DIM) for h in heads]

    def per_batch(b, carry):
        qb = [(rope(q_ref[b, :, c]) * (RET_HEAD_DIM ** -0.5)).astype(BF16) for c in cols]
        kr = [rope(k_ref[b, :, c]) for c in cols]
        kb = [x.astype(BF16) for x in kr]
        kz = [(kr[h] * zeta_ref[h]).astype(BF16) for h in heads]
        vb = [v_ref[b, :, c] for c in cols]
        s_prev = [s_ref[b, h] for h in heads]
        scores = [(_dot_nt(qb[h], kb[h]) * mask_ref[h]).astype(BF16) for h in heads]
        cross = [_dot(qb[h], s_prev[h].astype(BF16)) * xi_ref[h] for h in heads]
        kv = [_dot_tn(kz[h], vb[h]) for h in heads]
        inner = [_dot(scores[h], vb[h]) for h in heads]
        outs = []
        for h in heads:
            s_ref[b, h] = s_prev[h] * gamma_c[h] + kv[h]
            o = inner[h] + cross[h]
            mu = jnp.mean(o, axis=-1, keepdims=True)
            oc = o - mu
            var = jnp.mean(oc * oc, axis=-1, keepdims=True)
            on = oc * lax.rsqrt(var + EPS) * gain_ref[:, cols[h]]
            gate = g_ref[b, :, cols[h]]
            outs.append((gate * _sigmoid(gate) * on).astype(BF16))
        o_ref[b] = jnp.concatenate(outs, axis=1)
        return carry

    lax.fori_loop(0, bb, per_batch, 0, unroll=2)


def _retention(q, k, v, g, s0, gn_gain, *, pos0):
    batch, length, _ = q.shape
    padded = length if length % BF16_ROWS == 0 else pl.cdiv(length, BF16_ROWS) * BF16_ROWS
    chunk, gamma_c, mask, zeta, xi, cc, ss = _ret_tables(length, padded, pos0)
    if padded != length:
        pad = lambda a: jnp.pad(a, ((0, 0), (0, padded - length), (0, 0)))
        q, k, v, g = pad(q), pad(k), pad(v), pad(g)
    bb = 8
    tok = pl.BlockSpec((bb, chunk, RET_WIDTH), lambda i, c: (i, c, 0))
    tab = pl.BlockSpec((chunk, RET_HEAD_DIM), lambda i, c: (c, 0))
    state = pl.BlockSpec((bb, RET_HEADS, RET_HEAD_DIM, RET_HEAD_DIM), lambda i, c: (i, 0, 0, 0))
    out, s_new = pl.pallas_call(
        functools.partial(_ret_kernel, bb=bb, gamma_c=gamma_c),
        grid=(batch // bb, padded // chunk),
        in_specs=[tok, tok, tok, tok, tab, tab,
                  _full((RET_HEADS, chunk, chunk)), _full((RET_HEADS, chunk, RET_HEAD_DIM)),
                  _full((RET_HEADS, chunk, RET_HEAD_DIM)), _full((1, RET_WIDTH)), state],
        out_specs=[tok, state],
        out_shape=(jax.ShapeDtypeStruct((batch, padded, RET_WIDTH), BF16),
                   jax.ShapeDtypeStruct(s0.shape, F32)),
        compiler_params=_params("parallel", "arbitrary"),
        name="retention",
    )(q, k, v, g, cc, ss, mask, zeta, xi, gn_gain, s0)
    return out[:, :length], s_new


def _outq_kernel(x_ref, ssm_ref, ret_ref, wout_ref, g_ref, wq_ref, x1_ref, q_ref):
    x1 = (x_ref[...] + _dot(ssm_ref[...], wout_ref[:SSM_WIDTH, :])
          + _dot(ret_ref[...], wout_ref[SSM_WIDTH:, :]))
    x1_ref[...] = x1
    q_ref[...] = _dot(_rmsnorm(x1, g_ref[...]).astype(BF16), wq_ref[...]).astype(BF16)


def _out_and_query(x, ssm, ret, w_out, g_xattn, w_mq):
    t = x.shape[0]
    tm = min(t, 512)
    row = lambda w: pl.BlockSpec((tm, w), lambda i: (i, 0))
    return pl.pallas_call(
        _outq_kernel,
        grid=(t // tm,),
        in_specs=[row(D_MODEL), row(SSM_WIDTH), row(RET_WIDTH), _full((D_MODEL, D_MODEL)),
                  _full((1, D_MODEL)), _full((D_MODEL, D_MODEL))],
        out_specs=[row(D_MODEL), row(D_MODEL)],
        out_shape=(jax.ShapeDtypeStruct((t, D_MODEL), F32), jax.ShapeDtypeStruct((t, D_MODEL), BF16)),
        compiler_params=_params("parallel"),
        name="out_proj_query",
    )(x, ssm, ret, w_out, g_xattn, w_mq)


def _cache_row_order(x):
    halves = MEM_HEAD_DIM // 128
    tiles = [x[:, (h * halves + d) * 128:(h * halves + d + 1) * 128]
             for d in range(halves) for h in range(MEM_HEADS)]
    return jnp.swapaxes(jnp.stack(tiles, axis=0), 0, 1).reshape(x.shape[0] * CACHE_SUB, 128)


def _memkv_kernel(m_ref, g_ref, wk_ref, wv_ref, k_ref, v_ref, kb_ref, vb_ref):
    m = _rmsnorm(m_ref[0], g_ref[...]).astype(BF16)
    k = _dot(m, wk_ref[...])
    v = _dot(m, wv_ref[...])
    k_ref[0] = _cache_row_order(k)
    v_ref[0] = _cache_row_order(v)
    kb_ref[0] = k.astype(BF16)
    vb_ref[0] = v.astype(BF16)


def _memory_kv(mem, g_mem, w_mk, w_mv):
    batch = mem.shape[0]
    tok = pl.BlockSpec((1, MEM_LEN, D_MODEL), lambda i: (i, 0, 0))
    cache = pl.BlockSpec((1, CACHE_ROWS, 128), lambda i: (i, 0, 0))
    f = jax.ShapeDtypeStruct((batch, CACHE_ROWS, 128), F32)
    h = jax.ShapeDtypeStruct((batch, MEM_LEN, D_MODEL), BF16)
    return pl.pallas_call(
        _memkv_kernel,
        grid=(batch,),
        in_specs=[tok, _full((1, D_MODEL)), _full((D_MODEL, D_MODEL)), _full((D_MODEL, D_MODEL))],
        out_specs=[cache, cache, tok, tok],
        out_shape=(f, f, h, h),
        compiler_params=_params("parallel"),
        name="memory_kv",
    )(mem, g_mem, w_mk, w_mv)


def _outattn_kernel(x_ref, ssm_ref, ret_ref, wout_ref, g_ref, wq_ref, k_ref, v_ref, x1_ref, o_ref):
    x1 = (x_ref[...] + _dot(ssm_ref[...], wout_ref[:SSM_WIDTH, :])
          + _dot(ret_ref[...], wout_ref[SSM_WIDTH:, :]))
    x1_ref[...] = x1
    q = _dot(_rmsnorm(x1, g_ref[...]).astype(BF16), wq_ref[...]).astype(BF16)
    cols = [slice(h * MEM_HEAD_DIM, (h + 1) * MEM_HEAD_DIM) for h in range(MEM_HEADS)]
    scores = [_dot_nt(q[:, c], k_ref[0, :, c]) * (MEM_HEAD_DIM ** -0.5) for c in cols]
    probs = []
    for s in scores:
        e = jnp.exp(s - jnp.max(s, axis=-1, keepdims=True))
        probs.append((e / jnp.sum(e, axis=-1, keepdims=True)).astype(BF16))
    o_ref[...] = jnp.concatenate([_dot(p, v_ref[0, :, c]).astype(BF16) for p, c in zip(probs, cols)], axis=1)


def _out_and_attention(x, ssm, ret, w_out, g_xattn, w_mq, mk, mv, *, length):
    t = x.shape[0]
    tm = min(length, 1024)
    per_seq = length // tm
    row = lambda w: pl.BlockSpec((tm, w), lambda i: (i, 0))
    mem = pl.BlockSpec((1, MEM_LEN, D_MODEL), lambda i: (i // per_seq, 0, 0))
    return pl.pallas_call(
        _outattn_kernel,
        grid=(t // tm,),
        in_specs=[row(D_MODEL), row(SSM_WIDTH), row(RET_WIDTH), _full((D_MODEL, D_MODEL)),
                  _full((1, D_MODEL)), _full((D_MODEL, D_MODEL)), mem, mem],
        out_specs=[row(D_MODEL), row(D_MODEL)],
        out_shape=(jax.ShapeDtypeStruct((t, D_MODEL), F32), jax.ShapeDtypeStruct((t, D_MODEL), BF16)),
        compiler_params=_params("parallel"),
        name="out_proj_attention",
    )(x, ssm, ret, w_out, g_xattn, w_mq, mk, mv)


def _cached_attn_kernel(q_ref, k_ref, v_ref, o_ref, *, bb, length, between=None):
    rows = length * CACHE_SUB
    hit = (lax.broadcasted_iota(jnp.int32, (CACHE_SUB, CACHE_ROWS), 0)
           == lax.broadcasted_iota(jnp.int32, (CACHE_SUB, CACHE_ROWS), 1) % CACHE_SUB)
    first_half = lax.broadcasted_iota(jnp.int32, (length, 128), 1) % CACHE_SUB < MEM_HEADS
    tiles = CACHE_ROWS // 128

    def class_reduce(x, op):
        for shift in (8, 16, 32, 64):
            x = op(x, pltpu.roll(x, shift, axis=1))
        return x

    def softmax_rows(r):
        z = jnp.concatenate(
            [jnp.sum(jnp.where(hit, r[CACHE_SUB * t:CACHE_SUB * (t + 1)], 0.0), axis=0, keepdims=True)
             for t in range(length)], axis=0)
        parts = []
        for i in range(tiles):
            zi = z[:, 128 * i:128 * (i + 1)]
            parts.append(zi + pltpu.roll(zi, 128 - MEM_HEADS, axis=1))
        mx = parts[0]
        for pi in parts[1:]:
            mx = jnp.maximum(mx, pi)
        mx = class_reduce(mx, jnp.maximum)
        es = [jnp.exp(pi - mx) for pi in parts]
        tot = es[0]
        for ei in es[1:]:
            tot = tot + ei
        tot = class_reduce(tot, jnp.add)
        ps = []
        for ei in es:
            pi = ei / tot
            ps.append(jnp.where(first_half, pi, pltpu.roll(pi, MEM_HEADS, axis=1)))
        p = jnp.concatenate(ps, axis=1)
        return jnp.concatenate(
            [jnp.where(hit, jnp.broadcast_to(p[t:t + 1], (CACHE_SUB, CACHE_ROWS)), 0.0)
             for t in range(length)], axis=0).astype(BF16)

    scores = [_dot_nt(q_ref[b], k_ref[b].astype(BF16)) * (MEM_HEAD_DIM ** -0.5) for b in range(bb)]
    if between is not None:
        between()
    probs = [softmax_rows(r) for r in scores]
    for b in range(bb):
        o_ref[b] = _dot(probs[b], v_ref[b].astype(BF16)).astype(BF16)


def _to_cache_rows(a, lead):
    halves = MEM_HEAD_DIM // 128
    batch = a.size // (lead * D_MODEL)
    a = a.reshape(batch, lead, MEM_HEADS, halves, 128).transpose(0, 1, 3, 2, 4)
    return a.reshape(batch, lead * CACHE_SUB, 128)


def _from_cache_rows(a, lead):
    halves = MEM_HEAD_DIM // 128
    batch = a.shape[0]
    a = a.reshape(batch, lead, halves, MEM_HEADS, 128).transpose(0, 1, 3, 2, 4)
    return a.reshape(batch, lead, D_MODEL)


MLP_CHUNK = 1024


def _post_kernel(x1_ref, o_ref, wo_ref, gm_ref, wup_ref, wdn_ref, gf_ref, *rest, side_bb, side_len):
    if side_bb:
        qs_ref, ck_ref, cv_ref, y_ref, os_ref = rest
    else:
        (y_ref,) = rest
    x2 = x1_ref[...] + _dot(o_ref[...], wo_ref[...])
    result = []

    def mlp():
        h = _rmsnorm(x2, gm_ref[...]).astype(BF16)
        chunks = [slice(c, c + MLP_CHUNK) for c in range(0, D_FF, MLP_CHUNK)]
        acc = x2
        up = _dot(h, wup_ref[:, chunks[0]])
        for c, cols in enumerate(chunks):
            nxt = _dot(h, wup_ref[:, chunks[c + 1]]) if c + 1 < len(chunks) else None
            a = jnp.maximum(up, 0.0)
            acc = acc + _dot((a * a).astype(BF16), wdn_ref[cols, :])
            up = nxt
        result.append(acc)

    if side_bb:
        _cached_attn_kernel(qs_ref, ck_ref, cv_ref, os_ref, bb=side_bb, length=side_len, between=mlp)
    else:
        mlp()
    y_ref[...] = _rmsnorm(result[0], gf_ref[...])


def _attn_out_mlp(x1, o, w_mo, g_mlp, w_up, w_down, g_final, side=None):
    t = x1.shape[0]
    tm = min(t, 512)
    steps = t // tm
    row = pl.BlockSpec((tm, D_MODEL), lambda i: (i, 0))
    in_specs = [row, row, _full((D_MODEL, D_MODEL)), _full((1, D_MODEL)),
                _full((D_MODEL, D_FF)), _full((D_FF, D_MODEL)), _full((1, D_MODEL))]
    out_specs = [row]
    out_shape = [jax.ShapeDtypeStruct((t, D_MODEL), F32)]
    args = [x1, o, w_mo, g_mlp, w_up, w_down, g_final]
    side_bb = side_len = 0
    if side is not None:
        q_rows = side[0]
        side_bb, rem = divmod(q_rows.shape[0], steps)
        if rem or not side_bb:
            raise ValueError("side attention sequences must spread evenly over the grid steps")
        side_len = q_rows.shape[1] // CACHE_SUB
        blk = lambda r: pl.BlockSpec((side_bb, r, 128), lambda i: (i, 0, 0))
        in_specs += [blk(q_rows.shape[1]), blk(CACHE_ROWS), blk(CACHE_ROWS)]
        out_specs.append(blk(q_rows.shape[1]))
        out_shape.append(jax.ShapeDtypeStruct(q_rows.shape, BF16))
        args += list(side)
    outs = pl.pallas_call(
        functools.partial(_post_kernel, side_bb=side_bb, side_len=side_len),
        grid=(steps,),
        in_specs=in_specs,
        out_specs=out_specs,
        out_shape=out_shape,
        compiler_params=_params("parallel"),
        name="attn_out_mlp",
    )(*args)
    return outs if side is not None else outs[0]


def _mixers(x, s5_re, s5_im, ret_s, pos0, p):
    batch, length, _ = x.shape
    tokens = batch * length
    xf = x.reshape(tokens, D_MODEL)
    u, q, k, v, g = _project(xf, p["g_mix"], p["w_in"])
    ssm, s5_re_new, s5_im_new = _s5_mixer_blocked(
        u.reshape(batch, length, SSM_WIDTH), s5_re.reshape(batch, SSM_LANES), s5_im.reshape(batch, SSM_LANES),
        p["s5_maps"], p["d_skip"], p["w_glu"], batch=batch, length=length)
    tok3 = lambda a: a.reshape(batch, length, RET_WIDTH)
    ret, ret_new = _retention(tok3(q), tok3(k), tok3(v), tok3(g), ret_s, p["ret_gn"], pos0=pos0)
    states = (s5_re_new.reshape(1, batch, SSM_GROUPS, SSM_STATE),
              s5_im_new.reshape(1, batch, SSM_GROUPS, SSM_STATE), ret_new[None])
    return xf, ssm.reshape(tokens, SSM_WIDTH), ret.reshape(tokens, RET_WIDTH), states


def kernel(x_prompt, x_sample, mem_prompt, state_s5_re, state_s5_im, state_ret, cache_mem_k, cache_mem_v, g_mix, w_in, lam_re, lam_im, log_dt, b_re, b_im, c_re, c_im, d_skip, w_glu, ret_gn, w_out, g_xattn, g_mem, w_mq, w_mk, w_mv, w_mo, g_mlp, w_up, w_down, g_final):
    if g_mix.shape[0] != 1:
        raise ValueError("this kernel implements the single-layer configuration")
    bp, lp, _ = x_prompt.shape
    bs, ls, _ = x_sample.shape
    vec = lambda a: a.reshape(1, -1).astype(F32)
    p = dict(
        g_mix=vec(g_mix[0]), w_in=w_in[0].astype(BF16),
        s5_maps=_s5_block_maps(lam_re[0], lam_im[0], log_dt[0], b_re[0], b_im[0], c_re[0], c_im[0]),
        d_skip=vec(d_skip[0]), w_glu=w_glu[0].astype(BF16), ret_gn=vec(ret_gn[0]))
    w_out_b, w_mq_b, g_x = w_out[0].astype(BF16), w_mq[0].astype(BF16), vec(g_xattn[0])
    mlp = (w_mo[0].astype(BF16), vec(g_mlp[0]), w_up[0].astype(BF16), w_down[0].astype(BF16), vec(g_final))

    xs, ssm_s, ret_s, states_s = _mixers(x_sample, state_s5_re[0], state_s5_im[0], state_ret[0],
                                         float(PAST_LEN), p)
    x1_s, q_s = _out_and_query(xs, ssm_s, ret_s, w_out_b, g_x, w_mq_b)
    if (ls * CACHE_SUB) % BF16_ROWS:
        raise ValueError("cached attention needs whole bf16 tiles of query rows")
    side = (_to_cache_rows(q_s, ls), _to_cache_rows(cache_mem_k, MEM_LEN), _to_cache_rows(cache_mem_v, MEM_LEN))

    mk_rows, mv_rows, mk_p, mv_p = _memory_kv(mem_prompt, vec(g_mem[0]),
                                              w_mk[0].astype(BF16), w_mv[0].astype(BF16))
    zs = jnp.zeros((bp, SSM_GROUPS, SSM_STATE), F32)
    zr = jnp.zeros((bp, RET_HEADS, RET_HEAD_DIM, RET_HEAD_DIM), F32)
    xp, ssm_p, ret_p, states_p = _mixers(x_prompt, zs, zs, zr, 0.0, p)
    x1_p, o_p = _out_and_attention(xp, ssm_p, ret_p, w_out_b, g_x, w_mq_b, mk_p, mv_p, length=lp)
    y_p, o_s = _attn_out_mlp(x1_p, o_p, *mlp, side=side)
    y_s = _attn_out_mlp(x1_s, _from_cache_rows(o_s, ls).reshape(bs * ls, D_MODEL), *mlp)

    kv5 = lambda a: _from_cache_rows(a, MEM_LEN).reshape(1, bp, MEM_LEN, MEM_HEADS, MEM_HEAD_DIM)
    return (y_p.reshape(bp, lp, D_MODEL), y_s.reshape(bs, ls, D_MODEL), *states_p, kv5(mk_rows), kv5(mv_rows),
            *states_s)
```

```python
import functools
import math

import numpy as np
import jax
import jax.numpy as jnp
from jax import lax
from jax.experimental import pallas as pl
from jax.experimental.pallas import tpu as pltpu

F32 = jnp.float32
BF16 = jnp.bfloat16

D_MODEL = 1024
SSM_WIDTH = 512
SSM_GROUP = 16
SSM_GROUPS = 32
SSM_STATE = 64
SSM_LANES = SSM_GROUPS * SSM_STATE
RET_WIDTH = 512
RET_HEADS = 4
RET_HEAD_DIM = 128
RET_CHUNK = 128
ROPE_BASE = 10000.0
MEM_LEN = 256
MEM_HEADS = 4
MEM_HEAD_DIM = 256
D_FF = 4096
PROJ_WIDTH = SSM_WIDTH + 4 * RET_WIDTH
EPS = 1e-6
PAST_LEN = 16384

CACHE_ROWS = MEM_LEN * MEM_HEADS * MEM_HEAD_DIM // 128
CACHE_SUB = MEM_HEADS * MEM_HEAD_DIM // 128

BF16_ROWS = 16
VMEM_LIMIT = 56 * 1024 * 1024


def _dot(a, b):
    return jnp.dot(a, b, preferred_element_type=F32)


def _dot_nt(a, b):
    return lax.dot_general(a, b, (((1,), (1,)), ((), ())), preferred_element_type=F32)


def _dot_tn(a, b):
    return lax.dot_general(a, b, (((0,), (0,)), ((), ())), preferred_element_type=F32)


def _rmsnorm(x, g):
    return x * lax.rsqrt(jnp.mean(x * x, axis=-1, keepdims=True) + EPS) * g


def _sigmoid(x):
    return 1.0 / (1.0 + jnp.exp(-x))


def _gelu_tanh(x):
    c = math.sqrt(2.0 / math.pi)
    return x * (0.5 * (1.0 + jnp.tanh(c * (x + 0.044715 * (x * x * x)))))


def _params(*sem):
    return pltpu.CompilerParams(dimension_semantics=sem, vmem_limit_bytes=VMEM_LIMIT)


def _full(shape):
    return pl.BlockSpec(shape, lambda *_: (0,) * len(shape))


SSM_BLOCK = 4
SSM_TILE_GROUPS = 4
SSM_TILES = SSM_GROUPS // SSM_TILE_GROUPS


def _s5_prep_kernel(lr_ref, li_ref, ldt_ref, br_ref, bi_ref, cr_ref, ci_ref,
                    ar_ref, ai_ref, winr_ref, wini_ref, woutr_ref, wouti_ref, kdir_ref):
    lr = lr_ref[...]
    li = li_ref[...]
    dt = jnp.exp(ldt_ref[...])
    mag = jnp.exp(lr * dt)
    ab_re = mag * jnp.cos(li * dt)
    ab_im = mag * jnp.sin(li * dt)
    den = lr * lr + li * li
    f_re = ((ab_re - 1.0) * lr + ab_im * li) / den
    f_im = (ab_im * lr - (ab_re - 1.0) * li) / den
    bb_re = f_re * br_ref[...] - f_im * bi_ref[...]
    bb_im = f_re * bi_ref[...] + f_im * br_ref[...]
    cr = cr_ref[...]
    ci = ci_ref[...]
    pows = [(jnp.ones_like(ab_re), jnp.zeros_like(ab_re))]
    for _ in range(SSM_BLOCK):
        pr, pi = pows[-1]
        pows.append((pr * ab_re - pi * ab_im, pr * ab_im + pi * ab_re))
    ar_ref[...], ai_ref[...] = pows[SSM_BLOCK]

    R, T, TG = SSM_BLOCK, SSM_TILES, SSM_TILE_GROUPS
    rows = SSM_GROUPS * SSM_GROUP
    side = TG * SSM_GROUP
    exact = dict(precision=lax.Precision.HIGHEST, preferred_element_type=F32)
    iota = lambda shape, d: lax.broadcasted_iota(jnp.int32, shape, d)
    rep4 = lambda a, axis: jnp.concatenate([a] * TG, axis=axis)
    tiles = lambda a: a.reshape(T, side, a.shape[-1])

    group_of = lambda i: (i & (side - 1)) >> 4
    own_in = group_of(iota((rows, 256), 0)) == iota((rows, 256), 1) >> 6
    for s in range(R):
        pr, pi = pows[R - 1 - s]
        for ref, w in ((winr_ref, pr * bb_re - pi * bb_im), (wini_ref, pr * bb_im + pi * bb_re)):
            ref[:, side * s:side * (s + 1), :] = tiles(jnp.where(own_in, rep4(w, 1), 0.0)).astype(BF16)

    eye = (iota((rows, rows), 0) == iota((rows, rows), 1)).astype(F32)
    own_out = iota((256, rows), 0) >> 6 == group_of(iota((256, rows), 1))
    for ref, sign in ((woutr_ref, 1.0), (wouti_ref, -1.0)):
        per_lag = []
        for j in range(R):
            pr, pi = pows[j + 1]
            w = cr * pr - ci * pi if sign > 0 else -(cr * pi + ci * pr)
            wt = lax.dot_general(w, eye, (((0,), (0,)), ((), ())), **exact)
            per_lag.append(jnp.where(own_out, rep4(wt, 0), 0.0))
        for n in range(T):
            ref[n] = jnp.concatenate([m[:, side * n:side * (n + 1)] for m in per_lag], axis=1).astype(BF16)

    grouped = lambda a: a.reshape(SSM_GROUPS, SSM_GROUP, SSM_STATE)
    contract = lambda a, b: lax.dot_general(a, b, (((2,), (2,)), ((0,), (0,))), **exact)
    own_dir = group_of(iota((rows, side), 0)) == iota((rows, side), 1) >> 4
    by_lag = []
    for d in range(R):
        pr, pi = pows[d]
        er = grouped(pr * bb_re - pi * bb_im)
        ei = grouped(pr * bb_im + pi * bb_re)
        kt = (contract(er, grouped(cr)) - contract(ei, grouped(ci))).reshape(rows, SSM_GROUP)
        by_lag.append(tiles(jnp.where(own_dir, rep4(kt, 1), 0.0)))
    nothing = jnp.zeros_like(by_lag[0])
    for s in range(R):
        kdir_ref[:, side * s:side * (s + 1), :] = jnp.concatenate(
            [by_lag[j - s] if j >= s else nothing for j in range(R)], axis=2).astype(BF16)


def _s5_block_maps(lam_re, lam_im, log_dt, b_re, b_im, c_re, c_im):
    rows = SSM_GROUPS * SSM_GROUP
    rep = lambda a: jnp.repeat(a, SSM_GROUP, axis=0)
    ldt = jnp.broadcast_to(log_dt[:, None], (SSM_GROUPS, SSM_STATE))
    bt = lambda b: b.transpose(0, 2, 1).reshape(rows, SSM_STATE)
    ct = lambda c: c.reshape(rows, SSM_STATE)
    one = jax.ShapeDtypeStruct((rows, SSM_STATE), F32)
    tile = jax.ShapeDtypeStruct((SSM_TILES, 256, 256), BF16)
    ar, ai, winr, wini, woutr, wouti, direct = pl.pallas_call(
        _s5_prep_kernel,
        out_shape=(one, one, tile, tile, tile, tile, tile),
        compiler_params=pltpu.CompilerParams(vmem_limit_bytes=VMEM_LIMIT),
        name="s5_block_maps",
    )(rep(lam_re), rep(lam_im), rep(ldt), bt(b_re), bt(b_im), ct(c_re), ct(c_im))
    a_re = ar[::SSM_GROUP].reshape(1, SSM_LANES)
    a_im = ai[::SSM_GROUP].reshape(1, SSM_LANES)
    return a_re, a_im, winr, wini, woutr, wouti, direct


def _proj_kernel(x_ref, g_ref, w_ref, u_ref, q_ref, k_ref, v_ref, gate_ref):
    h = _rmsnorm(x_ref[...], g_ref[...]).astype(BF16)
    proj = _dot(h, w_ref[...])
    u_ref[...] = proj[:, :SSM_WIDTH]
    q_ref[...] = proj[:, SSM_WIDTH:SSM_WIDTH + RET_WIDTH]
    k_ref[...] = proj[:, SSM_WIDTH + RET_WIDTH:SSM_WIDTH + 2 * RET_WIDTH]
    v_ref[...] = proj[:, SSM_WIDTH + 2 * RET_WIDTH:SSM_WIDTH + 3 * RET_WIDTH].astype(BF16)
    gate_ref[...] = proj[:, SSM_WIDTH + 3 * RET_WIDTH:]


def _project(x, g_mix, w_in):
    t = x.shape[0]
    tm = min(t, 1024)
    row = lambda w: pl.BlockSpec((tm, w), lambda i: (i, 0))
    f = jax.ShapeDtypeStruct((t, 512), F32)
    return pl.pallas_call(
        _proj_kernel,
        grid=(t // tm,),
        in_specs=[row(D_MODEL), _full((1, D_MODEL)), _full((D_MODEL, PROJ_WIDTH))],
        out_specs=[row(512)] * 5,
        out_shape=(f, f, f, jax.ShapeDtypeStruct((t, 512), BF16), f),
        compiler_params=_params("parallel"),
        name="in_proj",
    )(x, g_mix, w_in)


def _s5_kernel(u_ref, h0r_ref, h0i_ref, abr_ref, abi_ref, bre_ref, bim_ref, cre_ref, ncim_ref,
               d_ref, wglu_ref, out_ref, hr_ref, hi_ref, bur_scr, bui_scr, xr_scr, xi_scr,
               *, batch, steps, unroll, width):
    half = SSM_LANES // 2
    kin = SSM_WIDTH // 2
    kout = SSM_WIDTH // 2

    @pl.when(pl.program_id(0) == 0)
    def _():
        hr_ref[...] = h0r_ref[...]
        hi_ref[...] = h0i_ref[...]

    rows = steps * batch
    u = jnp.swapaxes(u_ref[...], 0, 1).reshape(rows, SSM_WIDTH)
    ub = u.astype(BF16)
    for p in range(2):
        lanes = slice(half * p, half * (p + 1))
        uk = ub[:, kin * p:kin * (p + 1)]
        bur_scr[:, lanes] = _dot(uk, bre_ref[kin * p:kin * (p + 1), lanes])
        bui_scr[:, lanes] = _dot(uk, bim_ref[kin * p:kin * (p + 1), lanes])
    ys = []
    for p in range(2):
        lanes = slice(half * p, half * (p + 1))
        for j in range(half // width):
            glob = slice(half * p + j * width, half * p + (j + 1) * width)
            ar = jnp.broadcast_to(abr_ref[:, glob], (batch, width))
            ai = jnp.broadcast_to(abi_ref[:, glob], (batch, width))
            xr = hr_ref[:, glob]
            xi = hi_ref[:, glob]
            for i in range(steps // unroll):
                base = i * unroll * batch
                out_r, out_i = [], []
                for s in range(unroll):
                    at = slice(base + s * batch, base + (s + 1) * batch)
                    xr, xi = ar * xr - ai * xi + bur_scr[at, glob], ar * xi + ai * xr + bui_scr[at, glob]
                    out_r.append(xr)
                    out_i.append(xi)
                blk = slice(base, base + unroll * batch)
                xr_scr[blk, glob] = jnp.concatenate(out_r, axis=0).astype(BF16)
                xi_scr[blk, glob] = jnp.concatenate(out_i, axis=0).astype(BF16)
            hr_ref[:, glob] = xr
            hi_ref[:, glob] = xi
        cols = slice(kout * p, kout * (p + 1))
        ys.append(_dot(xr_scr[:, lanes], cre_ref[lanes, cols]) + _dot(xi_scr[:, lanes], ncim_ref[lanes, cols]))
    y = jnp.concatenate(ys, axis=1) + d_ref[...] * u
    z = _gelu_tanh(y)
    out = z * _sigmoid(_dot(z.astype(BF16), wglu_ref[...]))
    out_ref[...] = jnp.swapaxes(out.reshape(steps, batch, SSM_WIDTH), 0, 1).astype(BF16)


def _s5_mixer(u, h0_re, h0_im, ab_re, ab_im, b_re, b_im, c_re, nc_im, d_skip, w_glu, *, batch, length):
    steps = min(length, 128)
    unroll = max(1, min(steps, BF16_ROWS // batch))
    if (unroll * batch) % BF16_ROWS:
        raise ValueError("S5 scan stores need whole bf16 tiles")
    width = SSM_LANES // 2 if batch <= 8 else 128
    rows = steps * batch
    kern = functools.partial(_s5_kernel, batch=batch, steps=steps, unroll=unroll, width=width)
    st = jax.ShapeDtypeStruct((batch, SSM_LANES), F32)
    return pl.pallas_call(
        kern,
        grid=(length // steps,),
        in_specs=[pl.BlockSpec((batch, steps, SSM_WIDTH), lambda i: (0, i, 0)),
                  _full((batch, SSM_LANES)), _full((batch, SSM_LANES)),
                  _full((1, SSM_LANES)), _full((1, SSM_LANES)),
                  _full((SSM_WIDTH, SSM_LANES)), _full((SSM_WIDTH, SSM_LANES)),
                  _full((SSM_LANES, SSM_WIDTH)), _full((SSM_LANES, SSM_WIDTH)),
                  _full((1, SSM_WIDTH)), _full((SSM_WIDTH, SSM_WIDTH))],
        out_specs=[pl.BlockSpec((batch, steps, SSM_WIDTH), lambda i: (0, i, 0)),
                   _full((batch, SSM_LANES)), _full((batch, SSM_LANES))],
        out_shape=(jax.ShapeDtypeStruct((batch, length, SSM_WIDTH), BF16), st, st),
        scratch_shapes=[pltpu.VMEM((rows, SSM_LANES), F32), pltpu.VMEM((rows, SSM_LANES), F32),
                        pltpu.VMEM((rows, SSM_LANES), BF16), pltpu.VMEM((rows, SSM_LANES), BF16)],
        compiler_params=_params("arbitrary"),
        name="s5_mixer",
    )(u, h0_re, h0_im, ab_re, ab_im, b_re, b_im, c_re, nc_im, d_skip, w_glu)


def _s5_block_kernel(u_ref, h0r_ref, h0i_ref, ar_ref, ai_ref, winr_ref, wini_ref, woutr_ref, wouti_ref,
                     kdir_ref, d_ref, wglu_ref, out_ref, hr_ref, hi_ref, vr_scr, vi_scr, xr_scr, xi_scr,
                     *, batch, steps, pair, width):
    R, T = SSM_BLOCK, SSM_TILES
    chan = SSM_WIDTH // T
    blocks = steps // R
    rows, brow = steps * batch, blocks * batch

    @pl.when(pl.program_id(0) == 0)
    def _():
        hr_ref[...] = h0r_ref[...]
        hi_ref[...] = h0i_ref[...]

    u = jnp.swapaxes(u_ref[...], 0, 1).reshape(rows, SSM_WIDTH)
    u4 = u.reshape(blocks, R, batch, SSM_WIDTH)
    lag = [u4[:, s].reshape(brow, SSM_WIDTH) for s in range(R)]
    direct = []
    for n in range(T):
        un = jnp.concatenate([lag[s][:, chan * n:chan * (n + 1)] for s in range(R)], axis=1).astype(BF16)
        lanes = slice(256 * n, 256 * (n + 1))
        vr_scr[:, lanes] = _dot(un, winr_ref[n])
        vi_scr[:, lanes] = _dot(un, wini_ref[n])
        direct.append(_dot(un, kdir_ref[n]))

    for j in range(SSM_LANES // width):
        glob = slice(j * width, (j + 1) * width)
        ar = jnp.broadcast_to(ar_ref[:, glob], (batch, width))
        ai = jnp.broadcast_to(ai_ref[:, glob], (batch, width))
        xr = hr_ref[:, glob]
        xi = hi_ref[:, glob]
        for i in range(blocks // pair):
            prev_r, prev_i = [], []
            for s in range(pair):
                at = slice((i * pair + s) * batch, (i * pair + s + 1) * batch)
                prev_r.append(xr)
                prev_i.append(xi)
                xr, xi = ar * xr - ai * xi + vr_scr[at, glob], ar * xi + ai * xr + vi_scr[at, glob]
            blk = slice(i * pair * batch, (i + 1) * pair * batch)
            xr_scr[blk, glob] = jnp.concatenate(prev_r, axis=0).astype(BF16)
            xi_scr[blk, glob] = jnp.concatenate(prev_i, axis=0).astype(BF16)
        hr_ref[:, glob] = xr
        hi_ref[:, glob] = xi

    y4 = []
    for n in range(T):
        lanes = slice(256 * n, 256 * (n + 1))
        y4.append(_dot(xr_scr[:, lanes], woutr_ref[n]) + _dot(xi_scr[:, lanes], wouti_ref[n]) + direct[n])
    per_lag = [jnp.concatenate([y4[n][:, chan * j:chan * (j + 1)] for n in range(T)], axis=1)
               .reshape(blocks, batch, SSM_WIDTH) for j in range(R)]
    y = jnp.stack(per_lag, axis=1).reshape(rows, SSM_WIDTH) + d_ref[...] * u
    z = _gelu_tanh(y)
    out = z * _sigmoid(_dot(z.astype(BF16), wglu_ref[...]))
    out_ref[...] = jnp.swapaxes(out.reshape(steps, batch, SSM_WIDTH), 0, 1).astype(BF16)


def _s5_mixer_blocked(u, h0_re, h0_im, maps, d_skip, w_glu, *, batch, length):
    a_re, a_im, winr, wini, woutr, wouti, direct = maps
    steps = min(length, 128)
    if steps % SSM_BLOCK or length % steps:
        raise ValueError("sequence length must be a multiple of the S5 block")
    blocks = steps // SSM_BLOCK
    pair = max(1, min(blocks, BF16_ROWS // batch))
    if (pair * batch) % BF16_ROWS or blocks % pair:
        raise ValueError("S5 state stores need whole bf16 tiles")
    width = SSM_LANES // 2 if batch <= 8 else 128
    brow = blocks * batch
    kern = functools.partial(_s5_block_kernel, batch=batch, steps=steps, pair=pair, width=width)
    st = jax.ShapeDtypeStruct((batch, SSM_LANES), F32)
    tile = _full((SSM_TILES, 256, 256))
    return pl.pallas_call(
        kern,
        grid=(length // steps,),
        in_specs=[pl.BlockSpec((batch, steps, SSM_WIDTH), lambda i: (0, i, 0)),
                  _full((batch, SSM_LANES)), _full((batch, SSM_LANES)),
                  _full((1, SSM_LANES)), _full((1, SSM_LANES)),
                  tile, tile, tile, tile, tile,
                  _full((1, SSM_WIDTH)), _full((SSM_WIDTH, SSM_WIDTH))],
        out_specs=[pl.BlockSpec((batch, steps, SSM_WIDTH), lambda i: (0, i, 0)),
                   _full((batch, SSM_LANES)), _full((batch, SSM_LANES))],
        out_shape=(jax.ShapeDtypeStruct((batch, length, SSM_WIDTH), BF16), st, st),
        scratch_shapes=[pltpu.VMEM((brow, SSM_LANES), F32), pltpu.VMEM((brow, SSM_LANES), F32),
                        pltpu.VMEM((brow, SSM_LANES), BF16), pltpu.VMEM((brow, SSM_LANES), BF16)],
        compiler_params=_params("arbitrary"),
        name="s5_mixer",
    )(u, h0_re, h0_im, a_re, a_im, winr, wini, woutr, wouti, direct, d_skip, w_glu)


def _ret_tables(length, padded, pos0):
    chunk = RET_CHUNK if length % RET_CHUNK == 0 else length
    cpad = chunk if padded == length else padded
    lg = np.log(1.0 - 2.0 ** (-5.0 - np.arange(RET_HEADS, dtype=np.float64)))
    idx = np.arange(chunk, dtype=np.float64)
    diff = idx[:, None] - idx[None, :]
    mask = np.where(diff[None] >= 0, np.exp(np.maximum(diff, 0.0)[None] * lg[:, None, None]), 0.0)
    zeta = np.exp((chunk - 1.0 - idx)[None, :] * lg[:, None])
    xi = np.exp((idx + 1.0)[None, :] * lg[:, None])
    gamma_c = tuple(float(v) for v in np.exp(chunk * lg))
    pad2 = lambda a: np.pad(a, ((0, 0), (0, cpad - chunk), (0, cpad - chunk)))
    lane = lambda a: np.broadcast_to(np.pad(a, ((0, 0), (0, cpad - chunk)))[:, :, None],
                                     (RET_HEADS, cpad, RET_HEAD_DIM))
    half = RET_HEAD_DIM // 2
    inv = ROPE_BASE ** (-np.arange(half, dtype=np.float64) / half)
    ang = (pos0 + np.arange(length, dtype=np.float64))[:, None] * inv[None, :]
    cc = np.concatenate([np.cos(ang), np.cos(ang)], axis=1)
    ss = np.concatenate([-np.sin(ang), np.sin(ang)], axis=1)
    rows = lambda a: np.pad(a, ((0, padded - length), (0, 0)))
    f = lambda a: jnp.asarray(np.ascontiguousarray(a), dtype=F32)
    return cpad, gamma_c, f(pad2(mask)), f(lane(zeta)), f(lane(xi)), f(rows(cc)), f(rows(ss))


def _ret_kernel(q_ref, k_ref, v_ref, g_ref, cc_ref, ss_ref, mask_ref, zeta_ref, xi_ref, gain_ref,
                s0_ref, o_ref, s_ref, *, bb, gamma_c):
    @pl.when(pl.program_id(1) == 0)
    def _():
        s_ref[...] = s0_ref[...]

    cc = cc_ref[...]
    ss = ss_ref[...]
    half = RET_HEAD_DIM // 2

    def rope(x):
        return x * cc + pltpu.roll(x, half, axis=1) * ss

    heads = range(RET_HEADS)
    cols = [slice(h * RET_HEAD_DIM, (h + 1) * RET_HEAD_DIM) for h in heads]

    def per_batch(b, carry):
        qb = [(rope(q_ref[b, :, c]) * (RET_HEAD_DIM ** -0.5)).astype(BF16) for c in cols]
        kr = [rope(k_ref[b, :, c]) for c in cols]
        kb = [x.astype(BF16) for x in kr]
        kz = [(kr[h] * zeta_ref[h]).astype(BF16) for h in heads]
        vb = [v_ref[b, :, c] for c in cols]
        s_prev = [s_ref[b, h] for h in heads]
        scores = [(_dot_nt(qb[h], kb[h]) * mask_ref[h]).astype(BF16) for h in heads]
        cross = [_dot(qb[h], s_prev[h].astype(BF16)) * xi_ref[h] for h in heads]
        kv = [_dot_tn(kz[h], vb[h]) for h in heads]
        inner = [_dot(scores[h], vb[h]) for h in heads]
        outs = []
        for h in heads:
            s_ref[b, h] = s_prev[h] * gamma_c[h] + kv[h]
            o = inner[h] + cross[h]
            mu = jnp.mean(o, axis=-1, keepdims=True)
            oc = o - mu
            var = jnp.mean(oc * oc, axis=-1, keepdims=True)
            on = oc * lax.rsqrt(var + EPS) * gain_ref[:, cols[h]]
            gate = g_ref[b, :, cols[h]]
            outs.append((gate * _sigmoid(gate) * on).astype(BF16))
        o_ref[b] = jnp.concatenate(outs, axis=1)
        return carry

    lax.fori_loop(0, bb, per_batch, 0, unroll=2)


def _retention(q, k, v, g, s0, gn_gain, *, pos0):
    batch, length, _ = q.shape
    padded = length if length % BF16_ROWS == 0 else pl.cdiv(length, BF16_ROWS) * BF16_ROWS
    chunk, gamma_c, mask, zeta, xi, cc, ss = _ret_tables(length, padded, pos0)
    if padded != length:
        pad = lambda a: jnp.pad(a, ((0, 0), (0, padded - length), (0, 0)))
        q, k, v, g = pad(q), pad(k), pad(v), pad(g)
    bb = 8
    tok = pl.BlockSpec((bb, chunk, RET_WIDTH), lambda i, c: (i, c, 0))
    tab = pl.BlockSpec((chunk, RET_HEAD_DIM), lambda i, c: (c, 0))
    state = pl.BlockSpec((bb, RET_HEADS, RET_HEAD_DIM, RET_HEAD_DIM), lambda i, c: (i, 0, 0, 0))
    out, s_new = pl.pallas_call(
        functools.partial(_ret_kernel, bb=bb, gamma_c=gamma_c),
        grid=(batch // bb, padded // chunk),
        in_specs=[tok, tok, tok, tok, tab, tab,
                  _full((RET_HEADS, chunk, chunk)), _full((RET_HEADS, chunk, RET_HEAD_DIM)),
                  _full((RET_HEADS, chunk, RET_HEAD_DIM)), _full((1, RET_WIDTH)), state],
        out_specs=[tok, state],
        out_shape=(jax.ShapeDtypeStruct((batch, padded, RET_WIDTH), BF16),
                   jax.ShapeDtypeStruct(s0.shape, F32)),
        compiler_params=_params("parallel", "arbitrary"),
        name="retention",
    )(q, k, v, g, cc, ss, mask, zeta, xi, gn_gain, s0)
    return out[:, :length], s_new


def _outq_kernel(x_ref, ssm_ref, ret_ref, wout_ref, g_ref, wq_ref, x1_ref, q_ref):
    x1 = (x_ref[...] + _dot(ssm_ref[...], wout_ref[:SSM_WIDTH, :])
          + _dot(ret_ref[...], wout_ref[SSM_WIDTH:, :]))
    x1_ref[...] = x1
    q_ref[...] = _dot(_rmsnorm(x1, g_ref[...]).astype(BF16), wq_ref[...]).astype(BF16)


def _out_and_query(x, ssm, ret, w_out, g_xattn, w_mq):
    t = x.shape[0]
    tm = min(t, 512)
    row = lambda w: pl.BlockSpec((tm, w), lambda i: (i, 0))
    return pl.pallas_call(
        _outq_kernel,
        grid=(t // tm,),
        in_specs=[row(D_MODEL), row(SSM_WIDTH), row(RET_WIDTH), _full((D_MODEL, D_MODEL)),
                  _full((1, D_MODEL)), _full((D_MODEL, D_MODEL))],
        out_specs=[row(D_MODEL), row(D_MODEL)],
        out_shape=(jax.ShapeDtypeStruct((t, D_MODEL), F32), jax.ShapeDtypeStruct((t, D_MODEL), BF16)),
        compiler_params=_params("parallel"),
        name="out_proj_query",
    )(x, ssm, ret, w_out, g_xattn, w_mq)


def _cache_row_order(x):
    halves = MEM_HEAD_DIM // 128
    tiles = [x[:, (h * halves + d) * 128:(h * halves + d + 1) * 128]
             for d in range(halves) for h in range(MEM_HEADS)]
    return jnp.swapaxes(jnp.stack(tiles, axis=0), 0, 1).reshape(x.shape[0] * CACHE_SUB, 128)


def _memkv_kernel(m_ref, g_ref, wk_ref, wv_ref, k_ref, v_ref, kb_ref, vb_ref):
    m = _rmsnorm(m_ref[0], g_ref[...]).astype(BF16)
    k = _dot(m, wk_ref[...])
    v = _dot(m, wv_ref[...])
    k_ref[0] = _cache_row_order(k)
    v_ref[0] = _cache_row_order(v)
    kb_ref[0] = k.astype(BF16)
    vb_ref[0] = v.astype(BF16)


def _memory_kv(mem, g_mem, w_mk, w_mv):
    batch = mem.shape[0]
    tok = pl.BlockSpec((1, MEM_LEN, D_MODEL), lambda i: (i, 0, 0))
    cache = pl.BlockSpec((1, CACHE_ROWS, 128), lambda i: (i, 0, 0))
    f = jax.ShapeDtypeStruct((batch, CACHE_ROWS, 128), F32)
    h = jax.ShapeDtypeStruct((batch, MEM_LEN, D_MODEL), BF16)
    return pl.pallas_call(
        _memkv_kernel,
        grid=(batch,),
        in_specs=[tok, _full((1, D_MODEL)), _full((D_MODEL, D_MODEL)), _full((D_MODEL, D_MODEL))],
        out_specs=[cache, cache, tok, tok],
        out_shape=(f, f, h, h),
        compiler_params=_params("parallel"),
        name="memory_kv",
    )(mem, g_mem, w_mk, w_mv)


def _outattn_kernel(x_ref, ssm_ref, ret_ref, wout_ref, g_ref, wq_ref, k_ref, v_ref, x1_ref, o_ref):
    x1 = (x_ref[...] + _dot(ssm_ref[...], wout_ref[:SSM_WIDTH, :])
          + _dot(ret_ref[...], wout_ref[SSM_WIDTH:, :]))
    x1_ref[...] = x1
    q = _dot(_rmsnorm(x1, g_ref[...]).astype(BF16), wq_ref[...]).astype(BF16)
    cols = [slice(h * MEM_HEAD_DIM, (h + 1) * MEM_HEAD_DIM) for h in range(MEM_HEADS)]
    scores = [_dot_nt(q[:, c], k_ref[0, :, c]) * (MEM_HEAD_DIM ** -0.5) for c in cols]
    probs = []
    for s in scores:
        e = jnp.exp(s - jnp.max(s, axis=-1, keepdims=True))
        probs.append((e / jnp.sum(e, axis=-1, keepdims=True)).astype(BF16))
    o_ref[...] = jnp.concatenate([_dot(p, v_ref[0, :, c]).astype(BF16) for p, c in zip(probs, cols)], axis=1)


def _out_and_attention(x, ssm, ret, w_out, g_xattn, w_mq, mk, mv, *, length):
    t = x.shape[0]
    tm = min(length, 1024)
    per_seq = length // tm
    row = lambda w: pl.BlockSpec((tm, w), lambda i: (i, 0))
    mem = pl.BlockSpec((1, MEM_LEN, D_MODEL), lambda i: (i // per_seq, 0, 0))
    return pl.pallas_call(
        _outattn_kernel,
        grid=(t // tm,),
        in_specs=[row(D_MODEL), row(SSM_WIDTH), row(RET_WIDTH), _full((D_MODEL, D_MODEL)),
                  _full((1, D_MODEL)), _full((D_MODEL, D_MODEL)), mem, mem],
        out_specs=[row(D_MODEL), row(D_MODEL)],
        out_shape=(jax.ShapeDtypeStruct((t, D_MODEL), F32), jax.ShapeDtypeStruct((t, D_MODEL), BF16)),
        compiler_params=_params("parallel"),
        name="out_proj_attention",
    )(x, ssm, ret, w_out, g_xattn, w_mq, mk, mv)


def _cached_attn_kernel(q_ref, k_ref, v_ref, o_ref, *, bb, length, between=None):
    rows = length * CACHE_SUB
    hit = (lax.broadcasted_iota(jnp.int32, (CACHE_SUB, CACHE_ROWS), 0)
           == lax.broadcasted_iota(jnp.int32, (CACHE_SUB, CACHE_ROWS), 1) % CACHE_SUB)
    first_half = lax.broadcasted_iota(jnp.int32, (length, 128), 1) % CACHE_SUB < MEM_HEADS
    tiles = CACHE_ROWS // 128

    def class_reduce(x, op):
        for shift in (8, 16, 32, 64):
            x = op(x, pltpu.roll(x, shift, axis=1))
        return x

    def softmax_rows(r):
        z = jnp.concatenate(
            [jnp.sum(jnp.where(hit, r[CACHE_SUB * t:CACHE_SUB * (t + 1)], 0.0), axis=0, keepdims=True)
             for t in range(length)], axis=0)
        parts = []
        for i in range(tiles):
            zi = z[:, 128 * i:128 * (i + 1)]
            parts.append(zi + pltpu.roll(zi, 128 - MEM_HEADS, axis=1))
        mx = parts[0]
        for pi in parts[1:]:
            mx = jnp.maximum(mx, pi)
        mx = class_reduce(mx, jnp.maximum)
        es = [jnp.exp(pi - mx) for pi in parts]
        tot = es[0]
        for ei in es[1:]:
            tot = tot + ei
        tot = class_reduce(tot, jnp.add)
        ps = []
        for ei in es:
            pi = ei / tot
            ps.append(jnp.where(first_half, pi, pltpu.roll(pi, MEM_HEADS, axis=1)))
        p = jnp.concatenate(ps, axis=1)
        return jnp.concatenate(
            [jnp.where(hit, jnp.broadcast_to(p[t:t + 1], (CACHE_SUB, CACHE_ROWS)), 0.0)
             for t in range(length)], axis=0).astype(BF16)

    scores = [_dot_nt(q_ref[b], k_ref[b].astype(BF16)) * (MEM_HEAD_DIM ** -0.5) for b in range(bb)]
    if between is not None:
        between()
    probs = [softmax_rows(r) for r in scores]
    for b in range(bb):
        o_ref[b] = _dot(probs[b], v_ref[b].astype(BF16)).astype(BF16)


def _to_cache_rows(a, lead):
    halves = MEM_HEAD_DIM // 128
    batch = a.size // (lead * D_MODEL)
    a = a.reshape(batch, lead, MEM_HEADS, halves, 128).transpose(0, 1, 3, 2, 4)
    return a.reshape(batch, lead * CACHE_SUB, 128)


def _from_cache_rows(a, lead):
    halves = MEM_HEAD_DIM // 128
    batch = a.shape[0]
    a = a.reshape(batch, lead, halves, MEM_HEADS, 128).transpose(0, 1, 3, 2, 4)
    return a.reshape(batch, lead, D_MODEL)


MLP_CHUNK = 1024


def _post_kernel(x1_ref, o_ref, wo_ref, gm_ref, wup_ref, wdn_ref, gf_ref, *rest, side_bb, side_len):
    if side_bb:
        qs_ref, ck_ref, cv_ref, y_ref, os_ref = rest
    else:
        (y_ref,) = rest
    x2 = x1_ref[...] + _dot(o_ref[...], wo_ref[...])
    result = []

    def mlp():
        h = _rmsnorm(x2, gm_ref[...]).astype(BF16)
        chunks = [slice(c, c + MLP_CHUNK) for c in range(0, D_FF, MLP_CHUNK)]
        acc = x2
        up = _dot(h, wup_ref[:, chunks[0]])
        for c, cols in enumerate(chunks):
            nxt = _dot(h, wup_ref[:, chunks[c + 1]]) if c + 1 < len(chunks) else None
            a = jnp.maximum(up, 0.0)
            acc = acc + _dot((a * a).astype(BF16), wdn_ref[cols, :])
            up = nxt
        result.append(acc)

    if side_bb:
        _cached_attn_kernel(qs_ref, ck_ref, cv_ref, os_ref, bb=side_bb, length=side_len, between=mlp)
    else:
        mlp()
    y_ref[...] = _rmsnorm(result[0], gf_ref[...])


def _attn_out_mlp(x1, o, w_mo, g_mlp, w_up, w_down, g_final, side=None):
    t = x1.shape[0]
    tm = min(t, 512)
    steps = t // tm
    row = pl.BlockSpec((tm, D_MODEL), lambda i: (i, 0))
    in_specs = [row, row, _full((D_MODEL, D_MODEL)), _full((1, D_MODEL)),
                _full((D_MODEL, D_FF)), _full((D_FF, D_MODEL)), _full((1, D_MODEL))]
    out_specs = [row]
    out_shape = [jax.ShapeDtypeStruct((t, D_MODEL), F32)]
    args = [x1, o, w_mo, g_mlp, w_up, w_down, g_final]
    side_bb = side_len = 0
    if side is not None:
        q_rows = side[0]
        side_bb, rem = divmod(q_rows.shape[0], steps)
        if rem or not side_bb:
            raise ValueError("side attention sequences must spread evenly over the grid steps")
        side_len = q_rows.shape[1] // CACHE_SUB
        blk = lambda r: pl.BlockSpec((side_bb, r, 128), lambda i: (i, 0, 0))
        in_specs += [blk(q_rows.shape[1]), blk(CACHE_ROWS), blk(CACHE_ROWS)]
        out_specs.append(blk(q_rows.shape[1]))
        out_shape.append(jax.ShapeDtypeStruct(q_rows.shape, BF16))
        args += list(side)
    outs = pl.pallas_call(
        functools.partial(_post_kernel, side_bb=side_bb, side_len=side_len),
        grid=(steps,),
        in_specs=in_specs,
        out_specs=out_specs,
        out_shape=out_shape,
        compiler_params=_params("parallel"),
        name="attn_out_mlp",
    )(*args)
    return outs if side is not None else outs[0]


def _mixers(x, s5_re, s5_im, ret_s, pos0, p):
    batch, length, _ = x.shape
    tokens = batch * length
    xf = x.reshape(tokens, D_MODEL)
    u, q, k, v, g = _project(xf, p["g_mix"], p["w_in"])
    ssm, s5_re_new, s5_im_new = _s5_mixer_blocked(
        u.reshape(batch, length, SSM_WIDTH), s5_re.reshape(batch, SSM_LANES), s5_im.reshape(batch, SSM_LANES),
        p["s5_maps"], p["d_skip"], p["w_glu"], batch=batch, length=length)
    tok3 = lambda a: a.reshape(batch, length, RET_WIDTH)
    ret, ret_new = _retention(tok3(q), tok3(k), tok3(v), tok3(g), ret_s, p["ret_gn"], pos0=pos0)
    states = (s5_re_new.reshape(1, batch, SSM_GROUPS, SSM_STATE),
              s5_im_new.reshape(1, batch, SSM_GROUPS, SSM_STATE), ret_new[None])
    return xf, ssm.reshape(tokens, SSM_WIDTH), ret.reshape(tokens, RET_WIDTH), states


def kernel(x_prompt, x_sample, mem_prompt, state_s5_re, state_s5_im, state_ret, cache_mem_k, cache_mem_v, g_mix, w_in, lam_re, lam_im, log_dt, b_re, b_im, c_re, c_im, d_skip, w_glu, ret_gn, w_out, g_xattn, g_mem, w_mq, w_mk, w_mv, w_mo, g_mlp, w_up, w_down, g_final):
    if g_mix.shape[0] != 1:
        raise ValueError("this kernel implements the single-layer configuration")
    bp, lp, _ = x_prompt.shape
    bs, ls, _ = x_sample.shape
    vec = lambda a: a.reshape(1, -1).astype(F32)
    p = dict(
        g_mix=vec(g_mix[0]), w_in=w_in[0].astype(BF16),
        s5_maps=_s5_block_maps(lam_re[0], lam_im[0], log_dt[0], b_re[0], b_im[0], c_re[0], c_im[0]),
        d_skip=vec(d_skip[0]), w_glu=w_glu[0].astype(BF16), ret_gn=vec(ret_gn[0]))
    w_out_b, w_mq_b, g_x = w_out[0].astype(BF16), w_mq[0].astype(BF16), vec(g_xattn[0])
    mlp = (w_mo[0].astype(BF16), vec(g_mlp[0]), w_up[0].astype(BF16), w_down[0].astype(BF16), vec(g_final))

    xs, ssm_s, ret_s, states_s = _mixers(x_sample, state_s5_re[0], state_s5_im[0], state_ret[0],
                                         float(PAST_LEN), p)
    x1_s, q_s = _out_and_query(xs, ssm_s, ret_s, w_out_b, g_x, w_mq_b)
    if (ls * CACHE_SUB) % BF16_ROWS:
        raise ValueError("cached attention needs whole bf16 tiles of query rows")
    side = (_to_cache_rows(q_s, ls), _to_cache_rows(cache_mem_k, MEM_LEN), _to_cache_rows(cache_mem_v, MEM_LEN))

    mk_rows, mv_rows, mk_p, mv_p = _memory_kv(mem_prompt, vec(g_mem[0]),
                                              w_mk[0].astype(BF16), w_mv[0].astype(BF16))
    zs = jnp.zeros((bp, SSM_GROUPS, SSM_STATE), F32)
    zr = jnp.zeros((bp, RET_HEADS, RET_HEAD_DIM, RET_HEAD_DIM), F32)
    xp, ssm_p, ret_p, states_p = _mixers(x_prompt, zs, zs, zr, 0.0, p)
    x1_p, o_p = _out_and_attention(xp, ssm_p, ret_p, w_out_b, g_x, w_mq_b, mk_p, mv_p, length=lp)
    y_p, o_s = _attn_out_mlp(x1_p, o_p, *mlp, side=side)
    y_s = _attn_out_mlp(x1_s, _from_cache_rows(o_s, ls).reshape(bs * ls, D_MODEL), *mlp)

    kv5 = lambda a: _from_cache_rows(a, MEM_LEN).reshape(1, bp, MEM_LEN, MEM_HEADS, MEM_HEAD_DIM)
    return (y_p.reshape(bp, lp, D_MODEL), y_s.reshape(bs, ls, D_MODEL), *states_p, kv5(mk_rows), kv5(mv_rows),
            *states_s)
```

```python
import functools
import math

import numpy as np
import jax
import jax.numpy as jnp
from jax import lax
from jax.experimental import pallas as pl
from jax.experimental.pallas import tpu as pltpu

F32 = jnp.float32
BF16 = jnp.bfloat16

D_MODEL = 1024
SSM_WIDTH = 512
SSM_GROUP = 16
SSM_GROUPS = 32
SSM_STATE = 64
SSM_LANES = SSM_GROUPS * SSM_STATE
RET_WIDTH = 512
RET_HEADS = 4
RET_HEAD_DIM = 128
RET_CHUNK = 128
ROPE_BASE = 10000.0
MEM_LEN = 256
MEM_HEADS = 4
MEM_HEAD_DIM = 256
D_FF = 4096
PROJ_WIDTH = SSM_WIDTH + 4 * RET_WIDTH
EPS = 1e-6
PAST_LEN = 16384

CACHE_ROWS = MEM_LEN * MEM_HEADS * MEM_HEAD_DIM // 128
CACHE_SUB = MEM_HEADS * MEM_HEAD_DIM // 128

BF16_ROWS = 16
VMEM_LIMIT = 56 * 1024 * 1024


def _dot(a, b):
    return jnp.dot(a, b, preferred_element_type=F32)


def _dot_nt(a, b):
    return lax.dot_general(a, b, (((1,), (1,)), ((), ())), preferred_element_type=F32)


def _dot_tn(a, b):
    return lax.dot_general(a, b, (((0,), (0,)), ((), ())), preferred_element_type=F32)


def _rmsnorm(x, g):
    return x * lax.rsqrt(jnp.mean(x * x, axis=-1, keepdims=True) + EPS) * g


def _sigmoid(x):
    return 1.0 / (1.0 + jnp.exp(-x))


def _gelu_tanh(x):
    c = math.sqrt(2.0 / math.pi)
    return x * (0.5 * (1.0 + jnp.tanh(c * (x + 0.044715 * (x * x * x)))))


def _params(*sem):
    return pltpu.CompilerParams(dimension_semantics=sem, vmem_limit_bytes=VMEM_LIMIT)


def _full(shape):
    return pl.BlockSpec(shape, lambda *_: (0,) * len(shape))


SSM_BLOCK = 4
SSM_TILE_GROUPS = 4
SSM_TILES = SSM_GROUPS // SSM_TILE_GROUPS


def _s5_prep_kernel(lr_ref, li_ref, ldt_ref, br_ref, bi_ref, cr_ref, ci_ref,
                    ar_ref, ai_ref, winr_ref, wini_ref, woutr_ref, wouti_ref, kdir_ref):
    lr = lr_ref[...]
    li = li_ref[...]
    dt = jnp.exp(ldt_ref[...])
    mag = jnp.exp(lr * dt)
    ab_re = mag * jnp.cos(li * dt)
    ab_im = mag * jnp.sin(li * dt)
    den = lr * lr + li * li
    f_re = ((ab_re - 1.0) * lr + ab_im * li) / den
    f_im = (ab_im * lr - (ab_re - 1.0) * li) / den
    bb_re = f_re * br_ref[...] - f_im * bi_ref[...]
    bb_im = f_re * bi_ref[...] + f_im * br_ref[...]
    cr = cr_ref[...]
    ci = ci_ref[...]
    pows = [(jnp.ones_like(ab_re), jnp.zeros_like(ab_re))]
    for _ in range(SSM_BLOCK):
        pr, pi = pows[-1]
        pows.append((pr * ab_re - pi * ab_im, pr * ab_im + pi * ab_re))
    ar_ref[...], ai_ref[...] = pows[SSM_BLOCK]

    R, T, TG = SSM_BLOCK, SSM_TILES, SSM_TILE_GROUPS
    rows = SSM_GROUPS * SSM_GROUP
    side = TG * SSM_GROUP
    exact = dict(precision=lax.Precision.HIGHEST, preferred_element_type=F32)
    iota = lambda shape, d: lax.broadcasted_iota(jnp.int32, shape, d)
    rep4 = lambda a, axis: jnp.concatenate([a] * TG, axis=axis)
    tiles = lambda a: a.reshape(T, side, a.shape[-1])

    group_of = lambda i: (i & (side - 1)) >> 4
    own_in = group_of(iota((rows, 256), 0)) == iota((rows, 256), 1) >> 6
    for s in range(R):
        pr, pi = pows[R - 1 - s]
        for ref, w in ((winr_ref, pr * bb_re - pi * bb_im), (wini_ref, pr * bb_im + pi * bb_re)):
            ref[:, side * s:side * (s + 1), :] = tiles(jnp.where(own_in, rep4(w, 1), 0.0)).astype(BF16)

    eye = (iota((rows, rows), 0) == iota((rows, rows), 1)).astype(F32)
    own_out = iota((256, rows), 0) >> 6 == group_of(iota((256, rows), 1))
    for ref, sign in ((woutr_ref, 1.0), (wouti_ref, -1.0)):
        per_lag = []
        for j in range(R):
            pr, pi = pows[j + 1]
            w = cr * pr - ci * pi if sign > 0 else -(cr * pi + ci * pr)
            wt = lax.dot_general(w, eye, (((0,), (0,)), ((), ())), **exact)
            per_lag.append(jnp.where(own_out, rep4(wt, 0), 0.0))
        for n in range(T):
            ref[n] = jnp.concatenate([m[:, side * n:side * (n + 1)] for m in per_lag], axis=1).astype(BF16)

    grouped = lambda a: a.reshape(SSM_GROUPS, SSM_GROUP, SSM_STATE)
    contract = lambda a, b: lax.dot_general(a, b, (((2,), (2,)), ((0,), (0,))), **exact)
    own_dir = group_of(iota((rows, side), 0)) == iota((rows, side), 1) >> 4
    by_lag = []
    for d in range(R):
        pr, pi = pows[d]
        er = grouped(pr * bb_re - pi * bb_im)
        ei = grouped(pr * bb_im + pi * bb_re)
        kt = (contract(er, grouped(cr)) - contract(ei, grouped(ci))).reshape(rows, SSM_GROUP)
        by_lag.append(tiles(jnp.where(own_dir, rep4(kt, 1), 0.0)))
    nothing = jnp.zeros_like(by_lag[0])
    for s in range(R):
        kdir_ref[:, side * s:side * (s + 1), :] = jnp.concatenate(
            [by_lag[j - s] if j >= s else nothing for j in range(R)], axis=2).astype(BF16)


def _s5_block_maps(lam_re, lam_im, log_dt, b_re, b_im, c_re, c_im):
    rows = SSM_GROUPS * SSM_GROUP
    rep = lambda a: jnp.repeat(a, SSM_GROUP, axis=0)
    ldt = jnp.broadcast_to(log_dt[:, None], (SSM_GROUPS, SSM_STATE))
    bt = lambda b: b.transpose(0, 2, 1).reshape(rows, SSM_STATE)
    ct = lambda c: c.reshape(rows, SSM_STATE)
    one = jax.ShapeDtypeStruct((rows, SSM_STATE), F32)
    tile = jax.ShapeDtypeStruct((SSM_TILES, 256, 256), BF16)
    ar, ai, winr, wini, woutr, wouti, direct = pl.pallas_call(
        _s5_prep_kernel,
        out_shape=(one, one, tile, tile, tile, tile, tile),
        compiler_params=pltpu.CompilerParams(vmem_limit_bytes=VMEM_LIMIT),
        name="s5_block_maps",
    )(rep(lam_re), rep(lam_im), rep(ldt), bt(b_re), bt(b_im), ct(c_re), ct(c_im))
    a_re = ar[::SSM_GROUP].reshape(1, SSM_LANES)
    a_im = ai[::SSM_GROUP].reshape(1, SSM_LANES)
    return a_re, a_im, winr, wini, woutr, wouti, direct


def _proj_kernel(x_ref, g_ref, w_ref, *rest, n_cast):
    casts_in, rest = rest[:n_cast], rest[n_cast:]
    u_ref, q_ref, k_ref, v_ref, gate_ref = rest[:5]
    w = w_ref[...]
    if n_cast:
        w = w.astype(BF16)
        rest[5][...] = w
        for src, dst in zip(casts_in, rest[6:]):
            dst[...] = src[...].astype(BF16)
    h = _rmsnorm(x_ref[...], g_ref[...]).astype(BF16)
    proj = _dot(h, w)
    u_ref[...] = proj[:, :SSM_WIDTH]
    q_ref[...] = proj[:, SSM_WIDTH:SSM_WIDTH + RET_WIDTH]
    k_ref[...] = proj[:, SSM_WIDTH + RET_WIDTH:SSM_WIDTH + 2 * RET_WIDTH]
    v_ref[...] = proj[:, SSM_WIDTH + 2 * RET_WIDTH:SSM_WIDTH + 3 * RET_WIDTH].astype(BF16)
    gate_ref[...] = proj[:, SSM_WIDTH + 3 * RET_WIDTH:]


def _project(x, g_mix, w_in, casts=()):
    t = x.shape[0]
    tm = min(t, 512 if casts else 1024)
    steps = t // tm
    row = lambda w: pl.BlockSpec((tm, w), lambda i: (i, 0))
    f = jax.ShapeDtypeStruct((t, 512), F32)
    in_specs = [row(D_MODEL), _full((1, D_MODEL)), _full((D_MODEL, PROJ_WIDTH))]
    out_specs = [row(512)] * 5
    out_shape = [f, f, f, jax.ShapeDtypeStruct((t, 512), BF16), f]
    if casts:
        out_specs.append(_full((D_MODEL, PROJ_WIDTH)))
        out_shape.append(jax.ShapeDtypeStruct((D_MODEL, PROJ_WIDTH), BF16))
        for c in casts:
            rows, rem = divmod(c.shape[0], steps)
            if rem or rows % BF16_ROWS:
                raise ValueError("side-cast weights must split into whole bf16 tiles per grid step")
            blk = pl.BlockSpec((rows, c.shape[1]), lambda i: (i, 0))
            in_specs.append(blk)
            out_specs.append(blk)
            out_shape.append(jax.ShapeDtypeStruct(c.shape, BF16))
    outs = pl.pallas_call(
        functools.partial(_proj_kernel, n_cast=len(casts)),
        grid=(steps,),
        in_specs=in_specs,
        out_specs=out_specs,
        out_shape=out_shape,
        compiler_params=_params("arbitrary" if casts else "parallel"),
        name="in_proj",
    )(x, g_mix, w_in, *casts)
    return outs[:5], outs[5:]


def _s5_kernel(u_ref, h0r_ref, h0i_ref, abr_ref, abi_ref, bre_ref, bim_ref, cre_ref, ncim_ref,
               d_ref, wglu_ref, out_ref, hr_ref, hi_ref, bur_scr, bui_scr, xr_scr, xi_scr,
               *, batch, steps, unroll, width):
    half = SSM_LANES // 2
    kin = SSM_WIDTH // 2
    kout = SSM_WIDTH // 2

    @pl.when(pl.program_id(0) == 0)
    def _():
        hr_ref[...] = h0r_ref[...]
        hi_ref[...] = h0i_ref[...]

    rows = steps * batch
    u = jnp.swapaxes(u_ref[...], 0, 1).reshape(rows, SSM_WIDTH)
    ub = u.astype(BF16)
    for p in range(2):
        lanes = slice(half * p, half * (p + 1))
        uk = ub[:, kin * p:kin * (p + 1)]
        bur_scr[:, lanes] = _dot(uk, bre_ref[kin * p:kin * (p + 1), lanes])
        bui_scr[:, lanes] = _dot(uk, bim_ref[kin * p:kin * (p + 1), lanes])
    ys = []
    for p in range(2):
        lanes = slice(half * p, half * (p + 1))
        for j in range(half // width):
            glob = slice(half * p + j * width, half * p + (j + 1) * width)
            ar = jnp.broadcast_to(abr_ref[:, glob], (batch, width))
            ai = jnp.broadcast_to(abi_ref[:, glob], (batch, width))
            xr = hr_ref[:, glob]
            xi = hi_ref[:, glob]
            for i in range(steps // unroll):
                base = i * unroll * batch
                out_r, out_i = [], []
                for s in range(unroll):
                    at = slice(base + s * batch, base + (s + 1) * batch)
                    xr, xi = ar * xr - ai * xi + bur_scr[at, glob], ar * xi + ai * xr + bui_scr[at, glob]
                    out_r.append(xr)
                    out_i.append(xi)
                blk = slice(base, base + unroll * batch)
                xr_scr[blk, glob] = jnp.concatenate(out_r, axis=0).astype(BF16)
                xi_scr[blk, glob] = jnp.concatenate(out_i, axis=0).astype(BF16)
            hr_ref[:, glob] = xr
            hi_ref[:, glob] = xi
        cols = slice(kout * p, kout * (p + 1))
        ys.append(_dot(xr_scr[:, lanes], cre_ref[lanes, cols]) + _dot(xi_scr[:, lanes], ncim_ref[lanes, cols]))
    y = jnp.concatenate(ys, axis=1) + d_ref[...] * u
    z = _gelu_tanh(y)
    out = z * _sigmoid(_dot(z.astype(BF16), wglu_ref[...]))
    out_ref[...] = jnp.swapaxes(out.reshape(steps, batch, SSM_WIDTH), 0, 1).astype(BF16)


def _s5_mixer(u, h0_re, h0_im, ab_re, ab_im, b_re, b_im, c_re, nc_im, d_skip, w_glu, *, batch, length):
    steps = min(length, 128)
    unroll = max(1, min(steps, BF16_ROWS // batch))
    if (unroll * batch) % BF16_ROWS:
        raise ValueError("S5 scan stores need whole bf16 tiles")
    width = SSM_LANES // 2 if batch <= 8 else 128
    rows = steps * batch
    kern = functools.partial(_s5_kernel, batch=batch, steps=steps, unroll=unroll, width=width)
    st = jax.ShapeDtypeStruct((batch, SSM_LANES), F32)
    return pl.pallas_call(
        kern,
        grid=(length // steps,),
        in_specs=[pl.BlockSpec((batch, steps, SSM_WIDTH), lambda i: (0, i, 0)),
                  _full((batch, SSM_LANES)), _full((batch, SSM_LANES)),
                  _full((1, SSM_LANES)), _full((1, SSM_LANES)),
                  _full((SSM_WIDTH, SSM_LANES)), _full((SSM_WIDTH, SSM_LANES)),
                  _full((SSM_LANES, SSM_WIDTH)), _full((SSM_LANES, SSM_WIDTH)),
                  _full((1, SSM_WIDTH)), _full((SSM_WIDTH, SSM_WIDTH))],
        out_specs=[pl.BlockSpec((batch, steps, SSM_WIDTH), lambda i: (0, i, 0)),
                   _full((batch, SSM_LANES)), _full((batch, SSM_LANES))],
        out_shape=(jax.ShapeDtypeStruct((batch, length, SSM_WIDTH), BF16), st, st),
        scratch_shapes=[pltpu.VMEM((rows, SSM_LANES), F32), pltpu.VMEM((rows, SSM_LANES), F32),
                        pltpu.VMEM((rows, SSM_LANES), BF16), pltpu.VMEM((rows, SSM_LANES), BF16)],
        compiler_params=_params("arbitrary"),
        name="s5_mixer",
    )(u, h0_re, h0_im, ab_re, ab_im, b_re, b_im, c_re, nc_im, d_skip, w_glu)


def _s5_block_kernel(u_ref, h0r_ref, h0i_ref, ar_ref, ai_ref, winr_ref, wini_ref, woutr_ref, wouti_ref,
                     kdir_ref, d_ref, wglu_ref, out_ref, hr_ref, hi_ref, vr_scr, vi_scr, xr_scr, xi_scr,
                     *, batch, steps, pair, width):
    R, T = SSM_BLOCK, SSM_TILES
    chan = SSM_WIDTH // T
    blocks = steps // R
    rows, brow = steps * batch, blocks * batch

    @pl.when(pl.program_id(0) == 0)
    def _():
        hr_ref[...] = h0r_ref[...]
        hi_ref[...] = h0i_ref[...]

    u = jnp.swapaxes(u_ref[...], 0, 1).reshape(rows, SSM_WIDTH)
    u4 = u.reshape(blocks, R, batch, SSM_WIDTH)
    lag = [u4[:, s].reshape(brow, SSM_WIDTH) for s in range(R)]
    direct = []
    for n in range(T):
        un = jnp.concatenate([lag[s][:, chan * n:chan * (n + 1)] for s in range(R)], axis=1).astype(BF16)
        lanes = slice(256 * n, 256 * (n + 1))
        vr_scr[:, lanes] = _dot(un, winr_ref[n])
        vi_scr[:, lanes] = _dot(un, wini_ref[n])
        direct.append(_dot(un, kdir_ref[n]))

    for j in range(SSM_LANES // width):
        glob = slice(j * width, (j + 1) * width)
        ar = jnp.broadcast_to(ar_ref[:, glob], (batch, width))
        ai = jnp.broadcast_to(ai_ref[:, glob], (batch, width))
        xr = hr_ref[:, glob]
        xi = hi_ref[:, glob]
        for i in range(blocks // pair):
            prev_r, prev_i = [], []
            for s in range(pair):
                at = slice((i * pair + s) * batch, (i * pair + s + 1) * batch)
                prev_r.append(xr)
                prev_i.append(xi)
                xr, xi = ar * xr - ai * xi + vr_scr[at, glob], ar * xi + ai * xr + vi_scr[at, glob]
            blk = slice(i * pair * batch, (i + 1) * pair * batch)
            xr_scr[blk, glob] = jnp.concatenate(prev_r, axis=0).astype(BF16)
            xi_scr[blk, glob] = jnp.concatenate(prev_i, axis=0).astype(BF16)
        hr_ref[:, glob] = xr
        hi_ref[:, glob] = xi

    y4 = []
    for n in range(T):
        lanes = slice(256 * n, 256 * (n + 1))
        y4.append(_dot(xr_scr[:, lanes], woutr_ref[n]) + _dot(xi_scr[:, lanes], wouti_ref[n]) + direct[n])
    per_lag = [jnp.concatenate([y4[n][:, chan * j:chan * (j + 1)] for n in range(T)], axis=1)
               .reshape(blocks, batch, SSM_WIDTH) for j in range(R)]
    y = jnp.stack(per_lag, axis=1).reshape(rows, SSM_WIDTH) + d_ref[...] * u
    z = _gelu_tanh(y)
    out = z * _sigmoid(_dot(z.astype(BF16), wglu_ref[...]))
    out_ref[...] = jnp.swapaxes(out.reshape(steps, batch, SSM_WIDTH), 0, 1).astype(BF16)


def _s5_mixer_blocked(u, h0_re, h0_im, maps, d_skip, w_glu, *, batch, length):
    a_re, a_im, winr, wini, woutr, wouti, direct = maps
    steps = min(length, 128)
    if steps % SSM_BLOCK or length % steps:
        raise ValueError("sequence length must be a multiple of the S5 block")
    blocks = steps // SSM_BLOCK
    pair = max(1, min(blocks, BF16_ROWS // batch))
    if (pair * batch) % BF16_ROWS or blocks % pair:
        raise ValueError("S5 state stores need whole bf16 tiles")
    width = SSM_LANES // 2 if batch <= 8 else 128
    brow = blocks * batch
    kern = functools.partial(_s5_block_kernel, batch=batch, steps=steps, pair=pair, width=width)
    st = jax.ShapeDtypeStruct((batch, SSM_LANES), F32)
    tile = _full((SSM_TILES, 256, 256))
    return pl.pallas_call(
        kern,
        grid=(length // steps,),
        in_specs=[pl.BlockSpec((batch, steps, SSM_WIDTH), lambda i: (0, i, 0)),
                  _full((batch, SSM_LANES)), _full((batch, SSM_LANES)),
                  _full((1, SSM_LANES)), _full((1, SSM_LANES)),
                  tile, tile, tile, tile, tile,
                  _full((1, SSM_WIDTH)), _full((SSM_WIDTH, SSM_WIDTH))],
        out_specs=[pl.BlockSpec((batch, steps, SSM_WIDTH), lambda i: (0, i, 0)),
                   _full((batch, SSM_LANES)), _full((batch, SSM_LANES))],
        out_shape=(jax.ShapeDtypeStruct((batch, length, SSM_WIDTH), BF16), st, st),
        scratch_shapes=[pltpu.VMEM((brow, SSM_LANES), F32), pltpu.VMEM((brow, SSM_LANES), F32),
                        pltpu.VMEM((brow, SSM_LANES), BF16), pltpu.VMEM((brow, SSM_LANES), BF16)],
        compiler_params=_params("arbitrary"),
        name="s5_mixer",
    )(u, h0_re, h0_im, a_re, a_im, winr, wini, woutr, wouti, direct, d_skip, w_glu)


def _ret_tables(length, padded, pos0):
    chunk = RET_CHUNK if length % RET_CHUNK == 0 else length
    cpad = chunk if padded == length else padded
    lg = np.log(1.0 - 2.0 ** (-5.0 - np.arange(RET_HEADS, dtype=np.float64)))
    idx = np.arange(chunk, dtype=np.float64)
    diff = idx[:, None] - idx[None, :]
    mask = np.where(diff[None] >= 0, np.exp(np.maximum(diff, 0.0)[None] * lg[:, None, None]), 0.0)
    zeta = np.exp((chunk - 1.0 - idx)[None, :] * lg[:, None])
    xi = np.exp((idx + 1.0)[None, :] * lg[:, None])
    gamma_c = tuple(float(v) for v in np.exp(chunk * lg))
    pad2 = lambda a: np.pad(a, ((0, 0), (0, cpad - chunk), (0, cpad - chunk)))
    lane = lambda a: np.broadcast_to(np.pad(a, ((0, 0), (0, cpad - chunk)))[:, :, None],
                                     (RET_HEADS, cpad, RET_HEAD_DIM))
    half = RET_HEAD_DIM // 2
    inv = ROPE_BASE ** (-np.arange(half, dtype=np.float64) / half)
    ang = (pos0 + np.arange(length, dtype=np.float64))[:, None] * inv[None, :]
    cc = np.concatenate([np.cos(ang), np.cos(ang)], axis=1)
    ss = np.concatenate([-np.sin(ang), np.sin(ang)], axis=1)
    rows = lambda a: np.pad(a, ((0, padded - length), (0, 0)))
    f = lambda a: jnp.asarray(np.ascontiguousarray(a), dtype=F32)
    return cpad, gamma_c, f(pad2(mask)), f(lane(zeta)), f(lane(xi)), f(rows(cc)), f(rows(ss))


def _ret_kernel(q_ref, k_ref, v_ref, g_ref, cc_ref, ss_ref, mask_ref, zeta_ref, xi_ref, gain_ref,
                s0_ref, o_ref, s_ref, *, bb, gamma_c):
    @pl.when(pl.program_id(1) == 0)
    def _():
        s_ref[...] = s0_ref[...]

    cc = cc_ref[...]
    ss = ss_ref[...]
    half = RET_HEAD_DIM // 2

    def rope(x):
        return x * cc + pltpu.roll(x, half, axis=1) * ss

    heads = range(RET_HEADS)
    cols = [slice(h * RET_HEAD_DIM, (h + 1) * RET_HEAD_DIM) for h in heads]

    def per_batch(b, carry):
        qb = [(rope(q_ref[b, :, c]) * (RET_HEAD_DIM ** -0.5)).astype(BF16) for c in cols]
        kr = [rope(k_ref[b, :, c]) for c in cols]
        kb = [x.astype(BF16) for x in kr]
        kz = [(kr[h] * zeta_ref[h]).astype(BF16) for h in heads]
        vb = [v_ref[b, :, c] for c in cols]
        s_prev = [s_ref[b, h] for h in heads]
        scores = [(_dot_nt(qb[h], kb[h]) * mask_ref[h]).astype(BF16) for h in heads]
        cross = [_dot(qb[h], s_prev[h].astype(BF16)) * xi_ref[h] for h in heads]
        kv = [_dot_tn(kz[h], vb[h]) for h in heads]
        inner = [_dot(scores[h], vb[h]) for h in heads]
        outs = []
        for h in heads:
            s_ref[b, h] = s_prev[h] * gamma_c[h] + kv[h]
            o = inner[h] + cross[h]
            mu = jnp.mean(o, axis=-1, keepdims=True)
            oc = o - mu
            var = jnp.mean(oc * oc, axis=-1, keepdims=True)
            on = oc * lax.rsqrt(var + EPS) * gain_ref[:, cols[h]]
            gate = g_ref[b, :, cols[h]]
            outs.append((gate * _sigmoid(gate) * on).astype(BF16))
        o_ref[b] = jnp.concatenate(outs, axis=1)
        return carry

    lax.fori_loop(0, bb, per_batch, 0, unroll=2)


def _retention(q, k, v, g, s0, gn_gain, *, pos0):
    batch, length, _ = q.shape
    padded = length if length % BF16_ROWS == 0 else pl.cdiv(length, BF16_ROWS) * BF16_ROWS
    chunk, gamma_c, mask, zeta, xi, cc, ss = _ret_tables(length, padded, pos0)
    if padded != length:
        pad = lambda a: jnp.pad(a, ((0, 0), (0, padded - length), (0, 0)))
        q, k, v, g = pad(q), pad(k), pad(v), pad(g)
    bb = 8
    tok = pl.BlockSpec((bb, chunk, RET_WIDTH), lambda i, c: (i, c, 0))
    tab = pl.BlockSpec((chunk, RET_HEAD_DIM), lambda i, c: (c, 0))
    state = pl.BlockSpec((bb, RET_HEADS, RET_HEAD_DIM, RET_HEAD_DIM), lambda i, c: (i, 0, 0, 0))
    out, s_new = pl.pallas_call(
        functools.partial(_ret_kernel, bb=bb, gamma_c=gamma_c),
        grid=(batch // bb, padded // chunk),
        in_specs=[tok, tok, tok, tok, tab, tab,
                  _full((RET_HEADS, chunk, chunk)), _full((RET_HEADS, chunk, RET_HEAD_DIM)),
                  _full((RET_HEADS, chunk, RET_HEAD_DIM)), _full((1, RET_WIDTH)), state],
        out_specs=[tok, state],
        out_shape=(jax.ShapeDtypeStruct((batch, padded, RET_WIDTH), BF16),
                   jax.ShapeDtypeStruct(s0.shape, F32)),
        compiler_params=_params("parallel", "arbitrary"),
        name="retention",
    )(q, k, v, g, cc, ss, mask, zeta, xi, gn_gain, s0)
    return out[:, :length], s_new


def _outq_kernel(x_ref, ssm_ref, ret_ref, wout_ref, g_ref, wq_ref, x1_ref, q_ref):
    x1 = (x_ref[...] + _dot(ssm_ref[...], wout_ref[:SSM_WIDTH, :])
          + _dot(ret_ref[...], wout_ref[SSM_WIDTH:, :]))
    x1_ref[...] = x1
    q_ref[...] = _dot(_rmsnorm(x1, g_ref[...]).astype(BF16), wq_ref[...]).astype(BF16)


def _out_and_query(x, ssm, ret, w_out, g_xattn, w_mq):
    t = x.shape[0]
    tm = min(t, 512)
    row = lambda w: pl.BlockSpec((tm, w), lambda i: (i, 0))
    return pl.pallas_call(
        _outq_kernel,
        grid=(t // tm,),
        in_specs=[row(D_MODEL), row(SSM_WIDTH), row(RET_WIDTH), _full((D_MODEL, D_MODEL)),
                  _full((1, D_MODEL)), _full((D_MODEL, D_MODEL))],
        out_specs=[row(D_MODEL), row(D_MODEL)],
        out_shape=(jax.ShapeDtypeStruct((t, D_MODEL), F32), jax.ShapeDtypeStruct((t, D_MODEL), BF16)),
        compiler_params=_params("parallel"),
        name="out_proj_query",
    )(x, ssm, ret, w_out, g_xattn, w_mq)


def _cache_row_order(x):
    halves = MEM_HEAD_DIM // 128
    tiles = [x[:, (h * halves + d) * 128:(h * halves + d + 1) * 128]
             for d in range(halves) for h in range(MEM_HEADS)]
    return jnp.swapaxes(jnp.stack(tiles, axis=0), 0, 1).reshape(x.shape[0] * CACHE_SUB, 128)


def _memkv_kernel(m_ref, g_ref, wk_ref, wv_ref, k_ref, v_ref, kb_ref, vb_ref):
    m = _rmsnorm(m_ref[0], g_ref[...]).astype(BF16)
    k = _dot(m, wk_ref[...])
    v = _dot(m, wv_ref[...])
    k_ref[0] = _cache_row_order(k)
    v_ref[0] = _cache_row_order(v)
    kb_ref[0] = k.astype(BF16)
    vb_ref[0] = v.astype(BF16)


def _memory_kv(mem, g_mem, w_mk, w_mv):
    batch = mem.shape[0]
    tok = pl.BlockSpec((1, MEM_LEN, D_MODEL), lambda i: (i, 0, 0))
    cache = pl.BlockSpec((1, CACHE_ROWS, 128), lambda i: (i, 0, 0))
    f = jax.ShapeDtypeStruct((batch, CACHE_ROWS, 128), F32)
    h = jax.ShapeDtypeStruct((batch, MEM_LEN, D_MODEL), BF16)
    return pl.pallas_call(
        _memkv_kernel,
        grid=(batch,),
        in_specs=[tok, _full((1, D_MODEL)), _full((D_MODEL, D_MODEL)), _full((D_MODEL, D_MODEL))],
        out_specs=[cache, cache, tok, tok],
        out_shape=(f, f, h, h),
        compiler_params=_params("parallel"),
        name="memory_kv",
    )(mem, g_mem, w_mk, w_mv)


def _outattn_kernel(x_ref, ssm_ref, ret_ref, wout_ref, g_ref, wq_ref, k_ref, v_ref, x1_ref, o_ref):
    x1 = (x_ref[...] + _dot(ssm_ref[...], wout_ref[:SSM_WIDTH, :])
          + _dot(ret_ref[...], wout_ref[SSM_WIDTH:, :]))
    x1_ref[...] = x1
    q = _dot(_rmsnorm(x1, g_ref[...]).astype(BF16), wq_ref[...]).astype(BF16)
    cols = [slice(h * MEM_HEAD_DIM, (h + 1) * MEM_HEAD_DIM) for h in range(MEM_HEADS)]
    scores = [_dot_nt(q[:, c], k_ref[0, :, c]) * (MEM_HEAD_DIM ** -0.5) for c in cols]
    probs = []
    for s in scores:
        e = jnp.exp(s - jnp.max(s, axis=-1, keepdims=True))
        probs.append((e / jnp.sum(e, axis=-1, keepdims=True)).astype(BF16))
    o_ref[...] = jnp.concatenate([_dot(p, v_ref[0, :, c]).astype(BF16) for p, c in zip(probs, cols)], axis=1)


def _out_and_attention(x, ssm, ret, w_out, g_xattn, w_mq, mk, mv, *, length):
    t = x.shape[0]
    tm = min(length, 1024)
    per_seq = length // tm
    row = lambda w: pl.BlockSpec((tm, w), lambda i: (i, 0))
    mem = pl.BlockSpec((1, MEM_LEN, D_MODEL), lambda i: (i // per_seq, 0, 0))
    return pl.pallas_call(
        _outattn_kernel,
        grid=(t // tm,),
        in_specs=[row(D_MODEL), row(SSM_WIDTH), row(RET_WIDTH), _full((D_MODEL, D_MODEL)),
                  _full((1, D_MODEL)), _full((D_MODEL, D_MODEL)), mem, mem],
        out_specs=[row(D_MODEL), row(D_MODEL)],
        out_shape=(jax.ShapeDtypeStruct((t, D_MODEL), F32), jax.ShapeDtypeStruct((t, D_MODEL), BF16)),
        compiler_params=_params("parallel"),
        name="out_proj_attention",
    )(x, ssm, ret, w_out, g_xattn, w_mq, mk, mv)


def _cached_attn_kernel(q_ref, k_ref, v_ref, o_ref, *, bb, length, between=None):
    rows = length * CACHE_SUB
    hit = (lax.broadcasted_iota(jnp.int32, (CACHE_SUB, CACHE_ROWS), 0)
           == lax.broadcasted_iota(jnp.int32, (CACHE_SUB, CACHE_ROWS), 1) % CACHE_SUB)
    first_half = lax.broadcasted_iota(jnp.int32, (length, 128), 1) % CACHE_SUB < MEM_HEADS
    tiles = CACHE_ROWS // 128

    def class_reduce(x, op):
        for shift in (8, 16, 32, 64):
            x = op(x, pltpu.roll(x, shift, axis=1))
        return x

    def softmax_rows(r):
        z = jnp.concatenate(
            [jnp.sum(jnp.where(hit, r[CACHE_SUB * t:CACHE_SUB * (t + 1)], 0.0), axis=0, keepdims=True)
             for t in range(length)], axis=0)
        parts = []
        for i in range(tiles):
            zi = z[:, 128 * i:128 * (i + 1)]
            parts.append(zi + pltpu.roll(zi, 128 - MEM_HEADS, axis=1))
        mx = parts[0]
        for pi in parts[1:]:
            mx = jnp.maximum(mx, pi)
        mx = class_reduce(mx, jnp.maximum)
        es = [jnp.exp(pi - mx) for pi in parts]
        tot = es[0]
        for ei in es[1:]:
            tot = tot + ei
        tot = class_reduce(tot, jnp.add)
        ps = []
        for ei in es:
            pi = ei / tot
            ps.append(jnp.where(first_half, pi, pltpu.roll(pi, MEM_HEADS, axis=1)))
        p = jnp.concatenate(ps, axis=1)
        return jnp.concatenate(
            [jnp.where(hit, jnp.broadcast_to(p[t:t + 1], (CACHE_SUB, CACHE_ROWS)), 0.0)
             for t in range(length)], axis=0).astype(BF16)

    scores = [_dot_nt(q_ref[b], k_ref[b].astype(BF16)) * (MEM_HEAD_DIM ** -0.5) for b in range(bb)]
    if between is not None:
        between()
    probs = [softmax_rows(r) for r in scores]
    for b in range(bb):
        o_ref[b] = _dot(probs[b], v_ref[b].astype(BF16)).astype(BF16)


def _to_cache_rows(a, lead):
    halves = MEM_HEAD_DIM // 128
    batch = a.size // (lead * D_MODEL)
    a = a.reshape(batch, lead, MEM_HEADS, halves, 128).transpose(0, 1, 3, 2, 4)
    return a.reshape(batch, lead * CACHE_SUB, 128)


def _from_cache_rows(a, lead):
    halves = MEM_HEAD_DIM // 128
    batch = a.shape[0]
    a = a.reshape(batch, lead, halves, MEM_HEADS, 128).transpose(0, 1, 3, 2, 4)
    return a.reshape(batch, lead, D_MODEL)


MLP_CHUNK = 1024


def _post_kernel(x1_ref, o_ref, wo_ref, gm_ref, wup_ref, wdn_ref, gf_ref, *rest, side_bb, side_len):
    if side_bb:
        qs_ref, ck_ref, cv_ref, y_ref, os_ref = rest
    else:
        (y_ref,) = rest
    x2 = x1_ref[...] + _dot(o_ref[...], wo_ref[...])
    result = []

    def mlp():
        h = _rmsnorm(x2, gm_ref[...]).astype(BF16)
        chunks = [slice(c, c + MLP_CHUNK) for c in range(0, D_FF, MLP_CHUNK)]
        acc = x2
        up = _dot(h, wup_ref[:, chunks[0]])
        for c, cols in enumerate(chunks):
            nxt = _dot(h, wup_ref[:, chunks[c + 1]]) if c + 1 < len(chunks) else None
            a = jnp.maximum(up, 0.0)
            acc = acc + _dot((a * a).astype(BF16), wdn_ref[cols, :])
            up = nxt
        result.append(acc)

    if side_bb:
        _cached_attn_kernel(qs_ref, ck_ref, cv_ref, os_ref, bb=side_bb, length=side_len, between=mlp)
    else:
        mlp()
    y_ref[...] = _rmsnorm(result[0], gf_ref[...])


def _attn_out_mlp(x1, o, w_mo, g_mlp, w_up, w_down, g_final, side=None):
    t = x1.shape[0]
    tm = min(t, 512)
    steps = t // tm
    row = pl.BlockSpec((tm, D_MODEL), lambda i: (i, 0))
    in_specs = [row, row, _full((D_MODEL, D_MODEL)), _full((1, D_MODEL)),
                _full((D_MODEL, D_FF)), _full((D_FF, D_MODEL)), _full((1, D_MODEL))]
    out_specs = [row]
    out_shape = [jax.ShapeDtypeStruct((t, D_MODEL), F32)]
    args = [x1, o, w_mo, g_mlp, w_up, w_down, g_final]
    side_bb = side_len = 0
    if side is not None:
        q_rows = side[0]
        side_bb, rem = divmod(q_rows.shape[0], steps)
        if rem or not side_bb:
            raise ValueError("side attention sequences must spread evenly over the grid steps")
        side_len = q_rows.shape[1] // CACHE_SUB
        blk = lambda r: pl.BlockSpec((side_bb, r, 128), lambda i: (i, 0, 0))
        in_specs += [blk(q_rows.shape[1]), blk(CACHE_ROWS), blk(CACHE_ROWS)]
        out_specs.append(blk(q_rows.shape[1]))
        out_shape.append(jax.ShapeDtypeStruct(q_rows.shape, BF16))
        args += list(side)
    outs = pl.pallas_call(
        functools.partial(_post_kernel, side_bb=side_bb, side_len=side_len),
        grid=(steps,),
        in_specs=in_specs,
        out_specs=out_specs,
        out_shape=out_shape,
        compiler_params=_params("parallel"),
        name="attn_out_mlp",
    )(*args)
    return outs if side is not None else outs[0]


def _mixers(proj, batch, length, s5_re, s5_im, ret_s, pos0, p):
    tokens = batch * length
    u, q, k, v, g = proj
    ssm, s5_re_new, s5_im_new = _s5_mixer_blocked(
        u.reshape(batch, length, SSM_WIDTH), s5_re.reshape(batch, SSM_LANES), s5_im.reshape(batch, SSM_LANES),
        p["s5_maps"], p["d_skip"], p["w_glu"], batch=batch, length=length)
    tok3 = lambda a: a.reshape(batch, length, RET_WIDTH)
    ret, ret_new = _retention(tok3(q), tok3(k), tok3(v), tok3(g), ret_s, p["ret_gn"], pos0=pos0)
    states = (s5_re_new.reshape(1, batch, SSM_GROUPS, SSM_STATE),
              s5_im_new.reshape(1, batch, SSM_GROUPS, SSM_STATE), ret_new[None])
    return ssm.reshape(tokens, SSM_WIDTH), ret.reshape(tokens, RET_WIDTH), states


def kernel(x_prompt, x_sample, mem_prompt, state_s5_re, state_s5_im, state_ret, cache_mem_k, cache_mem_v, g_mix, w_in, lam_re, lam_im, log_dt, b_re, b_im, c_re, c_im, d_skip, w_glu, ret_gn, w_out, g_xattn, g_mem, w_mq, w_mk, w_mv, w_mo, g_mlp, w_up, w_down, g_final):
    if g_mix.shape[0] != 1:
        raise ValueError("this kernel implements the single-layer configuration")
    bp, lp, _ = x_prompt.shape
    bs, ls, _ = x_sample.shape
    vec = lambda a: a.reshape(1, -1).astype(F32)
    xp = x_prompt.reshape(bp * lp, D_MODEL)
    xs = x_sample.reshape(bs * ls, D_MODEL)
    proj_p, (w_in_b, w_up_b, w_down_b, w_out_b, w_mq_b, w_mo_b, w_mk_b, w_mv_b, w_glu_b) = _project(
        xp, vec(g_mix[0]), w_in[0],
        casts=(w_up[0], w_down[0], w_out[0], w_mq[0], w_mo[0], w_mk[0], w_mv[0], w_glu[0]))
    proj_s, _ = _project(xs, vec(g_mix[0]), w_in_b)
    p = dict(
        s5_maps=_s5_block_maps(lam_re[0], lam_im[0], log_dt[0], b_re[0], b_im[0], c_re[0], c_im[0]),
        d_skip=vec(d_skip[0]), w_glu=w_glu_b, ret_gn=vec(ret_gn[0]))
    g_x = vec(g_xattn[0])
    mlp = (w_mo_b, vec(g_mlp[0]), w_up_b, w_down_b, vec(g_final))

    ssm_s, ret_s, states_s = _mixers(proj_s, bs, ls, state_s5_re[0], state_s5_im[0], state_ret[0],
                                     float(PAST_LEN), p)
    x1_s, q_s = _out_and_query(xs, ssm_s, ret_s, w_out_b, g_x, w_mq_b)
    if (ls * CACHE_SUB) % BF16_ROWS:
        raise ValueError("cached attention needs whole bf16 tiles of query rows")
    side = (_to_cache_rows(q_s, ls), _to_cache_rows(cache_mem_k, MEM_LEN), _to_cache_rows(cache_mem_v, MEM_LEN))

    mk_rows, mv_rows, mk_p, mv_p = _memory_kv(mem_prompt, vec(g_mem[0]), w_mk_b, w_mv_b)
    zs = jnp.zeros((bp, SSM_GROUPS, SSM_STATE), F32)
    zr = jnp.zeros((bp, RET_HEADS, RET_HEAD_DIM, RET_HEAD_DIM), F32)
    ssm_p, ret_p, states_p = _mixers(proj_p, bp, lp, zs, zs, zr, 0.0, p)
    x1_p, o_p = _out_and_attention(xp, ssm_p, ret_p, w_out_b, g_x, w_mq_b, mk_p, mv_p, length=lp)
    y_p, o_s = _attn_out_mlp(x1_p, o_p, *mlp, side=side)
    y_s = _attn_out_mlp(x1_s, _from_cache_rows(o_s, ls).reshape(bs * ls, D_MODEL), *mlp)

    kv5 = lambda a: _from_cache_rows(a, MEM_LEN).reshape(1, bp, MEM_LEN, MEM_HEADS, MEM_HEAD_DIM)
    return (y_p.reshape(bp, lp, D_MODEL), y_s.reshape(bs, ls, D_MODEL), *states_p, kv5(mk_rows), kv5(mv_rows),
            *states_s)
```

```python
import functools
import math

import numpy as np
import jax
import jax.numpy as jnp
from jax import lax
from jax.experimental import pallas as pl
from jax.experimental.pallas import tpu as pltpu

F32 = jnp.float32
BF16 = jnp.bfloat16

D_MODEL = 1024
SSM_WIDTH = 512
SSM_GROUP = 16
SSM_GROUPS = 32
SSM_STATE = 64
SSM_LANES = SSM_GROUPS * SSM_STATE
RET_WIDTH = 512
RET_HEADS = 4
RET_HEAD_DIM = 128
RET_CHUNK = 128
ROPE_BASE = 10000.0
MEM_LEN = 256
MEM_HEADS = 4
MEM_HEAD_DIM = 256
MEM_SCALE = MEM_HEAD_DIM ** -0.5
D_FF = 4096
PROJ_WIDTH = SSM_WIDTH + 4 * RET_WIDTH
EPS = 1e-6
PAST_LEN = 16384

LANES = 128
MXU_TILE = 256
BF16_ROWS = 16
VMEM_LIMIT = 56 * 1024 * 1024

CACHE_ROWS = MEM_LEN * MEM_HEADS * MEM_HEAD_DIM // LANES
CACHE_SUB = MEM_HEADS * MEM_HEAD_DIM // LANES

assert math.frexp(MEM_SCALE)[0] == 0.5


def _dot(a, b):
    return jnp.dot(a, b, preferred_element_type=F32)


def _dot_nt(a, b):
    return lax.dot_general(a, b, (((1,), (1,)), ((), ())), preferred_element_type=F32)


def _dot_tn(a, b):
    return lax.dot_general(a, b, (((0,), (0,)), ((), ())), preferred_element_type=F32)


def _rmsnorm(x, g):
    return x * lax.rsqrt(jnp.mean(x * x, axis=-1, keepdims=True) + EPS) * g


def _sigmoid(x):
    return 1.0 / (1.0 + jnp.exp(-x))


def _gelu_tanh(x):
    c = math.sqrt(2.0 / math.pi)
    return x * (0.5 * (1.0 + jnp.tanh(c * (x + 0.044715 * (x * x * x)))))


def _params(*sem):
    return pltpu.CompilerParams(dimension_semantics=sem, vmem_limit_bytes=VMEM_LIMIT)


def _full(shape):
    return pl.BlockSpec(shape, lambda *_: (0,) * len(shape))


SSM_BLOCK = 4
SSM_TILE_GROUPS = MXU_TILE // SSM_STATE
SSM_TILES = SSM_GROUPS // SSM_TILE_GROUPS
SSM_TILE_CHANNELS = SSM_TILE_GROUPS * SSM_GROUP
GROUP_SHIFT = SSM_GROUP.bit_length() - 1
STATE_SHIFT = SSM_STATE.bit_length() - 1

assert SSM_BLOCK * SSM_TILE_CHANNELS == MXU_TILE and SSM_TILE_CHANNELS == SSM_STATE


def _s5_prep_kernel(lr_ref, li_ref, ldt_ref, br_ref, bi_ref, cr_ref, ci_ref,
                    ar_ref, ai_ref, winr_ref, wini_ref, woutr_ref, wouti_ref, kdir_ref):
    lr = lr_ref[...]
    li = li_ref[...]
    dt = jnp.exp(ldt_ref[...])
    mag = jnp.exp(lr * dt)
    ab_re = mag * jnp.cos(li * dt)
    ab_im = mag * jnp.sin(li * dt)
    den = lr * lr + li * li
    f_re = ((ab_re - 1.0) * lr + ab_im * li) / den
    f_im = (ab_im * lr - (ab_re - 1.0) * li) / den
    bb_re = f_re * br_ref[...] - f_im * bi_ref[...]
    bb_im = f_re * bi_ref[...] + f_im * br_ref[...]
    cr = cr_ref[...]
    ci = ci_ref[...]
    pows = [(jnp.ones_like(ab_re), jnp.zeros_like(ab_re))]
    for _ in range(SSM_BLOCK):
        pr, pi = pows[-1]
        pows.append((pr * ab_re - pi * ab_im, pr * ab_im + pi * ab_re))
    ar_ref[...], ai_ref[...] = pows[SSM_BLOCK]

    R, T, TG = SSM_BLOCK, SSM_TILES, SSM_TILE_GROUPS
    rows = SSM_GROUPS * SSM_GROUP
    side = SSM_TILE_CHANNELS
    exact = dict(precision=lax.Precision.HIGHEST, preferred_element_type=F32)
    iota = lambda shape, d: lax.broadcasted_iota(jnp.int32, shape, d)
    rep4 = lambda a, axis: jnp.concatenate([a] * TG, axis=axis)
    tiles = lambda a: a.reshape(T, side, a.shape[-1])
    group_of = lambda i: (i & (side - 1)) >> GROUP_SHIFT

    own_in = group_of(iota((rows, MXU_TILE), 0)) == iota((rows, MXU_TILE), 1) >> STATE_SHIFT
    for s in range(R):
        pr, pi = pows[R - 1 - s]
        for ref, w in ((winr_ref, pr * bb_re - pi * bb_im), (wini_ref, pr * bb_im + pi * bb_re)):
            ref[:, side * s:side * (s + 1), :] = tiles(jnp.where(own_in, rep4(w, 1), 0.0)).astype(BF16)

    eye = (iota((rows, rows), 0) == iota((rows, rows), 1)).astype(F32)
    own_out = iota((MXU_TILE, rows), 0) >> STATE_SHIFT == group_of(iota((MXU_TILE, rows), 1))
    for ref, real in ((woutr_ref, True), (wouti_ref, False)):
        per_lag = []
        for j in range(R):
            pr, pi = pows[j + 1]
            w = cr * pr - ci * pi if real else -(cr * pi + ci * pr)
            wt = lax.dot_general(w, eye, (((0,), (0,)), ((), ())), **exact)
            per_lag.append(jnp.where(own_out, rep4(wt, 0), 0.0))
        for n in range(T):
            ref[n] = jnp.concatenate([m[:, side * n:side * (n + 1)] for m in per_lag], axis=1).astype(BF16)

    grouped = lambda a: a.reshape(SSM_GROUPS, SSM_GROUP, SSM_STATE)
    contract = lambda a, b: lax.dot_general(a, b, (((2,), (2,)), ((0,), (0,))), **exact)
    own_dir = group_of(iota((rows, side), 0)) == iota((rows, side), 1) >> GROUP_SHIFT
    by_lag = []
    for d in range(R):
        pr, pi = pows[d]
        er = grouped(pr * bb_re - pi * bb_im)
        ei = grouped(pr * bb_im + pi * bb_re)
        kt = (contract(er, grouped(cr)) - contract(ei, grouped(ci))).reshape(rows, SSM_GROUP)
        by_lag.append(tiles(jnp.where(own_dir, rep4(kt, 1), 0.0)))
    nothing = jnp.zeros_like(by_lag[0])
    for s in range(R):
        kdir_ref[:, side * s:side * (s + 1), :] = jnp.concatenate(
            [by_lag[j - s] if j >= s else nothing for j in range(R)], axis=2).astype(BF16)


def _s5_block_maps(lam_re, lam_im, log_dt, b_re, b_im, c_re, c_im):
    rows = SSM_GROUPS * SSM_GROUP
    rep = lambda a: jnp.repeat(a, SSM_GROUP, axis=0)
    ldt = jnp.broadcast_to(log_dt[:, None], (SSM_GROUPS, SSM_STATE))
    bt = lambda b: b.transpose(0, 2, 1).reshape(rows, SSM_STATE)
    ct = lambda c: c.reshape(rows, SSM_STATE)
    one = jax.ShapeDtypeStruct((rows, SSM_STATE), F32)
    tile = jax.ShapeDtypeStruct((SSM_TILES, MXU_TILE, MXU_TILE), BF16)
    ar, ai, winr, wini, woutr, wouti, direct = pl.pallas_call(
        _s5_prep_kernel,
        out_shape=(one, one, tile, tile, tile, tile, tile),
        compiler_params=pltpu.CompilerParams(vmem_limit_bytes=VMEM_LIMIT),
        name="s5_block_maps",
    )(rep(lam_re), rep(lam_im), rep(ldt), bt(b_re), bt(b_im), ct(c_re), ct(c_im))
    a_re = ar[::SSM_GROUP].reshape(1, SSM_LANES)
    a_im = ai[::SSM_GROUP].reshape(1, SSM_LANES)
    return a_re, a_im, winr, wini, woutr, wouti, direct


def _retention_chunk(length):
    return RET_CHUNK if length % RET_CHUNK == 0 else length


def _log_gamma():
    return np.log(1.0 - 2.0 ** (-5.0 - np.arange(RET_HEADS, dtype=np.float64)))


def _rope_tables(length, pos0):
    half = RET_HEAD_DIM // 2
    inv = ROPE_BASE ** (-np.arange(half, dtype=np.float64) / half)
    ang = (pos0 + np.arange(length, dtype=np.float64))[:, None] * inv[None, :]
    return (np.concatenate([np.cos(ang), np.cos(ang)], axis=1),
            np.concatenate([-np.sin(ang), np.sin(ang)], axis=1))


def _zeta_rows(length):
    chunk = _retention_chunk(length)
    idx = np.arange(chunk, dtype=np.float64)
    zeta = np.exp((chunk - 1.0 - idx)[None, :] * _log_gamma()[:, None])
    return np.concatenate([np.broadcast_to(z[:, None], (chunk, RET_HEAD_DIM)) for z in zeta], axis=1)


def _decay_tables(length, padded):
    chunk = _retention_chunk(length)
    cpad = chunk if padded == length else padded
    lg = _log_gamma()
    idx = np.arange(chunk, dtype=np.float64)
    diff = idx[:, None] - idx[None, :]
    mask = np.where(diff[None] >= 0, np.exp(np.maximum(diff, 0.0)[None] * lg[:, None, None]), 0.0)
    xi = np.exp((idx + 1.0)[None, :] * lg[:, None])
    gamma_c = tuple(float(v) for v in np.exp(chunk * lg))
    mask = np.pad(mask, ((0, 0), (0, cpad - chunk), (0, cpad - chunk)))
    xi = np.broadcast_to(np.pad(xi, ((0, 0), (0, cpad - chunk)))[:, :, None], (RET_HEADS, cpad, RET_HEAD_DIM))
    f = lambda a: jnp.asarray(np.ascontiguousarray(a), dtype=F32)
    return cpad, gamma_c, f(mask), f(xi)


def _proj_kernel(x_ref, g_ref, w_ref, cc_ref, ss_ref, zeta_ref, *rest, n_cast):
    casts_in, rest = rest[:n_cast], rest[n_cast:]
    u_ref, q_ref, k_ref, kz_ref, v_ref, gate_ref = rest[:6]
    w = w_ref[...]
    if n_cast:
        w = w.astype(BF16)
        rest[6][...] = w
        for src, dst in zip(casts_in, rest[7:]):
            dst[...] = src[...].astype(BF16)
    h = _rmsnorm(x_ref[...], g_ref[...]).astype(BF16)
    proj = _dot(h, w)
    u_ref[...] = proj[:, :SSM_WIDTH]
    v_ref[...] = proj[:, SSM_WIDTH + 2 * RET_WIDTH:SSM_WIDTH + 3 * RET_WIDTH].astype(BF16)
    gate_ref[...] = proj[:, SSM_WIDTH + 3 * RET_WIDTH:]
    cc = cc_ref[...]
    ss = ss_ref[...]
    rope = lambda a: a * cc + pltpu.roll(a, RET_HEAD_DIM // 2, axis=1) * ss
    for hd in range(RET_HEADS):
        cols = slice(hd * RET_HEAD_DIM, (hd + 1) * RET_HEAD_DIM)
        q = rope(proj[:, SSM_WIDTH + hd * RET_HEAD_DIM:SSM_WIDTH + (hd + 1) * RET_HEAD_DIM])
        k = rope(proj[:, SSM_WIDTH + RET_WIDTH + hd * RET_HEAD_DIM:SSM_WIDTH + RET_WIDTH + (hd + 1) * RET_HEAD_DIM])
        q_ref[:, cols] = (q * (RET_HEAD_DIM ** -0.5)).astype(BF16)
        k_ref[:, cols] = k.astype(BF16)
        kz_ref[:, cols] = (k * zeta_ref[:, cols]).astype(BF16)


def _project(x, g_mix, w_in, *, length, pos0, casts=()):
    t = x.shape[0]
    tm = min(t, 512)
    steps = t // tm
    chunk = _retention_chunk(length)
    if tm % chunk or (length % tm and tm % length):
        raise ValueError("token tiles must hold whole retention chunks of whole or repeated sequences")
    cc, ss = _rope_tables(length, pos0)
    if length < tm:
        cc, ss = np.tile(cc, (tm // length, 1)), np.tile(ss, (tm // length, 1))
    tab_blocks = cc.shape[0] // tm
    zeta = np.tile(_zeta_rows(length), (tm // chunk, 1))
    const = lambda a: jnp.asarray(np.ascontiguousarray(a), dtype=F32)
    row = lambda w: pl.BlockSpec((tm, w), lambda i: (i, 0))
    tab = pl.BlockSpec((tm, RET_HEAD_DIM), lambda i: (i % tab_blocks, 0))
    f = jax.ShapeDtypeStruct((t, 512), F32)
    h = jax.ShapeDtypeStruct((t, 512), BF16)
    in_specs = [row(D_MODEL), _full((1, D_MODEL)), _full((D_MODEL, PROJ_WIDTH)), tab, tab, _full((tm, RET_WIDTH))]
    out_specs = [row(512)] * 6
    out_shape = [f, h, h, h, h, f]
    if casts:
        out_specs.append(_full((D_MODEL, PROJ_WIDTH)))
        out_shape.append(jax.ShapeDtypeStruct((D_MODEL, PROJ_WIDTH), BF16))
        for c in casts:
            rows, rem = divmod(c.shape[0], steps)
            if rem or rows % BF16_ROWS:
                raise ValueError("side-cast weights must split into whole bf16 tiles per grid step")
            blk = pl.BlockSpec((rows, c.shape[1]), lambda i: (i, 0))
            in_specs.append(blk)
            out_specs.append(blk)
            out_shape.append(jax.ShapeDtypeStruct(c.shape, BF16))
    outs = pl.pallas_call(
        functools.partial(_proj_kernel, n_cast=len(casts)),
        grid=(steps,),
        in_specs=in_specs,
        out_specs=out_specs,
        out_shape=out_shape,
        compiler_params=_params("arbitrary" if casts else "parallel"),
        name="in_proj",
    )(x, g_mix, w_in, const(cc), const(ss), const(zeta), *casts)
    return outs[:6], outs[6:]


def _s5_block_kernel(u_ref, h0r_ref, h0i_ref, ar_ref, ai_ref, winr_ref, wini_ref, woutr_ref, wouti_ref,
                     kdir_ref, d_ref, wglu_ref, out_ref, hr_ref, hi_ref, vr_scr, vi_scr, xr_scr, xi_scr,
                     *, batch, steps, pair, width):
    R, T = SSM_BLOCK, SSM_TILES
    chan = SSM_TILE_CHANNELS
    blocks = steps // R
    rows, brow = steps * batch, blocks * batch

    @pl.when(pl.program_id(0) == 0)
    def _():
        hr_ref[...] = h0r_ref[...]
        hi_ref[...] = h0i_ref[...]

    u = jnp.swapaxes(u_ref[...], 0, 1).reshape(rows, SSM_WIDTH)
    u4 = u.reshape(blocks, R, batch, SSM_WIDTH)
    lag = [u4[:, s].reshape(brow, SSM_WIDTH) for s in range(R)]
    direct = []
    for n in range(T):
        un = jnp.concatenate([lag[s][:, chan * n:chan * (n + 1)] for s in range(R)], axis=1).astype(BF16)
        lanes = slice(MXU_TILE * n, MXU_TILE * (n + 1))
        vr_scr[:, lanes] = _dot(un, winr_ref[n])
        vi_scr[:, lanes] = _dot(un, wini_ref[n])
        direct.append(_dot(un, kdir_ref[n]))

    for j in range(SSM_LANES // width):
        glob = slice(j * width, (j + 1) * width)
        ar = jnp.broadcast_to(ar_ref[:, glob], (batch, width))
        ai = jnp.broadcast_to(ai_ref[:, glob], (batch, width))
        xr = hr_ref[:, glob]
        xi = hi_ref[:, glob]
        for i in range(blocks // pair):
            prev_r, prev_i = [], []
            for s in range(pair):
                at = slice((i * pair + s) * batch, (i * pair + s + 1) * batch)
                prev_r.append(xr)
                prev_i.append(xi)
                xr, xi = ar * xr - ai * xi + vr_scr[at, glob], ar * xi + ai * xr + vi_scr[at, glob]
            blk = slice(i * pair * batch, (i + 1) * pair * batch)
            xr_scr[blk, glob] = jnp.concatenate(prev_r, axis=0).astype(BF16)
            xi_scr[blk, glob] = jnp.concatenate(prev_i, axis=0).astype(BF16)
        hr_ref[:, glob] = xr
        hi_ref[:, glob] = xi

    y4 = []
    for n in range(T):
        lanes = slice(MXU_TILE * n, MXU_TILE * (n + 1))
        y4.append(_dot(xr_scr[:, lanes], woutr_ref[n]) + _dot(xi_scr[:, lanes], wouti_ref[n]) + direct[n])
    per_lag = [jnp.concatenate([y4[n][:, chan * j:chan * (j + 1)] for n in range(T)], axis=1)
               .reshape(blocks, batch, SSM_WIDTH) for j in range(R)]
    y = jnp.stack(per_lag, axis=1).reshape(rows, SSM_WIDTH) + d_ref[...] * u
    z = _gelu_tanh(y)
    out = z * _sigmoid(_dot(z.astype(BF16), wglu_ref[...]))
    out_ref[...] = jnp.swapaxes(out.reshape(steps, batch, SSM_WIDTH), 0, 1).astype(BF16)


def _s5_mixer(u, h0_re, h0_im, maps, d_skip, w_glu, *, batch, length):
    a_re, a_im, winr, wini, woutr, wouti, direct = maps
    steps = min(length, 128)
    if steps % SSM_BLOCK or length % steps:
        raise ValueError("sequence length must be a multiple of the S5 block")
    blocks = steps // SSM_BLOCK
    pair = max(1, min(blocks, BF16_ROWS // batch))
    if (pair * batch) % BF16_ROWS or blocks % pair:
        raise ValueError("S5 state stores need whole bf16 tiles")
    width = SSM_LANES // 2 if batch <= 8 else LANES
    brow = blocks * batch
    kern = functools.partial(_s5_block_kernel, batch=batch, steps=steps, pair=pair, width=width)
    st = jax.ShapeDtypeStruct((batch, SSM_LANES), F32)
    tile = _full((SSM_TILES, MXU_TILE, MXU_TILE))
    return pl.pallas_call(
        kern,
        grid=(length // steps,),
        in_specs=[pl.BlockSpec((batch, steps, SSM_WIDTH), lambda i: (0, i, 0)),
                  _full((batch, SSM_LANES)), _full((batch, SSM_LANES)),
                  _full((1, SSM_LANES)), _full((1, SSM_LANES)),
                  tile, tile, tile, tile, tile,
                  _full((1, SSM_WIDTH)), _full((SSM_WIDTH, SSM_WIDTH))],
        out_specs=[pl.BlockSpec((batch, steps, SSM_WIDTH), lambda i: (0, i, 0)),
                   _full((batch, SSM_LANES)), _full((batch, SSM_LANES))],
        out_shape=(jax.ShapeDtypeStruct((batch, length, SSM_WIDTH), BF16), st, st),
        scratch_shapes=[pltpu.VMEM((brow, SSM_LANES), F32), pltpu.VMEM((brow, SSM_LANES), F32),
                        pltpu.VMEM((brow, SSM_LANES), BF16), pltpu.VMEM((brow, SSM_LANES), BF16)],
        compiler_params=_params("arbitrary"),
        name="s5_mixer",
    )(u, h0_re, h0_im, a_re, a_im, winr, wini, woutr, wouti, direct, d_skip, w_glu)


def _ret_kernel(q_ref, k_ref, kz_ref, v_ref, g_ref, mask_ref, xi_ref, gain_ref, s0_ref, o_ref, s_ref,
                *, bb, gamma_c):
    @pl.when(pl.program_id(1) == 0)
    def _():
        s_ref[...] = s0_ref[...]

    heads = range(RET_HEADS)
    cols = [slice(h * RET_HEAD_DIM, (h + 1) * RET_HEAD_DIM) for h in heads]

    def per_batch(b, carry):
        qb = [q_ref[b, :, c] for c in cols]
        vb = [v_ref[b, :, c] for c in cols]
        s_prev = [s_ref[b, h] for h in heads]
        scores = [(_dot_nt(qb[h], k_ref[b, :, cols[h]]) * mask_ref[h]).astype(BF16) for h in heads]
        cross = [_dot(qb[h], s_prev[h].astype(BF16)) * xi_ref[h] for h in heads]
        kv = [_dot_tn(kz_ref[b, :, cols[h]], vb[h]) for h in heads]
        inner = [_dot(scores[h], vb[h]) for h in heads]
        outs = []
        for h in heads:
            s_ref[b, h] = s_prev[h] * gamma_c[h] + kv[h]
            o = inner[h] + cross[h]
            mu = jnp.mean(o, axis=-1, keepdims=True)
            oc = o - mu
            var = jnp.mean(oc * oc, axis=-1, keepdims=True)
            on = oc * lax.rsqrt(var + EPS) * gain_ref[:, cols[h]]
            gate = g_ref[b, :, cols[h]]
            outs.append((gate * _sigmoid(gate) * on).astype(BF16))
        o_ref[b] = jnp.concatenate(outs, axis=1)
        return carry

    lax.fori_loop(0, bb, per_batch, 0, unroll=2)


def _retention(q, k, kz, v, g, s0, gn_gain):
    batch, length, _ = q.shape
    padded = pl.cdiv(length, BF16_ROWS) * BF16_ROWS
    chunk, gamma_c, mask, xi = _decay_tables(length, padded)
    if padded != length:
        pad = lambda a: jnp.pad(a, ((0, 0), (0, padded - length), (0, 0)))
        q, k, kz, v, g = pad(q), pad(k), pad(kz), pad(v), pad(g)
    bb = 8
    tok = pl.BlockSpec((bb, chunk, RET_WIDTH), lambda i, c: (i, c, 0))
    state = pl.BlockSpec((bb, RET_HEADS, RET_HEAD_DIM, RET_HEAD_DIM), lambda i, c: (i, 0, 0, 0))
    out, s_new = pl.pallas_call(
        functools.partial(_ret_kernel, bb=bb, gamma_c=gamma_c),
        grid=(batch // bb, padded // chunk),
        in_specs=[tok, tok, tok, tok, tok,
                  _full((RET_HEADS, chunk, chunk)), _full((RET_HEADS, chunk, RET_HEAD_DIM)),
                  _full((1, RET_WIDTH)), state],
        out_specs=[tok, state],
        out_shape=(jax.ShapeDtypeStruct((batch, padded, RET_WIDTH), BF16),
                   jax.ShapeDtypeStruct(s0.shape, F32)),
        compiler_params=_params("parallel", "arbitrary"),
        name="retention",
    )(q, k, kz, v, g, mask, xi, gn_gain, s0)
    return out[:, :length], s_new


def _mixer_out(x_ref, ssm_ref, ret_ref, wout_ref):
    return (x_ref[...] + _dot(ssm_ref[...], wout_ref[:SSM_WIDTH, :])
            + _dot(ret_ref[...], wout_ref[SSM_WIDTH:, :]))


def _query(x1, g_ref, wq_ref):
    return (_dot(_rmsnorm(x1, g_ref[...]).astype(BF16), wq_ref[...]) * MEM_SCALE).astype(BF16)


def _outq_kernel(x_ref, ssm_ref, ret_ref, wout_ref, g_ref, wq_ref, x1_ref, q_ref):
    x1 = _mixer_out(x_ref, ssm_ref, ret_ref, wout_ref)
    x1_ref[...] = x1
    q_ref[...] = _query(x1, g_ref, wq_ref)


def _out_and_query(x, ssm, ret, w_out, g_xattn, w_mq):
    t = x.shape[0]
    tm = min(t, 512)
    row = lambda w: pl.BlockSpec((tm, w), lambda i: (i, 0))
    return pl.pallas_call(
        _outq_kernel,
        grid=(t // tm,),
        in_specs=[row(D_MODEL), row(SSM_WIDTH), row(RET_WIDTH), _full((D_MODEL, D_MODEL)),
                  _full((1, D_MODEL)), _full((D_MODEL, D_MODEL))],
        out_specs=[row(D_MODEL), row(D_MODEL)],
        out_shape=(jax.ShapeDtypeStruct((t, D_MODEL), F32), jax.ShapeDtypeStruct((t, D_MODEL), BF16)),
        compiler_params=_params("parallel"),
        name="out_proj_query",
    )(x, ssm, ret, w_out, g_xattn, w_mq)


def _cache_row_order(x):
    halves = MEM_HEAD_DIM // LANES
    tiles = [x[:, (h * halves + d) * LANES:(h * halves + d + 1) * LANES]
             for d in range(halves) for h in range(MEM_HEADS)]
    return jnp.swapaxes(jnp.stack(tiles, axis=0), 0, 1).reshape(x.shape[0] * CACHE_SUB, LANES)


def _memkv_kernel(m_ref, g_ref, wk_ref, wv_ref, k_ref, v_ref, kb_ref, vb_ref):
    m = _rmsnorm(m_ref[0], g_ref[...]).astype(BF16)
    k = _dot(m, wk_ref[...])
    v = _dot(m, wv_ref[...])
    k_ref[0] = _cache_row_order(k)
    v_ref[0] = _cache_row_order(v)
    kb_ref[0] = k.astype(BF16)
    vb_ref[0] = v.astype(BF16)


def _memory_kv(mem, g_mem, w_mk, w_mv):
    batch = mem.shape[0]
    tok = pl.BlockSpec((1, MEM_LEN, D_MODEL), lambda i: (i, 0, 0))
    cache = pl.BlockSpec((1, CACHE_ROWS, LANES), lambda i: (i, 0, 0))
    f = jax.ShapeDtypeStruct((batch, CACHE_ROWS, LANES), F32)
    h = jax.ShapeDtypeStruct((batch, MEM_LEN, D_MODEL), BF16)
    return pl.pallas_call(
        _memkv_kernel,
        grid=(batch,),
        in_specs=[tok, _full((1, D_MODEL)), _full((D_MODEL, D_MODEL)), _full((D_MODEL, D_MODEL))],
        out_specs=[cache, cache, tok, tok],
        out_shape=(f, f, h, h),
        compiler_params=_params("parallel"),
        name="memory_kv",
    )(mem, g_mem, w_mk, w_mv)


def _outattn_kernel(x_ref, ssm_ref, ret_ref, wout_ref, g_ref, wq_ref, k_ref, v_ref, x1_ref, o_ref):
    x1 = _mixer_out(x_ref, ssm_ref, ret_ref, wout_ref)
    x1_ref[...] = x1
    q = _query(x1, g_ref, wq_ref)
    cols = [slice(h * MEM_HEAD_DIM, (h + 1) * MEM_HEAD_DIM) for h in range(MEM_HEADS)]
    scores = [_dot_nt(q[:, c], k_ref[0, :, c]) for c in cols]
    probs = []
    for s in scores:
        e = jnp.exp(s - jnp.max(s, axis=-1, keepdims=True))
        probs.append((e / jnp.sum(e, axis=-1, keepdims=True)).astype(BF16))
    o_ref[...] = jnp.concatenate([_dot(p, v_ref[0, :, c]).astype(BF16) for p, c in zip(probs, cols)], axis=1)


def _out_and_attention(x, ssm, ret, w_out, g_xattn, w_mq, mk, mv, *, length):
    t = x.shape[0]
    tm = min(length, 1024)
    per_seq = length // tm
    row = lambda w: pl.BlockSpec((tm, w), lambda i: (i, 0))
    mem = pl.BlockSpec((1, MEM_LEN, D_MODEL), lambda i: (i // per_seq, 0, 0))
    return pl.pallas_call(
        _outattn_kernel,
        grid=(t // tm,),
        in_specs=[row(D_MODEL), row(SSM_WIDTH), row(RET_WIDTH), _full((D_MODEL, D_MODEL)),
                  _full((1, D_MODEL)), _full((D_MODEL, D_MODEL)), mem, mem],
        out_specs=[row(D_MODEL), row(D_MODEL)],
        out_shape=(jax.ShapeDtypeStruct((t, D_MODEL), F32), jax.ShapeDtypeStruct((t, D_MODEL), BF16)),
        compiler_params=_params("parallel"),
        name="out_proj_attention",
    )(x, ssm, ret, w_out, g_xattn, w_mq, mk, mv)


def _cached_attention(q_ref, k_ref, v_ref, o_ref, *, bb, length, between=None):
    hit = (lax.broadcasted_iota(jnp.int32, (CACHE_SUB, CACHE_ROWS), 0)
           == lax.broadcasted_iota(jnp.int32, (CACHE_SUB, CACHE_ROWS), 1) % CACHE_SUB)
    first_half = lax.broadcasted_iota(jnp.int32, (length, LANES), 1) % CACHE_SUB < MEM_HEADS
    tiles = CACHE_ROWS // LANES

    def class_reduce(x, op):
        shift = CACHE_SUB
        while shift < LANES:
            x = op(x, pltpu.roll(x, shift, axis=1))
            shift *= 2
        return x

    def softmax_rows(r):
        z = jnp.concatenate(
            [jnp.sum(jnp.where(hit, r[CACHE_SUB * t:CACHE_SUB * (t + 1)], 0.0), axis=0, keepdims=True)
             for t in range(length)], axis=0)
        parts = []
        for i in range(tiles):
            zi = z[:, LANES * i:LANES * (i + 1)]
            parts.append(zi + pltpu.roll(zi, LANES - MEM_HEADS, axis=1))
        mx = parts[0]
        for pi in parts[1:]:
            mx = jnp.maximum(mx, pi)
        mx = class_reduce(mx, jnp.maximum)
        es = [jnp.exp(pi - mx) for pi in parts]
        tot = es[0]
        for ei in es[1:]:
            tot = tot + ei
        tot = class_reduce(tot, jnp.add)
        ps = []
        for ei in es:
            pi = ei / tot
            ps.append(jnp.where(first_half, pi, pltpu.roll(pi, MEM_HEADS, axis=1)))
        p = jnp.concatenate(ps, axis=1)
        return jnp.concatenate(
            [jnp.where(hit, jnp.broadcast_to(p[t:t + 1], (CACHE_SUB, CACHE_ROWS)), 0.0)
             for t in range(length)], axis=0).astype(BF16)

    scores = [_dot_nt(q_ref[b], k_ref[b].astype(BF16)) for b in range(bb)]
    if between is not None:
        between()
    probs = [softmax_rows(r) for r in scores]
    for b in range(bb):
        o_ref[b] = _dot(probs[b], v_ref[b].astype(BF16)).astype(BF16)


def _to_cache_rows(a, lead):
    halves = MEM_HEAD_DIM // LANES
    batch = a.size // (lead * D_MODEL)
    a = a.reshape(batch, lead, MEM_HEADS, halves, LANES).transpose(0, 1, 3, 2, 4)
    return a.reshape(batch, lead * CACHE_SUB, LANES)


def _from_cache_rows(a, lead):
    halves = MEM_HEAD_DIM // LANES
    batch = a.shape[0]
    a = a.reshape(batch, lead, halves, MEM_HEADS, LANES).transpose(0, 1, 3, 2, 4)
    return a.reshape(batch, lead, D_MODEL)


MLP_CHUNK = 1024


def _post_kernel(x1_ref, o_ref, wo_ref, gm_ref, wup_ref, wdn_ref, gf_ref, *rest, side_bb, side_len):
    if side_bb:
        qs_ref, ck_ref, cv_ref, y_ref, os_ref = rest
    else:
        (y_ref,) = rest
    x2 = x1_ref[...] + _dot(o_ref[...], wo_ref[...])
    result = []

    def mlp():
        h = _rmsnorm(x2, gm_ref[...]).astype(BF16)
        chunks = [slice(c, c + MLP_CHUNK) for c in range(0, D_FF, MLP_CHUNK)]
        acc = x2
        up = _dot(h, wup_ref[:, chunks[0]])
        for c, cols in enumerate(chunks):
            nxt = _dot(h, wup_ref[:, chunks[c + 1]]) if c + 1 < len(chunks) else None
            a = jnp.maximum(up, 0.0)
            acc = acc + _dot((a * a).astype(BF16), wdn_ref[cols, :])
            up = nxt
        result.append(acc)

    if side_bb:
        _cached_attention(qs_ref, ck_ref, cv_ref, os_ref, bb=side_bb, length=side_len, between=mlp)
    else:
        mlp()
    y_ref[...] = _rmsnorm(result[0], gf_ref[...])


def _attn_out_mlp(x1, o, w_mo, g_mlp, w_up, w_down, g_final, side=None):
    t = x1.shape[0]
    tm = min(t, 512)
    steps = t // tm
    row = pl.BlockSpec((tm, D_MODEL), lambda i: (i, 0))
    in_specs = [row, row, _full((D_MODEL, D_MODEL)), _full((1, D_MODEL)),
                _full((D_MODEL, D_FF)), _full((D_FF, D_MODEL)), _full((1, D_MODEL))]
    out_specs = [row]
    out_shape = [jax.ShapeDtypeStruct((t, D_MODEL), F32)]
    args = [x1, o, w_mo, g_mlp, w_up, w_down, g_final]
    side_bb = side_len = 0
    if side is not None:
        q_rows = side[0]
        side_bb, rem = divmod(q_rows.shape[0], steps)
        if rem or not side_bb:
            raise ValueError("side attention sequences must spread evenly over the grid steps")
        side_len = q_rows.shape[1] // CACHE_SUB
        blk = lambda r: pl.BlockSpec((side_bb, r, LANES), lambda i: (i, 0, 0))
        in_specs += [blk(q_rows.shape[1]), blk(CACHE_ROWS), blk(CACHE_ROWS)]
        out_specs.append(blk(q_rows.shape[1]))
        out_shape.append(jax.ShapeDtypeStruct(q_rows.shape, BF16))
        args += list(side)
    outs = pl.pallas_call(
        functools.partial(_post_kernel, side_bb=side_bb, side_len=side_len),
        grid=(steps,),
        in_specs=in_specs,
        out_specs=out_specs,
        out_shape=out_shape,
        compiler_params=_params("parallel"),
        name="attn_out_mlp",
    )(*args)
    return outs if side is not None else outs[0]


def _mixers(proj, batch, length, s5_re, s5_im, ret_s, p):
    tokens = batch * length
    u, q, k, kz, v, g = proj
    ssm, s5_re_new, s5_im_new = _s5_mixer(
        u.reshape(batch, length, SSM_WIDTH), s5_re.reshape(batch, SSM_LANES), s5_im.reshape(batch, SSM_LANES),
        p["s5_maps"], p["d_skip"], p["w_glu"], batch=batch, length=length)
    tok3 = lambda a: a.reshape(batch, length, RET_WIDTH)
    ret, ret_new = _retention(tok3(q), tok3(k), tok3(kz), tok3(v), tok3(g), ret_s, p["ret_gn"])
    states = (s5_re_new.reshape(1, batch, SSM_GROUPS, SSM_STATE),
              s5_im_new.reshape(1, batch, SSM_GROUPS, SSM_STATE), ret_new[None])
    return ssm.reshape(tokens, SSM_WIDTH), ret.reshape(tokens, RET_WIDTH), states


def kernel(x_prompt, x_sample, mem_prompt, state_s5_re, state_s5_im, state_ret, cache_mem_k, cache_mem_v, g_mix, w_in, lam_re, lam_im, log_dt, b_re, b_im, c_re, c_im, d_skip, w_glu, ret_gn, w_out, g_xattn, g_mem, w_mq, w_mk, w_mv, w_mo, g_mlp, w_up, w_down, g_final):
    if g_mix.shape[0] != 1:
        raise ValueError("this kernel implements the single-layer configuration")
    bp, lp, _ = x_prompt.shape
    bs, ls, _ = x_sample.shape
    vec = lambda a: a.reshape(1, -1).astype(F32)
    xp = x_prompt.reshape(bp * lp, D_MODEL)
    xs = x_sample.reshape(bs * ls, D_MODEL)
    proj_p, (w_in_b, w_up_b, w_down_b, w_out_b, w_mq_b, w_mo_b, w_mk_b, w_mv_b, w_glu_b) = _project(
        xp, vec(g_mix[0]), w_in[0], length=lp, pos0=0.0,
        casts=(w_up[0], w_down[0], w_out[0], w_mq[0], w_mo[0], w_mk[0], w_mv[0], w_glu[0]))
    proj_s, _ = _project(xs, vec(g_mix[0]), w_in_b, length=ls, pos0=float(PAST_LEN))
    p = dict(
        s5_maps=_s5_block_maps(lam_re[0], lam_im[0], log_dt[0], b_re[0], b_im[0], c_re[0], c_im[0]),
        d_skip=vec(d_skip[0]), w_glu=w_glu_b, ret_gn=vec(ret_gn[0]))
    g_x = vec(g_xattn[0])
    mlp = (w_mo_b, vec(g_mlp[0]), w_up_b, w_down_b, vec(g_final))

    ssm_s, ret_s, states_s = _mixers(proj_s, bs, ls, state_s5_re[0], state_s5_im[0], state_ret[0], p)
    x1_s, q_s = _out_and_query(xs, ssm_s, ret_s, w_out_b, g_x, w_mq_b)
    if (ls * CACHE_SUB) % BF16_ROWS:
        raise ValueError("cached attention needs whole bf16 tiles of query rows")
    side = (_to_cache_rows(q_s, ls), _to_cache_rows(cache_mem_k, MEM_LEN), _to_cache_rows(cache_mem_v, MEM_LEN))

    mk_rows, mv_rows, mk_p, mv_p = _memory_kv(mem_prompt, vec(g_mem[0]), w_mk_b, w_mv_b)
    zs = jnp.zeros((bp, SSM_GROUPS, SSM_STATE), F32)
    zr = jnp.zeros((bp, RET_HEADS, RET_HEAD_DIM, RET_HEAD_DIM), F32)
    ssm_p, ret_p, states_p = _mixers(proj_p, bp, lp, zs, zs, zr, p)
    x1_p, o_p = _out_and_attention(xp, ssm_p, ret_p, w_out_b, g_x, w_mq_b, mk_p, mv_p, length=lp)
    y_p, o_s = _attn_out_mlp(x1_p, o_p, *mlp, side=side)
    y_s = _attn_out_mlp(x1_s, _from_cache_rows(o_s, ls).reshape(bs * ls, D_MODEL), *mlp)

    kv5 = lambda a: _from_cache_rows(a, MEM_LEN).reshape(1, bp, MEM_LEN, MEM_HEADS, MEM_HEAD_DIM)
    return (y_p.reshape(bp, lp, D_MODEL), y_s.reshape(bs, ls, D_MODEL), *states_p, kv5(mk_rows), kv5(mv_rows),
            *states_s)
```

```python
import functools
import math

import numpy as np
import jax
import jax.numpy as jnp
from jax import lax
from jax.experimental import pallas as pl
from jax.experimental.pallas import tpu as pltpu

F32 = jnp.float32
BF16 = jnp.bfloat16

D_MODEL = 1024
SSM_WIDTH = 512
SSM_GROUP = 16
SSM_GROUPS = 32
SSM_STATE = 64
SSM_LANES = SSM_GROUPS * SSM_STATE
RET_WIDTH = 512
RET_HEADS = 4
RET_HEAD_DIM = 128
RET_CHUNK = 128
ROPE_BASE = 10000.0
MEM_LEN = 256
MEM_HEADS = 4
MEM_HEAD_DIM = 256
MEM_SCALE = MEM_HEAD_DIM ** -0.5
D_FF = 4096
PROJ_WIDTH = SSM_WIDTH + 4 * RET_WIDTH
EPS = 1e-6
PAST_LEN = 16384

LANES = 128
MXU_TILE = 256
BF16_ROWS = 16
VMEM_LIMIT = 56 * 1024 * 1024

CACHE_ROWS = MEM_LEN * MEM_HEADS * MEM_HEAD_DIM // LANES
CACHE_SUB = MEM_HEADS * MEM_HEAD_DIM // LANES

assert math.frexp(MEM_SCALE)[0] == 0.5


def _dot(a, b):
    return jnp.dot(a, b, preferred_element_type=F32)


def _dot_nt(a, b):
    return lax.dot_general(a, b, (((1,), (1,)), ((), ())), preferred_element_type=F32)


def _dot_tn(a, b):
    return lax.dot_general(a, b, (((0,), (0,)), ((), ())), preferred_element_type=F32)


def _rmsnorm(x, g):
    return x * lax.rsqrt(jnp.mean(x * x, axis=-1, keepdims=True) + EPS) * g


def _sigmoid(x):
    return 1.0 / (1.0 + jnp.exp(-x))


def _gelu_tanh(x):
    c = math.sqrt(2.0 / math.pi)
    return x * (0.5 * (1.0 + jnp.tanh(c * (x + 0.044715 * (x * x * x)))))


def _params(*sem):
    return pltpu.CompilerParams(dimension_semantics=sem, vmem_limit_bytes=VMEM_LIMIT)


def _full(shape):
    return pl.BlockSpec(shape, lambda *_: (0,) * len(shape))


SSM_BLOCK = 4
SSM_TILE_GROUPS = MXU_TILE // SSM_STATE
SSM_TILES = SSM_GROUPS // SSM_TILE_GROUPS
SSM_TILE_CHANNELS = SSM_TILE_GROUPS * SSM_GROUP
GROUP_SHIFT = SSM_GROUP.bit_length() - 1
STATE_SHIFT = SSM_STATE.bit_length() - 1

assert SSM_BLOCK * SSM_TILE_CHANNELS == MXU_TILE and SSM_TILE_CHANNELS == SSM_STATE


def _s5_prep_kernel(lr_ref, li_ref, ldt_ref, br_ref, bi_ref, cr_ref, ci_ref,
                    ar_ref, ai_ref, winr_ref, wini_ref, woutr_ref, wouti_ref, kdir_ref):
    lr = lr_ref[...]
    li = li_ref[...]
    dt = jnp.exp(ldt_ref[...])
    mag = jnp.exp(lr * dt)
    ab_re = mag * jnp.cos(li * dt)
    ab_im = mag * jnp.sin(li * dt)
    den = lr * lr + li * li
    f_re = ((ab_re - 1.0) * lr + ab_im * li) / den
    f_im = (ab_im * lr - (ab_re - 1.0) * li) / den
    bb_re = f_re * br_ref[...] - f_im * bi_ref[...]
    bb_im = f_re * bi_ref[...] + f_im * br_ref[...]
    cr = cr_ref[...]
    ci = ci_ref[...]
    pows = [(jnp.ones_like(ab_re), jnp.zeros_like(ab_re))]
    for _ in range(SSM_BLOCK):
        pr, pi = pows[-1]
        pows.append((pr * ab_re - pi * ab_im, pr * ab_im + pi * ab_re))
    ar_ref[...], ai_ref[...] = pows[SSM_BLOCK]

    R, T, TG = SSM_BLOCK, SSM_TILES, SSM_TILE_GROUPS
    rows = SSM_GROUPS * SSM_GROUP
    side = SSM_TILE_CHANNELS
    exact = dict(precision=lax.Precision.HIGHEST, preferred_element_type=F32)
    iota = lambda shape, d: lax.broadcasted_iota(jnp.int32, shape, d)
    rep4 = lambda a, axis: jnp.concatenate([a] * TG, axis=axis)
    tiles = lambda a: a.reshape(T, side, a.shape[-1])
    group_of = lambda i: (i & (side - 1)) >> GROUP_SHIFT

    own_in = group_of(iota((rows, MXU_TILE), 0)) == iota((rows, MXU_TILE), 1) >> STATE_SHIFT
    for s in range(R):
        pr, pi = pows[R - 1 - s]
        for ref, w in ((winr_ref, pr * bb_re - pi * bb_im), (wini_ref, pr * bb_im + pi * bb_re)):
            ref[:, side * s:side * (s + 1), :] = tiles(jnp.where(own_in, rep4(w, 1), 0.0)).astype(BF16)

    eye = (iota((rows, rows), 0) == iota((rows, rows), 1)).astype(F32)
    own_out = iota((MXU_TILE, rows), 0) >> STATE_SHIFT == group_of(iota((MXU_TILE, rows), 1))
    for ref, real in ((woutr_ref, True), (wouti_ref, False)):
        per_lag = []
        for j in range(R):
            pr, pi = pows[j + 1]
            w = cr * pr - ci * pi if real else -(cr * pi + ci * pr)
            wt = lax.dot_general(w, eye, (((0,), (0,)), ((), ())), **exact)
            per_lag.append(jnp.where(own_out, rep4(wt, 0), 0.0))
        for n in range(T):
            ref[n] = jnp.concatenate([m[:, side * n:side * (n + 1)] for m in per_lag], axis=1).astype(BF16)

    grouped = lambda a: a.reshape(SSM_GROUPS, SSM_GROUP, SSM_STATE)
    contract = lambda a, b: lax.dot_general(a, b, (((2,), (2,)), ((0,), (0,))), **exact)
    own_dir = group_of(iota((rows, side), 0)) == iota((rows, side), 1) >> GROUP_SHIFT
    by_lag = []
    for d in range(R):
        pr, pi = pows[d]
        er = grouped(pr * bb_re - pi * bb_im)
        ei = grouped(pr * bb_im + pi * bb_re)
        kt = (contract(er, grouped(cr)) - contract(ei, grouped(ci))).reshape(rows, SSM_GROUP)
        by_lag.append(tiles(jnp.where(own_dir, rep4(kt, 1), 0.0)))
    nothing = jnp.zeros_like(by_lag[0])
    for s in range(R):
        kdir_ref[:, side * s:side * (s + 1), :] = jnp.concatenate(
            [by_lag[j - s] if j >= s else nothing for j in range(R)], axis=2).astype(BF16)


def _s5_block_maps(lam_re, lam_im, log_dt, b_re, b_im, c_re, c_im):
    rows = SSM_GROUPS * SSM_GROUP
    rep = lambda a: jnp.repeat(a, SSM_GROUP, axis=0)
    ldt = jnp.broadcast_to(log_dt[:, None], (SSM_GROUPS, SSM_STATE))
    bt = lambda b: b.transpose(0, 2, 1).reshape(rows, SSM_STATE)
    ct = lambda c: c.reshape(rows, SSM_STATE)
    one = jax.ShapeDtypeStruct((rows, SSM_STATE), F32)
    tile = jax.ShapeDtypeStruct((SSM_TILES, MXU_TILE, MXU_TILE), BF16)
    ar, ai, winr, wini, woutr, wouti, direct = pl.pallas_call(
        _s5_prep_kernel,
        out_shape=(one, one, tile, tile, tile, tile, tile),
        compiler_params=pltpu.CompilerParams(vmem_limit_bytes=VMEM_LIMIT),
        name="s5_block_maps",
    )(rep(lam_re), rep(lam_im), rep(ldt), bt(b_re), bt(b_im), ct(c_re), ct(c_im))
    a_re = ar[::SSM_GROUP].reshape(1, SSM_LANES)
    a_im = ai[::SSM_GROUP].reshape(1, SSM_LANES)
    return a_re, a_im, winr, wini, woutr, wouti, direct


def _retention_chunk(length):
    return RET_CHUNK if length % RET_CHUNK == 0 else length


def _log_gamma():
    return np.log(1.0 - 2.0 ** (-5.0 - np.arange(RET_HEADS, dtype=np.float64)))


def _rope_tables(length, pos0):
    half = RET_HEAD_DIM // 2
    inv = ROPE_BASE ** (-np.arange(half, dtype=np.float64) / half)
    ang = (pos0 + np.arange(length, dtype=np.float64))[:, None] * inv[None, :]
    return (np.concatenate([np.cos(ang), np.cos(ang)], axis=1),
            np.concatenate([-np.sin(ang), np.sin(ang)], axis=1))


def _zeta_rows(length):
    chunk = _retention_chunk(length)
    idx = np.arange(chunk, dtype=np.float64)
    zeta = np.exp((chunk - 1.0 - idx)[None, :] * _log_gamma()[:, None])
    return np.concatenate([np.broadcast_to(z[:, None], (chunk, RET_HEAD_DIM)) for z in zeta], axis=1)


def _decay_tables(length, padded):
    chunk = _retention_chunk(length)
    cpad = chunk if padded == length else padded
    lg = _log_gamma()
    idx = np.arange(chunk, dtype=np.float64)
    diff = idx[:, None] - idx[None, :]
    mask = np.where(diff[None] >= 0, np.exp(np.maximum(diff, 0.0)[None] * lg[:, None, None]), 0.0)
    xi = np.exp((idx + 1.0)[None, :] * lg[:, None])
    gamma_c = tuple(float(v) for v in np.exp(chunk * lg))
    mask = np.pad(mask, ((0, 0), (0, cpad - chunk), (0, cpad - chunk)))
    xi = np.broadcast_to(np.pad(xi, ((0, 0), (0, cpad - chunk)))[:, :, None], (RET_HEADS, cpad, RET_HEAD_DIM))
    f = lambda a: jnp.asarray(np.ascontiguousarray(a), dtype=F32)
    return cpad, gamma_c, f(mask), f(xi)


def _proj_kernel(x_ref, g_ref, w_ref, cc_ref, ss_ref, zeta_ref, *rest, n_cast):
    casts_in, rest = rest[:n_cast], rest[n_cast:]
    u_ref, q_ref, k_ref, kz_ref, v_ref, gate_ref = rest[:6]
    w = w_ref[...]
    if n_cast:
        w = w.astype(BF16)
        for src, dst in zip(casts_in, rest[6:]):
            dst[...] = src[...].astype(BF16)
    h = _rmsnorm(x_ref[...], g_ref[...]).astype(BF16)
    proj = _dot(h, w)
    u_ref[...] = proj[:, :SSM_WIDTH]
    v_ref[...] = proj[:, SSM_WIDTH + 2 * RET_WIDTH:SSM_WIDTH + 3 * RET_WIDTH].astype(BF16)
    gate_ref[...] = proj[:, SSM_WIDTH + 3 * RET_WIDTH:]
    cc = cc_ref[...]
    ss = ss_ref[...]
    rope = lambda a: a * cc + pltpu.roll(a, RET_HEAD_DIM // 2, axis=1) * ss
    for hd in range(RET_HEADS):
        cols = slice(hd * RET_HEAD_DIM, (hd + 1) * RET_HEAD_DIM)
        q = rope(proj[:, SSM_WIDTH + hd * RET_HEAD_DIM:SSM_WIDTH + (hd + 1) * RET_HEAD_DIM])
        k = rope(proj[:, SSM_WIDTH + RET_WIDTH + hd * RET_HEAD_DIM:SSM_WIDTH + RET_WIDTH + (hd + 1) * RET_HEAD_DIM])
        q_ref[:, cols] = (q * (RET_HEAD_DIM ** -0.5)).astype(BF16)
        k_ref[:, cols] = k.astype(BF16)
        kz_ref[:, cols] = (k * zeta_ref[:, cols]).astype(BF16)


def _project(x, g_mix, w_in, *, length, pos0, casts=()):
    t = x.shape[0]
    tm = min(t, 1024)
    steps = t // tm
    chunk = _retention_chunk(length)
    casts = (w_in, *casts) if casts else ()
    if tm % chunk or (length % tm and tm % length):
        raise ValueError("token tiles must hold whole retention chunks of whole or repeated sequences")
    cc, ss = _rope_tables(length, pos0)
    if length < tm:
        cc, ss = np.tile(cc, (tm // length, 1)), np.tile(ss, (tm // length, 1))
    tab_blocks = cc.shape[0] // tm
    zeta = np.tile(_zeta_rows(length), (tm // chunk, 1))
    const = lambda a: jnp.asarray(np.ascontiguousarray(a), dtype=F32)
    row = lambda w: pl.BlockSpec((tm, w), lambda i: (i, 0))
    tab = pl.BlockSpec((tm, RET_HEAD_DIM), lambda i: (i % tab_blocks, 0))
    f = jax.ShapeDtypeStruct((t, 512), F32)
    h = jax.ShapeDtypeStruct((t, 512), BF16)
    in_specs = [row(D_MODEL), _full((1, D_MODEL)), _full((D_MODEL, PROJ_WIDTH)), tab, tab, _full((tm, RET_WIDTH))]
    out_specs = [row(512)] * 6
    out_shape = [f, h, h, h, h, f]
    if casts:
        for c in casts:
            rows, rem = divmod(c.shape[0], steps)
            if rem or rows % BF16_ROWS:
                raise ValueError("side-cast weights must split into whole bf16 tiles per grid step")
            blk = pl.BlockSpec((rows, c.shape[1]), lambda i: (i, 0))
            in_specs.append(blk)
            out_specs.append(blk)
            out_shape.append(jax.ShapeDtypeStruct(c.shape, BF16))
    outs = pl.pallas_call(
        functools.partial(_proj_kernel, n_cast=len(casts)),
        grid=(steps,),
        in_specs=in_specs,
        out_specs=out_specs,
        out_shape=out_shape,
        compiler_params=_params("arbitrary" if casts else "parallel"),
        name="in_proj",
    )(x, g_mix, w_in, const(cc), const(ss), const(zeta), *casts)
    return outs[:6], outs[6:]


def _s5_block_kernel(u_ref, h0r_ref, h0i_ref, ar_ref, ai_ref, winr_ref, wini_ref, woutr_ref, wouti_ref,
                     kdir_ref, d_ref, wglu_ref, out_ref, hr_ref, hi_ref, vr_scr, vi_scr, xr_scr, xi_scr,
                     *, batch, steps, pair, width):
    R, T = SSM_BLOCK, SSM_TILES
    chan = SSM_TILE_CHANNELS
    blocks = steps // R
    rows, brow = steps * batch, blocks * batch

    @pl.when(pl.program_id(0) == 0)
    def _():
        hr_ref[...] = h0r_ref[...]
        hi_ref[...] = h0i_ref[...]

    u = jnp.swapaxes(u_ref[...], 0, 1).reshape(rows, SSM_WIDTH)
    u4 = u.reshape(blocks, R, batch, SSM_WIDTH)
    lag = [u4[:, s].reshape(brow, SSM_WIDTH) for s in range(R)]
    direct = []
    for n in range(T):
        un = jnp.concatenate([lag[s][:, chan * n:chan * (n + 1)] for s in range(R)], axis=1).astype(BF16)
        lanes = slice(MXU_TILE * n, MXU_TILE * (n + 1))
        vr_scr[:, lanes] = _dot(un, winr_ref[n])
        vi_scr[:, lanes] = _dot(un, wini_ref[n])
        direct.append(_dot(un, kdir_ref[n]))

    for j in range(SSM_LANES // width):
        glob = slice(j * width, (j + 1) * width)
        ar = jnp.broadcast_to(ar_ref[:, glob], (batch, width))
        ai = jnp.broadcast_to(ai_ref[:, glob], (batch, width))
        xr = hr_ref[:, glob]
        xi = hi_ref[:, glob]
        for i in range(blocks // pair):
            prev_r, prev_i = [], []
            for s in range(pair):
                at = slice((i * pair + s) * batch, (i * pair + s + 1) * batch)
                prev_r.append(xr)
                prev_i.append(xi)
                xr, xi = ar * xr - ai * xi + vr_scr[at, glob], ar * xi + ai * xr + vi_scr[at, glob]
            blk = slice(i * pair * batch, (i + 1) * pair * batch)
            xr_scr[blk, glob] = jnp.concatenate(prev_r, axis=0).astype(BF16)
            xi_scr[blk, glob] = jnp.concatenate(prev_i, axis=0).astype(BF16)
        hr_ref[:, glob] = xr
        hi_ref[:, glob] = xi

    y4 = []
    for n in range(T):
        lanes = slice(MXU_TILE * n, MXU_TILE * (n + 1))
        y4.append(_dot(xr_scr[:, lanes], woutr_ref[n]) + _dot(xi_scr[:, lanes], wouti_ref[n]) + direct[n])
    per_lag = [jnp.concatenate([y4[n][:, chan * j:chan * (j + 1)] for n in range(T)], axis=1)
               .reshape(blocks, batch, SSM_WIDTH) for j in range(R)]
    y = jnp.stack(per_lag, axis=1).reshape(rows, SSM_WIDTH) + d_ref[...] * u
    z = _gelu_tanh(y)
    out = z * _sigmoid(_dot(z.astype(BF16), wglu_ref[...]))
    out_ref[...] = jnp.swapaxes(out.reshape(steps, batch, SSM_WIDTH), 0, 1).astype(BF16)


def _s5_mixer(u, h0_re, h0_im, maps, d_skip, w_glu, *, batch, length):
    a_re, a_im, winr, wini, woutr, wouti, direct = maps
    steps = min(length, 128)
    if steps % SSM_BLOCK or length % steps:
        raise ValueError("sequence length must be a multiple of the S5 block")
    blocks = steps // SSM_BLOCK
    pair = max(1, min(blocks, BF16_ROWS // batch))
    if (pair * batch) % BF16_ROWS or blocks % pair:
        raise ValueError("S5 state stores need whole bf16 tiles")
    width = SSM_LANES // 2 if batch <= 8 else LANES
    brow = blocks * batch
    kern = functools.partial(_s5_block_kernel, batch=batch, steps=steps, pair=pair, width=width)
    st = jax.ShapeDtypeStruct((batch, SSM_LANES), F32)
    tile = _full((SSM_TILES, MXU_TILE, MXU_TILE))
    return pl.pallas_call(
        kern,
        grid=(length // steps,),
        in_specs=[pl.BlockSpec((batch, steps, SSM_WIDTH), lambda i: (0, i, 0)),
                  _full((batch, SSM_LANES)), _full((batch, SSM_LANES)),
                  _full((1, SSM_LANES)), _full((1, SSM_LANES)),
                  tile, tile, tile, tile, tile,
                  _full((1, SSM_WIDTH)), _full((SSM_WIDTH, SSM_WIDTH))],
        out_specs=[pl.BlockSpec((batch, steps, SSM_WIDTH), lambda i: (0, i, 0)),
                   _full((batch, SSM_LANES)), _full((batch, SSM_LANES))],
        out_shape=(jax.ShapeDtypeStruct((batch, length, SSM_WIDTH), BF16), st, st),
        scratch_shapes=[pltpu.VMEM((brow, SSM_LANES), F32), pltpu.VMEM((brow, SSM_LANES), F32),
                        pltpu.VMEM((brow, SSM_LANES), BF16), pltpu.VMEM((brow, SSM_LANES), BF16)],
        compiler_params=_params("arbitrary"),
        name="s5_mixer",
    )(u, h0_re, h0_im, a_re, a_im, winr, wini, woutr, wouti, direct, d_skip, w_glu)


def _ret_kernel(q_ref, k_ref, kz_ref, v_ref, g_ref, mask_ref, xi_ref, gain_ref, s0_ref, o_ref, s_ref,
                *, bb, gamma_c, unroll):
    @pl.when(pl.program_id(1) == 0)
    def _():
        s_ref[...] = s0_ref[...]

    heads = range(RET_HEADS)
    cols = [slice(h * RET_HEAD_DIM, (h + 1) * RET_HEAD_DIM) for h in heads]

    def per_batch(b, carry):
        qb = [q_ref[b, :, c] for c in cols]
        vb = [v_ref[b, :, c] for c in cols]
        s_prev = [s_ref[b, h] for h in heads]
        scores = [(_dot_nt(qb[h], k_ref[b, :, cols[h]]) * mask_ref[h]).astype(BF16) for h in heads]
        cross = [_dot(qb[h], s_prev[h].astype(BF16)) * xi_ref[h] for h in heads]
        kv = [_dot_tn(kz_ref[b, :, cols[h]], vb[h]) for h in heads]
        inner = [_dot(scores[h], vb[h]) for h in heads]
        outs = []
        for h in heads:
            s_ref[b, h] = s_prev[h] * gamma_c[h] + kv[h]
            o = inner[h] + cross[h]
            mu = jnp.mean(o, axis=-1, keepdims=True)
            oc = o - mu
            var = jnp.mean(oc * oc, axis=-1, keepdims=True)
            on = oc * lax.rsqrt(var + EPS) * gain_ref[:, cols[h]]
            gate = g_ref[b, :, cols[h]]
            outs.append((gate * _sigmoid(gate) * on).astype(BF16))
        o_ref[b] = jnp.concatenate(outs, axis=1)
        return carry

    lax.fori_loop(0, bb, per_batch, 0, unroll=unroll)


def _retention(q, k, kz, v, g, s0, gn_gain):
    batch, length, _ = q.shape
    padded = pl.cdiv(length, BF16_ROWS) * BF16_ROWS
    chunk, gamma_c, mask, xi = _decay_tables(length, padded)
    if padded != length:
        pad = lambda a: jnp.pad(a, ((0, 0), (0, padded - length), (0, 0)))
        q, k, kz, v, g = pad(q), pad(k), pad(kz), pad(v), pad(g)
    bb = 8
    unroll = 2 if chunk > BF16_ROWS else bb
    tok = pl.BlockSpec((bb, chunk, RET_WIDTH), lambda i, c: (i, c, 0))
    state = pl.BlockSpec((bb, RET_HEADS, RET_HEAD_DIM, RET_HEAD_DIM), lambda i, c: (i, 0, 0, 0))
    out, s_new = pl.pallas_call(
        functools.partial(_ret_kernel, bb=bb, gamma_c=gamma_c, unroll=unroll),
        grid=(batch // bb, padded // chunk),
        in_specs=[tok, tok, tok, tok, tok,
                  _full((RET_HEADS, chunk, chunk)), _full((RET_HEADS, chunk, RET_HEAD_DIM)),
                  _full((1, RET_WIDTH)), state],
        out_specs=[tok, state],
        out_shape=(jax.ShapeDtypeStruct((batch, padded, RET_WIDTH), BF16),
                   jax.ShapeDtypeStruct(s0.shape, F32)),
        compiler_params=_params("parallel", "arbitrary"),
        name="retention",
    )(q, k, kz, v, g, mask, xi, gn_gain, s0)
    return out[:, :length], s_new


def _mixer_out(x_ref, ssm_ref, ret_ref, wout_ref):
    return (x_ref[...] + _dot(ssm_ref[...], wout_ref[:SSM_WIDTH, :])
            + _dot(ret_ref[...], wout_ref[SSM_WIDTH:, :]))


def _query(x1, g_ref, wq_ref):
    return (_dot(_rmsnorm(x1, g_ref[...]).astype(BF16), wq_ref[...]) * MEM_SCALE).astype(BF16)


def _outq_kernel(x_ref, ssm_ref, ret_ref, wout_ref, g_ref, wq_ref, x1_ref, q_ref):
    x1 = _mixer_out(x_ref, ssm_ref, ret_ref, wout_ref)
    x1_ref[...] = x1
    q_ref[...] = _query(x1, g_ref, wq_ref)


def _out_and_query(x, ssm, ret, w_out, g_xattn, w_mq):
    t = x.shape[0]
    tm = min(t, 512)
    row = lambda w: pl.BlockSpec((tm, w), lambda i: (i, 0))
    return pl.pallas_call(
        _outq_kernel,
        grid=(t // tm,),
        in_specs=[row(D_MODEL), row(SSM_WIDTH), row(RET_WIDTH), _full((D_MODEL, D_MODEL)),
                  _full((1, D_MODEL)), _full((D_MODEL, D_MODEL))],
        out_specs=[row(D_MODEL), row(D_MODEL)],
        out_shape=(jax.ShapeDtypeStruct((t, D_MODEL), F32), jax.ShapeDtypeStruct((t, D_MODEL), BF16)),
        compiler_params=_params("parallel"),
        name="out_proj_query",
    )(x, ssm, ret, w_out, g_xattn, w_mq)


def _cache_row_order(x):
    halves = MEM_HEAD_DIM // LANES
    tiles = [x[:, (h * halves + d) * LANES:(h * halves + d + 1) * LANES]
             for d in range(halves) for h in range(MEM_HEADS)]
    return jnp.swapaxes(jnp.stack(tiles, axis=0), 0, 1).reshape(x.shape[0] * CACHE_SUB, LANES)


def _memkv_kernel(m_ref, g_ref, wk_ref, wv_ref, k_ref, v_ref, kb_ref, vb_ref):
    m = _rmsnorm(m_ref[0], g_ref[...]).astype(BF16)
    k = _dot(m, wk_ref[...])
    v = _dot(m, wv_ref[...])
    k_ref[0] = _cache_row_order(k)
    v_ref[0] = _cache_row_order(v)
    kb_ref[0] = k.astype(BF16)
    vb_ref[0] = v.astype(BF16)


def _memory_kv(mem, g_mem, w_mk, w_mv):
    batch = mem.shape[0]
    tok = pl.BlockSpec((1, MEM_LEN, D_MODEL), lambda i: (i, 0, 0))
    cache = pl.BlockSpec((1, CACHE_ROWS, LANES), lambda i: (i, 0, 0))
    f = jax.ShapeDtypeStruct((batch, CACHE_ROWS, LANES), F32)
    h = jax.ShapeDtypeStruct((batch, MEM_LEN, D_MODEL), BF16)
    return pl.pallas_call(
        _memkv_kernel,
        grid=(batch,),
        in_specs=[tok, _full((1, D_MODEL)), _full((D_MODEL, D_MODEL)), _full((D_MODEL, D_MODEL))],
        out_specs=[cache, cache, tok, tok],
        out_shape=(f, f, h, h),
        compiler_params=_params("parallel"),
        name="memory_kv",
    )(mem, g_mem, w_mk, w_mv)


def _outattn_kernel(x_ref, ssm_ref, ret_ref, wout_ref, g_ref, wq_ref, k_ref, v_ref, x1_ref, o_ref):
    x1 = _mixer_out(x_ref, ssm_ref, ret_ref, wout_ref)
    x1_ref[...] = x1
    q = _query(x1, g_ref, wq_ref)
    cols = [slice(h * MEM_HEAD_DIM, (h + 1) * MEM_HEAD_DIM) for h in range(MEM_HEADS)]
    scores = [_dot_nt(q[:, c], k_ref[0, :, c]) for c in cols]
    probs = []
    for s in scores:
        e = jnp.exp(s - jnp.max(s, axis=-1, keepdims=True))
        probs.append((e / jnp.sum(e, axis=-1, keepdims=True)).astype(BF16))
    o_ref[...] = jnp.concatenate([_dot(p, v_ref[0, :, c]).astype(BF16) for p, c in zip(probs, cols)], axis=1)


def _out_and_attention(x, ssm, ret, w_out, g_xattn, w_mq, mk, mv, *, length):
    t = x.shape[0]
    tm = min(length, 1024)
    per_seq = length // tm
    row = lambda w: pl.BlockSpec((tm, w), lambda i: (i, 0))
    mem = pl.BlockSpec((1, MEM_LEN, D_MODEL), lambda i: (i // per_seq, 0, 0))
    return pl.pallas_call(
        _outattn_kernel,
        grid=(t // tm,),
        in_specs=[row(D_MODEL), row(SSM_WIDTH), row(RET_WIDTH), _full((D_MODEL, D_MODEL)),
                  _full((1, D_MODEL)), _full((D_MODEL, D_MODEL)), mem, mem],
        out_specs=[row(D_MODEL), row(D_MODEL)],
        out_shape=(jax.ShapeDtypeStruct((t, D_MODEL), F32), jax.ShapeDtypeStruct((t, D_MODEL), BF16)),
        compiler_params=_params("parallel"),
        name="out_proj_attention",
    )(x, ssm, ret, w_out, g_xattn, w_mq, mk, mv)


def _cached_attention(q_ref, k_ref, v_ref, o_ref, *, bb, length, between=None):
    hit = (lax.broadcasted_iota(jnp.int32, (CACHE_SUB, CACHE_ROWS), 0)
           == lax.broadcasted_iota(jnp.int32, (CACHE_SUB, CACHE_ROWS), 1) % CACHE_SUB)
    first_half = lax.broadcasted_iota(jnp.int32, (length, LANES), 1) % CACHE_SUB < MEM_HEADS
    tiles = CACHE_ROWS // LANES

    def class_reduce(x, op):
        shift = CACHE_SUB
        while shift < LANES:
            x = op(x, pltpu.roll(x, shift, axis=1))
            shift *= 2
        return x

    def softmax_rows(r):
        z = jnp.concatenate(
            [jnp.sum(jnp.where(hit, r[CACHE_SUB * t:CACHE_SUB * (t + 1)], 0.0), axis=0, keepdims=True)
             for t in range(length)], axis=0)
        parts = []
        for i in range(tiles):
            zi = z[:, LANES * i:LANES * (i + 1)]
            parts.append(zi + pltpu.roll(zi, LANES - MEM_HEADS, axis=1))
        mx = parts[0]
        for pi in parts[1:]:
            mx = jnp.maximum(mx, pi)
        mx = class_reduce(mx, jnp.maximum)
        es = [jnp.exp(pi - mx) for pi in parts]
        tot = es[0]
        for ei in es[1:]:
            tot = tot + ei
        tot = class_reduce(tot, jnp.add)
        ps = []
        for ei in es:
            pi = ei / tot
            ps.append(jnp.where(first_half, pi, pltpu.roll(pi, MEM_HEADS, axis=1)))
        p = jnp.concatenate(ps, axis=1)
        return jnp.concatenate(
            [jnp.where(hit, jnp.broadcast_to(p[t:t + 1], (CACHE_SUB, CACHE_ROWS)), 0.0)
             for t in range(length)], axis=0).astype(BF16)

    scores = [_dot_nt(q_ref[b], k_ref[b].astype(BF16)) for b in range(bb)]
    if between is not None:
        between()
    probs = [softmax_rows(r) for r in scores]
    for b in range(bb):
        o_ref[b] = _dot(probs[b], v_ref[b].astype(BF16)).astype(BF16)


def _to_cache_rows(a, lead):
    halves = MEM_HEAD_DIM // LANES
    batch = a.size // (lead * D_MODEL)
    a = a.reshape(batch, lead, MEM_HEADS, halves, LANES).transpose(0, 1, 3, 2, 4)
    return a.reshape(batch, lead * CACHE_SUB, LANES)


def _from_cache_rows(a, lead):
    halves = MEM_HEAD_DIM // LANES
    batch = a.shape[0]
    a = a.reshape(batch, lead, halves, MEM_HEADS, LANES).transpose(0, 1, 3, 2, 4)
    return a.reshape(batch, lead, D_MODEL)


MLP_CHUNK = 1024


def _post_kernel(x1_ref, o_ref, wo_ref, gm_ref, wup_ref, wdn_ref, gf_ref, *rest, side_bb, side_len):
    if side_bb:
        qs_ref, ck_ref, cv_ref, y_ref, os_ref = rest
    else:
        (y_ref,) = rest
    x2 = x1_ref[...] + _dot(o_ref[...], wo_ref[...])
    result = []

    def mlp():
        h = _rmsnorm(x2, gm_ref[...]).astype(BF16)
        chunks = [slice(c, c + MLP_CHUNK) for c in range(0, D_FF, MLP_CHUNK)]
        acc = x2
        up = _dot(h, wup_ref[:, chunks[0]])
        for c, cols in enumerate(chunks):
            nxt = _dot(h, wup_ref[:, chunks[c + 1]]) if c + 1 < len(chunks) else None
            a = jnp.maximum(up, 0.0)
            acc = acc + _dot((a * a).astype(BF16), wdn_ref[cols, :])
            up = nxt
        result.append(acc)

    if side_bb:
        _cached_attention(qs_ref, ck_ref, cv_ref, os_ref, bb=side_bb, length=side_len, between=mlp)
    else:
        mlp()
    y_ref[...] = _rmsnorm(result[0], gf_ref[...])


def _attn_out_mlp(x1, o, w_mo, g_mlp, w_up, w_down, g_final, side=None):
    t = x1.shape[0]
    tm = min(t, 512)
    steps = t // tm
    row = pl.BlockSpec((tm, D_MODEL), lambda i: (i, 0))
    in_specs = [row, row, _full((D_MODEL, D_MODEL)), _full((1, D_MODEL)),
                _full((D_MODEL, D_FF)), _full((D_FF, D_MODEL)), _full((1, D_MODEL))]
    out_specs = [row]
    out_shape = [jax.ShapeDtypeStruct((t, D_MODEL), F32)]
    args = [x1, o, w_mo, g_mlp, w_up, w_down, g_final]
    side_bb = side_len = 0
    if side is not None:
        q_rows = side[0]
        side_bb, rem = divmod(q_rows.shape[0], steps)
        if rem or not side_bb:
            raise ValueError("side attention sequences must spread evenly over the grid steps")
        side_len = q_rows.shape[1] // CACHE_SUB
        blk = lambda r: pl.BlockSpec((side_bb, r, LANES), lambda i: (i, 0, 0))
        in_specs += [blk(q_rows.shape[1]), blk(CACHE_ROWS), blk(CACHE_ROWS)]
        out_specs.append(blk(q_rows.shape[1]))
        out_shape.append(jax.ShapeDtypeStruct(q_rows.shape, BF16))
        args += list(side)
    outs = pl.pallas_call(
        functools.partial(_post_kernel, side_bb=side_bb, side_len=side_len),
        grid=(steps,),
        in_specs=in_specs,
        out_specs=out_specs,
        out_shape=out_shape,
        compiler_params=_params("parallel"),
        name="attn_out_mlp",
    )(*args)
    return outs if side is not None else outs[0]


def _mixers(proj, batch, length, s5_re, s5_im, ret_s, p):
    tokens = batch * length
    u, q, k, kz, v, g = proj
    ssm, s5_re_new, s5_im_new = _s5_mixer(
        u.reshape(batch, length, SSM_WIDTH), s5_re.reshape(batch, SSM_LANES), s5_im.reshape(batch, SSM_LANES),
        p["s5_maps"], p["d_skip"], p["w_glu"], batch=batch, length=length)
    tok3 = lambda a: a.reshape(batch, length, RET_WIDTH)
    ret, ret_new = _retention(tok3(q), tok3(k), tok3(kz), tok3(v), tok3(g), ret_s, p["ret_gn"])
    states = (s5_re_new.reshape(1, batch, SSM_GROUPS, SSM_STATE),
              s5_im_new.reshape(1, batch, SSM_GROUPS, SSM_STATE), ret_new[None])
    return ssm.reshape(tokens, SSM_WIDTH), ret.reshape(tokens, RET_WIDTH), states


def kernel(x_prompt, x_sample, mem_prompt, state_s5_re, state_s5_im, state_ret, cache_mem_k, cache_mem_v, g_mix, w_in, lam_re, lam_im, log_dt, b_re, b_im, c_re, c_im, d_skip, w_glu, ret_gn, w_out, g_xattn, g_mem, w_mq, w_mk, w_mv, w_mo, g_mlp, w_up, w_down, g_final):
    if g_mix.shape[0] != 1:
        raise ValueError("this kernel implements the single-layer configuration")
    bp, lp, _ = x_prompt.shape
    bs, ls, _ = x_sample.shape
    vec = lambda a: a.reshape(1, -1).astype(F32)
    xp = x_prompt.reshape(bp * lp, D_MODEL)
    xs = x_sample.reshape(bs * ls, D_MODEL)
    proj_p, (w_in_b, w_up_b, w_down_b, w_out_b, w_mq_b, w_mo_b, w_mk_b, w_mv_b, w_glu_b) = _project(
        xp, vec(g_mix[0]), w_in[0], length=lp, pos0=0.0,
        casts=(w_up[0], w_down[0], w_out[0], w_mq[0], w_mo[0], w_mk[0], w_mv[0], w_glu[0]))
    proj_s, _ = _project(xs, vec(g_mix[0]), w_in_b, length=ls, pos0=float(PAST_LEN))
    p = dict(
        s5_maps=_s5_block_maps(lam_re[0], lam_im[0], log_dt[0], b_re[0], b_im[0], c_re[0], c_im[0]),
        d_skip=vec(d_skip[0]), w_glu=w_glu_b, ret_gn=vec(ret_gn[0]))
    g_x = vec(g_xattn[0])
    mlp = (w_mo_b, vec(g_mlp[0]), w_up_b, w_down_b, vec(g_final))

    ssm_s, ret_s, states_s = _mixers(proj_s, bs, ls, state_s5_re[0], state_s5_im[0], state_ret[0], p)
    x1_s, q_s = _out_and_query(xs, ssm_s, ret_s, w_out_b, g_x, w_mq_b)
    if (ls * CACHE_SUB) % BF16_ROWS:
        raise ValueError("cached attention needs whole bf16 tiles of query rows")
    side = (_to_cache_rows(q_s, ls), _to_cache_rows(cache_mem_k, MEM_LEN), _to_cache_rows(cache_mem_v, MEM_LEN))

    mk_rows, mv_rows, mk_p, mv_p = _memory_kv(mem_prompt, vec(g_mem[0]), w_mk_b, w_mv_b)
    zs = jnp.zeros((bp, SSM_GROUPS, SSM_STATE), F32)
    zr = jnp.zeros((bp, RET_HEADS, RET_HEAD_DIM, RET_HEAD_DIM), F32)
    ssm_p, ret_p, states_p = _mixers(proj_p, bp, lp, zs, zs, zr, p)
    x1_p, o_p = _out_and_attention(xp, ssm_p, ret_p, w_out_b, g_x, w_mq_b, mk_p, mv_p, length=lp)
    y_p, o_s = _attn_out_mlp(x1_p, o_p, *mlp, side=side)
    y_s = _attn_out_mlp(x1_s, _from_cache_rows(o_s, ls).reshape(bs * ls, D_MODEL), *mlp)

    kv5 = lambda a: _from_cache_rows(a, MEM_LEN).reshape(1, bp, MEM_LEN, MEM_HEADS, MEM_HEAD_DIM)
    return (y_p.reshape(bp, lp, D_MODEL), y_s.reshape(bs, ls, D_MODEL), *states_p, kv5(mk_rows), kv5(mv_rows),
            *states_s)
```

```python
import functools
import math

import numpy as np
import jax
import jax.numpy as jnp
from jax import lax
from jax.experimental import pallas as pl
from jax.experimental.pallas import tpu as pltpu

F32 = jnp.float32
BF16 = jnp.bfloat16

D_MODEL = 1024
SSM_WIDTH = 512
SSM_GROUP = 16
SSM_GROUPS = 32
SSM_STATE = 64
SSM_LANES = SSM_GROUPS * SSM_STATE
RET_WIDTH = 512
RET_HEADS = 4
RET_HEAD_DIM = 128
RET_CHUNK = 128
ROPE_BASE = 10000.0
MEM_LEN = 256
MEM_HEADS = 4
MEM_HEAD_DIM = 256
MEM_SCALE = MEM_HEAD_DIM ** -0.5
D_FF = 4096
PROJ_WIDTH = SSM_WIDTH + 4 * RET_WIDTH
EPS = 1e-6
PAST_LEN = 16384

LANES = 128
MXU_TILE = 256
BF16_ROWS = 16
VMEM_LIMIT = 56 * 1024 * 1024

CACHE_ROWS = MEM_LEN * MEM_HEADS * MEM_HEAD_DIM // LANES
CACHE_SUB = MEM_HEADS * MEM_HEAD_DIM // LANES

assert math.frexp(MEM_SCALE)[0] == 0.5


def _dot(a, b):
    return jnp.dot(a, b, preferred_element_type=F32)


def _dot_nt(a, b):
    return lax.dot_general(a, b, (((1,), (1,)), ((), ())), preferred_element_type=F32)


def _dot_tn(a, b):
    return lax.dot_general(a, b, (((0,), (0,)), ((), ())), preferred_element_type=F32)


def _rmsnorm(x, g):
    return x * lax.rsqrt(jnp.mean(x * x, axis=-1, keepdims=True) + EPS) * g


def _sigmoid(x):
    return 1.0 / (1.0 + jnp.exp(-x))


def _gelu_tanh(x):
    c = math.sqrt(2.0 / math.pi)
    return x * (0.5 * (1.0 + jnp.tanh(c * (x + 0.044715 * (x * x * x)))))


def _params(*sem):
    return pltpu.CompilerParams(dimension_semantics=sem, vmem_limit_bytes=VMEM_LIMIT)


def _full(shape):
    return pl.BlockSpec(shape, lambda *_: (0,) * len(shape))


SSM_BLOCK = 4
SSM_TILE_GROUPS = MXU_TILE // SSM_STATE
SSM_TILES = SSM_GROUPS // SSM_TILE_GROUPS
SSM_TILE_CHANNELS = SSM_TILE_GROUPS * SSM_GROUP
GROUP_SHIFT = SSM_GROUP.bit_length() - 1
STATE_SHIFT = SSM_STATE.bit_length() - 1

assert SSM_BLOCK * SSM_TILE_CHANNELS == MXU_TILE and SSM_TILE_CHANNELS == SSM_STATE


def _s5_prep_kernel(lr_ref, li_ref, ldt_ref, br_ref, bi_ref, cr_ref, ci_ref,
                    ar_ref, ai_ref, winr_ref, wini_ref, woutr_ref, wouti_ref, kdir_ref):
    lr = lr_ref[...]
    li = li_ref[...]
    dt = jnp.exp(ldt_ref[...])
    mag = jnp.exp(lr * dt)
    ab_re = mag * jnp.cos(li * dt)
    ab_im = mag * jnp.sin(li * dt)
    den = lr * lr + li * li
    f_re = ((ab_re - 1.0) * lr + ab_im * li) / den
    f_im = (ab_im * lr - (ab_re - 1.0) * li) / den
    bb_re = f_re * br_ref[...] - f_im * bi_ref[...]
    bb_im = f_re * bi_ref[...] + f_im * br_ref[...]
    cr = cr_ref[...]
    ci = ci_ref[...]
    pows = [(jnp.ones_like(ab_re), jnp.zeros_like(ab_re))]
    for _ in range(SSM_BLOCK):
        pr, pi = pows[-1]
        pows.append((pr * ab_re - pi * ab_im, pr * ab_im + pi * ab_re))
    ar_ref[...], ai_ref[...] = pows[SSM_BLOCK]

    R, T, TG = SSM_BLOCK, SSM_TILES, SSM_TILE_GROUPS
    rows = SSM_GROUPS * SSM_GROUP
    side = SSM_TILE_CHANNELS
    exact = dict(precision=lax.Precision.HIGHEST, preferred_element_type=F32)
    iota = lambda shape, d: lax.broadcasted_iota(jnp.int32, shape, d)
    rep4 = lambda a, axis: jnp.concatenate([a] * TG, axis=axis)
    tiles = lambda a: a.reshape(T, side, a.shape[-1])
    group_of = lambda i: (i & (side - 1)) >> GROUP_SHIFT

    own_in = group_of(iota((rows, MXU_TILE), 0)) == iota((rows, MXU_TILE), 1) >> STATE_SHIFT
    for s in range(R):
        pr, pi = pows[R - 1 - s]
        for ref, w in ((winr_ref, pr * bb_re - pi * bb_im), (wini_ref, pr * bb_im + pi * bb_re)):
            ref[:, side * s:side * (s + 1), :] = tiles(jnp.where(own_in, rep4(w, 1), 0.0)).astype(BF16)

    eye = (iota((rows, rows), 0) == iota((rows, rows), 1)).astype(F32)
    own_out = iota((MXU_TILE, rows), 0) >> STATE_SHIFT == group_of(iota((MXU_TILE, rows), 1))
    for ref, real in ((woutr_ref, True), (wouti_ref, False)):
        per_lag = []
        for j in range(R):
            pr, pi = pows[j + 1]
            w = cr * pr - ci * pi if real else -(cr * pi + ci * pr)
            wt = lax.dot_general(w, eye, (((0,), (0,)), ((), ())), **exact)
            per_lag.append(jnp.where(own_out, rep4(wt, 0), 0.0))
        for n in range(T):
            ref[n] = jnp.concatenate([m[:, side * n:side * (n + 1)] for m in per_lag], axis=1).astype(BF16)

    grouped = lambda a: a.reshape(SSM_GROUPS, SSM_GROUP, SSM_STATE)
    contract = lambda a, b: lax.dot_general(a, b, (((2,), (2,)), ((0,), (0,))), **exact)
    own_dir = group_of(iota((rows, side), 0)) == iota((rows, side), 1) >> GROUP_SHIFT
    by_lag = []
    for d in range(R):
        pr, pi = pows[d]
        er = grouped(pr * bb_re - pi * bb_im)
        ei = grouped(pr * bb_im + pi * bb_re)
        kt = (contract(er, grouped(cr)) - contract(ei, grouped(ci))).reshape(rows, SSM_GROUP)
        by_lag.append(tiles(jnp.where(own_dir, rep4(kt, 1), 0.0)))
    nothing = jnp.zeros_like(by_lag[0])
    for s in range(R):
        kdir_ref[:, side * s:side * (s + 1), :] = jnp.concatenate(
            [by_lag[j - s] if j >= s else nothing for j in range(R)], axis=2).astype(BF16)


def _s5_block_maps(lam_re, lam_im, log_dt, b_re, b_im, c_re, c_im):
    rows = SSM_GROUPS * SSM_GROUP
    rep = lambda a: jnp.repeat(a, SSM_GROUP, axis=0)
    ldt = jnp.broadcast_to(log_dt[:, None], (SSM_GROUPS, SSM_STATE))
    bt = lambda b: b.transpose(0, 2, 1).reshape(rows, SSM_STATE)
    ct = lambda c: c.reshape(rows, SSM_STATE)
    one = jax.ShapeDtypeStruct((rows, SSM_STATE), F32)
    tile = jax.ShapeDtypeStruct((SSM_TILES, MXU_TILE, MXU_TILE), BF16)
    ar, ai, winr, wini, woutr, wouti, direct = pl.pallas_call(
        _s5_prep_kernel,
        out_shape=(one, one, tile, tile, tile, tile, tile),
        compiler_params=pltpu.CompilerParams(vmem_limit_bytes=VMEM_LIMIT),
        name="s5_block_maps",
    )(rep(lam_re), rep(lam_im), rep(ldt), bt(b_re), bt(b_im), ct(c_re), ct(c_im))
    a_re = ar[::SSM_GROUP].reshape(1, SSM_LANES)
    a_im = ai[::SSM_GROUP].reshape(1, SSM_LANES)
    return a_re, a_im, winr, wini, woutr, wouti, direct


def _retention_chunk(length):
    return RET_CHUNK if length % RET_CHUNK == 0 else length


def _log_gamma():
    return np.log(1.0 - 2.0 ** (-5.0 - np.arange(RET_HEADS, dtype=np.float64)))


def _rope_tables(length, pos0):
    half = RET_HEAD_DIM // 2
    inv = ROPE_BASE ** (-np.arange(half, dtype=np.float64) / half)
    ang = (pos0 + np.arange(length, dtype=np.float64))[:, None] * inv[None, :]
    return (np.concatenate([np.cos(ang), np.cos(ang)], axis=1),
            np.concatenate([-np.sin(ang), np.sin(ang)], axis=1))


def _zeta_rows(length):
    chunk = _retention_chunk(length)
    idx = np.arange(chunk, dtype=np.float64)
    zeta = np.exp((chunk - 1.0 - idx)[None, :] * _log_gamma()[:, None])
    return np.concatenate([np.broadcast_to(z[:, None], (chunk, RET_HEAD_DIM)) for z in zeta], axis=1)


def _decay_tables(length, padded):
    chunk = _retention_chunk(length)
    cpad = chunk if padded == length else padded
    lg = _log_gamma()
    idx = np.arange(chunk, dtype=np.float64)
    diff = idx[:, None] - idx[None, :]
    mask = np.where(diff[None] >= 0, np.exp(np.maximum(diff, 0.0)[None] * lg[:, None, None]), 0.0)
    xi = np.exp((idx + 1.0)[None, :] * lg[:, None])
    gamma_c = tuple(float(v) for v in np.exp(chunk * lg))
    mask = np.pad(mask, ((0, 0), (0, cpad - chunk), (0, cpad - chunk)))
    xi = np.broadcast_to(np.pad(xi, ((0, 0), (0, cpad - chunk)))[:, :, None], (RET_HEADS, cpad, RET_HEAD_DIM))
    f = lambda a: jnp.asarray(np.ascontiguousarray(a), dtype=F32)
    return cpad, gamma_c, f(mask), f(xi)


def _proj_kernel(x_ref, g_ref, w_ref, cc_ref, ss_ref, zeta_ref, *rest, n_cast):
    casts_in, rest = rest[:n_cast], rest[n_cast:]
    u_ref, q_ref, k_ref, kz_ref, v_ref, gate_ref = rest[:6]
    w = w_ref[...]
    if n_cast:
        w = w.astype(BF16)
        for src, dst in zip(casts_in, rest[6:]):
            dst[...] = src[...].astype(BF16)
    h = _rmsnorm(x_ref[...], g_ref[...]).astype(BF16)
    proj = _dot(h, w)
    u_ref[...] = proj[:, :SSM_WIDTH]
    v_ref[...] = proj[:, SSM_WIDTH + 2 * RET_WIDTH:SSM_WIDTH + 3 * RET_WIDTH].astype(BF16)
    gate_ref[...] = proj[:, SSM_WIDTH + 3 * RET_WIDTH:]
    cc = cc_ref[...]
    ss = ss_ref[...]
    rope = lambda a: a * cc + pltpu.roll(a, RET_HEAD_DIM // 2, axis=1) * ss
    for hd in range(RET_HEADS):
        cols = slice(hd * RET_HEAD_DIM, (hd + 1) * RET_HEAD_DIM)
        q = rope(proj[:, SSM_WIDTH + hd * RET_HEAD_DIM:SSM_WIDTH + (hd + 1) * RET_HEAD_DIM])
        k = rope(proj[:, SSM_WIDTH + RET_WIDTH + hd * RET_HEAD_DIM:SSM_WIDTH + RET_WIDTH + (hd + 1) * RET_HEAD_DIM])
        q_ref[:, cols] = (q * (RET_HEAD_DIM ** -0.5)).astype(BF16)
        k_ref[:, cols] = k.astype(BF16)
        kz_ref[:, cols] = (k * zeta_ref[:, cols]).astype(BF16)


def _project(x, g_mix, w_in, *, length, pos0, casts=()):
    t = x.shape[0]
    tm = min(t, 1024)
    steps = t // tm
    chunk = _retention_chunk(length)
    casts = (w_in, *casts) if casts else ()
    if tm % chunk or (length % tm and tm % length):
        raise ValueError("token tiles must hold whole retention chunks of whole or repeated sequences")
    cc, ss = _rope_tables(length, pos0)
    if length < tm:
        cc, ss = np.tile(cc, (tm // length, 1)), np.tile(ss, (tm // length, 1))
    tab_blocks = cc.shape[0] // tm
    zeta = np.tile(_zeta_rows(length), (tm // chunk, 1))
    const = lambda a: jnp.asarray(np.ascontiguousarray(a), dtype=F32)
    row = lambda w: pl.BlockSpec((tm, w), lambda i: (i, 0))
    tab = pl.BlockSpec((tm, RET_HEAD_DIM), lambda i: (i % tab_blocks, 0))
    f = jax.ShapeDtypeStruct((t, 512), F32)
    h = jax.ShapeDtypeStruct((t, 512), BF16)
    in_specs = [row(D_MODEL), _full((1, D_MODEL)), _full((D_MODEL, PROJ_WIDTH)), tab, tab, _full((tm, RET_WIDTH))]
    out_specs = [row(512)] * 6
    out_shape = [f, h, h, h, h, f]
    if casts:
        for c in casts:
            rows, rem = divmod(c.shape[0], steps)
            if rem or rows % BF16_ROWS:
                raise ValueError("side-cast weights must split into whole bf16 tiles per grid step")
            blk = pl.BlockSpec((rows, c.shape[1]), lambda i: (i, 0))
            in_specs.append(blk)
            out_specs.append(blk)
            out_shape.append(jax.ShapeDtypeStruct(c.shape, BF16))
    outs = pl.pallas_call(
        functools.partial(_proj_kernel, n_cast=len(casts)),
        grid=(steps,),
        in_specs=in_specs,
        out_specs=out_specs,
        out_shape=out_shape,
        compiler_params=_params("arbitrary" if casts else "parallel"),
        name="in_proj",
    )(x, g_mix, w_in, const(cc), const(ss), const(zeta), *casts)
    return outs[:6], outs[6:]


def _s5_block_kernel(u_ref, h0r_ref, h0i_ref, ar_ref, ai_ref, winr_ref, wini_ref, woutr_ref, wouti_ref,
                     kdir_ref, d_ref, wglu_ref, out_ref, hr_ref, hi_ref, vr_scr, vi_scr, xr_scr, xi_scr,
                     *, batch, steps, pair, width):
    R, T = SSM_BLOCK, SSM_TILES
    chan = SSM_TILE_CHANNELS
    blocks = steps // R
    rows, brow = steps * batch, blocks * batch

    @pl.when(pl.program_id(0) == 0)
    def _():
        hr_ref[...] = h0r_ref[...]
        hi_ref[...] = h0i_ref[...]

    u = jnp.swapaxes(u_ref[...], 0, 1).reshape(rows, SSM_WIDTH)
    u4 = u.reshape(blocks, R, batch, SSM_WIDTH)
    lag = [u4[:, s].reshape(brow, SSM_WIDTH) for s in range(R)]
    direct = []
    for n in range(T):
        un = jnp.concatenate([lag[s][:, chan * n:chan * (n + 1)] for s in range(R)], axis=1).astype(BF16)
        lanes = slice(MXU_TILE * n, MXU_TILE * (n + 1))
        vr_scr[:, lanes] = _dot(un, winr_ref[n])
        vi_scr[:, lanes] = _dot(un, wini_ref[n])
        direct.append(_dot(un, kdir_ref[n]))

    for j in range(SSM_LANES // width):
        glob = slice(j * width, (j + 1) * width)
        ar = jnp.broadcast_to(ar_ref[:, glob], (batch, width))
        ai = jnp.broadcast_to(ai_ref[:, glob], (batch, width))
        xr = hr_ref[:, glob]
        xi = hi_ref[:, glob]
        for i in range(blocks // pair):
            prev_r, prev_i = [], []
            for s in range(pair):
                at = slice((i * pair + s) * batch, (i * pair + s + 1) * batch)
                prev_r.append(xr)
                prev_i.append(xi)
                xr, xi = ar * xr - ai * xi + vr_scr[at, glob], ar * xi + ai * xr + vi_scr[at, glob]
            blk = slice(i * pair * batch, (i + 1) * pair * batch)
            xr_scr[blk, glob] = jnp.concatenate(prev_r, axis=0).astype(BF16)
            xi_scr[blk, glob] = jnp.concatenate(prev_i, axis=0).astype(BF16)
        hr_ref[:, glob] = xr
        hi_ref[:, glob] = xi

    y4 = []
    for n in range(T):
        lanes = slice(MXU_TILE * n, MXU_TILE * (n + 1))
        y4.append(_dot(xr_scr[:, lanes], woutr_ref[n]) + _dot(xi_scr[:, lanes], wouti_ref[n]) + direct[n])
    per_lag = [jnp.concatenate([y4[n][:, chan * j:chan * (j + 1)] for n in range(T)], axis=1)
               .reshape(blocks, batch, SSM_WIDTH) for j in range(R)]
    y = jnp.stack(per_lag, axis=1).reshape(rows, SSM_WIDTH) + d_ref[...] * u
    z = _gelu_tanh(y)
    out = z * _sigmoid(_dot(z.astype(BF16), wglu_ref[...]))
    out_ref[...] = jnp.swapaxes(out.reshape(steps, batch, SSM_WIDTH), 0, 1).astype(BF16)


def _s5_mixer(u, h0_re, h0_im, maps, d_skip, w_glu, *, batch, length):
    a_re, a_im, winr, wini, woutr, wouti, direct = maps
    steps = min(length, 256)
    if steps % SSM_BLOCK or length % steps:
        raise ValueError("sequence length must be a multiple of the S5 block")
    blocks = steps // SSM_BLOCK
    pair = max(1, min(blocks, BF16_ROWS // batch))
    if (pair * batch) % BF16_ROWS or blocks % pair:
        raise ValueError("S5 state stores need whole bf16 tiles")
    width = SSM_LANES // 2 if batch <= 8 else LANES
    brow = blocks * batch
    kern = functools.partial(_s5_block_kernel, batch=batch, steps=steps, pair=pair, width=width)
    st = jax.ShapeDtypeStruct((batch, SSM_LANES), F32)
    tile = _full((SSM_TILES, MXU_TILE, MXU_TILE))
    return pl.pallas_call(
        kern,
        grid=(length // steps,),
        in_specs=[pl.BlockSpec((batch, steps, SSM_WIDTH), lambda i: (0, i, 0)),
                  _full((batch, SSM_LANES)), _full((batch, SSM_LANES)),
                  _full((1, SSM_LANES)), _full((1, SSM_LANES)),
                  tile, tile, tile, tile, tile,
                  _full((1, SSM_WIDTH)), _full((SSM_WIDTH, SSM_WIDTH))],
        out_specs=[pl.BlockSpec((batch, steps, SSM_WIDTH), lambda i: (0, i, 0)),
                   _full((batch, SSM_LANES)), _full((batch, SSM_LANES))],
        out_shape=(jax.ShapeDtypeStruct((batch, length, SSM_WIDTH), BF16), st, st),
        scratch_shapes=[pltpu.VMEM((brow, SSM_LANES), F32), pltpu.VMEM((brow, SSM_LANES), F32),
                        pltpu.VMEM((brow, SSM_LANES), BF16), pltpu.VMEM((brow, SSM_LANES), BF16)],
        compiler_params=_params("arbitrary"),
        name="s5_mixer",
    )(u, h0_re, h0_im, a_re, a_im, winr, wini, woutr, wouti, direct, d_skip, w_glu)


def _ret_kernel(q_ref, k_ref, kz_ref, v_ref, g_ref, mask_ref, xi_ref, gain_ref, s0_ref, o_ref, s_ref,
                *, bb, gamma_c, unroll):
    @pl.when(pl.program_id(1) == 0)
    def _():
        s_ref[...] = s0_ref[...]

    heads = range(RET_HEADS)
    cols = [slice(h * RET_HEAD_DIM, (h + 1) * RET_HEAD_DIM) for h in heads]

    def per_batch(b, carry):
        qb = [q_ref[b, :, c] for c in cols]
        vb = [v_ref[b, :, c] for c in cols]
        s_prev = [s_ref[b, h] for h in heads]
        scores = [(_dot_nt(qb[h], k_ref[b, :, cols[h]]) * mask_ref[h]).astype(BF16) for h in heads]
        cross = [_dot(qb[h], s_prev[h].astype(BF16)) * xi_ref[h] for h in heads]
        kv = [_dot_tn(kz_ref[b, :, cols[h]], vb[h]) for h in heads]
        inner = [_dot(scores[h], vb[h]) for h in heads]
        outs = []
        for h in heads:
            s_ref[b, h] = s_prev[h] * gamma_c[h] + kv[h]
            o = inner[h] + cross[h]
            mu = jnp.mean(o, axis=-1, keepdims=True)
            oc = o - mu
            var = jnp.mean(oc * oc, axis=-1, keepdims=True)
            on = oc * lax.rsqrt(var + EPS) * gain_ref[:, cols[h]]
            gate = g_ref[b, :, cols[h]]
            outs.append((gate * _sigmoid(gate) * on).astype(BF16))
        o_ref[b] = jnp.concatenate(outs, axis=1)
        return carry

    lax.fori_loop(0, bb, per_batch, 0, unroll=unroll)


def _retention(q, k, kz, v, g, s0, gn_gain):
    batch, length, _ = q.shape
    padded = pl.cdiv(length, BF16_ROWS) * BF16_ROWS
    chunk, gamma_c, mask, xi = _decay_tables(length, padded)
    if padded != length:
        pad = lambda a: jnp.pad(a, ((0, 0), (0, padded - length), (0, 0)))
        q, k, kz, v, g = pad(q), pad(k), pad(kz), pad(v), pad(g)
    bb = 8
    unroll = 2 if chunk > BF16_ROWS else bb
    tok = pl.BlockSpec((bb, chunk, RET_WIDTH), lambda i, c: (i, c, 0))
    state = pl.BlockSpec((bb, RET_HEADS, RET_HEAD_DIM, RET_HEAD_DIM), lambda i, c: (i, 0, 0, 0))
    out, s_new = pl.pallas_call(
        functools.partial(_ret_kernel, bb=bb, gamma_c=gamma_c, unroll=unroll),
        grid=(batch // bb, padded // chunk),
        in_specs=[tok, tok, tok, tok, tok,
                  _full((RET_HEADS, chunk, chunk)), _full((RET_HEADS, chunk, RET_HEAD_DIM)),
                  _full((1, RET_WIDTH)), state],
        out_specs=[tok, state],
        out_shape=(jax.ShapeDtypeStruct((batch, padded, RET_WIDTH), BF16),
                   jax.ShapeDtypeStruct(s0.shape, F32)),
        compiler_params=_params("parallel", "arbitrary"),
        name="retention",
    )(q, k, kz, v, g, mask, xi, gn_gain, s0)
    return out[:, :length], s_new


def _mixer_out(x_ref, ssm_ref, ret_ref, wout_ref):
    return (x_ref[...] + _dot(ssm_ref[...], wout_ref[:SSM_WIDTH, :])
            + _dot(ret_ref[...], wout_ref[SSM_WIDTH:, :]))


def _query(x1, g_ref, wq_ref):
    return (_dot(_rmsnorm(x1, g_ref[...]).astype(BF16), wq_ref[...]) * MEM_SCALE).astype(BF16)


def _outq_kernel(x_ref, ssm_ref, ret_ref, wout_ref, g_ref, wq_ref, x1_ref, q_ref):
    x1 = _mixer_out(x_ref, ssm_ref, ret_ref, wout_ref)
    x1_ref[...] = x1
    q_ref[...] = _query(x1, g_ref, wq_ref)


def _out_and_query(x, ssm, ret, w_out, g_xattn, w_mq):
    t = x.shape[0]
    tm = min(t, 512)
    row = lambda w: pl.BlockSpec((tm, w), lambda i: (i, 0))
    return pl.pallas_call(
        _outq_kernel,
        grid=(t // tm,),
        in_specs=[row(D_MODEL), row(SSM_WIDTH), row(RET_WIDTH), _full((D_MODEL, D_MODEL)),
                  _full((1, D_MODEL)), _full((D_MODEL, D_MODEL))],
        out_specs=[row(D_MODEL), row(D_MODEL)],
        out_shape=(jax.ShapeDtypeStruct((t, D_MODEL), F32), jax.ShapeDtypeStruct((t, D_MODEL), BF16)),
        compiler_params=_params("parallel"),
        name="out_proj_query",
    )(x, ssm, ret, w_out, g_xattn, w_mq)


def _cache_row_order(x):
    halves = MEM_HEAD_DIM // LANES
    tiles = [x[:, (h * halves + d) * LANES:(h * halves + d + 1) * LANES]
             for d in range(halves) for h in range(MEM_HEADS)]
    return jnp.swapaxes(jnp.stack(tiles, axis=0), 0, 1).reshape(x.shape[0] * CACHE_SUB, LANES)


def _memkv_kernel(m_ref, g_ref, wk_ref, wv_ref, k_ref, v_ref, kb_ref, vb_ref):
    m = _rmsnorm(m_ref[0], g_ref[...]).astype(BF16)
    k = _dot(m, wk_ref[...])
    v = _dot(m, wv_ref[...])
    k_ref[0] = _cache_row_order(k)
    v_ref[0] = _cache_row_order(v)
    kb_ref[0] = k.astype(BF16)
    vb_ref[0] = v.astype(BF16)


def _memory_kv(mem, g_mem, w_mk, w_mv):
    batch = mem.shape[0]
    tok = pl.BlockSpec((1, MEM_LEN, D_MODEL), lambda i: (i, 0, 0))
    cache = pl.BlockSpec((1, CACHE_ROWS, LANES), lambda i: (i, 0, 0))
    f = jax.ShapeDtypeStruct((batch, CACHE_ROWS, LANES), F32)
    h = jax.ShapeDtypeStruct((batch, MEM_LEN, D_MODEL), BF16)
    return pl.pallas_call(
        _memkv_kernel,
        grid=(batch,),
        in_specs=[tok, _full((1, D_MODEL)), _full((D_MODEL, D_MODEL)), _full((D_MODEL, D_MODEL))],
        out_specs=[cache, cache, tok, tok],
        out_shape=(f, f, h, h),
        compiler_params=_params("parallel"),
        name="memory_kv",
    )(mem, g_mem, w_mk, w_mv)


def _outattn_kernel(x_ref, ssm_ref, ret_ref, wout_ref, g_ref, wq_ref, k_ref, v_ref, x1_ref, o_ref):
    x1 = _mixer_out(x_ref, ssm_ref, ret_ref, wout_ref)
    x1_ref[...] = x1
    q = _query(x1, g_ref, wq_ref)
    cols = [slice(h * MEM_HEAD_DIM, (h + 1) * MEM_HEAD_DIM) for h in range(MEM_HEADS)]
    scores = [_dot_nt(q[:, c], k_ref[0, :, c]) for c in cols]
    probs = []
    for s in scores:
        e = jnp.exp(s - jnp.max(s, axis=-1, keepdims=True))
        probs.append((e / jnp.sum(e, axis=-1, keepdims=True)).astype(BF16))
    o_ref[...] = jnp.concatenate([_dot(p, v_ref[0, :, c]).astype(BF16) for p, c in zip(probs, cols)], axis=1)


def _out_and_attention(x, ssm, ret, w_out, g_xattn, w_mq, mk, mv, *, length):
    t = x.shape[0]
    tm = min(length, 1024)
    per_seq = length // tm
    row = lambda w: pl.BlockSpec((tm, w), lambda i: (i, 0))
    mem = pl.BlockSpec((1, MEM_LEN, D_MODEL), lambda i: (i // per_seq, 0, 0))
    return pl.pallas_call(
        _outattn_kernel,
        grid=(t // tm,),
        in_specs=[row(D_MODEL), row(SSM_WIDTH), row(RET_WIDTH), _full((D_MODEL, D_MODEL)),
                  _full((1, D_MODEL)), _full((D_MODEL, D_MODEL)), mem, mem],
        out_specs=[row(D_MODEL), row(D_MODEL)],
        out_shape=(jax.ShapeDtypeStruct((t, D_MODEL), F32), jax.ShapeDtypeStruct((t, D_MODEL), BF16)),
        compiler_params=_params("parallel"),
        name="out_proj_attention",
    )(x, ssm, ret, w_out, g_xattn, w_mq, mk, mv)


def _cached_attention(q_ref, k_ref, v_ref, o_ref, *, bb, length, between=None):
    hit = (lax.broadcasted_iota(jnp.int32, (CACHE_SUB, CACHE_ROWS), 0)
           == lax.broadcasted_iota(jnp.int32, (CACHE_SUB, CACHE_ROWS), 1) % CACHE_SUB)
    first_half = lax.broadcasted_iota(jnp.int32, (length, LANES), 1) % CACHE_SUB < MEM_HEADS
    tiles = CACHE_ROWS // LANES

    def class_reduce(x, op):
        shift = CACHE_SUB
        while shift < LANES:
            x = op(x, pltpu.roll(x, shift, axis=1))
            shift *= 2
        return x

    def softmax_rows(r):
        z = jnp.concatenate(
            [jnp.sum(jnp.where(hit, r[CACHE_SUB * t:CACHE_SUB * (t + 1)], 0.0), axis=0, keepdims=True)
             for t in range(length)], axis=0)
        parts = []
        for i in range(tiles):
            zi = z[:, LANES * i:LANES * (i + 1)]
            parts.append(zi + pltpu.roll(zi, LANES - MEM_HEADS, axis=1))
        mx = parts[0]
        for pi in parts[1:]:
            mx = jnp.maximum(mx, pi)
        mx = class_reduce(mx, jnp.maximum)
        es = [jnp.exp(pi - mx) for pi in parts]
        tot = es[0]
        for ei in es[1:]:
            tot = tot + ei
        tot = class_reduce(tot, jnp.add)
        ps = []
        for ei in es:
            pi = ei / tot
            ps.append(jnp.where(first_half, pi, pltpu.roll(pi, MEM_HEADS, axis=1)))
        p = jnp.concatenate(ps, axis=1)
        return jnp.concatenate(
            [jnp.where(hit, jnp.broadcast_to(p[t:t + 1], (CACHE_SUB, CACHE_ROWS)), 0.0)
             for t in range(length)], axis=0).astype(BF16)

    scores = [_dot_nt(q_ref[b], k_ref[b].astype(BF16)) for b in range(bb)]
    if between is not None:
        between()
    probs = [softmax_rows(r) for r in scores]
    for b in range(bb):
        o_ref[b] = _dot(probs[b], v_ref[b].astype(BF16)).astype(BF16)


def _to_cache_rows(a, lead):
    halves = MEM_HEAD_DIM // LANES
    batch = a.size // (lead * D_MODEL)
    a = a.reshape(batch, lead, MEM_HEADS, halves, LANES).transpose(0, 1, 3, 2, 4)
    return a.reshape(batch, lead * CACHE_SUB, LANES)


def _from_cache_rows(a, lead):
    halves = MEM_HEAD_DIM // LANES
    batch = a.shape[0]
    a = a.reshape(batch, lead, halves, MEM_HEADS, LANES).transpose(0, 1, 3, 2, 4)
    return a.reshape(batch, lead, D_MODEL)


MLP_CHUNK = 1024


def _post_kernel(x1_ref, o_ref, wo_ref, gm_ref, wup_ref, wdn_ref, gf_ref, *rest, side_bb, side_len):
    if side_bb:
        qs_ref, ck_ref, cv_ref, y_ref, os_ref = rest
    else:
        (y_ref,) = rest
    x2 = x1_ref[...] + _dot(o_ref[...], wo_ref[...])
    result = []

    def mlp():
        h = _rmsnorm(x2, gm_ref[...]).astype(BF16)
        chunks = [slice(c, c + MLP_CHUNK) for c in range(0, D_FF, MLP_CHUNK)]
        acc = x2
        up = _dot(h, wup_ref[:, chunks[0]])
        for c, cols in enumerate(chunks):
            nxt = _dot(h, wup_ref[:, chunks[c + 1]]) if c + 1 < len(chunks) else None
            a = jnp.maximum(up, 0.0)
            acc = acc + _dot((a * a).astype(BF16), wdn_ref[cols, :])
            up = nxt
        result.append(acc)

    if side_bb:
        _cached_attention(qs_ref, ck_ref, cv_ref, os_ref, bb=side_bb, length=side_len, between=mlp)
    else:
        mlp()
    y_ref[...] = _rmsnorm(result[0], gf_ref[...])


def _attn_out_mlp(x1, o, w_mo, g_mlp, w_up, w_down, g_final, side=None):
    t = x1.shape[0]
    tm = min(t, 512)
    steps = t // tm
    row = pl.BlockSpec((tm, D_MODEL), lambda i: (i, 0))
    in_specs = [row, row, _full((D_MODEL, D_MODEL)), _full((1, D_MODEL)),
                _full((D_MODEL, D_FF)), _full((D_FF, D_MODEL)), _full((1, D_MODEL))]
    out_specs = [row]
    out_shape = [jax.ShapeDtypeStruct((t, D_MODEL), F32)]
    args = [x1, o, w_mo, g_mlp, w_up, w_down, g_final]
    side_bb = side_len = 0
    if side is not None:
        q_rows = side[0]
        side_bb, rem = divmod(q_rows.shape[0], steps)
        if rem or not side_bb:
            raise ValueError("side attention sequences must spread evenly over the grid steps")
        side_len = q_rows.shape[1] // CACHE_SUB
        blk = lambda r: pl.BlockSpec((side_bb, r, LANES), lambda i: (i, 0, 0))
        in_specs += [blk(q_rows.shape[1]), blk(CACHE_ROWS), blk(CACHE_ROWS)]
        out_specs.append(blk(q_rows.shape[1]))
        out_shape.append(jax.ShapeDtypeStruct(q_rows.shape, BF16))
        args += list(side)
    outs = pl.pallas_call(
        functools.partial(_post_kernel, side_bb=side_bb, side_len=side_len),
        grid=(steps,),
        in_specs=in_specs,
        out_specs=out_specs,
        out_shape=out_shape,
        compiler_params=_params("parallel"),
        name="attn_out_mlp",
    )(*args)
    return outs if side is not None else outs[0]


def _mixers(proj, batch, length, s5_re, s5_im, ret_s, p):
    tokens = batch * length
    u, q, k, kz, v, g = proj
    ssm, s5_re_new, s5_im_new = _s5_mixer(
        u.reshape(batch, length, SSM_WIDTH), s5_re.reshape(batch, SSM_LANES), s5_im.reshape(batch, SSM_LANES),
        p["s5_maps"], p["d_skip"], p["w_glu"], batch=batch, length=length)
    tok3 = lambda a: a.reshape(batch, length, RET_WIDTH)
    ret, ret_new = _retention(tok3(q), tok3(k), tok3(kz), tok3(v), tok3(g), ret_s, p["ret_gn"])
    states = (s5_re_new.reshape(1, batch, SSM_GROUPS, SSM_STATE),
              s5_im_new.reshape(1, batch, SSM_GROUPS, SSM_STATE), ret_new[None])
    return ssm.reshape(tokens, SSM_WIDTH), ret.reshape(tokens, RET_WIDTH), states


def kernel(x_prompt, x_sample, mem_prompt, state_s5_re, state_s5_im, state_ret, cache_mem_k, cache_mem_v, g_mix, w_in, lam_re, lam_im, log_dt, b_re, b_im, c_re, c_im, d_skip, w_glu, ret_gn, w_out, g_xattn, g_mem, w_mq, w_mk, w_mv, w_mo, g_mlp, w_up, w_down, g_final):
    if g_mix.shape[0] != 1:
        raise ValueError("this kernel implements the single-layer configuration")
    bp, lp, _ = x_prompt.shape
    bs, ls, _ = x_sample.shape
    vec = lambda a: a.reshape(1, -1).astype(F32)
    xp = x_prompt.reshape(bp * lp, D_MODEL)
    xs = x_sample.reshape(bs * ls, D_MODEL)
    proj_p, (w_in_b, w_up_b, w_down_b) = _project(
        xp, vec(g_mix[0]), w_in[0], length=lp, pos0=0.0, casts=(w_up[0], w_down[0]))
    w_out_b, w_mq_b, w_mo_b, w_mk_b, w_mv_b, w_glu_b = (
        w[0].astype(BF16) for w in (w_out, w_mq, w_mo, w_mk, w_mv, w_glu))
    proj_s, _ = _project(xs, vec(g_mix[0]), w_in_b, length=ls, pos0=float(PAST_LEN))
    p = dict(
        s5_maps=_s5_block_maps(lam_re[0], lam_im[0], log_dt[0], b_re[0], b_im[0], c_re[0], c_im[0]),
        d_skip=vec(d_skip[0]), w_glu=w_glu_b, ret_gn=vec(ret_gn[0]))
    g_x = vec(g_xattn[0])
    mlp = (w_mo_b, vec(g_mlp[0]), w_up_b, w_down_b, vec(g_final))

    ssm_s, ret_s, states_s = _mixers(proj_s, bs, ls, state_s5_re[0], state_s5_im[0], state_ret[0], p)
    x1_s, q_s = _out_and_query(xs, ssm_s, ret_s, w_out_b, g_x, w_mq_b)
    if (ls * CACHE_SUB) % BF16_ROWS:
        raise ValueError("cached attention needs whole bf16 tiles of query rows")
    side = (_to_cache_rows(q_s, ls), _to_cache_rows(cache_mem_k, MEM_LEN), _to_cache_rows(cache_mem_v, MEM_LEN))

    mk_rows, mv_rows, mk_p, mv_p = _memory_kv(mem_prompt, vec(g_mem[0]), w_mk_b, w_mv_b)
    zs = jnp.zeros((bp, SSM_GROUPS, SSM_STATE), F32)
    zr = jnp.zeros((bp, RET_HEADS, RET_HEAD_DIM, RET_HEAD_DIM), F32)
    ssm_p, ret_p, states_p = _mixers(proj_p, bp, lp, zs, zs, zr, p)
    x1_p, o_p = _out_and_attention(xp, ssm_p, ret_p, w_out_b, g_x, w_mq_b, mk_p, mv_p, length=lp)
    y_p, o_s = _attn_out_mlp(x1_p, o_p, *mlp, side=side)
    y_s = _attn_out_mlp(x1_s, _from_cache_rows(o_s, ls).reshape(bs * ls, D_MODEL), *mlp)

    kv5 = lambda a: _from_cache_rows(a, MEM_LEN).reshape(1, bp, MEM_LEN, MEM_HEADS, MEM_HEAD_DIM)
    return (y_p.reshape(bp, lp, D_MODEL), y_s.reshape(bs, ls, D_MODEL), *states_p, kv5(mk_rows), kv5(mv_rows),
            *states_s)
```

```python
import functools
import math

import numpy as np
import jax
import jax.numpy as jnp
from jax import lax
from jax.experimental import pallas as pl
from jax.experimental.pallas import tpu as pltpu

F32 = jnp.float32
BF16 = jnp.bfloat16

D_MODEL = 1024
SSM_WIDTH = 512
SSM_GROUP = 16
SSM_GROUPS = 32
SSM_STATE = 64
SSM_LANES = SSM_GROUPS * SSM_STATE
RET_WIDTH = 512
RET_HEADS = 4
RET_HEAD_DIM = 128
RET_CHUNK = 128
ROPE_BASE = 10000.0
MEM_LEN = 256
MEM_HEADS = 4
MEM_HEAD_DIM = 256
MEM_SCALE = MEM_HEAD_DIM ** -0.5
D_FF = 4096
PROJ_WIDTH = SSM_WIDTH + 4 * RET_WIDTH
EPS = 1e-6
PAST_LEN = 16384

LANES = 128
MXU_TILE = 256
BF16_ROWS = 16
VMEM_LIMIT = 56 * 1024 * 1024

CACHE_ROWS = MEM_LEN * MEM_HEADS * MEM_HEAD_DIM // LANES
CACHE_SUB = MEM_HEADS * MEM_HEAD_DIM // LANES

assert math.frexp(MEM_SCALE)[0] == 0.5


def _dot(a, b):
    return jnp.dot(a, b, preferred_element_type=F32)


def _dot_nt(a, b):
    return lax.dot_general(a, b, (((1,), (1,)), ((), ())), preferred_element_type=F32)


def _dot_tn(a, b):
    return lax.dot_general(a, b, (((0,), (0,)), ((), ())), preferred_element_type=F32)


def _rmsnorm(x, g):
    return x * lax.rsqrt(jnp.mean(x * x, axis=-1, keepdims=True) + EPS) * g


def _sigmoid(x):
    return 1.0 / (1.0 + jnp.exp(-x))


def _gelu_tanh(x):
    c = math.sqrt(2.0 / math.pi)
    return x * (0.5 * jnp.tanh(x * (c + (c * 0.044715) * (x * x))) + 0.5)


def _params(*sem):
    return pltpu.CompilerParams(dimension_semantics=sem, vmem_limit_bytes=VMEM_LIMIT)


def _full(shape):
    return pl.BlockSpec(shape, lambda *_: (0,) * len(shape))


SSM_BLOCK = 4
SSM_TILE_GROUPS = MXU_TILE // SSM_STATE
SSM_TILES = SSM_GROUPS // SSM_TILE_GROUPS
SSM_TILE_CHANNELS = SSM_TILE_GROUPS * SSM_GROUP
GROUP_SHIFT = SSM_GROUP.bit_length() - 1
STATE_SHIFT = SSM_STATE.bit_length() - 1

assert SSM_BLOCK * SSM_TILE_CHANNELS == MXU_TILE and SSM_TILE_CHANNELS == SSM_STATE


def _s5_prep_kernel(lr_ref, li_ref, ldt_ref, br_ref, bi_ref, cr_ref, ci_ref,
                    ar_ref, ai_ref, winr_ref, wini_ref, woutr_ref, wouti_ref, kdir_ref):
    lr = lr_ref[...]
    li = li_ref[...]
    dt = jnp.exp(ldt_ref[...])
    mag = jnp.exp(lr * dt)
    ab_re = mag * jnp.cos(li * dt)
    ab_im = mag * jnp.sin(li * dt)
    den = lr * lr + li * li
    f_re = ((ab_re - 1.0) * lr + ab_im * li) / den
    f_im = (ab_im * lr - (ab_re - 1.0) * li) / den
    bb_re = f_re * br_ref[...] - f_im * bi_ref[...]
    bb_im = f_re * bi_ref[...] + f_im * br_ref[...]
    cr = cr_ref[...]
    ci = ci_ref[...]
    pows = [(jnp.ones_like(ab_re), jnp.zeros_like(ab_re))]
    for _ in range(SSM_BLOCK):
        pr, pi = pows[-1]
        pows.append((pr * ab_re - pi * ab_im, pr * ab_im + pi * ab_re))
    ar_ref[...], ai_ref[...] = pows[SSM_BLOCK]

    R, T, TG = SSM_BLOCK, SSM_TILES, SSM_TILE_GROUPS
    rows = SSM_GROUPS * SSM_GROUP
    side = SSM_TILE_CHANNELS
    exact = dict(precision=lax.Precision.HIGHEST, preferred_element_type=F32)
    iota = lambda shape, d: lax.broadcasted_iota(jnp.int32, shape, d)
    rep4 = lambda a, axis: jnp.concatenate([a] * TG, axis=axis)
    tiles = lambda a: a.reshape(T, side, a.shape[-1])
    group_of = lambda i: (i & (side - 1)) >> GROUP_SHIFT

    own_in = group_of(iota((rows, MXU_TILE), 0)) == iota((rows, MXU_TILE), 1) >> STATE_SHIFT
    for s in range(R):
        pr, pi = pows[R - 1 - s]
        for ref, w in ((winr_ref, pr * bb_re - pi * bb_im), (wini_ref, pr * bb_im + pi * bb_re)):
            ref[:, side * s:side * (s + 1), :] = tiles(jnp.where(own_in, rep4(w, 1), 0.0)).astype(BF16)

    eye = (iota((rows, rows), 0) == iota((rows, rows), 1)).astype(F32)
    own_out = iota((MXU_TILE, rows), 0) >> STATE_SHIFT == group_of(iota((MXU_TILE, rows), 1))
    for ref, real in ((woutr_ref, True), (wouti_ref, False)):
        per_lag = []
        for j in range(R):
            pr, pi = pows[j + 1]
            w = cr * pr - ci * pi if real else -(cr * pi + ci * pr)
            wt = lax.dot_general(w, eye, (((0,), (0,)), ((), ())), **exact)
            per_lag.append(jnp.where(own_out, rep4(wt, 0), 0.0))
        for n in range(T):
            ref[n] = jnp.concatenate([m[:, side * n:side * (n + 1)] for m in per_lag], axis=1).astype(BF16)

    grouped = lambda a: a.reshape(SSM_GROUPS, SSM_GROUP, SSM_STATE)
    contract = lambda a, b: lax.dot_general(a, b, (((2,), (2,)), ((0,), (0,))), **exact)
    own_dir = group_of(iota((rows, side), 0)) == iota((rows, side), 1) >> GROUP_SHIFT
    by_lag = []
    for d in range(R):
        pr, pi = pows[d]
        er = grouped(pr * bb_re - pi * bb_im)
        ei = grouped(pr * bb_im + pi * bb_re)
        kt = (contract(er, grouped(cr)) - contract(ei, grouped(ci))).reshape(rows, SSM_GROUP)
        by_lag.append(tiles(jnp.where(own_dir, rep4(kt, 1), 0.0)))
    nothing = jnp.zeros_like(by_lag[0])
    for s in range(R):
        kdir_ref[:, side * s:side * (s + 1), :] = jnp.concatenate(
            [by_lag[j - s] if j >= s else nothing for j in range(R)], axis=2).astype(BF16)


def _s5_block_maps(lam_re, lam_im, log_dt, b_re, b_im, c_re, c_im):
    rows = SSM_GROUPS * SSM_GROUP
    rep = lambda a: jnp.repeat(a, SSM_GROUP, axis=0)
    ldt = jnp.broadcast_to(log_dt[:, None], (SSM_GROUPS, SSM_STATE))
    bt = lambda b: b.transpose(0, 2, 1).reshape(rows, SSM_STATE)
    ct = lambda c: c.reshape(rows, SSM_STATE)
    one = jax.ShapeDtypeStruct((rows, SSM_STATE), F32)
    tile = jax.ShapeDtypeStruct((SSM_TILES, MXU_TILE, MXU_TILE), BF16)
    ar, ai, winr, wini, woutr, wouti, direct = pl.pallas_call(
        _s5_prep_kernel,
        out_shape=(one, one, tile, tile, tile, tile, tile),
        compiler_params=pltpu.CompilerParams(vmem_limit_bytes=VMEM_LIMIT),
        name="s5_block_maps",
    )(rep(lam_re), rep(lam_im), rep(ldt), bt(b_re), bt(b_im), ct(c_re), ct(c_im))
    a_re = ar[::SSM_GROUP].reshape(1, SSM_LANES)
    a_im = ai[::SSM_GROUP].reshape(1, SSM_LANES)
    return a_re, a_im, winr, wini, woutr, wouti, direct


def _retention_chunk(length):
    return RET_CHUNK if length % RET_CHUNK == 0 else length


def _log_gamma():
    return np.log(1.0 - 2.0 ** (-5.0 - np.arange(RET_HEADS, dtype=np.float64)))


def _rope_tables(length, pos0):
    half = RET_HEAD_DIM // 2
    inv = ROPE_BASE ** (-np.arange(half, dtype=np.float64) / half)
    ang = (pos0 + np.arange(length, dtype=np.float64))[:, None] * inv[None, :]
    return (np.concatenate([np.cos(ang), np.cos(ang)], axis=1),
            np.concatenate([-np.sin(ang), np.sin(ang)], axis=1))


def _zeta_rows(length):
    chunk = _retention_chunk(length)
    idx = np.arange(chunk, dtype=np.float64)
    zeta = np.exp((chunk - 1.0 - idx)[None, :] * _log_gamma()[:, None])
    return np.concatenate([np.broadcast_to(z[:, None], (chunk, RET_HEAD_DIM)) for z in zeta], axis=1)


def _decay_tables(length, padded):
    chunk = _retention_chunk(length)
    cpad = chunk if padded == length else padded
    lg = _log_gamma()
    idx = np.arange(chunk, dtype=np.float64)
    diff = idx[:, None] - idx[None, :]
    mask = np.where(diff[None] >= 0, np.exp(np.maximum(diff, 0.0)[None] * lg[:, None, None]), 0.0)
    xi = np.exp((idx + 1.0)[None, :] * lg[:, None])
    gamma_c = tuple(float(v) for v in np.exp(chunk * lg))
    mask = np.pad(mask, ((0, 0), (0, cpad - chunk), (0, cpad - chunk)))
    xi = np.broadcast_to(np.pad(xi, ((0, 0), (0, cpad - chunk)))[:, :, None], (RET_HEADS, cpad, RET_HEAD_DIM))
    f = lambda a: jnp.asarray(np.ascontiguousarray(a), dtype=F32)
    return cpad, gamma_c, f(mask), f(xi)


def _proj_kernel(x_ref, g_ref, w_ref, cc_ref, ss_ref, zeta_ref, *rest, n_cast):
    casts_in, rest = rest[:n_cast], rest[n_cast:]
    u_ref, q_ref, k_ref, kz_ref, v_ref, gate_ref = rest[:6]
    w = w_ref[...]
    if n_cast:
        w = w.astype(BF16)
        for src, dst in zip(casts_in, rest[6:]):
            dst[...] = src[...].astype(BF16)
    h = _rmsnorm(x_ref[...], g_ref[...]).astype(BF16)
    proj = _dot(h, w)
    u_ref[...] = proj[:, :SSM_WIDTH]
    v_ref[...] = proj[:, SSM_WIDTH + 2 * RET_WIDTH:SSM_WIDTH + 3 * RET_WIDTH].astype(BF16)
    gate_ref[...] = proj[:, SSM_WIDTH + 3 * RET_WIDTH:]
    cc = cc_ref[...]
    ss = ss_ref[...]
    rope = lambda a: a * cc + pltpu.roll(a, RET_HEAD_DIM // 2, axis=1) * ss
    for hd in range(RET_HEADS):
        cols = slice(hd * RET_HEAD_DIM, (hd + 1) * RET_HEAD_DIM)
        q = rope(proj[:, SSM_WIDTH + hd * RET_HEAD_DIM:SSM_WIDTH + (hd + 1) * RET_HEAD_DIM])
        k = rope(proj[:, SSM_WIDTH + RET_WIDTH + hd * RET_HEAD_DIM:SSM_WIDTH + RET_WIDTH + (hd + 1) * RET_HEAD_DIM])
        q_ref[:, cols] = (q * (RET_HEAD_DIM ** -0.5)).astype(BF16)
        k_ref[:, cols] = k.astype(BF16)
        kz_ref[:, cols] = (k * zeta_ref[:, cols]).astype(BF16)


def _project(x, g_mix, w_in, *, length, pos0, casts=()):
    t = x.shape[0]
    tm = min(t, 1024)
    steps = t // tm
    chunk = _retention_chunk(length)
    casts = (w_in, *casts) if casts else ()
    if tm % chunk or (length % tm and tm % length):
        raise ValueError("token tiles must hold whole retention chunks of whole or repeated sequences")
    cc, ss = _rope_tables(length, pos0)
    if length < tm:
        cc, ss = np.tile(cc, (tm // length, 1)), np.tile(ss, (tm // length, 1))
    tab_blocks = cc.shape[0] // tm
    zeta = np.tile(_zeta_rows(length), (tm // chunk, 1))
    const = lambda a: jnp.asarray(np.ascontiguousarray(a), dtype=F32)
    row = lambda w: pl.BlockSpec((tm, w), lambda i: (i, 0))
    tab = pl.BlockSpec((tm, RET_HEAD_DIM), lambda i: (i % tab_blocks, 0))
    f = jax.ShapeDtypeStruct((t, 512), F32)
    h = jax.ShapeDtypeStruct((t, 512), BF16)
    in_specs = [row(D_MODEL), _full((1, D_MODEL)), _full((D_MODEL, PROJ_WIDTH)), tab, tab, _full((tm, RET_WIDTH))]
    out_specs = [row(512)] * 6
    out_shape = [f, h, h, h, h, f]
    if casts:
        for c in casts:
            rows, rem = divmod(c.shape[0], steps)
            if rem or rows % BF16_ROWS:
                raise ValueError("side-cast weights must split into whole bf16 tiles per grid step")
            blk = pl.BlockSpec((rows, c.shape[1]), lambda i: (i, 0))
            in_specs.append(blk)
            out_specs.append(blk)
            out_shape.append(jax.ShapeDtypeStruct(c.shape, BF16))
    outs = pl.pallas_call(
        functools.partial(_proj_kernel, n_cast=len(casts)),
        grid=(steps,),
        in_specs=in_specs,
        out_specs=out_specs,
        out_shape=out_shape,
        compiler_params=_params("arbitrary" if casts else "parallel"),
        name="in_proj",
    )(x, g_mix, w_in, const(cc), const(ss), const(zeta), *casts)
    return outs[:6], outs[6:]


def _s5_block_kernel(u_ref, h0r_ref, h0i_ref, ar_ref, ai_ref, winr_ref, wini_ref, woutr_ref, wouti_ref,
                     kdir_ref, d_ref, wglu_ref, out_ref, hr_ref, hi_ref, vr_scr, vi_scr, xr_scr, xi_scr,
                     *, batch, steps, pair, width):
    R, T = SSM_BLOCK, SSM_TILES
    chan = SSM_TILE_CHANNELS
    blocks = steps // R
    rows, brow = steps * batch, blocks * batch

    @pl.when(pl.program_id(0) == 0)
    def _():
        hr_ref[...] = h0r_ref[...]
        hi_ref[...] = h0i_ref[...]

    u = jnp.swapaxes(u_ref[...], 0, 1).reshape(rows, SSM_WIDTH)
    u4 = u.reshape(blocks, R, batch, SSM_WIDTH)
    lag = [u4[:, s].reshape(brow, SSM_WIDTH) for s in range(R)]
    direct = []
    for n in range(T):
        un = jnp.concatenate([lag[s][:, chan * n:chan * (n + 1)] for s in range(R)], axis=1).astype(BF16)
        lanes = slice(MXU_TILE * n, MXU_TILE * (n + 1))
        vr_scr[:, lanes] = _dot(un, winr_ref[n])
        vi_scr[:, lanes] = _dot(un, wini_ref[n])
        direct.append(_dot(un, kdir_ref[n]))

    for j in range(SSM_LANES // width):
        glob = slice(j * width, (j + 1) * width)
        ar = jnp.broadcast_to(ar_ref[:, glob], (batch, width))
        ai = jnp.broadcast_to(ai_ref[:, glob], (batch, width))
        xr = hr_ref[:, glob]
        xi = hi_ref[:, glob]
        for i in range(blocks // pair):
            prev_r, prev_i = [], []
            for s in range(pair):
                at = slice((i * pair + s) * batch, (i * pair + s + 1) * batch)
                prev_r.append(xr)
                prev_i.append(xi)
                xr, xi = ar * xr - ai * xi + vr_scr[at, glob], ar * xi + ai * xr + vi_scr[at, glob]
            blk = slice(i * pair * batch, (i + 1) * pair * batch)
            xr_scr[blk, glob] = jnp.concatenate(prev_r, axis=0).astype(BF16)
            xi_scr[blk, glob] = jnp.concatenate(prev_i, axis=0).astype(BF16)
        hr_ref[:, glob] = xr
        hi_ref[:, glob] = xi

    y4 = []
    for n in range(T):
        lanes = slice(MXU_TILE * n, MXU_TILE * (n + 1))
        y4.append(_dot(xr_scr[:, lanes], woutr_ref[n]) + _dot(xi_scr[:, lanes], wouti_ref[n]) + direct[n])
    per_lag = [jnp.concatenate([y4[n][:, chan * j:chan * (j + 1)] for n in range(T)], axis=1)
               .reshape(blocks, batch, SSM_WIDTH) for j in range(R)]
    y = jnp.stack(per_lag, axis=1).reshape(rows, SSM_WIDTH) + d_ref[...] * u
    z = _gelu_tanh(y)
    out = z * _sigmoid(_dot(z.astype(BF16), wglu_ref[...]))
    out_ref[...] = jnp.swapaxes(out.reshape(steps, batch, SSM_WIDTH), 0, 1).astype(BF16)


def _s5_mixer(u, h0_re, h0_im, maps, d_skip, w_glu, *, batch, length):
    a_re, a_im, winr, wini, woutr, wouti, direct = maps
    steps = min(length, 256)
    if steps % SSM_BLOCK or length % steps:
        raise ValueError("sequence length must be a multiple of the S5 block")
    blocks = steps // SSM_BLOCK
    pair = max(1, min(blocks, BF16_ROWS // batch))
    if (pair * batch) % BF16_ROWS or blocks % pair:
        raise ValueError("S5 state stores need whole bf16 tiles")
    width = SSM_LANES // 2 if batch <= 8 else LANES
    brow = blocks * batch
    kern = functools.partial(_s5_block_kernel, batch=batch, steps=steps, pair=pair, width=width)
    st = jax.ShapeDtypeStruct((batch, SSM_LANES), F32)
    tile = _full((SSM_TILES, MXU_TILE, MXU_TILE))
    return pl.pallas_call(
        kern,
        grid=(length // steps,),
        in_specs=[pl.BlockSpec((batch, steps, SSM_WIDTH), lambda i: (0, i, 0)),
                  _full((batch, SSM_LANES)), _full((batch, SSM_LANES)),
                  _full((1, SSM_LANES)), _full((1, SSM_LANES)),
                  tile, tile, tile, tile, tile,
                  _full((1, SSM_WIDTH)), _full((SSM_WIDTH, SSM_WIDTH))],
        out_specs=[pl.BlockSpec((batch, steps, SSM_WIDTH), lambda i: (0, i, 0)),
                   _full((batch, SSM_LANES)), _full((batch, SSM_LANES))],
        out_shape=(jax.ShapeDtypeStruct((batch, length, SSM_WIDTH), BF16), st, st),
        scratch_shapes=[pltpu.VMEM((brow, SSM_LANES), F32), pltpu.VMEM((brow, SSM_LANES), F32),
                        pltpu.VMEM((brow, SSM_LANES), BF16), pltpu.VMEM((brow, SSM_LANES), BF16)],
        compiler_params=_params("arbitrary"),
        name="s5_mixer",
    )(u, h0_re, h0_im, a_re, a_im, winr, wini, woutr, wouti, direct, d_skip, w_glu)


def _ret_kernel(q_ref, k_ref, kz_ref, v_ref, g_ref, mask_ref, xi_ref, gain_ref, s0_ref, o_ref, s_ref,
                *, bb, gamma_c, unroll, chunk, chunks):
    @pl.when(pl.program_id(1) == 0)
    def _():
        s_ref[...] = s0_ref[...]

    heads = range(RET_HEADS)
    cols = [slice(h * RET_HEAD_DIM, (h + 1) * RET_HEAD_DIM) for h in heads]

    def one_chunk(b, rows):
        qb = [q_ref[b, rows, c] for c in cols]
        vb = [v_ref[b, rows, c] for c in cols]
        s_prev = [s_ref[b, h] for h in heads]
        scores = [(_dot_nt(qb[h], k_ref[b, rows, cols[h]]) * mask_ref[h]).astype(BF16) for h in heads]
        cross = [_dot(qb[h], s_prev[h].astype(BF16)) * xi_ref[h] for h in heads]
        kv = [_dot_tn(kz_ref[b, rows, cols[h]], vb[h]) for h in heads]
        inner = [_dot(scores[h], vb[h]) for h in heads]
        outs = []
        for h in heads:
            s_ref[b, h] = s_prev[h] * gamma_c[h] + kv[h]
            o = inner[h] + cross[h]
            mu = jnp.mean(o, axis=-1, keepdims=True)
            oc = o - mu
            var = jnp.mean(oc * oc, axis=-1, keepdims=True)
            on = oc * lax.rsqrt(var + EPS) * gain_ref[:, cols[h]]
            gate = g_ref[b, rows, cols[h]]
            outs.append((gate * _sigmoid(gate) * on).astype(BF16))
        o_ref[b, rows, :] = jnp.concatenate(outs, axis=1)

    def per_batch(b, carry):
        for c in range(chunks):
            one_chunk(b, slice(c * chunk, (c + 1) * chunk))
        return carry

    lax.fori_loop(0, bb, per_batch, 0, unroll=unroll)


def _retention(q, k, kz, v, g, s0, gn_gain):
    batch, length, _ = q.shape
    padded = pl.cdiv(length, BF16_ROWS) * BF16_ROWS
    chunk, gamma_c, mask, xi = _decay_tables(length, padded)
    if padded != length:
        pad = lambda a: jnp.pad(a, ((0, 0), (0, padded - length), (0, 0)))
        q, k, kz, v, g = pad(q), pad(k), pad(kz), pad(v), pad(g)
    bb = 8
    unroll = 2 if chunk > BF16_ROWS else bb
    chunks = 2 if (padded // chunk) % 2 == 0 else 1
    tok = pl.BlockSpec((bb, chunks * chunk, RET_WIDTH), lambda i, c: (i, c, 0))
    state = pl.BlockSpec((bb, RET_HEADS, RET_HEAD_DIM, RET_HEAD_DIM), lambda i, c: (i, 0, 0, 0))
    out, s_new = pl.pallas_call(
        functools.partial(_ret_kernel, bb=bb, gamma_c=gamma_c, unroll=unroll, chunk=chunk, chunks=chunks),
        grid=(batch // bb, padded // (chunks * chunk)),
        in_specs=[tok, tok, tok, tok, tok,
                  _full((RET_HEADS, chunk, chunk)), _full((RET_HEADS, chunk, RET_HEAD_DIM)),
                  _full((1, RET_WIDTH)), state],
        out_specs=[tok, state],
        out_shape=(jax.ShapeDtypeStruct((batch, padded, RET_WIDTH), BF16),
                   jax.ShapeDtypeStruct(s0.shape, F32)),
        compiler_params=_params("parallel", "arbitrary"),
        name="retention",
    )(q, k, kz, v, g, mask, xi, gn_gain, s0)
    return out[:, :length], s_new


def _mixer_out(x_ref, ssm_ref, ret_ref, wout_ref):
    return (x_ref[...] + _dot(ssm_ref[...], wout_ref[:SSM_WIDTH, :])
            + _dot(ret_ref[...], wout_ref[SSM_WIDTH:, :]))


def _query(x1, g_ref, wq_ref):
    return (_dot(_rmsnorm(x1, g_ref[...]).astype(BF16), wq_ref[...]) * MEM_SCALE).astype(BF16)


def _outq_kernel(x_ref, ssm_ref, ret_ref, wout_ref, g_ref, wq_ref, x1_ref, q_ref):
    x1 = _mixer_out(x_ref, ssm_ref, ret_ref, wout_ref)
    x1_ref[...] = x1
    q_ref[...] = _query(x1, g_ref, wq_ref)


def _out_and_query(x, ssm, ret, w_out, g_xattn, w_mq):
    t = x.shape[0]
    tm = min(t, 512)
    row = lambda w: pl.BlockSpec((tm, w), lambda i: (i, 0))
    return pl.pallas_call(
        _outq_kernel,
        grid=(t // tm,),
        in_specs=[row(D_MODEL), row(SSM_WIDTH), row(RET_WIDTH), _full((D_MODEL, D_MODEL)),
                  _full((1, D_MODEL)), _full((D_MODEL, D_MODEL))],
        out_specs=[row(D_MODEL), row(D_MODEL)],
        out_shape=(jax.ShapeDtypeStruct((t, D_MODEL), F32), jax.ShapeDtypeStruct((t, D_MODEL), BF16)),
        compiler_params=_params("parallel"),
        name="out_proj_query",
    )(x, ssm, ret, w_out, g_xattn, w_mq)


def _cache_row_order(x):
    halves = MEM_HEAD_DIM // LANES
    tiles = [x[:, (h * halves + d) * LANES:(h * halves + d + 1) * LANES]
             for d in range(halves) for h in range(MEM_HEADS)]
    return jnp.swapaxes(jnp.stack(tiles, axis=0), 0, 1).reshape(x.shape[0] * CACHE_SUB, LANES)


def _memkv_kernel(m_ref, g_ref, wk_ref, wv_ref, k_ref, v_ref, kb_ref, vb_ref):
    m = _rmsnorm(m_ref[0], g_ref[...]).astype(BF16)
    k = _dot(m, wk_ref[...])
    v = _dot(m, wv_ref[...])
    k_ref[0] = _cache_row_order(k)
    v_ref[0] = _cache_row_order(v)
    kb_ref[0] = k.astype(BF16)
    vb_ref[0] = v.astype(BF16)


def _memory_kv(mem, g_mem, w_mk, w_mv):
    batch = mem.shape[0]
    tok = pl.BlockSpec((1, MEM_LEN, D_MODEL), lambda i: (i, 0, 0))
    cache = pl.BlockSpec((1, CACHE_ROWS, LANES), lambda i: (i, 0, 0))
    f = jax.ShapeDtypeStruct((batch, CACHE_ROWS, LANES), F32)
    h = jax.ShapeDtypeStruct((batch, MEM_LEN, D_MODEL), BF16)
    return pl.pallas_call(
        _memkv_kernel,
        grid=(batch,),
        in_specs=[tok, _full((1, D_MODEL)), _full((D_MODEL, D_MODEL)), _full((D_MODEL, D_MODEL))],
        out_specs=[cache, cache, tok, tok],
        out_shape=(f, f, h, h),
        compiler_params=_params("parallel"),
        name="memory_kv",
    )(mem, g_mem, w_mk, w_mv)


def _outattn_kernel(x_ref, ssm_ref, ret_ref, wout_ref, g_ref, wq_ref, k_ref, v_ref, x1_ref, o_ref):
    x1 = _mixer_out(x_ref, ssm_ref, ret_ref, wout_ref)
    x1_ref[...] = x1
    q = _query(x1, g_ref, wq_ref)
    cols = [slice(h * MEM_HEAD_DIM, (h + 1) * MEM_HEAD_DIM) for h in range(MEM_HEADS)]
    scores = [_dot_nt(q[:, c], k_ref[0, :, c]) for c in cols]
    probs = []
    for s in scores:
        e = jnp.exp(s - jnp.max(s, axis=-1, keepdims=True))
        probs.append((e / jnp.sum(e, axis=-1, keepdims=True)).astype(BF16))
    o_ref[...] = jnp.concatenate([_dot(p, v_ref[0, :, c]).astype(BF16) for p, c in zip(probs, cols)], axis=1)


def _out_and_attention(x, ssm, ret, w_out, g_xattn, w_mq, mk, mv, *, length):
    t = x.shape[0]
    tm = min(length, 1024)
    per_seq = length // tm
    row = lambda w: pl.BlockSpec((tm, w), lambda i: (i, 0))
    mem = pl.BlockSpec((1, MEM_LEN, D_MODEL), lambda i: (i // per_seq, 0, 0))
    return pl.pallas_call(
        _outattn_kernel,
        grid=(t // tm,),
        in_specs=[row(D_MODEL), row(SSM_WIDTH), row(RET_WIDTH), _full((D_MODEL, D_MODEL)),
                  _full((1, D_MODEL)), _full((D_MODEL, D_MODEL)), mem, mem],
        out_specs=[row(D_MODEL), row(D_MODEL)],
        out_shape=(jax.ShapeDtypeStruct((t, D_MODEL), F32), jax.ShapeDtypeStruct((t, D_MODEL), BF16)),
        compiler_params=_params("parallel"),
        name="out_proj_attention",
    )(x, ssm, ret, w_out, g_xattn, w_mq, mk, mv)


def _cached_attention(q_ref, k_ref, v_ref, o_ref, *, bb, length, between=None):
    hit = (lax.broadcasted_iota(jnp.int32, (CACHE_SUB, CACHE_ROWS), 0)
           == lax.broadcasted_iota(jnp.int32, (CACHE_SUB, CACHE_ROWS), 1) % CACHE_SUB)
    first_half = lax.broadcasted_iota(jnp.int32, (length, LANES), 1) % CACHE_SUB < MEM_HEADS
    tiles = CACHE_ROWS // LANES

    def class_reduce(x, op):
        shift = CACHE_SUB
        while shift < LANES:
            x = op(x, pltpu.roll(x, shift, axis=1))
            shift *= 2
        return x

    def softmax_rows(r):
        z = jnp.concatenate(
            [jnp.sum(jnp.where(hit, r[CACHE_SUB * t:CACHE_SUB * (t + 1)], 0.0), axis=0, keepdims=True)
             for t in range(length)], axis=0)
        parts = []
        for i in range(tiles):
            zi = z[:, LANES * i:LANES * (i + 1)]
            parts.append(zi + pltpu.roll(zi, LANES - MEM_HEADS, axis=1))
        mx = parts[0]
        for pi in parts[1:]:
            mx = jnp.maximum(mx, pi)
        mx = class_reduce(mx, jnp.maximum)
        es = [jnp.exp(pi - mx) for pi in parts]
        tot = es[0]
        for ei in es[1:]:
            tot = tot + ei
        tot = class_reduce(tot, jnp.add)
        ps = []
        for ei in es:
            pi = ei / tot
            ps.append(jnp.where(first_half, pi, pltpu.roll(pi, MEM_HEADS, axis=1)))
        p = jnp.concatenate(ps, axis=1)
        return jnp.concatenate(
            [jnp.where(hit, jnp.broadcast_to(p[t:t + 1], (CACHE_SUB, CACHE_ROWS)), 0.0)
             for t in range(length)], axis=0).astype(BF16)

    scores = [_dot_nt(q_ref[b], k_ref[b].astype(BF16)) for b in range(bb)]
    if between is not None:
        between()
    probs = [softmax_rows(r) for r in scores]
    for b in range(bb):
        o_ref[b] = _dot(probs[b], v_ref[b].astype(BF16)).astype(BF16)


def _to_cache_rows(a, lead):
    halves = MEM_HEAD_DIM // LANES
    batch = a.size // (lead * D_MODEL)
    a = a.reshape(batch, lead, MEM_HEADS, halves, LANES).transpose(0, 1, 3, 2, 4)
    return a.reshape(batch, lead * CACHE_SUB, LANES)


def _from_cache_rows(a, lead):
    halves = MEM_HEAD_DIM // LANES
    batch = a.shape[0]
    a = a.reshape(batch, lead, halves, MEM_HEADS, LANES).transpose(0, 1, 3, 2, 4)
    return a.reshape(batch, lead, D_MODEL)


MLP_CHUNK = 1024


def _post_kernel(x1_ref, o_ref, wo_ref, gm_ref, wup_ref, wdn_ref, gf_ref, *rest, side_bb, side_len):
    if side_bb:
        qs_ref, ck_ref, cv_ref, y_ref, os_ref = rest
    else:
        (y_ref,) = rest
    x2 = x1_ref[...] + _dot(o_ref[...], wo_ref[...])
    result = []

    def mlp():
        h = _rmsnorm(x2, gm_ref[...]).astype(BF16)
        chunks = [slice(c, c + MLP_CHUNK) for c in range(0, D_FF, MLP_CHUNK)]
        acc = x2
        up = _dot(h, wup_ref[:, chunks[0]])
        for c, cols in enumerate(chunks):
            nxt = _dot(h, wup_ref[:, chunks[c + 1]]) if c + 1 < len(chunks) else None
            a = jnp.maximum(up, 0.0)
            acc = acc + _dot((a * a).astype(BF16), wdn_ref[cols, :])
            up = nxt
        result.append(acc)

    if side_bb:
        _cached_attention(qs_ref, ck_ref, cv_ref, os_ref, bb=side_bb, length=side_len, between=mlp)
    else:
        mlp()
    y_ref[...] = _rmsnorm(result[0], gf_ref[...])


def _attn_out_mlp(x1, o, w_mo, g_mlp, w_up, w_down, g_final, side=None):
    t = x1.shape[0]
    tm = min(t, 512)
    steps = t // tm
    row = pl.BlockSpec((tm, D_MODEL), lambda i: (i, 0))
    in_specs = [row, row, _full((D_MODEL, D_MODEL)), _full((1, D_MODEL)),
                _full((D_MODEL, D_FF)), _full((D_FF, D_MODEL)), _full((1, D_MODEL))]
    out_specs = [row]
    out_shape = [jax.ShapeDtypeStruct((t, D_MODEL), F32)]
    args = [x1, o, w_mo, g_mlp, w_up, w_down, g_final]
    side_bb = side_len = 0
    if side is not None:
        q_rows = side[0]
        side_bb, rem = divmod(q_rows.shape[0], steps)
        if rem or not side_bb:
            raise ValueError("side attention sequences must spread evenly over the grid steps")
        side_len = q_rows.shape[1] // CACHE_SUB
        blk = lambda r: pl.BlockSpec((side_bb, r, LANES), lambda i: (i, 0, 0))
        in_specs += [blk(q_rows.shape[1]), blk(CACHE_ROWS), blk(CACHE_ROWS)]
        out_specs.append(blk(q_rows.shape[1]))
        out_shape.append(jax.ShapeDtypeStruct(q_rows.shape, BF16))
        args += list(side)
    outs = pl.pallas_call(
        functools.partial(_post_kernel, side_bb=side_bb, side_len=side_len),
        grid=(steps,),
        in_specs=in_specs,
        out_specs=out_specs,
        out_shape=out_shape,
        compiler_params=_params("parallel"),
        name="attn_out_mlp",
    )(*args)
    return outs if side is not None else outs[0]


def _mixers(proj, batch, length, s5_re, s5_im, ret_s, p):
    tokens = batch * length
    u, q, k, kz, v, g = proj
    ssm, s5_re_new, s5_im_new = _s5_mixer(
        u.reshape(batch, length, SSM_WIDTH), s5_re.reshape(batch, SSM_LANES), s5_im.reshape(batch, SSM_LANES),
        p["s5_maps"], p["d_skip"], p["w_glu"], batch=batch, length=length)
    tok3 = lambda a: a.reshape(batch, length, RET_WIDTH)
    ret, ret_new = _retention(tok3(q), tok3(k), tok3(kz), tok3(v), tok3(g), ret_s, p["ret_gn"])
    states = (s5_re_new.reshape(1, batch, SSM_GROUPS, SSM_STATE),
              s5_im_new.reshape(1, batch, SSM_GROUPS, SSM_STATE), ret_new[None])
    return ssm.reshape(tokens, SSM_WIDTH), ret.reshape(tokens, RET_WIDTH), states


def kernel(x_prompt, x_sample, mem_prompt, state_s5_re, state_s5_im, state_ret, cache_mem_k, cache_mem_v, g_mix, w_in, lam_re, lam_im, log_dt, b_re, b_im, c_re, c_im, d_skip, w_glu, ret_gn, w_out, g_xattn, g_mem, w_mq, w_mk, w_mv, w_mo, g_mlp, w_up, w_down, g_final):
    if g_mix.shape[0] != 1:
        raise ValueError("this kernel implements the single-layer configuration")
    bp, lp, _ = x_prompt.shape
    bs, ls, _ = x_sample.shape
    vec = lambda a: a.reshape(1, -1).astype(F32)
    xp = x_prompt.reshape(bp * lp, D_MODEL)
    xs = x_sample.reshape(bs * ls, D_MODEL)
    proj_p, (w_in_b, w_up_b, w_down_b) = _project(
        xp, vec(g_mix[0]), w_in[0], length=lp, pos0=0.0, casts=(w_up[0], w_down[0]))
    w_out_b, w_mq_b, w_mo_b, w_mk_b, w_mv_b, w_glu_b = (
        w[0].astype(BF16) for w in (w_out, w_mq, w_mo, w_mk, w_mv, w_glu))
    proj_s, _ = _project(xs, vec(g_mix[0]), w_in_b, length=ls, pos0=float(PAST_LEN))
    p = dict(
        s5_maps=_s5_block_maps(lam_re[0], lam_im[0], log_dt[0], b_re[0], b_im[0], c_re[0], c_im[0]),
        d_skip=vec(d_skip[0]), w_glu=w_glu_b, ret_gn=vec(ret_gn[0]))
    g_x = vec(g_xattn[0])
    mlp = (w_mo_b, vec(g_mlp[0]), w_up_b, w_down_b, vec(g_final))

    ssm_s, ret_s, states_s = _mixers(proj_s, bs, ls, state_s5_re[0], state_s5_im[0], state_ret[0], p)
    x1_s, q_s = _out_and_query(xs, ssm_s, ret_s, w_out_b, g_x, w_mq_b)
    if (ls * CACHE_SUB) % BF16_ROWS:
        raise ValueError("cached attention needs whole bf16 tiles of query rows")
    side = (_to_cache_rows(q_s, ls), _to_cache_rows(cache_mem_k, MEM_LEN), _to_cache_rows(cache_mem_v, MEM_LEN))

    mk_rows, mv_rows, mk_p, mv_p = _memory_kv(mem_prompt, vec(g_mem[0]), w_mk_b, w_mv_b)
    zs = jnp.zeros((bp, SSM_GROUPS, SSM_STATE), F32)
    zr = jnp.zeros((bp, RET_HEADS, RET_HEAD_DIM, RET_HEAD_DIM), F32)
    ssm_p, ret_p, states_p = _mixers(proj_p, bp, lp, zs, zs, zr, p)
    x1_p, o_p = _out_and_attention(xp, ssm_p, ret_p, w_out_b, g_x, w_mq_b, mk_p, mv_p, length=lp)
    y_p, o_s = _attn_out_mlp(x1_p, o_p, *mlp, side=side)
    y_s = _attn_out_mlp(x1_s, _from_cache_rows(o_s, ls).reshape(bs * ls, D_MODEL), *mlp)

    kv5 = lambda a: _from_cache_rows(a, MEM_LEN).reshape(1, bp, MEM_LEN, MEM_HEADS, MEM_HEAD_DIM)
    return (y_p.reshape(bp, lp, D_MODEL), y_s.reshape(bs, ls, D_MODEL), *states_p, kv5(mk_rows), kv5(mv_rows),
            *states_s)
```

```python
import functools
import math

import numpy as np
import jax
import jax.numpy as jnp
from jax import lax
from jax.experimental import pallas as pl
from jax.experimental.pallas import tpu as pltpu

F32 = jnp.float32
BF16 = jnp.bfloat16

D_MODEL = 1024
SSM_WIDTH = 512
SSM_GROUP = 16
SSM_GROUPS = 32
SSM_STATE = 64
SSM_LANES = SSM_GROUPS * SSM_STATE
RET_WIDTH = 512
RET_HEADS = 4
RET_HEAD_DIM = 128
RET_CHUNK = 128
ROPE_BASE = 10000.0
MEM_LEN = 256
MEM_HEADS = 4
MEM_HEAD_DIM = 256
MEM_SCALE = MEM_HEAD_DIM ** -0.5
D_FF = 4096
PROJ_WIDTH = SSM_WIDTH + 4 * RET_WIDTH
EPS = 1e-6
PAST_LEN = 16384

LANES = 128
MXU_TILE = 256
BF16_ROWS = 16
VMEM_LIMIT = 56 * 1024 * 1024

CACHE_ROWS = MEM_LEN * MEM_HEADS * MEM_HEAD_DIM // LANES
CACHE_SUB = MEM_HEADS * MEM_HEAD_DIM // LANES

assert math.frexp(MEM_SCALE)[0] == 0.5


def _dot(a, b):
    return jnp.dot(a, b, preferred_element_type=F32)


def _dot_nt(a, b):
    return lax.dot_general(a, b, (((1,), (1,)), ((), ())), preferred_element_type=F32)


def _dot_tn(a, b):
    return lax.dot_general(a, b, (((0,), (0,)), ((), ())), preferred_element_type=F32)


def _rmsnorm(x, g):
    return x * lax.rsqrt(jnp.mean(x * x, axis=-1, keepdims=True) + EPS) * g


def _sigmoid(x):
    return 1.0 / (1.0 + jnp.exp(-x))


def _gelu_tanh(x):
    c = math.sqrt(2.0 / math.pi)
    return x * (0.5 * jnp.tanh(x * (c + (c * 0.044715) * (x * x))) + 0.5)


def _params(*sem):
    return pltpu.CompilerParams(dimension_semantics=sem, vmem_limit_bytes=VMEM_LIMIT)


def _full(shape):
    return pl.BlockSpec(shape, lambda *_: (0,) * len(shape))


SSM_BLOCK = 4
SSM_TILE_GROUPS = MXU_TILE // SSM_STATE
SSM_TILES = SSM_GROUPS // SSM_TILE_GROUPS
SSM_TILE_CHANNELS = SSM_TILE_GROUPS * SSM_GROUP
GROUP_SHIFT = SSM_GROUP.bit_length() - 1
STATE_SHIFT = SSM_STATE.bit_length() - 1

assert SSM_BLOCK * SSM_TILE_CHANNELS == MXU_TILE and SSM_TILE_CHANNELS == SSM_STATE


def _s5_prep_kernel(lr_ref, li_ref, ldt_ref, br_ref, bi_ref, cr_ref, ci_ref,
                    ar_ref, ai_ref, winr_ref, wini_ref, woutr_ref, wouti_ref, kdir_ref):
    lr = lr_ref[...]
    li = li_ref[...]
    dt = jnp.exp(ldt_ref[...])
    mag = jnp.exp(lr * dt)
    ab_re = mag * jnp.cos(li * dt)
    ab_im = mag * jnp.sin(li * dt)
    den = lr * lr + li * li
    f_re = ((ab_re - 1.0) * lr + ab_im * li) / den
    f_im = (ab_im * lr - (ab_re - 1.0) * li) / den
    bb_re = f_re * br_ref[...] - f_im * bi_ref[...]
    bb_im = f_re * bi_ref[...] + f_im * br_ref[...]
    cr = cr_ref[...]
    ci = ci_ref[...]
    pows = [(jnp.ones_like(ab_re), jnp.zeros_like(ab_re))]
    for _ in range(SSM_BLOCK):
        pr, pi = pows[-1]
        pows.append((pr * ab_re - pi * ab_im, pr * ab_im + pi * ab_re))
    ar_ref[...], ai_ref[...] = pows[SSM_BLOCK]

    R, T, TG = SSM_BLOCK, SSM_TILES, SSM_TILE_GROUPS
    rows = SSM_GROUPS * SSM_GROUP
    side = SSM_TILE_CHANNELS
    exact = dict(precision=lax.Precision.HIGHEST, preferred_element_type=F32)
    iota = lambda shape, d: lax.broadcasted_iota(jnp.int32, shape, d)
    rep4 = lambda a, axis: jnp.concatenate([a] * TG, axis=axis)
    tiles = lambda a: a.reshape(T, side, a.shape[-1])
    group_of = lambda i: (i & (side - 1)) >> GROUP_SHIFT

    own_in = group_of(iota((rows, MXU_TILE), 0)) == iota((rows, MXU_TILE), 1) >> STATE_SHIFT
    for s in range(R):
        pr, pi = pows[R - 1 - s]
        for ref, w in ((winr_ref, pr * bb_re - pi * bb_im), (wini_ref, pr * bb_im + pi * bb_re)):
            ref[:, side * s:side * (s + 1), :] = tiles(jnp.where(own_in, rep4(w, 1), 0.0)).astype(BF16)

    eye = (iota((rows, rows), 0) == iota((rows, rows), 1)).astype(F32)
    own_out = iota((MXU_TILE, rows), 0) >> STATE_SHIFT == group_of(iota((MXU_TILE, rows), 1))
    for ref, real in ((woutr_ref, True), (wouti_ref, False)):
        per_lag = []
        for j in range(R):
            pr, pi = pows[j + 1]
            w = cr * pr - ci * pi if real else -(cr * pi + ci * pr)
            wt = lax.dot_general(w, eye, (((0,), (0,)), ((), ())), **exact)
            per_lag.append(jnp.where(own_out, rep4(wt, 0), 0.0))
        for n in range(T):
            ref[n] = jnp.concatenate([m[:, side * n:side * (n + 1)] for m in per_lag], axis=1).astype(BF16)

    grouped = lambda a: a.reshape(SSM_GROUPS, SSM_GROUP, SSM_STATE)
    contract = lambda a, b: lax.dot_general(a, b, (((2,), (2,)), ((0,), (0,))), **exact)
    own_dir = group_of(iota((rows, side), 0)) == iota((rows, side), 1) >> GROUP_SHIFT
    by_lag = []
    for d in range(R):
        pr, pi = pows[d]
        er = grouped(pr * bb_re - pi * bb_im)
        ei = grouped(pr * bb_im + pi * bb_re)
        kt = (contract(er, grouped(cr)) - contract(ei, grouped(ci))).reshape(rows, SSM_GROUP)
        by_lag.append(tiles(jnp.where(own_dir, rep4(kt, 1), 0.0)))
    nothing = jnp.zeros_like(by_lag[0])
    for s in range(R):
        kdir_ref[:, side * s:side * (s + 1), :] = jnp.concatenate(
            [by_lag[j - s] if j >= s else nothing for j in range(R)], axis=2).astype(BF16)


def _s5_block_maps(lam_re, lam_im, log_dt, b_re, b_im, c_re, c_im):
    rows = SSM_GROUPS * SSM_GROUP
    rep = lambda a: jnp.repeat(a, SSM_GROUP, axis=0)
    ldt = jnp.broadcast_to(log_dt[:, None], (SSM_GROUPS, SSM_STATE))
    bt = lambda b: b.transpose(0, 2, 1).reshape(rows, SSM_STATE)
    ct = lambda c: c.reshape(rows, SSM_STATE)
    one = jax.ShapeDtypeStruct((rows, SSM_STATE), F32)
    tile = jax.ShapeDtypeStruct((SSM_TILES, MXU_TILE, MXU_TILE), BF16)
    ar, ai, winr, wini, woutr, wouti, direct = pl.pallas_call(
        _s5_prep_kernel,
        out_shape=(one, one, tile, tile, tile, tile, tile),
        compiler_params=pltpu.CompilerParams(vmem_limit_bytes=VMEM_LIMIT),
        name="s5_block_maps",
    )(rep(lam_re), rep(lam_im), rep(ldt), bt(b_re), bt(b_im), ct(c_re), ct(c_im))
    a_re = ar[::SSM_GROUP].reshape(1, SSM_LANES)
    a_im = ai[::SSM_GROUP].reshape(1, SSM_LANES)
    return a_re, a_im, winr, wini, woutr, wouti, direct


def _retention_chunk(length):
    return RET_CHUNK if length % RET_CHUNK == 0 else length


def _log_gamma():
    return np.log(1.0 - 2.0 ** (-5.0 - np.arange(RET_HEADS, dtype=np.float64)))


def _rope_tables(length, pos0):
    half = RET_HEAD_DIM // 2
    inv = ROPE_BASE ** (-np.arange(half, dtype=np.float64) / half)
    ang = (pos0 + np.arange(length, dtype=np.float64))[:, None] * inv[None, :]
    return (np.concatenate([np.cos(ang), np.cos(ang)], axis=1),
            np.concatenate([-np.sin(ang), np.sin(ang)], axis=1))


def _zeta_rows(length):
    chunk = _retention_chunk(length)
    idx = np.arange(chunk, dtype=np.float64)
    zeta = np.exp((chunk - 1.0 - idx)[None, :] * _log_gamma()[:, None])
    return np.concatenate([np.broadcast_to(z[:, None], (chunk, RET_HEAD_DIM)) for z in zeta], axis=1)


def _decay_tables(length, padded):
    chunk = _retention_chunk(length)
    cpad = chunk if padded == length else padded
    lg = _log_gamma()
    idx = np.arange(chunk, dtype=np.float64)
    diff = idx[:, None] - idx[None, :]
    mask = np.where(diff[None] >= 0, np.exp(np.maximum(diff, 0.0)[None] * lg[:, None, None]), 0.0)
    xi = np.exp((idx + 1.0)[None, :] * lg[:, None])
    gamma_c = tuple(float(v) for v in np.exp(chunk * lg))
    mask = np.pad(mask, ((0, 0), (0, cpad - chunk), (0, cpad - chunk)))
    xi = np.broadcast_to(np.pad(xi, ((0, 0), (0, cpad - chunk)))[:, :, None], (RET_HEADS, cpad, RET_HEAD_DIM))
    f = lambda a: jnp.asarray(np.ascontiguousarray(a), dtype=F32)
    return cpad, gamma_c, f(mask), f(xi)


def _proj_kernel(x_ref, g_ref, w_ref, cc_ref, ss_ref, zeta_ref, *rest, n_cast):
    casts_in, rest = rest[:n_cast], rest[n_cast:]
    u_ref, q_ref, k_ref, kz_ref, v_ref, gate_ref = rest[:6]
    w = w_ref[...]
    if n_cast:
        w = w.astype(BF16)
        for src, dst in zip(casts_in, rest[6:]):
            dst[...] = src[...].astype(BF16)
    h = _rmsnorm(x_ref[...], g_ref[...]).astype(BF16)
    proj = _dot(h, w)
    u_ref[...] = proj[:, :SSM_WIDTH]
    v_ref[...] = proj[:, SSM_WIDTH + 2 * RET_WIDTH:SSM_WIDTH + 3 * RET_WIDTH].astype(BF16)
    gate_ref[...] = proj[:, SSM_WIDTH + 3 * RET_WIDTH:]
    cc = cc_ref[...]
    ss = ss_ref[...]
    rope = lambda a: a * cc + pltpu.roll(a, RET_HEAD_DIM // 2, axis=1) * ss
    for hd in range(RET_HEADS):
        cols = slice(hd * RET_HEAD_DIM, (hd + 1) * RET_HEAD_DIM)
        q = rope(proj[:, SSM_WIDTH + hd * RET_HEAD_DIM:SSM_WIDTH + (hd + 1) * RET_HEAD_DIM])
        k = rope(proj[:, SSM_WIDTH + RET_WIDTH + hd * RET_HEAD_DIM:SSM_WIDTH + RET_WIDTH + (hd + 1) * RET_HEAD_DIM])
        q_ref[:, cols] = (q * (RET_HEAD_DIM ** -0.5)).astype(BF16)
        k_ref[:, cols] = k.astype(BF16)
        kz_ref[:, cols] = (k * zeta_ref[:, cols]).astype(BF16)


def _project(x, g_mix, w_in, *, length, pos0, casts=()):
    t = x.shape[0]
    tm = min(t, 1024)
    steps = t // tm
    chunk = _retention_chunk(length)
    casts = (w_in, *casts) if casts else ()
    if tm % chunk or (length % tm and tm % length):
        raise ValueError("token tiles must hold whole retention chunks of whole or repeated sequences")
    cc, ss = _rope_tables(length, pos0)
    if length < tm:
        cc, ss = np.tile(cc, (tm // length, 1)), np.tile(ss, (tm // length, 1))
    tab_blocks = cc.shape[0] // tm
    zeta = np.tile(_zeta_rows(length), (tm // chunk, 1))
    const = lambda a: jnp.asarray(np.ascontiguousarray(a), dtype=F32)
    row = lambda w: pl.BlockSpec((tm, w), lambda i: (i, 0))
    tab = pl.BlockSpec((tm, RET_HEAD_DIM), lambda i: (i % tab_blocks, 0))
    f = jax.ShapeDtypeStruct((t, 512), F32)
    h = jax.ShapeDtypeStruct((t, 512), BF16)
    in_specs = [row(D_MODEL), _full((1, D_MODEL)), _full((D_MODEL, PROJ_WIDTH)), tab, tab, _full((tm, RET_WIDTH))]
    out_specs = [row(512)] * 6
    out_shape = [f, h, h, h, h, f]
    if casts:
        for c in casts:
            rows, rem = divmod(c.shape[0], steps)
            if rem or rows % BF16_ROWS:
                raise ValueError("side-cast weights must split into whole bf16 tiles per grid step")
            blk = pl.BlockSpec((rows, c.shape[1]), lambda i: (i, 0))
            in_specs.append(blk)
            out_specs.append(blk)
            out_shape.append(jax.ShapeDtypeStruct(c.shape, BF16))
    outs = pl.pallas_call(
        functools.partial(_proj_kernel, n_cast=len(casts)),
        grid=(steps,),
        in_specs=in_specs,
        out_specs=out_specs,
        out_shape=out_shape,
        compiler_params=_params("arbitrary" if casts else "parallel"),
        name="in_proj",
    )(x, g_mix, w_in, const(cc), const(ss), const(zeta), *casts)
    return outs[:6], outs[6:]


def _s5_block_kernel(u_ref, h0r_ref, h0i_ref, ar_ref, ai_ref, winr_ref, wini_ref, woutr_ref, wouti_ref,
                     kdir_ref, d_ref, wglu_ref, out_ref, hr_ref, hi_ref, vr_scr, vi_scr, xr_scr, xi_scr,
                     *, batch, steps, pair, width):
    R, T = SSM_BLOCK, SSM_TILES
    chan = SSM_TILE_CHANNELS
    blocks = steps // R
    rows, brow = steps * batch, blocks * batch

    @pl.when(pl.program_id(0) == 0)
    def _():
        hr_ref[...] = h0r_ref[...]
        hi_ref[...] = h0i_ref[...]

    u = jnp.swapaxes(u_ref[...], 0, 1).reshape(rows, SSM_WIDTH)
    u4 = u.reshape(blocks, R, batch, SSM_WIDTH)
    lag = [u4[:, s].reshape(brow, SSM_WIDTH) for s in range(R)]
    direct = []
    for n in range(T):
        un = jnp.concatenate([lag[s][:, chan * n:chan * (n + 1)] for s in range(R)], axis=1).astype(BF16)
        lanes = slice(MXU_TILE * n, MXU_TILE * (n + 1))
        vr_scr[:, lanes] = _dot(un, winr_ref[n])
        vi_scr[:, lanes] = _dot(un, wini_ref[n])
        direct.append(_dot(un, kdir_ref[n]))

    for j in range(SSM_LANES // width):
        glob = slice(j * width, (j + 1) * width)
        ar = jnp.broadcast_to(ar_ref[:, glob], (batch, width))
        ai = jnp.broadcast_to(ai_ref[:, glob], (batch, width))
        xr = hr_ref[:, glob]
        xi = hi_ref[:, glob]
        for i in range(blocks // pair):
            prev_r, prev_i = [], []
            for s in range(pair):
                at = slice((i * pair + s) * batch, (i * pair + s + 1) * batch)
                prev_r.append(xr)
                prev_i.append(xi)
                xr, xi = ar * xr - ai * xi + vr_scr[at, glob], ar * xi + ai * xr + vi_scr[at, glob]
            blk = slice(i * pair * batch, (i + 1) * pair * batch)
            xr_scr[blk, glob] = jnp.concatenate(prev_r, axis=0).astype(BF16)
            xi_scr[blk, glob] = jnp.concatenate(prev_i, axis=0).astype(BF16)
        hr_ref[:, glob] = xr
        hi_ref[:, glob] = xi

    y4 = []
    for n in range(T):
        lanes = slice(MXU_TILE * n, MXU_TILE * (n + 1))
        y4.append(_dot(xr_scr[:, lanes], woutr_ref[n]) + _dot(xi_scr[:, lanes], wouti_ref[n]) + direct[n])
    per_lag = [jnp.concatenate([y4[n][:, chan * j:chan * (j + 1)] for n in range(T)], axis=1)
               .reshape(blocks, batch, SSM_WIDTH) for j in range(R)]
    y = jnp.stack(per_lag, axis=1).reshape(rows, SSM_WIDTH) + d_ref[...] * u
    z = _gelu_tanh(y)
    out = z * _sigmoid(_dot(z.astype(BF16), wglu_ref[...]))
    out_ref[...] = jnp.swapaxes(out.reshape(steps, batch, SSM_WIDTH), 0, 1).astype(BF16)


def _s5_mixer(u, h0_re, h0_im, maps, d_skip, w_glu, *, batch, length):
    a_re, a_im, winr, wini, woutr, wouti, direct = maps
    steps = min(length, 256)
    if steps % SSM_BLOCK or length % steps:
        raise ValueError("sequence length must be a multiple of the S5 block")
    blocks = steps // SSM_BLOCK
    pair = max(1, min(blocks, BF16_ROWS // batch))
    if (pair * batch) % BF16_ROWS or blocks % pair:
        raise ValueError("S5 state stores need whole bf16 tiles")
    width = SSM_LANES // 2 if batch <= 8 else LANES
    brow = blocks * batch
    kern = functools.partial(_s5_block_kernel, batch=batch, steps=steps, pair=pair, width=width)
    st = jax.ShapeDtypeStruct((batch, SSM_LANES), F32)
    tile = _full((SSM_TILES, MXU_TILE, MXU_TILE))
    return pl.pallas_call(
        kern,
        grid=(length // steps,),
        in_specs=[pl.BlockSpec((batch, steps, SSM_WIDTH), lambda i: (0, i, 0)),
                  _full((batch, SSM_LANES)), _full((batch, SSM_LANES)),
                  _full((1, SSM_LANES)), _full((1, SSM_LANES)),
                  tile, tile, tile, tile, tile,
                  _full((1, SSM_WIDTH)), _full((SSM_WIDTH, SSM_WIDTH))],
        out_specs=[pl.BlockSpec((batch, steps, SSM_WIDTH), lambda i: (0, i, 0)),
                   _full((batch, SSM_LANES)), _full((batch, SSM_LANES))],
        out_shape=(jax.ShapeDtypeStruct((batch, length, SSM_WIDTH), BF16), st, st),
        scratch_shapes=[pltpu.VMEM((brow, SSM_LANES), F32), pltpu.VMEM((brow, SSM_LANES), F32),
                        pltpu.VMEM((brow, SSM_LANES), BF16), pltpu.VMEM((brow, SSM_LANES), BF16)],
        compiler_params=_params("arbitrary"),
        name="s5_mixer",
    )(u, h0_re, h0_im, a_re, a_im, winr, wini, woutr, wouti, direct, d_skip, w_glu)


def _ret_kernel(q_ref, k_ref, kz_ref, v_ref, g_ref, mask_ref, xi_ref, gain_ref, s0_ref, o_ref, s_ref,
                *, bb, gamma_c, unroll, chunk, chunks):
    @pl.when(pl.program_id(1) == 0)
    def _():
        s_ref[...] = s0_ref[...]

    heads = range(RET_HEADS)
    cols = [slice(h * RET_HEAD_DIM, (h + 1) * RET_HEAD_DIM) for h in heads]

    def one_chunk(b, rows):
        qb = [q_ref[b, rows, c] for c in cols]
        vb = [v_ref[b, rows, c] for c in cols]
        s_prev = [s_ref[b, h] for h in heads]
        scores = [(_dot_nt(qb[h], k_ref[b, rows, cols[h]]) * mask_ref[h]).astype(BF16) for h in heads]
        cross = [_dot(qb[h], s_prev[h].astype(BF16)) * xi_ref[h] for h in heads]
        kv = [_dot_tn(kz_ref[b, rows, cols[h]], vb[h]) for h in heads]
        inner = [_dot(scores[h], vb[h]) for h in heads]
        outs = []
        for h in heads:
            s_ref[b, h] = s_prev[h] * gamma_c[h] + kv[h]
            o = inner[h] + cross[h]
            mu = jnp.mean(o, axis=-1, keepdims=True)
            oc = o - mu
            var = jnp.mean(oc * oc, axis=-1, keepdims=True)
            on = oc * lax.rsqrt(var + EPS) * gain_ref[:, cols[h]]
            gate = g_ref[b, rows, cols[h]]
            outs.append((gate * _sigmoid(gate) * on).astype(BF16))
        o_ref[b, rows, :] = jnp.concatenate(outs, axis=1)

    def per_batch(b, carry):
        for c in range(chunks):
            one_chunk(b, slice(c * chunk, (c + 1) * chunk))
        return carry

    if unroll == bb:
        for b in range(bb):
            per_batch(b, 0)
    else:
        lax.fori_loop(0, bb, per_batch, 0, unroll=unroll)


def _retention(q, k, kz, v, g, s0, gn_gain):
    batch, length, _ = q.shape
    padded = pl.cdiv(length, BF16_ROWS) * BF16_ROWS
    chunk, gamma_c, mask, xi = _decay_tables(length, padded)
    if padded != length:
        pad = lambda a: jnp.pad(a, ((0, 0), (0, padded - length), (0, 0)))
        q, k, kz, v, g = pad(q), pad(k), pad(kz), pad(v), pad(g)
    bb, unroll = (8, 2) if chunk > BF16_ROWS else (16, 16)
    chunks = 2 if (padded // chunk) % 2 == 0 else 1
    tok = pl.BlockSpec((bb, chunks * chunk, RET_WIDTH), lambda i, c: (i, c, 0))
    state = pl.BlockSpec((bb, RET_HEADS, RET_HEAD_DIM, RET_HEAD_DIM), lambda i, c: (i, 0, 0, 0))
    out, s_new = pl.pallas_call(
        functools.partial(_ret_kernel, bb=bb, gamma_c=gamma_c, unroll=unroll, chunk=chunk, chunks=chunks),
        grid=(batch // bb, padded // (chunks * chunk)),
        in_specs=[tok, tok, tok, tok, tok,
                  _full((RET_HEADS, chunk, chunk)), _full((RET_HEADS, chunk, RET_HEAD_DIM)),
                  _full((1, RET_WIDTH)), state],
        out_specs=[tok, state],
        out_shape=(jax.ShapeDtypeStruct((batch, padded, RET_WIDTH), BF16),
                   jax.ShapeDtypeStruct(s0.shape, F32)),
        compiler_params=_params("parallel", "arbitrary"),
        name="retention",
    )(q, k, kz, v, g, mask, xi, gn_gain, s0)
    return out[:, :length], s_new


def _mixer_out(x_ref, ssm_ref, ret_ref, wout_ref, rows=slice(None)):
    return (x_ref[rows, :] + _dot(ssm_ref[rows, :], wout_ref[:SSM_WIDTH, :])
            + _dot(ret_ref[rows, :], wout_ref[SSM_WIDTH:, :]))


def _query(x1, g_ref, wq_ref):
    return (_dot(_rmsnorm(x1, g_ref[...]).astype(BF16), wq_ref[...]) * MEM_SCALE).astype(BF16)


def _outq_kernel(x_ref, ssm_ref, ret_ref, wout_ref, g_ref, wq_ref, x1_ref, q_ref):
    x1 = _mixer_out(x_ref, ssm_ref, ret_ref, wout_ref)
    x1_ref[...] = x1
    q_ref[...] = _query(x1, g_ref, wq_ref)


def _out_and_query(x, ssm, ret, w_out, g_xattn, w_mq):
    t = x.shape[0]
    tm = min(t, 512)
    row = lambda w: pl.BlockSpec((tm, w), lambda i: (i, 0))
    return pl.pallas_call(
        _outq_kernel,
        grid=(t // tm,),
        in_specs=[row(D_MODEL), row(SSM_WIDTH), row(RET_WIDTH), _full((D_MODEL, D_MODEL)),
                  _full((1, D_MODEL)), _full((D_MODEL, D_MODEL))],
        out_specs=[row(D_MODEL), row(D_MODEL)],
        out_shape=(jax.ShapeDtypeStruct((t, D_MODEL), F32), jax.ShapeDtypeStruct((t, D_MODEL), BF16)),
        compiler_params=_params("parallel"),
        name="out_proj_query",
    )(x, ssm, ret, w_out, g_xattn, w_mq)


def _cache_row_order(x):
    halves = MEM_HEAD_DIM // LANES
    tiles = [x[:, (h * halves + d) * LANES:(h * halves + d + 1) * LANES]
             for d in range(halves) for h in range(MEM_HEADS)]
    return jnp.swapaxes(jnp.stack(tiles, axis=0), 0, 1).reshape(x.shape[0] * CACHE_SUB, LANES)


def _memkv_kernel(m_ref, g_ref, wk_ref, wv_ref, k_ref, v_ref, kb_ref, vb_ref):
    m = _rmsnorm(m_ref[0], g_ref[...]).astype(BF16)
    k = _dot(m, wk_ref[...])
    v = _dot(m, wv_ref[...])
    k_ref[0] = _cache_row_order(k)
    v_ref[0] = _cache_row_order(v)
    kb_ref[0] = k.astype(BF16)
    vb_ref[0] = v.astype(BF16)


def _memory_kv(mem, g_mem, w_mk, w_mv):
    batch = mem.shape[0]
    tok = pl.BlockSpec((1, MEM_LEN, D_MODEL), lambda i: (i, 0, 0))
    cache = pl.BlockSpec((1, CACHE_ROWS, LANES), lambda i: (i, 0, 0))
    f = jax.ShapeDtypeStruct((batch, CACHE_ROWS, LANES), F32)
    h = jax.ShapeDtypeStruct((batch, MEM_LEN, D_MODEL), BF16)
    return pl.pallas_call(
        _memkv_kernel,
        grid=(batch,),
        in_specs=[tok, _full((1, D_MODEL)), _full((D_MODEL, D_MODEL)), _full((D_MODEL, D_MODEL))],
        out_specs=[cache, cache, tok, tok],
        out_shape=(f, f, h, h),
        compiler_params=_params("parallel"),
        name="memory_kv",
    )(mem, g_mem, w_mk, w_mv)


ATTN_ROWS = 512


def _outattn_kernel(x_ref, ssm_ref, ret_ref, wout_ref, g_ref, wq_ref, k_ref, v_ref, x1_ref, o_ref):
    cols = [slice(h * MEM_HEAD_DIM, (h + 1) * MEM_HEAD_DIM) for h in range(MEM_HEADS)]
    groups = [slice(r, r + ATTN_ROWS) for r in range(0, x_ref.shape[0], ATTN_ROWS)]

    def scores_of(rows):
        x1 = _mixer_out(x_ref, ssm_ref, ret_ref, wout_ref, rows)
        x1_ref[rows, :] = x1
        q = _query(x1, g_ref, wq_ref)
        return [_dot_nt(q[:, c], k_ref[0, :, c]) for c in cols]

    def attend(rows, scores):
        outs = []
        for s, c in zip(scores, cols):
            e = jnp.exp(s - jnp.max(s, axis=-1, keepdims=True))
            p = (e / jnp.sum(e, axis=-1, keepdims=True)).astype(BF16)
            outs.append(_dot(p, v_ref[0, :, c]).astype(BF16))
        o_ref[rows, :] = jnp.concatenate(outs, axis=1)

    pending = None
    for rows in groups:
        scores = scores_of(rows)
        if pending is not None:
            attend(*pending)
        pending = (rows, scores)
    attend(*pending)


def _out_and_attention(x, ssm, ret, w_out, g_xattn, w_mq, mk, mv, *, length):
    t = x.shape[0]
    tm = min(length, 1024)
    per_seq = length // tm
    row = lambda w: pl.BlockSpec((tm, w), lambda i: (i, 0))
    mem = pl.BlockSpec((1, MEM_LEN, D_MODEL), lambda i: (i // per_seq, 0, 0))
    return pl.pallas_call(
        _outattn_kernel,
        grid=(t // tm,),
        in_specs=[row(D_MODEL), row(SSM_WIDTH), row(RET_WIDTH), _full((D_MODEL, D_MODEL)),
                  _full((1, D_MODEL)), _full((D_MODEL, D_MODEL)), mem, mem],
        out_specs=[row(D_MODEL), row(D_MODEL)],
        out_shape=(jax.ShapeDtypeStruct((t, D_MODEL), F32), jax.ShapeDtypeStruct((t, D_MODEL), BF16)),
        compiler_params=_params("parallel"),
        name="out_proj_attention",
    )(x, ssm, ret, w_out, g_xattn, w_mq, mk, mv)


def _cached_attention(q_ref, k_ref, v_ref, o_ref, *, bb, length, between=None):
    hit = (lax.broadcasted_iota(jnp.int32, (CACHE_SUB, CACHE_ROWS), 0)
           == lax.broadcasted_iota(jnp.int32, (CACHE_SUB, CACHE_ROWS), 1) % CACHE_SUB)
    first_half = lax.broadcasted_iota(jnp.int32, (length, LANES), 1) % CACHE_SUB < MEM_HEADS
    tiles = CACHE_ROWS // LANES

    def class_reduce(x, op):
        shift = CACHE_SUB
        while shift < LANES:
            x = op(x, pltpu.roll(x, shift, axis=1))
            shift *= 2
        return x

    def softmax_rows(r):
        z = jnp.concatenate(
            [jnp.sum(jnp.where(hit, r[CACHE_SUB * t:CACHE_SUB * (t + 1)], 0.0), axis=0, keepdims=True)
             for t in range(length)], axis=0)
        parts = []
        for i in range(tiles):
            zi = z[:, LANES * i:LANES * (i + 1)]
            parts.append(zi + pltpu.roll(zi, LANES - MEM_HEADS, axis=1))
        mx = parts[0]
        for pi in parts[1:]:
            mx = jnp.maximum(mx, pi)
        mx = class_reduce(mx, jnp.maximum)
        es = [jnp.exp(pi - mx) for pi in parts]
        tot = es[0]
        for ei in es[1:]:
            tot = tot + ei
        tot = class_reduce(tot, jnp.add)
        ps = []
        for ei in es:
            pi = ei / tot
            ps.append(jnp.where(first_half, pi, pltpu.roll(pi, MEM_HEADS, axis=1)))
        p = jnp.concatenate(ps, axis=1)
        return jnp.concatenate(
            [jnp.where(hit, jnp.broadcast_to(p[t:t + 1], (CACHE_SUB, CACHE_ROWS)), 0.0)
             for t in range(length)], axis=0).astype(BF16)

    scores = [_dot_nt(q_ref[b], k_ref[b].astype(BF16)) for b in range(bb)]
    if between is not None:
        between()
    probs = [softmax_rows(r) for r in scores]
    for b in range(bb):
        o_ref[b] = _dot(probs[b], v_ref[b].astype(BF16)).astype(BF16)


def _to_cache_rows(a, lead):
    halves = MEM_HEAD_DIM // LANES
    batch = a.size // (lead * D_MODEL)
    a = a.reshape(batch, lead, MEM_HEADS, halves, LANES).transpose(0, 1, 3, 2, 4)
    return a.reshape(batch, lead * CACHE_SUB, LANES)


def _from_cache_rows(a, lead):
    halves = MEM_HEAD_DIM // LANES
    batch = a.shape[0]
    a = a.reshape(batch, lead, halves, MEM_HEADS, LANES).transpose(0, 1, 3, 2, 4)
    return a.reshape(batch, lead, D_MODEL)


MLP_CHUNK = 1024


def _post_kernel(x1_ref, o_ref, wo_ref, gm_ref, wup_ref, wdn_ref, gf_ref, *rest, side_bb, side_len):
    if side_bb:
        qs_ref, ck_ref, cv_ref, y_ref, os_ref = rest
    else:
        (y_ref,) = rest
    x2 = x1_ref[...] + _dot(o_ref[...], wo_ref[...])
    result = []

    def mlp():
        h = _rmsnorm(x2, gm_ref[...]).astype(BF16)
        chunks = [slice(c, c + MLP_CHUNK) for c in range(0, D_FF, MLP_CHUNK)]
        acc = x2
        up = _dot(h, wup_ref[:, chunks[0]])
        for c, cols in enumerate(chunks):
            nxt = _dot(h, wup_ref[:, chunks[c + 1]]) if c + 1 < len(chunks) else None
            a = jnp.maximum(up, 0.0)
            acc = acc + _dot((a * a).astype(BF16), wdn_ref[cols, :])
            up = nxt
        result.append(acc)

    if side_bb:
        _cached_attention(qs_ref, ck_ref, cv_ref, os_ref, bb=side_bb, length=side_len, between=mlp)
    else:
        mlp()
    y_ref[...] = _rmsnorm(result[0], gf_ref[...])


def _attn_out_mlp(x1, o, w_mo, g_mlp, w_up, w_down, g_final, side=None):
    t = x1.shape[0]
    tm = min(t, 512)
    steps = t // tm
    row = pl.BlockSpec((tm, D_MODEL), lambda i: (i, 0))
    in_specs = [row, row, _full((D_MODEL, D_MODEL)), _full((1, D_MODEL)),
                _full((D_MODEL, D_FF)), _full((D_FF, D_MODEL)), _full((1, D_MODEL))]
    out_specs = [row]
    out_shape = [jax.ShapeDtypeStruct((t, D_MODEL), F32)]
    args = [x1, o, w_mo, g_mlp, w_up, w_down, g_final]
    side_bb = side_len = 0
    if side is not None:
        q_rows = side[0]
        side_bb, rem = divmod(q_rows.shape[0], steps)
        if rem or not side_bb:
            raise ValueError("side attention sequences must spread evenly over the grid steps")
        side_len = q_rows.shape[1] // CACHE_SUB
        blk = lambda r: pl.BlockSpec((side_bb, r, LANES), lambda i: (i, 0, 0))
        in_specs += [blk(q_rows.shape[1]), blk(CACHE_ROWS), blk(CACHE_ROWS)]
        out_specs.append(blk(q_rows.shape[1]))
        out_shape.append(jax.ShapeDtypeStruct(q_rows.shape, BF16))
        args += list(side)
    outs = pl.pallas_call(
        functools.partial(_post_kernel, side_bb=side_bb, side_len=side_len),
        grid=(steps,),
        in_specs=in_specs,
        out_specs=out_specs,
        out_shape=out_shape,
        compiler_params=_params("parallel"),
        name="attn_out_mlp",
    )(*args)
    return outs if side is not None else outs[0]


def _mixers(proj, batch, length, s5_re, s5_im, ret_s, p):
    tokens = batch * length
    u, q, k, kz, v, g = proj
    ssm, s5_re_new, s5_im_new = _s5_mixer(
        u.reshape(batch, length, SSM_WIDTH), s5_re.reshape(batch, SSM_LANES), s5_im.reshape(batch, SSM_LANES),
        p["s5_maps"], p["d_skip"], p["w_glu"], batch=batch, length=length)
    tok3 = lambda a: a.reshape(batch, length, RET_WIDTH)
    ret, ret_new = _retention(tok3(q), tok3(k), tok3(kz), tok3(v), tok3(g), ret_s, p["ret_gn"])
    states = (s5_re_new.reshape(1, batch, SSM_GROUPS, SSM_STATE),
              s5_im_new.reshape(1, batch, SSM_GROUPS, SSM_STATE), ret_new[None])
    return ssm.reshape(tokens, SSM_WIDTH), ret.reshape(tokens, RET_WIDTH), states


def kernel(x_prompt, x_sample, mem_prompt, state_s5_re, state_s5_im, state_ret, cache_mem_k, cache_mem_v, g_mix, w_in, lam_re, lam_im, log_dt, b_re, b_im, c_re, c_im, d_skip, w_glu, ret_gn, w_out, g_xattn, g_mem, w_mq, w_mk, w_mv, w_mo, g_mlp, w_up, w_down, g_final):
    if g_mix.shape[0] != 1:
        raise ValueError("this kernel implements the single-layer configuration")
    bp, lp, _ = x_prompt.shape
    bs, ls, _ = x_sample.shape
    vec = lambda a: a.reshape(1, -1).astype(F32)
    xp = x_prompt.reshape(bp * lp, D_MODEL)
    xs = x_sample.reshape(bs * ls, D_MODEL)
    proj_p, (w_in_b, w_up_b, w_down_b) = _project(
        xp, vec(g_mix[0]), w_in[0], length=lp, pos0=0.0, casts=(w_up[0], w_down[0]))
    w_out_b, w_mq_b, w_mo_b, w_mk_b, w_mv_b, w_glu_b = (
        w[0].astype(BF16) for w in (w_out, w_mq, w_mo, w_mk, w_mv, w_glu))
    proj_s, _ = _project(xs, vec(g_mix[0]), w_in_b, length=ls, pos0=float(PAST_LEN))
    p = dict(
        s5_maps=_s5_block_maps(lam_re[0], lam_im[0], log_dt[0], b_re[0], b_im[0], c_re[0], c_im[0]),
        d_skip=vec(d_skip[0]), w_glu=w_glu_b, ret_gn=vec(ret_gn[0]))
    g_x = vec(g_xattn[0])
    mlp = (w_mo_b, vec(g_mlp[0]), w_up_b, w_down_b, vec(g_final))

    ssm_s, ret_s, states_s = _mixers(proj_s, bs, ls, state_s5_re[0], state_s5_im[0], state_ret[0], p)
    x1_s, q_s = _out_and_query(xs, ssm_s, ret_s, w_out_b, g_x, w_mq_b)
    if (ls * CACHE_SUB) % BF16_ROWS:
        raise ValueError("cached attention needs whole bf16 tiles of query rows")
    side = (_to_cache_rows(q_s, ls), _to_cache_rows(cache_mem_k, MEM_LEN), _to_cache_rows(cache_mem_v, MEM_LEN))

    mk_rows, mv_rows, mk_p, mv_p = _memory_kv(mem_prompt, vec(g_mem[0]), w_mk_b, w_mv_b)
    zs = jnp.zeros((bp, SSM_GROUPS, SSM_STATE), F32)
    zr = jnp.zeros((bp, RET_HEADS, RET_HEAD_DIM, RET_HEAD_DIM), F32)
    ssm_p, ret_p, states_p = _mixers(proj_p, bp, lp, zs, zs, zr, p)
    x1_p, o_p = _out_and_attention(xp, ssm_p, ret_p, w_out_b, g_x, w_mq_b, mk_p, mv_p, length=lp)
    y_p, o_s = _attn_out_mlp(x1_p, o_p, *mlp, side=side)
    y_s = _attn_out_mlp(x1_s, _from_cache_rows(o_s, ls).reshape(bs * ls, D_MODEL), *mlp)

    kv5 = lambda a: _from_cache_rows(a, MEM_LEN).reshape(1, bp, MEM_LEN, MEM_HEADS, MEM_HEAD_DIM)
    return (y_p.reshape(bp, lp, D_MODEL), y_s.reshape(bs, ls, D_MODEL), *states_p, kv5(mk_rows), kv5(mv_rows),
            *states_s)
```

```python
import functools
import math

import numpy as np
import jax
import jax.numpy as jnp
from jax import lax
from jax.experimental import pallas as pl
from jax.experimental.pallas import tpu as pltpu

F32 = jnp.float32
BF16 = jnp.bfloat16

D_MODEL = 1024
SSM_WIDTH = 512
SSM_GROUP = 16
SSM_GROUPS = 32
SSM_STATE = 64
SSM_LANES = SSM_GROUPS * SSM_STATE
RET_WIDTH = 512
RET_HEADS = 4
RET_HEAD_DIM = 128
RET_CHUNK = 128
ROPE_BASE = 10000.0
MEM_LEN = 256
MEM_HEADS = 4
MEM_HEAD_DIM = 256
MEM_SCALE = MEM_HEAD_DIM ** -0.5
D_FF = 4096
PROJ_WIDTH = SSM_WIDTH + 4 * RET_WIDTH
EPS = 1e-6
PAST_LEN = 16384

LANES = 128
MXU_TILE = 256
BF16_ROWS = 16
VMEM_LIMIT = 56 * 1024 * 1024

CACHE_ROWS = MEM_LEN * MEM_HEADS * MEM_HEAD_DIM // LANES
CACHE_SUB = MEM_HEADS * MEM_HEAD_DIM // LANES

assert math.frexp(MEM_SCALE)[0] == 0.5


def _dot(a, b):
    return jnp.dot(a, b, preferred_element_type=F32)


def _dot_nt(a, b):
    return lax.dot_general(a, b, (((1,), (1,)), ((), ())), preferred_element_type=F32)


def _dot_tn(a, b):
    return lax.dot_general(a, b, (((0,), (0,)), ((), ())), preferred_element_type=F32)


def _rmsnorm(x, g):
    return x * lax.rsqrt(jnp.mean(x * x, axis=-1, keepdims=True) + EPS) * g


def _sigmoid(x):
    return 1.0 / (1.0 + jnp.exp(-x))


def _gelu_tanh(x):
    c = math.sqrt(2.0 / math.pi)
    return x * (0.5 * jnp.tanh(x * (c + (c * 0.044715) * (x * x))) + 0.5)


def _params(*sem):
    return pltpu.CompilerParams(dimension_semantics=sem, vmem_limit_bytes=VMEM_LIMIT)


def _full(shape):
    return pl.BlockSpec(shape, lambda *_: (0,) * len(shape))


SSM_BLOCK = 4
SSM_TILE_GROUPS = MXU_TILE // SSM_STATE
SSM_TILES = SSM_GROUPS // SSM_TILE_GROUPS
SSM_TILE_CHANNELS = SSM_TILE_GROUPS * SSM_GROUP
GROUP_SHIFT = SSM_GROUP.bit_length() - 1
STATE_SHIFT = SSM_STATE.bit_length() - 1

assert SSM_BLOCK * SSM_TILE_CHANNELS == MXU_TILE and SSM_TILE_CHANNELS == SSM_STATE


def _s5_prep_kernel(lr_ref, li_ref, ldt_ref, br_ref, bi_ref, cr_ref, ci_ref,
                    ar_ref, ai_ref, winr_ref, wini_ref, woutr_ref, wouti_ref, kdir_ref):
    lr = lr_ref[...]
    li = li_ref[...]
    dt = jnp.exp(ldt_ref[...])
    mag = jnp.exp(lr * dt)
    ab_re = mag * jnp.cos(li * dt)
    ab_im = mag * jnp.sin(li * dt)
    den = lr * lr + li * li
    f_re = ((ab_re - 1.0) * lr + ab_im * li) / den
    f_im = (ab_im * lr - (ab_re - 1.0) * li) / den
    bb_re = f_re * br_ref[...] - f_im * bi_ref[...]
    bb_im = f_re * bi_ref[...] + f_im * br_ref[...]
    cr = cr_ref[...]
    ci = ci_ref[...]
    pows = [(jnp.ones_like(ab_re), jnp.zeros_like(ab_re))]
    for _ in range(SSM_BLOCK):
        pr, pi = pows[-1]
        pows.append((pr * ab_re - pi * ab_im, pr * ab_im + pi * ab_re))
    ar_ref[...], ai_ref[...] = pows[SSM_BLOCK]

    R, T, TG = SSM_BLOCK, SSM_TILES, SSM_TILE_GROUPS
    rows = SSM_GROUPS * SSM_GROUP
    side = SSM_TILE_CHANNELS
    exact = dict(precision=lax.Precision.HIGHEST, preferred_element_type=F32)
    iota = lambda shape, d: lax.broadcasted_iota(jnp.int32, shape, d)
    rep4 = lambda a, axis: jnp.concatenate([a] * TG, axis=axis)
    tiles = lambda a: a.reshape(T, side, a.shape[-1])
    group_of = lambda i: (i & (side - 1)) >> GROUP_SHIFT

    own_in = group_of(iota((rows, MXU_TILE), 0)) == iota((rows, MXU_TILE), 1) >> STATE_SHIFT
    for s in range(R):
        pr, pi = pows[R - 1 - s]
        for ref, w in ((winr_ref, pr * bb_re - pi * bb_im), (wini_ref, pr * bb_im + pi * bb_re)):
            ref[:, side * s:side * (s + 1), :] = tiles(jnp.where(own_in, rep4(w, 1), 0.0)).astype(BF16)

    eye = (iota((rows, rows), 0) == iota((rows, rows), 1)).astype(F32)
    own_out = iota((MXU_TILE, rows), 0) >> STATE_SHIFT == group_of(iota((MXU_TILE, rows), 1))
    for ref, real in ((woutr_ref, True), (wouti_ref, False)):
        per_lag = []
        for j in range(R):
            pr, pi = pows[j + 1]
            w = cr * pr - ci * pi if real else -(cr * pi + ci * pr)
            wt = lax.dot_general(w, eye, (((0,), (0,)), ((), ())), **exact)
            per_lag.append(jnp.where(own_out, rep4(wt, 0), 0.0))
        for n in range(T):
            ref[n] = jnp.concatenate([m[:, side * n:side * (n + 1)] for m in per_lag], axis=1).astype(BF16)

    grouped = lambda a: a.reshape(SSM_GROUPS, SSM_GROUP, SSM_STATE)
    contract = lambda a, b: lax.dot_general(a, b, (((2,), (2,)), ((0,), (0,))), **exact)
    own_dir = group_of(iota((rows, side), 0)) == iota((rows, side), 1) >> GROUP_SHIFT
    by_lag = []
    for d in range(R):
        pr, pi = pows[d]
        er = grouped(pr * bb_re - pi * bb_im)
        ei = grouped(pr * bb_im + pi * bb_re)
        kt = (contract(er, grouped(cr)) - contract(ei, grouped(ci))).reshape(rows, SSM_GROUP)
        by_lag.append(tiles(jnp.where(own_dir, rep4(kt, 1), 0.0)))
    nothing = jnp.zeros_like(by_lag[0])
    for s in range(R):
        kdir_ref[:, side * s:side * (s + 1), :] = jnp.concatenate(
            [by_lag[j - s] if j >= s else nothing for j in range(R)], axis=2).astype(BF16)


def _s5_block_maps(lam_re, lam_im, log_dt, b_re, b_im, c_re, c_im):
    rows = SSM_GROUPS * SSM_GROUP
    rep = lambda a: jnp.repeat(a, SSM_GROUP, axis=0)
    ldt = jnp.broadcast_to(log_dt[:, None], (SSM_GROUPS, SSM_STATE))
    bt = lambda b: b.transpose(0, 2, 1).reshape(rows, SSM_STATE)
    ct = lambda c: c.reshape(rows, SSM_STATE)
    one = jax.ShapeDtypeStruct((rows, SSM_STATE), F32)
    tile = jax.ShapeDtypeStruct((SSM_TILES, MXU_TILE, MXU_TILE), BF16)
    ar, ai, winr, wini, woutr, wouti, direct = pl.pallas_call(
        _s5_prep_kernel,
        out_shape=(one, one, tile, tile, tile, tile, tile),
        compiler_params=pltpu.CompilerParams(vmem_limit_bytes=VMEM_LIMIT),
        name="s5_block_maps",
    )(rep(lam_re), rep(lam_im), rep(ldt), bt(b_re), bt(b_im), ct(c_re), ct(c_im))
    a_re = ar[::SSM_GROUP].reshape(1, SSM_LANES)
    a_im = ai[::SSM_GROUP].reshape(1, SSM_LANES)
    return a_re, a_im, winr, wini, woutr, wouti, direct


def _retention_chunk(length):
    return RET_CHUNK if length % RET_CHUNK == 0 else length


def _log_gamma():
    return np.log(1.0 - 2.0 ** (-5.0 - np.arange(RET_HEADS, dtype=np.float64)))


def _rope_tables(length, pos0):
    half = RET_HEAD_DIM // 2
    inv = ROPE_BASE ** (-np.arange(half, dtype=np.float64) / half)
    ang = (pos0 + np.arange(length, dtype=np.float64))[:, None] * inv[None, :]
    return (np.concatenate([np.cos(ang), np.cos(ang)], axis=1),
            np.concatenate([-np.sin(ang), np.sin(ang)], axis=1))


def _zeta_rows(length):
    chunk = _retention_chunk(length)
    idx = np.arange(chunk, dtype=np.float64)
    zeta = np.exp((chunk - 1.0 - idx)[None, :] * _log_gamma()[:, None])
    return np.concatenate([np.broadcast_to(z[:, None], (chunk, RET_HEAD_DIM)) for z in zeta], axis=1)


def _decay_tables(length, padded):
    chunk = _retention_chunk(length)
    cpad = chunk if padded == length else padded
    lg = _log_gamma()
    idx = np.arange(chunk, dtype=np.float64)
    diff = idx[:, None] - idx[None, :]
    mask = np.where(diff[None] >= 0, np.exp(np.maximum(diff, 0.0)[None] * lg[:, None, None]), 0.0)
    xi = np.exp((idx + 1.0)[None, :] * lg[:, None])
    gamma_c = tuple(float(v) for v in np.exp(chunk * lg))
    mask = np.pad(mask, ((0, 0), (0, cpad - chunk), (0, cpad - chunk)))
    xi = np.broadcast_to(np.pad(xi, ((0, 0), (0, cpad - chunk)))[:, :, None], (RET_HEADS, cpad, RET_HEAD_DIM))
    f = lambda a: jnp.asarray(np.ascontiguousarray(a), dtype=F32)
    return cpad, gamma_c, f(mask), f(xi)


def _ret_lanes(part, head):
    lo = part * RET_WIDTH + head * RET_HEAD_DIM
    return slice(lo, lo + RET_HEAD_DIM)


def _proj_kernel(x_ref, g_ref, w_ref, cc_ref, ss_ref, zeta_ref, *rest, n_cast):
    casts_in, rest = rest[:n_cast], rest[n_cast:]
    u_ref, ret_ref, gate_ref = rest[:3]
    w = w_ref[...]
    if n_cast:
        w = w.astype(BF16)
        for src, dst in zip(casts_in, rest[3:]):
            dst[...] = src[...].astype(BF16)
    h = _rmsnorm(x_ref[...], g_ref[...]).astype(BF16)
    proj = _dot(h, w)
    u_ref[...] = proj[:, :SSM_WIDTH]
    ret_ref[:, 3 * RET_WIDTH:] = proj[:, SSM_WIDTH + 2 * RET_WIDTH:SSM_WIDTH + 3 * RET_WIDTH].astype(BF16)
    gate_ref[...] = proj[:, SSM_WIDTH + 3 * RET_WIDTH:]
    cc = cc_ref[...]
    ss = ss_ref[...]
    rope = lambda a: a * cc + pltpu.roll(a, RET_HEAD_DIM // 2, axis=1) * ss
    for hd in range(RET_HEADS):
        cols = slice(hd * RET_HEAD_DIM, (hd + 1) * RET_HEAD_DIM)
        q = rope(proj[:, SSM_WIDTH + hd * RET_HEAD_DIM:SSM_WIDTH + (hd + 1) * RET_HEAD_DIM])
        k = rope(proj[:, SSM_WIDTH + RET_WIDTH + hd * RET_HEAD_DIM:SSM_WIDTH + RET_WIDTH + (hd + 1) * RET_HEAD_DIM])
        ret_ref[:, _ret_lanes(0, hd)] = (q * (RET_HEAD_DIM ** -0.5)).astype(BF16)
        ret_ref[:, _ret_lanes(1, hd)] = k.astype(BF16)
        ret_ref[:, _ret_lanes(2, hd)] = (k * zeta_ref[:, cols]).astype(BF16)


def _project(x, g_mix, w_in, *, length, pos0, casts=()):
    t = x.shape[0]
    tm = min(t, 1024)
    steps = t // tm
    chunk = _retention_chunk(length)
    casts = (w_in, *casts) if casts else ()
    if tm % chunk or (length % tm and tm % length):
        raise ValueError("token tiles must hold whole retention chunks of whole or repeated sequences")
    cc, ss = _rope_tables(length, pos0)
    if length < tm:
        cc, ss = np.tile(cc, (tm // length, 1)), np.tile(ss, (tm // length, 1))
    tab_blocks = cc.shape[0] // tm
    zeta = np.tile(_zeta_rows(length), (tm // chunk, 1))
    const = lambda a: jnp.asarray(np.ascontiguousarray(a), dtype=F32)
    row = lambda w: pl.BlockSpec((tm, w), lambda i: (i, 0))
    tab = pl.BlockSpec((tm, RET_HEAD_DIM), lambda i: (i % tab_blocks, 0))
    f = jax.ShapeDtypeStruct((t, 512), F32)
    h = jax.ShapeDtypeStruct((t, 4 * RET_WIDTH), BF16)
    in_specs = [row(D_MODEL), _full((1, D_MODEL)), _full((D_MODEL, PROJ_WIDTH)), tab, tab, _full((tm, RET_WIDTH))]
    out_specs = [row(512), row(4 * RET_WIDTH), row(512)]
    out_shape = [f, h, f]
    if casts:
        for c in casts:
            rows, rem = divmod(c.shape[0], steps)
            if rem or rows % BF16_ROWS:
                raise ValueError("side-cast weights must split into whole bf16 tiles per grid step")
            blk = pl.BlockSpec((rows, c.shape[1]), lambda i: (i, 0))
            in_specs.append(blk)
            out_specs.append(blk)
            out_shape.append(jax.ShapeDtypeStruct(c.shape, BF16))
    outs = pl.pallas_call(
        functools.partial(_proj_kernel, n_cast=len(casts)),
        grid=(steps,),
        in_specs=in_specs,
        out_specs=out_specs,
        out_shape=out_shape,
        compiler_params=_params("arbitrary" if casts else "parallel"),
        name="in_proj",
    )(x, g_mix, w_in, const(cc), const(ss), const(zeta), *casts)
    return outs[:3], outs[3:]


def _s5_block_kernel(u_ref, h0r_ref, h0i_ref, ar_ref, ai_ref, winr_ref, wini_ref, woutr_ref, wouti_ref,
                     kdir_ref, d_ref, wglu_ref, out_ref, hr_ref, hi_ref, vr_scr, vi_scr, xr_scr, xi_scr,
                     *, batch, steps, pair, width):
    R, T = SSM_BLOCK, SSM_TILES
    chan = SSM_TILE_CHANNELS
    blocks = steps // R
    rows, brow = steps * batch, blocks * batch

    @pl.when(pl.program_id(0) == 0)
    def _():
        hr_ref[...] = h0r_ref[...]
        hi_ref[...] = h0i_ref[...]

    u = jnp.swapaxes(u_ref[...], 0, 1).reshape(rows, SSM_WIDTH)
    u4 = u.reshape(blocks, R, batch, SSM_WIDTH)
    lag = [u4[:, s].reshape(brow, SSM_WIDTH) for s in range(R)]
    direct = []
    for n in range(T):
        un = jnp.concatenate([lag[s][:, chan * n:chan * (n + 1)] for s in range(R)], axis=1).astype(BF16)
        lanes = slice(MXU_TILE * n, MXU_TILE * (n + 1))
        vr_scr[:, lanes] = _dot(un, winr_ref[n])
        vi_scr[:, lanes] = _dot(un, wini_ref[n])
        direct.append(_dot(un, kdir_ref[n]))

    for j in range(SSM_LANES // width):
        glob = slice(j * width, (j + 1) * width)
        ar = jnp.broadcast_to(ar_ref[:, glob], (batch, width))
        ai = jnp.broadcast_to(ai_ref[:, glob], (batch, width))
        xr = hr_ref[:, glob]
        xi = hi_ref[:, glob]
        for i in range(blocks // pair):
            prev_r, prev_i = [], []
            for s in range(pair):
                at = slice((i * pair + s) * batch, (i * pair + s + 1) * batch)
                prev_r.append(xr)
                prev_i.append(xi)
                xr, xi = ar * xr - ai * xi + vr_scr[at, glob], ar * xi + ai * xr + vi_scr[at, glob]
            blk = slice(i * pair * batch, (i + 1) * pair * batch)
            xr_scr[blk, glob] = jnp.concatenate(prev_r, axis=0).astype(BF16)
            xi_scr[blk, glob] = jnp.concatenate(prev_i, axis=0).astype(BF16)
        hr_ref[:, glob] = xr
        hi_ref[:, glob] = xi

    y4 = []
    for n in range(T):
        lanes = slice(MXU_TILE * n, MXU_TILE * (n + 1))
        y4.append(_dot(xr_scr[:, lanes], woutr_ref[n]) + _dot(xi_scr[:, lanes], wouti_ref[n]) + direct[n])
    per_lag = [jnp.concatenate([y4[n][:, chan * j:chan * (j + 1)] for n in range(T)], axis=1)
               .reshape(blocks, batch, SSM_WIDTH) for j in range(R)]
    y = jnp.stack(per_lag, axis=1).reshape(rows, SSM_WIDTH) + d_ref[...] * u
    z = _gelu_tanh(y)
    out = z * _sigmoid(_dot(z.astype(BF16), wglu_ref[...]))
    out_ref[...] = jnp.swapaxes(out.reshape(steps, batch, SSM_WIDTH), 0, 1).astype(BF16)


def _s5_mixer(u, h0_re, h0_im, maps, d_skip, w_glu, *, batch, length):
    a_re, a_im, winr, wini, woutr, wouti, direct = maps
    steps = min(length, 256)
    if steps % SSM_BLOCK or length % steps:
        raise ValueError("sequence length must be a multiple of the S5 block")
    blocks = steps // SSM_BLOCK
    pair = max(1, min(blocks, BF16_ROWS // batch))
    if (pair * batch) % BF16_ROWS or blocks % pair:
        raise ValueError("S5 state stores need whole bf16 tiles")
    width = SSM_LANES // 2 if batch <= 8 else LANES
    brow = blocks * batch
    kern = functools.partial(_s5_block_kernel, batch=batch, steps=steps, pair=pair, width=width)
    st = jax.ShapeDtypeStruct((batch, SSM_LANES), F32)
    tile = _full((SSM_TILES, MXU_TILE, MXU_TILE))
    return pl.pallas_call(
        kern,
        grid=(length // steps,),
        in_specs=[pl.BlockSpec((batch, steps, SSM_WIDTH), lambda i: (0, i, 0)),
                  _full((batch, SSM_LANES)), _full((batch, SSM_LANES)),
                  _full((1, SSM_LANES)), _full((1, SSM_LANES)),
                  tile, tile, tile, tile, tile,
                  _full((1, SSM_WIDTH)), _full((SSM_WIDTH, SSM_WIDTH))],
        out_specs=[pl.BlockSpec((batch, steps, SSM_WIDTH), lambda i: (0, i, 0)),
                   _full((batch, SSM_LANES)), _full((batch, SSM_LANES))],
        out_shape=(jax.ShapeDtypeStruct((batch, length, SSM_WIDTH), BF16), st, st),
        scratch_shapes=[pltpu.VMEM((brow, SSM_LANES), F32), pltpu.VMEM((brow, SSM_LANES), F32),
                        pltpu.VMEM((brow, SSM_LANES), BF16), pltpu.VMEM((brow, SSM_LANES), BF16)],
        compiler_params=_params("arbitrary"),
        name="s5_mixer",
    )(u, h0_re, h0_im, a_re, a_im, winr, wini, woutr, wouti, direct, d_skip, w_glu)


def _ret_kernel(p_ref, g_ref, mask_ref, xi_ref, gain_ref, s0_ref, o_ref, s_ref,
                *, bb, gamma_c, unroll, chunk, chunks):
    @pl.when(pl.program_id(1) == 0)
    def _():
        s_ref[...] = s0_ref[...]

    heads = range(RET_HEADS)
    cols = [slice(h * RET_HEAD_DIM, (h + 1) * RET_HEAD_DIM) for h in heads]

    def one_chunk(b, rows):
        qb = [p_ref[b, rows, _ret_lanes(0, h)] for h in heads]
        vb = [p_ref[b, rows, _ret_lanes(3, h)] for h in heads]
        s_prev = [s_ref[b, h] for h in heads]
        scores = [(_dot_nt(qb[h], p_ref[b, rows, _ret_lanes(1, h)]) * mask_ref[h]).astype(BF16) for h in heads]
        cross = [_dot(qb[h], s_prev[h].astype(BF16)) * xi_ref[h] for h in heads]
        kv = [_dot_tn(p_ref[b, rows, _ret_lanes(2, h)], vb[h]) for h in heads]
        inner = [_dot(scores[h], vb[h]) for h in heads]
        outs = []
        for h in heads:
            s_ref[b, h] = s_prev[h] * gamma_c[h] + kv[h]
            o = inner[h] + cross[h]
            mu = jnp.mean(o, axis=-1, keepdims=True)
            oc = o - mu
            var = jnp.mean(oc * oc, axis=-1, keepdims=True)
            on = oc * lax.rsqrt(var + EPS) * gain_ref[:, cols[h]]
            gate = g_ref[b, rows, cols[h]]
            outs.append((gate * _sigmoid(gate) * on).astype(BF16))
        o_ref[b, rows, :] = jnp.concatenate(outs, axis=1)

    def per_batch(b, carry):
        for c in range(chunks):
            one_chunk(b, slice(c * chunk, (c + 1) * chunk))
        return carry

    if unroll == bb:
        for b in range(bb):
            per_batch(b, 0)
    else:
        lax.fori_loop(0, bb, per_batch, 0, unroll=unroll)


def _retention(packed, g, s0, gn_gain):
    batch, length, _ = g.shape
    padded = pl.cdiv(length, BF16_ROWS) * BF16_ROWS
    chunk, gamma_c, mask, xi = _decay_tables(length, padded)
    if padded != length:
        pad = lambda a: jnp.pad(a, ((0, 0), (0, padded - length), (0, 0)))
        packed, g = pad(packed), pad(g)
    bb, unroll = (8, 2) if chunk > BF16_ROWS else (16, 16)
    chunks = 2 if (padded // chunk) % 2 == 0 else 1
    tok = pl.BlockSpec((bb, chunks * chunk, RET_WIDTH), lambda i, c: (i, c, 0))
    state = pl.BlockSpec((bb, RET_HEADS, RET_HEAD_DIM, RET_HEAD_DIM), lambda i, c: (i, 0, 0, 0))
    out, s_new = pl.pallas_call(
        functools.partial(_ret_kernel, bb=bb, gamma_c=gamma_c, unroll=unroll, chunk=chunk, chunks=chunks),
        grid=(batch // bb, padded // (chunks * chunk)),
        in_specs=[pl.BlockSpec((bb, chunks * chunk, 4 * RET_WIDTH), lambda i, c: (i, c, 0)), tok,
                  _full((RET_HEADS, chunk, chunk)), _full((RET_HEADS, chunk, RET_HEAD_DIM)),
                  _full((1, RET_WIDTH)), state],
        out_specs=[tok, state],
        out_shape=(jax.ShapeDtypeStruct((batch, padded, RET_WIDTH), BF16),
                   jax.ShapeDtypeStruct(s0.shape, F32)),
        compiler_params=_params("parallel", "arbitrary"),
        name="retention",
    )(packed, g, mask, xi, gn_gain, s0)
    return out[:, :length], s_new


def _mixer_out(x_ref, ssm_ref, ret_ref, wout_ref, rows=slice(None)):
    return (x_ref[rows, :] + _dot(ssm_ref[rows, :], wout_ref[:SSM_WIDTH, :])
            + _dot(ret_ref[rows, :], wout_ref[SSM_WIDTH:, :]))


def _query(x1, g_ref, wq_ref):
    return (_dot(_rmsnorm(x1, g_ref[...]).astype(BF16), wq_ref[...]) * MEM_SCALE).astype(BF16)


def _outq_kernel(x_ref, ssm_ref, ret_ref, wout_ref, g_ref, wq_ref, x1_ref, q_ref):
    x1 = _mixer_out(x_ref, ssm_ref, ret_ref, wout_ref)
    x1_ref[...] = x1
    q_ref[...] = _query(x1, g_ref, wq_ref)


def _out_and_query(x, ssm, ret, w_out, g_xattn, w_mq):
    t = x.shape[0]
    tm = min(t, 512)
    row = lambda w: pl.BlockSpec((tm, w), lambda i: (i, 0))
    return pl.pallas_call(
        _outq_kernel,
        grid=(t // tm,),
        in_specs=[row(D_MODEL), row(SSM_WIDTH), row(RET_WIDTH), _full((D_MODEL, D_MODEL)),
                  _full((1, D_MODEL)), _full((D_MODEL, D_MODEL))],
        out_specs=[row(D_MODEL), row(D_MODEL)],
        out_shape=(jax.ShapeDtypeStruct((t, D_MODEL), F32), jax.ShapeDtypeStruct((t, D_MODEL), BF16)),
        compiler_params=_params("parallel"),
        name="out_proj_query",
    )(x, ssm, ret, w_out, g_xattn, w_mq)


def _cache_row_order(x):
    halves = MEM_HEAD_DIM // LANES
    tiles = [x[:, (h * halves + d) * LANES:(h * halves + d + 1) * LANES]
             for d in range(halves) for h in range(MEM_HEADS)]
    return jnp.swapaxes(jnp.stack(tiles, axis=0), 0, 1).reshape(x.shape[0] * CACHE_SUB, LANES)


def _memkv_kernel(m_ref, g_ref, wk_ref, wv_ref, k_ref, v_ref, kb_ref, vb_ref):
    m = _rmsnorm(m_ref[0], g_ref[...]).astype(BF16)
    k = _dot(m, wk_ref[...])
    v = _dot(m, wv_ref[...])
    k_ref[0] = _cache_row_order(k)
    v_ref[0] = _cache_row_order(v)
    kb_ref[0] = k.astype(BF16)
    vb_ref[0] = v.astype(BF16)


def _memory_kv(mem, g_mem, w_mk, w_mv):
    batch = mem.shape[0]
    tok = pl.BlockSpec((1, MEM_LEN, D_MODEL), lambda i: (i, 0, 0))
    cache = pl.BlockSpec((1, CACHE_ROWS, LANES), lambda i: (i, 0, 0))
    f = jax.ShapeDtypeStruct((batch, CACHE_ROWS, LANES), F32)
    h = jax.ShapeDtypeStruct((batch, MEM_LEN, D_MODEL), BF16)
    return pl.pallas_call(
        _memkv_kernel,
        grid=(batch,),
        in_specs=[tok, _full((1, D_MODEL)), _full((D_MODEL, D_MODEL)), _full((D_MODEL, D_MODEL))],
        out_specs=[cache, cache, tok, tok],
        out_shape=(f, f, h, h),
        compiler_params=_params("parallel"),
        name="memory_kv",
    )(mem, g_mem, w_mk, w_mv)


ATTN_ROWS = 512


def _outattn_kernel(x_ref, ssm_ref, ret_ref, wout_ref, g_ref, wq_ref, k_ref, v_ref, x1_ref, o_ref):
    cols = [slice(h * MEM_HEAD_DIM, (h + 1) * MEM_HEAD_DIM) for h in range(MEM_HEADS)]
    groups = [slice(r, r + ATTN_ROWS) for r in range(0, x_ref.shape[0], ATTN_ROWS)]

    def scores_of(rows):
        x1 = _mixer_out(x_ref, ssm_ref, ret_ref, wout_ref, rows)
        x1_ref[rows, :] = x1
        q = _query(x1, g_ref, wq_ref)
        return [_dot_nt(q[:, c], k_ref[0, :, c]) for c in cols]

    def attend(rows, scores):
        outs = []
        for s, c in zip(scores, cols):
            e = jnp.exp(s - jnp.max(s, axis=-1, keepdims=True))
            p = (e / jnp.sum(e, axis=-1, keepdims=True)).astype(BF16)
            outs.append(_dot(p, v_ref[0, :, c]).astype(BF16))
        o_ref[rows, :] = jnp.concatenate(outs, axis=1)

    pending = None
    for rows in groups:
        scores = scores_of(rows)
        if pending is not None:
            attend(*pending)
        pending = (rows, scores)
    attend(*pending)


def _out_and_attention(x, ssm, ret, w_out, g_xattn, w_mq, mk, mv, *, length):
    t = x.shape[0]
    tm = min(length, 1024)
    per_seq = length // tm
    row = lambda w: pl.BlockSpec((tm, w), lambda i: (i, 0))
    mem = pl.BlockSpec((1, MEM_LEN, D_MODEL), lambda i: (i // per_seq, 0, 0))
    return pl.pallas_call(
        _outattn_kernel,
        grid=(t // tm,),
        in_specs=[row(D_MODEL), row(SSM_WIDTH), row(RET_WIDTH), _full((D_MODEL, D_MODEL)),
                  _full((1, D_MODEL)), _full((D_MODEL, D_MODEL)), mem, mem],
        out_specs=[row(D_MODEL), row(D_MODEL)],
        out_shape=(jax.ShapeDtypeStruct((t, D_MODEL), F32), jax.ShapeDtypeStruct((t, D_MODEL), BF16)),
        compiler_params=_params("parallel"),
        name="out_proj_attention",
    )(x, ssm, ret, w_out, g_xattn, w_mq, mk, mv)


def _cached_attention(q_ref, k_ref, v_ref, o_ref, *, bb, length, between=None):
    hit = (lax.broadcasted_iota(jnp.int32, (CACHE_SUB, CACHE_ROWS), 0)
           == lax.broadcasted_iota(jnp.int32, (CACHE_SUB, CACHE_ROWS), 1) % CACHE_SUB)
    first_half = lax.broadcasted_iota(jnp.int32, (length, LANES), 1) % CACHE_SUB < MEM_HEADS
    tiles = CACHE_ROWS // LANES

    def class_reduce(x, op):
        shift = CACHE_SUB
        while shift < LANES:
            x = op(x, pltpu.roll(x, shift, axis=1))
            shift *= 2
        return x

    def softmax_rows(r):
        z = jnp.concatenate(
            [jnp.sum(jnp.where(hit, r[CACHE_SUB * t:CACHE_SUB * (t + 1)], 0.0), axis=0, keepdims=True)
             for t in range(length)], axis=0)
        parts = []
        for i in range(tiles):
            zi = z[:, LANES * i:LANES * (i + 1)]
            parts.append(zi + pltpu.roll(zi, LANES - MEM_HEADS, axis=1))
        mx = parts[0]
        for pi in parts[1:]:
            mx = jnp.maximum(mx, pi)
        mx = class_reduce(mx, jnp.maximum)
        es = [jnp.exp(pi - mx) for pi in parts]
        tot = es[0]
        for ei in es[1:]:
            tot = tot + ei
        tot = class_reduce(tot, jnp.add)
        ps = []
        for ei in es:
            pi = ei / tot
            ps.append(jnp.where(first_half, pi, pltpu.roll(pi, MEM_HEADS, axis=1)))
        p = jnp.concatenate(ps, axis=1)
        return jnp.concatenate(
            [jnp.where(hit, jnp.broadcast_to(p[t:t + 1], (CACHE_SUB, CACHE_ROWS)), 0.0)
             for t in range(length)], axis=0).astype(BF16)

    scores = [_dot_nt(q_ref[b], k_ref[b].astype(BF16)) for b in range(bb)]
    if between is not None:
        between()
    probs = [softmax_rows(r) for r in scores]
    for b in range(bb):
        o_ref[b] = _dot(probs[b], v_ref[b].astype(BF16)).astype(BF16)


def _to_cache_rows(a, lead):
    halves = MEM_HEAD_DIM // LANES
    batch = a.size // (lead * D_MODEL)
    a = a.reshape(batch, lead, MEM_HEADS, halves, LANES).transpose(0, 1, 3, 2, 4)
    return a.reshape(batch, lead * CACHE_SUB, LANES)


def _from_cache_rows(a, lead):
    halves = MEM_HEAD_DIM // LANES
    batch = a.shape[0]
    a = a.reshape(batch, lead, halves, MEM_HEADS, LANES).transpose(0, 1, 3, 2, 4)
    return a.reshape(batch, lead, D_MODEL)


MLP_CHUNK = 1024


def _post_kernel(x1_ref, o_ref, wo_ref, gm_ref, wup_ref, wdn_ref, gf_ref, *rest, side_bb, side_len):
    if side_bb:
        qs_ref, ck_ref, cv_ref, y_ref, os_ref = rest
    else:
        (y_ref,) = rest
    x2 = x1_ref[...] + _dot(o_ref[...], wo_ref[...])
    result = []

    def mlp():
        h = _rmsnorm(x2, gm_ref[...]).astype(BF16)
        chunks = [slice(c, c + MLP_CHUNK) for c in range(0, D_FF, MLP_CHUNK)]
        acc = x2
        up = _dot(h, wup_ref[:, chunks[0]])
        for c, cols in enumerate(chunks):
            nxt = _dot(h, wup_ref[:, chunks[c + 1]]) if c + 1 < len(chunks) else None
            a = jnp.maximum(up, 0.0)
            acc = acc + _dot((a * a).astype(BF16), wdn_ref[cols, :])
            up = nxt
        result.append(acc)

    if side_bb:
        _cached_attention(qs_ref, ck_ref, cv_ref, os_ref, bb=side_bb, length=side_len, between=mlp)
    else:
        mlp()
    y_ref[...] = _rmsnorm(result[0], gf_ref[...])


def _attn_out_mlp(x1, o, w_mo, g_mlp, w_up, w_down, g_final, side=None):
    t = x1.shape[0]
    tm = min(t, 512)
    steps = t // tm
    row = pl.BlockSpec((tm, D_MODEL), lambda i: (i, 0))
    in_specs = [row, row, _full((D_MODEL, D_MODEL)), _full((1, D_MODEL)),
                _full((D_MODEL, D_FF)), _full((D_FF, D_MODEL)), _full((1, D_MODEL))]
    out_specs = [row]
    out_shape = [jax.ShapeDtypeStruct((t, D_MODEL), F32)]
    args = [x1, o, w_mo, g_mlp, w_up, w_down, g_final]
    side_bb = side_len = 0
    if side is not None:
        q_rows = side[0]
        side_bb, rem = divmod(q_rows.shape[0], steps)
        if rem or not side_bb:
            raise ValueError("side attention sequences must spread evenly over the grid steps")
        side_len = q_rows.shape[1] // CACHE_SUB
        blk = lambda r: pl.BlockSpec((side_bb, r, LANES), lambda i: (i, 0, 0))
        in_specs += [blk(q_rows.shape[1]), blk(CACHE_ROWS), blk(CACHE_ROWS)]
        out_specs.append(blk(q_rows.shape[1]))
        out_shape.append(jax.ShapeDtypeStruct(q_rows.shape, BF16))
        args += list(side)
    outs = pl.pallas_call(
        functools.partial(_post_kernel, side_bb=side_bb, side_len=side_len),
        grid=(steps,),
        in_specs=in_specs,
        out_specs=out_specs,
        out_shape=out_shape,
        compiler_params=_params("parallel"),
        name="attn_out_mlp",
    )(*args)
    return outs if side is not None else outs[0]


def _mixers(proj, batch, length, s5_re, s5_im, ret_s, p):
    tokens = batch * length
    u, packed, g = proj
    ssm, s5_re_new, s5_im_new = _s5_mixer(
        u.reshape(batch, length, SSM_WIDTH), s5_re.reshape(batch, SSM_LANES), s5_im.reshape(batch, SSM_LANES),
        p["s5_maps"], p["d_skip"], p["w_glu"], batch=batch, length=length)
    tok3 = lambda a: a.reshape(batch, length, RET_WIDTH)
    ret, ret_new = _retention(packed.reshape(batch, length, 4 * RET_WIDTH), tok3(g), ret_s, p["ret_gn"])
    states = (s5_re_new.reshape(1, batch, SSM_GROUPS, SSM_STATE),
              s5_im_new.reshape(1, batch, SSM_GROUPS, SSM_STATE), ret_new[None])
    return ssm.reshape(tokens, SSM_WIDTH), ret.reshape(tokens, RET_WIDTH), states


def kernel(x_prompt, x_sample, mem_prompt, state_s5_re, state_s5_im, state_ret, cache_mem_k, cache_mem_v, g_mix, w_in, lam_re, lam_im, log_dt, b_re, b_im, c_re, c_im, d_skip, w_glu, ret_gn, w_out, g_xattn, g_mem, w_mq, w_mk, w_mv, w_mo, g_mlp, w_up, w_down, g_final):
    if g_mix.shape[0] != 1:
        raise ValueError("this kernel implements the single-layer configuration")
    bp, lp, _ = x_prompt.shape
    bs, ls, _ = x_sample.shape
    vec = lambda a: a.reshape(1, -1).astype(F32)
    xp = x_prompt.reshape(bp * lp, D_MODEL)
    xs = x_sample.reshape(bs * ls, D_MODEL)
    proj_p, (w_in_b, w_up_b, w_down_b) = _project(
        xp, vec(g_mix[0]), w_in[0], length=lp, pos0=0.0, casts=(w_up[0], w_down[0]))
    w_out_b, w_mq_b, w_mo_b, w_mk_b, w_mv_b, w_glu_b = (
        w[0].astype(BF16) for w in (w_out, w_mq, w_mo, w_mk, w_mv, w_glu))
    proj_s, _ = _project(xs, vec(g_mix[0]), w_in_b, length=ls, pos0=float(PAST_LEN))
    p = dict(
        s5_maps=_s5_block_maps(lam_re[0], lam_im[0], log_dt[0], b_re[0], b_im[0], c_re[0], c_im[0]),
        d_skip=vec(d_skip[0]), w_glu=w_glu_b, ret_gn=vec(ret_gn[0]))
    g_x = vec(g_xattn[0])
    mlp = (w_mo_b, vec(g_mlp[0]), w_up_b, w_down_b, vec(g_final))

    ssm_s, ret_s, states_s = _mixers(proj_s, bs, ls, state_s5_re[0], state_s5_im[0], state_ret[0], p)
    x1_s, q_s = _out_and_query(xs, ssm_s, ret_s, w_out_b, g_x, w_mq_b)
    if (ls * CACHE_SUB) % BF16_ROWS:
        raise ValueError("cached attention needs whole bf16 tiles of query rows")
    side = (_to_cache_rows(q_s, ls), _to_cache_rows(cache_mem_k, MEM_LEN), _to_cache_rows(cache_mem_v, MEM_LEN))

    mk_rows, mv_rows, mk_p, mv_p = _memory_kv(mem_prompt, vec(g_mem[0]), w_mk_b, w_mv_b)
    zs = jnp.zeros((bp, SSM_GROUPS, SSM_STATE), F32)
    zr = jnp.zeros((bp, RET_HEADS, RET_HEAD_DIM, RET_HEAD_DIM), F32)
    ssm_p, ret_p, states_p = _mixers(proj_p, bp, lp, zs, zs, zr, p)
    x1_p, o_p = _out_and_attention(xp, ssm_p, ret_p, w_out_b, g_x, w_mq_b, mk_p, mv_p, length=lp)
    y_p, o_s = _attn_out_mlp(x1_p, o_p, *mlp, side=side)
    y_s = _attn_out_mlp(x1_s, _from_cache_rows(o_s, ls).reshape(bs * ls, D_MODEL), *mlp)

    kv5 = lambda a: _from_cache_rows(a, MEM_LEN).reshape(1, bp, MEM_LEN, MEM_HEADS, MEM_HEAD_DIM)
    return (y_p.reshape(bp, lp, D_MODEL), y_s.reshape(bs, ls, D_MODEL), *states_p, kv5(mk_rows), kv5(mv_rows),
            *states_s)
```

```python
import functools
import math

import numpy as np
import jax
import jax.numpy as jnp
from jax import lax
from jax.experimental import pallas as pl
from jax.experimental.pallas import tpu as pltpu

F32 = jnp.float32
BF16 = jnp.bfloat16

D_MODEL = 1024
SSM_WIDTH = 512
SSM_GROUP = 16
SSM_GROUPS = 32
SSM_STATE = 64
SSM_LANES = SSM_GROUPS * SSM_STATE
RET_WIDTH = 512
RET_HEADS = 4
RET_HEAD_DIM = 128
RET_CHUNK = 128
ROPE_BASE = 10000.0
MEM_LEN = 256
MEM_HEADS = 4
MEM_HEAD_DIM = 256
MEM_SCALE = MEM_HEAD_DIM ** -0.5
D_FF = 4096
PROJ_WIDTH = SSM_WIDTH + 4 * RET_WIDTH
EPS = 1e-6
PAST_LEN = 16384

LANES = 128
MXU_TILE = 256
BF16_ROWS = 16
VMEM_LIMIT = 56 * 1024 * 1024

ROWS_LARGE = 1024
ROWS_SMALL = 512
S5_STEPS = 256

CACHE_ROWS = MEM_LEN * MEM_HEADS * MEM_HEAD_DIM // LANES
CACHE_SUB = MEM_HEADS * MEM_HEAD_DIM // LANES

assert math.frexp(MEM_SCALE)[0] == 0.5


def _dot(a, b):
    return jnp.dot(a, b, preferred_element_type=F32)


def _dot_nt(a, b):
    return lax.dot_general(a, b, (((1,), (1,)), ((), ())), preferred_element_type=F32)


def _dot_tn(a, b):
    return lax.dot_general(a, b, (((0,), (0,)), ((), ())), preferred_element_type=F32)


def _rmsnorm(x, g):
    return x * lax.rsqrt(jnp.mean(x * x, axis=-1, keepdims=True) + EPS) * g


def _sigmoid(x):
    return 1.0 / (1.0 + jnp.exp(-x))


def _gelu_tanh(x):
    c = math.sqrt(2.0 / math.pi)
    return x * (0.5 * jnp.tanh(x * (c + (c * 0.044715) * (x * x))) + 0.5)


def _params(*sem):
    return pltpu.CompilerParams(dimension_semantics=sem, vmem_limit_bytes=VMEM_LIMIT)


def _full(shape):
    return pl.BlockSpec(shape, lambda *_: (0,) * len(shape))


SSM_BLOCK = 4
SSM_TILE_GROUPS = MXU_TILE // SSM_STATE
SSM_TILES = SSM_GROUPS // SSM_TILE_GROUPS
SSM_TILE_CHANNELS = SSM_TILE_GROUPS * SSM_GROUP
GROUP_SHIFT = SSM_GROUP.bit_length() - 1
STATE_SHIFT = SSM_STATE.bit_length() - 1

assert SSM_BLOCK * SSM_TILE_CHANNELS == MXU_TILE and SSM_TILE_CHANNELS == SSM_STATE


def _s5_prep_kernel(lr_ref, li_ref, ldt_ref, br_ref, bi_ref, cr_ref, ci_ref,
                    ar_ref, ai_ref, winr_ref, wini_ref, woutr_ref, wouti_ref, kdir_ref):
    lr = lr_ref[...]
    li = li_ref[...]
    dt = jnp.exp(ldt_ref[...])
    mag = jnp.exp(lr * dt)
    ab_re = mag * jnp.cos(li * dt)
    ab_im = mag * jnp.sin(li * dt)
    den = lr * lr + li * li
    f_re = ((ab_re - 1.0) * lr + ab_im * li) / den
    f_im = (ab_im * lr - (ab_re - 1.0) * li) / den
    bb_re = f_re * br_ref[...] - f_im * bi_ref[...]
    bb_im = f_re * bi_ref[...] + f_im * br_ref[...]
    cr = cr_ref[...]
    ci = ci_ref[...]
    pows = [(jnp.ones_like(ab_re), jnp.zeros_like(ab_re))]
    for _ in range(SSM_BLOCK):
        pr, pi = pows[-1]
        pows.append((pr * ab_re - pi * ab_im, pr * ab_im + pi * ab_re))
    ar_ref[...], ai_ref[...] = pows[SSM_BLOCK]

    R, T, TG = SSM_BLOCK, SSM_TILES, SSM_TILE_GROUPS
    rows = SSM_GROUPS * SSM_GROUP
    side = SSM_TILE_CHANNELS
    exact = dict(precision=lax.Precision.HIGHEST, preferred_element_type=F32)
    iota = lambda shape, d: lax.broadcasted_iota(jnp.int32, shape, d)
    rep4 = lambda a, axis: jnp.concatenate([a] * TG, axis=axis)
    tiles = lambda a: a.reshape(T, side, a.shape[-1])
    group_of = lambda i: (i & (side - 1)) >> GROUP_SHIFT

    own_in = group_of(iota((rows, MXU_TILE), 0)) == iota((rows, MXU_TILE), 1) >> STATE_SHIFT
    for s in range(R):
        pr, pi = pows[R - 1 - s]
        for ref, w in ((winr_ref, pr * bb_re - pi * bb_im), (wini_ref, pr * bb_im + pi * bb_re)):
            ref[:, side * s:side * (s + 1), :] = tiles(jnp.where(own_in, rep4(w, 1), 0.0)).astype(BF16)

    eye = (iota((rows, rows), 0) == iota((rows, rows), 1)).astype(F32)
    own_out = iota((MXU_TILE, rows), 0) >> STATE_SHIFT == group_of(iota((MXU_TILE, rows), 1))
    for ref, real in ((woutr_ref, True), (wouti_ref, False)):
        per_lag = []
        for j in range(R):
            pr, pi = pows[j + 1]
            w = cr * pr - ci * pi if real else -(cr * pi + ci * pr)
            wt = lax.dot_general(w, eye, (((0,), (0,)), ((), ())), **exact)
            per_lag.append(jnp.where(own_out, rep4(wt, 0), 0.0))
        for n in range(T):
            ref[n] = jnp.concatenate([m[:, side * n:side * (n + 1)] for m in per_lag], axis=1).astype(BF16)

    grouped = lambda a: a.reshape(SSM_GROUPS, SSM_GROUP, SSM_STATE)
    contract = lambda a, b: lax.dot_general(a, b, (((2,), (2,)), ((0,), (0,))), **exact)
    own_dir = group_of(iota((rows, side), 0)) == iota((rows, side), 1) >> GROUP_SHIFT
    by_lag = []
    for d in range(R):
        pr, pi = pows[d]
        er = grouped(pr * bb_re - pi * bb_im)
        ei = grouped(pr * bb_im + pi * bb_re)
        kt = (contract(er, grouped(cr)) - contract(ei, grouped(ci))).reshape(rows, SSM_GROUP)
        by_lag.append(tiles(jnp.where(own_dir, rep4(kt, 1), 0.0)))
    nothing = jnp.zeros_like(by_lag[0])
    for s in range(R):
        kdir_ref[:, side * s:side * (s + 1), :] = jnp.concatenate(
            [by_lag[j - s] if j >= s else nothing for j in range(R)], axis=2).astype(BF16)


def _s5_block_maps(lam_re, lam_im, log_dt, b_re, b_im, c_re, c_im):
    rows = SSM_GROUPS * SSM_GROUP
    rep = lambda a: jnp.repeat(a, SSM_GROUP, axis=0)
    ldt = jnp.broadcast_to(log_dt[:, None], (SSM_GROUPS, SSM_STATE))
    bt = lambda b: b.transpose(0, 2, 1).reshape(rows, SSM_STATE)
    ct = lambda c: c.reshape(rows, SSM_STATE)
    one = jax.ShapeDtypeStruct((rows, SSM_STATE), F32)
    tile = jax.ShapeDtypeStruct((SSM_TILES, MXU_TILE, MXU_TILE), BF16)
    ar, ai, winr, wini, woutr, wouti, direct = pl.pallas_call(
        _s5_prep_kernel,
        out_shape=(one, one, tile, tile, tile, tile, tile),
        compiler_params=pltpu.CompilerParams(vmem_limit_bytes=VMEM_LIMIT),
        name="s5_block_maps",
    )(rep(lam_re), rep(lam_im), rep(ldt), bt(b_re), bt(b_im), ct(c_re), ct(c_im))
    a_re = ar[::SSM_GROUP].reshape(1, SSM_LANES)
    a_im = ai[::SSM_GROUP].reshape(1, SSM_LANES)
    return a_re, a_im, winr, wini, woutr, wouti, direct


def _retention_chunk(length):
    return RET_CHUNK if length % RET_CHUNK == 0 else length


def _log_gamma():
    return np.log(1.0 - 2.0 ** (-5.0 - np.arange(RET_HEADS, dtype=np.float64)))


def _rope_tables(length, pos0):
    half = RET_HEAD_DIM // 2
    inv = ROPE_BASE ** (-np.arange(half, dtype=np.float64) / half)
    ang = (pos0 + np.arange(length, dtype=np.float64))[:, None] * inv[None, :]
    return (np.concatenate([np.cos(ang), np.cos(ang)], axis=1),
            np.concatenate([-np.sin(ang), np.sin(ang)], axis=1))


def _zeta_rows(length):
    chunk = _retention_chunk(length)
    idx = np.arange(chunk, dtype=np.float64)
    zeta = np.exp((chunk - 1.0 - idx)[None, :] * _log_gamma()[:, None])
    return np.concatenate([np.broadcast_to(z[:, None], (chunk, RET_HEAD_DIM)) for z in zeta], axis=1)


def _decay_tables(length, padded):
    chunk = _retention_chunk(length)
    cpad = chunk if padded == length else padded
    lg = _log_gamma()
    idx = np.arange(chunk, dtype=np.float64)
    diff = idx[:, None] - idx[None, :]
    mask = np.where(diff[None] >= 0, np.exp(np.maximum(diff, 0.0)[None] * lg[:, None, None]), 0.0)
    xi = np.exp((idx + 1.0)[None, :] * lg[:, None])
    gamma_c = tuple(float(v) for v in np.exp(chunk * lg))
    mask = np.pad(mask, ((0, 0), (0, cpad - chunk), (0, cpad - chunk)))
    xi = np.broadcast_to(np.pad(xi, ((0, 0), (0, cpad - chunk)))[:, :, None], (RET_HEADS, cpad, RET_HEAD_DIM))
    f = lambda a: jnp.asarray(np.ascontiguousarray(a), dtype=F32)
    return cpad, gamma_c, f(mask), f(xi)


def _ret_lanes(part, head):
    lo = part * RET_WIDTH + head * RET_HEAD_DIM
    return slice(lo, lo + RET_HEAD_DIM)


def _proj_kernel(x_ref, g_ref, w_ref, cc_ref, ss_ref, zeta_ref, *rest, n_cast):
    casts_in, rest = rest[:n_cast], rest[n_cast:]
    u_ref, ret_ref, gate_ref = rest[:3]
    w = w_ref[...]
    if n_cast:
        w = w.astype(BF16)
        for src, dst in zip(casts_in, rest[3:]):
            dst[...] = src[...].astype(BF16)
    h = _rmsnorm(x_ref[...], g_ref[...]).astype(BF16)
    proj = _dot(h, w)
    u_ref[...] = proj[:, :SSM_WIDTH]
    ret_ref[:, 3 * RET_WIDTH:] = proj[:, SSM_WIDTH + 2 * RET_WIDTH:SSM_WIDTH + 3 * RET_WIDTH].astype(BF16)
    gate_ref[...] = proj[:, SSM_WIDTH + 3 * RET_WIDTH:]
    cc = cc_ref[...]
    ss = ss_ref[...]
    rope = lambda a: a * cc + pltpu.roll(a, RET_HEAD_DIM // 2, axis=1) * ss
    for hd in range(RET_HEADS):
        cols = slice(hd * RET_HEAD_DIM, (hd + 1) * RET_HEAD_DIM)
        q = rope(proj[:, SSM_WIDTH + hd * RET_HEAD_DIM:SSM_WIDTH + (hd + 1) * RET_HEAD_DIM])
        k = rope(proj[:, SSM_WIDTH + RET_WIDTH + hd * RET_HEAD_DIM:SSM_WIDTH + RET_WIDTH + (hd + 1) * RET_HEAD_DIM])
        ret_ref[:, _ret_lanes(0, hd)] = (q * (RET_HEAD_DIM ** -0.5)).astype(BF16)
        ret_ref[:, _ret_lanes(1, hd)] = k.astype(BF16)
        ret_ref[:, _ret_lanes(2, hd)] = (k * zeta_ref[:, cols]).astype(BF16)


def _project(x, g_mix, w_in, *, length, pos0, casts=()):
    t = x.shape[0]
    tm = min(t, ROWS_LARGE)
    steps = t // tm
    chunk = _retention_chunk(length)
    casts = (w_in, *casts) if casts else ()
    if tm % chunk or (length % tm and tm % length):
        raise ValueError("token tiles must hold whole retention chunks of whole or repeated sequences")
    cc, ss = _rope_tables(length, pos0)
    if length < tm:
        cc, ss = np.tile(cc, (tm // length, 1)), np.tile(ss, (tm // length, 1))
    tab_blocks = cc.shape[0] // tm
    zeta = np.tile(_zeta_rows(length), (tm // chunk, 1))
    const = lambda a: jnp.asarray(np.ascontiguousarray(a), dtype=F32)
    row = lambda w: pl.BlockSpec((tm, w), lambda i: (i, 0))
    tab = pl.BlockSpec((tm, RET_HEAD_DIM), lambda i: (i % tab_blocks, 0))
    f = jax.ShapeDtypeStruct((t, SSM_WIDTH), F32)
    h = jax.ShapeDtypeStruct((t, 4 * RET_WIDTH), BF16)
    in_specs = [row(D_MODEL), _full((1, D_MODEL)), _full((D_MODEL, PROJ_WIDTH)), tab, tab, _full((tm, RET_WIDTH))]
    out_specs = [row(SSM_WIDTH), row(4 * RET_WIDTH), row(RET_WIDTH)]
    out_shape = [f, h, f]
    if casts:
        for c in casts:
            rows, rem = divmod(c.shape[0], steps)
            if rem or rows % BF16_ROWS:
                raise ValueError("side-cast weights must split into whole bf16 tiles per grid step")
            blk = pl.BlockSpec((rows, c.shape[1]), lambda i: (i, 0))
            in_specs.append(blk)
            out_specs.append(blk)
            out_shape.append(jax.ShapeDtypeStruct(c.shape, BF16))
    outs = pl.pallas_call(
        functools.partial(_proj_kernel, n_cast=len(casts)),
        grid=(steps,),
        in_specs=in_specs,
        out_specs=out_specs,
        out_shape=out_shape,
        compiler_params=_params("arbitrary" if casts else "parallel"),
        name="in_proj",
    )(x, g_mix, w_in, const(cc), const(ss), const(zeta), *casts)
    return outs[:3], outs[3:]


def _s5_block_kernel(u_ref, h0r_ref, h0i_ref, ar_ref, ai_ref, winr_ref, wini_ref, woutr_ref, wouti_ref,
                     kdir_ref, d_ref, wglu_ref, out_ref, hr_ref, hi_ref, vr_scr, vi_scr, xr_scr, xi_scr,
                     *, batch, steps, pair, width):
    R, T = SSM_BLOCK, SSM_TILES
    chan = SSM_TILE_CHANNELS
    blocks = steps // R
    rows, brow = steps * batch, blocks * batch

    @pl.when(pl.program_id(0) == 0)
    def _():
        hr_ref[...] = h0r_ref[...]
        hi_ref[...] = h0i_ref[...]

    u = jnp.swapaxes(u_ref[...], 0, 1).reshape(rows, SSM_WIDTH)
    u4 = u.reshape(blocks, R, batch, SSM_WIDTH)
    lag = [u4[:, s].reshape(brow, SSM_WIDTH) for s in range(R)]
    direct = []
    for n in range(T):
        un = jnp.concatenate([lag[s][:, chan * n:chan * (n + 1)] for s in range(R)], axis=1).astype(BF16)
        lanes = slice(MXU_TILE * n, MXU_TILE * (n + 1))
        vr_scr[:, lanes] = _dot(un, winr_ref[n])
        vi_scr[:, lanes] = _dot(un, wini_ref[n])
        direct.append(_dot(un, kdir_ref[n]))

    for j in range(SSM_LANES // width):
        glob = slice(j * width, (j + 1) * width)
        ar = jnp.broadcast_to(ar_ref[:, glob], (batch, width))
        ai = jnp.broadcast_to(ai_ref[:, glob], (batch, width))
        xr = hr_ref[:, glob]
        xi = hi_ref[:, glob]
        for i in range(blocks // pair):
            prev_r, prev_i = [], []
            for s in range(pair):
                at = slice((i * pair + s) * batch, (i * pair + s + 1) * batch)
                prev_r.append(xr)
                prev_i.append(xi)
                xr, xi = ar * xr - ai * xi + vr_scr[at, glob], ar * xi + ai * xr + vi_scr[at, glob]
            blk = slice(i * pair * batch, (i + 1) * pair * batch)
            xr_scr[blk, glob] = jnp.concatenate(prev_r, axis=0).astype(BF16)
            xi_scr[blk, glob] = jnp.concatenate(prev_i, axis=0).astype(BF16)
        hr_ref[:, glob] = xr
        hi_ref[:, glob] = xi

    y4 = []
    for n in range(T):
        lanes = slice(MXU_TILE * n, MXU_TILE * (n + 1))
        y4.append(_dot(xr_scr[:, lanes], woutr_ref[n]) + _dot(xi_scr[:, lanes], wouti_ref[n]) + direct[n])
    per_lag = [jnp.concatenate([y4[n][:, chan * j:chan * (j + 1)] for n in range(T)], axis=1)
               .reshape(blocks, batch, SSM_WIDTH) for j in range(R)]
    y = jnp.stack(per_lag, axis=1).reshape(rows, SSM_WIDTH) + d_ref[...] * u
    z = _gelu_tanh(y)
    out = z * _sigmoid(_dot(z.astype(BF16), wglu_ref[...]))
    out_ref[...] = jnp.swapaxes(out.reshape(steps, batch, SSM_WIDTH), 0, 1).astype(BF16)


def _s5_mixer(u, h0_re, h0_im, maps, d_skip, w_glu, *, batch, length):
    a_re, a_im, winr, wini, woutr, wouti, direct = maps
    steps = min(length, S5_STEPS)
    if steps % SSM_BLOCK or length % steps:
        raise ValueError("sequence length must be a multiple of the S5 block")
    blocks = steps // SSM_BLOCK
    pair = max(1, min(blocks, BF16_ROWS // batch))
    if (pair * batch) % BF16_ROWS or blocks % pair:
        raise ValueError("S5 state stores need whole bf16 tiles")
    width = SSM_LANES // 2 if batch <= 8 else LANES
    brow = blocks * batch
    kern = functools.partial(_s5_block_kernel, batch=batch, steps=steps, pair=pair, width=width)
    st = jax.ShapeDtypeStruct((batch, SSM_LANES), F32)
    tile = _full((SSM_TILES, MXU_TILE, MXU_TILE))
    return pl.pallas_call(
        kern,
        grid=(length // steps,),
        in_specs=[pl.BlockSpec((batch, steps, SSM_WIDTH), lambda i: (0, i, 0)),
                  _full((batch, SSM_LANES)), _full((batch, SSM_LANES)),
                  _full((1, SSM_LANES)), _full((1, SSM_LANES)),
                  tile, tile, tile, tile, tile,
                  _full((1, SSM_WIDTH)), _full((SSM_WIDTH, SSM_WIDTH))],
        out_specs=[pl.BlockSpec((batch, steps, SSM_WIDTH), lambda i: (0, i, 0)),
                   _full((batch, SSM_LANES)), _full((batch, SSM_LANES))],
        out_shape=(jax.ShapeDtypeStruct((batch, length, SSM_WIDTH), BF16), st, st),
        scratch_shapes=[pltpu.VMEM((brow, SSM_LANES), F32), pltpu.VMEM((brow, SSM_LANES), F32),
                        pltpu.VMEM((brow, SSM_LANES), BF16), pltpu.VMEM((brow, SSM_LANES), BF16)],
        compiler_params=_params("arbitrary"),
        name="s5_mixer",
    )(u, h0_re, h0_im, a_re, a_im, winr, wini, woutr, wouti, direct, d_skip, w_glu)


def _ret_kernel(p_ref, g_ref, mask_ref, xi_ref, gain_ref, s0_ref, o_ref, s_ref,
                *, bb, gamma_c, unroll, chunk, chunks):
    @pl.when(pl.program_id(1) == 0)
    def _():
        s_ref[...] = s0_ref[...]

    heads = range(RET_HEADS)
    cols = [slice(h * RET_HEAD_DIM, (h + 1) * RET_HEAD_DIM) for h in heads]

    def one_chunk(b, rows):
        qb = [p_ref[b, rows, _ret_lanes(0, h)] for h in heads]
        vb = [p_ref[b, rows, _ret_lanes(3, h)] for h in heads]
        s_prev = [s_ref[b, h] for h in heads]
        scores = [(_dot_nt(qb[h], p_ref[b, rows, _ret_lanes(1, h)]) * mask_ref[h]).astype(BF16) for h in heads]
        cross = [_dot(qb[h], s_prev[h].astype(BF16)) * xi_ref[h] for h in heads]
        kv = [_dot_tn(p_ref[b, rows, _ret_lanes(2, h)], vb[h]) for h in heads]
        inner = [_dot(scores[h], vb[h]) for h in heads]
        outs = []
        for h in heads:
            s_ref[b, h] = s_prev[h] * gamma_c[h] + kv[h]
            o = inner[h] + cross[h]
            mu = jnp.mean(o, axis=-1, keepdims=True)
            oc = o - mu
            var = jnp.mean(oc * oc, axis=-1, keepdims=True)
            on = oc * lax.rsqrt(var + EPS) * gain_ref[:, cols[h]]
            gate = g_ref[b, rows, cols[h]]
            outs.append((gate * _sigmoid(gate) * on).astype(BF16))
        o_ref[b, rows, :] = jnp.concatenate(outs, axis=1)

    def per_batch(b, carry):
        for c in range(chunks):
            one_chunk(b, slice(c * chunk, (c + 1) * chunk))
        return carry

    if unroll == bb:
        for b in range(bb):
            per_batch(b, 0)
    else:
        lax.fori_loop(0, bb, per_batch, 0, unroll=unroll)


def _retention(packed, g, s0, gn_gain):
    batch, length, _ = g.shape
    padded = pl.cdiv(length, BF16_ROWS) * BF16_ROWS
    chunk, gamma_c, mask, xi = _decay_tables(length, padded)
    if padded != length:
        pad = lambda a: jnp.pad(a, ((0, 0), (0, padded - length), (0, 0)))
        packed, g = pad(packed), pad(g)
    bb, unroll = (8, 2) if chunk > BF16_ROWS else (16, 16)
    chunks = 2 if (padded // chunk) % 2 == 0 else 1
    tok = pl.BlockSpec((bb, chunks * chunk, RET_WIDTH), lambda i, c: (i, c, 0))
    state = pl.BlockSpec((bb, RET_HEADS, RET_HEAD_DIM, RET_HEAD_DIM), lambda i, c: (i, 0, 0, 0))
    out, s_new = pl.pallas_call(
        functools.partial(_ret_kernel, bb=bb, gamma_c=gamma_c, unroll=unroll, chunk=chunk, chunks=chunks),
        grid=(batch // bb, padded // (chunks * chunk)),
        in_specs=[pl.BlockSpec((bb, chunks * chunk, 4 * RET_WIDTH), lambda i, c: (i, c, 0)), tok,
                  _full((RET_HEADS, chunk, chunk)), _full((RET_HEADS, chunk, RET_HEAD_DIM)),
                  _full((1, RET_WIDTH)), state],
        out_specs=[tok, state],
        out_shape=(jax.ShapeDtypeStruct((batch, padded, RET_WIDTH), BF16),
                   jax.ShapeDtypeStruct(s0.shape, F32)),
        compiler_params=_params("parallel", "arbitrary"),
        name="retention",
    )(packed, g, mask, xi, gn_gain, s0)
    return out[:, :length], s_new


def _mixer_out(x_ref, ssm_ref, ret_ref, wout_ref, rows=slice(None)):
    return (x_ref[rows, :] + _dot(ssm_ref[rows, :], wout_ref[:SSM_WIDTH, :])
            + _dot(ret_ref[rows, :], wout_ref[SSM_WIDTH:, :]))


def _query(x1, g_ref, wq_ref):
    return (_dot(_rmsnorm(x1, g_ref[...]).astype(BF16), wq_ref[...]) * MEM_SCALE).astype(BF16)


def _outq_kernel(x_ref, ssm_ref, ret_ref, wout_ref, g_ref, wq_ref, x1_ref, q_ref):
    x1 = _mixer_out(x_ref, ssm_ref, ret_ref, wout_ref)
    x1_ref[...] = x1
    q_ref[...] = _query(x1, g_ref, wq_ref)


def _out_and_query(x, ssm, ret, w_out, g_xattn, w_mq):
    t = x.shape[0]
    tm = min(t, ROWS_SMALL)
    row = lambda w: pl.BlockSpec((tm, w), lambda i: (i, 0))
    return pl.pallas_call(
        _outq_kernel,
        grid=(t // tm,),
        in_specs=[row(D_MODEL), row(SSM_WIDTH), row(RET_WIDTH), _full((D_MODEL, D_MODEL)),
                  _full((1, D_MODEL)), _full((D_MODEL, D_MODEL))],
        out_specs=[row(D_MODEL), row(D_MODEL)],
        out_shape=(jax.ShapeDtypeStruct((t, D_MODEL), F32), jax.ShapeDtypeStruct((t, D_MODEL), BF16)),
        compiler_params=_params("parallel"),
        name="out_proj_query",
    )(x, ssm, ret, w_out, g_xattn, w_mq)


def _cache_row_order(x):
    halves = MEM_HEAD_DIM // LANES
    tiles = [x[:, (h * halves + d) * LANES:(h * halves + d + 1) * LANES]
             for d in range(halves) for h in range(MEM_HEADS)]
    return jnp.swapaxes(jnp.stack(tiles, axis=0), 0, 1).reshape(x.shape[0] * CACHE_SUB, LANES)


MEMKV_SEQS = 2


def _memkv_kernel(m_ref, g_ref, wk_ref, wv_ref, k_ref, v_ref, kb_ref, vb_ref):
    seqs = m_ref.shape[0]
    m = _rmsnorm(m_ref[...].reshape(seqs * MEM_LEN, D_MODEL), g_ref[...]).astype(BF16)
    k = _dot(m, wk_ref[...])
    v = _dot(m, wv_ref[...])
    k_ref[...] = _cache_row_order(k).reshape(k_ref.shape)
    v_ref[...] = _cache_row_order(v).reshape(v_ref.shape)
    kb_ref[...] = k.astype(BF16).reshape(kb_ref.shape)
    vb_ref[...] = v.astype(BF16).reshape(vb_ref.shape)


def _memory_kv(mem, g_mem, w_mk, w_mv):
    batch = mem.shape[0]
    seqs = MEMKV_SEQS if batch % MEMKV_SEQS == 0 else 1
    tok = pl.BlockSpec((seqs, MEM_LEN, D_MODEL), lambda i: (i, 0, 0))
    cache = pl.BlockSpec((seqs, CACHE_ROWS, LANES), lambda i: (i, 0, 0))
    f = jax.ShapeDtypeStruct((batch, CACHE_ROWS, LANES), F32)
    h = jax.ShapeDtypeStruct((batch, MEM_LEN, D_MODEL), BF16)
    return pl.pallas_call(
        _memkv_kernel,
        grid=(batch // seqs,),
        in_specs=[tok, _full((1, D_MODEL)), _full((D_MODEL, D_MODEL)), _full((D_MODEL, D_MODEL))],
        out_specs=[cache, cache, tok, tok],
        out_shape=(f, f, h, h),
        compiler_params=_params("parallel"),
        name="memory_kv",
    )(mem, g_mem, w_mk, w_mv)


ATTN_ROWS = 512


def _outattn_kernel(x_ref, ssm_ref, ret_ref, wout_ref, g_ref, wq_ref, k_ref, v_ref, x1_ref, o_ref):
    cols = [slice(h * MEM_HEAD_DIM, (h + 1) * MEM_HEAD_DIM) for h in range(MEM_HEADS)]
    groups = [slice(r, r + ATTN_ROWS) for r in range(0, x_ref.shape[0], ATTN_ROWS)]

    def scores_of(rows):
        x1 = _mixer_out(x_ref, ssm_ref, ret_ref, wout_ref, rows)
        x1_ref[rows, :] = x1
        q = _query(x1, g_ref, wq_ref)
        return [_dot_nt(q[:, c], k_ref[0, :, c]) for c in cols]

    def attend(rows, scores):
        outs = []
        for s, c in zip(scores, cols):
            e = jnp.exp(s - jnp.max(s, axis=-1, keepdims=True))
            p = (e / jnp.sum(e, axis=-1, keepdims=True)).astype(BF16)
            outs.append(_dot(p, v_ref[0, :, c]).astype(BF16))
        o_ref[rows, :] = jnp.concatenate(outs, axis=1)

    pending = None
    for rows in groups:
        scores = scores_of(rows)
        if pending is not None:
            attend(*pending)
        pending = (rows, scores)
    attend(*pending)


def _out_and_attention(x, ssm, ret, w_out, g_xattn, w_mq, mk, mv, *, length):
    t = x.shape[0]
    tm = min(length, ROWS_LARGE)
    per_seq = length // tm
    row = lambda w: pl.BlockSpec((tm, w), lambda i: (i, 0))
    mem = pl.BlockSpec((1, MEM_LEN, D_MODEL), lambda i: (i // per_seq, 0, 0))
    return pl.pallas_call(
        _outattn_kernel,
        grid=(t // tm,),
        in_specs=[row(D_MODEL), row(SSM_WIDTH), row(RET_WIDTH), _full((D_MODEL, D_MODEL)),
                  _full((1, D_MODEL)), _full((D_MODEL, D_MODEL)), mem, mem],
        out_specs=[row(D_MODEL), row(D_MODEL)],
        out_shape=(jax.ShapeDtypeStruct((t, D_MODEL), F32), jax.ShapeDtypeStruct((t, D_MODEL), BF16)),
        compiler_params=_params("parallel"),
        name="out_proj_attention",
    )(x, ssm, ret, w_out, g_xattn, w_mq, mk, mv)


def _cached_attention(q_ref, k_ref, v_ref, o_ref, *, bb, length):
    hit = (lax.broadcasted_iota(jnp.int32, (CACHE_SUB, CACHE_ROWS), 0)
           == lax.broadcasted_iota(jnp.int32, (CACHE_SUB, CACHE_ROWS), 1) % CACHE_SUB)
    first_half = lax.broadcasted_iota(jnp.int32, (length, LANES), 1) % CACHE_SUB < MEM_HEADS
    tiles = CACHE_ROWS // LANES

    def class_reduce(x, op):
        shift = CACHE_SUB
        while shift < LANES:
            x = op(x, pltpu.roll(x, shift, axis=1))
            shift *= 2
        return x

    def softmax_rows(r):
        z = jnp.concatenate(
            [jnp.sum(jnp.where(hit, r[CACHE_SUB * t:CACHE_SUB * (t + 1)], 0.0), axis=0, keepdims=True)
             for t in range(length)], axis=0)
        parts = []
        for i in range(tiles):
            zi = z[:, LANES * i:LANES * (i + 1)]
            parts.append(zi + pltpu.roll(zi, LANES - MEM_HEADS, axis=1))
        mx = parts[0]
        for pi in parts[1:]:
            mx = jnp.maximum(mx, pi)
        mx = class_reduce(mx, jnp.maximum)
        es = [jnp.exp(pi - mx) for pi in parts]
        tot = es[0]
        for ei in es[1:]:
            tot = tot + ei
        tot = class_reduce(tot, jnp.add)
        ps = []
        for ei in es:
            pi = ei / tot
            ps.append(jnp.where(first_half, pi, pltpu.roll(pi, MEM_HEADS, axis=1)))
        p = jnp.concatenate(ps, axis=1)
        return jnp.concatenate(
            [jnp.where(hit, jnp.broadcast_to(p[t:t + 1], (CACHE_SUB, CACHE_ROWS)), 0.0)
             for t in range(length)], axis=0).astype(BF16)

    def start():
        return [_dot_nt(q_ref[b], k_ref[b].astype(BF16)) for b in range(bb)]

    def finish(scores):
        probs = [softmax_rows(r) for r in scores]
        for b in range(bb):
            o_ref[b] = _dot(probs[b], v_ref[b].astype(BF16)).astype(BF16)

    return start, finish


def _to_cache_rows(a, lead):
    halves = MEM_HEAD_DIM // LANES
    batch = a.size // (lead * D_MODEL)
    a = a.reshape(batch, lead, MEM_HEADS, halves, LANES).transpose(0, 1, 3, 2, 4)
    return a.reshape(batch, lead * CACHE_SUB, LANES)


def _from_cache_rows(a, lead):
    halves = MEM_HEAD_DIM // LANES
    batch = a.shape[0]
    a = a.reshape(batch, lead, halves, MEM_HEADS, LANES).transpose(0, 1, 3, 2, 4)
    return a.reshape(batch, lead, D_MODEL)


MLP_CHUNK = 1024


def _post_kernel(x1_ref, o_ref, wo_ref, gm_ref, wup_ref, wdn_ref, gf_ref, *rest, side_bb, side_len):
    if side_bb:
        qs_ref, ck_ref, cv_ref, y_ref, os_ref = rest
    else:
        (y_ref,) = rest
    x2 = x1_ref[...] + _dot(o_ref[...], wo_ref[...])
    if side_bb:
        side_start, side_finish = _cached_attention(qs_ref, ck_ref, cv_ref, os_ref, bb=side_bb, length=side_len)
        side_scores = side_start()
    h = _rmsnorm(x2, gm_ref[...]).astype(BF16)
    chunks = [slice(c, c + MLP_CHUNK) for c in range(0, D_FF, MLP_CHUNK)]
    acc = x2
    up = _dot(h, wup_ref[:, chunks[0]])
    for c, cols in enumerate(chunks):
        nxt = _dot(h, wup_ref[:, chunks[c + 1]]) if c + 1 < len(chunks) else None
        if side_bb and c == len(chunks) - 1:
            side_finish(side_scores)
        a = jnp.maximum(up, 0.0)
        acc = acc + _dot((a * a).astype(BF16), wdn_ref[cols, :])
        up = nxt
    y_ref[...] = _rmsnorm(acc, gf_ref[...])


def _attn_out_mlp(x1, o, w_mo, g_mlp, w_up, w_down, g_final, side=None):
    t = x1.shape[0]
    tm = min(t, ROWS_SMALL)
    steps = t // tm
    row = pl.BlockSpec((tm, D_MODEL), lambda i: (i, 0))
    in_specs = [row, row, _full((D_MODEL, D_MODEL)), _full((1, D_MODEL)),
                _full((D_MODEL, D_FF)), _full((D_FF, D_MODEL)), _full((1, D_MODEL))]
    out_specs = [row]
    out_shape = [jax.ShapeDtypeStruct((t, D_MODEL), F32)]
    args = [x1, o, w_mo, g_mlp, w_up, w_down, g_final]
    side_bb = side_len = 0
    if side is not None:
        q_rows = side[0]
        side_bb, rem = divmod(q_rows.shape[0], steps)
        if rem or not side_bb:
            raise ValueError("side attention sequences must spread evenly over the grid steps")
        side_len = q_rows.shape[1] // CACHE_SUB
        blk = lambda r: pl.BlockSpec((side_bb, r, LANES), lambda i: (i, 0, 0))
        in_specs += [blk(q_rows.shape[1]), blk(CACHE_ROWS), blk(CACHE_ROWS)]
        out_specs.append(blk(q_rows.shape[1]))
        out_shape.append(jax.ShapeDtypeStruct(q_rows.shape, BF16))
        args += list(side)
    outs = pl.pallas_call(
        functools.partial(_post_kernel, side_bb=side_bb, side_len=side_len),
        grid=(steps,),
        in_specs=in_specs,
        out_specs=out_specs,
        out_shape=out_shape,
        compiler_params=_params("parallel"),
        name="attn_out_mlp",
    )(*args)
    return outs if side is not None else outs[0]


def _mixers(proj, batch, length, s5_re, s5_im, ret_s, p):
    tokens = batch * length
    u, packed, g = proj
    ssm, s5_re_new, s5_im_new = _s5_mixer(
        u.reshape(batch, length, SSM_WIDTH), s5_re.reshape(batch, SSM_LANES), s5_im.reshape(batch, SSM_LANES),
        p["s5_maps"], p["d_skip"], p["w_glu"], batch=batch, length=length)
    tok3 = lambda a: a.reshape(batch, length, RET_WIDTH)
    ret, ret_new = _retention(packed.reshape(batch, length, 4 * RET_WIDTH), tok3(g), ret_s, p["ret_gn"])
    states = (s5_re_new.reshape(1, batch, SSM_GROUPS, SSM_STATE),
              s5_im_new.reshape(1, batch, SSM_GROUPS, SSM_STATE), ret_new[None])
    return ssm.reshape(tokens, SSM_WIDTH), ret.reshape(tokens, RET_WIDTH), states


def kernel(x_prompt, x_sample, mem_prompt, state_s5_re, state_s5_im, state_ret, cache_mem_k, cache_mem_v, g_mix, w_in, lam_re, lam_im, log_dt, b_re, b_im, c_re, c_im, d_skip, w_glu, ret_gn, w_out, g_xattn, g_mem, w_mq, w_mk, w_mv, w_mo, g_mlp, w_up, w_down, g_final):
    if g_mix.shape[0] != 1:
        raise ValueError("this kernel implements the single-layer configuration")
    bp, lp, _ = x_prompt.shape
    bs, ls, _ = x_sample.shape
    vec = lambda a: a.reshape(1, -1).astype(F32)
    xp = x_prompt.reshape(bp * lp, D_MODEL)
    xs = x_sample.reshape(bs * ls, D_MODEL)
    proj_p, (w_in_b, w_up_b, w_down_b) = _project(
        xp, vec(g_mix[0]), w_in[0], length=lp, pos0=0.0, casts=(w_up[0], w_down[0]))
    w_out_b, w_mq_b, w_mo_b, w_mk_b, w_mv_b, w_glu_b = (
        w[0].astype(BF16) for w in (w_out, w_mq, w_mo, w_mk, w_mv, w_glu))
    proj_s, _ = _project(xs, vec(g_mix[0]), w_in_b, length=ls, pos0=float(PAST_LEN))
    p = dict(
        s5_maps=_s5_block_maps(lam_re[0], lam_im[0], log_dt[0], b_re[0], b_im[0], c_re[0], c_im[0]),
        d_skip=vec(d_skip[0]), w_glu=w_glu_b, ret_gn=vec(ret_gn[0]))
    g_x = vec(g_xattn[0])
    mlp = (w_mo_b, vec(g_mlp[0]), w_up_b, w_down_b, vec(g_final))

    ssm_s, ret_s, states_s = _mixers(proj_s, bs, ls, state_s5_re[0], state_s5_im[0], state_ret[0], p)
    x1_s, q_s = _out_and_query(xs, ssm_s, ret_s, w_out_b, g_x, w_mq_b)
    if (ls * CACHE_SUB) % BF16_ROWS:
        raise ValueError("cached attention needs whole bf16 tiles of query rows")
    side = (_to_cache_rows(q_s, ls), _to_cache_rows(cache_mem_k, MEM_LEN), _to_cache_rows(cache_mem_v, MEM_LEN))

    mk_rows, mv_rows, mk_p, mv_p = _memory_kv(mem_prompt, vec(g_mem[0]), w_mk_b, w_mv_b)
    zs = jnp.zeros((bp, SSM_GROUPS, SSM_STATE), F32)
    zr = jnp.zeros((bp, RET_HEADS, RET_HEAD_DIM, RET_HEAD_DIM), F32)
    ssm_p, ret_p, states_p = _mixers(proj_p, bp, lp, zs, zs, zr, p)
    x1_p, o_p = _out_and_attention(xp, ssm_p, ret_p, w_out_b, g_x, w_mq_b, mk_p, mv_p, length=lp)
    y_p, o_s = _attn_out_mlp(x1_p, o_p, *mlp, side=side)
    y_s = _attn_out_mlp(x1_s, _from_cache_rows(o_s, ls).reshape(bs * ls, D_MODEL), *mlp)

    kv5 = lambda a: _from_cache_rows(a, MEM_LEN).reshape(1, bp, MEM_LEN, MEM_HEADS, MEM_HEAD_DIM)
    return (y_p.reshape(bp, lp, D_MODEL), y_s.reshape(bs, ls, D_MODEL), *states_p, kv5(mk_rows), kv5(mv_rows),
            *states_s)
```

```python
import functools
import math

import numpy as np
import jax
import jax.numpy as jnp
from jax import lax
from jax.experimental import pallas as pl
from jax.experimental.pallas import tpu as pltpu

F32 = jnp.float32
BF16 = jnp.bfloat16

D_MODEL = 1024
SSM_WIDTH = 512
SSM_GROUP = 16
SSM_GROUPS = 32
SSM_STATE = 64
SSM_LANES = SSM_GROUPS * SSM_STATE
RET_WIDTH = 512
RET_HEADS = 4
RET_HEAD_DIM = 128
RET_CHUNK = 128
ROPE_BASE = 10000.0
MEM_LEN = 256
MEM_HEADS = 4
MEM_HEAD_DIM = 256
MEM_SCALE = MEM_HEAD_DIM ** -0.5
D_FF = 4096
PROJ_WIDTH = SSM_WIDTH + 4 * RET_WIDTH
EPS = 1e-6
PAST_LEN = 16384

LANES = 128
MXU_TILE = 256
BF16_ROWS = 16
VMEM_LIMIT = 56 * 1024 * 1024

ROWS_LARGE = 1024
ROWS_SMALL = 512
S5_STEPS = 256

CACHE_ROWS = MEM_LEN * MEM_HEADS * MEM_HEAD_DIM // LANES
CACHE_SUB = MEM_HEADS * MEM_HEAD_DIM // LANES

assert math.frexp(MEM_SCALE)[0] == 0.5


def _dot(a, b):
    return jnp.dot(a, b, preferred_element_type=F32)


def _dot_nt(a, b):
    return lax.dot_general(a, b, (((1,), (1,)), ((), ())), preferred_element_type=F32)


def _dot_tn(a, b):
    return lax.dot_general(a, b, (((0,), (0,)), ((), ())), preferred_element_type=F32)


def _rmsnorm(x, g):
    return x * lax.rsqrt(jnp.mean(x * x, axis=-1, keepdims=True) + EPS) * g


def _sigmoid(x):
    return 1.0 / (1.0 + jnp.exp(-x))


def _gelu_tanh(x):
    c = math.sqrt(2.0 / math.pi)
    return x * (0.5 * jnp.tanh(x * (c + (c * 0.044715) * (x * x))) + 0.5)


def _params(*sem):
    return pltpu.CompilerParams(dimension_semantics=sem, vmem_limit_bytes=VMEM_LIMIT)


def _full(shape):
    return pl.BlockSpec(shape, lambda *_: (0,) * len(shape))


SSM_BLOCK = 4
SSM_TILE_GROUPS = MXU_TILE // SSM_STATE
SSM_TILES = SSM_GROUPS // SSM_TILE_GROUPS
SSM_TILE_CHANNELS = SSM_TILE_GROUPS * SSM_GROUP
GROUP_SHIFT = SSM_GROUP.bit_length() - 1
STATE_SHIFT = SSM_STATE.bit_length() - 1

assert SSM_BLOCK * SSM_TILE_CHANNELS == MXU_TILE and SSM_TILE_CHANNELS == SSM_STATE


def _s5_prep_kernel(lr_ref, li_ref, ldt_ref, br_ref, bi_ref, cr_ref, ci_ref,
                    ar_ref, ai_ref, winr_ref, wini_ref, woutr_ref, wouti_ref, kdir_ref):
    lr = lr_ref[...]
    li = li_ref[...]
    dt = jnp.exp(ldt_ref[...])
    mag = jnp.exp(lr * dt)
    ab_re = mag * jnp.cos(li * dt)
    ab_im = mag * jnp.sin(li * dt)
    den = lr * lr + li * li
    f_re = ((ab_re - 1.0) * lr + ab_im * li) / den
    f_im = (ab_im * lr - (ab_re - 1.0) * li) / den
    bb_re = f_re * br_ref[...] - f_im * bi_ref[...]
    bb_im = f_re * bi_ref[...] + f_im * br_ref[...]
    cr = cr_ref[...]
    ci = ci_ref[...]
    pows = [(jnp.ones_like(ab_re), jnp.zeros_like(ab_re))]
    for _ in range(SSM_BLOCK):
        pr, pi = pows[-1]
        pows.append((pr * ab_re - pi * ab_im, pr * ab_im + pi * ab_re))
    ar_ref[...], ai_ref[...] = pows[SSM_BLOCK]

    R, T, TG = SSM_BLOCK, SSM_TILES, SSM_TILE_GROUPS
    rows = SSM_GROUPS * SSM_GROUP
    side = SSM_TILE_CHANNELS
    exact = dict(precision=lax.Precision.HIGHEST, preferred_element_type=F32)
    iota = lambda shape, d: lax.broadcasted_iota(jnp.int32, shape, d)
    rep4 = lambda a, axis: jnp.concatenate([a] * TG, axis=axis)
    tiles = lambda a: a.reshape(T, side, a.shape[-1])
    group_of = lambda i: (i & (side - 1)) >> GROUP_SHIFT

    own_in = group_of(iota((rows, MXU_TILE), 0)) == iota((rows, MXU_TILE), 1) >> STATE_SHIFT
    for s in range(R):
        pr, pi = pows[R - 1 - s]
        for ref, w in ((winr_ref, pr * bb_re - pi * bb_im), (wini_ref, pr * bb_im + pi * bb_re)):
            ref[:, side * s:side * (s + 1), :] = tiles(jnp.where(own_in, rep4(w, 1), 0.0)).astype(BF16)

    own_out = iota((MXU_TILE, rows), 0) >> STATE_SHIFT == group_of(iota((MXU_TILE, rows), 1))
    for ref, real in ((woutr_ref, True), (wouti_ref, False)):
        per_lag = []
        for j in range(R):
            pr, pi = pows[j + 1]
            w = cr * pr - ci * pi if real else -(cr * pi + ci * pr)
            wt = w.T
            per_lag.append(jnp.where(own_out, rep4(wt, 0), 0.0))
        for n in range(T):
            ref[n] = jnp.concatenate([m[:, side * n:side * (n + 1)] for m in per_lag], axis=1).astype(BF16)

    grouped = lambda a: a.reshape(SSM_GROUPS, SSM_GROUP, SSM_STATE)
    contract = lambda a, b: lax.dot_general(a, b, (((2,), (2,)), ((0,), (0,))), **exact)
    own_dir = group_of(iota((rows, side), 0)) == iota((rows, side), 1) >> GROUP_SHIFT
    by_lag = []
    for d in range(R):
        pr, pi = pows[d]
        er = grouped(pr * bb_re - pi * bb_im)
        ei = grouped(pr * bb_im + pi * bb_re)
        kt = (contract(er, grouped(cr)) - contract(ei, grouped(ci))).reshape(rows, SSM_GROUP)
        by_lag.append(tiles(jnp.where(own_dir, rep4(kt, 1), 0.0)))
    nothing = jnp.zeros_like(by_lag[0])
    for s in range(R):
        kdir_ref[:, side * s:side * (s + 1), :] = jnp.concatenate(
            [by_lag[j - s] if j >= s else nothing for j in range(R)], axis=2).astype(BF16)


def _s5_block_maps(lam_re, lam_im, log_dt, b_re, b_im, c_re, c_im):
    rows = SSM_GROUPS * SSM_GROUP
    rep = lambda a: jnp.repeat(a, SSM_GROUP, axis=0)
    ldt = jnp.broadcast_to(log_dt[:, None], (SSM_GROUPS, SSM_STATE))
    bt = lambda b: b.transpose(0, 2, 1).reshape(rows, SSM_STATE)
    ct = lambda c: c.reshape(rows, SSM_STATE)
    one = jax.ShapeDtypeStruct((rows, SSM_STATE), F32)
    tile = jax.ShapeDtypeStruct((SSM_TILES, MXU_TILE, MXU_TILE), BF16)
    ar, ai, winr, wini, woutr, wouti, direct = pl.pallas_call(
        _s5_prep_kernel,
        out_shape=(one, one, tile, tile, tile, tile, tile),
        compiler_params=pltpu.CompilerParams(vmem_limit_bytes=VMEM_LIMIT),
        name="s5_block_maps",
    )(rep(lam_re), rep(lam_im), rep(ldt), bt(b_re), bt(b_im), ct(c_re), ct(c_im))
    a_re = ar[::SSM_GROUP].reshape(1, SSM_LANES)
    a_im = ai[::SSM_GROUP].reshape(1, SSM_LANES)
    return a_re, a_im, winr, wini, woutr, wouti, direct


def _retention_chunk(length):
    return RET_CHUNK if length % RET_CHUNK == 0 else length


def _log_gamma():
    return np.log(1.0 - 2.0 ** (-5.0 - np.arange(RET_HEADS, dtype=np.float64)))


def _rope_tables(length, pos0):
    half = RET_HEAD_DIM // 2
    inv = ROPE_BASE ** (-np.arange(half, dtype=np.float64) / half)
    ang = (pos0 + np.arange(length, dtype=np.float64))[:, None] * inv[None, :]
    return (np.concatenate([np.cos(ang), np.cos(ang)], axis=1),
            np.concatenate([-np.sin(ang), np.sin(ang)], axis=1))


def _zeta_rows(length):
    chunk = _retention_chunk(length)
    idx = np.arange(chunk, dtype=np.float64)
    zeta = np.exp((chunk - 1.0 - idx)[None, :] * _log_gamma()[:, None])
    return np.concatenate([np.broadcast_to(z[:, None], (chunk, RET_HEAD_DIM)) for z in zeta], axis=1)


def _decay_tables(length, padded):
    chunk = _retention_chunk(length)
    cpad = chunk if padded == length else padded
    lg = _log_gamma()
    idx = np.arange(chunk, dtype=np.float64)
    diff = idx[:, None] - idx[None, :]
    mask = np.where(diff[None] >= 0, np.exp(np.maximum(diff, 0.0)[None] * lg[:, None, None]), 0.0)
    xi = np.exp((idx + 1.0)[None, :] * lg[:, None])
    gamma_c = tuple(float(v) for v in np.exp(chunk * lg))
    mask = np.pad(mask, ((0, 0), (0, cpad - chunk), (0, cpad - chunk)))
    xi = np.broadcast_to(np.pad(xi, ((0, 0), (0, cpad - chunk)))[:, :, None], (RET_HEADS, cpad, RET_HEAD_DIM))
    f = lambda a: jnp.asarray(np.ascontiguousarray(a), dtype=F32)
    return cpad, gamma_c, f(mask), f(xi)


def _ret_lanes(part, head):
    lo = part * RET_WIDTH + head * RET_HEAD_DIM
    return slice(lo, lo + RET_HEAD_DIM)


def _proj_kernel(x_ref, g_ref, w_ref, cc_ref, ss_ref, zeta_ref, *rest, n_cast):
    casts_in, rest = rest[:n_cast], rest[n_cast:]
    u_ref, ret_ref, gate_ref = rest[:3]
    w = w_ref[...]
    if n_cast:
        w = w.astype(BF16)
        for src, dst in zip(casts_in, rest[3:]):
            dst[...] = src[...].astype(BF16)
    h = _rmsnorm(x_ref[...], g_ref[...]).astype(BF16)
    proj = _dot(h, w)
    u_ref[...] = proj[:, :SSM_WIDTH]
    ret_ref[:, 3 * RET_WIDTH:] = proj[:, SSM_WIDTH + 2 * RET_WIDTH:SSM_WIDTH + 3 * RET_WIDTH].astype(BF16)
    gate_ref[...] = proj[:, SSM_WIDTH + 3 * RET_WIDTH:]
    cc = cc_ref[...]
    ss = ss_ref[...]
    rope = lambda a: a * cc + pltpu.roll(a, RET_HEAD_DIM // 2, axis=1) * ss
    for hd in range(RET_HEADS):
        cols = slice(hd * RET_HEAD_DIM, (hd + 1) * RET_HEAD_DIM)
        q = rope(proj[:, SSM_WIDTH + hd * RET_HEAD_DIM:SSM_WIDTH + (hd + 1) * RET_HEAD_DIM])
        k = rope(proj[:, SSM_WIDTH + RET_WIDTH + hd * RET_HEAD_DIM:SSM_WIDTH + RET_WIDTH + (hd + 1) * RET_HEAD_DIM])
        ret_ref[:, _ret_lanes(0, hd)] = (q * (RET_HEAD_DIM ** -0.5)).astype(BF16)
        ret_ref[:, _ret_lanes(1, hd)] = k.astype(BF16)
        ret_ref[:, _ret_lanes(2, hd)] = (k * zeta_ref[:, cols]).astype(BF16)


def _project(x, g_mix, w_in, *, length, pos0, casts=()):
    t = x.shape[0]
    tm = min(t, ROWS_LARGE)
    steps = t // tm
    chunk = _retention_chunk(length)
    casts = (w_in, *casts) if casts else ()
    if tm % chunk or (length % tm and tm % length):
        raise ValueError("token tiles must hold whole retention chunks of whole or repeated sequences")
    cc, ss = _rope_tables(length, pos0)
    if length < tm:
        cc, ss = np.tile(cc, (tm // length, 1)), np.tile(ss, (tm // length, 1))
    tab_blocks = cc.shape[0] // tm
    zeta = np.tile(_zeta_rows(length), (tm // chunk, 1))
    const = lambda a: jnp.asarray(np.ascontiguousarray(a), dtype=F32)
    row = lambda w: pl.BlockSpec((tm, w), lambda i: (i, 0))
    tab = pl.BlockSpec((tm, RET_HEAD_DIM), lambda i: (i % tab_blocks, 0))
    f = jax.ShapeDtypeStruct((t, SSM_WIDTH), F32)
    h = jax.ShapeDtypeStruct((t, 4 * RET_WIDTH), BF16)
    in_specs = [row(D_MODEL), _full((1, D_MODEL)), _full((D_MODEL, PROJ_WIDTH)), tab, tab, _full((tm, RET_WIDTH))]
    out_specs = [row(SSM_WIDTH), row(4 * RET_WIDTH), row(RET_WIDTH)]
    out_shape = [f, h, f]
    if casts:
        for c in casts:
            rows, rem = divmod(c.shape[0], steps)
            if rem or rows % BF16_ROWS:
                raise ValueError("side-cast weights must split into whole bf16 tiles per grid step")
            blk = pl.BlockSpec((rows, c.shape[1]), lambda i: (i, 0))
            in_specs.append(blk)
            out_specs.append(blk)
            out_shape.append(jax.ShapeDtypeStruct(c.shape, BF16))
    outs = pl.pallas_call(
        functools.partial(_proj_kernel, n_cast=len(casts)),
        grid=(steps,),
        in_specs=in_specs,
        out_specs=out_specs,
        out_shape=out_shape,
        compiler_params=_params("arbitrary" if casts else "parallel"),
        name="in_proj",
    )(x, g_mix, w_in, const(cc), const(ss), const(zeta), *casts)
    return outs[:3], outs[3:]


def _s5_block_kernel(u_ref, h0r_ref, h0i_ref, ar_ref, ai_ref, winr_ref, wini_ref, woutr_ref, wouti_ref,
                     kdir_ref, d_ref, wglu_ref, out_ref, hr_ref, hi_ref, vr_scr, vi_scr, xr_scr, xi_scr,
                     *, batch, steps, pair, width):
    R, T = SSM_BLOCK, SSM_TILES
    chan = SSM_TILE_CHANNELS
    blocks = steps // R
    rows, brow = steps * batch, blocks * batch

    @pl.when(pl.program_id(0) == 0)
    def _():
        hr_ref[...] = h0r_ref[...]
        hi_ref[...] = h0i_ref[...]

    u = jnp.swapaxes(u_ref[...], 0, 1).reshape(rows, SSM_WIDTH)
    u4 = u.reshape(blocks, R, batch, SSM_WIDTH)
    lag = [u4[:, s].reshape(brow, SSM_WIDTH) for s in range(R)]
    direct = []
    for n in range(T):
        un = jnp.concatenate([lag[s][:, chan * n:chan * (n + 1)] for s in range(R)], axis=1).astype(BF16)
        lanes = slice(MXU_TILE * n, MXU_TILE * (n + 1))
        vr_scr[:, lanes] = _dot(un, winr_ref[n])
        vi_scr[:, lanes] = _dot(un, wini_ref[n])
        direct.append(_dot(un, kdir_ref[n]))

    for j in range(SSM_LANES // width):
        glob = slice(j * width, (j + 1) * width)
        ar = jnp.broadcast_to(ar_ref[:, glob], (batch, width))
        ai = jnp.broadcast_to(ai_ref[:, glob], (batch, width))
        xr = hr_ref[:, glob]
        xi = hi_ref[:, glob]
        for i in range(blocks // pair):
            prev_r, prev_i = [], []
            for s in range(pair):
                at = slice((i * pair + s) * batch, (i * pair + s + 1) * batch)
                prev_r.append(xr)
                prev_i.append(xi)
                xr, xi = ar * xr - ai * xi + vr_scr[at, glob], ar * xi + ai * xr + vi_scr[at, glob]
            blk = slice(i * pair * batch, (i + 1) * pair * batch)
            xr_scr[blk, glob] = jnp.concatenate(prev_r, axis=0).astype(BF16)
            xi_scr[blk, glob] = jnp.concatenate(prev_i, axis=0).astype(BF16)
        hr_ref[:, glob] = xr
        hi_ref[:, glob] = xi

    y4 = []
    for n in range(T):
        lanes = slice(MXU_TILE * n, MXU_TILE * (n + 1))
        y4.append(_dot(xr_scr[:, lanes], woutr_ref[n]) + _dot(xi_scr[:, lanes], wouti_ref[n]) + direct[n])
    per_lag = [jnp.concatenate([y4[n][:, chan * j:chan * (j + 1)] for n in range(T)], axis=1)
               .reshape(blocks, batch, SSM_WIDTH) for j in range(R)]
    y = jnp.stack(per_lag, axis=1).reshape(rows, SSM_WIDTH) + d_ref[...] * u
    z = _gelu_tanh(y)
    out = z * _sigmoid(_dot(z.astype(BF16), wglu_ref[...]))
    out_ref[...] = jnp.swapaxes(out.reshape(steps, batch, SSM_WIDTH), 0, 1).astype(BF16)


def _s5_mixer(u, h0_re, h0_im, maps, d_skip, w_glu, *, batch, length):
    a_re, a_im, winr, wini, woutr, wouti, direct = maps
    steps = min(length, S5_STEPS)
    if steps % SSM_BLOCK or length % steps:
        raise ValueError("sequence length must be a multiple of the S5 block")
    blocks = steps // SSM_BLOCK
    pair = max(1, min(blocks, BF16_ROWS // batch))
    if (pair * batch) % BF16_ROWS or blocks % pair:
        raise ValueError("S5 state stores need whole bf16 tiles")
    width = SSM_LANES // 2 if batch <= 8 else LANES
    brow = blocks * batch
    kern = functools.partial(_s5_block_kernel, batch=batch, steps=steps, pair=pair, width=width)
    st = jax.ShapeDtypeStruct((batch, SSM_LANES), F32)
    tile = _full((SSM_TILES, MXU_TILE, MXU_TILE))
    return pl.pallas_call(
        kern,
        grid=(length // steps,),
        in_specs=[pl.BlockSpec((batch, steps, SSM_WIDTH), lambda i: (0, i, 0)),
                  _full((batch, SSM_LANES)), _full((batch, SSM_LANES)),
                  _full((1, SSM_LANES)), _full((1, SSM_LANES)),
                  tile, tile, tile, tile, tile,
                  _full((1, SSM_WIDTH)), _full((SSM_WIDTH, SSM_WIDTH))],
        out_specs=[pl.BlockSpec((batch, steps, SSM_WIDTH), lambda i: (0, i, 0)),
                   _full((batch, SSM_LANES)), _full((batch, SSM_LANES))],
        out_shape=(jax.ShapeDtypeStruct((batch, length, SSM_WIDTH), BF16), st, st),
        scratch_shapes=[pltpu.VMEM((brow, SSM_LANES), F32), pltpu.VMEM((brow, SSM_LANES), F32),
                        pltpu.VMEM((brow, SSM_LANES), BF16), pltpu.VMEM((brow, SSM_LANES), BF16)],
        compiler_params=_params("arbitrary"),
        name="s5_mixer",
    )(u, h0_re, h0_im, a_re, a_im, winr, wini, woutr, wouti, direct, d_skip, w_glu)


def _ret_kernel(p_ref, g_ref, mask_ref, xi_ref, gain_ref, s0_ref, o_ref, s_ref,
                *, bb, gamma_c, unroll, chunk, chunks):
    @pl.when(pl.program_id(1) == 0)
    def _():
        s_ref[...] = s0_ref[...]

    heads = range(RET_HEADS)
    cols = [slice(h * RET_HEAD_DIM, (h + 1) * RET_HEAD_DIM) for h in heads]

    def one_chunk(b, rows):
        qb = [p_ref[b, rows, _ret_lanes(0, h)] for h in heads]
        vb = [p_ref[b, rows, _ret_lanes(3, h)] for h in heads]
        s_prev = [s_ref[b, h] for h in heads]
        scores = [(_dot_nt(qb[h], p_ref[b, rows, _ret_lanes(1, h)]) * mask_ref[h]).astype(BF16) for h in heads]
        cross = [_dot(qb[h], s_prev[h].astype(BF16)) * xi_ref[h] for h in heads]
        kv = [_dot_tn(p_ref[b, rows, _ret_lanes(2, h)], vb[h]) for h in heads]
        inner = [_dot(scores[h], vb[h]) for h in heads]
        outs = []
        for h in heads:
            s_ref[b, h] = s_prev[h] * gamma_c[h] + kv[h]
            o = inner[h] + cross[h]
            mu = jnp.mean(o, axis=-1, keepdims=True)
            oc = o - mu
            var = jnp.mean(oc * oc, axis=-1, keepdims=True)
            on = oc * lax.rsqrt(var + EPS) * gain_ref[:, cols[h]]
            gate = g_ref[b, rows, cols[h]]
            outs.append((gate * _sigmoid(gate) * on).astype(BF16))
        o_ref[b, rows, :] = jnp.concatenate(outs, axis=1)

    def per_batch(b, carry):
        for c in range(chunks):
            one_chunk(b, slice(c * chunk, (c + 1) * chunk))
        return carry

    if unroll == bb:
        for b in range(bb):
            per_batch(b, 0)
    else:
        lax.fori_loop(0, bb, per_batch, 0, unroll=unroll)


def _retention(packed, g, s0, gn_gain):
    batch, length, _ = g.shape
    padded = pl.cdiv(length, BF16_ROWS) * BF16_ROWS
    chunk, gamma_c, mask, xi = _decay_tables(length, padded)
    if padded != length:
        pad = lambda a: jnp.pad(a, ((0, 0), (0, padded - length), (0, 0)))
        packed, g = pad(packed), pad(g)
    bb, unroll = (8, 2) if chunk > BF16_ROWS else (16, 16)
    chunks = 2 if (padded // chunk) % 2 == 0 else 1
    tok = pl.BlockSpec((bb, chunks * chunk, RET_WIDTH), lambda i, c: (i, c, 0))
    state = pl.BlockSpec((bb, RET_HEADS, RET_HEAD_DIM, RET_HEAD_DIM), lambda i, c: (i, 0, 0, 0))
    out, s_new = pl.pallas_call(
        functools.partial(_ret_kernel, bb=bb, gamma_c=gamma_c, unroll=unroll, chunk=chunk, chunks=chunks),
        grid=(batch // bb, padded // (chunks * chunk)),
        in_specs=[pl.BlockSpec((bb, chunks * chunk, 4 * RET_WIDTH), lambda i, c: (i, c, 0)), tok,
                  _full((RET_HEADS, chunk, chunk)), _full((RET_HEADS, chunk, RET_HEAD_DIM)),
                  _full((1, RET_WIDTH)), state],
        out_specs=[tok, state],
        out_shape=(jax.ShapeDtypeStruct((batch, padded, RET_WIDTH), BF16),
                   jax.ShapeDtypeStruct(s0.shape, F32)),
        compiler_params=_params("parallel", "arbitrary"),
        name="retention",
    )(packed, g, mask, xi, gn_gain, s0)
    return out[:, :length], s_new


def _mixer_out(x_ref, ssm_ref, ret_ref, wout_ref, rows=slice(None)):
    return (x_ref[rows, :] + _dot(ssm_ref[rows, :], wout_ref[:SSM_WIDTH, :])
            + _dot(ret_ref[rows, :], wout_ref[SSM_WIDTH:, :]))


def _query(x1, g_ref, wq_ref):
    return (_dot(_rmsnorm(x1, g_ref[...]).astype(BF16), wq_ref[...]) * MEM_SCALE).astype(BF16)


def _outq_kernel(x_ref, ssm_ref, ret_ref, wout_ref, g_ref, wq_ref, x1_ref, q_ref):
    x1 = _mixer_out(x_ref, ssm_ref, ret_ref, wout_ref)
    x1_ref[...] = x1
    q_ref[...] = _query(x1, g_ref, wq_ref)


def _out_and_query(x, ssm, ret, w_out, g_xattn, w_mq):
    t = x.shape[0]
    tm = min(t, ROWS_SMALL)
    row = lambda w: pl.BlockSpec((tm, w), lambda i: (i, 0))
    return pl.pallas_call(
        _outq_kernel,
        grid=(t // tm,),
        in_specs=[row(D_MODEL), row(SSM_WIDTH), row(RET_WIDTH), _full((D_MODEL, D_MODEL)),
                  _full((1, D_MODEL)), _full((D_MODEL, D_MODEL))],
        out_specs=[row(D_MODEL), row(D_MODEL)],
        out_shape=(jax.ShapeDtypeStruct((t, D_MODEL), F32), jax.ShapeDtypeStruct((t, D_MODEL), BF16)),
        compiler_params=_params("parallel"),
        name="out_proj_query",
    )(x, ssm, ret, w_out, g_xattn, w_mq)


def _cache_row_order(x):
    halves = MEM_HEAD_DIM // LANES
    tiles = [x[:, (h * halves + d) * LANES:(h * halves + d + 1) * LANES]
             for d in range(halves) for h in range(MEM_HEADS)]
    return jnp.swapaxes(jnp.stack(tiles, axis=0), 0, 1).reshape(x.shape[0] * CACHE_SUB, LANES)


MEMKV_SEQS = 2


def _memkv_kernel(m_ref, g_ref, wk_ref, wv_ref, k_ref, v_ref, kb_ref, vb_ref):
    seqs = m_ref.shape[0]
    m = _rmsnorm(m_ref[...].reshape(seqs * MEM_LEN, D_MODEL), g_ref[...]).astype(BF16)
    k = _dot(m, wk_ref[...])
    v = _dot(m, wv_ref[...])
    k_ref[...] = _cache_row_order(k).reshape(k_ref.shape)
    v_ref[...] = _cache_row_order(v).reshape(v_ref.shape)
    kb_ref[...] = k.astype(BF16).reshape(kb_ref.shape)
    vb_ref[...] = v.astype(BF16).reshape(vb_ref.shape)


def _memory_kv(mem, g_mem, w_mk, w_mv):
    batch = mem.shape[0]
    seqs = MEMKV_SEQS if batch % MEMKV_SEQS == 0 else 1
    tok = pl.BlockSpec((seqs, MEM_LEN, D_MODEL), lambda i: (i, 0, 0))
    cache = pl.BlockSpec((seqs, CACHE_ROWS, LANES), lambda i: (i, 0, 0))
    f = jax.ShapeDtypeStruct((batch, CACHE_ROWS, LANES), F32)
    h = jax.ShapeDtypeStruct((batch, MEM_LEN, D_MODEL), BF16)
    return pl.pallas_call(
        _memkv_kernel,
        grid=(batch // seqs,),
        in_specs=[tok, _full((1, D_MODEL)), _full((D_MODEL, D_MODEL)), _full((D_MODEL, D_MODEL))],
        out_specs=[cache, cache, tok, tok],
        out_shape=(f, f, h, h),
        compiler_params=_params("parallel"),
        name="memory_kv",
    )(mem, g_mem, w_mk, w_mv)


ATTN_ROWS = 512


def _outattn_kernel(x_ref, ssm_ref, ret_ref, wout_ref, g_ref, wq_ref, k_ref, v_ref, x1_ref, o_ref):
    cols = [slice(h * MEM_HEAD_DIM, (h + 1) * MEM_HEAD_DIM) for h in range(MEM_HEADS)]
    groups = [slice(r, r + ATTN_ROWS) for r in range(0, x_ref.shape[0], ATTN_ROWS)]

    def scores_of(rows):
        x1 = _mixer_out(x_ref, ssm_ref, ret_ref, wout_ref, rows)
        x1_ref[rows, :] = x1
        q = _query(x1, g_ref, wq_ref)
        return [_dot_nt(q[:, c], k_ref[0, :, c]) for c in cols]

    def attend(rows, scores):
        outs = []
        for s, c in zip(scores, cols):
            e = jnp.exp(s - jnp.max(s, axis=-1, keepdims=True))
            p = (e / jnp.sum(e, axis=-1, keepdims=True)).astype(BF16)
            outs.append(_dot(p, v_ref[0, :, c]).astype(BF16))
        o_ref[rows, :] = jnp.concatenate(outs, axis=1)

    pending = None
    for rows in groups:
        scores = scores_of(rows)
        if pending is not None:
            attend(*pending)
        pending = (rows, scores)
    attend(*pending)


def _out_and_attention(x, ssm, ret, w_out, g_xattn, w_mq, mk, mv, *, length):
    t = x.shape[0]
    tm = min(length, ROWS_LARGE)
    per_seq = length // tm
    row = lambda w: pl.BlockSpec((tm, w), lambda i: (i, 0))
    mem = pl.BlockSpec((1, MEM_LEN, D_MODEL), lambda i: (i // per_seq, 0, 0))
    return pl.pallas_call(
        _outattn_kernel,
        grid=(t // tm,),
        in_specs=[row(D_MODEL), row(SSM_WIDTH), row(RET_WIDTH), _full((D_MODEL, D_MODEL)),
                  _full((1, D_MODEL)), _full((D_MODEL, D_MODEL)), mem, mem],
        out_specs=[row(D_MODEL), row(D_MODEL)],
        out_shape=(jax.ShapeDtypeStruct((t, D_MODEL), F32), jax.ShapeDtypeStruct((t, D_MODEL), BF16)),
        compiler_params=_params("parallel"),
        name="out_proj_attention",
    )(x, ssm, ret, w_out, g_xattn, w_mq, mk, mv)


def _cached_attention(q_ref, k_ref, v_ref, o_ref, *, bb, length):
    hit = (lax.broadcasted_iota(jnp.int32, (CACHE_SUB, CACHE_ROWS), 0)
           == lax.broadcasted_iota(jnp.int32, (CACHE_SUB, CACHE_ROWS), 1) % CACHE_SUB)
    first_half = lax.broadcasted_iota(jnp.int32, (length, LANES), 1) % CACHE_SUB < MEM_HEADS
    tiles = CACHE_ROWS // LANES

    def class_reduce(x, op):
        shift = CACHE_SUB
        while shift < LANES:
            x = op(x, pltpu.roll(x, shift, axis=1))
            shift *= 2
        return x

    def softmax_rows(r):
        z = jnp.concatenate(
            [jnp.sum(jnp.where(hit, r[CACHE_SUB * t:CACHE_SUB * (t + 1)], 0.0), axis=0, keepdims=True)
             for t in range(length)], axis=0)
        parts = []
        for i in range(tiles):
            zi = z[:, LANES * i:LANES * (i + 1)]
            parts.append(zi + pltpu.roll(zi, LANES - MEM_HEADS, axis=1))
        mx = parts[0]
        for pi in parts[1:]:
            mx = jnp.maximum(mx, pi)
        mx = class_reduce(mx, jnp.maximum)
        es = [jnp.exp(pi - mx) for pi in parts]
        tot = es[0]
        for ei in es[1:]:
            tot = tot + ei
        tot = class_reduce(tot, jnp.add)
        ps = []
        for ei in es:
            pi = ei / tot
            ps.append(jnp.where(first_half, pi, pltpu.roll(pi, MEM_HEADS, axis=1)))
        p = jnp.concatenate(ps, axis=1)
        return jnp.concatenate(
            [jnp.where(hit, jnp.broadcast_to(p[t:t + 1], (CACHE_SUB, CACHE_ROWS)), 0.0)
             for t in range(length)], axis=0).astype(BF16)

    def start():
        return [_dot_nt(q_ref[b], k_ref[b].astype(BF16)) for b in range(bb)]

    def finish(scores):
        probs = [softmax_rows(r) for r in scores]
        for b in range(bb):
            o_ref[b] = _dot(probs[b], v_ref[b].astype(BF16)).astype(BF16)

    return start, finish


def _to_cache_rows(a, lead):
    halves = MEM_HEAD_DIM // LANES
    batch = a.size // (lead * D_MODEL)
    a = a.reshape(batch, lead, MEM_HEADS, halves, LANES).transpose(0, 1, 3, 2, 4)
    return a.reshape(batch, lead * CACHE_SUB, LANES)


def _from_cache_rows(a, lead):
    halves = MEM_HEAD_DIM // LANES
    batch = a.shape[0]
    a = a.reshape(batch, lead, halves, MEM_HEADS, LANES).transpose(0, 1, 3, 2, 4)
    return a.reshape(batch, lead, D_MODEL)


MLP_CHUNK = 1024


def _post_kernel(x1_ref, o_ref, wo_ref, gm_ref, wup_ref, wdn_ref, gf_ref, *rest, side_bb, side_len):
    if side_bb:
        qs_ref, ck_ref, cv_ref, y_ref, os_ref = rest
    else:
        (y_ref,) = rest
    x2 = x1_ref[...] + _dot(o_ref[...], wo_ref[...])
    if side_bb:
        side_start, side_finish = _cached_attention(qs_ref, ck_ref, cv_ref, os_ref, bb=side_bb, length=side_len)
        side_scores = side_start()
    h = _rmsnorm(x2, gm_ref[...]).astype(BF16)
    chunks = [slice(c, c + MLP_CHUNK) for c in range(0, D_FF, MLP_CHUNK)]
    acc = x2
    up = _dot(h, wup_ref[:, chunks[0]])
    for c, cols in enumerate(chunks):
        nxt = _dot(h, wup_ref[:, chunks[c + 1]]) if c + 1 < len(chunks) else None
        if side_bb and c == len(chunks) - 1:
            side_finish(side_scores)
        a = jnp.maximum(up, 0.0)
        acc = acc + _dot((a * a).astype(BF16), wdn_ref[cols, :])
        up = nxt
    y_ref[...] = _rmsnorm(acc, gf_ref[...])


def _attn_out_mlp(x1, o, w_mo, g_mlp, w_up, w_down, g_final, side=None):
    t = x1.shape[0]
    tm = min(t, ROWS_SMALL)
    steps = t // tm
    row = pl.BlockSpec((tm, D_MODEL), lambda i: (i, 0))
    in_specs = [row, row, _full((D_MODEL, D_MODEL)), _full((1, D_MODEL)),
                _full((D_MODEL, D_FF)), _full((D_FF, D_MODEL)), _full((1, D_MODEL))]
    out_specs = [row]
    out_shape = [jax.ShapeDtypeStruct((t, D_MODEL), F32)]
    args = [x1, o, w_mo, g_mlp, w_up, w_down, g_final]
    side_bb = side_len = 0
    if side is not None:
        q_rows = side[0]
        side_bb, rem = divmod(q_rows.shape[0], steps)
        if rem or not side_bb:
            raise ValueError("side attention sequences must spread evenly over the grid steps")
        side_len = q_rows.shape[1] // CACHE_SUB
        blk = lambda r: pl.BlockSpec((side_bb, r, LANES), lambda i: (i, 0, 0))
        in_specs += [blk(q_rows.shape[1]), blk(CACHE_ROWS), blk(CACHE_ROWS)]
        out_specs.append(blk(q_rows.shape[1]))
        out_shape.append(jax.ShapeDtypeStruct(q_rows.shape, BF16))
        args += list(side)
    outs = pl.pallas_call(
        functools.partial(_post_kernel, side_bb=side_bb, side_len=side_len),
        grid=(steps,),
        in_specs=in_specs,
        out_specs=out_specs,
        out_shape=out_shape,
        compiler_params=_params("parallel"),
        name="attn_out_mlp",
    )(*args)
    return outs if side is not None else outs[0]


def _mixers(proj, batch, length, s5_re, s5_im, ret_s, p):
    tokens = batch * length
    u, packed, g = proj
    ssm, s5_re_new, s5_im_new = _s5_mixer(
        u.reshape(batch, length, SSM_WIDTH), s5_re.reshape(batch, SSM_LANES), s5_im.reshape(batch, SSM_LANES),
        p["s5_maps"], p["d_skip"], p["w_glu"], batch=batch, length=length)
    tok3 = lambda a: a.reshape(batch, length, RET_WIDTH)
    ret, ret_new = _retention(packed.reshape(batch, length, 4 * RET_WIDTH), tok3(g), ret_s, p["ret_gn"])
    states = (s5_re_new.reshape(1, batch, SSM_GROUPS, SSM_STATE),
              s5_im_new.reshape(1, batch, SSM_GROUPS, SSM_STATE), ret_new[None])
    return ssm.reshape(tokens, SSM_WIDTH), ret.reshape(tokens, RET_WIDTH), states


def kernel(x_prompt, x_sample, mem_prompt, state_s5_re, state_s5_im, state_ret, cache_mem_k, cache_mem_v, g_mix, w_in, lam_re, lam_im, log_dt, b_re, b_im, c_re, c_im, d_skip, w_glu, ret_gn, w_out, g_xattn, g_mem, w_mq, w_mk, w_mv, w_mo, g_mlp, w_up, w_down, g_final):
    if g_mix.shape[0] != 1:
        raise ValueError("this kernel implements the single-layer configuration")
    bp, lp, _ = x_prompt.shape
    bs, ls, _ = x_sample.shape
    vec = lambda a: a.reshape(1, -1).astype(F32)
    xp = x_prompt.reshape(bp * lp, D_MODEL)
    xs = x_sample.reshape(bs * ls, D_MODEL)
    proj_p, (w_in_b, w_up_b, w_down_b) = _project(
        xp, vec(g_mix[0]), w_in[0], length=lp, pos0=0.0, casts=(w_up[0], w_down[0]))
    w_out_b, w_mq_b, w_mo_b, w_mk_b, w_mv_b, w_glu_b = (
        w[0].astype(BF16) for w in (w_out, w_mq, w_mo, w_mk, w_mv, w_glu))
    proj_s, _ = _project(xs, vec(g_mix[0]), w_in_b, length=ls, pos0=float(PAST_LEN))
    p = dict(
        s5_maps=_s5_block_maps(lam_re[0], lam_im[0], log_dt[0], b_re[0], b_im[0], c_re[0], c_im[0]),
        d_skip=vec(d_skip[0]), w_glu=w_glu_b, ret_gn=vec(ret_gn[0]))
    g_x = vec(g_xattn[0])
    mlp = (w_mo_b, vec(g_mlp[0]), w_up_b, w_down_b, vec(g_final))

    ssm_s, ret_s, states_s = _mixers(proj_s, bs, ls, state_s5_re[0], state_s5_im[0], state_ret[0], p)
    x1_s, q_s = _out_and_query(xs, ssm_s, ret_s, w_out_b, g_x, w_mq_b)
    if (ls * CACHE_SUB) % BF16_ROWS:
        raise ValueError("cached attention needs whole bf16 tiles of query rows")
    side = (_to_cache_rows(q_s, ls), _to_cache_rows(cache_mem_k, MEM_LEN), _to_cache_rows(cache_mem_v, MEM_LEN))

    mk_rows, mv_rows, mk_p, mv_p = _memory_kv(mem_prompt, vec(g_mem[0]), w_mk_b, w_mv_b)
    zs = jnp.zeros((bp, SSM_GROUPS, SSM_STATE), F32)
    zr = jnp.zeros((bp, RET_HEADS, RET_HEAD_DIM, RET_HEAD_DIM), F32)
    ssm_p, ret_p, states_p = _mixers(proj_p, bp, lp, zs, zs, zr, p)
    x1_p, o_p = _out_and_attention(xp, ssm_p, ret_p, w_out_b, g_x, w_mq_b, mk_p, mv_p, length=lp)
    y_p, o_s = _attn_out_mlp(x1_p, o_p, *mlp, side=side)
    y_s = _attn_out_mlp(x1_s, _from_cache_rows(o_s, ls).reshape(bs * ls, D_MODEL), *mlp)

    kv5 = lambda a: _from_cache_rows(a, MEM_LEN).reshape(1, bp, MEM_LEN, MEM_HEADS, MEM_HEAD_DIM)
    return (y_p.reshape(bp, lp, D_MODEL), y_s.reshape(bs, ls, D_MODEL), *states_p, kv5(mk_rows), kv5(mv_rows),
            *states_s)
```

```python
import functools
import math

import numpy as np
import jax
import jax.numpy as jnp
from jax import lax
from jax.experimental import pallas as pl
from jax.experimental.pallas import tpu as pltpu

F32 = jnp.float32
BF16 = jnp.bfloat16

D_MODEL = 1024
SSM_WIDTH = 512
SSM_GROUP = 16
SSM_GROUPS = 32
SSM_STATE = 64
SSM_LANES = SSM_GROUPS * SSM_STATE
RET_WIDTH = 512
RET_HEADS = 4
RET_HEAD_DIM = 128
RET_CHUNK = 128
ROPE_BASE = 10000.0
MEM_LEN = 256
MEM_HEADS = 4
MEM_HEAD_DIM = 256
MEM_SCALE = MEM_HEAD_DIM ** -0.5
D_FF = 4096
PROJ_WIDTH = SSM_WIDTH + 4 * RET_WIDTH
EPS = 1e-6
PAST_LEN = 16384

LANES = 128
MXU_TILE = 256
BF16_ROWS = 16
VMEM_LIMIT = 56 * 1024 * 1024

ROWS_LARGE = 1024
ROWS_SMALL = 512
S5_STEPS = 256

CACHE_ROWS = MEM_LEN * MEM_HEADS * MEM_HEAD_DIM // LANES
CACHE_SUB = MEM_HEADS * MEM_HEAD_DIM // LANES

assert math.frexp(MEM_SCALE)[0] == 0.5


def _dot(a, b):
    return jnp.dot(a, b, preferred_element_type=F32)


def _dot_nt(a, b):
    return lax.dot_general(a, b, (((1,), (1,)), ((), ())), preferred_element_type=F32)


def _dot_tn(a, b):
    return lax.dot_general(a, b, (((0,), (0,)), ((), ())), preferred_element_type=F32)


def _rmsnorm(x, g):
    return x * lax.rsqrt(jnp.mean(x * x, axis=-1, keepdims=True) + EPS) * g


def _sigmoid(x):
    return 1.0 / (1.0 + jnp.exp(-x))


def _gelu_tanh(x):
    c = math.sqrt(2.0 / math.pi)
    return x * (0.5 * jnp.tanh(x * (c + (c * 0.044715) * (x * x))) + 0.5)


def _params(*sem):
    return pltpu.CompilerParams(dimension_semantics=sem, vmem_limit_bytes=VMEM_LIMIT)


def _full(shape):
    return pl.BlockSpec(shape, lambda *_: (0,) * len(shape))


SSM_BLOCK = 4
SSM_TILE_GROUPS = MXU_TILE // SSM_STATE
SSM_TILES = SSM_GROUPS // SSM_TILE_GROUPS
SSM_TILE_CHANNELS = SSM_TILE_GROUPS * SSM_GROUP
GROUP_SHIFT = SSM_GROUP.bit_length() - 1
STATE_SHIFT = SSM_STATE.bit_length() - 1

assert SSM_BLOCK * SSM_TILE_CHANNELS == MXU_TILE and SSM_TILE_CHANNELS == SSM_STATE


def _s5_prep_kernel(lr_ref, li_ref, ldt_ref, br_ref, bi_ref, cr_ref, ci_ref,
                    ar_ref, ai_ref, winr_ref, wini_ref, woutr_ref, wouti_ref, kdir_ref):
    lr = lr_ref[...]
    li = li_ref[...]
    dt = jnp.exp(ldt_ref[...])
    mag = jnp.exp(lr * dt)
    ab_re = mag * jnp.cos(li * dt)
    ab_im = mag * jnp.sin(li * dt)
    den = lr * lr + li * li
    f_re = ((ab_re - 1.0) * lr + ab_im * li) / den
    f_im = (ab_im * lr - (ab_re - 1.0) * li) / den
    bb_re = f_re * br_ref[...] - f_im * bi_ref[...]
    bb_im = f_re * bi_ref[...] + f_im * br_ref[...]
    cr = cr_ref[...]
    ci = ci_ref[...]
    pows = [(jnp.ones_like(ab_re), jnp.zeros_like(ab_re))]
    for _ in range(SSM_BLOCK):
        pr, pi = pows[-1]
        pows.append((pr * ab_re - pi * ab_im, pr * ab_im + pi * ab_re))
    ar_ref[...], ai_ref[...] = pows[SSM_BLOCK]

    R, T, TG = SSM_BLOCK, SSM_TILES, SSM_TILE_GROUPS
    rows = SSM_GROUPS * SSM_GROUP
    side = SSM_TILE_CHANNELS
    exact = dict(precision=lax.Precision.HIGHEST, preferred_element_type=F32)
    iota = lambda shape, d: lax.broadcasted_iota(jnp.int32, shape, d)
    rep4 = lambda a, axis: jnp.concatenate([a] * TG, axis=axis)
    tiles = lambda a: a.reshape(T, side, a.shape[-1])
    group_of = lambda i: (i & (side - 1)) >> GROUP_SHIFT

    own_in = group_of(iota((rows, MXU_TILE), 0)) == iota((rows, MXU_TILE), 1) >> STATE_SHIFT
    for s in range(R):
        pr, pi = pows[R - 1 - s]
        for ref, w in ((winr_ref, pr * bb_re - pi * bb_im), (wini_ref, pr * bb_im + pi * bb_re)):
            ref[:, side * s:side * (s + 1), :] = tiles(jnp.where(own_in, rep4(w, 1), 0.0)).astype(BF16)

    own_out = iota((MXU_TILE, rows), 0) >> STATE_SHIFT == group_of(iota((MXU_TILE, rows), 1))
    for ref, real in ((woutr_ref, True), (wouti_ref, False)):
        per_lag = []
        for j in range(R):
            pr, pi = pows[j + 1]
            w = cr * pr - ci * pi if real else -(cr * pi + ci * pr)
            wt = w.T
            per_lag.append(jnp.where(own_out, rep4(wt, 0), 0.0))
        for n in range(T):
            ref[n] = jnp.concatenate([m[:, side * n:side * (n + 1)] for m in per_lag], axis=1).astype(BF16)

    grouped = lambda a: a.reshape(SSM_GROUPS, SSM_GROUP, SSM_STATE)
    contract = lambda a, b: lax.dot_general(a, b, (((2,), (2,)), ((0,), (0,))), **exact)
    own_dir = group_of(iota((rows, side), 0)) == iota((rows, side), 1) >> GROUP_SHIFT
    by_lag = []
    for d in range(R):
        pr, pi = pows[d]
        er = grouped(pr * bb_re - pi * bb_im)
        ei = grouped(pr * bb_im + pi * bb_re)
        kt = (contract(er, grouped(cr)) - contract(ei, grouped(ci))).reshape(rows, SSM_GROUP)
        by_lag.append(tiles(jnp.where(own_dir, rep4(kt, 1), 0.0)))
    nothing = jnp.zeros_like(by_lag[0])
    for s in range(R):
        kdir_ref[:, side * s:side * (s + 1), :] = jnp.concatenate(
            [by_lag[j - s] if j >= s else nothing for j in range(R)], axis=2).astype(BF16)


def _s5_block_maps(lam_re, lam_im, log_dt, b_re, b_im, c_re, c_im):
    rows = SSM_GROUPS * SSM_GROUP
    rep = lambda a: jnp.repeat(a, SSM_GROUP, axis=0)
    ldt = jnp.broadcast_to(log_dt[:, None], (SSM_GROUPS, SSM_STATE))
    bt = lambda b: b.transpose(0, 2, 1).reshape(rows, SSM_STATE)
    ct = lambda c: c.reshape(rows, SSM_STATE)
    one = jax.ShapeDtypeStruct((rows, SSM_STATE), F32)
    tile = jax.ShapeDtypeStruct((SSM_TILES, MXU_TILE, MXU_TILE), BF16)
    ar, ai, winr, wini, woutr, wouti, direct = pl.pallas_call(
        _s5_prep_kernel,
        out_shape=(one, one, tile, tile, tile, tile, tile),
        compiler_params=pltpu.CompilerParams(vmem_limit_bytes=VMEM_LIMIT),
        name="s5_block_maps",
    )(rep(lam_re), rep(lam_im), rep(ldt), bt(b_re), bt(b_im), ct(c_re), ct(c_im))
    a_re = ar[::SSM_GROUP].reshape(1, SSM_LANES)
    a_im = ai[::SSM_GROUP].reshape(1, SSM_LANES)
    return a_re, a_im, winr, wini, woutr, wouti, direct


def _retention_chunk(length):
    return RET_CHUNK if length % RET_CHUNK == 0 else length


def _log_gamma():
    return np.log(1.0 - 2.0 ** (-5.0 - np.arange(RET_HEADS, dtype=np.float64)))


def _rope_tables(length, pos0):
    half = RET_HEAD_DIM // 2
    inv = ROPE_BASE ** (-np.arange(half, dtype=np.float64) / half)
    ang = (pos0 + np.arange(length, dtype=np.float64))[:, None] * inv[None, :]
    return (np.concatenate([np.cos(ang), np.cos(ang)], axis=1),
            np.concatenate([-np.sin(ang), np.sin(ang)], axis=1))


def _zeta_rows(length):
    chunk = _retention_chunk(length)
    idx = np.arange(chunk, dtype=np.float64)
    zeta = np.exp((chunk - 1.0 - idx)[None, :] * _log_gamma()[:, None])
    return np.concatenate([np.broadcast_to(z[:, None], (chunk, RET_HEAD_DIM)) for z in zeta], axis=1)


def _decay_tables(length, padded):
    chunk = _retention_chunk(length)
    cpad = chunk if padded == length else padded
    lg = _log_gamma()
    idx = np.arange(chunk, dtype=np.float64)
    diff = idx[:, None] - idx[None, :]
    mask = np.where(diff[None] >= 0, np.exp(np.maximum(diff, 0.0)[None] * lg[:, None, None]), 0.0)
    xi = np.exp((idx + 1.0)[None, :] * lg[:, None])
    gamma_c = tuple(float(v) for v in np.exp(chunk * lg))
    mask = np.pad(mask, ((0, 0), (0, cpad - chunk), (0, cpad - chunk)))
    xi = np.broadcast_to(np.pad(xi, ((0, 0), (0, cpad - chunk)))[:, :, None], (RET_HEADS, cpad, RET_HEAD_DIM))
    f = lambda a: jnp.asarray(np.ascontiguousarray(a), dtype=F32)
    return cpad, gamma_c, f(mask), f(xi)


def _ret_lanes(part, head):
    lo = part * RET_WIDTH + head * RET_HEAD_DIM
    return slice(lo, lo + RET_HEAD_DIM)


def _proj_kernel(x_ref, g_ref, w_ref, cc_ref, ss_ref, zeta_ref, *rest, n_cast):
    casts_in, rest = rest[:n_cast], rest[n_cast:]
    u_ref, ret_ref, gate_ref = rest[:3]
    w = w_ref[...]
    if n_cast:
        w = w.astype(BF16)
        for src, dst in zip(casts_in, rest[3:]):
            dst[...] = src[...].astype(BF16)
    h = _rmsnorm(x_ref[...], g_ref[...]).astype(BF16)
    proj = _dot(h, w)
    u_ref[...] = proj[:, :SSM_WIDTH]
    ret_ref[:, 3 * RET_WIDTH:] = proj[:, SSM_WIDTH + 2 * RET_WIDTH:SSM_WIDTH + 3 * RET_WIDTH].astype(BF16)
    gate_ref[...] = proj[:, SSM_WIDTH + 3 * RET_WIDTH:]
    cc = cc_ref[...]
    ss = ss_ref[...]
    rope = lambda a: a * cc + pltpu.roll(a, RET_HEAD_DIM // 2, axis=1) * ss
    for hd in range(RET_HEADS):
        cols = slice(hd * RET_HEAD_DIM, (hd + 1) * RET_HEAD_DIM)
        q = rope(proj[:, SSM_WIDTH + hd * RET_HEAD_DIM:SSM_WIDTH + (hd + 1) * RET_HEAD_DIM])
        k = rope(proj[:, SSM_WIDTH + RET_WIDTH + hd * RET_HEAD_DIM:SSM_WIDTH + RET_WIDTH + (hd + 1) * RET_HEAD_DIM])
        ret_ref[:, _ret_lanes(0, hd)] = (q * (RET_HEAD_DIM ** -0.5)).astype(BF16)
        ret_ref[:, _ret_lanes(1, hd)] = k.astype(BF16)
        ret_ref[:, _ret_lanes(2, hd)] = (k * zeta_ref[:, cols]).astype(BF16)


def _project(x, g_mix, w_in, *, length, pos0, casts=()):
    t = x.shape[0]
    tm = min(t, ROWS_LARGE)
    steps = t // tm
    chunk = _retention_chunk(length)
    casts = (w_in, *casts) if casts else ()
    if tm % chunk or (length % tm and tm % length):
        raise ValueError("token tiles must hold whole retention chunks of whole or repeated sequences")
    cc, ss = _rope_tables(length, pos0)
    if length < tm:
        cc, ss = np.tile(cc, (tm // length, 1)), np.tile(ss, (tm // length, 1))
    tab_blocks = cc.shape[0] // tm
    zeta = np.tile(_zeta_rows(length), (tm // chunk, 1))
    const = lambda a: jnp.asarray(np.ascontiguousarray(a), dtype=F32)
    row = lambda w: pl.BlockSpec((tm, w), lambda i: (i, 0))
    tab = pl.BlockSpec((tm, RET_HEAD_DIM), lambda i: (i % tab_blocks, 0))
    f = jax.ShapeDtypeStruct((t, SSM_WIDTH), F32)
    h = jax.ShapeDtypeStruct((t, 4 * RET_WIDTH), BF16)
    in_specs = [row(D_MODEL), _full((1, D_MODEL)), _full((D_MODEL, PROJ_WIDTH)), tab, tab, _full((tm, RET_WIDTH))]
    out_specs = [row(SSM_WIDTH), row(4 * RET_WIDTH), row(RET_WIDTH)]
    out_shape = [f, h, f]
    if casts:
        for c in casts:
            rows, rem = divmod(c.shape[0], steps)
            if rem or rows % BF16_ROWS:
                raise ValueError("side-cast weights must split into whole bf16 tiles per grid step")
            blk = pl.BlockSpec((rows, c.shape[1]), lambda i: (i, 0))
            in_specs.append(blk)
            out_specs.append(blk)
            out_shape.append(jax.ShapeDtypeStruct(c.shape, BF16))
    outs = pl.pallas_call(
        functools.partial(_proj_kernel, n_cast=len(casts)),
        grid=(steps,),
        in_specs=in_specs,
        out_specs=out_specs,
        out_shape=out_shape,
        compiler_params=_params("arbitrary" if casts else "parallel"),
        name="in_proj",
    )(x, g_mix, w_in, const(cc), const(ss), const(zeta), *casts)
    return outs[:3], outs[3:]


def _s5_block_kernel(u_ref, h0r_ref, h0i_ref, ar_ref, ai_ref, winr_ref, wini_ref, woutr_ref, wouti_ref,
                     kdir_ref, d_ref, wglu_ref, out_ref, hr_ref, hi_ref, vr_scr, vi_scr, xr_scr, xi_scr,
                     *, batch, steps, pair, width):
    R, T = SSM_BLOCK, SSM_TILES
    chan = SSM_TILE_CHANNELS
    blocks = steps // R
    rows, brow = steps * batch, blocks * batch

    @pl.when(pl.program_id(0) == 0)
    def _():
        hr_ref[...] = h0r_ref[...]
        hi_ref[...] = h0i_ref[...]

    u = jnp.swapaxes(u_ref[...], 0, 1).reshape(rows, SSM_WIDTH)
    u4 = u.reshape(blocks, R, batch, SSM_WIDTH)
    lag = [u4[:, s].reshape(brow, SSM_WIDTH) for s in range(R)]
    direct = []
    for n in range(T):
        un = jnp.concatenate([lag[s][:, chan * n:chan * (n + 1)] for s in range(R)], axis=1).astype(BF16)
        lanes = slice(MXU_TILE * n, MXU_TILE * (n + 1))
        vr_scr[:, lanes] = _dot(un, winr_ref[n])
        vi_scr[:, lanes] = _dot(un, wini_ref[n])
        direct.append(_dot(un, kdir_ref[n]))

    for j in range(SSM_LANES // width):
        glob = slice(j * width, (j + 1) * width)
        ar = jnp.broadcast_to(ar_ref[:, glob], (batch, width))
        ai = jnp.broadcast_to(ai_ref[:, glob], (batch, width))
        xr = hr_ref[:, glob]
        xi = hi_ref[:, glob]
        for i in range(blocks // pair):
            prev_r, prev_i = [], []
            for s in range(pair):
                at = slice((i * pair + s) * batch, (i * pair + s + 1) * batch)
                prev_r.append(xr)
                prev_i.append(xi)
                xr, xi = ar * xr - ai * xi + vr_scr[at, glob], ar * xi + ai * xr + vi_scr[at, glob]
            blk = slice(i * pair * batch, (i + 1) * pair * batch)
            xr_scr[blk, glob] = jnp.concatenate(prev_r, axis=0).astype(BF16)
            xi_scr[blk, glob] = jnp.concatenate(prev_i, axis=0).astype(BF16)
        hr_ref[:, glob] = xr
        hi_ref[:, glob] = xi

    y4 = []
    for n in range(T):
        lanes = slice(MXU_TILE * n, MXU_TILE * (n + 1))
        y4.append(_dot(xr_scr[:, lanes], woutr_ref[n]) + _dot(xi_scr[:, lanes], wouti_ref[n]) + direct[n])
    per_lag = [jnp.concatenate([y4[n][:, chan * j:chan * (j + 1)] for n in range(T)], axis=1)
               .reshape(blocks, batch, SSM_WIDTH) for j in range(R)]
    y = jnp.stack(per_lag, axis=1).reshape(rows, SSM_WIDTH) + d_ref[...] * u
    z = _gelu_tanh(y)
    out = z * _sigmoid(_dot(z.astype(BF16), wglu_ref[...]))
    out_ref[...] = jnp.swapaxes(out.reshape(steps, batch, SSM_WIDTH), 0, 1).astype(BF16)


def _s5_mixer(u, h0_re, h0_im, maps, d_skip, w_glu, *, batch, length):
    a_re, a_im, winr, wini, woutr, wouti, direct = maps
    steps = min(length, S5_STEPS)
    if steps % SSM_BLOCK or length % steps:
        raise ValueError("sequence length must be a multiple of the S5 block")
    blocks = steps // SSM_BLOCK
    pair = max(1, min(blocks, BF16_ROWS // batch))
    if (pair * batch) % BF16_ROWS or blocks % pair:
        raise ValueError("S5 state stores need whole bf16 tiles")
    width = SSM_LANES // 2 if batch <= 8 else LANES
    brow = blocks * batch
    kern = functools.partial(_s5_block_kernel, batch=batch, steps=steps, pair=pair, width=width)
    st = jax.ShapeDtypeStruct((batch, SSM_LANES), F32)
    tile = _full((SSM_TILES, MXU_TILE, MXU_TILE))
    return pl.pallas_call(
        kern,
        grid=(length // steps,),
        in_specs=[pl.BlockSpec((batch, steps, SSM_WIDTH), lambda i: (0, i, 0)),
                  _full((batch, SSM_LANES)), _full((batch, SSM_LANES)),
                  _full((1, SSM_LANES)), _full((1, SSM_LANES)),
                  tile, tile, tile, tile, tile,
                  _full((1, SSM_WIDTH)), _full((SSM_WIDTH, SSM_WIDTH))],
        out_specs=[pl.BlockSpec((batch, steps, SSM_WIDTH), lambda i: (0, i, 0)),
                   _full((batch, SSM_LANES)), _full((batch, SSM_LANES))],
        out_shape=(jax.ShapeDtypeStruct((batch, length, SSM_WIDTH), BF16), st, st),
        scratch_shapes=[pltpu.VMEM((brow, SSM_LANES), F32), pltpu.VMEM((brow, SSM_LANES), F32),
                        pltpu.VMEM((brow, SSM_LANES), BF16), pltpu.VMEM((brow, SSM_LANES), BF16)],
        compiler_params=_params("arbitrary"),
        name="s5_mixer",
    )(u, h0_re, h0_im, a_re, a_im, winr, wini, woutr, wouti, direct, d_skip, w_glu)


def _ret_kernel(p_ref, g_ref, mask_ref, xi_ref, gain_ref, s0_ref, o_ref, s_ref,
                *, bb, gamma_c, unroll, chunk, chunks):
    @pl.when(pl.program_id(1) == 0)
    def _():
        s_ref[...] = s0_ref[...]

    heads = range(RET_HEADS)
    cols = [slice(h * RET_HEAD_DIM, (h + 1) * RET_HEAD_DIM) for h in heads]

    def one_chunk(b, rows):
        qb = [p_ref[b, rows, _ret_lanes(0, h)] for h in heads]
        vb = [p_ref[b, rows, _ret_lanes(3, h)] for h in heads]
        s_prev = [s_ref[b, h] for h in heads]
        scores = [(_dot_nt(qb[h], p_ref[b, rows, _ret_lanes(1, h)]) * mask_ref[h]).astype(BF16) for h in heads]
        cross = [_dot(qb[h], s_prev[h].astype(BF16)) * xi_ref[h] for h in heads]
        kv = [_dot_tn(p_ref[b, rows, _ret_lanes(2, h)], vb[h]) for h in heads]
        inner = [_dot(scores[h], vb[h]) for h in heads]
        outs = []
        for h in heads:
            s_ref[b, h] = s_prev[h] * gamma_c[h] + kv[h]
            o = inner[h] + cross[h]
            mu = jnp.mean(o, axis=-1, keepdims=True)
            oc = o - mu
            var = jnp.mean(oc * oc, axis=-1, keepdims=True)
            on = oc * lax.rsqrt(var + EPS) * gain_ref[:, cols[h]]
            gate = g_ref[b, rows, cols[h]]
            outs.append((gate * _sigmoid(gate) * on).astype(BF16))
        o_ref[b, rows, :] = jnp.concatenate(outs, axis=1)

    def per_batch(b, carry):
        for c in range(chunks):
            one_chunk(b, slice(c * chunk, (c + 1) * chunk))
        return carry

    if unroll == bb:
        for b in range(bb):
            per_batch(b, 0)
    else:
        lax.fori_loop(0, bb, per_batch, 0, unroll=unroll)


def _retention(packed, g, s0, gn_gain):
    batch, length, _ = g.shape
    padded = pl.cdiv(length, BF16_ROWS) * BF16_ROWS
    chunk, gamma_c, mask, xi = _decay_tables(length, padded)
    if padded != length:
        pad = lambda a: jnp.pad(a, ((0, 0), (0, padded - length), (0, 0)))
        packed, g = pad(packed), pad(g)
    bb, unroll = (8, 2) if chunk > BF16_ROWS else (16, 16)
    chunks = 2 if (padded // chunk) % 2 == 0 else 1
    tok = pl.BlockSpec((bb, chunks * chunk, RET_WIDTH), lambda i, c: (i, c, 0))
    state = pl.BlockSpec((bb, RET_HEADS, RET_HEAD_DIM, RET_HEAD_DIM), lambda i, c: (i, 0, 0, 0))
    out, s_new = pl.pallas_call(
        functools.partial(_ret_kernel, bb=bb, gamma_c=gamma_c, unroll=unroll, chunk=chunk, chunks=chunks),
        grid=(batch // bb, padded // (chunks * chunk)),
        in_specs=[pl.BlockSpec((bb, chunks * chunk, 4 * RET_WIDTH), lambda i, c: (i, c, 0)), tok,
                  _full((RET_HEADS, chunk, chunk)), _full((RET_HEADS, chunk, RET_HEAD_DIM)),
                  _full((1, RET_WIDTH)), state],
        out_specs=[tok, state],
        out_shape=(jax.ShapeDtypeStruct((batch, padded, RET_WIDTH), BF16),
                   jax.ShapeDtypeStruct(s0.shape, F32)),
        compiler_params=_params("parallel", "arbitrary"),
        name="retention",
    )(packed, g, mask, xi, gn_gain, s0)
    return out[:, :length], s_new


def _mixer_out(x_ref, ssm_ref, ret_ref, wout_ref, rows=slice(None)):
    return (x_ref[rows, :] + _dot(ssm_ref[rows, :], wout_ref[:SSM_WIDTH, :])
            + _dot(ret_ref[rows, :], wout_ref[SSM_WIDTH:, :]))


def _query(x1, g_ref, wq_ref):
    return (_dot(_rmsnorm(x1, g_ref[...]).astype(BF16), wq_ref[...]) * MEM_SCALE).astype(BF16)


def _outq_kernel(x_ref, ssm_ref, ret_ref, wout_ref, g_ref, wq_ref, x1_ref, q_ref):
    x1 = _mixer_out(x_ref, ssm_ref, ret_ref, wout_ref)
    x1_ref[...] = x1
    q_ref[...] = _query(x1, g_ref, wq_ref)


def _out_and_query(x, ssm, ret, w_out, g_xattn, w_mq):
    t = x.shape[0]
    tm = min(t, ROWS_SMALL)
    row = lambda w: pl.BlockSpec((tm, w), lambda i: (i, 0))
    return pl.pallas_call(
        _outq_kernel,
        grid=(t // tm,),
        in_specs=[row(D_MODEL), row(SSM_WIDTH), row(RET_WIDTH), _full((D_MODEL, D_MODEL)),
                  _full((1, D_MODEL)), _full((D_MODEL, D_MODEL))],
        out_specs=[row(D_MODEL), row(D_MODEL)],
        out_shape=(jax.ShapeDtypeStruct((t, D_MODEL), F32), jax.ShapeDtypeStruct((t, D_MODEL), BF16)),
        compiler_params=_params("parallel"),
        name="out_proj_query",
    )(x, ssm, ret, w_out, g_xattn, w_mq)


def _cache_row_order(x):
    halves = MEM_HEAD_DIM // LANES
    tiles = [x[:, (h * halves + d) * LANES:(h * halves + d + 1) * LANES]
             for d in range(halves) for h in range(MEM_HEADS)]
    return jnp.swapaxes(jnp.stack(tiles, axis=0), 0, 1).reshape(x.shape[0] * CACHE_SUB, LANES)


MEMKV_SEQS = 2


def _memkv_kernel(m_ref, g_ref, wk_ref, wv_ref, k_ref, v_ref, kb_ref, vb_ref):
    seqs = m_ref.shape[0]
    m = _rmsnorm(m_ref[...].reshape(seqs * MEM_LEN, D_MODEL), g_ref[...]).astype(BF16)
    k = _dot(m, wk_ref[...])
    v = _dot(m, wv_ref[...])
    k_ref[...] = _cache_row_order(k).reshape(k_ref.shape)
    v_ref[...] = _cache_row_order(v).reshape(v_ref.shape)
    kb_ref[...] = k.astype(BF16).reshape(kb_ref.shape)
    vb_ref[...] = v.astype(BF16).reshape(vb_ref.shape)


def _memory_kv(mem, g_mem, w_mk, w_mv):
    batch = mem.shape[0]
    seqs = MEMKV_SEQS if batch % MEMKV_SEQS == 0 else 1
    tok = pl.BlockSpec((seqs, MEM_LEN, D_MODEL), lambda i: (i, 0, 0))
    cache = pl.BlockSpec((seqs, CACHE_ROWS, LANES), lambda i: (i, 0, 0))
    f = jax.ShapeDtypeStruct((batch, CACHE_ROWS, LANES), F32)
    h = jax.ShapeDtypeStruct((batch, MEM_LEN, D_MODEL), BF16)
    return pl.pallas_call(
        _memkv_kernel,
        grid=(batch // seqs,),
        in_specs=[tok, _full((1, D_MODEL)), _full((D_MODEL, D_MODEL)), _full((D_MODEL, D_MODEL))],
        out_specs=[cache, cache, tok, tok],
        out_shape=(f, f, h, h),
        compiler_params=_params("parallel"),
        name="memory_kv",
    )(mem, g_mem, w_mk, w_mv)


ATTN_ROWS = 512


def _outattn_kernel(x_ref, ssm_ref, ret_ref, wout_ref, g_ref, wq_ref, k_ref, v_ref, wo_ref, x2_ref):
    cols = [slice(h * MEM_HEAD_DIM, (h + 1) * MEM_HEAD_DIM) for h in range(MEM_HEADS)]
    groups = [slice(r, r + ATTN_ROWS) for r in range(0, x_ref.shape[0], ATTN_ROWS)]

    def scores_of(rows):
        x1 = _mixer_out(x_ref, ssm_ref, ret_ref, wout_ref, rows)
        q = _query(x1, g_ref, wq_ref)
        return x1, [_dot_nt(q[:, c], k_ref[0, :, c]) for c in cols]

    def attend(rows, x1, scores):
        outs = []
        for s, c in zip(scores, cols):
            e = jnp.exp(s - jnp.max(s, axis=-1, keepdims=True))
            p = (e / jnp.sum(e, axis=-1, keepdims=True)).astype(BF16)
            outs.append(_dot(p, v_ref[0, :, c]).astype(BF16))
        x2_ref[rows, :] = x1 + _dot(jnp.concatenate(outs, axis=1), wo_ref[...])

    pending = None
    for rows in groups:
        x1, scores = scores_of(rows)
        if pending is not None:
            attend(*pending)
        pending = (rows, x1, scores)
    attend(*pending)


def _out_and_attention(x, ssm, ret, w_out, g_xattn, w_mq, mk, mv, w_mo, *, length):
    t = x.shape[0]
    tm = min(length, ROWS_LARGE)
    per_seq = length // tm
    row = lambda w: pl.BlockSpec((tm, w), lambda i: (i, 0))
    mem = pl.BlockSpec((1, MEM_LEN, D_MODEL), lambda i: (i // per_seq, 0, 0))
    return pl.pallas_call(
        _outattn_kernel,
        grid=(t // tm,),
        in_specs=[row(D_MODEL), row(SSM_WIDTH), row(RET_WIDTH), _full((D_MODEL, D_MODEL)),
                  _full((1, D_MODEL)), _full((D_MODEL, D_MODEL)), mem, mem, _full((D_MODEL, D_MODEL))],
        out_specs=row(D_MODEL),
        out_shape=jax.ShapeDtypeStruct((t, D_MODEL), F32),
        compiler_params=_params("parallel"),
        name="out_proj_attention",
    )(x, ssm, ret, w_out, g_xattn, w_mq, mk, mv, w_mo)


def _cached_attention(q_ref, k_ref, v_ref, o_ref, *, bb, length):
    hit = (lax.broadcasted_iota(jnp.int32, (CACHE_SUB, CACHE_ROWS), 0)
           == lax.broadcasted_iota(jnp.int32, (CACHE_SUB, CACHE_ROWS), 1) % CACHE_SUB)
    first_half = lax.broadcasted_iota(jnp.int32, (length, LANES), 1) % CACHE_SUB < MEM_HEADS
    tiles = CACHE_ROWS // LANES

    def class_reduce(x, op):
        shift = CACHE_SUB
        while shift < LANES:
            x = op(x, pltpu.roll(x, shift, axis=1))
            shift *= 2
        return x

    def softmax_rows(r):
        z = jnp.concatenate(
            [jnp.sum(jnp.where(hit, r[CACHE_SUB * t:CACHE_SUB * (t + 1)], 0.0), axis=0, keepdims=True)
             for t in range(length)], axis=0)
        parts = []
        for i in range(tiles):
            zi = z[:, LANES * i:LANES * (i + 1)]
            parts.append(zi + pltpu.roll(zi, LANES - MEM_HEADS, axis=1))
        mx = parts[0]
        for pi in parts[1:]:
            mx = jnp.maximum(mx, pi)
        mx = class_reduce(mx, jnp.maximum)
        es = [jnp.exp(pi - mx) for pi in parts]
        tot = es[0]
        for ei in es[1:]:
            tot = tot + ei
        tot = class_reduce(tot, jnp.add)
        ps = []
        for ei in es:
            pi = ei / tot
            ps.append(jnp.where(first_half, pi, pltpu.roll(pi, MEM_HEADS, axis=1)))
        p = jnp.concatenate(ps, axis=1)
        return jnp.concatenate(
            [jnp.where(hit, jnp.broadcast_to(p[t:t + 1], (CACHE_SUB, CACHE_ROWS)), 0.0)
             for t in range(length)], axis=0).astype(BF16)

    def start():
        return [_dot_nt(q_ref[b], k_ref[b].astype(BF16)) for b in range(bb)]

    def finish(scores):
        probs = [softmax_rows(r) for r in scores]
        for b in range(bb):
            o_ref[b] = _dot(probs[b], v_ref[b].astype(BF16)).astype(BF16)

    return start, finish


def _to_cache_rows(a, lead):
    halves = MEM_HEAD_DIM // LANES
    batch = a.size // (lead * D_MODEL)
    a = a.reshape(batch, lead, MEM_HEADS, halves, LANES).transpose(0, 1, 3, 2, 4)
    return a.reshape(batch, lead * CACHE_SUB, LANES)


def _from_cache_rows(a, lead):
    halves = MEM_HEAD_DIM // LANES
    batch = a.shape[0]
    a = a.reshape(batch, lead, halves, MEM_HEADS, LANES).transpose(0, 1, 3, 2, 4)
    return a.reshape(batch, lead, D_MODEL)


MLP_CHUNK = 1024


def _post_kernel(x_ref, *rest, attn_out, side_bb, side_len):
    if attn_out:
        o_ref, wo_ref, *rest = rest
    gm_ref, wup_ref, wdn_ref, gf_ref, *rest = rest
    if side_bb:
        qs_ref, ck_ref, cv_ref, y_ref, os_ref = rest
    else:
        (y_ref,) = rest
    x2 = x_ref[...]
    if attn_out:
        x2 = x2 + _dot(o_ref[...], wo_ref[...])
    if side_bb:
        side_start, side_finish = _cached_attention(qs_ref, ck_ref, cv_ref, os_ref, bb=side_bb, length=side_len)
        side_scores = side_start()
    h = _rmsnorm(x2, gm_ref[...]).astype(BF16)
    chunks = [slice(c, c + MLP_CHUNK) for c in range(0, D_FF, MLP_CHUNK)]
    acc = x2
    up = _dot(h, wup_ref[:, chunks[0]])
    for c, cols in enumerate(chunks):
        nxt = _dot(h, wup_ref[:, chunks[c + 1]]) if c + 1 < len(chunks) else None
        if side_bb and c == len(chunks) - 1:
            side_finish(side_scores)
        a = jnp.maximum(up, 0.0)
        acc = acc + _dot((a * a).astype(BF16), wdn_ref[cols, :])
        up = nxt
    y_ref[...] = _rmsnorm(acc, gf_ref[...])


def _mlp(x, g_mlp, w_up, w_down, g_final, attn=None, side=None):
    t = x.shape[0]
    tm = min(t, ROWS_SMALL)
    steps = t // tm
    row = pl.BlockSpec((tm, D_MODEL), lambda i: (i, 0))
    in_specs = [row]
    args = [x]
    if attn is not None:
        in_specs += [row, _full((D_MODEL, D_MODEL))]
        args += list(attn)
    in_specs += [_full((1, D_MODEL)), _full((D_MODEL, D_FF)), _full((D_FF, D_MODEL)), _full((1, D_MODEL))]
    args += [g_mlp, w_up, w_down, g_final]
    out_specs = [row]
    out_shape = [jax.ShapeDtypeStruct((t, D_MODEL), F32)]
    side_bb = side_len = 0
    if side is not None:
        q_rows = side[0]
        side_bb, rem = divmod(q_rows.shape[0], steps)
        if rem or not side_bb:
            raise ValueError("side attention sequences must spread evenly over the grid steps")
        side_len = q_rows.shape[1] // CACHE_SUB
        blk = lambda r: pl.BlockSpec((side_bb, r, LANES), lambda i: (i, 0, 0))
        in_specs += [blk(q_rows.shape[1]), blk(CACHE_ROWS), blk(CACHE_ROWS)]
        out_specs.append(blk(q_rows.shape[1]))
        out_shape.append(jax.ShapeDtypeStruct(q_rows.shape, BF16))
        args += list(side)
    outs = pl.pallas_call(
        functools.partial(_post_kernel, attn_out=attn is not None, side_bb=side_bb, side_len=side_len),
        grid=(steps,),
        in_specs=in_specs,
        out_specs=out_specs,
        out_shape=out_shape,
        compiler_params=_params("parallel"),
        name="attn_out_mlp",
    )(*args)
    return outs if side is not None else outs[0]


def _mixers(proj, batch, length, s5_re, s5_im, ret_s, p):
    tokens = batch * length
    u, packed, g = proj
    ssm, s5_re_new, s5_im_new = _s5_mixer(
        u.reshape(batch, length, SSM_WIDTH), s5_re.reshape(batch, SSM_LANES), s5_im.reshape(batch, SSM_LANES),
        p["s5_maps"], p["d_skip"], p["w_glu"], batch=batch, length=length)
    tok3 = lambda a: a.reshape(batch, length, RET_WIDTH)
    ret, ret_new = _retention(packed.reshape(batch, length, 4 * RET_WIDTH), tok3(g), ret_s, p["ret_gn"])
    states = (s5_re_new.reshape(1, batch, SSM_GROUPS, SSM_STATE),
              s5_im_new.reshape(1, batch, SSM_GROUPS, SSM_STATE), ret_new[None])
    return ssm.reshape(tokens, SSM_WIDTH), ret.reshape(tokens, RET_WIDTH), states


def kernel(x_prompt, x_sample, mem_prompt, state_s5_re, state_s5_im, state_ret, cache_mem_k, cache_mem_v, g_mix, w_in, lam_re, lam_im, log_dt, b_re, b_im, c_re, c_im, d_skip, w_glu, ret_gn, w_out, g_xattn, g_mem, w_mq, w_mk, w_mv, w_mo, g_mlp, w_up, w_down, g_final):
    if g_mix.shape[0] != 1:
        raise ValueError("this kernel implements the single-layer configuration")
    bp, lp, _ = x_prompt.shape
    bs, ls, _ = x_sample.shape
    vec = lambda a: a.reshape(1, -1).astype(F32)
    xp = x_prompt.reshape(bp * lp, D_MODEL)
    xs = x_sample.reshape(bs * ls, D_MODEL)
    proj_p, (w_in_b, w_up_b, w_down_b) = _project(
        xp, vec(g_mix[0]), w_in[0], length=lp, pos0=0.0, casts=(w_up[0], w_down[0]))
    w_out_b, w_mq_b, w_mo_b, w_mk_b, w_mv_b, w_glu_b = (
        w[0].astype(BF16) for w in (w_out, w_mq, w_mo, w_mk, w_mv, w_glu))
    proj_s, _ = _project(xs, vec(g_mix[0]), w_in_b, length=ls, pos0=float(PAST_LEN))
    p = dict(
        s5_maps=_s5_block_maps(lam_re[0], lam_im[0], log_dt[0], b_re[0], b_im[0], c_re[0], c_im[0]),
        d_skip=vec(d_skip[0]), w_glu=w_glu_b, ret_gn=vec(ret_gn[0]))
    g_x = vec(g_xattn[0])
    mlp = (vec(g_mlp[0]), w_up_b, w_down_b, vec(g_final))

    ssm_s, ret_s, states_s = _mixers(proj_s, bs, ls, state_s5_re[0], state_s5_im[0], state_ret[0], p)
    x1_s, q_s = _out_and_query(xs, ssm_s, ret_s, w_out_b, g_x, w_mq_b)
    if (ls * CACHE_SUB) % BF16_ROWS:
        raise ValueError("cached attention needs whole bf16 tiles of query rows")
    side = (_to_cache_rows(q_s, ls), _to_cache_rows(cache_mem_k, MEM_LEN), _to_cache_rows(cache_mem_v, MEM_LEN))

    mk_rows, mv_rows, mk_p, mv_p = _memory_kv(mem_prompt, vec(g_mem[0]), w_mk_b, w_mv_b)
    zs = jnp.zeros((bp, SSM_GROUPS, SSM_STATE), F32)
    zr = jnp.zeros((bp, RET_HEADS, RET_HEAD_DIM, RET_HEAD_DIM), F32)
    ssm_p, ret_p, states_p = _mixers(proj_p, bp, lp, zs, zs, zr, p)
    x2_p = _out_and_attention(xp, ssm_p, ret_p, w_out_b, g_x, w_mq_b, mk_p, mv_p, w_mo_b, length=lp)
    y_p, o_s = _mlp(x2_p, *mlp, side=side)
    y_s = _mlp(x1_s, *mlp, attn=(_from_cache_rows(o_s, ls).reshape(bs * ls, D_MODEL), w_mo_b))

    kv5 = lambda a: _from_cache_rows(a, MEM_LEN).reshape(1, bp, MEM_LEN, MEM_HEADS, MEM_HEAD_DIM)
    return (y_p.reshape(bp, lp, D_MODEL), y_s.reshape(bs, ls, D_MODEL), *states_p, kv5(mk_rows), kv5(mv_rows),
            *states_s)
```

```python
import functools
import math

import numpy as np
import jax
import jax.numpy as jnp
from jax import lax
from jax.experimental import pallas as pl
from jax.experimental.pallas import tpu as pltpu

F32 = jnp.float32
BF16 = jnp.bfloat16

D_MODEL = 1024
SSM_WIDTH = 512
SSM_GROUP = 16
SSM_GROUPS = 32
SSM_STATE = 64
SSM_LANES = SSM_GROUPS * SSM_STATE
RET_WIDTH = 512
RET_HEADS = 4
RET_HEAD_DIM = 128
RET_CHUNK = 128
ROPE_BASE = 10000.0
MEM_LEN = 256
MEM_HEADS = 4
MEM_HEAD_DIM = 256
MEM_SCALE = MEM_HEAD_DIM ** -0.5
D_FF = 4096
PROJ_WIDTH = SSM_WIDTH + 4 * RET_WIDTH
EPS = 1e-6
PAST_LEN = 16384

LANES = 128
MXU_TILE = 256
BF16_ROWS = 16
VMEM_LIMIT = 56 * 1024 * 1024

ROWS_LARGE = 1024
ROWS_SMALL = 512
S5_STEPS = 256

CACHE_ROWS = MEM_LEN * MEM_HEADS * MEM_HEAD_DIM // LANES
CACHE_SUB = MEM_HEADS * MEM_HEAD_DIM // LANES

assert math.frexp(MEM_SCALE)[0] == 0.5


def _dot(a, b):
    return jnp.dot(a, b, preferred_element_type=F32)


def _dot_nt(a, b):
    return lax.dot_general(a, b, (((1,), (1,)), ((), ())), preferred_element_type=F32)


def _dot_tn(a, b):
    return lax.dot_general(a, b, (((0,), (0,)), ((), ())), preferred_element_type=F32)


def _rmsnorm(x, g):
    return x * lax.rsqrt(jnp.mean(x * x, axis=-1, keepdims=True) + EPS) * g


def _sigmoid(x):
    return 1.0 / (1.0 + jnp.exp(-x))


def _gelu_tanh(x):
    c = math.sqrt(2.0 / math.pi)
    return x * (0.5 * jnp.tanh(x * (c + (c * 0.044715) * (x * x))) + 0.5)


def _params(*sem, fuse=None):
    return pltpu.CompilerParams(dimension_semantics=sem, vmem_limit_bytes=VMEM_LIMIT, allow_input_fusion=fuse)


def _fuse(n_inputs, *which):
    return [i in which for i in range(n_inputs)]


def _full(shape):
    return pl.BlockSpec(shape, lambda *_: (0,) * len(shape))


SSM_BLOCK = 4
SSM_TILE_GROUPS = MXU_TILE // SSM_STATE
SSM_TILES = SSM_GROUPS // SSM_TILE_GROUPS
SSM_TILE_CHANNELS = SSM_TILE_GROUPS * SSM_GROUP
GROUP_SHIFT = SSM_GROUP.bit_length() - 1
STATE_SHIFT = SSM_STATE.bit_length() - 1

assert SSM_BLOCK * SSM_TILE_CHANNELS == MXU_TILE and SSM_TILE_CHANNELS == SSM_STATE


def _s5_prep_kernel(lr_ref, li_ref, ldt_ref, br_ref, bi_ref, cr_ref, ci_ref,
                    ar_ref, ai_ref, winr_ref, wini_ref, woutr_ref, wouti_ref, kdir_ref):
    lr = lr_ref[...]
    li = li_ref[...]
    dt = jnp.exp(ldt_ref[...])
    mag = jnp.exp(lr * dt)
    ab_re = mag * jnp.cos(li * dt)
    ab_im = mag * jnp.sin(li * dt)
    den = lr * lr + li * li
    f_re = ((ab_re - 1.0) * lr + ab_im * li) / den
    f_im = (ab_im * lr - (ab_re - 1.0) * li) / den
    bb_re = f_re * br_ref[...] - f_im * bi_ref[...]
    bb_im = f_re * bi_ref[...] + f_im * br_ref[...]
    cr = cr_ref[...]
    ci = ci_ref[...]
    pows = [(jnp.ones_like(ab_re), jnp.zeros_like(ab_re))]
    for _ in range(SSM_BLOCK):
        pr, pi = pows[-1]
        pows.append((pr * ab_re - pi * ab_im, pr * ab_im + pi * ab_re))
    ar_ref[...], ai_ref[...] = pows[SSM_BLOCK]

    R, T, TG = SSM_BLOCK, SSM_TILES, SSM_TILE_GROUPS
    rows = SSM_GROUPS * SSM_GROUP
    side = SSM_TILE_CHANNELS
    exact = dict(precision=lax.Precision.HIGHEST, preferred_element_type=F32)
    iota = lambda shape, d: lax.broadcasted_iota(jnp.int32, shape, d)
    rep4 = lambda a, axis: jnp.concatenate([a] * TG, axis=axis)
    tiles = lambda a: a.reshape(T, side, a.shape[-1])
    group_of = lambda i: (i & (side - 1)) >> GROUP_SHIFT

    own_in = group_of(iota((rows, MXU_TILE), 0)) == iota((rows, MXU_TILE), 1) >> STATE_SHIFT
    for s in range(R):
        pr, pi = pows[R - 1 - s]
        for ref, w in ((winr_ref, pr * bb_re - pi * bb_im), (wini_ref, pr * bb_im + pi * bb_re)):
            ref[:, side * s:side * (s + 1), :] = tiles(jnp.where(own_in, rep4(w, 1), 0.0)).astype(BF16)

    own_out = iota((MXU_TILE, rows), 0) >> STATE_SHIFT == group_of(iota((MXU_TILE, rows), 1))
    for ref, real in ((woutr_ref, True), (wouti_ref, False)):
        per_lag = []
        for j in range(R):
            pr, pi = pows[j + 1]
            w = cr * pr - ci * pi if real else -(cr * pi + ci * pr)
            wt = w.T
            per_lag.append(jnp.where(own_out, rep4(wt, 0), 0.0))
        for n in range(T):
            ref[n] = jnp.concatenate([m[:, side * n:side * (n + 1)] for m in per_lag], axis=1).astype(BF16)

    grouped = lambda a: a.reshape(SSM_GROUPS, SSM_GROUP, SSM_STATE)
    contract = lambda a, b: lax.dot_general(a, b, (((2,), (2,)), ((0,), (0,))), **exact)
    own_dir = group_of(iota((rows, side), 0)) == iota((rows, side), 1) >> GROUP_SHIFT
    by_lag = []
    for d in range(R):
        pr, pi = pows[d]
        er = grouped(pr * bb_re - pi * bb_im)
        ei = grouped(pr * bb_im + pi * bb_re)
        kt = (contract(er, grouped(cr)) - contract(ei, grouped(ci))).reshape(rows, SSM_GROUP)
        by_lag.append(tiles(jnp.where(own_dir, rep4(kt, 1), 0.0)))
    nothing = jnp.zeros_like(by_lag[0])
    for s in range(R):
        kdir_ref[:, side * s:side * (s + 1), :] = jnp.concatenate(
            [by_lag[j - s] if j >= s else nothing for j in range(R)], axis=2).astype(BF16)


def _s5_block_maps(lam_re, lam_im, log_dt, b_re, b_im, c_re, c_im):
    rows = SSM_GROUPS * SSM_GROUP
    rep = lambda a: jnp.repeat(a, SSM_GROUP, axis=0)
    ldt = jnp.broadcast_to(log_dt[:, None], (SSM_GROUPS, SSM_STATE))
    bt = lambda b: b.transpose(0, 2, 1).reshape(rows, SSM_STATE)
    ct = lambda c: c.reshape(rows, SSM_STATE)
    one = jax.ShapeDtypeStruct((rows, SSM_STATE), F32)
    tile = jax.ShapeDtypeStruct((SSM_TILES, MXU_TILE, MXU_TILE), BF16)
    ar, ai, winr, wini, woutr, wouti, direct = pl.pallas_call(
        _s5_prep_kernel,
        out_shape=(one, one, tile, tile, tile, tile, tile),
        compiler_params=pltpu.CompilerParams(vmem_limit_bytes=VMEM_LIMIT),
        name="s5_block_maps",
    )(rep(lam_re), rep(lam_im), rep(ldt), bt(b_re), bt(b_im), ct(c_re), ct(c_im))
    a_re = ar[::SSM_GROUP].reshape(1, SSM_LANES)
    a_im = ai[::SSM_GROUP].reshape(1, SSM_LANES)
    return a_re, a_im, winr, wini, woutr, wouti, direct


def _retention_chunk(length):
    return RET_CHUNK if length % RET_CHUNK == 0 else length


def _log_gamma():
    return np.log(1.0 - 2.0 ** (-5.0 - np.arange(RET_HEADS, dtype=np.float64)))


def _rope_tables(length, pos0):
    half = RET_HEAD_DIM // 2
    inv = ROPE_BASE ** (-np.arange(half, dtype=np.float64) / half)
    ang = (pos0 + np.arange(length, dtype=np.float64))[:, None] * inv[None, :]
    return (np.concatenate([np.cos(ang), np.cos(ang)], axis=1),
            np.concatenate([-np.sin(ang), np.sin(ang)], axis=1))


def _zeta_rows(length):
    chunk = _retention_chunk(length)
    idx = np.arange(chunk, dtype=np.float64)
    zeta = np.exp((chunk - 1.0 - idx)[None, :] * _log_gamma()[:, None])
    return np.concatenate([np.broadcast_to(z[:, None], (chunk, RET_HEAD_DIM)) for z in zeta], axis=1)


def _decay_tables(length, padded):
    chunk = _retention_chunk(length)
    cpad = chunk if padded == length else padded
    lg = _log_gamma()
    idx = np.arange(chunk, dtype=np.float64)
    diff = idx[:, None] - idx[None, :]
    mask = np.where(diff[None] >= 0, np.exp(np.maximum(diff, 0.0)[None] * lg[:, None, None]), 0.0)
    xi = np.exp((idx + 1.0)[None, :] * lg[:, None])
    gamma_c = tuple(float(v) for v in np.exp(chunk * lg))
    mask = np.pad(mask, ((0, 0), (0, cpad - chunk), (0, cpad - chunk)))
    xi = np.broadcast_to(np.pad(xi, ((0, 0), (0, cpad - chunk)))[:, :, None], (RET_HEADS, cpad, RET_HEAD_DIM))
    f = lambda a: jnp.asarray(np.ascontiguousarray(a), dtype=F32)
    return cpad, gamma_c, f(mask), f(xi)


def _ret_lanes(part, head):
    lo = part * RET_WIDTH + head * RET_HEAD_DIM
    return slice(lo, lo + RET_HEAD_DIM)


def _proj_kernel(x_ref, g_ref, w_ref, cc_ref, ss_ref, zeta_ref, *rest, n_cast):
    casts_in, rest = rest[:n_cast], rest[n_cast:]
    u_ref, ret_ref, gate_ref = rest[:3]
    w = w_ref[...]
    if n_cast:
        w = w.astype(BF16)
        for src, dst in zip(casts_in, rest[3:]):
            dst[...] = src[...].astype(BF16)
    h = _rmsnorm(x_ref[...], g_ref[...]).astype(BF16)
    proj = _dot(h, w)
    u_ref[...] = proj[:, :SSM_WIDTH]
    ret_ref[:, 3 * RET_WIDTH:] = proj[:, SSM_WIDTH + 2 * RET_WIDTH:SSM_WIDTH + 3 * RET_WIDTH].astype(BF16)
    gate_ref[...] = proj[:, SSM_WIDTH + 3 * RET_WIDTH:]
    cc = cc_ref[...]
    ss = ss_ref[...]
    rope = lambda a: a * cc + pltpu.roll(a, RET_HEAD_DIM // 2, axis=1) * ss
    for hd in range(RET_HEADS):
        cols = slice(hd * RET_HEAD_DIM, (hd + 1) * RET_HEAD_DIM)
        q = rope(proj[:, SSM_WIDTH + hd * RET_HEAD_DIM:SSM_WIDTH + (hd + 1) * RET_HEAD_DIM])
        k = rope(proj[:, SSM_WIDTH + RET_WIDTH + hd * RET_HEAD_DIM:SSM_WIDTH + RET_WIDTH + (hd + 1) * RET_HEAD_DIM])
        ret_ref[:, _ret_lanes(0, hd)] = (q * (RET_HEAD_DIM ** -0.5)).astype(BF16)
        ret_ref[:, _ret_lanes(1, hd)] = k.astype(BF16)
        ret_ref[:, _ret_lanes(2, hd)] = (k * zeta_ref[:, cols]).astype(BF16)


def _project(x, g_mix, w_in, *, length, pos0, casts=()):
    t = x.shape[0]
    tm = min(t, ROWS_LARGE)
    steps = t // tm
    chunk = _retention_chunk(length)
    casts = (w_in, *casts) if casts else ()
    if tm % chunk or (length % tm and tm % length):
        raise ValueError("token tiles must hold whole retention chunks of whole or repeated sequences")
    cc, ss = _rope_tables(length, pos0)
    if length < tm:
        cc, ss = np.tile(cc, (tm // length, 1)), np.tile(ss, (tm // length, 1))
    tab_blocks = cc.shape[0] // tm
    zeta = np.tile(_zeta_rows(length), (tm // chunk, 1))
    const = lambda a: jnp.asarray(np.ascontiguousarray(a), dtype=F32)
    row = lambda w: pl.BlockSpec((tm, w), lambda i: (i, 0))
    tab = pl.BlockSpec((tm, RET_HEAD_DIM), lambda i: (i % tab_blocks, 0))
    f = jax.ShapeDtypeStruct((t, SSM_WIDTH), F32)
    h = jax.ShapeDtypeStruct((t, 4 * RET_WIDTH), BF16)
    in_specs = [row(D_MODEL), _full((1, D_MODEL)), _full((D_MODEL, PROJ_WIDTH)), tab, tab, _full((tm, RET_WIDTH))]
    out_specs = [row(SSM_WIDTH), row(4 * RET_WIDTH), row(RET_WIDTH)]
    out_shape = [f, h, f]
    if casts:
        for c in casts:
            rows, rem = divmod(c.shape[0], steps)
            if rem or rows % BF16_ROWS:
                raise ValueError("side-cast weights must split into whole bf16 tiles per grid step")
            blk = pl.BlockSpec((rows, c.shape[1]), lambda i: (i, 0))
            in_specs.append(blk)
            out_specs.append(blk)
            out_shape.append(jax.ShapeDtypeStruct(c.shape, BF16))
    outs = pl.pallas_call(
        functools.partial(_proj_kernel, n_cast=len(casts)),
        grid=(steps,),
        in_specs=in_specs,
        out_specs=out_specs,
        out_shape=out_shape,
        compiler_params=_params("arbitrary" if casts else "parallel", fuse=_fuse(len(in_specs), 0)),
        name="in_proj",
    )(x, g_mix, w_in, const(cc), const(ss), const(zeta), *casts)
    return outs[:3], outs[3:]


def _s5_block_kernel(u_ref, h0r_ref, h0i_ref, ar_ref, ai_ref, winr_ref, wini_ref, woutr_ref, wouti_ref,
                     kdir_ref, d_ref, wglu_ref, out_ref, hr_ref, hi_ref, vr_scr, vi_scr, xr_scr, xi_scr,
                     *, batch, steps, pair, width):
    R, T = SSM_BLOCK, SSM_TILES
    chan = SSM_TILE_CHANNELS
    blocks = steps // R
    rows, brow = steps * batch, blocks * batch

    @pl.when(pl.program_id(0) == 0)
    def _():
        hr_ref[...] = h0r_ref[...]
        hi_ref[...] = h0i_ref[...]

    u = jnp.swapaxes(u_ref[...], 0, 1).reshape(rows, SSM_WIDTH)
    u4 = u.reshape(blocks, R, batch, SSM_WIDTH)
    lag = [u4[:, s].reshape(brow, SSM_WIDTH) for s in range(R)]
    direct = []
    for n in range(T):
        un = jnp.concatenate([lag[s][:, chan * n:chan * (n + 1)] for s in range(R)], axis=1).astype(BF16)
        lanes = slice(MXU_TILE * n, MXU_TILE * (n + 1))
        vr_scr[:, lanes] = _dot(un, winr_ref[n])
        vi_scr[:, lanes] = _dot(un, wini_ref[n])
        direct.append(_dot(un, kdir_ref[n]))

    for j in range(SSM_LANES // width):
        glob = slice(j * width, (j + 1) * width)
        ar = jnp.broadcast_to(ar_ref[:, glob], (batch, width))
        ai = jnp.broadcast_to(ai_ref[:, glob], (batch, width))
        xr = hr_ref[:, glob]
        xi = hi_ref[:, glob]
        for i in range(blocks // pair):
            prev_r, prev_i = [], []
            for s in range(pair):
                at = slice((i * pair + s) * batch, (i * pair + s + 1) * batch)
                prev_r.append(xr)
                prev_i.append(xi)
                xr, xi = ar * xr - ai * xi + vr_scr[at, glob], ar * xi + ai * xr + vi_scr[at, glob]
            blk = slice(i * pair * batch, (i + 1) * pair * batch)
            xr_scr[blk, glob] = jnp.concatenate(prev_r, axis=0).astype(BF16)
            xi_scr[blk, glob] = jnp.concatenate(prev_i, axis=0).astype(BF16)
        hr_ref[:, glob] = xr
        hi_ref[:, glob] = xi

    y4 = []
    for n in range(T):
        lanes = slice(MXU_TILE * n, MXU_TILE * (n + 1))
        y4.append(_dot(xr_scr[:, lanes], woutr_ref[n]) + _dot(xi_scr[:, lanes], wouti_ref[n]) + direct[n])
    per_lag = [jnp.concatenate([y4[n][:, chan * j:chan * (j + 1)] for n in range(T)], axis=1)
               .reshape(blocks, batch, SSM_WIDTH) for j in range(R)]
    y = jnp.stack(per_lag, axis=1).reshape(rows, SSM_WIDTH) + d_ref[...] * u
    z = _gelu_tanh(y)
    out = z * _sigmoid(_dot(z.astype(BF16), wglu_ref[...]))
    out_ref[...] = jnp.swapaxes(out.reshape(steps, batch, SSM_WIDTH), 0, 1).astype(BF16)


def _s5_mixer(u, h0_re, h0_im, maps, d_skip, w_glu, *, batch, length):
    a_re, a_im, winr, wini, woutr, wouti, direct = maps
    steps = min(length, S5_STEPS)
    if steps % SSM_BLOCK or length % steps:
        raise ValueError("sequence length must be a multiple of the S5 block")
    blocks = steps // SSM_BLOCK
    pair = max(1, min(blocks, BF16_ROWS // batch))
    if (pair * batch) % BF16_ROWS or blocks % pair:
        raise ValueError("S5 state stores need whole bf16 tiles")
    width = SSM_LANES // 2 if batch <= 8 else LANES
    brow = blocks * batch
    kern = functools.partial(_s5_block_kernel, batch=batch, steps=steps, pair=pair, width=width)
    st = jax.ShapeDtypeStruct((batch, SSM_LANES), F32)
    tile = _full((SSM_TILES, MXU_TILE, MXU_TILE))
    return pl.pallas_call(
        kern,
        grid=(length // steps,),
        in_specs=[pl.BlockSpec((batch, steps, SSM_WIDTH), lambda i: (0, i, 0)),
                  _full((batch, SSM_LANES)), _full((batch, SSM_LANES)),
                  _full((1, SSM_LANES)), _full((1, SSM_LANES)),
                  tile, tile, tile, tile, tile,
                  _full((1, SSM_WIDTH)), _full((SSM_WIDTH, SSM_WIDTH))],
        out_specs=[pl.BlockSpec((batch, steps, SSM_WIDTH), lambda i: (0, i, 0)),
                   _full((batch, SSM_LANES)), _full((batch, SSM_LANES))],
        out_shape=(jax.ShapeDtypeStruct((batch, length, SSM_WIDTH), BF16), st, st),
        scratch_shapes=[pltpu.VMEM((brow, SSM_LANES), F32), pltpu.VMEM((brow, SSM_LANES), F32),
                        pltpu.VMEM((brow, SSM_LANES), BF16), pltpu.VMEM((brow, SSM_LANES), BF16)],
        compiler_params=_params("arbitrary", fuse=_fuse(12, 11)),
        name="s5_mixer",
    )(u, h0_re, h0_im, a_re, a_im, winr, wini, woutr, wouti, direct, d_skip, w_glu)


def _ret_kernel(p_ref, g_ref, mask_ref, xi_ref, gain_ref, s0_ref, o_ref, s_ref,
                *, bb, gamma_c, unroll, chunk, chunks):
    @pl.when(pl.program_id(1) == 0)
    def _():
        s_ref[...] = s0_ref[...]

    heads = range(RET_HEADS)
    cols = [slice(h * RET_HEAD_DIM, (h + 1) * RET_HEAD_DIM) for h in heads]

    def one_chunk(b, rows):
        qb = [p_ref[b, rows, _ret_lanes(0, h)] for h in heads]
        vb = [p_ref[b, rows, _ret_lanes(3, h)] for h in heads]
        s_prev = [s_ref[b, h] for h in heads]
        scores = [(_dot_nt(qb[h], p_ref[b, rows, _ret_lanes(1, h)]) * mask_ref[h]).astype(BF16) for h in heads]
        cross = [_dot(qb[h], s_prev[h].astype(BF16)) * xi_ref[h] for h in heads]
        kv = [_dot_tn(p_ref[b, rows, _ret_lanes(2, h)], vb[h]) for h in heads]
        inner = [_dot(scores[h], vb[h]) for h in heads]
        outs = []
        for h in heads:
            s_ref[b, h] = s_prev[h] * gamma_c[h] + kv[h]
            o = inner[h] + cross[h]
            mu = jnp.mean(o, axis=-1, keepdims=True)
            oc = o - mu
            var = jnp.mean(oc * oc, axis=-1, keepdims=True)
            on = oc * lax.rsqrt(var + EPS) * gain_ref[:, cols[h]]
            gate = g_ref[b, rows, cols[h]]
            outs.append((gate * _sigmoid(gate) * on).astype(BF16))
        o_ref[b, rows, :] = jnp.concatenate(outs, axis=1)

    def per_batch(b, carry):
        for c in range(chunks):
            one_chunk(b, slice(c * chunk, (c + 1) * chunk))
        return carry

    if unroll == bb:
        for b in range(bb):
            per_batch(b, 0)
    else:
        lax.fori_loop(0, bb, per_batch, 0, unroll=unroll)


def _retention(packed, g, s0, gn_gain):
    batch, length, _ = g.shape
    padded = pl.cdiv(length, BF16_ROWS) * BF16_ROWS
    chunk, gamma_c, mask, xi = _decay_tables(length, padded)
    if padded != length:
        pad = lambda a: jnp.pad(a, ((0, 0), (0, padded - length), (0, 0)))
        packed, g = pad(packed), pad(g)
    bb, unroll = (8, 2) if chunk > BF16_ROWS else (16, 16)
    chunks = 2 if (padded // chunk) % 2 == 0 else 1
    tok = pl.BlockSpec((bb, chunks * chunk, RET_WIDTH), lambda i, c: (i, c, 0))
    state = pl.BlockSpec((bb, RET_HEADS, RET_HEAD_DIM, RET_HEAD_DIM), lambda i, c: (i, 0, 0, 0))
    out, s_new = pl.pallas_call(
        functools.partial(_ret_kernel, bb=bb, gamma_c=gamma_c, unroll=unroll, chunk=chunk, chunks=chunks),
        grid=(batch // bb, padded // (chunks * chunk)),
        in_specs=[pl.BlockSpec((bb, chunks * chunk, 4 * RET_WIDTH), lambda i, c: (i, c, 0)), tok,
                  _full((RET_HEADS, chunk, chunk)), _full((RET_HEADS, chunk, RET_HEAD_DIM)),
                  _full((1, RET_WIDTH)), state],
        out_specs=[tok, state],
        out_shape=(jax.ShapeDtypeStruct((batch, padded, RET_WIDTH), BF16),
                   jax.ShapeDtypeStruct(s0.shape, F32)),
        compiler_params=_params("parallel", "arbitrary", fuse=_fuse(6, 0, 1)),
        name="retention",
    )(packed, g, mask, xi, gn_gain, s0)
    return out[:, :length], s_new


def _mixer_out(x_ref, ssm_ref, ret_ref, wout_ref, rows=slice(None)):
    return (x_ref[rows, :] + _dot(ssm_ref[rows, :], wout_ref[:SSM_WIDTH, :])
            + _dot(ret_ref[rows, :], wout_ref[SSM_WIDTH:, :]))


def _query(x1, g_ref, wq_ref):
    return (_dot(_rmsnorm(x1, g_ref[...]).astype(BF16), wq_ref[...]) * MEM_SCALE).astype(BF16)


def _outq_kernel(x_ref, ssm_ref, ret_ref, wout_ref, g_ref, wq_ref, x1_ref, q_ref):
    x1 = _mixer_out(x_ref, ssm_ref, ret_ref, wout_ref)
    x1_ref[...] = x1
    q_ref[...] = _query(x1, g_ref, wq_ref)


def _out_and_query(x, ssm, ret, w_out, g_xattn, w_mq):
    t = x.shape[0]
    tm = min(t, ROWS_SMALL)
    row = lambda w: pl.BlockSpec((tm, w), lambda i: (i, 0))
    return pl.pallas_call(
        _outq_kernel,
        grid=(t // tm,),
        in_specs=[row(D_MODEL), row(SSM_WIDTH), row(RET_WIDTH), _full((D_MODEL, D_MODEL)),
                  _full((1, D_MODEL)), _full((D_MODEL, D_MODEL))],
        out_specs=[row(D_MODEL), row(D_MODEL)],
        out_shape=(jax.ShapeDtypeStruct((t, D_MODEL), F32), jax.ShapeDtypeStruct((t, D_MODEL), BF16)),
        compiler_params=_params("parallel", fuse=_fuse(6, 0, 3, 5)),
        name="out_proj_query",
    )(x, ssm, ret, w_out, g_xattn, w_mq)


def _cache_row_order(x):
    halves = MEM_HEAD_DIM // LANES
    tiles = [x[:, (h * halves + d) * LANES:(h * halves + d + 1) * LANES]
             for d in range(halves) for h in range(MEM_HEADS)]
    return jnp.swapaxes(jnp.stack(tiles, axis=0), 0, 1).reshape(x.shape[0] * CACHE_SUB, LANES)


MEMKV_SEQS = 2


def _memkv_kernel(m_ref, g_ref, wk_ref, wv_ref, k_ref, v_ref, kb_ref, vb_ref):
    seqs = m_ref.shape[0]
    m = _rmsnorm(m_ref[...].reshape(seqs * MEM_LEN, D_MODEL), g_ref[...]).astype(BF16)
    k = _dot(m, wk_ref[...])
    v = _dot(m, wv_ref[...])
    k_ref[...] = _cache_row_order(k).reshape(k_ref.shape)
    v_ref[...] = _cache_row_order(v).reshape(v_ref.shape)
    kb_ref[...] = k.astype(BF16).reshape(kb_ref.shape)
    vb_ref[...] = v.astype(BF16).reshape(vb_ref.shape)


def _memory_kv(mem, g_mem, w_mk, w_mv):
    batch = mem.shape[0]
    seqs = MEMKV_SEQS if batch % MEMKV_SEQS == 0 else 1
    tok = pl.BlockSpec((seqs, MEM_LEN, D_MODEL), lambda i: (i, 0, 0))
    cache = pl.BlockSpec((seqs, CACHE_ROWS, LANES), lambda i: (i, 0, 0))
    f = jax.ShapeDtypeStruct((batch, CACHE_ROWS, LANES), F32)
    h = jax.ShapeDtypeStruct((batch, MEM_LEN, D_MODEL), BF16)
    return pl.pallas_call(
        _memkv_kernel,
        grid=(batch // seqs,),
        in_specs=[tok, _full((1, D_MODEL)), _full((D_MODEL, D_MODEL)), _full((D_MODEL, D_MODEL))],
        out_specs=[cache, cache, tok, tok],
        out_shape=(f, f, h, h),
        compiler_params=_params("parallel", fuse=_fuse(4, 2, 3)),
        name="memory_kv",
    )(mem, g_mem, w_mk, w_mv)


ATTN_ROWS = 512


def _outattn_kernel(x_ref, ssm_ref, ret_ref, wout_ref, g_ref, wq_ref, k_ref, v_ref, x1_ref, o_ref):
    cols = [slice(h * MEM_HEAD_DIM, (h + 1) * MEM_HEAD_DIM) for h in range(MEM_HEADS)]
    groups = [slice(r, r + ATTN_ROWS) for r in range(0, x_ref.shape[0], ATTN_ROWS)]

    def scores_of(rows):
        x1 = _mixer_out(x_ref, ssm_ref, ret_ref, wout_ref, rows)
        x1_ref[rows, :] = x1
        q = _query(x1, g_ref, wq_ref)
        return [_dot_nt(q[:, c], k_ref[0, :, c]) for c in cols]

    def attend(rows, scores):
        outs = []
        for s, c in zip(scores, cols):
            e = jnp.exp(s - jnp.max(s, axis=-1, keepdims=True))
            p = (e / jnp.sum(e, axis=-1, keepdims=True)).astype(BF16)
            outs.append(_dot(p, v_ref[0, :, c]).astype(BF16))
        o_ref[rows, :] = jnp.concatenate(outs, axis=1)

    pending = None
    for rows in groups:
        scores = scores_of(rows)
        if pending is not None:
            attend(*pending)
        pending = (rows, scores)
    attend(*pending)


def _out_and_attention(x, ssm, ret, w_out, g_xattn, w_mq, mk, mv, *, length):
    t = x.shape[0]
    tm = min(length, ROWS_LARGE)
    per_seq = length // tm
    row = lambda w: pl.BlockSpec((tm, w), lambda i: (i, 0))
    mem = pl.BlockSpec((1, MEM_LEN, D_MODEL), lambda i: (i // per_seq, 0, 0))
    return pl.pallas_call(
        _outattn_kernel,
        grid=(t // tm,),
        in_specs=[row(D_MODEL), row(SSM_WIDTH), row(RET_WIDTH), _full((D_MODEL, D_MODEL)),
                  _full((1, D_MODEL)), _full((D_MODEL, D_MODEL)), mem, mem],
        out_specs=[row(D_MODEL), row(D_MODEL)],
        out_shape=(jax.ShapeDtypeStruct((t, D_MODEL), F32), jax.ShapeDtypeStruct((t, D_MODEL), BF16)),
        compiler_params=_params("parallel", fuse=_fuse(8, 3, 5)),
        name="out_proj_attention",
    )(x, ssm, ret, w_out, g_xattn, w_mq, mk, mv)


def _cached_attention(q_ref, k_ref, v_ref, o_ref, *, bb, length):
    hit = (lax.broadcasted_iota(jnp.int32, (CACHE_SUB, CACHE_ROWS), 0)
           == lax.broadcasted_iota(jnp.int32, (CACHE_SUB, CACHE_ROWS), 1) % CACHE_SUB)
    first_half = lax.broadcasted_iota(jnp.int32, (length, LANES), 1) % CACHE_SUB < MEM_HEADS
    tiles = CACHE_ROWS // LANES

    def class_reduce(x, op):
        shift = CACHE_SUB
        while shift < LANES:
            x = op(x, pltpu.roll(x, shift, axis=1))
            shift *= 2
        return x

    def softmax_rows(r):
        z = jnp.concatenate(
            [jnp.sum(jnp.where(hit, r[CACHE_SUB * t:CACHE_SUB * (t + 1)], 0.0), axis=0, keepdims=True)
             for t in range(length)], axis=0)
        parts = []
        for i in range(tiles):
            zi = z[:, LANES * i:LANES * (i + 1)]
            parts.append(zi + pltpu.roll(zi, LANES - MEM_HEADS, axis=1))
        mx = parts[0]
        for pi in parts[1:]:
            mx = jnp.maximum(mx, pi)
        mx = class_reduce(mx, jnp.maximum)
        es = [jnp.exp(pi - mx) for pi in parts]
        tot = es[0]
        for ei in es[1:]:
            tot = tot + ei
        tot = class_reduce(tot, jnp.add)
        ps = []
        for ei in es:
            pi = ei / tot
            ps.append(jnp.where(first_half, pi, pltpu.roll(pi, MEM_HEADS, axis=1)))
        p = jnp.concatenate(ps, axis=1)
        return jnp.concatenate(
            [jnp.where(hit, jnp.broadcast_to(p[t:t + 1], (CACHE_SUB, CACHE_ROWS)), 0.0)
             for t in range(length)], axis=0).astype(BF16)

    def start():
        return [_dot_nt(q_ref[b], k_ref[b].astype(BF16)) for b in range(bb)]

    def finish(scores):
        probs = [softmax_rows(r) for r in scores]
        for b in range(bb):
            o_ref[b] = _dot(probs[b], v_ref[b].astype(BF16)).astype(BF16)

    return start, finish


def _to_cache_rows(a, lead):
    halves = MEM_HEAD_DIM // LANES
    batch = a.size // (lead * D_MODEL)
    a = a.reshape(batch, lead, MEM_HEADS, halves, LANES).transpose(0, 1, 3, 2, 4)
    return a.reshape(batch, lead * CACHE_SUB, LANES)


def _from_cache_rows(a, lead):
    halves = MEM_HEAD_DIM // LANES
    batch = a.shape[0]
    a = a.reshape(batch, lead, halves, MEM_HEADS, LANES).transpose(0, 1, 3, 2, 4)
    return a.reshape(batch, lead, D_MODEL)


MLP_CHUNK = 1024


def _post_kernel(x1_ref, o_ref, wo_ref, gm_ref, wup_ref, wdn_ref, gf_ref, *rest, side_bb, side_len):
    if side_bb:
        qs_ref, ck_ref, cv_ref, y_ref, os_ref = rest
    else:
        (y_ref,) = rest
    x2 = x1_ref[...] + _dot(o_ref[...], wo_ref[...])
    if side_bb:
        side_start, side_finish = _cached_attention(qs_ref, ck_ref, cv_ref, os_ref, bb=side_bb, length=side_len)
        side_scores = side_start()
    h = _rmsnorm(x2, gm_ref[...]).astype(BF16)
    chunks = [slice(c, c + MLP_CHUNK) for c in range(0, D_FF, MLP_CHUNK)]
    acc = x2
    up = _dot(h, wup_ref[:, chunks[0]])
    for c, cols in enumerate(chunks):
        nxt = _dot(h, wup_ref[:, chunks[c + 1]]) if c + 1 < len(chunks) else None
        if side_bb and c == len(chunks) - 1:
            side_finish(side_scores)
        a = jnp.maximum(up, 0.0)
        acc = acc + _dot((a * a).astype(BF16), wdn_ref[cols, :])
        up = nxt
    y_ref[...] = _rmsnorm(acc, gf_ref[...])


def _attn_out_mlp(x1, o, w_mo, g_mlp, w_up, w_down, g_final, side=None):
    t = x1.shape[0]
    tm = min(t, ROWS_SMALL)
    steps = t // tm
    row = pl.BlockSpec((tm, D_MODEL), lambda i: (i, 0))
    in_specs = [row, row, _full((D_MODEL, D_MODEL)), _full((1, D_MODEL)),
                _full((D_MODEL, D_FF)), _full((D_FF, D_MODEL)), _full((1, D_MODEL))]
    out_specs = [row]
    out_shape = [jax.ShapeDtypeStruct((t, D_MODEL), F32)]
    args = [x1, o, w_mo, g_mlp, w_up, w_down, g_final]
    side_bb = side_len = 0
    if side is not None:
        q_rows = side[0]
        side_bb, rem = divmod(q_rows.shape[0], steps)
        if rem or not side_bb:
            raise ValueError("side attention sequences must spread evenly over the grid steps")
        side_len = q_rows.shape[1] // CACHE_SUB
        blk = lambda r: pl.BlockSpec((side_bb, r, LANES), lambda i: (i, 0, 0))
        in_specs += [blk(q_rows.shape[1]), blk(CACHE_ROWS), blk(CACHE_ROWS)]
        out_specs.append(blk(q_rows.shape[1]))
        out_shape.append(jax.ShapeDtypeStruct(q_rows.shape, BF16))
        args += list(side)
    outs = pl.pallas_call(
        functools.partial(_post_kernel, side_bb=side_bb, side_len=side_len),
        grid=(steps,),
        in_specs=in_specs,
        out_specs=out_specs,
        out_shape=out_shape,
        compiler_params=_params("parallel", fuse=_fuse(len(in_specs), 1, 2, 7)),
        name="attn_out_mlp",
    )(*args)
    return outs if side is not None else outs[0]


def _mixers(proj, batch, length, s5_re, s5_im, ret_s, p):
    tokens = batch * length
    u, packed, g = proj
    ssm, s5_re_new, s5_im_new = _s5_mixer(
        u.reshape(batch, length, SSM_WIDTH), s5_re.reshape(batch, SSM_LANES), s5_im.reshape(batch, SSM_LANES),
        p["s5_maps"], p["d_skip"], p["w_glu"], batch=batch, length=length)
    tok3 = lambda a: a.reshape(batch, length, RET_WIDTH)
    ret, ret_new = _retention(packed.reshape(batch, length, 4 * RET_WIDTH), tok3(g), ret_s, p["ret_gn"])
    states = (s5_re_new.reshape(1, batch, SSM_GROUPS, SSM_STATE),
              s5_im_new.reshape(1, batch, SSM_GROUPS, SSM_STATE), ret_new[None])
    return ssm.reshape(tokens, SSM_WIDTH), ret.reshape(tokens, RET_WIDTH), states


def kernel(x_prompt, x_sample, mem_prompt, state_s5_re, state_s5_im, state_ret, cache_mem_k, cache_mem_v, g_mix, w_in, lam_re, lam_im, log_dt, b_re, b_im, c_re, c_im, d_skip, w_glu, ret_gn, w_out, g_xattn, g_mem, w_mq, w_mk, w_mv, w_mo, g_mlp, w_up, w_down, g_final):
    if g_mix.shape[0] != 1:
        raise ValueError("this kernel implements the single-layer configuration")
    bp, lp, _ = x_prompt.shape
    bs, ls, _ = x_sample.shape
    vec = lambda a: a.reshape(1, -1).astype(F32)
    xp = x_prompt.reshape(bp * lp, D_MODEL)
    xs = x_sample.reshape(bs * ls, D_MODEL)
    proj_p, (w_in_b, w_up_b, w_down_b) = _project(
        xp, vec(g_mix[0]), w_in[0], length=lp, pos0=0.0, casts=(w_up[0], w_down[0]))
    w_out_b, w_mq_b, w_mo_b, w_mk_b, w_mv_b, w_glu_b = (
        w[0].astype(BF16) for w in (w_out, w_mq, w_mo, w_mk, w_mv, w_glu))
    proj_s, _ = _project(xs, vec(g_mix[0]), w_in_b, length=ls, pos0=float(PAST_LEN))
    p = dict(
        s5_maps=_s5_block_maps(lam_re[0], lam_im[0], log_dt[0], b_re[0], b_im[0], c_re[0], c_im[0]),
        d_skip=vec(d_skip[0]), w_glu=w_glu_b, ret_gn=vec(ret_gn[0]))
    g_x = vec(g_xattn[0])
    mlp = (w_mo_b, vec(g_mlp[0]), w_up_b, w_down_b, vec(g_final))

    ssm_s, ret_s, states_s = _mixers(proj_s, bs, ls, state_s5_re[0], state_s5_im[0], state_ret[0], p)
    x1_s, q_s = _out_and_query(xs, ssm_s, ret_s, w_out_b, g_x, w_mq_b)
    if (ls * CACHE_SUB) % BF16_ROWS:
        raise ValueError("cached attention needs whole bf16 tiles of query rows")
    side = (_to_cache_rows(q_s, ls), _to_cache_rows(cache_mem_k, MEM_LEN), _to_cache_rows(cache_mem_v, MEM_LEN))

    mk_rows, mv_rows, mk_p, mv_p = _memory_kv(mem_prompt, vec(g_mem[0]), w_mk_b, w_mv_b)
    zs = jnp.zeros((bp, SSM_GROUPS, SSM_STATE), F32)
    zr = jnp.zeros((bp, RET_HEADS, RET_HEAD_DIM, RET_HEAD_DIM), F32)
    ssm_p, ret_p, states_p = _mixers(proj_p, bp, lp, zs, zs, zr, p)
    x1_p, o_p = _out_and_attention(xp, ssm_p, ret_p, w_out_b, g_x, w_mq_b, mk_p, mv_p, length=lp)
    y_p, o_s = _attn_out_mlp(x1_p, o_p, *mlp, side=side)
    y_s = _attn_out_mlp(x1_s, _from_cache_rows(o_s, ls).reshape(bs * ls, D_MODEL), *mlp)

    kv5 = lambda a: _from_cache_rows(a, MEM_LEN).reshape(1, bp, MEM_LEN, MEM_HEADS, MEM_HEAD_DIM)
    return (y_p.reshape(bp, lp, D_MODEL), y_s.reshape(bs, ls, D_MODEL), *states_p, kv5(mk_rows), kv5(mv_rows),
            *states_s)
```

```python
import functools
import math

import numpy as np
import jax
import jax.numpy as jnp
from jax import lax
from jax.experimental import pallas as pl
from jax.experimental.pallas import tpu as pltpu

F32 = jnp.float32
BF16 = jnp.bfloat16

D_MODEL = 1024
SSM_WIDTH = 512
SSM_GROUP = 16
SSM_GROUPS = 32
SSM_STATE = 64
SSM_LANES = SSM_GROUPS * SSM_STATE
RET_WIDTH = 512
RET_HEADS = 4
RET_HEAD_DIM = 128
RET_CHUNK = 128
ROPE_BASE = 10000.0
MEM_LEN = 256
MEM_HEADS = 4
MEM_HEAD_DIM = 256
MEM_SCALE = MEM_HEAD_DIM ** -0.5
D_FF = 4096
PROJ_WIDTH = SSM_WIDTH + 4 * RET_WIDTH
EPS = 1e-6
PAST_LEN = 16384

LANES = 128
MXU_TILE = 256
BF16_ROWS = 16
VMEM_LIMIT = 56 * 1024 * 1024

ROWS_LARGE = 1024
ROWS_SMALL = 512
S5_STEPS = 256

CACHE_ROWS = MEM_LEN * MEM_HEADS * MEM_HEAD_DIM // LANES
CACHE_SUB = MEM_HEADS * MEM_HEAD_DIM // LANES

assert math.frexp(MEM_SCALE)[0] == 0.5


def _dot(a, b):
    return jnp.dot(a, b, preferred_element_type=F32)


def _dot_nt(a, b):
    return lax.dot_general(a, b, (((1,), (1,)), ((), ())), preferred_element_type=F32)


def _dot_tn(a, b):
    return lax.dot_general(a, b, (((0,), (0,)), ((), ())), preferred_element_type=F32)


def _rmsnorm(x, g):
    return x * lax.rsqrt(jnp.mean(x * x, axis=-1, keepdims=True) + EPS) * g


def _sigmoid(x):
    return 1.0 / (1.0 + jnp.exp(-x))


def _gelu_tanh(x):
    c = math.sqrt(2.0 / math.pi)
    return x * (0.5 * jnp.tanh(x * (c + (c * 0.044715) * (x * x))) + 0.5)


def _params(*sem, fuse=None):
    return pltpu.CompilerParams(dimension_semantics=sem, vmem_limit_bytes=VMEM_LIMIT, allow_input_fusion=fuse)


def _fuse(n_inputs, *which):
    return [i in which for i in range(n_inputs)]


def _full(shape):
    return pl.BlockSpec(shape, lambda *_: (0,) * len(shape))


SSM_BLOCK = 4
SSM_TILE_GROUPS = MXU_TILE // SSM_STATE
SSM_TILES = SSM_GROUPS // SSM_TILE_GROUPS
SSM_TILE_CHANNELS = SSM_TILE_GROUPS * SSM_GROUP
GROUP_SHIFT = SSM_GROUP.bit_length() - 1
STATE_SHIFT = SSM_STATE.bit_length() - 1

assert SSM_BLOCK * SSM_TILE_CHANNELS == MXU_TILE and SSM_TILE_CHANNELS == SSM_STATE


def _s5_prep_kernel(lr_ref, li_ref, ldt_ref, br_ref, bi_ref, cr_ref, ci_ref,
                    ar_ref, ai_ref, winr_ref, wini_ref, woutr_ref, wouti_ref, kdir_ref):
    lr = lr_ref[...]
    li = li_ref[...]
    dt = jnp.exp(ldt_ref[...])
    mag = jnp.exp(lr * dt)
    ab_re = mag * jnp.cos(li * dt)
    ab_im = mag * jnp.sin(li * dt)
    den = lr * lr + li * li
    f_re = ((ab_re - 1.0) * lr + ab_im * li) / den
    f_im = (ab_im * lr - (ab_re - 1.0) * li) / den
    bb_re = f_re * br_ref[...] - f_im * bi_ref[...]
    bb_im = f_re * bi_ref[...] + f_im * br_ref[...]
    cr = cr_ref[...]
    ci = ci_ref[...]
    pows = [(jnp.ones_like(ab_re), jnp.zeros_like(ab_re))]
    for _ in range(SSM_BLOCK):
        pr, pi = pows[-1]
        pows.append((pr * ab_re - pi * ab_im, pr * ab_im + pi * ab_re))
    ar_ref[...], ai_ref[...] = pows[SSM_BLOCK]

    R, T, TG = SSM_BLOCK, SSM_TILES, SSM_TILE_GROUPS
    rows = SSM_GROUPS * SSM_GROUP
    side = SSM_TILE_CHANNELS
    exact = dict(precision=lax.Precision.HIGHEST, preferred_element_type=F32)
    iota = lambda shape, d: lax.broadcasted_iota(jnp.int32, shape, d)
    rep4 = lambda a, axis: jnp.concatenate([a] * TG, axis=axis)
    tiles = lambda a: a.reshape(T, side, a.shape[-1])
    group_of = lambda i: (i & (side - 1)) >> GROUP_SHIFT

    own_in = group_of(iota((rows, MXU_TILE), 0)) == iota((rows, MXU_TILE), 1) >> STATE_SHIFT
    for s in range(R):
        pr, pi = pows[R - 1 - s]
        for ref, w in ((winr_ref, pr * bb_re - pi * bb_im), (wini_ref, pr * bb_im + pi * bb_re)):
            ref[:, side * s:side * (s + 1), :] = tiles(jnp.where(own_in, rep4(w, 1), 0.0)).astype(BF16)

    own_out = iota((MXU_TILE, rows), 0) >> STATE_SHIFT == group_of(iota((MXU_TILE, rows), 1))
    for ref, real in ((woutr_ref, True), (wouti_ref, False)):
        per_lag = []
        for j in range(R):
            pr, pi = pows[j + 1]
            w = cr * pr - ci * pi if real else -(cr * pi + ci * pr)
            wt = w.T
            per_lag.append(jnp.where(own_out, rep4(wt, 0), 0.0))
        for n in range(T):
            ref[n] = jnp.concatenate([m[:, side * n:side * (n + 1)] for m in per_lag], axis=1).astype(BF16)

    grouped = lambda a: a.reshape(SSM_GROUPS, SSM_GROUP, SSM_STATE)
    contract = lambda a, b: lax.dot_general(a, b, (((2,), (2,)), ((0,), (0,))), **exact)
    own_dir = group_of(iota((rows, side), 0)) == iota((rows, side), 1) >> GROUP_SHIFT
    by_lag = []
    for d in range(R):
        pr, pi = pows[d]
        er = grouped(pr * bb_re - pi * bb_im)
        ei = grouped(pr * bb_im + pi * bb_re)
        kt = (contract(er, grouped(cr)) - contract(ei, grouped(ci))).reshape(rows, SSM_GROUP)
        by_lag.append(tiles(jnp.where(own_dir, rep4(kt, 1), 0.0)))
    nothing = jnp.zeros_like(by_lag[0])
    for s in range(R):
        kdir_ref[:, side * s:side * (s + 1), :] = jnp.concatenate(
            [by_lag[j - s] if j >= s else nothing for j in range(R)], axis=2).astype(BF16)


def _s5_block_maps(lam_re, lam_im, log_dt, b_re, b_im, c_re, c_im):
    rows = SSM_GROUPS * SSM_GROUP
    rep = lambda a: jnp.repeat(a, SSM_GROUP, axis=0)
    ldt = jnp.broadcast_to(log_dt[:, None], (SSM_GROUPS, SSM_STATE))
    bt = lambda b: b.transpose(0, 2, 1).reshape(rows, SSM_STATE)
    ct = lambda c: c.reshape(rows, SSM_STATE)
    one = jax.ShapeDtypeStruct((rows, SSM_STATE), F32)
    tile = jax.ShapeDtypeStruct((SSM_TILES, MXU_TILE, MXU_TILE), BF16)
    ar, ai, winr, wini, woutr, wouti, direct = pl.pallas_call(
        _s5_prep_kernel,
        out_shape=(one, one, tile, tile, tile, tile, tile),
        compiler_params=pltpu.CompilerParams(vmem_limit_bytes=VMEM_LIMIT),
        name="s5_block_maps",
    )(rep(lam_re), rep(lam_im), rep(ldt), bt(b_re), bt(b_im), ct(c_re), ct(c_im))
    a_re = ar[::SSM_GROUP].reshape(1, SSM_LANES)
    a_im = ai[::SSM_GROUP].reshape(1, SSM_LANES)
    return a_re, a_im, winr, wini, woutr, wouti, direct


def _retention_chunk(length):
    return RET_CHUNK if length % RET_CHUNK == 0 else length


def _log_gamma():
    return np.log(1.0 - 2.0 ** (-5.0 - np.arange(RET_HEADS, dtype=np.float64)))


def _rope_tables(length, pos0):
    half = RET_HEAD_DIM // 2
    inv = ROPE_BASE ** (-np.arange(half, dtype=np.float64) / half)
    ang = (pos0 + np.arange(length, dtype=np.float64))[:, None] * inv[None, :]
    return (np.concatenate([np.cos(ang), np.cos(ang)], axis=1),
            np.concatenate([-np.sin(ang), np.sin(ang)], axis=1))


def _zeta_rows(length):
    chunk = _retention_chunk(length)
    idx = np.arange(chunk, dtype=np.float64)
    zeta = np.exp((chunk - 1.0 - idx)[None, :] * _log_gamma()[:, None])
    return np.concatenate([np.broadcast_to(z[:, None], (chunk, RET_HEAD_DIM)) for z in zeta], axis=1)


def _decay_tables(length, padded):
    chunk = _retention_chunk(length)
    cpad = chunk if padded == length else padded
    lg = _log_gamma()
    idx = np.arange(chunk, dtype=np.float64)
    diff = idx[:, None] - idx[None, :]
    mask = np.where(diff[None] >= 0, np.exp(np.maximum(diff, 0.0)[None] * lg[:, None, None]), 0.0)
    xi = np.exp((idx + 1.0)[None, :] * lg[:, None])
    gamma_c = tuple(float(v) for v in np.exp(chunk * lg))
    mask = np.pad(mask, ((0, 0), (0, cpad - chunk), (0, cpad - chunk)))
    xi = np.broadcast_to(np.pad(xi, ((0, 0), (0, cpad - chunk)))[:, :, None], (RET_HEADS, cpad, RET_HEAD_DIM))
    f = lambda a: jnp.asarray(np.ascontiguousarray(a), dtype=F32)
    return cpad, gamma_c, f(mask), f(xi)


def _ret_lanes(part, head):
    lo = part * RET_WIDTH + head * RET_HEAD_DIM
    return slice(lo, lo + RET_HEAD_DIM)


def _proj_kernel(x_ref, g_ref, w_ref, cc_ref, ss_ref, zeta_ref, *rest, n_cast):
    casts_in, rest = rest[:n_cast], rest[n_cast:]
    u_ref, ret_ref, gate_ref = rest[:3]
    w = w_ref[...]
    if n_cast:
        w = w.astype(BF16)
        for src, dst in zip(casts_in, rest[3:]):
            dst[...] = src[...].astype(BF16)
    h = _rmsnorm(x_ref[...], g_ref[...]).astype(BF16)
    proj = _dot(h, w)
    u_ref[...] = proj[:, :SSM_WIDTH]
    ret_ref[:, 3 * RET_WIDTH:] = proj[:, SSM_WIDTH + 2 * RET_WIDTH:SSM_WIDTH + 3 * RET_WIDTH].astype(BF16)
    gate_ref[...] = proj[:, SSM_WIDTH + 3 * RET_WIDTH:]
    cc = cc_ref[...]
    ss = ss_ref[...]
    rope = lambda a: a * cc + pltpu.roll(a, RET_HEAD_DIM // 2, axis=1) * ss
    for hd in range(RET_HEADS):
        cols = slice(hd * RET_HEAD_DIM, (hd + 1) * RET_HEAD_DIM)
        q = rope(proj[:, SSM_WIDTH + hd * RET_HEAD_DIM:SSM_WIDTH + (hd + 1) * RET_HEAD_DIM])
        k = rope(proj[:, SSM_WIDTH + RET_WIDTH + hd * RET_HEAD_DIM:SSM_WIDTH + RET_WIDTH + (hd + 1) * RET_HEAD_DIM])
        ret_ref[:, _ret_lanes(0, hd)] = (q * (RET_HEAD_DIM ** -0.5)).astype(BF16)
        ret_ref[:, _ret_lanes(1, hd)] = k.astype(BF16)
        ret_ref[:, _ret_lanes(2, hd)] = (k * zeta_ref[:, cols]).astype(BF16)


def _project(x, g_mix, w_in, *, length, pos0, casts=()):
    t = x.shape[0]
    tm = min(t, ROWS_LARGE)
    steps = t // tm
    chunk = _retention_chunk(length)
    casts = (w_in, *casts) if casts else ()
    if tm % chunk or (length % tm and tm % length):
        raise ValueError("token tiles must hold whole retention chunks of whole or repeated sequences")
    cc, ss = _rope_tables(length, pos0)
    if length < tm:
        cc, ss = np.tile(cc, (tm // length, 1)), np.tile(ss, (tm // length, 1))
    tab_blocks = cc.shape[0] // tm
    zeta = np.tile(_zeta_rows(length), (tm // chunk, 1))
    const = lambda a: jnp.asarray(np.ascontiguousarray(a), dtype=F32)
    row = lambda w: pl.BlockSpec((tm, w), lambda i: (i, 0))
    tab = pl.BlockSpec((tm, RET_HEAD_DIM), lambda i: (i % tab_blocks, 0))
    f = jax.ShapeDtypeStruct((t, SSM_WIDTH), F32)
    h = jax.ShapeDtypeStruct((t, 4 * RET_WIDTH), BF16)
    in_specs = [row(D_MODEL), _full((1, D_MODEL)), _full((D_MODEL, PROJ_WIDTH)), tab, tab, _full((tm, RET_WIDTH))]
    out_specs = [row(SSM_WIDTH), row(4 * RET_WIDTH), row(RET_WIDTH)]
    out_shape = [f, h, f]
    if casts:
        for c in casts:
            rows, rem = divmod(c.shape[0], steps)
            if rem or rows % BF16_ROWS:
                raise ValueError("side-cast weights must split into whole bf16 tiles per grid step")
            blk = pl.BlockSpec((rows, c.shape[1]), lambda i: (i, 0))
            in_specs.append(blk)
            out_specs.append(blk)
            out_shape.append(jax.ShapeDtypeStruct(c.shape, BF16))
    outs = pl.pallas_call(
        functools.partial(_proj_kernel, n_cast=len(casts)),
        grid=(steps,),
        in_specs=in_specs,
        out_specs=out_specs,
        out_shape=out_shape,
        compiler_params=_params("arbitrary" if casts else "parallel", fuse=_fuse(len(in_specs), 0)),
        name="in_proj",
    )(x, g_mix, w_in, const(cc), const(ss), const(zeta), *casts)
    return outs[:3], outs[3:]


def _s5_block_kernel(u_ref, h0r_ref, h0i_ref, ar_ref, ai_ref, winr_ref, wini_ref, woutr_ref, wouti_ref,
                     kdir_ref, d_ref, wglu_ref, out_ref, hr_ref, hi_ref, vr_scr, vi_scr, xr_scr, xi_scr,
                     *, batch, steps, pair, width):
    R, T = SSM_BLOCK, SSM_TILES
    chan = SSM_TILE_CHANNELS
    blocks = steps // R
    rows, brow = steps * batch, blocks * batch

    @pl.when(pl.program_id(0) == 0)
    def _():
        hr_ref[...] = h0r_ref[...]
        hi_ref[...] = h0i_ref[...]

    u = jnp.swapaxes(u_ref[...], 0, 1).reshape(rows, SSM_WIDTH)
    u4 = u.reshape(blocks, R, batch, SSM_WIDTH)
    lag = [u4[:, s].reshape(brow, SSM_WIDTH) for s in range(R)]
    direct = []
    for n in range(T):
        un = jnp.concatenate([lag[s][:, chan * n:chan * (n + 1)] for s in range(R)], axis=1).astype(BF16)
        lanes = slice(MXU_TILE * n, MXU_TILE * (n + 1))
        vr_scr[:, lanes] = _dot(un, winr_ref[n])
        vi_scr[:, lanes] = _dot(un, wini_ref[n])
        direct.append(_dot(un, kdir_ref[n]))

    for j in range(SSM_LANES // width):
        glob = slice(j * width, (j + 1) * width)
        ar = jnp.broadcast_to(ar_ref[:, glob], (batch, width))
        ai = jnp.broadcast_to(ai_ref[:, glob], (batch, width))
        xr = hr_ref[:, glob]
        xi = hi_ref[:, glob]
        for i in range(blocks // pair):
            prev_r, prev_i = [], []
            for s in range(pair):
                at = slice((i * pair + s) * batch, (i * pair + s + 1) * batch)
                prev_r.append(xr)
                prev_i.append(xi)
                xr, xi = ar * xr - ai * xi + vr_scr[at, glob], ar * xi + ai * xr + vi_scr[at, glob]
            blk = slice(i * pair * batch, (i + 1) * pair * batch)
            xr_scr[blk, glob] = jnp.concatenate(prev_r, axis=0).astype(BF16)
            xi_scr[blk, glob] = jnp.concatenate(prev_i, axis=0).astype(BF16)
        hr_ref[:, glob] = xr
        hi_ref[:, glob] = xi

    y4 = []
    for n in range(T):
        lanes = slice(MXU_TILE * n, MXU_TILE * (n + 1))
        y4.append(_dot(xr_scr[:, lanes], woutr_ref[n]) + _dot(xi_scr[:, lanes], wouti_ref[n]) + direct[n])
    per_lag = [jnp.concatenate([y4[n][:, chan * j:chan * (j + 1)] for n in range(T)], axis=1)
               .reshape(blocks, batch, SSM_WIDTH) for j in range(R)]
    y = jnp.stack(per_lag, axis=1).reshape(rows, SSM_WIDTH) + d_ref[...] * u
    z = _gelu_tanh(y)
    out = z * _sigmoid(_dot(z.astype(BF16), wglu_ref[...]))
    out_ref[...] = jnp.swapaxes(out.reshape(steps, batch, SSM_WIDTH), 0, 1).astype(BF16)


def _s5_mixer(u, h0_re, h0_im, maps, d_skip, w_glu, *, batch, length):
    a_re, a_im, winr, wini, woutr, wouti, direct = maps
    steps = min(length, S5_STEPS)
    if steps % SSM_BLOCK or length % steps:
        raise ValueError("sequence length must be a multiple of the S5 block")
    blocks = steps // SSM_BLOCK
    pair = max(1, min(blocks, BF16_ROWS // batch))
    if (pair * batch) % BF16_ROWS or blocks % pair:
        raise ValueError("S5 state stores need whole bf16 tiles")
    width = SSM_LANES // 2 if batch <= 8 else LANES
    brow = blocks * batch
    kern = functools.partial(_s5_block_kernel, batch=batch, steps=steps, pair=pair, width=width)
    st = jax.ShapeDtypeStruct((batch, SSM_LANES), F32)
    tile = _full((SSM_TILES, MXU_TILE, MXU_TILE))
    return pl.pallas_call(
        kern,
        grid=(length // steps,),
        in_specs=[pl.BlockSpec((batch, steps, SSM_WIDTH), lambda i: (0, i, 0)),
                  _full((batch, SSM_LANES)), _full((batch, SSM_LANES)),
                  _full((1, SSM_LANES)), _full((1, SSM_LANES)),
                  tile, tile, tile, tile, tile,
                  _full((1, SSM_WIDTH)), _full((SSM_WIDTH, SSM_WIDTH))],
        out_specs=[pl.BlockSpec((batch, steps, SSM_WIDTH), lambda i: (0, i, 0)),
                   _full((batch, SSM_LANES)), _full((batch, SSM_LANES))],
        out_shape=(jax.ShapeDtypeStruct((batch, length, SSM_WIDTH), BF16), st, st),
        scratch_shapes=[pltpu.VMEM((brow, SSM_LANES), F32), pltpu.VMEM((brow, SSM_LANES), F32),
                        pltpu.VMEM((brow, SSM_LANES), BF16), pltpu.VMEM((brow, SSM_LANES), BF16)],
        compiler_params=_params("arbitrary", fuse=_fuse(12, 0, 1, 2, 11)),
        name="s5_mixer",
    )(u, h0_re, h0_im, a_re, a_im, winr, wini, woutr, wouti, direct, d_skip, w_glu)


def _ret_kernel(p_ref, g_ref, mask_ref, xi_ref, gain_ref, s0_ref, o_ref, s_ref,
                *, bb, gamma_c, unroll, chunk, chunks):
    @pl.when(pl.program_id(1) == 0)
    def _():
        s_ref[...] = s0_ref[...]

    heads = range(RET_HEADS)
    cols = [slice(h * RET_HEAD_DIM, (h + 1) * RET_HEAD_DIM) for h in heads]

    def one_chunk(b, rows):
        qb = [p_ref[b, rows, _ret_lanes(0, h)] for h in heads]
        vb = [p_ref[b, rows, _ret_lanes(3, h)] for h in heads]
        s_prev = [s_ref[b, h] for h in heads]
        scores = [(_dot_nt(qb[h], p_ref[b, rows, _ret_lanes(1, h)]) * mask_ref[h]).astype(BF16) for h in heads]
        cross = [_dot(qb[h], s_prev[h].astype(BF16)) * xi_ref[h] for h in heads]
        kv = [_dot_tn(p_ref[b, rows, _ret_lanes(2, h)], vb[h]) for h in heads]
        inner = [_dot(scores[h], vb[h]) for h in heads]
        outs = []
        for h in heads:
            s_ref[b, h] = s_prev[h] * gamma_c[h] + kv[h]
            o = inner[h] + cross[h]
            mu = jnp.mean(o, axis=-1, keepdims=True)
            oc = o - mu
            var = jnp.mean(oc * oc, axis=-1, keepdims=True)
            on = oc * lax.rsqrt(var + EPS) * gain_ref[:, cols[h]]
            gate = g_ref[b, rows, cols[h]]
            outs.append((gate * _sigmoid(gate) * on).astype(BF16))
        o_ref[b, rows, :] = jnp.concatenate(outs, axis=1)

    def per_batch(b, carry):
        for c in range(chunks):
            one_chunk(b, slice(c * chunk, (c + 1) * chunk))
        return carry

    if unroll == bb:
        for b in range(bb):
            per_batch(b, 0)
    else:
        lax.fori_loop(0, bb, per_batch, 0, unroll=unroll)


def _retention(packed, g, s0, gn_gain):
    batch, length, _ = g.shape
    padded = pl.cdiv(length, BF16_ROWS) * BF16_ROWS
    chunk, gamma_c, mask, xi = _decay_tables(length, padded)
    if padded != length:
        pad = lambda a: jnp.pad(a, ((0, 0), (0, padded - length), (0, 0)))
        packed, g = pad(packed), pad(g)
    bb, unroll = (8, 2) if chunk > BF16_ROWS else (16, 16)
    chunks = 2 if (padded // chunk) % 2 == 0 else 1
    tok = pl.BlockSpec((bb, chunks * chunk, RET_WIDTH), lambda i, c: (i, c, 0))
    state = pl.BlockSpec((bb, RET_HEADS, RET_HEAD_DIM, RET_HEAD_DIM), lambda i, c: (i, 0, 0, 0))
    out, s_new = pl.pallas_call(
        functools.partial(_ret_kernel, bb=bb, gamma_c=gamma_c, unroll=unroll, chunk=chunk, chunks=chunks),
        grid=(batch // bb, padded // (chunks * chunk)),
        in_specs=[pl.BlockSpec((bb, chunks * chunk, 4 * RET_WIDTH), lambda i, c: (i, c, 0)), tok,
                  _full((RET_HEADS, chunk, chunk)), _full((RET_HEADS, chunk, RET_HEAD_DIM)),
                  _full((1, RET_WIDTH)), state],
        out_specs=[tok, state],
        out_shape=(jax.ShapeDtypeStruct((batch, padded, RET_WIDTH), BF16),
                   jax.ShapeDtypeStruct(s0.shape, F32)),
        compiler_params=_params("parallel", "arbitrary", fuse=_fuse(6, 0, 1)),
        name="retention",
    )(packed, g, mask, xi, gn_gain, s0)
    return out[:, :length], s_new


def _mixer_out(x_ref, ssm_ref, ret_ref, wout_ref, rows=slice(None)):
    return (x_ref[rows, :] + _dot(ssm_ref[rows, :], wout_ref[:SSM_WIDTH, :].astype(BF16))
            + _dot(ret_ref[rows, :], wout_ref[SSM_WIDTH:, :].astype(BF16)))


def _query(x1, g_ref, wq_ref):
    return (_dot(_rmsnorm(x1, g_ref[...]).astype(BF16), wq_ref[...].astype(BF16)) * MEM_SCALE).astype(BF16)


def _outq_kernel(x_ref, ssm_ref, ret_ref, wout_ref, g_ref, wq_ref, x1_ref, q_ref):
    x1 = _mixer_out(x_ref, ssm_ref, ret_ref, wout_ref)
    x1_ref[...] = x1
    q_ref[...] = _query(x1, g_ref, wq_ref)


def _out_and_query(x, ssm, ret, w_out, g_xattn, w_mq):
    t = x.shape[0]
    tm = min(t, ROWS_SMALL)
    row = lambda w: pl.BlockSpec((tm, w), lambda i: (i, 0))
    return pl.pallas_call(
        _outq_kernel,
        grid=(t // tm,),
        in_specs=[row(D_MODEL), row(SSM_WIDTH), row(RET_WIDTH), _full((D_MODEL, D_MODEL)),
                  _full((1, D_MODEL)), _full((D_MODEL, D_MODEL))],
        out_specs=[row(D_MODEL), row(D_MODEL)],
        out_shape=(jax.ShapeDtypeStruct((t, D_MODEL), F32), jax.ShapeDtypeStruct((t, D_MODEL), BF16)),
        compiler_params=_params("parallel", fuse=_fuse(6, 0, 3, 5)),
        name="out_proj_query",
    )(x, ssm, ret, w_out, g_xattn, w_mq)


def _cache_row_order(x):
    halves = MEM_HEAD_DIM // LANES
    tiles = [x[:, (h * halves + d) * LANES:(h * halves + d + 1) * LANES]
             for d in range(halves) for h in range(MEM_HEADS)]
    return jnp.swapaxes(jnp.stack(tiles, axis=0), 0, 1).reshape(x.shape[0] * CACHE_SUB, LANES)


MEMKV_SEQS = 2


def _memkv_kernel(m_ref, g_ref, wk_ref, wv_ref, k_ref, v_ref, kb_ref, vb_ref):
    seqs = m_ref.shape[0]
    m = _rmsnorm(m_ref[...].reshape(seqs * MEM_LEN, D_MODEL), g_ref[...]).astype(BF16)
    k = _dot(m, wk_ref[...])
    v = _dot(m, wv_ref[...])
    k_ref[...] = _cache_row_order(k).reshape(k_ref.shape)
    v_ref[...] = _cache_row_order(v).reshape(v_ref.shape)
    kb_ref[...] = k.astype(BF16).reshape(kb_ref.shape)
    vb_ref[...] = v.astype(BF16).reshape(vb_ref.shape)


def _memory_kv(mem, g_mem, w_mk, w_mv):
    batch = mem.shape[0]
    seqs = MEMKV_SEQS if batch % MEMKV_SEQS == 0 else 1
    tok = pl.BlockSpec((seqs, MEM_LEN, D_MODEL), lambda i: (i, 0, 0))
    cache = pl.BlockSpec((seqs, CACHE_ROWS, LANES), lambda i: (i, 0, 0))
    f = jax.ShapeDtypeStruct((batch, CACHE_ROWS, LANES), F32)
    h = jax.ShapeDtypeStruct((batch, MEM_LEN, D_MODEL), BF16)
    return pl.pallas_call(
        _memkv_kernel,
        grid=(batch // seqs,),
        in_specs=[tok, _full((1, D_MODEL)), _full((D_MODEL, D_MODEL)), _full((D_MODEL, D_MODEL))],
        out_specs=[cache, cache, tok, tok],
        out_shape=(f, f, h, h),
        compiler_params=_params("parallel", fuse=_fuse(4, 2, 3)),
        name="memory_kv",
    )(mem, g_mem, w_mk, w_mv)


ATTN_ROWS = 512


def _outattn_kernel(x_ref, ssm_ref, ret_ref, wout_ref, g_ref, wq_ref, k_ref, v_ref, x1_ref, o_ref):
    cols = [slice(h * MEM_HEAD_DIM, (h + 1) * MEM_HEAD_DIM) for h in range(MEM_HEADS)]
    groups = [slice(r, r + ATTN_ROWS) for r in range(0, x_ref.shape[0], ATTN_ROWS)]

    def scores_of(rows):
        x1 = _mixer_out(x_ref, ssm_ref, ret_ref, wout_ref, rows)
        x1_ref[rows, :] = x1
        q = _query(x1, g_ref, wq_ref)
        return [_dot_nt(q[:, c], k_ref[0, :, c]) for c in cols]

    def attend(rows, scores):
        outs = []
        for s, c in zip(scores, cols):
            e = jnp.exp(s - jnp.max(s, axis=-1, keepdims=True))
            p = (e / jnp.sum(e, axis=-1, keepdims=True)).astype(BF16)
            outs.append(_dot(p, v_ref[0, :, c]).astype(BF16))
        o_ref[rows, :] = jnp.concatenate(outs, axis=1)

    pending = None
    for rows in groups:
        scores = scores_of(rows)
        if pending is not None:
            attend(*pending)
        pending = (rows, scores)
    attend(*pending)


def _out_and_attention(x, ssm, ret, w_out, g_xattn, w_mq, mk, mv, *, length):
    t = x.shape[0]
    tm = min(length, ROWS_LARGE)
    per_seq = length // tm
    row = lambda w: pl.BlockSpec((tm, w), lambda i: (i, 0))
    mem = pl.BlockSpec((1, MEM_LEN, D_MODEL), lambda i: (i // per_seq, 0, 0))
    return pl.pallas_call(
        _outattn_kernel,
        grid=(t // tm,),
        in_specs=[row(D_MODEL), row(SSM_WIDTH), row(RET_WIDTH), _full((D_MODEL, D_MODEL)),
                  _full((1, D_MODEL)), _full((D_MODEL, D_MODEL)), mem, mem],
        out_specs=[row(D_MODEL), row(D_MODEL)],
        out_shape=(jax.ShapeDtypeStruct((t, D_MODEL), F32), jax.ShapeDtypeStruct((t, D_MODEL), BF16)),
        compiler_params=_params("parallel", fuse=_fuse(8, 3, 5)),
        name="out_proj_attention",
    )(x, ssm, ret, w_out, g_xattn, w_mq, mk, mv)


def _cached_attention(q_ref, k_ref, v_ref, o_ref, *, bb, length):
    hit = (lax.broadcasted_iota(jnp.int32, (CACHE_SUB, CACHE_ROWS), 0)
           == lax.broadcasted_iota(jnp.int32, (CACHE_SUB, CACHE_ROWS), 1) % CACHE_SUB)
    first_half = lax.broadcasted_iota(jnp.int32, (length, LANES), 1) % CACHE_SUB < MEM_HEADS
    tiles = CACHE_ROWS // LANES

    def class_reduce(x, op):
        shift = CACHE_SUB
        while shift < LANES:
            x = op(x, pltpu.roll(x, shift, axis=1))
            shift *= 2
        return x

    def softmax_rows(r):
        z = jnp.concatenate(
            [jnp.sum(jnp.where(hit, r[CACHE_SUB * t:CACHE_SUB * (t + 1)], 0.0), axis=0, keepdims=True)
             for t in range(length)], axis=0)
        parts = []
        for i in range(tiles):
            zi = z[:, LANES * i:LANES * (i + 1)]
            parts.append(zi + pltpu.roll(zi, LANES - MEM_HEADS, axis=1))
        mx = parts[0]
        for pi in parts[1:]:
            mx = jnp.maximum(mx, pi)
        mx = class_reduce(mx, jnp.maximum)
        es = [jnp.exp(pi - mx) for pi in parts]
        tot = es[0]
        for ei in es[1:]:
            tot = tot + ei
        tot = class_reduce(tot, jnp.add)
        ps = []
        for ei in es:
            pi = ei / tot
            ps.append(jnp.where(first_half, pi, pltpu.roll(pi, MEM_HEADS, axis=1)))
        p = jnp.concatenate(ps, axis=1)
        return jnp.concatenate(
            [jnp.where(hit, jnp.broadcast_to(p[t:t + 1], (CACHE_SUB, CACHE_ROWS)), 0.0)
             for t in range(length)], axis=0).astype(BF16)

    def start():
        return [_dot_nt(q_ref[b], k_ref[b].astype(BF16)) for b in range(bb)]

    def finish(scores):
        probs = [softmax_rows(r) for r in scores]
        for b in range(bb):
            o_ref[b] = _dot(probs[b], v_ref[b].astype(BF16)).astype(BF16)

    return start, finish


def _to_cache_rows(a, lead):
    halves = MEM_HEAD_DIM // LANES
    batch = a.size // (lead * D_MODEL)
    a = a.reshape(batch, lead, MEM_HEADS, halves, LANES).transpose(0, 1, 3, 2, 4)
    return a.reshape(batch, lead * CACHE_SUB, LANES)


def _from_cache_rows(a, lead):
    halves = MEM_HEAD_DIM // LANES
    batch = a.shape[0]
    a = a.reshape(batch, lead, halves, MEM_HEADS, LANES).transpose(0, 1, 3, 2, 4)
    return a.reshape(batch, lead, D_MODEL)


MLP_CHUNK = 1024


def _post_kernel(x1_ref, o_ref, wo_ref, gm_ref, wup_ref, wdn_ref, gf_ref, *rest, side_bb, side_len):
    if side_bb:
        qs_ref, ck_ref, cv_ref, y_ref, os_ref = rest
    else:
        (y_ref,) = rest
    x2 = x1_ref[...] + _dot(o_ref[...], wo_ref[...])
    if side_bb:
        side_start, side_finish = _cached_attention(qs_ref, ck_ref, cv_ref, os_ref, bb=side_bb, length=side_len)
        side_scores = side_start()
    h = _rmsnorm(x2, gm_ref[...]).astype(BF16)
    chunks = [slice(c, c + MLP_CHUNK) for c in range(0, D_FF, MLP_CHUNK)]
    acc = x2
    up = _dot(h, wup_ref[:, chunks[0]])
    for c, cols in enumerate(chunks):
        nxt = _dot(h, wup_ref[:, chunks[c + 1]]) if c + 1 < len(chunks) else None
        if side_bb and c == len(chunks) - 1:
            side_finish(side_scores)
        a = jnp.maximum(up, 0.0)
        acc = acc + _dot((a * a).astype(BF16), wdn_ref[cols, :])
        up = nxt
    y_ref[...] = _rmsnorm(acc, gf_ref[...])


def _attn_out_mlp(x1, o, w_mo, g_mlp, w_up, w_down, g_final, side=None):
    t = x1.shape[0]
    tm = min(t, ROWS_SMALL)
    steps = t // tm
    row = pl.BlockSpec((tm, D_MODEL), lambda i: (i, 0))
    in_specs = [row, row, _full((D_MODEL, D_MODEL)), _full((1, D_MODEL)),
                _full((D_MODEL, D_FF)), _full((D_FF, D_MODEL)), _full((1, D_MODEL))]
    out_specs = [row]
    out_shape = [jax.ShapeDtypeStruct((t, D_MODEL), F32)]
    args = [x1, o, w_mo, g_mlp, w_up, w_down, g_final]
    side_bb = side_len = 0
    if side is not None:
        q_rows = side[0]
        side_bb, rem = divmod(q_rows.shape[0], steps)
        if rem or not side_bb:
            raise ValueError("side attention sequences must spread evenly over the grid steps")
        side_len = q_rows.shape[1] // CACHE_SUB
        blk = lambda r: pl.BlockSpec((side_bb, r, LANES), lambda i: (i, 0, 0))
        in_specs += [blk(q_rows.shape[1]), blk(CACHE_ROWS), blk(CACHE_ROWS)]
        out_specs.append(blk(q_rows.shape[1]))
        out_shape.append(jax.ShapeDtypeStruct(q_rows.shape, BF16))
        args += list(side)
    outs = pl.pallas_call(
        functools.partial(_post_kernel, side_bb=side_bb, side_len=side_len),
        grid=(steps,),
        in_specs=in_specs,
        out_specs=out_specs,
        out_shape=out_shape,
        compiler_params=_params("parallel", fuse=_fuse(len(in_specs), 1, 2, 7)),
        name="attn_out_mlp",
    )(*args)
    return outs if side is not None else outs[0]


def _mixers(proj, batch, length, s5_re, s5_im, ret_s, p):
    tokens = batch * length
    u, packed, g = proj
    ssm, s5_re_new, s5_im_new = _s5_mixer(
        u.reshape(batch, length, SSM_WIDTH), s5_re.reshape(batch, SSM_LANES), s5_im.reshape(batch, SSM_LANES),
        p["s5_maps"], p["d_skip"], p["w_glu"], batch=batch, length=length)
    tok3 = lambda a: a.reshape(batch, length, RET_WIDTH)
    ret, ret_new = _retention(packed.reshape(batch, length, 4 * RET_WIDTH), tok3(g), ret_s, p["ret_gn"])
    states = (s5_re_new.reshape(1, batch, SSM_GROUPS, SSM_STATE),
              s5_im_new.reshape(1, batch, SSM_GROUPS, SSM_STATE), ret_new[None])
    return ssm.reshape(tokens, SSM_WIDTH), ret.reshape(tokens, RET_WIDTH), states


def kernel(x_prompt, x_sample, mem_prompt, state_s5_re, state_s5_im, state_ret, cache_mem_k, cache_mem_v, g_mix, w_in, lam_re, lam_im, log_dt, b_re, b_im, c_re, c_im, d_skip, w_glu, ret_gn, w_out, g_xattn, g_mem, w_mq, w_mk, w_mv, w_mo, g_mlp, w_up, w_down, g_final):
    if g_mix.shape[0] != 1:
        raise ValueError("this kernel implements the single-layer configuration")
    bp, lp, _ = x_prompt.shape
    bs, ls, _ = x_sample.shape
    vec = lambda a: a.reshape(1, -1).astype(F32)
    xp = x_prompt.reshape(bp * lp, D_MODEL)
    xs = x_sample.reshape(bs * ls, D_MODEL)
    proj_p, (w_in_b, w_up_b, w_down_b) = _project(
        xp, vec(g_mix[0]), w_in[0], length=lp, pos0=0.0, casts=(w_up[0], w_down[0]))
    w_mo_b, w_mk_b, w_mv_b, w_glu_b = (w[0].astype(BF16) for w in (w_mo, w_mk, w_mv, w_glu))
    proj_s, _ = _project(xs, vec(g_mix[0]), w_in_b, length=ls, pos0=float(PAST_LEN))
    p = dict(
        s5_maps=_s5_block_maps(lam_re[0], lam_im[0], log_dt[0], b_re[0], b_im[0], c_re[0], c_im[0]),
        d_skip=vec(d_skip[0]), w_glu=w_glu_b, ret_gn=vec(ret_gn[0]))
    g_x = vec(g_xattn[0])
    mlp = (w_mo_b, vec(g_mlp[0]), w_up_b, w_down_b, vec(g_final))

    ssm_s, ret_s, states_s = _mixers(proj_s, bs, ls, state_s5_re[0], state_s5_im[0], state_ret[0], p)
    x1_s, q_s = _out_and_query(xs, ssm_s, ret_s, w_out[0], g_x, w_mq[0])
    if (ls * CACHE_SUB) % BF16_ROWS:
        raise ValueError("cached attention needs whole bf16 tiles of query rows")
    side = (_to_cache_rows(q_s, ls), _to_cache_rows(cache_mem_k, MEM_LEN), _to_cache_rows(cache_mem_v, MEM_LEN))

    mk_rows, mv_rows, mk_p, mv_p = _memory_kv(mem_prompt, vec(g_mem[0]), w_mk_b, w_mv_b)
    zs = jnp.zeros((bp, SSM_GROUPS, SSM_STATE), F32)
    zr = jnp.zeros((bp, RET_HEADS, RET_HEAD_DIM, RET_HEAD_DIM), F32)
    ssm_p, ret_p, states_p = _mixers(proj_p, bp, lp, zs, zs, zr, p)
    x1_p, o_p = _out_and_attention(xp, ssm_p, ret_p, w_out[0], g_x, w_mq[0], mk_p, mv_p, length=lp)
    y_p, o_s = _attn_out_mlp(x1_p, o_p, *mlp, side=side)
    y_s = _attn_out_mlp(x1_s, _from_cache_rows(o_s, ls).reshape(bs * ls, D_MODEL), *mlp)

    kv5 = lambda a: _from_cache_rows(a, MEM_LEN).reshape(1, bp, MEM_LEN, MEM_HEADS, MEM_HEAD_DIM)
    return (y_p.reshape(bp, lp, D_MODEL), y_s.reshape(bs, ls, D_MODEL), *states_p, kv5(mk_rows), kv5(mv_rows),
            *states_s)
```

```python
import functools
import math

import numpy as np
import jax
import jax.numpy as jnp
from jax import lax
from jax.experimental import pallas as pl
from jax.experimental.pallas import tpu as pltpu

F32 = jnp.float32
BF16 = jnp.bfloat16

D_MODEL = 1024
SSM_WIDTH = 512
SSM_GROUP = 16
SSM_GROUPS = 32
SSM_STATE = 64
SSM_LANES = SSM_GROUPS * SSM_STATE
RET_WIDTH = 512
RET_HEADS = 4
RET_HEAD_DIM = 128
RET_CHUNK = 128
ROPE_BASE = 10000.0
MEM_LEN = 256
MEM_HEADS = 4
MEM_HEAD_DIM = 256
MEM_SCALE = MEM_HEAD_DIM ** -0.5
D_FF = 4096
PROJ_WIDTH = SSM_WIDTH + 4 * RET_WIDTH
EPS = 1e-6
PAST_LEN = 16384

LANES = 128
MXU_TILE = 256
BF16_ROWS = 16
VMEM_LIMIT = 56 * 1024 * 1024

ROWS_LARGE = 1024
ROWS_SMALL = 512
S5_STEPS = 256

CACHE_ROWS = MEM_LEN * MEM_HEADS * MEM_HEAD_DIM // LANES
CACHE_SUB = MEM_HEADS * MEM_HEAD_DIM // LANES

assert math.frexp(MEM_SCALE)[0] == 0.5


def _dot(a, b):
    return jnp.dot(a, b, preferred_element_type=F32)


def _dot_nt(a, b):
    return lax.dot_general(a, b, (((1,), (1,)), ((), ())), preferred_element_type=F32)


def _dot_tn(a, b):
    return lax.dot_general(a, b, (((0,), (0,)), ((), ())), preferred_element_type=F32)


def _rmsnorm(x, g):
    return x * lax.rsqrt(jnp.mean(x * x, axis=-1, keepdims=True) + EPS) * g


def _sigmoid(x):
    return 1.0 / (1.0 + jnp.exp(-x))


def _gelu_tanh(x):
    c = math.sqrt(2.0 / math.pi)
    return x * (0.5 * jnp.tanh(x * (c + (c * 0.044715) * (x * x))) + 0.5)


def _params(*sem, fuse=None):
    return pltpu.CompilerParams(dimension_semantics=sem, vmem_limit_bytes=VMEM_LIMIT, allow_input_fusion=fuse)


def _fuse(n_inputs, *which):
    return [i in which for i in range(n_inputs)]


def _full(shape):
    return pl.BlockSpec(shape, lambda *_: (0,) * len(shape))


SSM_BLOCK = 4
SSM_TILE_GROUPS = MXU_TILE // SSM_STATE
SSM_TILES = SSM_GROUPS // SSM_TILE_GROUPS
SSM_TILE_CHANNELS = SSM_TILE_GROUPS * SSM_GROUP
GROUP_SHIFT = SSM_GROUP.bit_length() - 1
STATE_SHIFT = SSM_STATE.bit_length() - 1

assert SSM_BLOCK * SSM_TILE_CHANNELS == MXU_TILE and SSM_TILE_CHANNELS == SSM_STATE


def _s5_prep_kernel(lr_ref, li_ref, ldt_ref, br_ref, bi_ref, cr_ref, ci_ref,
                    ar_ref, ai_ref, winr_ref, wini_ref, woutr_ref, wouti_ref, kdir_ref):
    lr = lr_ref[...]
    li = li_ref[...]
    dt = jnp.exp(ldt_ref[...])
    mag = jnp.exp(lr * dt)
    ab_re = mag * jnp.cos(li * dt)
    ab_im = mag * jnp.sin(li * dt)
    den = lr * lr + li * li
    f_re = ((ab_re - 1.0) * lr + ab_im * li) / den
    f_im = (ab_im * lr - (ab_re - 1.0) * li) / den
    bb_re = f_re * br_ref[...] - f_im * bi_ref[...]
    bb_im = f_re * bi_ref[...] + f_im * br_ref[...]
    cr = cr_ref[...]
    ci = ci_ref[...]
    pows = [(jnp.ones_like(ab_re), jnp.zeros_like(ab_re))]
    for _ in range(SSM_BLOCK):
        pr, pi = pows[-1]
        pows.append((pr * ab_re - pi * ab_im, pr * ab_im + pi * ab_re))
    ar_ref[...], ai_ref[...] = pows[SSM_BLOCK]

    R, T, TG = SSM_BLOCK, SSM_TILES, SSM_TILE_GROUPS
    rows = SSM_GROUPS * SSM_GROUP
    side = SSM_TILE_CHANNELS
    exact = dict(precision=lax.Precision.HIGHEST, preferred_element_type=F32)
    iota = lambda shape, d: lax.broadcasted_iota(jnp.int32, shape, d)
    rep4 = lambda a, axis: jnp.concatenate([a] * TG, axis=axis)
    tiles = lambda a: a.reshape(T, side, a.shape[-1])
    group_of = lambda i: (i & (side - 1)) >> GROUP_SHIFT

    own_in = group_of(iota((rows, MXU_TILE), 0)) == iota((rows, MXU_TILE), 1) >> STATE_SHIFT
    for s in range(R):
        pr, pi = pows[R - 1 - s]
        for ref, w in ((winr_ref, pr * bb_re - pi * bb_im), (wini_ref, pr * bb_im + pi * bb_re)):
            ref[:, side * s:side * (s + 1), :] = tiles(jnp.where(own_in, rep4(w, 1), 0.0)).astype(BF16)

    own_out = iota((MXU_TILE, rows), 0) >> STATE_SHIFT == group_of(iota((MXU_TILE, rows), 1))
    for ref, real in ((woutr_ref, True), (wouti_ref, False)):
        per_lag = []
        for j in range(R):
            pr, pi = pows[j + 1]
            w = cr * pr - ci * pi if real else -(cr * pi + ci * pr)
            wt = w.T
            per_lag.append(jnp.where(own_out, rep4(wt, 0), 0.0))
        for n in range(T):
            ref[n] = jnp.concatenate([m[:, side * n:side * (n + 1)] for m in per_lag], axis=1).astype(BF16)

    grouped = lambda a: a.reshape(SSM_GROUPS, SSM_GROUP, SSM_STATE)
    contract = lambda a, b: lax.dot_general(a, b, (((2,), (2,)), ((0,), (0,))), **exact)
    own_dir = group_of(iota((rows, side), 0)) == iota((rows, side), 1) >> GROUP_SHIFT
    by_lag = []
    for d in range(R):
        pr, pi = pows[d]
        er = grouped(pr * bb_re - pi * bb_im)
        ei = grouped(pr * bb_im + pi * bb_re)
        kt = (contract(er, grouped(cr)) - contract(ei, grouped(ci))).reshape(rows, SSM_GROUP)
        by_lag.append(tiles(jnp.where(own_dir, rep4(kt, 1), 0.0)))
    nothing = jnp.zeros_like(by_lag[0])
    for s in range(R):
        kdir_ref[:, side * s:side * (s + 1), :] = jnp.concatenate(
            [by_lag[j - s] if j >= s else nothing for j in range(R)], axis=2).astype(BF16)


def _s5_block_maps(lam_re, lam_im, log_dt, b_re, b_im, c_re, c_im):
    rows = SSM_GROUPS * SSM_GROUP
    rep = lambda a: jnp.repeat(a, SSM_GROUP, axis=0)
    ldt = jnp.broadcast_to(log_dt[:, None], (SSM_GROUPS, SSM_STATE))
    bt = lambda b: b.transpose(0, 2, 1).reshape(rows, SSM_STATE)
    ct = lambda c: c.reshape(rows, SSM_STATE)
    one = jax.ShapeDtypeStruct((rows, SSM_STATE), F32)
    tile = jax.ShapeDtypeStruct((SSM_TILES, MXU_TILE, MXU_TILE), BF16)
    ar, ai, winr, wini, woutr, wouti, direct = pl.pallas_call(
        _s5_prep_kernel,
        out_shape=(one, one, tile, tile, tile, tile, tile),
        compiler_params=pltpu.CompilerParams(vmem_limit_bytes=VMEM_LIMIT),
        name="s5_block_maps",
    )(rep(lam_re), rep(lam_im), rep(ldt), bt(b_re), bt(b_im), ct(c_re), ct(c_im))
    a_re = ar[::SSM_GROUP].reshape(1, SSM_LANES)
    a_im = ai[::SSM_GROUP].reshape(1, SSM_LANES)
    return a_re, a_im, winr, wini, woutr, wouti, direct


def _retention_chunk(length):
    return RET_CHUNK if length % RET_CHUNK == 0 else length


def _log_gamma():
    return np.log(1.0 - 2.0 ** (-5.0 - np.arange(RET_HEADS, dtype=np.float64)))


def _rope_tables(length, pos0):
    half = RET_HEAD_DIM // 2
    inv = ROPE_BASE ** (-np.arange(half, dtype=np.float64) / half)
    ang = (pos0 + np.arange(length, dtype=np.float64))[:, None] * inv[None, :]
    return (np.concatenate([np.cos(ang), np.cos(ang)], axis=1),
            np.concatenate([-np.sin(ang), np.sin(ang)], axis=1))


def _zeta_rows(length):
    chunk = _retention_chunk(length)
    idx = np.arange(chunk, dtype=np.float64)
    zeta = np.exp((chunk - 1.0 - idx)[None, :] * _log_gamma()[:, None])
    return np.concatenate([np.broadcast_to(z[:, None], (chunk, RET_HEAD_DIM)) for z in zeta], axis=1)


def _decay_tables(length, padded):
    chunk = _retention_chunk(length)
    cpad = chunk if padded == length else padded
    lg = _log_gamma()
    idx = np.arange(chunk, dtype=np.float64)
    diff = idx[:, None] - idx[None, :]
    mask = np.where(diff[None] >= 0, np.exp(np.maximum(diff, 0.0)[None] * lg[:, None, None]), 0.0)
    xi = np.exp((idx + 1.0)[None, :] * lg[:, None])
    gamma_c = tuple(float(v) for v in np.exp(chunk * lg))
    mask = np.pad(mask, ((0, 0), (0, cpad - chunk), (0, cpad - chunk)))
    xi = np.broadcast_to(np.pad(xi, ((0, 0), (0, cpad - chunk)))[:, :, None], (RET_HEADS, cpad, RET_HEAD_DIM))
    f = lambda a: jnp.asarray(np.ascontiguousarray(a), dtype=F32)
    return cpad, gamma_c, f(mask), f(xi)


def _ret_lanes(part, head):
    lo = part * RET_WIDTH + head * RET_HEAD_DIM
    return slice(lo, lo + RET_HEAD_DIM)


def _proj_kernel(x_ref, g_ref, w_ref, cc_ref, ss_ref, zeta_ref, *rest, n_cast):
    casts_in, rest = rest[:n_cast], rest[n_cast:]
    u_ref, ret_ref, gate_ref = rest[:3]
    w = w_ref[...]
    if n_cast:
        w = w.astype(BF16)
        for src, dst in zip(casts_in, rest[3:]):
            dst[...] = src[...].astype(BF16)
    h = _rmsnorm(x_ref[...], g_ref[...]).astype(BF16)
    proj = _dot(h, w)
    u_ref[...] = proj[:, :SSM_WIDTH]
    ret_ref[:, 3 * RET_WIDTH:] = proj[:, SSM_WIDTH + 2 * RET_WIDTH:SSM_WIDTH + 3 * RET_WIDTH].astype(BF16)
    gate_ref[...] = proj[:, SSM_WIDTH + 3 * RET_WIDTH:]
    cc = cc_ref[...]
    ss = ss_ref[...]
    rope = lambda a: a * cc + pltpu.roll(a, RET_HEAD_DIM // 2, axis=1) * ss
    for hd in range(RET_HEADS):
        cols = slice(hd * RET_HEAD_DIM, (hd + 1) * RET_HEAD_DIM)
        q = rope(proj[:, SSM_WIDTH + hd * RET_HEAD_DIM:SSM_WIDTH + (hd + 1) * RET_HEAD_DIM])
        k = rope(proj[:, SSM_WIDTH + RET_WIDTH + hd * RET_HEAD_DIM:SSM_WIDTH + RET_WIDTH + (hd + 1) * RET_HEAD_DIM])
        ret_ref[:, _ret_lanes(0, hd)] = (q * (RET_HEAD_DIM ** -0.5)).astype(BF16)
        ret_ref[:, _ret_lanes(1, hd)] = k.astype(BF16)
        ret_ref[:, _ret_lanes(2, hd)] = (k * zeta_ref[:, cols]).astype(BF16)


def _project(x, g_mix, w_in, *, length, pos0, casts=()):
    t = x.shape[0]
    tm = min(t, ROWS_LARGE)
    steps = t // tm
    chunk = _retention_chunk(length)
    casts = (w_in, *casts) if casts else ()
    if tm % chunk or (length % tm and tm % length):
        raise ValueError("token tiles must hold whole retention chunks of whole or repeated sequences")
    cc, ss = _rope_tables(length, pos0)
    if length < tm:
        cc, ss = np.tile(cc, (tm // length, 1)), np.tile(ss, (tm // length, 1))
    tab_blocks = cc.shape[0] // tm
    zeta = np.tile(_zeta_rows(length), (tm // chunk, 1))
    const = lambda a: jnp.asarray(np.ascontiguousarray(a), dtype=F32)
    row = lambda w: pl.BlockSpec((tm, w), lambda i: (i, 0))
    tab = pl.BlockSpec((tm, RET_HEAD_DIM), lambda i: (i % tab_blocks, 0))
    f = jax.ShapeDtypeStruct((t, SSM_WIDTH), F32)
    h = jax.ShapeDtypeStruct((t, 4 * RET_WIDTH), BF16)
    in_specs = [row(D_MODEL), _full((1, D_MODEL)), _full((D_MODEL, PROJ_WIDTH)), tab, tab, _full((tm, RET_WIDTH))]
    out_specs = [row(SSM_WIDTH), row(4 * RET_WIDTH), row(RET_WIDTH)]
    out_shape = [f, h, f]
    if casts:
        for c in casts:
            rows, rem = divmod(c.shape[0], steps)
            if rem or rows % BF16_ROWS:
                raise ValueError("side-cast weights must split into whole bf16 tiles per grid step")
            blk = pl.BlockSpec((rows, c.shape[1]), lambda i: (i, 0))
            in_specs.append(blk)
            out_specs.append(blk)
            out_shape.append(jax.ShapeDtypeStruct(c.shape, BF16))
    outs = pl.pallas_call(
        functools.partial(_proj_kernel, n_cast=len(casts)),
        grid=(steps,),
        in_specs=in_specs,
        out_specs=out_specs,
        out_shape=out_shape,
        compiler_params=_params("arbitrary" if casts else "parallel", fuse=_fuse(len(in_specs), 0)),
        name="in_proj",
    )(x, g_mix, w_in, const(cc), const(ss), const(zeta), *casts)
    return outs[:3], outs[3:]


def _s5_block_kernel(u_ref, h0r_ref, h0i_ref, ar_ref, ai_ref, winr_ref, wini_ref, woutr_ref, wouti_ref,
                     kdir_ref, d_ref, wglu_ref, out_ref, hr_ref, hi_ref, vr_scr, vi_scr, xr_scr, xi_scr,
                     *, batch, steps, pair, width):
    R, T = SSM_BLOCK, SSM_TILES
    chan = SSM_TILE_CHANNELS
    blocks = steps // R
    rows, brow = steps * batch, blocks * batch

    @pl.when(pl.program_id(0) == 0)
    def _():
        hr_ref[...] = h0r_ref[...]
        hi_ref[...] = h0i_ref[...]

    u = jnp.swapaxes(u_ref[...], 0, 1).reshape(rows, SSM_WIDTH)
    u4 = u.reshape(blocks, R, batch, SSM_WIDTH)
    lag = [u4[:, s].reshape(brow, SSM_WIDTH) for s in range(R)]
    direct = []
    for n in range(T):
        un = jnp.concatenate([lag[s][:, chan * n:chan * (n + 1)] for s in range(R)], axis=1).astype(BF16)
        lanes = slice(MXU_TILE * n, MXU_TILE * (n + 1))
        vr_scr[:, lanes] = _dot(un, winr_ref[n])
        vi_scr[:, lanes] = _dot(un, wini_ref[n])
        direct.append(_dot(un, kdir_ref[n]))

    for j in range(SSM_LANES // width):
        glob = slice(j * width, (j + 1) * width)
        ar = jnp.broadcast_to(ar_ref[:, glob], (batch, width))
        ai = jnp.broadcast_to(ai_ref[:, glob], (batch, width))
        xr = hr_ref[:, glob]
        xi = hi_ref[:, glob]
        for i in range(blocks // pair):
            prev_r, prev_i = [], []
            for s in range(pair):
                at = slice((i * pair + s) * batch, (i * pair + s + 1) * batch)
                prev_r.append(xr)
                prev_i.append(xi)
                xr, xi = ar * xr - ai * xi + vr_scr[at, glob], ar * xi + ai * xr + vi_scr[at, glob]
            blk = slice(i * pair * batch, (i + 1) * pair * batch)
            xr_scr[blk, glob] = jnp.concatenate(prev_r, axis=0).astype(BF16)
            xi_scr[blk, glob] = jnp.concatenate(prev_i, axis=0).astype(BF16)
        hr_ref[:, glob] = xr
        hi_ref[:, glob] = xi

    y4 = []
    for n in range(T):
        lanes = slice(MXU_TILE * n, MXU_TILE * (n + 1))
        y4.append(_dot(xr_scr[:, lanes], woutr_ref[n]) + _dot(xi_scr[:, lanes], wouti_ref[n]) + direct[n])
    per_lag = [jnp.concatenate([y4[n][:, chan * j:chan * (j + 1)] for n in range(T)], axis=1)
               .reshape(blocks, batch, SSM_WIDTH) for j in range(R)]
    y = jnp.stack(per_lag, axis=1).reshape(rows, SSM_WIDTH) + d_ref[...] * u
    z = _gelu_tanh(y)
    out = z * _sigmoid(_dot(z.astype(BF16), wglu_ref[...].astype(BF16)))
    out_ref[...] = jnp.swapaxes(out.reshape(steps, batch, SSM_WIDTH), 0, 1).astype(BF16)


def _s5_mixer(u, h0_re, h0_im, maps, d_skip, w_glu, *, batch, length):
    a_re, a_im, winr, wini, woutr, wouti, direct = maps
    steps = min(length, S5_STEPS)
    if steps % SSM_BLOCK or length % steps:
        raise ValueError("sequence length must be a multiple of the S5 block")
    blocks = steps // SSM_BLOCK
    pair = max(1, min(blocks, BF16_ROWS // batch))
    if (pair * batch) % BF16_ROWS or blocks % pair:
        raise ValueError("S5 state stores need whole bf16 tiles")
    width = SSM_LANES // 2 if batch <= 8 else LANES
    brow = blocks * batch
    kern = functools.partial(_s5_block_kernel, batch=batch, steps=steps, pair=pair, width=width)
    st = jax.ShapeDtypeStruct((batch, SSM_LANES), F32)
    tile = _full((SSM_TILES, MXU_TILE, MXU_TILE))
    return pl.pallas_call(
        kern,
        grid=(length // steps,),
        in_specs=[pl.BlockSpec((batch, steps, SSM_WIDTH), lambda i: (0, i, 0)),
                  _full((batch, SSM_LANES)), _full((batch, SSM_LANES)),
                  _full((1, SSM_LANES)), _full((1, SSM_LANES)),
                  tile, tile, tile, tile, tile,
                  _full((1, SSM_WIDTH)), _full((SSM_WIDTH, SSM_WIDTH))],
        out_specs=[pl.BlockSpec((batch, steps, SSM_WIDTH), lambda i: (0, i, 0)),
                   _full((batch, SSM_LANES)), _full((batch, SSM_LANES))],
        out_shape=(jax.ShapeDtypeStruct((batch, length, SSM_WIDTH), BF16), st, st),
        scratch_shapes=[pltpu.VMEM((brow, SSM_LANES), F32), pltpu.VMEM((brow, SSM_LANES), F32),
                        pltpu.VMEM((brow, SSM_LANES), BF16), pltpu.VMEM((brow, SSM_LANES), BF16)],
        compiler_params=_params("arbitrary", fuse=_fuse(12, 11)),
        name="s5_mixer",
    )(u, h0_re, h0_im, a_re, a_im, winr, wini, woutr, wouti, direct, d_skip, w_glu)


def _ret_kernel(p_ref, g_ref, mask_ref, xi_ref, gain_ref, s0_ref, o_ref, s_ref,
                *, bb, gamma_c, unroll, chunk, chunks):
    @pl.when(pl.program_id(1) == 0)
    def _():
        s_ref[...] = s0_ref[...]

    heads = range(RET_HEADS)
    cols = [slice(h * RET_HEAD_DIM, (h + 1) * RET_HEAD_DIM) for h in heads]

    def one_chunk(b, rows):
        qb = [p_ref[b, rows, _ret_lanes(0, h)] for h in heads]
        vb = [p_ref[b, rows, _ret_lanes(3, h)] for h in heads]
        s_prev = [s_ref[b, h] for h in heads]
        scores = [(_dot_nt(qb[h], p_ref[b, rows, _ret_lanes(1, h)]) * mask_ref[h]).astype(BF16) for h in heads]
        cross = [_dot(qb[h], s_prev[h].astype(BF16)) * xi_ref[h] for h in heads]
        kv = [_dot_tn(p_ref[b, rows, _ret_lanes(2, h)], vb[h]) for h in heads]
        inner = [_dot(scores[h], vb[h]) for h in heads]
        outs = []
        for h in heads:
            s_ref[b, h] = s_prev[h] * gamma_c[h] + kv[h]
            o = inner[h] + cross[h]
            mu = jnp.mean(o, axis=-1, keepdims=True)
            oc = o - mu
            var = jnp.mean(oc * oc, axis=-1, keepdims=True)
            on = oc * lax.rsqrt(var + EPS) * gain_ref[:, cols[h]]
            gate = g_ref[b, rows, cols[h]]
            outs.append((gate * _sigmoid(gate) * on).astype(BF16))
        o_ref[b, rows, :] = jnp.concatenate(outs, axis=1)

    def per_batch(b, carry):
        for c in range(chunks):
            one_chunk(b, slice(c * chunk, (c + 1) * chunk))
        return carry

    if unroll == bb:
        for b in range(bb):
            per_batch(b, 0)
    else:
        lax.fori_loop(0, bb, per_batch, 0, unroll=unroll)


def _retention(packed, g, s0, gn_gain):
    batch, length, _ = g.shape
    padded = pl.cdiv(length, BF16_ROWS) * BF16_ROWS
    chunk, gamma_c, mask, xi = _decay_tables(length, padded)
    if padded != length:
        pad = lambda a: jnp.pad(a, ((0, 0), (0, padded - length), (0, 0)))
        packed, g = pad(packed), pad(g)
    bb, unroll = (8, 2) if chunk > BF16_ROWS else (16, 16)
    chunks = 2 if (padded // chunk) % 2 == 0 else 1
    tok = pl.BlockSpec((bb, chunks * chunk, RET_WIDTH), lambda i, c: (i, c, 0))
    state = pl.BlockSpec((bb, RET_HEADS, RET_HEAD_DIM, RET_HEAD_DIM), lambda i, c: (i, 0, 0, 0))
    out, s_new = pl.pallas_call(
        functools.partial(_ret_kernel, bb=bb, gamma_c=gamma_c, unroll=unroll, chunk=chunk, chunks=chunks),
        grid=(batch // bb, padded // (chunks * chunk)),
        in_specs=[pl.BlockSpec((bb, chunks * chunk, 4 * RET_WIDTH), lambda i, c: (i, c, 0)), tok,
                  _full((RET_HEADS, chunk, chunk)), _full((RET_HEADS, chunk, RET_HEAD_DIM)),
                  _full((1, RET_WIDTH)), state],
        out_specs=[tok, state],
        out_shape=(jax.ShapeDtypeStruct((batch, padded, RET_WIDTH), BF16),
                   jax.ShapeDtypeStruct(s0.shape, F32)),
        compiler_params=_params("parallel", "arbitrary", fuse=_fuse(6, 0, 1)),
        name="retention",
    )(packed, g, mask, xi, gn_gain, s0)
    return out[:, :length], s_new


def _mixer_out(x_ref, ssm_ref, ret_ref, wout_ref, rows=slice(None)):
    return (x_ref[rows, :] + _dot(ssm_ref[rows, :], wout_ref[:SSM_WIDTH, :].astype(BF16))
            + _dot(ret_ref[rows, :], wout_ref[SSM_WIDTH:, :].astype(BF16)))


def _query(x1, g_ref, wq_ref):
    return (_dot(_rmsnorm(x1, g_ref[...]).astype(BF16), wq_ref[...].astype(BF16)) * MEM_SCALE).astype(BF16)


def _outq_kernel(x_ref, ssm_ref, ret_ref, wout_ref, g_ref, wq_ref, x1_ref, q_ref):
    x1 = _mixer_out(x_ref, ssm_ref, ret_ref, wout_ref)
    x1_ref[...] = x1
    q_ref[...] = _query(x1, g_ref, wq_ref)


def _out_and_query(x, ssm, ret, w_out, g_xattn, w_mq):
    t = x.shape[0]
    tm = min(t, ROWS_SMALL)
    row = lambda w: pl.BlockSpec((tm, w), lambda i: (i, 0))
    return pl.pallas_call(
        _outq_kernel,
        grid=(t // tm,),
        in_specs=[row(D_MODEL), row(SSM_WIDTH), row(RET_WIDTH), _full((D_MODEL, D_MODEL)),
                  _full((1, D_MODEL)), _full((D_MODEL, D_MODEL))],
        out_specs=[row(D_MODEL), row(D_MODEL)],
        out_shape=(jax.ShapeDtypeStruct((t, D_MODEL), F32), jax.ShapeDtypeStruct((t, D_MODEL), BF16)),
        compiler_params=_params("parallel", fuse=_fuse(6, 0, 3, 5)),
        name="out_proj_query",
    )(x, ssm, ret, w_out, g_xattn, w_mq)


def _cache_row_order(x):
    halves = MEM_HEAD_DIM // LANES
    tiles = [x[:, (h * halves + d) * LANES:(h * halves + d + 1) * LANES]
             for d in range(halves) for h in range(MEM_HEADS)]
    return jnp.swapaxes(jnp.stack(tiles, axis=0), 0, 1).reshape(x.shape[0] * CACHE_SUB, LANES)


MEMKV_SEQS = 2


def _memkv_kernel(m_ref, g_ref, wk_ref, wv_ref, k_ref, v_ref, kb_ref, vb_ref):
    seqs = m_ref.shape[0]
    m = _rmsnorm(m_ref[...].reshape(seqs * MEM_LEN, D_MODEL), g_ref[...]).astype(BF16)
    k = _dot(m, wk_ref[...])
    v = _dot(m, wv_ref[...])
    k_ref[...] = _cache_row_order(k).reshape(k_ref.shape)
    v_ref[...] = _cache_row_order(v).reshape(v_ref.shape)
    kb_ref[...] = k.astype(BF16).reshape(kb_ref.shape)
    vb_ref[...] = v.astype(BF16).reshape(vb_ref.shape)


def _memory_kv(mem, g_mem, w_mk, w_mv):
    batch = mem.shape[0]
    seqs = MEMKV_SEQS if batch % MEMKV_SEQS == 0 else 1
    tok = pl.BlockSpec((seqs, MEM_LEN, D_MODEL), lambda i: (i, 0, 0))
    cache = pl.BlockSpec((seqs, CACHE_ROWS, LANES), lambda i: (i, 0, 0))
    f = jax.ShapeDtypeStruct((batch, CACHE_ROWS, LANES), F32)
    h = jax.ShapeDtypeStruct((batch, MEM_LEN, D_MODEL), BF16)
    return pl.pallas_call(
        _memkv_kernel,
        grid=(batch // seqs,),
        in_specs=[tok, _full((1, D_MODEL)), _full((D_MODEL, D_MODEL)), _full((D_MODEL, D_MODEL))],
        out_specs=[cache, cache, tok, tok],
        out_shape=(f, f, h, h),
        compiler_params=_params("parallel", fuse=_fuse(4, 2, 3)),
        name="memory_kv",
    )(mem, g_mem, w_mk, w_mv)


ATTN_ROWS = 512


def _outattn_kernel(x_ref, ssm_ref, ret_ref, wout_ref, g_ref, wq_ref, k_ref, v_ref, x1_ref, o_ref):
    cols = [slice(h * MEM_HEAD_DIM, (h + 1) * MEM_HEAD_DIM) for h in range(MEM_HEADS)]
    groups = [slice(r, r + ATTN_ROWS) for r in range(0, x_ref.shape[0], ATTN_ROWS)]

    def scores_of(rows):
        x1 = _mixer_out(x_ref, ssm_ref, ret_ref, wout_ref, rows)
        x1_ref[rows, :] = x1
        q = _query(x1, g_ref, wq_ref)
        return [_dot_nt(q[:, c], k_ref[0, :, c]) for c in cols]

    def attend(rows, scores):
        outs = []
        for s, c in zip(scores, cols):
            e = jnp.exp(s - jnp.max(s, axis=-1, keepdims=True))
            p = (e / jnp.sum(e, axis=-1, keepdims=True)).astype(BF16)
            outs.append(_dot(p, v_ref[0, :, c]).astype(BF16))
        o_ref[rows, :] = jnp.concatenate(outs, axis=1)

    pending = None
    for rows in groups:
        scores = scores_of(rows)
        if pending is not None:
            attend(*pending)
        pending = (rows, scores)
    attend(*pending)


def _out_and_attention(x, ssm, ret, w_out, g_xattn, w_mq, mk, mv, *, length):
    t = x.shape[0]
    tm = min(length, ROWS_LARGE)
    per_seq = length // tm
    row = lambda w: pl.BlockSpec((tm, w), lambda i: (i, 0))
    mem = pl.BlockSpec((1, MEM_LEN, D_MODEL), lambda i: (i // per_seq, 0, 0))
    return pl.pallas_call(
        _outattn_kernel,
        grid=(t // tm,),
        in_specs=[row(D_MODEL), row(SSM_WIDTH), row(RET_WIDTH), _full((D_MODEL, D_MODEL)),
                  _full((1, D_MODEL)), _full((D_MODEL, D_MODEL)), mem, mem],
        out_specs=[row(D_MODEL), row(D_MODEL)],
        out_shape=(jax.ShapeDtypeStruct((t, D_MODEL), F32), jax.ShapeDtypeStruct((t, D_MODEL), BF16)),
        compiler_params=_params("parallel", fuse=_fuse(8, 3, 5)),
        name="out_proj_attention",
    )(x, ssm, ret, w_out, g_xattn, w_mq, mk, mv)


def _cached_attention(q_ref, k_ref, v_ref, o_ref, *, bb, length):
    hit = (lax.broadcasted_iota(jnp.int32, (CACHE_SUB, CACHE_ROWS), 0)
           == lax.broadcasted_iota(jnp.int32, (CACHE_SUB, CACHE_ROWS), 1) % CACHE_SUB)
    first_half = lax.broadcasted_iota(jnp.int32, (length, LANES), 1) % CACHE_SUB < MEM_HEADS
    tiles = CACHE_ROWS // LANES

    def class_reduce(x, op):
        shift = CACHE_SUB
        while shift < LANES:
            x = op(x, pltpu.roll(x, shift, axis=1))
            shift *= 2
        return x

    def softmax_rows(r):
        z = jnp.concatenate(
            [jnp.sum(jnp.where(hit, r[CACHE_SUB * t:CACHE_SUB * (t + 1)], 0.0), axis=0, keepdims=True)
             for t in range(length)], axis=0)
        parts = []
        for i in range(tiles):
            zi = z[:, LANES * i:LANES * (i + 1)]
            parts.append(zi + pltpu.roll(zi, LANES - MEM_HEADS, axis=1))
        mx = parts[0]
        for pi in parts[1:]:
            mx = jnp.maximum(mx, pi)
        mx = class_reduce(mx, jnp.maximum)
        es = [jnp.exp(pi - mx) for pi in parts]
        tot = es[0]
        for ei in es[1:]:
            tot = tot + ei
        tot = class_reduce(tot, jnp.add)
        ps = []
        for ei in es:
            pi = ei / tot
            ps.append(jnp.where(first_half, pi, pltpu.roll(pi, MEM_HEADS, axis=1)))
        p = jnp.concatenate(ps, axis=1)
        return jnp.concatenate(
            [jnp.where(hit, jnp.broadcast_to(p[t:t + 1], (CACHE_SUB, CACHE_ROWS)), 0.0)
             for t in range(length)], axis=0).astype(BF16)

    def start():
        return [_dot_nt(q_ref[b], k_ref[b].astype(BF16)) for b in range(bb)]

    def finish(scores):
        probs = [softmax_rows(r) for r in scores]
        for b in range(bb):
            o_ref[b] = _dot(probs[b], v_ref[b].astype(BF16)).astype(BF16)

    return start, finish


def _to_cache_rows(a, lead):
    halves = MEM_HEAD_DIM // LANES
    batch = a.size // (lead * D_MODEL)
    a = a.reshape(batch, lead, MEM_HEADS, halves, LANES).transpose(0, 1, 3, 2, 4)
    return a.reshape(batch, lead * CACHE_SUB, LANES)


def _from_cache_rows(a, lead):
    halves = MEM_HEAD_DIM // LANES
    batch = a.shape[0]
    a = a.reshape(batch, lead, halves, MEM_HEADS, LANES).transpose(0, 1, 3, 2, 4)
    return a.reshape(batch, lead, D_MODEL)


MLP_CHUNK = 1024


def _post_kernel(x1_ref, o_ref, wo_ref, gm_ref, wup_ref, wdn_ref, gf_ref, *rest, side_bb, side_len):
    if side_bb:
        qs_ref, ck_ref, cv_ref, y_ref, os_ref = rest
    else:
        (y_ref,) = rest
    x2 = x1_ref[...] + _dot(o_ref[...], wo_ref[...].astype(BF16))
    if side_bb:
        side_start, side_finish = _cached_attention(qs_ref, ck_ref, cv_ref, os_ref, bb=side_bb, length=side_len)
        side_scores = side_start()
    h = _rmsnorm(x2, gm_ref[...]).astype(BF16)
    chunks = [slice(c, c + MLP_CHUNK) for c in range(0, D_FF, MLP_CHUNK)]
    acc = x2
    up = _dot(h, wup_ref[:, chunks[0]])
    for c, cols in enumerate(chunks):
        nxt = _dot(h, wup_ref[:, chunks[c + 1]]) if c + 1 < len(chunks) else None
        if side_bb and c == len(chunks) - 1:
            side_finish(side_scores)
        a = jnp.maximum(up, 0.0)
        acc = acc + _dot((a * a).astype(BF16), wdn_ref[cols, :])
        up = nxt
    y_ref[...] = _rmsnorm(acc, gf_ref[...])


def _attn_out_mlp(x1, o, w_mo, g_mlp, w_up, w_down, g_final, side=None):
    t = x1.shape[0]
    tm = min(t, ROWS_SMALL)
    steps = t // tm
    row = pl.BlockSpec((tm, D_MODEL), lambda i: (i, 0))
    in_specs = [row, row, _full((D_MODEL, D_MODEL)), _full((1, D_MODEL)),
                _full((D_MODEL, D_FF)), _full((D_FF, D_MODEL)), _full((1, D_MODEL))]
    out_specs = [row]
    out_shape = [jax.ShapeDtypeStruct((t, D_MODEL), F32)]
    args = [x1, o, w_mo, g_mlp, w_up, w_down, g_final]
    side_bb = side_len = 0
    if side is not None:
        q_rows = side[0]
        side_bb, rem = divmod(q_rows.shape[0], steps)
        if rem or not side_bb:
            raise ValueError("side attention sequences must spread evenly over the grid steps")
        side_len = q_rows.shape[1] // CACHE_SUB
        blk = lambda r: pl.BlockSpec((side_bb, r, LANES), lambda i: (i, 0, 0))
        in_specs += [blk(q_rows.shape[1]), blk(CACHE_ROWS), blk(CACHE_ROWS)]
        out_specs.append(blk(q_rows.shape[1]))
        out_shape.append(jax.ShapeDtypeStruct(q_rows.shape, BF16))
        args += list(side)
    outs = pl.pallas_call(
        functools.partial(_post_kernel, side_bb=side_bb, side_len=side_len),
        grid=(steps,),
        in_specs=in_specs,
        out_specs=out_specs,
        out_shape=out_shape,
        compiler_params=_params("parallel", fuse=_fuse(len(in_specs), 1, 2, 7)),
        name="attn_out_mlp",
    )(*args)
    return outs if side is not None else outs[0]


def _mixers(proj, batch, length, s5_re, s5_im, ret_s, p):
    tokens = batch * length
    u, packed, g = proj
    ssm, s5_re_new, s5_im_new = _s5_mixer(
        u.reshape(batch, length, SSM_WIDTH), s5_re.reshape(batch, SSM_LANES), s5_im.reshape(batch, SSM_LANES),
        p["s5_maps"], p["d_skip"], p["w_glu"], batch=batch, length=length)
    tok3 = lambda a: a.reshape(batch, length, RET_WIDTH)
    ret, ret_new = _retention(packed.reshape(batch, length, 4 * RET_WIDTH), tok3(g), ret_s, p["ret_gn"])
    states = (s5_re_new.reshape(1, batch, SSM_GROUPS, SSM_STATE),
              s5_im_new.reshape(1, batch, SSM_GROUPS, SSM_STATE), ret_new[None])
    return ssm.reshape(tokens, SSM_WIDTH), ret.reshape(tokens, RET_WIDTH), states


def kernel(x_prompt, x_sample, mem_prompt, state_s5_re, state_s5_im, state_ret, cache_mem_k, cache_mem_v, g_mix, w_in, lam_re, lam_im, log_dt, b_re, b_im, c_re, c_im, d_skip, w_glu, ret_gn, w_out, g_xattn, g_mem, w_mq, w_mk, w_mv, w_mo, g_mlp, w_up, w_down, g_final):
    if g_mix.shape[0] != 1:
        raise ValueError("this kernel implements the single-layer configuration")
    bp, lp, _ = x_prompt.shape
    bs, ls, _ = x_sample.shape
    vec = lambda a: a.reshape(1, -1).astype(F32)
    xp = x_prompt.reshape(bp * lp, D_MODEL)
    xs = x_sample.reshape(bs * ls, D_MODEL)
    proj_p, (w_in_b, w_up_b, w_down_b) = _project(
        xp, vec(g_mix[0]), w_in[0], length=lp, pos0=0.0, casts=(w_up[0], w_down[0]))
    w_out_b, w_mq_b, w_mo_b, w_mk_b, w_mv_b, w_glu_b = (
        w[0].astype(BF16) for w in (w_out, w_mq, w_mo, w_mk, w_mv, w_glu))
    proj_s, _ = _project(xs, vec(g_mix[0]), w_in_b, length=ls, pos0=float(PAST_LEN))
    p = dict(
        s5_maps=_s5_block_maps(lam_re[0], lam_im[0], log_dt[0], b_re[0], b_im[0], c_re[0], c_im[0]),
        d_skip=vec(d_skip[0]), w_glu=w_glu_b, ret_gn=vec(ret_gn[0]))
    g_x = vec(g_xattn[0])
    mlp = (w_mo_b, vec(g_mlp[0]), w_up_b, w_down_b, vec(g_final))

    ssm_s, ret_s, states_s = _mixers(proj_s, bs, ls, state_s5_re[0], state_s5_im[0], state_ret[0],
                                     dict(p, w_glu=w_glu[0]))
    x1_s, q_s = _out_and_query(xs, ssm_s, ret_s, w_out[0], g_x, w_mq[0])
    if (ls * CACHE_SUB) % BF16_ROWS:
        raise ValueError("cached attention needs whole bf16 tiles of query rows")
    side = (_to_cache_rows(q_s, ls), _to_cache_rows(cache_mem_k, MEM_LEN), _to_cache_rows(cache_mem_v, MEM_LEN))

    mk_rows, mv_rows, mk_p, mv_p = _memory_kv(mem_prompt, vec(g_mem[0]), w_mk_b, w_mv_b)
    zs = jnp.zeros((bp, SSM_GROUPS, SSM_STATE), F32)
    zr = jnp.zeros((bp, RET_HEADS, RET_HEAD_DIM, RET_HEAD_DIM), F32)
    ssm_p, ret_p, states_p = _mixers(proj_p, bp, lp, zs, zs, zr, p)
    x1_p, o_p = _out_and_attention(xp, ssm_p, ret_p, w_out_b, g_x, w_mq_b, mk_p, mv_p, length=lp)
    y_p, o_s = _attn_out_mlp(x1_p, o_p, *mlp, side=side)
    y_s = _attn_out_mlp(x1_s, _from_cache_rows(o_s, ls).reshape(bs * ls, D_MODEL), w_mo[0], *mlp[1:])

    kv5 = lambda a: _from_cache_rows(a, MEM_LEN).reshape(1, bp, MEM_LEN, MEM_HEADS, MEM_HEAD_DIM)
    return (y_p.reshape(bp, lp, D_MODEL), y_s.reshape(bs, ls, D_MODEL), *states_p, kv5(mk_rows), kv5(mv_rows),
            *states_s)
```

```python
import functools
import math

import numpy as np
import jax
import jax.numpy as jnp
from jax import lax
from jax.experimental import pallas as pl
from jax.experimental.pallas import tpu as pltpu

F32 = jnp.float32
BF16 = jnp.bfloat16

D_MODEL = 1024
SSM_WIDTH = 512
SSM_GROUP = 16
SSM_GROUPS = 32
SSM_STATE = 64
SSM_LANES = SSM_GROUPS * SSM_STATE
RET_WIDTH = 512
RET_HEADS = 4
RET_HEAD_DIM = 128
RET_CHUNK = 128
ROPE_BASE = 10000.0
MEM_LEN = 256
MEM_HEADS = 4
MEM_HEAD_DIM = 256
MEM_SCALE = MEM_HEAD_DIM ** -0.5
D_FF = 4096
PROJ_WIDTH = SSM_WIDTH + 4 * RET_WIDTH
EPS = 1e-6
PAST_LEN = 16384

LANES = 128
MXU_TILE = 256
BF16_ROWS = 16
VMEM_LIMIT = 56 * 1024 * 1024

ROWS_LARGE = 1024
ROWS_SMALL = 512
S5_STEPS = 256

CACHE_ROWS = MEM_LEN * MEM_HEADS * MEM_HEAD_DIM // LANES
CACHE_SUB = MEM_HEADS * MEM_HEAD_DIM // LANES

assert math.frexp(MEM_SCALE)[0] == 0.5


def _dot(a, b):
    return jnp.dot(a, b, preferred_element_type=F32)


def _dot_nt(a, b):
    return lax.dot_general(a, b, (((1,), (1,)), ((), ())), preferred_element_type=F32)


def _dot_tn(a, b):
    return lax.dot_general(a, b, (((0,), (0,)), ((), ())), preferred_element_type=F32)


def _rmsnorm(x, g):
    return x * lax.rsqrt(jnp.mean(x * x, axis=-1, keepdims=True) + EPS) * g


def _sigmoid(x):
    return 1.0 / (1.0 + jnp.exp(-x))


def _gelu_tanh(x):
    c = math.sqrt(2.0 / math.pi)
    return x * (0.5 * jnp.tanh(x * (c + (c * 0.044715) * (x * x))) + 0.5)


def _params(*sem, fuse=None):
    return pltpu.CompilerParams(dimension_semantics=sem, vmem_limit_bytes=VMEM_LIMIT, allow_input_fusion=fuse)


def _fuse(n_inputs, *which):
    return [i in which for i in range(n_inputs)]


def _full(shape):
    return pl.BlockSpec(shape, lambda *_: (0,) * len(shape))


SSM_BLOCK = 4
SSM_TILE_GROUPS = MXU_TILE // SSM_STATE
SSM_TILES = SSM_GROUPS // SSM_TILE_GROUPS
SSM_TILE_CHANNELS = SSM_TILE_GROUPS * SSM_GROUP
GROUP_SHIFT = SSM_GROUP.bit_length() - 1
STATE_SHIFT = SSM_STATE.bit_length() - 1

assert SSM_BLOCK * SSM_TILE_CHANNELS == MXU_TILE and SSM_TILE_CHANNELS == SSM_STATE


def _s5_prep_kernel(lr_ref, li_ref, ldt_ref, br_ref, bi_ref, cr_ref, ci_ref,
                    ar_ref, ai_ref, winr_ref, wini_ref, woutr_ref, wouti_ref, kdir_ref):
    lr = lr_ref[...]
    li = li_ref[...]
    dt = jnp.exp(ldt_ref[...])
    mag = jnp.exp(lr * dt)
    ab_re = mag * jnp.cos(li * dt)
    ab_im = mag * jnp.sin(li * dt)
    den = lr * lr + li * li
    f_re = ((ab_re - 1.0) * lr + ab_im * li) / den
    f_im = (ab_im * lr - (ab_re - 1.0) * li) / den
    bb_re = f_re * br_ref[...] - f_im * bi_ref[...]
    bb_im = f_re * bi_ref[...] + f_im * br_ref[...]
    cr = cr_ref[...]
    ci = ci_ref[...]
    pows = [(jnp.ones_like(ab_re), jnp.zeros_like(ab_re))]
    for _ in range(SSM_BLOCK):
        pr, pi = pows[-1]
        pows.append((pr * ab_re - pi * ab_im, pr * ab_im + pi * ab_re))
    ar_ref[...], ai_ref[...] = pows[SSM_BLOCK]

    R, T, TG = SSM_BLOCK, SSM_TILES, SSM_TILE_GROUPS
    rows = SSM_GROUPS * SSM_GROUP
    side = SSM_TILE_CHANNELS
    exact = dict(precision=lax.Precision.HIGHEST, preferred_element_type=F32)
    iota = lambda shape, d: lax.broadcasted_iota(jnp.int32, shape, d)
    rep4 = lambda a, axis: jnp.concatenate([a] * TG, axis=axis)
    tiles = lambda a: a.reshape(T, side, a.shape[-1])
    group_of = lambda i: (i & (side - 1)) >> GROUP_SHIFT

    own_in = group_of(iota((rows, MXU_TILE), 0)) == iota((rows, MXU_TILE), 1) >> STATE_SHIFT
    for s in range(R):
        pr, pi = pows[R - 1 - s]
        for ref, w in ((winr_ref, pr * bb_re - pi * bb_im), (wini_ref, pr * bb_im + pi * bb_re)):
            ref[:, side * s:side * (s + 1), :] = tiles(jnp.where(own_in, rep4(w, 1), 0.0)).astype(BF16)

    own_out = iota((MXU_TILE, rows), 0) >> STATE_SHIFT == group_of(iota((MXU_TILE, rows), 1))
    for ref, real in ((woutr_ref, True), (wouti_ref, False)):
        per_lag = []
        for j in range(R):
            pr, pi = pows[j + 1]
            w = cr * pr - ci * pi if real else -(cr * pi + ci * pr)
            wt = w.T
            per_lag.append(jnp.where(own_out, rep4(wt, 0), 0.0))
        for n in range(T):
            ref[n] = jnp.concatenate([m[:, side * n:side * (n + 1)] for m in per_lag], axis=1).astype(BF16)

    grouped = lambda a: a.reshape(SSM_GROUPS, SSM_GROUP, SSM_STATE)
    contract = lambda a, b: lax.dot_general(a, b, (((2,), (2,)), ((0,), (0,))), **exact)
    own_dir = group_of(iota((rows, side), 0)) == iota((rows, side), 1) >> GROUP_SHIFT
    by_lag = []
    for d in range(R):
        pr, pi = pows[d]
        er = grouped(pr * bb_re - pi * bb_im)
        ei = grouped(pr * bb_im + pi * bb_re)
        kt = (contract(er, grouped(cr)) - contract(ei, grouped(ci))).reshape(rows, SSM_GROUP)
        by_lag.append(tiles(jnp.where(own_dir, rep4(kt, 1), 0.0)))
    nothing = jnp.zeros_like(by_lag[0])
    for s in range(R):
        kdir_ref[:, side * s:side * (s + 1), :] = jnp.concatenate(
            [by_lag[j - s] if j >= s else nothing for j in range(R)], axis=2).astype(BF16)


def _s5_block_maps(lam_re, lam_im, log_dt, b_re, b_im, c_re, c_im):
    rows = SSM_GROUPS * SSM_GROUP
    rep = lambda a: jnp.repeat(a, SSM_GROUP, axis=0)
    ldt = jnp.broadcast_to(log_dt[:, None], (SSM_GROUPS, SSM_STATE))
    bt = lambda b: b.transpose(0, 2, 1).reshape(rows, SSM_STATE)
    ct = lambda c: c.reshape(rows, SSM_STATE)
    one = jax.ShapeDtypeStruct((rows, SSM_STATE), F32)
    tile = jax.ShapeDtypeStruct((SSM_TILES, MXU_TILE, MXU_TILE), BF16)
    ar, ai, winr, wini, woutr, wouti, direct = pl.pallas_call(
        _s5_prep_kernel,
        out_shape=(one, one, tile, tile, tile, tile, tile),
        compiler_params=_params(fuse=[True] * 7),
        name="s5_block_maps",
    )(rep(lam_re), rep(lam_im), rep(ldt), bt(b_re), bt(b_im), ct(c_re), ct(c_im))
    a_re = ar[::SSM_GROUP].reshape(1, SSM_LANES)
    a_im = ai[::SSM_GROUP].reshape(1, SSM_LANES)
    return a_re, a_im, winr, wini, woutr, wouti, direct


def _retention_chunk(length):
    return RET_CHUNK if length % RET_CHUNK == 0 else length


def _log_gamma():
    return np.log(1.0 - 2.0 ** (-5.0 - np.arange(RET_HEADS, dtype=np.float64)))


def _rope_tables(length, pos0):
    half = RET_HEAD_DIM // 2
    inv = ROPE_BASE ** (-np.arange(half, dtype=np.float64) / half)
    ang = (pos0 + np.arange(length, dtype=np.float64))[:, None] * inv[None, :]
    return (np.concatenate([np.cos(ang), np.cos(ang)], axis=1),
            np.concatenate([-np.sin(ang), np.sin(ang)], axis=1))


def _zeta_rows(length):
    chunk = _retention_chunk(length)
    idx = np.arange(chunk, dtype=np.float64)
    zeta = np.exp((chunk - 1.0 - idx)[None, :] * _log_gamma()[:, None])
    return np.concatenate([np.broadcast_to(z[:, None], (chunk, RET_HEAD_DIM)) for z in zeta], axis=1)


def _decay_tables(length, padded):
    chunk = _retention_chunk(length)
    cpad = chunk if padded == length else padded
    lg = _log_gamma()
    idx = np.arange(chunk, dtype=np.float64)
    diff = idx[:, None] - idx[None, :]
    mask = np.where(diff[None] >= 0, np.exp(np.maximum(diff, 0.0)[None] * lg[:, None, None]), 0.0)
    xi = np.exp((idx + 1.0)[None, :] * lg[:, None])
    gamma_c = tuple(float(v) for v in np.exp(chunk * lg))
    mask = np.pad(mask, ((0, 0), (0, cpad - chunk), (0, cpad - chunk)))
    xi = np.broadcast_to(np.pad(xi, ((0, 0), (0, cpad - chunk)))[:, :, None], (RET_HEADS, cpad, RET_HEAD_DIM))
    f = lambda a: jnp.asarray(np.ascontiguousarray(a), dtype=F32)
    return cpad, gamma_c, f(mask), f(xi)


def _ret_lanes(part, head):
    lo = part * RET_WIDTH + head * RET_HEAD_DIM
    return slice(lo, lo + RET_HEAD_DIM)


def _proj_kernel(x_ref, g_ref, w_ref, cc_ref, ss_ref, zeta_ref, *rest, n_cast):
    casts_in, rest = rest[:n_cast], rest[n_cast:]
    u_ref, ret_ref, gate_ref = rest[:3]
    w = w_ref[...]
    if n_cast:
        w = w.astype(BF16)
        for src, dst in zip(casts_in, rest[3:]):
            dst[...] = src[...].astype(BF16)
    h = _rmsnorm(x_ref[...], g_ref[...]).astype(BF16)
    proj = _dot(h, w)
    u_ref[...] = proj[:, :SSM_WIDTH]
    ret_ref[:, 3 * RET_WIDTH:] = proj[:, SSM_WIDTH + 2 * RET_WIDTH:SSM_WIDTH + 3 * RET_WIDTH].astype(BF16)
    gate_ref[...] = proj[:, SSM_WIDTH + 3 * RET_WIDTH:]
    cc = cc_ref[...]
    ss = ss_ref[...]
    rope = lambda a: a * cc + pltpu.roll(a, RET_HEAD_DIM // 2, axis=1) * ss
    for hd in range(RET_HEADS):
        cols = slice(hd * RET_HEAD_DIM, (hd + 1) * RET_HEAD_DIM)
        q = rope(proj[:, SSM_WIDTH + hd * RET_HEAD_DIM:SSM_WIDTH + (hd + 1) * RET_HEAD_DIM])
        k = rope(proj[:, SSM_WIDTH + RET_WIDTH + hd * RET_HEAD_DIM:SSM_WIDTH + RET_WIDTH + (hd + 1) * RET_HEAD_DIM])
        ret_ref[:, _ret_lanes(0, hd)] = (q * (RET_HEAD_DIM ** -0.5)).astype(BF16)
        ret_ref[:, _ret_lanes(1, hd)] = k.astype(BF16)
        ret_ref[:, _ret_lanes(2, hd)] = (k * zeta_ref[:, cols]).astype(BF16)


def _project(x, g_mix, w_in, *, length, pos0, casts=()):
    t = x.shape[0]
    tm = min(t, ROWS_LARGE)
    steps = t // tm
    chunk = _retention_chunk(length)
    casts = (w_in, *casts) if casts else ()
    if tm % chunk or (length % tm and tm % length):
        raise ValueError("token tiles must hold whole retention chunks of whole or repeated sequences")
    cc, ss = _rope_tables(length, pos0)
    if length < tm:
        cc, ss = np.tile(cc, (tm // length, 1)), np.tile(ss, (tm // length, 1))
    tab_blocks = cc.shape[0] // tm
    zeta = np.tile(_zeta_rows(length), (tm // chunk, 1))
    const = lambda a: jnp.asarray(np.ascontiguousarray(a), dtype=F32)
    row = lambda w: pl.BlockSpec((tm, w), lambda i: (i, 0))
    tab = pl.BlockSpec((tm, RET_HEAD_DIM), lambda i: (i % tab_blocks, 0))
    f = jax.ShapeDtypeStruct((t, SSM_WIDTH), F32)
    h = jax.ShapeDtypeStruct((t, 4 * RET_WIDTH), BF16)
    in_specs = [row(D_MODEL), _full((1, D_MODEL)), _full((D_MODEL, PROJ_WIDTH)), tab, tab, _full((tm, RET_WIDTH))]
    out_specs = [row(SSM_WIDTH), row(4 * RET_WIDTH), row(RET_WIDTH)]
    out_shape = [f, h, f]
    if casts:
        for c in casts:
            rows, rem = divmod(c.shape[0], steps)
            if rem or rows % BF16_ROWS:
                raise ValueError("side-cast weights must split into whole bf16 tiles per grid step")
            blk = pl.BlockSpec((rows, c.shape[1]), lambda i: (i, 0))
            in_specs.append(blk)
            out_specs.append(blk)
            out_shape.append(jax.ShapeDtypeStruct(c.shape, BF16))
    outs = pl.pallas_call(
        functools.partial(_proj_kernel, n_cast=len(casts)),
        grid=(steps,),
        in_specs=in_specs,
        out_specs=out_specs,
        out_shape=out_shape,
        compiler_params=_params("arbitrary" if casts else "parallel", fuse=_fuse(len(in_specs), 0)),
        name="in_proj",
    )(x, g_mix, w_in, const(cc), const(ss), const(zeta), *casts)
    return outs[:3], outs[3:]


def _s5_block_kernel(u_ref, h0r_ref, h0i_ref, ar_ref, ai_ref, winr_ref, wini_ref, woutr_ref, wouti_ref,
                     kdir_ref, d_ref, wglu_ref, out_ref, hr_ref, hi_ref, vr_scr, vi_scr, xr_scr, xi_scr,
                     *, batch, steps, pair, width):
    R, T = SSM_BLOCK, SSM_TILES
    chan = SSM_TILE_CHANNELS
    blocks = steps // R
    rows, brow = steps * batch, blocks * batch

    @pl.when(pl.program_id(0) == 0)
    def _():
        hr_ref[...] = h0r_ref[...]
        hi_ref[...] = h0i_ref[...]

    u = jnp.swapaxes(u_ref[...], 0, 1).reshape(rows, SSM_WIDTH)
    u4 = u.reshape(blocks, R, batch, SSM_WIDTH)
    lag = [u4[:, s].reshape(brow, SSM_WIDTH) for s in range(R)]
    direct = []
    for n in range(T):
        un = jnp.concatenate([lag[s][:, chan * n:chan * (n + 1)] for s in range(R)], axis=1).astype(BF16)
        lanes = slice(MXU_TILE * n, MXU_TILE * (n + 1))
        vr_scr[:, lanes] = _dot(un, winr_ref[n])
        vi_scr[:, lanes] = _dot(un, wini_ref[n])
        direct.append(_dot(un, kdir_ref[n]))

    for j in range(SSM_LANES // width):
        glob = slice(j * width, (j + 1) * width)
        ar = jnp.broadcast_to(ar_ref[:, glob], (batch, width))
        ai = jnp.broadcast_to(ai_ref[:, glob], (batch, width))
        xr = hr_ref[:, glob]
        xi = hi_ref[:, glob]
        for i in range(blocks // pair):
            prev_r, prev_i = [], []
            for s in range(pair):
                at = slice((i * pair + s) * batch, (i * pair + s + 1) * batch)
                prev_r.append(xr)
                prev_i.append(xi)
                xr, xi = ar * xr - ai * xi + vr_scr[at, glob], ar * xi + ai * xr + vi_scr[at, glob]
            blk = slice(i * pair * batch, (i + 1) * pair * batch)
            xr_scr[blk, glob] = jnp.concatenate(prev_r, axis=0).astype(BF16)
            xi_scr[blk, glob] = jnp.concatenate(prev_i, axis=0).astype(BF16)
        hr_ref[:, glob] = xr
        hi_ref[:, glob] = xi

    y4 = []
    for n in range(T):
        lanes = slice(MXU_TILE * n, MXU_TILE * (n + 1))
        y4.append(_dot(xr_scr[:, lanes], woutr_ref[n]) + _dot(xi_scr[:, lanes], wouti_ref[n]) + direct[n])
    per_lag = [jnp.concatenate([y4[n][:, chan * j:chan * (j + 1)] for n in range(T)], axis=1)
               .reshape(blocks, batch, SSM_WIDTH) for j in range(R)]
    y = jnp.stack(per_lag, axis=1).reshape(rows, SSM_WIDTH) + d_ref[...] * u
    z = _gelu_tanh(y)
    out = z * _sigmoid(_dot(z.astype(BF16), wglu_ref[...].astype(BF16)))
    out_ref[...] = jnp.swapaxes(out.reshape(steps, batch, SSM_WIDTH), 0, 1).astype(BF16)


def _s5_mixer(u, h0_re, h0_im, maps, d_skip, w_glu, *, batch, length):
    a_re, a_im, winr, wini, woutr, wouti, direct = maps
    steps = min(length, S5_STEPS)
    if steps % SSM_BLOCK or length % steps:
        raise ValueError("sequence length must be a multiple of the S5 block")
    blocks = steps // SSM_BLOCK
    pair = max(1, min(blocks, BF16_ROWS // batch))
    if (pair * batch) % BF16_ROWS or blocks % pair:
        raise ValueError("S5 state stores need whole bf16 tiles")
    width = SSM_LANES // 2 if batch <= 8 else LANES
    brow = blocks * batch
    kern = functools.partial(_s5_block_kernel, batch=batch, steps=steps, pair=pair, width=width)
    st = jax.ShapeDtypeStruct((batch, SSM_LANES), F32)
    tile = _full((SSM_TILES, MXU_TILE, MXU_TILE))
    return pl.pallas_call(
        kern,
        grid=(length // steps,),
        in_specs=[pl.BlockSpec((batch, steps, SSM_WIDTH), lambda i: (0, i, 0)),
                  _full((batch, SSM_LANES)), _full((batch, SSM_LANES)),
                  _full((1, SSM_LANES)), _full((1, SSM_LANES)),
                  tile, tile, tile, tile, tile,
                  _full((1, SSM_WIDTH)), _full((SSM_WIDTH, SSM_WIDTH))],
        out_specs=[pl.BlockSpec((batch, steps, SSM_WIDTH), lambda i: (0, i, 0)),
                   _full((batch, SSM_LANES)), _full((batch, SSM_LANES))],
        out_shape=(jax.ShapeDtypeStruct((batch, length, SSM_WIDTH), BF16), st, st),
        scratch_shapes=[pltpu.VMEM((brow, SSM_LANES), F32), pltpu.VMEM((brow, SSM_LANES), F32),
                        pltpu.VMEM((brow, SSM_LANES), BF16), pltpu.VMEM((brow, SSM_LANES), BF16)],
        compiler_params=_params("arbitrary", fuse=_fuse(12, 1, 2, 3, 4, 11)),
        name="s5_mixer",
    )(u, h0_re, h0_im, a_re, a_im, winr, wini, woutr, wouti, direct, d_skip, w_glu)


def _ret_kernel(p_ref, g_ref, mask_ref, xi_ref, gain_ref, s0_ref, o_ref, s_ref,
                *, bb, gamma_c, unroll, chunk, chunks):
    @pl.when(pl.program_id(1) == 0)
    def _():
        s_ref[...] = s0_ref[...]

    heads = range(RET_HEADS)
    cols = [slice(h * RET_HEAD_DIM, (h + 1) * RET_HEAD_DIM) for h in heads]

    def one_chunk(b, rows):
        qb = [p_ref[b, rows, _ret_lanes(0, h)] for h in heads]
        vb = [p_ref[b, rows, _ret_lanes(3, h)] for h in heads]
        s_prev = [s_ref[b, h] for h in heads]
        scores = [(_dot_nt(qb[h], p_ref[b, rows, _ret_lanes(1, h)]) * mask_ref[h]).astype(BF16) for h in heads]
        cross = [_dot(qb[h], s_prev[h].astype(BF16)) * xi_ref[h] for h in heads]
        kv = [_dot_tn(p_ref[b, rows, _ret_lanes(2, h)], vb[h]) for h in heads]
        inner = [_dot(scores[h], vb[h]) for h in heads]
        outs = []
        for h in heads:
            s_ref[b, h] = s_prev[h] * gamma_c[h] + kv[h]
            o = inner[h] + cross[h]
            mu = jnp.mean(o, axis=-1, keepdims=True)
            oc = o - mu
            var = jnp.mean(oc * oc, axis=-1, keepdims=True)
            on = oc * lax.rsqrt(var + EPS) * gain_ref[:, cols[h]]
            gate = g_ref[b, rows, cols[h]]
            outs.append((gate * _sigmoid(gate) * on).astype(BF16))
        o_ref[b, rows, :] = jnp.concatenate(outs, axis=1)

    def per_batch(b, carry):
        for c in range(chunks):
            one_chunk(b, slice(c * chunk, (c + 1) * chunk))
        return carry

    if unroll == bb:
        for b in range(bb):
            per_batch(b, 0)
    else:
        lax.fori_loop(0, bb, per_batch, 0, unroll=unroll)


def _retention(packed, g, s0, gn_gain):
    batch, length, _ = g.shape
    padded = pl.cdiv(length, BF16_ROWS) * BF16_ROWS
    chunk, gamma_c, mask, xi = _decay_tables(length, padded)
    if padded != length:
        pad = lambda a: jnp.pad(a, ((0, 0), (0, padded - length), (0, 0)))
        packed, g = pad(packed), pad(g)
    bb, unroll = (8, 2) if chunk > BF16_ROWS else (16, 16)
    chunks = 2 if (padded // chunk) % 2 == 0 else 1
    tok = pl.BlockSpec((bb, chunks * chunk, RET_WIDTH), lambda i, c: (i, c, 0))
    state = pl.BlockSpec((bb, RET_HEADS, RET_HEAD_DIM, RET_HEAD_DIM), lambda i, c: (i, 0, 0, 0))
    out, s_new = pl.pallas_call(
        functools.partial(_ret_kernel, bb=bb, gamma_c=gamma_c, unroll=unroll, chunk=chunk, chunks=chunks),
        grid=(batch // bb, padded // (chunks * chunk)),
        in_specs=[pl.BlockSpec((bb, chunks * chunk, 4 * RET_WIDTH), lambda i, c: (i, c, 0)), tok,
                  _full((RET_HEADS, chunk, chunk)), _full((RET_HEADS, chunk, RET_HEAD_DIM)),
                  _full((1, RET_WIDTH)), state],
        out_specs=[tok, state],
        out_shape=(jax.ShapeDtypeStruct((batch, padded, RET_WIDTH), BF16),
                   jax.ShapeDtypeStruct(s0.shape, F32)),
        compiler_params=_params("parallel", "arbitrary", fuse=_fuse(6, 0, 1, 5)),
        name="retention",
    )(packed, g, mask, xi, gn_gain, s0)
    return out[:, :length], s_new


def _mixer_out(x_ref, ssm_ref, ret_ref, wout_ref, rows=slice(None)):
    return (x_ref[rows, :] + _dot(ssm_ref[rows, :], wout_ref[:SSM_WIDTH, :].astype(BF16))
            + _dot(ret_ref[rows, :], wout_ref[SSM_WIDTH:, :].astype(BF16)))


def _query(x1, g_ref, wq_ref):
    return (_dot(_rmsnorm(x1, g_ref[...]).astype(BF16), wq_ref[...].astype(BF16)) * MEM_SCALE).astype(BF16)


def _outq_kernel(x_ref, ssm_ref, ret_ref, wout_ref, g_ref, wq_ref, x1_ref, q_ref):
    x1 = _mixer_out(x_ref, ssm_ref, ret_ref, wout_ref)
    x1_ref[...] = x1
    q_ref[...] = _query(x1, g_ref, wq_ref)


def _out_and_query(x, ssm, ret, w_out, g_xattn, w_mq):
    t = x.shape[0]
    tm = min(t, ROWS_SMALL)
    row = lambda w: pl.BlockSpec((tm, w), lambda i: (i, 0))
    return pl.pallas_call(
        _outq_kernel,
        grid=(t // tm,),
        in_specs=[row(D_MODEL), row(SSM_WIDTH), row(RET_WIDTH), _full((D_MODEL, D_MODEL)),
                  _full((1, D_MODEL)), _full((D_MODEL, D_MODEL))],
        out_specs=[row(D_MODEL), row(D_MODEL)],
        out_shape=(jax.ShapeDtypeStruct((t, D_MODEL), F32), jax.ShapeDtypeStruct((t, D_MODEL), BF16)),
        compiler_params=_params("parallel", fuse=_fuse(6, 0, 3, 5)),
        name="out_proj_query",
    )(x, ssm, ret, w_out, g_xattn, w_mq)


def _cache_row_order(x):
    halves = MEM_HEAD_DIM // LANES
    tiles = [x[:, (h * halves + d) * LANES:(h * halves + d + 1) * LANES]
             for d in range(halves) for h in range(MEM_HEADS)]
    return jnp.swapaxes(jnp.stack(tiles, axis=0), 0, 1).reshape(x.shape[0] * CACHE_SUB, LANES)


MEMKV_SEQS = 2


def _memkv_kernel(m_ref, g_ref, wk_ref, wv_ref, k_ref, v_ref, kb_ref, vb_ref):
    seqs = m_ref.shape[0]
    m = _rmsnorm(m_ref[...].reshape(seqs * MEM_LEN, D_MODEL), g_ref[...]).astype(BF16)
    k = _dot(m, wk_ref[...])
    v = _dot(m, wv_ref[...])
    k_ref[...] = _cache_row_order(k).reshape(k_ref.shape)
    v_ref[...] = _cache_row_order(v).reshape(v_ref.shape)
    kb_ref[...] = k.astype(BF16).reshape(kb_ref.shape)
    vb_ref[...] = v.astype(BF16).reshape(vb_ref.shape)


def _memory_kv(mem, g_mem, w_mk, w_mv):
    batch = mem.shape[0]
    seqs = MEMKV_SEQS if batch % MEMKV_SEQS == 0 else 1
    tok = pl.BlockSpec((seqs, MEM_LEN, D_MODEL), lambda i: (i, 0, 0))
    cache = pl.BlockSpec((seqs, CACHE_ROWS, LANES), lambda i: (i, 0, 0))
    f = jax.ShapeDtypeStruct((batch, CACHE_ROWS, LANES), F32)
    h = jax.ShapeDtypeStruct((batch, MEM_LEN, D_MODEL), BF16)
    return pl.pallas_call(
        _memkv_kernel,
        grid=(batch // seqs,),
        in_specs=[tok, _full((1, D_MODEL)), _full((D_MODEL, D_MODEL)), _full((D_MODEL, D_MODEL))],
        out_specs=[cache, cache, tok, tok],
        out_shape=(f, f, h, h),
        compiler_params=_params("parallel", fuse=_fuse(4, 2, 3)),
        name="memory_kv",
    )(mem, g_mem, w_mk, w_mv)


ATTN_ROWS = 512


def _outattn_kernel(x_ref, ssm_ref, ret_ref, wout_ref, g_ref, wq_ref, k_ref, v_ref, x1_ref, o_ref):
    cols = [slice(h * MEM_HEAD_DIM, (h + 1) * MEM_HEAD_DIM) for h in range(MEM_HEADS)]
    groups = [slice(r, r + ATTN_ROWS) for r in range(0, x_ref.shape[0], ATTN_ROWS)]

    def scores_of(rows):
        x1 = _mixer_out(x_ref, ssm_ref, ret_ref, wout_ref, rows)
        x1_ref[rows, :] = x1
        q = _query(x1, g_ref, wq_ref)
        return [_dot_nt(q[:, c], k_ref[0, :, c]) for c in cols]

    def attend(rows, scores):
        outs = []
        for s, c in zip(scores, cols):
            e = jnp.exp(s - jnp.max(s, axis=-1, keepdims=True))
            p = (e / jnp.sum(e, axis=-1, keepdims=True)).astype(BF16)
            outs.append(_dot(p, v_ref[0, :, c]).astype(BF16))
        o_ref[rows, :] = jnp.concatenate(outs, axis=1)

    pending = None
    for rows in groups:
        scores = scores_of(rows)
        if pending is not None:
            attend(*pending)
        pending = (rows, scores)
    attend(*pending)


def _out_and_attention(x, ssm, ret, w_out, g_xattn, w_mq, mk, mv, *, length):
    t = x.shape[0]
    tm = min(length, ROWS_LARGE)
    per_seq = length // tm
    row = lambda w: pl.BlockSpec((tm, w), lambda i: (i, 0))
    mem = pl.BlockSpec((1, MEM_LEN, D_MODEL), lambda i: (i // per_seq, 0, 0))
    return pl.pallas_call(
        _outattn_kernel,
        grid=(t // tm,),
        in_specs=[row(D_MODEL), row(SSM_WIDTH), row(RET_WIDTH), _full((D_MODEL, D_MODEL)),
                  _full((1, D_MODEL)), _full((D_MODEL, D_MODEL)), mem, mem],
        out_specs=[row(D_MODEL), row(D_MODEL)],
        out_shape=(jax.ShapeDtypeStruct((t, D_MODEL), F32), jax.ShapeDtypeStruct((t, D_MODEL), BF16)),
        compiler_params=_params("parallel", fuse=_fuse(8, 3, 5)),
        name="out_proj_attention",
    )(x, ssm, ret, w_out, g_xattn, w_mq, mk, mv)


def _cached_attention(q_ref, k_ref, v_ref, o_ref, *, bb, length):
    hit = (lax.broadcasted_iota(jnp.int32, (CACHE_SUB, CACHE_ROWS), 0)
           == lax.broadcasted_iota(jnp.int32, (CACHE_SUB, CACHE_ROWS), 1) % CACHE_SUB)
    first_half = lax.broadcasted_iota(jnp.int32, (length, LANES), 1) % CACHE_SUB < MEM_HEADS
    tiles = CACHE_ROWS // LANES

    def class_reduce(x, op):
        shift = CACHE_SUB
        while shift < LANES:
            x = op(x, pltpu.roll(x, shift, axis=1))
            shift *= 2
        return x

    def softmax_rows(r):
        z = jnp.concatenate(
            [jnp.sum(jnp.where(hit, r[CACHE_SUB * t:CACHE_SUB * (t + 1)], 0.0), axis=0, keepdims=True)
             for t in range(length)], axis=0)
        parts = []
        for i in range(tiles):
            zi = z[:, LANES * i:LANES * (i + 1)]
            parts.append(zi + pltpu.roll(zi, LANES - MEM_HEADS, axis=1))
        mx = parts[0]
        for pi in parts[1:]:
            mx = jnp.maximum(mx, pi)
        mx = class_reduce(mx, jnp.maximum)
        es = [jnp.exp(pi - mx) for pi in parts]
        tot = es[0]
        for ei in es[1:]:
            tot = tot + ei
        tot = class_reduce(tot, jnp.add)
        ps = []
        for ei in es:
            pi = ei / tot
            ps.append(jnp.where(first_half, pi, pltpu.roll(pi, MEM_HEADS, axis=1)))
        p = jnp.concatenate(ps, axis=1)
        return jnp.concatenate(
            [jnp.where(hit, jnp.broadcast_to(p[t:t + 1], (CACHE_SUB, CACHE_ROWS)), 0.0)
             for t in range(length)], axis=0).astype(BF16)

    def start():
        return [_dot_nt(q_ref[b], k_ref[b].astype(BF16)) for b in range(bb)]

    def finish(scores):
        probs = [softmax_rows(r) for r in scores]
        for b in range(bb):
            o_ref[b] = _dot(probs[b], v_ref[b].astype(BF16)).astype(BF16)

    return start, finish


def _to_cache_rows(a, lead):
    halves = MEM_HEAD_DIM // LANES
    batch = a.size // (lead * D_MODEL)
    a = a.reshape(batch, lead, MEM_HEADS, halves, LANES).transpose(0, 1, 3, 2, 4)
    return a.reshape(batch, lead * CACHE_SUB, LANES)


def _from_cache_rows(a, lead):
    halves = MEM_HEAD_DIM // LANES
    batch = a.shape[0]
    a = a.reshape(batch, lead, halves, MEM_HEADS, LANES).transpose(0, 1, 3, 2, 4)
    return a.reshape(batch, lead, D_MODEL)


MLP_CHUNK = 1024


def _post_kernel(x1_ref, o_ref, wo_ref, gm_ref, wup_ref, wdn_ref, gf_ref, *rest, side_bb, side_len):
    if side_bb:
        qs_ref, ck_ref, cv_ref, y_ref, os_ref = rest
    else:
        (y_ref,) = rest
    x2 = x1_ref[...] + _dot(o_ref[...], wo_ref[...].astype(BF16))
    if side_bb:
        side_start, side_finish = _cached_attention(qs_ref, ck_ref, cv_ref, os_ref, bb=side_bb, length=side_len)
        side_scores = side_start()
    h = _rmsnorm(x2, gm_ref[...]).astype(BF16)
    chunks = [slice(c, c + MLP_CHUNK) for c in range(0, D_FF, MLP_CHUNK)]
    acc = x2
    up = _dot(h, wup_ref[:, chunks[0]])
    for c, cols in enumerate(chunks):
        nxt = _dot(h, wup_ref[:, chunks[c + 1]]) if c + 1 < len(chunks) else None
        if side_bb and c == len(chunks) - 1:
            side_finish(side_scores)
        a = jnp.maximum(up, 0.0)
        acc = acc + _dot((a * a).astype(BF16), wdn_ref[cols, :])
        up = nxt
    y_ref[...] = _rmsnorm(acc, gf_ref[...])


def _attn_out_mlp(x1, o, w_mo, g_mlp, w_up, w_down, g_final, side=None):
    t = x1.shape[0]
    tm = min(t, ROWS_SMALL)
    steps = t // tm
    row = pl.BlockSpec((tm, D_MODEL), lambda i: (i, 0))
    in_specs = [row, row, _full((D_MODEL, D_MODEL)), _full((1, D_MODEL)),
                _full((D_MODEL, D_FF)), _full((D_FF, D_MODEL)), _full((1, D_MODEL))]
    out_specs = [row]
    out_shape = [jax.ShapeDtypeStruct((t, D_MODEL), F32)]
    args = [x1, o, w_mo, g_mlp, w_up, w_down, g_final]
    side_bb = side_len = 0
    if side is not None:
        q_rows = side[0]
        side_bb, rem = divmod(q_rows.shape[0], steps)
        if rem or not side_bb:
            raise ValueError("side attention sequences must spread evenly over the grid steps")
        side_len = q_rows.shape[1] // CACHE_SUB
        blk = lambda r: pl.BlockSpec((side_bb, r, LANES), lambda i: (i, 0, 0))
        in_specs += [blk(q_rows.shape[1]), blk(CACHE_ROWS), blk(CACHE_ROWS)]
        out_specs.append(blk(q_rows.shape[1]))
        out_shape.append(jax.ShapeDtypeStruct(q_rows.shape, BF16))
        args += list(side)
    outs = pl.pallas_call(
        functools.partial(_post_kernel, side_bb=side_bb, side_len=side_len),
        grid=(steps,),
        in_specs=in_specs,
        out_specs=out_specs,
        out_shape=out_shape,
        compiler_params=_params("parallel", fuse=_fuse(len(in_specs), 1, 2, 7)),
        name="attn_out_mlp",
    )(*args)
    return outs if side is not None else outs[0]


def _mixers(proj, batch, length, s5_re, s5_im, ret_s, p):
    tokens = batch * length
    u, packed, g = proj
    ssm, s5_re_new, s5_im_new = _s5_mixer(
        u.reshape(batch, length, SSM_WIDTH), s5_re.reshape(batch, SSM_LANES), s5_im.reshape(batch, SSM_LANES),
        p["s5_maps"], p["d_skip"], p["w_glu"], batch=batch, length=length)
    tok3 = lambda a: a.reshape(batch, length, RET_WIDTH)
    ret, ret_new = _retention(packed.reshape(batch, length, 4 * RET_WIDTH), tok3(g), ret_s, p["ret_gn"])
    states = (s5_re_new.reshape(1, batch, SSM_GROUPS, SSM_STATE),
              s5_im_new.reshape(1, batch, SSM_GROUPS, SSM_STATE), ret_new[None])
    return ssm.reshape(tokens, SSM_WIDTH), ret.reshape(tokens, RET_WIDTH), states


def kernel(x_prompt, x_sample, mem_prompt, state_s5_re, state_s5_im, state_ret, cache_mem_k, cache_mem_v, g_mix, w_in, lam_re, lam_im, log_dt, b_re, b_im, c_re, c_im, d_skip, w_glu, ret_gn, w_out, g_xattn, g_mem, w_mq, w_mk, w_mv, w_mo, g_mlp, w_up, w_down, g_final):
    if g_mix.shape[0] != 1:
        raise ValueError("this kernel implements the single-layer configuration")
    bp, lp, _ = x_prompt.shape
    bs, ls, _ = x_sample.shape
    vec = lambda a: a.reshape(1, -1).astype(F32)
    xp = x_prompt.reshape(bp * lp, D_MODEL)
    xs = x_sample.reshape(bs * ls, D_MODEL)
    proj_p, (w_in_b, w_up_b, w_down_b) = _project(
        xp, vec(g_mix[0]), w_in[0], length=lp, pos0=0.0, casts=(w_up[0], w_down[0]))
    w_out_b, w_mq_b, w_mo_b, w_mk_b, w_mv_b, w_glu_b = (
        w[0].astype(BF16) for w in (w_out, w_mq, w_mo, w_mk, w_mv, w_glu))
    proj_s, _ = _project(xs, vec(g_mix[0]), w_in_b, length=ls, pos0=float(PAST_LEN))
    p = dict(
        s5_maps=_s5_block_maps(lam_re[0], lam_im[0], log_dt[0], b_re[0], b_im[0], c_re[0], c_im[0]),
        d_skip=vec(d_skip[0]), w_glu=w_glu_b, ret_gn=vec(ret_gn[0]))
    g_x = vec(g_xattn[0])
    mlp = (w_mo_b, vec(g_mlp[0]), w_up_b, w_down_b, vec(g_final))

    ssm_s, ret_s, states_s = _mixers(proj_s, bs, ls, state_s5_re[0], state_s5_im[0], state_ret[0],
                                     dict(p, w_glu=w_glu[0]))
    x1_s, q_s = _out_and_query(xs, ssm_s, ret_s, w_out[0], g_x, w_mq[0])
    if (ls * CACHE_SUB) % BF16_ROWS:
        raise ValueError("cached attention needs whole bf16 tiles of query rows")
    side = (_to_cache_rows(q_s, ls), _to_cache_rows(cache_mem_k, MEM_LEN), _to_cache_rows(cache_mem_v, MEM_LEN))

    mk_rows, mv_rows, mk_p, mv_p = _memory_kv(mem_prompt, vec(g_mem[0]), w_mk_b, w_mv_b)
    zs = jnp.zeros((bp, SSM_GROUPS, SSM_STATE), F32)
    zr = jnp.zeros((bp, RET_HEADS, RET_HEAD_DIM, RET_HEAD_DIM), F32)
    ssm_p, ret_p, states_p = _mixers(proj_p, bp, lp, zs, zs, zr, p)
    x1_p, o_p = _out_and_attention(xp, ssm_p, ret_p, w_out_b, g_x, w_mq_b, mk_p, mv_p, length=lp)
    y_p, o_s = _attn_out_mlp(x1_p, o_p, *mlp, side=side)
    y_s = _attn_out_mlp(x1_s, _from_cache_rows(o_s, ls).reshape(bs * ls, D_MODEL), w_mo[0], *mlp[1:])

    kv5 = lambda a: _from_cache_rows(a, MEM_LEN).reshape(1, bp, MEM_LEN, MEM_HEADS, MEM_HEAD_DIM)
    return (y_p.reshape(bp, lp, D_MODEL), y_s.reshape(bs, ls, D_MODEL), *states_p, kv5(mk_rows), kv5(mv_rows),
            *states_s)
```

```python
import functools
import math

import numpy as np
import jax
import jax.numpy as jnp
from jax import lax
from jax.experimental import pallas as pl
from jax.experimental.pallas import tpu as pltpu

F32 = jnp.float32
BF16 = jnp.bfloat16

D_MODEL = 1024
SSM_WIDTH = 512
SSM_GROUP = 16
SSM_GROUPS = 32
SSM_STATE = 64
SSM_LANES = SSM_GROUPS * SSM_STATE
RET_WIDTH = 512
RET_HEADS = 4
RET_HEAD_DIM = 128
RET_CHUNK = 128
ROPE_BASE = 10000.0
MEM_LEN = 256
MEM_HEADS = 4
MEM_HEAD_DIM = 256
MEM_SCALE = MEM_HEAD_DIM ** -0.5
D_FF = 4096
PROJ_WIDTH = SSM_WIDTH + 4 * RET_WIDTH
EPS = 1e-6
PAST_LEN = 16384

LANES = 128
MXU_TILE = 256
BF16_ROWS = 16
VMEM_LIMIT = 56 * 1024 * 1024

ROWS_LARGE = 1024
ROWS_SMALL = 512
S5_STEPS = 256

CACHE_ROWS = MEM_LEN * MEM_HEADS * MEM_HEAD_DIM // LANES
CACHE_SUB = MEM_HEADS * MEM_HEAD_DIM // LANES

assert math.frexp(MEM_SCALE)[0] == 0.5


def _dot(a, b):
    return jnp.dot(a, b, preferred_element_type=F32)


def _dot_nt(a, b):
    return lax.dot_general(a, b, (((1,), (1,)), ((), ())), preferred_element_type=F32)


def _dot_tn(a, b):
    return lax.dot_general(a, b, (((0,), (0,)), ((), ())), preferred_element_type=F32)


def _rmsnorm(x, g):
    return x * lax.rsqrt(jnp.mean(x * x, axis=-1, keepdims=True) + EPS) * g


def _sigmoid(x):
    return 1.0 / (1.0 + jnp.exp(-x))


def _gelu_tanh(x):
    c = math.sqrt(2.0 / math.pi)
    return x * (0.5 * jnp.tanh(x * (c + (c * 0.044715) * (x * x))) + 0.5)


def _params(*sem, fuse=None):
    return pltpu.CompilerParams(dimension_semantics=sem, vmem_limit_bytes=VMEM_LIMIT, allow_input_fusion=fuse)


def _fuse(n_inputs, *which):
    return [i in which for i in range(n_inputs)]


def _full(shape):
    return pl.BlockSpec(shape, lambda *_: (0,) * len(shape))


SSM_BLOCK = 4
SSM_TILE_GROUPS = MXU_TILE // SSM_STATE
SSM_TILES = SSM_GROUPS // SSM_TILE_GROUPS
SSM_TILE_CHANNELS = SSM_TILE_GROUPS * SSM_GROUP
GROUP_SHIFT = SSM_GROUP.bit_length() - 1
STATE_SHIFT = SSM_STATE.bit_length() - 1

assert SSM_BLOCK * SSM_TILE_CHANNELS == MXU_TILE and SSM_TILE_CHANNELS == SSM_STATE


def _s5_prep_kernel(lr_ref, li_ref, ldt_ref, br_ref, bi_ref, cr_ref, ci_ref,
                    ar_ref, ai_ref, winr_ref, wini_ref, woutr_ref, wouti_ref, kdir_ref):
    lr = lr_ref[...]
    li = li_ref[...]
    dt = jnp.exp(ldt_ref[...])
    mag = jnp.exp(lr * dt)
    ab_re = mag * jnp.cos(li * dt)
    ab_im = mag * jnp.sin(li * dt)
    den = lr * lr + li * li
    f_re = ((ab_re - 1.0) * lr + ab_im * li) / den
    f_im = (ab_im * lr - (ab_re - 1.0) * li) / den
    bb_re = f_re * br_ref[...] - f_im * bi_ref[...]
    bb_im = f_re * bi_ref[...] + f_im * br_ref[...]
    cr = cr_ref[...]
    ci = ci_ref[...]
    pows = [(jnp.ones_like(ab_re), jnp.zeros_like(ab_re))]
    for _ in range(SSM_BLOCK):
        pr, pi = pows[-1]
        pows.append((pr * ab_re - pi * ab_im, pr * ab_im + pi * ab_re))
    ar_ref[...], ai_ref[...] = pows[SSM_BLOCK]

    R, T, TG = SSM_BLOCK, SSM_TILES, SSM_TILE_GROUPS
    rows = SSM_GROUPS * SSM_GROUP
    side = SSM_TILE_CHANNELS
    exact = dict(precision=lax.Precision.HIGHEST, preferred_element_type=F32)
    iota = lambda shape, d: lax.broadcasted_iota(jnp.int32, shape, d)
    rep4 = lambda a, axis: jnp.concatenate([a] * TG, axis=axis)
    tiles = lambda a: a.reshape(T, side, a.shape[-1])
    group_of = lambda i: (i & (side - 1)) >> GROUP_SHIFT

    own_in = group_of(iota((rows, MXU_TILE), 0)) == iota((rows, MXU_TILE), 1) >> STATE_SHIFT
    for s in range(R):
        pr, pi = pows[R - 1 - s]
        for ref, w in ((winr_ref, pr * bb_re - pi * bb_im), (wini_ref, pr * bb_im + pi * bb_re)):
            ref[:, side * s:side * (s + 1), :] = tiles(jnp.where(own_in, rep4(w, 1), 0.0)).astype(BF16)

    own_out = iota((MXU_TILE, rows), 0) >> STATE_SHIFT == group_of(iota((MXU_TILE, rows), 1))
    for ref, real in ((woutr_ref, True), (wouti_ref, False)):
        per_lag = []
        for j in range(R):
            pr, pi = pows[j + 1]
            w = cr * pr - ci * pi if real else -(cr * pi + ci * pr)
            wt = w.T
            per_lag.append(jnp.where(own_out, rep4(wt, 0), 0.0))
        for n in range(T):
            ref[n] = jnp.concatenate([m[:, side * n:side * (n + 1)] for m in per_lag], axis=1).astype(BF16)

    grouped = lambda a: a.reshape(SSM_GROUPS, SSM_GROUP, SSM_STATE)
    contract = lambda a, b: lax.dot_general(a, b, (((2,), (2,)), ((0,), (0,))), **exact)
    own_dir = group_of(iota((rows, side), 0)) == iota((rows, side), 1) >> GROUP_SHIFT
    by_lag = []
    for d in range(R):
        pr, pi = pows[d]
        er = grouped(pr * bb_re - pi * bb_im)
        ei = grouped(pr * bb_im + pi * bb_re)
        kt = (contract(er, grouped(cr)) - contract(ei, grouped(ci))).reshape(rows, SSM_GROUP)
        by_lag.append(tiles(jnp.where(own_dir, rep4(kt, 1), 0.0)))
    nothing = jnp.zeros_like(by_lag[0])
    for s in range(R):
        kdir_ref[:, side * s:side * (s + 1), :] = jnp.concatenate(
            [by_lag[j - s] if j >= s else nothing for j in range(R)], axis=2).astype(BF16)


def _s5_block_maps(lam_re, lam_im, log_dt, b_re, b_im, c_re, c_im):
    rows = SSM_GROUPS * SSM_GROUP
    rep = lambda a: jnp.repeat(a, SSM_GROUP, axis=0)
    ldt = jnp.broadcast_to(log_dt[:, None], (SSM_GROUPS, SSM_STATE))
    bt = lambda b: b.transpose(0, 2, 1).reshape(rows, SSM_STATE)
    ct = lambda c: c.reshape(rows, SSM_STATE)
    one = jax.ShapeDtypeStruct((rows, SSM_STATE), F32)
    tile = jax.ShapeDtypeStruct((SSM_TILES, MXU_TILE, MXU_TILE), BF16)
    ar, ai, winr, wini, woutr, wouti, direct = pl.pallas_call(
        _s5_prep_kernel,
        out_shape=(one, one, tile, tile, tile, tile, tile),
        compiler_params=pltpu.CompilerParams(vmem_limit_bytes=VMEM_LIMIT),
        name="s5_block_maps",
    )(rep(lam_re), rep(lam_im), rep(ldt), bt(b_re), bt(b_im), ct(c_re), ct(c_im))
    a_re = ar[::SSM_GROUP].reshape(1, SSM_LANES)
    a_im = ai[::SSM_GROUP].reshape(1, SSM_LANES)
    return a_re, a_im, winr, wini, woutr, wouti, direct


def _retention_chunk(length):
    return RET_CHUNK if length % RET_CHUNK == 0 else length


def _log_gamma():
    return np.log(1.0 - 2.0 ** (-5.0 - np.arange(RET_HEADS, dtype=np.float64)))


def _rope_tables(length, pos0):
    half = RET_HEAD_DIM // 2
    inv = ROPE_BASE ** (-np.arange(half, dtype=np.float64) / half)
    ang = (pos0 + np.arange(length, dtype=np.float64))[:, None] * inv[None, :]
    return (np.concatenate([np.cos(ang), np.cos(ang)], axis=1),
            np.concatenate([-np.sin(ang), np.sin(ang)], axis=1))


def _zeta_rows(length):
    chunk = _retention_chunk(length)
    idx = np.arange(chunk, dtype=np.float64)
    zeta = np.exp((chunk - 1.0 - idx)[None, :] * _log_gamma()[:, None])
    return np.concatenate([np.broadcast_to(z[:, None], (chunk, RET_HEAD_DIM)) for z in zeta], axis=1)


def _decay_tables(length, padded):
    chunk = _retention_chunk(length)
    cpad = chunk if padded == length else padded
    lg = _log_gamma()
    idx = np.arange(chunk, dtype=np.float64)
    diff = idx[:, None] - idx[None, :]
    mask = np.where(diff[None] >= 0, np.exp(np.maximum(diff, 0.0)[None] * lg[:, None, None]), 0.0)
    xi = np.exp((idx + 1.0)[None, :] * lg[:, None])
    gamma_c = tuple(float(v) for v in np.exp(chunk * lg))
    mask = np.pad(mask, ((0, 0), (0, cpad - chunk), (0, cpad - chunk)))
    xi = np.broadcast_to(np.pad(xi, ((0, 0), (0, cpad - chunk)))[:, :, None], (RET_HEADS, cpad, RET_HEAD_DIM))
    f = lambda a: jnp.asarray(np.ascontiguousarray(a), dtype=F32)
    return cpad, gamma_c, f(mask), f(xi)


def _ret_lanes(part, head):
    lo = part * RET_WIDTH + head * RET_HEAD_DIM
    return slice(lo, lo + RET_HEAD_DIM)


def _proj_kernel(x_ref, g_ref, w_ref, cc_ref, ss_ref, zeta_ref, *rest, n_cast):
    casts_in, rest = rest[:n_cast], rest[n_cast:]
    u_ref, ret_ref, gate_ref = rest[:3]
    w = w_ref[...]
    if n_cast:
        w = w.astype(BF16)
        for src, dst in zip(casts_in, rest[3:]):
            dst[...] = src[...].astype(BF16)
    h = _rmsnorm(x_ref[...], g_ref[...]).astype(BF16)
    proj = _dot(h, w)
    u_ref[...] = proj[:, :SSM_WIDTH]
    ret_ref[:, 3 * RET_WIDTH:] = proj[:, SSM_WIDTH + 2 * RET_WIDTH:SSM_WIDTH + 3 * RET_WIDTH].astype(BF16)
    gate_ref[...] = proj[:, SSM_WIDTH + 3 * RET_WIDTH:]
    cc = cc_ref[...]
    ss = ss_ref[...]
    rope = lambda a: a * cc + pltpu.roll(a, RET_HEAD_DIM // 2, axis=1) * ss
    for hd in range(RET_HEADS):
        cols = slice(hd * RET_HEAD_DIM, (hd + 1) * RET_HEAD_DIM)
        q = rope(proj[:, SSM_WIDTH + hd * RET_HEAD_DIM:SSM_WIDTH + (hd + 1) * RET_HEAD_DIM])
        k = rope(proj[:, SSM_WIDTH + RET_WIDTH + hd * RET_HEAD_DIM:SSM_WIDTH + RET_WIDTH + (hd + 1) * RET_HEAD_DIM])
        ret_ref[:, _ret_lanes(0, hd)] = (q * (RET_HEAD_DIM ** -0.5)).astype(BF16)
        ret_ref[:, _ret_lanes(1, hd)] = k.astype(BF16)
        ret_ref[:, _ret_lanes(2, hd)] = (k * zeta_ref[:, cols]).astype(BF16)


def _project(x, g_mix, w_in, *, length, pos0, casts=()):
    t = x.shape[0]
    tm = min(t, ROWS_LARGE)
    steps = t // tm
    chunk = _retention_chunk(length)
    casts = (w_in, *casts) if casts else ()
    if tm % chunk or (length % tm and tm % length):
        raise ValueError("token tiles must hold whole retention chunks of whole or repeated sequences")
    cc, ss = _rope_tables(length, pos0)
    if length < tm:
        cc, ss = np.tile(cc, (tm // length, 1)), np.tile(ss, (tm // length, 1))
    tab_blocks = cc.shape[0] // tm
    zeta = np.tile(_zeta_rows(length), (tm // chunk, 1))
    const = lambda a: jnp.asarray(np.ascontiguousarray(a), dtype=F32)
    row = lambda w: pl.BlockSpec((tm, w), lambda i: (i, 0))
    tab = pl.BlockSpec((tm, RET_HEAD_DIM), lambda i: (i % tab_blocks, 0))
    f = jax.ShapeDtypeStruct((t, SSM_WIDTH), F32)
    h = jax.ShapeDtypeStruct((t, 4 * RET_WIDTH), BF16)
    in_specs = [row(D_MODEL), _full((1, D_MODEL)), _full((D_MODEL, PROJ_WIDTH)), tab, tab, _full((tm, RET_WIDTH))]
    out_specs = [row(SSM_WIDTH), row(4 * RET_WIDTH), row(RET_WIDTH)]
    out_shape = [f, h, f]
    if casts:
        for c in casts:
            rows, rem = divmod(c.shape[0], steps)
            if rem or rows % BF16_ROWS:
                raise ValueError("side-cast weights must split into whole bf16 tiles per grid step")
            blk = pl.BlockSpec((rows, c.shape[1]), lambda i: (i, 0))
            in_specs.append(blk)
            out_specs.append(blk)
            out_shape.append(jax.ShapeDtypeStruct(c.shape, BF16))
    outs = pl.pallas_call(
        functools.partial(_proj_kernel, n_cast=len(casts)),
        grid=(steps,),
        in_specs=in_specs,
        out_specs=out_specs,
        out_shape=out_shape,
        compiler_params=_params("arbitrary" if casts else "parallel", fuse=_fuse(len(in_specs), 0)),
        name="in_proj",
    )(x, g_mix, w_in, const(cc), const(ss), const(zeta), *casts)
    return outs[:3], outs[3:]


def _s5_block_kernel(u_ref, h0r_ref, h0i_ref, ar_ref, ai_ref, winr_ref, wini_ref, woutr_ref, wouti_ref,
                     kdir_ref, d_ref, wglu_ref, out_ref, hr_ref, hi_ref, vr_scr, vi_scr, xr_scr, xi_scr,
                     *, batch, steps, pair, width):
    R, T = SSM_BLOCK, SSM_TILES
    chan = SSM_TILE_CHANNELS
    blocks = steps // R
    rows, brow = steps * batch, blocks * batch

    @pl.when(pl.program_id(0) == 0)
    def _():
        hr_ref[...] = h0r_ref[...]
        hi_ref[...] = h0i_ref[...]

    u = jnp.swapaxes(u_ref[...], 0, 1).reshape(rows, SSM_WIDTH)
    u4 = u.reshape(blocks, R, batch, SSM_WIDTH)
    lag = [u4[:, s].reshape(brow, SSM_WIDTH) for s in range(R)]
    direct = []
    for n in range(T):
        un = jnp.concatenate([lag[s][:, chan * n:chan * (n + 1)] for s in range(R)], axis=1).astype(BF16)
        lanes = slice(MXU_TILE * n, MXU_TILE * (n + 1))
        vr_scr[:, lanes] = _dot(un, winr_ref[n])
        vi_scr[:, lanes] = _dot(un, wini_ref[n])
        direct.append(_dot(un, kdir_ref[n]))

    for j in range(SSM_LANES // width):
        glob = slice(j * width, (j + 1) * width)
        ar = jnp.broadcast_to(ar_ref[:, glob], (batch, width))
        ai = jnp.broadcast_to(ai_ref[:, glob], (batch, width))
        xr = hr_ref[:, glob]
        xi = hi_ref[:, glob]
        for i in range(blocks // pair):
            prev_r, prev_i = [], []
            for s in range(pair):
                at = slice((i * pair + s) * batch, (i * pair + s + 1) * batch)
                prev_r.append(xr)
                prev_i.append(xi)
                xr, xi = ar * xr - ai * xi + vr_scr[at, glob], ar * xi + ai * xr + vi_scr[at, glob]
            blk = slice(i * pair * batch, (i + 1) * pair * batch)
            xr_scr[blk, glob] = jnp.concatenate(prev_r, axis=0).astype(BF16)
            xi_scr[blk, glob] = jnp.concatenate(prev_i, axis=0).astype(BF16)
        hr_ref[:, glob] = xr
        hi_ref[:, glob] = xi

    y4 = []
    for n in range(T):
        lanes = slice(MXU_TILE * n, MXU_TILE * (n + 1))
        y4.append(_dot(xr_scr[:, lanes], woutr_ref[n]) + _dot(xi_scr[:, lanes], wouti_ref[n]) + direct[n])
    per_lag = [jnp.concatenate([y4[n][:, chan * j:chan * (j + 1)] for n in range(T)], axis=1)
               .reshape(blocks, batch, SSM_WIDTH) for j in range(R)]
    y = jnp.stack(per_lag, axis=1).reshape(rows, SSM_WIDTH) + d_ref[...] * u
    z = _gelu_tanh(y)
    out = z * _sigmoid(_dot(z.astype(BF16), wglu_ref[...].astype(BF16)))
    out_ref[...] = jnp.swapaxes(out.reshape(steps, batch, SSM_WIDTH), 0, 1).astype(BF16)


def _s5_mixer(u, h0_re, h0_im, maps, d_skip, w_glu, *, batch, length):
    a_re, a_im, winr, wini, woutr, wouti, direct = maps
    steps = min(length, S5_STEPS)
    if steps % SSM_BLOCK or length % steps:
        raise ValueError("sequence length must be a multiple of the S5 block")
    blocks = steps // SSM_BLOCK
    pair = max(1, min(blocks, BF16_ROWS // batch))
    if (pair * batch) % BF16_ROWS or blocks % pair:
        raise ValueError("S5 state stores need whole bf16 tiles")
    width = SSM_LANES // 2 if batch <= 8 else LANES
    brow = blocks * batch
    kern = functools.partial(_s5_block_kernel, batch=batch, steps=steps, pair=pair, width=width)
    st = jax.ShapeDtypeStruct((batch, SSM_LANES), F32)
    tile = _full((SSM_TILES, MXU_TILE, MXU_TILE))
    return pl.pallas_call(
        kern,
        grid=(length // steps,),
        in_specs=[pl.BlockSpec((batch, steps, SSM_WIDTH), lambda i: (0, i, 0)),
                  _full((batch, SSM_LANES)), _full((batch, SSM_LANES)),
                  _full((1, SSM_LANES)), _full((1, SSM_LANES)),
                  tile, tile, tile, tile, tile,
                  _full((1, SSM_WIDTH)), _full((SSM_WIDTH, SSM_WIDTH))],
        out_specs=[pl.BlockSpec((batch, steps, SSM_WIDTH), lambda i: (0, i, 0)),
                   _full((batch, SSM_LANES)), _full((batch, SSM_LANES))],
        out_shape=(jax.ShapeDtypeStruct((batch, length, SSM_WIDTH), BF16), st, st),
        scratch_shapes=[pltpu.VMEM((brow, SSM_LANES), F32), pltpu.VMEM((brow, SSM_LANES), F32),
                        pltpu.VMEM((brow, SSM_LANES), BF16), pltpu.VMEM((brow, SSM_LANES), BF16)],
        compiler_params=_params("arbitrary", fuse=_fuse(12, 11)),
        name="s5_mixer",
    )(u, h0_re, h0_im, a_re, a_im, winr, wini, woutr, wouti, direct, d_skip, w_glu)


def _ret_kernel(p_ref, g_ref, mask_ref, xi_ref, gain_ref, s0_ref, o_ref, s_ref,
                *, bb, gamma_c, unroll, chunk, chunks):
    @pl.when(pl.program_id(1) == 0)
    def _():
        s_ref[...] = s0_ref[...]

    heads = range(RET_HEADS)
    cols = [slice(h * RET_HEAD_DIM, (h + 1) * RET_HEAD_DIM) for h in heads]

    def one_chunk(b, rows):
        qb = [p_ref[b, rows, _ret_lanes(0, h)] for h in heads]
        vb = [p_ref[b, rows, _ret_lanes(3, h)] for h in heads]
        s_prev = [s_ref[b, h] for h in heads]
        scores = [(_dot_nt(qb[h], p_ref[b, rows, _ret_lanes(1, h)]) * mask_ref[h]).astype(BF16) for h in heads]
        cross = [_dot(qb[h], s_prev[h].astype(BF16)) * xi_ref[h] for h in heads]
        kv = [_dot_tn(p_ref[b, rows, _ret_lanes(2, h)], vb[h]) for h in heads]
        inner = [_dot(scores[h], vb[h]) for h in heads]
        outs = []
        for h in heads:
            s_ref[b, h] = s_prev[h] * gamma_c[h] + kv[h]
            o = inner[h] + cross[h]
            mu = jnp.mean(o, axis=-1, keepdims=True)
            oc = o - mu
            var = jnp.mean(oc * oc, axis=-1, keepdims=True)
            on = oc * lax.rsqrt(var + EPS) * gain_ref[:, cols[h]]
            gate = g_ref[b, rows, cols[h]]
            outs.append((gate * _sigmoid(gate) * on).astype(BF16))
        o_ref[b, rows, :] = jnp.concatenate(outs, axis=1)

    def per_batch(b, carry):
        for c in range(chunks):
            one_chunk(b, slice(c * chunk, (c + 1) * chunk))
        return carry

    if unroll == bb:
        for b in range(bb):
            per_batch(b, 0)
    else:
        lax.fori_loop(0, bb, per_batch, 0, unroll=unroll)


def _retention(packed, g, s0, gn_gain):
    batch, length, _ = g.shape
    padded = pl.cdiv(length, BF16_ROWS) * BF16_ROWS
    chunk, gamma_c, mask, xi = _decay_tables(length, padded)
    if padded != length:
        pad = lambda a: jnp.pad(a, ((0, 0), (0, padded - length), (0, 0)))
        packed, g = pad(packed), pad(g)
    bb, unroll = (8, 2) if chunk > BF16_ROWS else (32, 32)
    chunks = 2 if (padded // chunk) % 2 == 0 else 1
    tok = pl.BlockSpec((bb, chunks * chunk, RET_WIDTH), lambda i, c: (i, c, 0))
    state = pl.BlockSpec((bb, RET_HEADS, RET_HEAD_DIM, RET_HEAD_DIM), lambda i, c: (i, 0, 0, 0))
    out, s_new = pl.pallas_call(
        functools.partial(_ret_kernel, bb=bb, gamma_c=gamma_c, unroll=unroll, chunk=chunk, chunks=chunks),
        grid=(batch // bb, padded // (chunks * chunk)),
        in_specs=[pl.BlockSpec((bb, chunks * chunk, 4 * RET_WIDTH), lambda i, c: (i, c, 0)), tok,
                  _full((RET_HEADS, chunk, chunk)), _full((RET_HEADS, chunk, RET_HEAD_DIM)),
                  _full((1, RET_WIDTH)), state],
        out_specs=[tok, state],
        out_shape=(jax.ShapeDtypeStruct((batch, padded, RET_WIDTH), BF16),
                   jax.ShapeDtypeStruct(s0.shape, F32)),
        compiler_params=_params("parallel", "arbitrary", fuse=_fuse(6, 0, 1)),
        name="retention",
    )(packed, g, mask, xi, gn_gain, s0)
    return out[:, :length], s_new


def _mixer_out(x_ref, ssm_ref, ret_ref, wout_ref, rows=slice(None)):
    return (x_ref[rows, :] + _dot(ssm_ref[rows, :], wout_ref[:SSM_WIDTH, :].astype(BF16))
            + _dot(ret_ref[rows, :], wout_ref[SSM_WIDTH:, :].astype(BF16)))


def _query(x1, g_ref, wq_ref):
    return (_dot(_rmsnorm(x1, g_ref[...]).astype(BF16), wq_ref[...].astype(BF16)) * MEM_SCALE).astype(BF16)


def _outq_kernel(x_ref, ssm_ref, ret_ref, wout_ref, g_ref, wq_ref, x1_ref, q_ref):
    x1 = _mixer_out(x_ref, ssm_ref, ret_ref, wout_ref)
    x1_ref[...] = x1
    q_ref[...] = _query(x1, g_ref, wq_ref)


def _out_and_query(x, ssm, ret, w_out, g_xattn, w_mq):
    t = x.shape[0]
    tm = min(t, ROWS_SMALL)
    row = lambda w: pl.BlockSpec((tm, w), lambda i: (i, 0))
    return pl.pallas_call(
        _outq_kernel,
        grid=(t // tm,),
        in_specs=[row(D_MODEL), row(SSM_WIDTH), row(RET_WIDTH), _full((D_MODEL, D_MODEL)),
                  _full((1, D_MODEL)), _full((D_MODEL, D_MODEL))],
        out_specs=[row(D_MODEL), row(D_MODEL)],
        out_shape=(jax.ShapeDtypeStruct((t, D_MODEL), F32), jax.ShapeDtypeStruct((t, D_MODEL), BF16)),
        compiler_params=_params("parallel", fuse=_fuse(6, 0, 3, 5)),
        name="out_proj_query",
    )(x, ssm, ret, w_out, g_xattn, w_mq)


def _cache_row_order(x):
    halves = MEM_HEAD_DIM // LANES
    tiles = [x[:, (h * halves + d) * LANES:(h * halves + d + 1) * LANES]
             for d in range(halves) for h in range(MEM_HEADS)]
    return jnp.swapaxes(jnp.stack(tiles, axis=0), 0, 1).reshape(x.shape[0] * CACHE_SUB, LANES)


MEMKV_SEQS = 4


def _memkv_kernel(m_ref, g_ref, wk_ref, wv_ref, k_ref, v_ref, kb_ref, vb_ref):
    seqs = m_ref.shape[0]
    m = _rmsnorm(m_ref[...].reshape(seqs * MEM_LEN, D_MODEL), g_ref[...]).astype(BF16)
    k = _dot(m, wk_ref[...])
    v = _dot(m, wv_ref[...])
    k_ref[...] = _cache_row_order(k).reshape(k_ref.shape)
    v_ref[...] = _cache_row_order(v).reshape(v_ref.shape)
    kb_ref[...] = k.astype(BF16).reshape(kb_ref.shape)
    vb_ref[...] = v.astype(BF16).reshape(vb_ref.shape)


def _memory_kv(mem, g_mem, w_mk, w_mv):
    batch = mem.shape[0]
    seqs = MEMKV_SEQS if batch % MEMKV_SEQS == 0 else 1
    tok = pl.BlockSpec((seqs, MEM_LEN, D_MODEL), lambda i: (i, 0, 0))
    cache = pl.BlockSpec((seqs, CACHE_ROWS, LANES), lambda i: (i, 0, 0))
    f = jax.ShapeDtypeStruct((batch, CACHE_ROWS, LANES), F32)
    h = jax.ShapeDtypeStruct((batch, MEM_LEN, D_MODEL), BF16)
    return pl.pallas_call(
        _memkv_kernel,
        grid=(batch // seqs,),
        in_specs=[tok, _full((1, D_MODEL)), _full((D_MODEL, D_MODEL)), _full((D_MODEL, D_MODEL))],
        out_specs=[cache, cache, tok, tok],
        out_shape=(f, f, h, h),
        compiler_params=_params("parallel", fuse=_fuse(4, 2, 3)),
        name="memory_kv",
    )(mem, g_mem, w_mk, w_mv)


ATTN_ROWS = 512


def _outattn_kernel(x_ref, ssm_ref, ret_ref, wout_ref, g_ref, wq_ref, k_ref, v_ref, x1_ref, o_ref):
    cols = [slice(h * MEM_HEAD_DIM, (h + 1) * MEM_HEAD_DIM) for h in range(MEM_HEADS)]
    groups = [slice(r, r + ATTN_ROWS) for r in range(0, x_ref.shape[0], ATTN_ROWS)]

    def scores_of(rows):
        x1 = _mixer_out(x_ref, ssm_ref, ret_ref, wout_ref, rows)
        x1_ref[rows, :] = x1
        q = _query(x1, g_ref, wq_ref)
        return [_dot_nt(q[:, c], k_ref[0, :, c]) for c in cols]

    def attend(rows, scores):
        outs = []
        for s, c in zip(scores, cols):
            e = jnp.exp(s - jnp.max(s, axis=-1, keepdims=True))
            p = (e / jnp.sum(e, axis=-1, keepdims=True)).astype(BF16)
            outs.append(_dot(p, v_ref[0, :, c]).astype(BF16))
        o_ref[rows, :] = jnp.concatenate(outs, axis=1)

    pending = None
    for rows in groups:
        scores = scores_of(rows)
        if pending is not None:
            attend(*pending)
        pending = (rows, scores)
    attend(*pending)


def _out_and_attention(x, ssm, ret, w_out, g_xattn, w_mq, mk, mv, *, length):
    t = x.shape[0]
    tm = min(length, ROWS_LARGE)
    per_seq = length // tm
    row = lambda w: pl.BlockSpec((tm, w), lambda i: (i, 0))
    mem = pl.BlockSpec((1, MEM_LEN, D_MODEL), lambda i: (i // per_seq, 0, 0))
    return pl.pallas_call(
        _outattn_kernel,
        grid=(t // tm,),
        in_specs=[row(D_MODEL), row(SSM_WIDTH), row(RET_WIDTH), _full((D_MODEL, D_MODEL)),
                  _full((1, D_MODEL)), _full((D_MODEL, D_MODEL)), mem, mem],
        out_specs=[row(D_MODEL), row(D_MODEL)],
        out_shape=(jax.ShapeDtypeStruct((t, D_MODEL), F32), jax.ShapeDtypeStruct((t, D_MODEL), BF16)),
        compiler_params=_params("parallel", fuse=_fuse(8, 3, 5)),
        name="out_proj_attention",
    )(x, ssm, ret, w_out, g_xattn, w_mq, mk, mv)


def _cached_attention(q_ref, k_ref, v_ref, o_ref, *, bb, length):
    hit = (lax.broadcasted_iota(jnp.int32, (CACHE_SUB, CACHE_ROWS), 0)
           == lax.broadcasted_iota(jnp.int32, (CACHE_SUB, CACHE_ROWS), 1) % CACHE_SUB)
    first_half = lax.broadcasted_iota(jnp.int32, (length, LANES), 1) % CACHE_SUB < MEM_HEADS
    tiles = CACHE_ROWS // LANES

    def class_reduce(x, op):
        shift = CACHE_SUB
        while shift < LANES:
            x = op(x, pltpu.roll(x, shift, axis=1))
            shift *= 2
        return x

    def softmax_rows(r):
        z = jnp.concatenate(
            [jnp.sum(jnp.where(hit, r[CACHE_SUB * t:CACHE_SUB * (t + 1)], 0.0), axis=0, keepdims=True)
             for t in range(length)], axis=0)
        parts = []
        for i in range(tiles):
            zi = z[:, LANES * i:LANES * (i + 1)]
            parts.append(zi + pltpu.roll(zi, LANES - MEM_HEADS, axis=1))
        mx = parts[0]
        for pi in parts[1:]:
            mx = jnp.maximum(mx, pi)
        mx = class_reduce(mx, jnp.maximum)
        es = [jnp.exp(pi - mx) for pi in parts]
        tot = es[0]
        for ei in es[1:]:
            tot = tot + ei
        tot = class_reduce(tot, jnp.add)
        ps = []
        for ei in es:
            pi = ei / tot
            ps.append(jnp.where(first_half, pi, pltpu.roll(pi, MEM_HEADS, axis=1)))
        p = jnp.concatenate(ps, axis=1)
        return jnp.concatenate(
            [jnp.where(hit, jnp.broadcast_to(p[t:t + 1], (CACHE_SUB, CACHE_ROWS)), 0.0)
             for t in range(length)], axis=0).astype(BF16)

    def start():
        return [_dot_nt(q_ref[b], k_ref[b].astype(BF16)) for b in range(bb)]

    def finish(scores):
        probs = [softmax_rows(r) for r in scores]
        for b in range(bb):
            o_ref[b] = _dot(probs[b], v_ref[b].astype(BF16)).astype(BF16)

    return start, finish


def _to_cache_rows(a, lead):
    halves = MEM_HEAD_DIM // LANES
    batch = a.size // (lead * D_MODEL)
    a = a.reshape(batch, lead, MEM_HEADS, halves, LANES).transpose(0, 1, 3, 2, 4)
    return a.reshape(batch, lead * CACHE_SUB, LANES)


def _from_cache_rows(a, lead):
    halves = MEM_HEAD_DIM // LANES
    batch = a.shape[0]
    a = a.reshape(batch, lead, halves, MEM_HEADS, LANES).transpose(0, 1, 3, 2, 4)
    return a.reshape(batch, lead, D_MODEL)


MLP_CHUNK = 1024


def _post_kernel(x1_ref, o_ref, wo_ref, gm_ref, wup_ref, wdn_ref, gf_ref, *rest, side_bb, side_len):
    if side_bb:
        qs_ref, ck_ref, cv_ref, y_ref, os_ref = rest
    else:
        (y_ref,) = rest
    x2 = x1_ref[...] + _dot(o_ref[...], wo_ref[...].astype(BF16))
    if side_bb:
        side_start, side_finish = _cached_attention(qs_ref, ck_ref, cv_ref, os_ref, bb=side_bb, length=side_len)
        side_scores = side_start()
    h = _rmsnorm(x2, gm_ref[...]).astype(BF16)
    chunks = [slice(c, c + MLP_CHUNK) for c in range(0, D_FF, MLP_CHUNK)]
    acc = x2
    up = _dot(h, wup_ref[:, chunks[0]])
    for c, cols in enumerate(chunks):
        nxt = _dot(h, wup_ref[:, chunks[c + 1]]) if c + 1 < len(chunks) else None
        if side_bb and c == len(chunks) - 1:
            side_finish(side_scores)
        a = jnp.maximum(up, 0.0)
        acc = acc + _dot((a * a).astype(BF16), wdn_ref[cols, :])
        up = nxt
    y_ref[...] = _rmsnorm(acc, gf_ref[...])


def _attn_out_mlp(x1, o, w_mo, g_mlp, w_up, w_down, g_final, side=None):
    t = x1.shape[0]
    tm = min(t, ROWS_SMALL)
    steps = t // tm
    row = pl.BlockSpec((tm, D_MODEL), lambda i: (i, 0))
    in_specs = [row, row, _full((D_MODEL, D_MODEL)), _full((1, D_MODEL)),
                _full((D_MODEL, D_FF)), _full((D_FF, D_MODEL)), _full((1, D_MODEL))]
    out_specs = [row]
    out_shape = [jax.ShapeDtypeStruct((t, D_MODEL), F32)]
    args = [x1, o, w_mo, g_mlp, w_up, w_down, g_final]
    side_bb = side_len = 0
    if side is not None:
        q_rows = side[0]
        side_bb, rem = divmod(q_rows.shape[0], steps)
        if rem or not side_bb:
            raise ValueError("side attention sequences must spread evenly over the grid steps")
        side_len = q_rows.shape[1] // CACHE_SUB
        blk = lambda r: pl.BlockSpec((side_bb, r, LANES), lambda i: (i, 0, 0))
        in_specs += [blk(q_rows.shape[1]), blk(CACHE_ROWS), blk(CACHE_ROWS)]
        out_specs.append(blk(q_rows.shape[1]))
        out_shape.append(jax.ShapeDtypeStruct(q_rows.shape, BF16))
        args += list(side)
    outs = pl.pallas_call(
        functools.partial(_post_kernel, side_bb=side_bb, side_len=side_len),
        grid=(steps,),
        in_specs=in_specs,
        out_specs=out_specs,
        out_shape=out_shape,
        compiler_params=_params("parallel", fuse=_fuse(len(in_specs), 1, 2, 7)),
        name="attn_out_mlp",
    )(*args)
    return outs if side is not None else outs[0]


def _mixers(proj, batch, length, s5_re, s5_im, ret_s, p):
    tokens = batch * length
    u, packed, g = proj
    ssm, s5_re_new, s5_im_new = _s5_mixer(
        u.reshape(batch, length, SSM_WIDTH), s5_re.reshape(batch, SSM_LANES), s5_im.reshape(batch, SSM_LANES),
        p["s5_maps"], p["d_skip"], p["w_glu"], batch=batch, length=length)
    tok3 = lambda a: a.reshape(batch, length, RET_WIDTH)
    ret, ret_new = _retention(packed.reshape(batch, length, 4 * RET_WIDTH), tok3(g), ret_s, p["ret_gn"])
    states = (s5_re_new.reshape(1, batch, SSM_GROUPS, SSM_STATE),
              s5_im_new.reshape(1, batch, SSM_GROUPS, SSM_STATE), ret_new[None])
    return ssm.reshape(tokens, SSM_WIDTH), ret.reshape(tokens, RET_WIDTH), states


def kernel(x_prompt, x_sample, mem_prompt, state_s5_re, state_s5_im, state_ret, cache_mem_k, cache_mem_v, g_mix, w_in, lam_re, lam_im, log_dt, b_re, b_im, c_re, c_im, d_skip, w_glu, ret_gn, w_out, g_xattn, g_mem, w_mq, w_mk, w_mv, w_mo, g_mlp, w_up, w_down, g_final):
    if g_mix.shape[0] != 1:
        raise ValueError("this kernel implements the single-layer configuration")
    bp, lp, _ = x_prompt.shape
    bs, ls, _ = x_sample.shape
    vec = lambda a: a.reshape(1, -1).astype(F32)
    xp = x_prompt.reshape(bp * lp, D_MODEL)
    xs = x_sample.reshape(bs * ls, D_MODEL)
    proj_p, (w_in_b, w_up_b, w_down_b) = _project(
        xp, vec(g_mix[0]), w_in[0], length=lp, pos0=0.0, casts=(w_up[0], w_down[0]))
    w_out_b, w_mq_b, w_mo_b, w_mk_b, w_mv_b, w_glu_b = (
        w[0].astype(BF16) for w in (w_out, w_mq, w_mo, w_mk, w_mv, w_glu))
    proj_s, _ = _project(xs, vec(g_mix[0]), w_in_b, length=ls, pos0=float(PAST_LEN))
    p = dict(
        s5_maps=_s5_block_maps(lam_re[0], lam_im[0], log_dt[0], b_re[0], b_im[0], c_re[0], c_im[0]),
        d_skip=vec(d_skip[0]), w_glu=w_glu_b, ret_gn=vec(ret_gn[0]))
    g_x = vec(g_xattn[0])
    mlp = (w_mo_b, vec(g_mlp[0]), w_up_b, w_down_b, vec(g_final))

    ssm_s, ret_s, states_s = _mixers(proj_s, bs, ls, state_s5_re[0], state_s5_im[0], state_ret[0],
                                     dict(p, w_glu=w_glu[0]))
    x1_s, q_s = _out_and_query(xs, ssm_s, ret_s, w_out[0], g_x, w_mq[0])
    if (ls * CACHE_SUB) % BF16_ROWS:
        raise ValueError("cached attention needs whole bf16 tiles of query rows")
    side = (_to_cache_rows(q_s, ls), _to_cache_rows(cache_mem_k, MEM_LEN), _to_cache_rows(cache_mem_v, MEM_LEN))

    mk_rows, mv_rows, mk_p, mv_p = _memory_kv(mem_prompt, vec(g_mem[0]), w_mk_b, w_mv_b)
    zs = jnp.zeros((bp, SSM_GROUPS, SSM_STATE), F32)
    zr = jnp.zeros((bp, RET_HEADS, RET_HEAD_DIM, RET_HEAD_DIM), F32)
    ssm_p, ret_p, states_p = _mixers(proj_p, bp, lp, zs, zs, zr, p)
    x1_p, o_p = _out_and_attention(xp, ssm_p, ret_p, w_out_b, g_x, w_mq_b, mk_p, mv_p, length=lp)
    y_p, o_s = _attn_out_mlp(x1_p, o_p, *mlp, side=side)
    y_s = _attn_out_mlp(x1_s, _from_cache_rows(o_s, ls).reshape(bs * ls, D_MODEL), w_mo[0], *mlp[1:])

    kv5 = lambda a: _from_cache_rows(a, MEM_LEN).reshape(1, bp, MEM_LEN, MEM_HEADS, MEM_HEAD_DIM)
    return (y_p.reshape(bp, lp, D_MODEL), y_s.reshape(bs, ls, D_MODEL), *states_p, kv5(mk_rows), kv5(mv_rows),
            *states_s)
```

```python
import functools
import math

import numpy as np
import jax
import jax.numpy as jnp
from jax import lax
from jax.experimental import pallas as pl
from jax.experimental.pallas import tpu as pltpu

F32 = jnp.float32
BF16 = jnp.bfloat16

D_MODEL = 1024
SSM_WIDTH = 512
SSM_GROUP = 16
SSM_GROUPS = 32
SSM_STATE = 64
SSM_LANES = SSM_GROUPS * SSM_STATE
RET_WIDTH = 512
RET_HEADS = 4
RET_HEAD_DIM = 128
RET_CHUNK = 128
ROPE_BASE = 10000.0
MEM_LEN = 256
MEM_HEADS = 4
MEM_HEAD_DIM = 256
MEM_SCALE = MEM_HEAD_DIM ** -0.5
D_FF = 4096
PROJ_WIDTH = SSM_WIDTH + 4 * RET_WIDTH
EPS = 1e-6
PAST_LEN = 16384

LANES = 128
MXU_TILE = 256
BF16_ROWS = 16
VMEM_LIMIT = 56 * 1024 * 1024

ROWS_LARGE = 1024
ROWS_SMALL = 512
S5_STEPS = 256

CACHE_ROWS = MEM_LEN * MEM_HEADS * MEM_HEAD_DIM // LANES
CACHE_SUB = MEM_HEADS * MEM_HEAD_DIM // LANES

assert math.frexp(MEM_SCALE)[0] == 0.5


def _dot(a, b):
    return jnp.dot(a, b, preferred_element_type=F32)


def _dot_nt(a, b):
    return lax.dot_general(a, b, (((1,), (1,)), ((), ())), preferred_element_type=F32)


def _dot_tn(a, b):
    return lax.dot_general(a, b, (((0,), (0,)), ((), ())), preferred_element_type=F32)


def _rmsnorm(x, g):
    return x * lax.rsqrt(jnp.mean(x * x, axis=-1, keepdims=True) + EPS) * g


def _sigmoid(x):
    return 1.0 / (1.0 + jnp.exp(-x))


def _gelu_tanh(x):
    c = math.sqrt(2.0 / math.pi)
    return x * (0.5 * jnp.tanh(x * (c + (c * 0.044715) * (x * x))) + 0.5)


def _params(*sem, fuse=None):
    return pltpu.CompilerParams(dimension_semantics=sem, vmem_limit_bytes=VMEM_LIMIT, allow_input_fusion=fuse)


def _fuse(n_inputs, *which):
    return [i in which for i in range(n_inputs)]


def _full(shape):
    return pl.BlockSpec(shape, lambda *_: (0,) * len(shape))


SSM_BLOCK = 4
SSM_TILE_GROUPS = MXU_TILE // SSM_STATE
SSM_TILES = SSM_GROUPS // SSM_TILE_GROUPS
SSM_TILE_CHANNELS = SSM_TILE_GROUPS * SSM_GROUP
GROUP_SHIFT = SSM_GROUP.bit_length() - 1
STATE_SHIFT = SSM_STATE.bit_length() - 1

assert SSM_BLOCK * SSM_TILE_CHANNELS == MXU_TILE and SSM_TILE_CHANNELS == SSM_STATE


def _s5_prep_kernel(lr_ref, li_ref, ldt_ref, br_ref, bi_ref, cr_ref, ci_ref,
                    ar_ref, ai_ref, winr_ref, wini_ref, woutr_ref, wouti_ref, kdir_ref):
    lr = lr_ref[...]
    li = li_ref[...]
    dt = jnp.exp(ldt_ref[...])
    mag = jnp.exp(lr * dt)
    ab_re = mag * jnp.cos(li * dt)
    ab_im = mag * jnp.sin(li * dt)
    den = lr * lr + li * li
    f_re = ((ab_re - 1.0) * lr + ab_im * li) / den
    f_im = (ab_im * lr - (ab_re - 1.0) * li) / den
    bb_re = f_re * br_ref[...] - f_im * bi_ref[...]
    bb_im = f_re * bi_ref[...] + f_im * br_ref[...]
    cr = cr_ref[...]
    ci = ci_ref[...]
    pows = [(jnp.ones_like(ab_re), jnp.zeros_like(ab_re))]
    for _ in range(SSM_BLOCK):
        pr, pi = pows[-1]
        pows.append((pr * ab_re - pi * ab_im, pr * ab_im + pi * ab_re))
    ar_ref[...], ai_ref[...] = pows[SSM_BLOCK]

    R, T, TG = SSM_BLOCK, SSM_TILES, SSM_TILE_GROUPS
    rows = SSM_GROUPS * SSM_GROUP
    side = SSM_TILE_CHANNELS
    exact = dict(precision=lax.Precision.HIGHEST, preferred_element_type=F32)
    iota = lambda shape, d: lax.broadcasted_iota(jnp.int32, shape, d)
    rep4 = lambda a, axis: jnp.concatenate([a] * TG, axis=axis)
    tiles = lambda a: a.reshape(T, side, a.shape[-1])
    group_of = lambda i: (i & (side - 1)) >> GROUP_SHIFT

    own_in = group_of(iota((rows, MXU_TILE), 0)) == iota((rows, MXU_TILE), 1) >> STATE_SHIFT
    for s in range(R):
        pr, pi = pows[R - 1 - s]
        for ref, w in ((winr_ref, pr * bb_re - pi * bb_im), (wini_ref, pr * bb_im + pi * bb_re)):
            ref[:, side * s:side * (s + 1), :] = tiles(jnp.where(own_in, rep4(w, 1), 0.0)).astype(BF16)

    own_out = iota((MXU_TILE, rows), 0) >> STATE_SHIFT == group_of(iota((MXU_TILE, rows), 1))
    for ref, real in ((woutr_ref, True), (wouti_ref, False)):
        per_lag = []
        for j in range(R):
            pr, pi = pows[j + 1]
            w = cr * pr - ci * pi if real else -(cr * pi + ci * pr)
            wt = w.T
            per_lag.append(jnp.where(own_out, rep4(wt, 0), 0.0))
        for n in range(T):
            ref[n] = jnp.concatenate([m[:, side * n:side * (n + 1)] for m in per_lag], axis=1).astype(BF16)

    grouped = lambda a: a.reshape(SSM_GROUPS, SSM_GROUP, SSM_STATE)
    contract = lambda a, b: lax.dot_general(a, b, (((2,), (2,)), ((0,), (0,))), **exact)
    own_dir = group_of(iota((rows, side), 0)) == iota((rows, side), 1) >> GROUP_SHIFT
    by_lag = []
    for d in range(R):
        pr, pi = pows[d]
        er = grouped(pr * bb_re - pi * bb_im)
        ei = grouped(pr * bb_im + pi * bb_re)
        kt = (contract(er, grouped(cr)) - contract(ei, grouped(ci))).reshape(rows, SSM_GROUP)
        by_lag.append(tiles(jnp.where(own_dir, rep4(kt, 1), 0.0)))
    nothing = jnp.zeros_like(by_lag[0])
    for s in range(R):
        kdir_ref[:, side * s:side * (s + 1), :] = jnp.concatenate(
            [by_lag[j - s] if j >= s else nothing for j in range(R)], axis=2).astype(BF16)


def _s5_block_maps(lam_re, lam_im, log_dt, b_re, b_im, c_re, c_im):
    rows = SSM_GROUPS * SSM_GROUP
    rep = lambda a: jnp.repeat(a, SSM_GROUP, axis=0)
    ldt = jnp.broadcast_to(log_dt[:, None], (SSM_GROUPS, SSM_STATE))
    bt = lambda b: b.transpose(0, 2, 1).reshape(rows, SSM_STATE)
    ct = lambda c: c.reshape(rows, SSM_STATE)
    one = jax.ShapeDtypeStruct((rows, SSM_STATE), F32)
    tile = jax.ShapeDtypeStruct((SSM_TILES, MXU_TILE, MXU_TILE), BF16)
    ar, ai, winr, wini, woutr, wouti, direct = pl.pallas_call(
        _s5_prep_kernel,
        out_shape=(one, one, tile, tile, tile, tile, tile),
        compiler_params=pltpu.CompilerParams(vmem_limit_bytes=VMEM_LIMIT),
        name="s5_block_maps",
    )(rep(lam_re), rep(lam_im), rep(ldt), bt(b_re), bt(b_im), ct(c_re), ct(c_im))
    a_re = ar[::SSM_GROUP].reshape(1, SSM_LANES)
    a_im = ai[::SSM_GROUP].reshape(1, SSM_LANES)
    return a_re, a_im, winr, wini, woutr, wouti, direct


def _retention_chunk(length):
    return RET_CHUNK if length % RET_CHUNK == 0 else length


def _log_gamma():
    return np.log(1.0 - 2.0 ** (-5.0 - np.arange(RET_HEADS, dtype=np.float64)))


def _rope_tables(length, pos0):
    half = RET_HEAD_DIM // 2
    inv = ROPE_BASE ** (-np.arange(half, dtype=np.float64) / half)
    ang = (pos0 + np.arange(length, dtype=np.float64))[:, None] * inv[None, :]
    return (np.concatenate([np.cos(ang), np.cos(ang)], axis=1),
            np.concatenate([-np.sin(ang), np.sin(ang)], axis=1))


def _zeta_rows(length):
    chunk = _retention_chunk(length)
    idx = np.arange(chunk, dtype=np.float64)
    zeta = np.exp((chunk - 1.0 - idx)[None, :] * _log_gamma()[:, None])
    return np.concatenate([np.broadcast_to(z[:, None], (chunk, RET_HEAD_DIM)) for z in zeta], axis=1)


def _decay_tables(length, padded):
    chunk = _retention_chunk(length)
    cpad = chunk if padded == length else padded
    lg = _log_gamma()
    idx = np.arange(chunk, dtype=np.float64)
    diff = idx[:, None] - idx[None, :]
    mask = np.where(diff[None] >= 0, np.exp(np.maximum(diff, 0.0)[None] * lg[:, None, None]), 0.0)
    xi = np.exp((idx + 1.0)[None, :] * lg[:, None])
    gamma_c = tuple(float(v) for v in np.exp(chunk * lg))
    mask = np.pad(mask, ((0, 0), (0, cpad - chunk), (0, cpad - chunk)))
    xi = np.broadcast_to(np.pad(xi, ((0, 0), (0, cpad - chunk)))[:, :, None], (RET_HEADS, cpad, RET_HEAD_DIM))
    f = lambda a: jnp.asarray(np.ascontiguousarray(a), dtype=F32)
    return cpad, gamma_c, f(mask), f(xi)


def _ret_lanes(part, head):
    lo = part * RET_WIDTH + head * RET_HEAD_DIM
    return slice(lo, lo + RET_HEAD_DIM)


def _proj_kernel(x_ref, g_ref, w_ref, cc_ref, ss_ref, zeta_ref, *rest, n_cast):
    casts_in, rest = rest[:n_cast], rest[n_cast:]
    u_ref, ret_ref, gate_ref = rest[:3]
    w = w_ref[...]
    if n_cast:
        w = w.astype(BF16)
        for src, dst in zip(casts_in, rest[3:]):
            dst[...] = src[...].astype(BF16)
    h = _rmsnorm(x_ref[...], g_ref[...]).astype(BF16)
    proj = _dot(h, w)
    u_ref[...] = proj[:, :SSM_WIDTH]
    ret_ref[:, 3 * RET_WIDTH:] = proj[:, SSM_WIDTH + 2 * RET_WIDTH:SSM_WIDTH + 3 * RET_WIDTH].astype(BF16)
    gate_ref[...] = proj[:, SSM_WIDTH + 3 * RET_WIDTH:]
    cc = cc_ref[...]
    ss = ss_ref[...]
    rope = lambda a: a * cc + pltpu.roll(a, RET_HEAD_DIM // 2, axis=1) * ss
    for hd in range(RET_HEADS):
        cols = slice(hd * RET_HEAD_DIM, (hd + 1) * RET_HEAD_DIM)
        q = rope(proj[:, SSM_WIDTH + hd * RET_HEAD_DIM:SSM_WIDTH + (hd + 1) * RET_HEAD_DIM])
        k = rope(proj[:, SSM_WIDTH + RET_WIDTH + hd * RET_HEAD_DIM:SSM_WIDTH + RET_WIDTH + (hd + 1) * RET_HEAD_DIM])
        ret_ref[:, _ret_lanes(0, hd)] = (q * (RET_HEAD_DIM ** -0.5)).astype(BF16)
        ret_ref[:, _ret_lanes(1, hd)] = k.astype(BF16)
        ret_ref[:, _ret_lanes(2, hd)] = (k * zeta_ref[:, cols]).astype(BF16)


def _project(x, g_mix, w_in, *, length, pos0, casts=()):
    t = x.shape[0]
    tm = min(t, ROWS_LARGE)
    steps = t // tm
    chunk = _retention_chunk(length)
    casts = (w_in, *casts) if casts else ()
    if tm % chunk or (length % tm and tm % length):
        raise ValueError("token tiles must hold whole retention chunks of whole or repeated sequences")
    cc, ss = _rope_tables(length, pos0)
    if length < tm:
        cc, ss = np.tile(cc, (tm // length, 1)), np.tile(ss, (tm // length, 1))
    tab_blocks = cc.shape[0] // tm
    zeta = np.tile(_zeta_rows(length), (tm // chunk, 1))
    const = lambda a: jnp.asarray(np.ascontiguousarray(a), dtype=F32)
    row = lambda w: pl.BlockSpec((tm, w), lambda i: (i, 0))
    tab = pl.BlockSpec((tm, RET_HEAD_DIM), lambda i: (i % tab_blocks, 0))
    f = jax.ShapeDtypeStruct((t, SSM_WIDTH), F32)
    h = jax.ShapeDtypeStruct((t, 4 * RET_WIDTH), BF16)
    in_specs = [row(D_MODEL), _full((1, D_MODEL)), _full((D_MODEL, PROJ_WIDTH)), tab, tab, _full((tm, RET_WIDTH))]
    out_specs = [row(SSM_WIDTH), row(4 * RET_WIDTH), row(RET_WIDTH)]
    out_shape = [f, h, f]
    if casts:
        for c in casts:
            rows, rem = divmod(c.shape[0], steps)
            if rem or rows % BF16_ROWS:
                raise ValueError("side-cast weights must split into whole bf16 tiles per grid step")
            blk = pl.BlockSpec((rows, c.shape[1]), lambda i: (i, 0))
            in_specs.append(blk)
            out_specs.append(blk)
            out_shape.append(jax.ShapeDtypeStruct(c.shape, BF16))
    outs = pl.pallas_call(
        functools.partial(_proj_kernel, n_cast=len(casts)),
        grid=(steps,),
        in_specs=in_specs,
        out_specs=out_specs,
        out_shape=out_shape,
        compiler_params=_params("arbitrary" if casts else "parallel", fuse=_fuse(len(in_specs), 0)),
        name="in_proj",
    )(x, g_mix, w_in, const(cc), const(ss), const(zeta), *casts)
    return outs[:3], outs[3:]


def _s5_block_kernel(u_ref, h0r_ref, h0i_ref, ar_ref, ai_ref, winr_ref, wini_ref, woutr_ref, wouti_ref,
                     kdir_ref, d_ref, wglu_ref, out_ref, hr_ref, hi_ref, vr_scr, vi_scr, xr_scr, xi_scr,
                     *, batch, steps, pair, width):
    R, T = SSM_BLOCK, SSM_TILES
    chan = SSM_TILE_CHANNELS
    blocks = steps // R
    rows, brow = steps * batch, blocks * batch

    @pl.when(pl.program_id(0) == 0)
    def _():
        hr_ref[...] = h0r_ref[...]
        hi_ref[...] = h0i_ref[...]

    u = jnp.swapaxes(u_ref[...], 0, 1).reshape(rows, SSM_WIDTH)
    u4 = u.reshape(blocks, R, batch, SSM_WIDTH)
    lag = [u4[:, s].reshape(brow, SSM_WIDTH) for s in range(R)]
    direct = []
    for n in range(T):
        un = jnp.concatenate([lag[s][:, chan * n:chan * (n + 1)] for s in range(R)], axis=1).astype(BF16)
        lanes = slice(MXU_TILE * n, MXU_TILE * (n + 1))
        vr_scr[:, lanes] = _dot(un, winr_ref[n])
        vi_scr[:, lanes] = _dot(un, wini_ref[n])
        direct.append(_dot(un, kdir_ref[n]))

    for j in range(SSM_LANES // width):
        glob = slice(j * width, (j + 1) * width)
        ar = jnp.broadcast_to(ar_ref[:, glob], (batch, width))
        ai = jnp.broadcast_to(ai_ref[:, glob], (batch, width))
        xr = hr_ref[:, glob]
        xi = hi_ref[:, glob]
        for i in range(blocks // pair):
            prev_r, prev_i = [], []
            for s in range(pair):
                at = slice((i * pair + s) * batch, (i * pair + s + 1) * batch)
                prev_r.append(xr)
                prev_i.append(xi)
                xr, xi = ar * xr - ai * xi + vr_scr[at, glob], ar * xi + ai * xr + vi_scr[at, glob]
            blk = slice(i * pair * batch, (i + 1) * pair * batch)
            xr_scr[blk, glob] = jnp.concatenate(prev_r, axis=0).astype(BF16)
            xi_scr[blk, glob] = jnp.concatenate(prev_i, axis=0).astype(BF16)
        hr_ref[:, glob] = xr
        hi_ref[:, glob] = xi

    y4 = []
    for n in range(T):
        lanes = slice(MXU_TILE * n, MXU_TILE * (n + 1))
        y4.append(_dot(xr_scr[:, lanes], woutr_ref[n]) + _dot(xi_scr[:, lanes], wouti_ref[n]) + direct[n])
    per_lag = [jnp.concatenate([y4[n][:, chan * j:chan * (j + 1)] for n in range(T)], axis=1)
               .reshape(blocks, batch, SSM_WIDTH) for j in range(R)]
    y = jnp.stack(per_lag, axis=1).reshape(rows, SSM_WIDTH) + d_ref[...] * u
    z = _gelu_tanh(y)
    out = z * _sigmoid(_dot(z.astype(BF16), wglu_ref[...].astype(BF16)))
    out_ref[...] = jnp.swapaxes(out.reshape(steps, batch, SSM_WIDTH), 0, 1).astype(BF16)


def _s5_mixer(u, h0_re, h0_im, maps, d_skip, w_glu, *, batch, length):
    a_re, a_im, winr, wini, woutr, wouti, direct = maps
    steps = min(length, S5_STEPS)
    if steps % SSM_BLOCK or length % steps:
        raise ValueError("sequence length must be a multiple of the S5 block")
    blocks = steps // SSM_BLOCK
    pair = max(1, min(blocks, BF16_ROWS // batch))
    if (pair * batch) % BF16_ROWS or blocks % pair:
        raise ValueError("S5 state stores need whole bf16 tiles")
    width = SSM_LANES // 2 if batch <= 8 else LANES
    brow = blocks * batch
    kern = functools.partial(_s5_block_kernel, batch=batch, steps=steps, pair=pair, width=width)
    st = jax.ShapeDtypeStruct((batch, SSM_LANES), F32)
    tile = _full((SSM_TILES, MXU_TILE, MXU_TILE))
    return pl.pallas_call(
        kern,
        grid=(length // steps,),
        in_specs=[pl.BlockSpec((batch, steps, SSM_WIDTH), lambda i: (0, i, 0)),
                  _full((batch, SSM_LANES)), _full((batch, SSM_LANES)),
                  _full((1, SSM_LANES)), _full((1, SSM_LANES)),
                  tile, tile, tile, tile, tile,
                  _full((1, SSM_WIDTH)), _full((SSM_WIDTH, SSM_WIDTH))],
        out_specs=[pl.BlockSpec((batch, steps, SSM_WIDTH), lambda i: (0, i, 0)),
                   _full((batch, SSM_LANES)), _full((batch, SSM_LANES))],
        out_shape=(jax.ShapeDtypeStruct((batch, length, SSM_WIDTH), BF16), st, st),
        scratch_shapes=[pltpu.VMEM((brow, SSM_LANES), F32), pltpu.VMEM((brow, SSM_LANES), F32),
                        pltpu.VMEM((brow, SSM_LANES), BF16), pltpu.VMEM((brow, SSM_LANES), BF16)],
        compiler_params=_params("arbitrary", fuse=_fuse(12, 11)),
        name="s5_mixer",
    )(u, h0_re, h0_im, a_re, a_im, winr, wini, woutr, wouti, direct, d_skip, w_glu)


def _ret_kernel(p_ref, g_ref, mask_ref, xi_ref, gain_ref, s0_ref, o_ref, s_ref,
                *, bb, gamma_c, unroll, chunk, chunks):
    @pl.when(pl.program_id(1) == 0)
    def _():
        s_ref[...] = s0_ref[...]

    heads = range(RET_HEADS)
    cols = [slice(h * RET_HEAD_DIM, (h + 1) * RET_HEAD_DIM) for h in heads]

    def one_chunk(b, rows):
        qb = [p_ref[b, rows, _ret_lanes(0, h)] for h in heads]
        vb = [p_ref[b, rows, _ret_lanes(3, h)] for h in heads]
        s_prev = [s_ref[b, h] for h in heads]
        scores = [(_dot_nt(qb[h], p_ref[b, rows, _ret_lanes(1, h)]) * mask_ref[h]).astype(BF16) for h in heads]
        cross = [_dot(qb[h], s_prev[h].astype(BF16)) * xi_ref[h] for h in heads]
        kv = [_dot_tn(p_ref[b, rows, _ret_lanes(2, h)], vb[h]) for h in heads]
        inner = [_dot(scores[h], vb[h]) for h in heads]
        outs = []
        for h in heads:
            s_ref[b, h] = s_prev[h] * gamma_c[h] + kv[h]
            o = inner[h] + cross[h]
            mu = jnp.mean(o, axis=-1, keepdims=True)
            oc = o - mu
            var = jnp.mean(oc * oc, axis=-1, keepdims=True)
            on = oc * lax.rsqrt(var + EPS) * gain_ref[:, cols[h]]
            gate = g_ref[b, rows, cols[h]]
            outs.append((gate * _sigmoid(gate) * on).astype(BF16))
        o_ref[b, rows, :] = jnp.concatenate(outs, axis=1)

    def per_batch(b, carry):
        for c in range(chunks):
            one_chunk(b, slice(c * chunk, (c + 1) * chunk))
        return carry

    if unroll == bb:
        for b in range(bb):
            per_batch(b, 0)
    else:
        lax.fori_loop(0, bb, per_batch, 0, unroll=unroll)


def _retention(packed, g, s0, gn_gain):
    batch, length, _ = g.shape
    padded = pl.cdiv(length, BF16_ROWS) * BF16_ROWS
    chunk, gamma_c, mask, xi = _decay_tables(length, padded)
    if padded != length:
        pad = lambda a: jnp.pad(a, ((0, 0), (0, padded - length), (0, 0)))
        packed, g = pad(packed), pad(g)
    bb, unroll = (8, 2) if chunk > BF16_ROWS else (16, 16)
    chunks = 2 if (padded // chunk) % 2 == 0 else 1
    tok = pl.BlockSpec((bb, chunks * chunk, RET_WIDTH), lambda i, c: (i, c, 0))
    state = pl.BlockSpec((bb, RET_HEADS, RET_HEAD_DIM, RET_HEAD_DIM), lambda i, c: (i, 0, 0, 0))
    out, s_new = pl.pallas_call(
        functools.partial(_ret_kernel, bb=bb, gamma_c=gamma_c, unroll=unroll, chunk=chunk, chunks=chunks),
        grid=(batch // bb, padded // (chunks * chunk)),
        in_specs=[pl.BlockSpec((bb, chunks * chunk, 4 * RET_WIDTH), lambda i, c: (i, c, 0)), tok,
                  _full((RET_HEADS, chunk, chunk)), _full((RET_HEADS, chunk, RET_HEAD_DIM)),
                  _full((1, RET_WIDTH)), state],
        out_specs=[tok, state],
        out_shape=(jax.ShapeDtypeStruct((batch, padded, RET_WIDTH), BF16),
                   jax.ShapeDtypeStruct(s0.shape, F32)),
        compiler_params=_params("parallel", "arbitrary", fuse=_fuse(6, 0, 1)),
        name="retention",
    )(packed, g, mask, xi, gn_gain, s0)
    return out[:, :length], s_new


def _mixer_out(x_ref, ssm_ref, ret_ref, wout_ref, rows=slice(None)):
    return (x_ref[rows, :] + _dot(ssm_ref[rows, :], wout_ref[:SSM_WIDTH, :].astype(BF16))
            + _dot(ret_ref[rows, :], wout_ref[SSM_WIDTH:, :].astype(BF16)))


def _query(x1, g_ref, wq_ref):
    return (_dot(_rmsnorm(x1, g_ref[...]).astype(BF16), wq_ref[...].astype(BF16)) * MEM_SCALE).astype(BF16)


def _outq_kernel(x_ref, ssm_ref, ret_ref, wout_ref, g_ref, wq_ref, x1_ref, q_ref):
    x1 = _mixer_out(x_ref, ssm_ref, ret_ref, wout_ref)
    x1_ref[...] = x1
    q_ref[...] = _query(x1, g_ref, wq_ref)


def _out_and_query(x, ssm, ret, w_out, g_xattn, w_mq):
    t = x.shape[0]
    tm = min(t, ROWS_SMALL)
    row = lambda w: pl.BlockSpec((tm, w), lambda i: (i, 0))
    return pl.pallas_call(
        _outq_kernel,
        grid=(t // tm,),
        in_specs=[row(D_MODEL), row(SSM_WIDTH), row(RET_WIDTH), _full((D_MODEL, D_MODEL)),
                  _full((1, D_MODEL)), _full((D_MODEL, D_MODEL))],
        out_specs=[row(D_MODEL), row(D_MODEL)],
        out_shape=(jax.ShapeDtypeStruct((t, D_MODEL), F32), jax.ShapeDtypeStruct((t, D_MODEL), BF16)),
        compiler_params=_params("parallel", fuse=_fuse(6, 0, 3, 5)),
        name="out_proj_query",
    )(x, ssm, ret, w_out, g_xattn, w_mq)


def _cache_row_order(x):
    halves = MEM_HEAD_DIM // LANES
    tiles = [x[:, (h * halves + d) * LANES:(h * halves + d + 1) * LANES]
             for d in range(halves) for h in range(MEM_HEADS)]
    return jnp.swapaxes(jnp.stack(tiles, axis=0), 0, 1).reshape(x.shape[0] * CACHE_SUB, LANES)


MEMKV_SEQS = 2


def _memkv_kernel(m_ref, g_ref, wk_ref, wv_ref, k_ref, v_ref, kb_ref, vb_ref):
    seqs = m_ref.shape[0]
    m = _rmsnorm(m_ref[...].reshape(seqs * MEM_LEN, D_MODEL), g_ref[...]).astype(BF16)
    k = _dot(m, wk_ref[...])
    v = _dot(m, wv_ref[...])
    k_ref[...] = _cache_row_order(k).reshape(k_ref.shape)
    v_ref[...] = _cache_row_order(v).reshape(v_ref.shape)
    kb_ref[...] = k.astype(BF16).reshape(kb_ref.shape)
    vb_ref[...] = v.astype(BF16).reshape(vb_ref.shape)


def _memory_kv(mem, g_mem, w_mk, w_mv):
    batch = mem.shape[0]
    seqs = MEMKV_SEQS if batch % MEMKV_SEQS == 0 else 1
    tok = pl.BlockSpec((seqs, MEM_LEN, D_MODEL), lambda i: (i, 0, 0))
    cache = pl.BlockSpec((seqs, CACHE_ROWS, LANES), lambda i: (i, 0, 0))
    f = jax.ShapeDtypeStruct((batch, CACHE_ROWS, LANES), F32)
    h = jax.ShapeDtypeStruct((batch, MEM_LEN, D_MODEL), BF16)
    return pl.pallas_call(
        _memkv_kernel,
        grid=(batch // seqs,),
        in_specs=[tok, _full((1, D_MODEL)), _full((D_MODEL, D_MODEL)), _full((D_MODEL, D_MODEL))],
        out_specs=[cache, cache, tok, tok],
        out_shape=(f, f, h, h),
        compiler_params=_params("parallel", fuse=_fuse(4, 2, 3)),
        name="memory_kv",
    )(mem, g_mem, w_mk, w_mv)


ATTN_ROWS = 512


def _outattn_kernel(x_ref, ssm_ref, ret_ref, wout_ref, g_ref, wq_ref, k_ref, v_ref, x1_ref, o_ref):
    cols = [slice(h * MEM_HEAD_DIM, (h + 1) * MEM_HEAD_DIM) for h in range(MEM_HEADS)]
    groups = [slice(r, r + ATTN_ROWS) for r in range(0, x_ref.shape[0], ATTN_ROWS)]

    def scores_of(rows):
        x1 = _mixer_out(x_ref, ssm_ref, ret_ref, wout_ref, rows)
        x1_ref[rows, :] = x1
        q = _query(x1, g_ref, wq_ref)
        return [_dot_nt(q[:, c], k_ref[0, :, c]) for c in cols]

    def attend(rows, scores):
        outs = []
        for s, c in zip(scores, cols):
            e = jnp.exp(s - jnp.max(s, axis=-1, keepdims=True))
            p = (e / jnp.sum(e, axis=-1, keepdims=True)).astype(BF16)
            outs.append(_dot(p, v_ref[0, :, c]).astype(BF16))
        o_ref[rows, :] = jnp.concatenate(outs, axis=1)

    pending = None
    for rows in groups:
        scores = scores_of(rows)
        if pending is not None:
            attend(*pending)
        pending = (rows, scores)
    attend(*pending)


def _out_and_attention(x, ssm, ret, w_out, g_xattn, w_mq, mk, mv, *, length):
    t = x.shape[0]
    tm = min(length, ROWS_LARGE)
    per_seq = length // tm
    row = lambda w: pl.BlockSpec((tm, w), lambda i: (i, 0))
    mem = pl.BlockSpec((1, MEM_LEN, D_MODEL), lambda i: (i // per_seq, 0, 0))
    return pl.pallas_call(
        _outattn_kernel,
        grid=(t // tm,),
        in_specs=[row(D_MODEL), row(SSM_WIDTH), row(RET_WIDTH), _full((D_MODEL, D_MODEL)),
                  _full((1, D_MODEL)), _full((D_MODEL, D_MODEL)), mem, mem],
        out_specs=[row(D_MODEL), row(D_MODEL)],
        out_shape=(jax.ShapeDtypeStruct((t, D_MODEL), F32), jax.ShapeDtypeStruct((t, D_MODEL), BF16)),
        compiler_params=_params("parallel", fuse=_fuse(8, 3, 5)),
        name="out_proj_attention",
    )(x, ssm, ret, w_out, g_xattn, w_mq, mk, mv)


def _cached_attention(q_ref, k_ref, v_ref, o_ref, *, bb, length):
    hit = (lax.broadcasted_iota(jnp.int32, (CACHE_SUB, CACHE_ROWS), 0)
           == lax.broadcasted_iota(jnp.int32, (CACHE_SUB, CACHE_ROWS), 1) % CACHE_SUB)
    first_half = lax.broadcasted_iota(jnp.int32, (length, LANES), 1) % CACHE_SUB < MEM_HEADS
    tiles = CACHE_ROWS // LANES

    def class_reduce(x, op):
        shift = CACHE_SUB
        while shift < LANES:
            x = op(x, pltpu.roll(x, shift, axis=1))
            shift *= 2
        return x

    def softmax_rows(r):
        z = jnp.concatenate(
            [jnp.sum(jnp.where(hit, r[CACHE_SUB * t:CACHE_SUB * (t + 1)], 0.0), axis=0, keepdims=True)
             for t in range(length)], axis=0)
        parts = []
        for i in range(tiles):
            zi = z[:, LANES * i:LANES * (i + 1)]
            parts.append(zi + pltpu.roll(zi, LANES - MEM_HEADS, axis=1))
        mx = parts[0]
        for pi in parts[1:]:
            mx = jnp.maximum(mx, pi)
        mx = class_reduce(mx, jnp.maximum)
        es = [jnp.exp(pi - mx) for pi in parts]
        tot = es[0]
        for ei in es[1:]:
            tot = tot + ei
        tot = class_reduce(tot, jnp.add)
        ps = []
        for ei in es:
            pi = ei / tot
            ps.append(jnp.where(first_half, pi, pltpu.roll(pi, MEM_HEADS, axis=1)))
        p = jnp.concatenate(ps, axis=1)
        return jnp.concatenate(
            [jnp.where(hit, jnp.broadcast_to(p[t:t + 1], (CACHE_SUB, CACHE_ROWS)), 0.0)
             for t in range(length)], axis=0).astype(BF16)

    def start():
        return [_dot_nt(q_ref[b], k_ref[b].astype(BF16)) for b in range(bb)]

    def finish(scores):
        probs = [softmax_rows(r) for r in scores]
        for b in range(bb):
            o_ref[b] = _dot(probs[b], v_ref[b].astype(BF16)).astype(BF16)

    return start, finish


def _to_cache_rows(a, lead):
    halves = MEM_HEAD_DIM // LANES
    batch = a.size // (lead * D_MODEL)
    a = a.reshape(batch, lead, MEM_HEADS, halves, LANES).transpose(0, 1, 3, 2, 4)
    return a.reshape(batch, lead * CACHE_SUB, LANES)


def _from_cache_rows(a, lead):
    halves = MEM_HEAD_DIM // LANES
    batch = a.shape[0]
    a = a.reshape(batch, lead, halves, MEM_HEADS, LANES).transpose(0, 1, 3, 2, 4)
    return a.reshape(batch, lead, D_MODEL)


MLP_CHUNK = 1024


def _post_kernel(x1_ref, o_ref, wo_ref, gm_ref, wup_ref, wdn_ref, gf_ref, *rest, side_bb, side_len):
    if side_bb:
        qs_ref, ck_ref, cv_ref, y_ref, os_ref = rest
    else:
        (y_ref,) = rest
    x2 = x1_ref[...] + _dot(o_ref[...], wo_ref[...].astype(BF16))
    if side_bb:
        side_start, side_finish = _cached_attention(qs_ref, ck_ref, cv_ref, os_ref, bb=side_bb, length=side_len)
        side_scores = side_start()
    h = _rmsnorm(x2, gm_ref[...]).astype(BF16)
    chunks = [slice(c, c + MLP_CHUNK) for c in range(0, D_FF, MLP_CHUNK)]
    acc = x2
    up = _dot(h, wup_ref[:, chunks[0]])
    for c, cols in enumerate(chunks):
        nxt = _dot(h, wup_ref[:, chunks[c + 1]]) if c + 1 < len(chunks) else None
        if side_bb and c == len(chunks) - 1:
            side_finish(side_scores)
        a = jnp.maximum(up, 0.0)
        acc = acc + _dot((a * a).astype(BF16), wdn_ref[cols, :])
        up = nxt
    y_ref[...] = _rmsnorm(acc, gf_ref[...])


def _attn_out_mlp(x1, o, w_mo, g_mlp, w_up, w_down, g_final, side=None):
    t = x1.shape[0]
    tm = min(t, ROWS_SMALL)
    steps = t // tm
    row = pl.BlockSpec((tm, D_MODEL), lambda i: (i, 0))
    in_specs = [row, row, _full((D_MODEL, D_MODEL)), _full((1, D_MODEL)),
                _full((D_MODEL, D_FF)), _full((D_FF, D_MODEL)), _full((1, D_MODEL))]
    out_specs = [row]
    out_shape = [jax.ShapeDtypeStruct((t, D_MODEL), F32)]
    args = [x1, o, w_mo, g_mlp, w_up, w_down, g_final]
    side_bb = side_len = 0
    if side is not None:
        q_rows = side[0]
        side_bb, rem = divmod(q_rows.shape[0], steps)
        if rem or not side_bb:
            raise ValueError("side attention sequences must spread evenly over the grid steps")
        side_len = q_rows.shape[1] // CACHE_SUB
        blk = lambda r: pl.BlockSpec((side_bb, r, LANES), lambda i: (i, 0, 0))
        in_specs += [blk(q_rows.shape[1]), blk(CACHE_ROWS), blk(CACHE_ROWS)]
        out_specs.append(blk(q_rows.shape[1]))
        out_shape.append(jax.ShapeDtypeStruct(q_rows.shape, BF16))
        args += list(side)
    outs = pl.pallas_call(
        functools.partial(_post_kernel, side_bb=side_bb, side_len=side_len),
        grid=(steps,),
        in_specs=in_specs,
        out_specs=out_specs,
        out_shape=out_shape,
        compiler_params=_params("parallel", fuse=_fuse(len(in_specs), 1, 2, 7)),
        name="attn_out_mlp",
    )(*args)
    return outs if side is not None else outs[0]


def _post_stream_kernel(x1_ref, o_ref, wo_ref, gm_ref, wup_hbm, wdn_hbm, gf_ref, y_ref, up_buf, dn_buf, sem):
    chunks = D_FF // MLP_CHUNK

    def copies(c):
        cols = pl.ds(c * MLP_CHUNK, MLP_CHUNK)
        return (pltpu.make_async_copy(wup_hbm.at[:, cols], up_buf.at[c], sem.at[0, c]),
                pltpu.make_async_copy(wdn_hbm.at[cols, :], dn_buf.at[c], sem.at[1, c]))

    for c in range(chunks):
        for cp in copies(c):
            cp.start()
    x2 = x1_ref[...] + _dot(o_ref[...], wo_ref[...].astype(BF16))
    h = _rmsnorm(x2, gm_ref[...]).astype(BF16)
    acc = x2
    for c in range(chunks):
        up_copy, dn_copy = copies(c)
        up_copy.wait()
        a = jnp.maximum(_dot(h, up_buf[c]), 0.0)
        dn_copy.wait()
        acc = acc + _dot((a * a).astype(BF16), dn_buf[c])
    y_ref[...] = _rmsnorm(acc, gf_ref[...])


def _attn_out_mlp_streamed(x1, o, w_mo, g_mlp, w_up, w_down, g_final):
    t = x1.shape[0]
    chunks = D_FF // MLP_CHUNK
    hbm = pl.BlockSpec(memory_space=pl.ANY)
    return pl.pallas_call(
        _post_stream_kernel,
        grid=(1,),
        in_specs=[_full((t, D_MODEL)), _full((t, D_MODEL)), _full((D_MODEL, D_MODEL)), _full((1, D_MODEL)),
                  hbm, hbm, _full((1, D_MODEL))],
        out_specs=_full((t, D_MODEL)),
        out_shape=jax.ShapeDtypeStruct((t, D_MODEL), F32),
        scratch_shapes=[pltpu.VMEM((chunks, D_MODEL, MLP_CHUNK), BF16),
                        pltpu.VMEM((chunks, MLP_CHUNK, D_MODEL), BF16),
                        pltpu.SemaphoreType.DMA((2, chunks))],
        compiler_params=_params("arbitrary", fuse=_fuse(7, 1, 2)),
        name="attn_out_mlp_streamed",
    )(x1, o, w_mo, g_mlp, w_up, w_down, g_final)


def _mixers(proj, batch, length, s5_re, s5_im, ret_s, p):
    tokens = batch * length
    u, packed, g = proj
    ssm, s5_re_new, s5_im_new = _s5_mixer(
        u.reshape(batch, length, SSM_WIDTH), s5_re.reshape(batch, SSM_LANES), s5_im.reshape(batch, SSM_LANES),
        p["s5_maps"], p["d_skip"], p["w_glu"], batch=batch, length=length)
    tok3 = lambda a: a.reshape(batch, length, RET_WIDTH)
    ret, ret_new = _retention(packed.reshape(batch, length, 4 * RET_WIDTH), tok3(g), ret_s, p["ret_gn"])
    states = (s5_re_new.reshape(1, batch, SSM_GROUPS, SSM_STATE),
              s5_im_new.reshape(1, batch, SSM_GROUPS, SSM_STATE), ret_new[None])
    return ssm.reshape(tokens, SSM_WIDTH), ret.reshape(tokens, RET_WIDTH), states


def kernel(x_prompt, x_sample, mem_prompt, state_s5_re, state_s5_im, state_ret, cache_mem_k, cache_mem_v, g_mix, w_in, lam_re, lam_im, log_dt, b_re, b_im, c_re, c_im, d_skip, w_glu, ret_gn, w_out, g_xattn, g_mem, w_mq, w_mk, w_mv, w_mo, g_mlp, w_up, w_down, g_final):
    if g_mix.shape[0] != 1:
        raise ValueError("this kernel implements the single-layer configuration")
    bp, lp, _ = x_prompt.shape
    bs, ls, _ = x_sample.shape
    vec = lambda a: a.reshape(1, -1).astype(F32)
    xp = x_prompt.reshape(bp * lp, D_MODEL)
    xs = x_sample.reshape(bs * ls, D_MODEL)
    proj_p, (w_in_b, w_up_b, w_down_b) = _project(
        xp, vec(g_mix[0]), w_in[0], length=lp, pos0=0.0, casts=(w_up[0], w_down[0]))
    w_out_b, w_mq_b, w_mo_b, w_mk_b, w_mv_b, w_glu_b = (
        w[0].astype(BF16) for w in (w_out, w_mq, w_mo, w_mk, w_mv, w_glu))
    proj_s, _ = _project(xs, vec(g_mix[0]), w_in_b, length=ls, pos0=float(PAST_LEN))
    p = dict(
        s5_maps=_s5_block_maps(lam_re[0], lam_im[0], log_dt[0], b_re[0], b_im[0], c_re[0], c_im[0]),
        d_skip=vec(d_skip[0]), w_glu=w_glu_b, ret_gn=vec(ret_gn[0]))
    g_x = vec(g_xattn[0])
    mlp = (w_mo_b, vec(g_mlp[0]), w_up_b, w_down_b, vec(g_final))

    ssm_s, ret_s, states_s = _mixers(proj_s, bs, ls, state_s5_re[0], state_s5_im[0], state_ret[0],
                                     dict(p, w_glu=w_glu[0]))
    x1_s, q_s = _out_and_query(xs, ssm_s, ret_s, w_out[0], g_x, w_mq[0])
    if (ls * CACHE_SUB) % BF16_ROWS:
        raise ValueError("cached attention needs whole bf16 tiles of query rows")
    side = (_to_cache_rows(q_s, ls), _to_cache_rows(cache_mem_k, MEM_LEN), _to_cache_rows(cache_mem_v, MEM_LEN))

    mk_rows, mv_rows, mk_p, mv_p = _memory_kv(mem_prompt, vec(g_mem[0]), w_mk_b, w_mv_b)
    zs = jnp.zeros((bp, SSM_GROUPS, SSM_STATE), F32)
    zr = jnp.zeros((bp, RET_HEADS, RET_HEAD_DIM, RET_HEAD_DIM), F32)
    ssm_p, ret_p, states_p = _mixers(proj_p, bp, lp, zs, zs, zr, p)
    x1_p, o_p = _out_and_attention(xp, ssm_p, ret_p, w_out_b, g_x, w_mq_b, mk_p, mv_p, length=lp)
    y_p, o_s = _attn_out_mlp(x1_p, o_p, *mlp, side=side)
    mlp_s = _attn_out_mlp_streamed if bs * ls <= ROWS_SMALL else _attn_out_mlp
    y_s = mlp_s(x1_s, _from_cache_rows(o_s, ls).reshape(bs * ls, D_MODEL), w_mo[0], *mlp[1:])

    kv5 = lambda a: _from_cache_rows(a, MEM_LEN).reshape(1, bp, MEM_LEN, MEM_HEADS, MEM_HEAD_DIM)
    return (y_p.reshape(bp, lp, D_MODEL), y_s.reshape(bs, ls, D_MODEL), *states_p, kv5(mk_rows), kv5(mv_rows),
            *states_s)
```

```python
import functools
import math

import numpy as np
import jax
import jax.numpy as jnp
from jax import lax
from jax.experimental import pallas as pl
from jax.experimental.pallas import tpu as pltpu

F32 = jnp.float32
BF16 = jnp.bfloat16

D_MODEL = 1024
SSM_WIDTH = 512
SSM_GROUP = 16
SSM_GROUPS = 32
SSM_STATE = 64
SSM_LANES = SSM_GROUPS * SSM_STATE
RET_WIDTH = 512
RET_HEADS = 4
RET_HEAD_DIM = 128
RET_CHUNK = 128
ROPE_BASE = 10000.0
MEM_LEN = 256
MEM_HEADS = 4
MEM_HEAD_DIM = 256
MEM_SCALE = MEM_HEAD_DIM ** -0.5
D_FF = 4096
PROJ_WIDTH = SSM_WIDTH + 4 * RET_WIDTH
EPS = 1e-6
PAST_LEN = 16384

LANES = 128
MXU_TILE = 256
BF16_ROWS = 16
VMEM_LIMIT = 56 * 1024 * 1024

ROWS_LARGE = 1024
ROWS_SMALL = 512
S5_STEPS = 256

CACHE_ROWS = MEM_LEN * MEM_HEADS * MEM_HEAD_DIM // LANES
CACHE_SUB = MEM_HEADS * MEM_HEAD_DIM // LANES

assert math.frexp(MEM_SCALE)[0] == 0.5


def _dot(a, b):
    return jnp.dot(a, b, preferred_element_type=F32)


def _dot_nt(a, b):
    return lax.dot_general(a, b, (((1,), (1,)), ((), ())), preferred_element_type=F32)


def _dot_tn(a, b):
    return lax.dot_general(a, b, (((0,), (0,)), ((), ())), preferred_element_type=F32)


def _rmsnorm(x, g):
    return x * lax.rsqrt(jnp.mean(x * x, axis=-1, keepdims=True) + EPS) * g


def _sigmoid(x):
    return 1.0 / (1.0 + jnp.exp(-x))


def _gelu_tanh(x):
    c = math.sqrt(2.0 / math.pi)
    return x * (0.5 * jnp.tanh(x * (c + (c * 0.044715) * (x * x))) + 0.5)


def _params(*sem, fuse=None):
    return pltpu.CompilerParams(dimension_semantics=sem, vmem_limit_bytes=VMEM_LIMIT, allow_input_fusion=fuse)


def _fuse(n_inputs, *which):
    return [i in which for i in range(n_inputs)]


def _full(shape):
    return pl.BlockSpec(shape, lambda *_: (0,) * len(shape))


SSM_BLOCK = 4
SSM_TILE_GROUPS = MXU_TILE // SSM_STATE
SSM_TILES = SSM_GROUPS // SSM_TILE_GROUPS
SSM_TILE_CHANNELS = SSM_TILE_GROUPS * SSM_GROUP
GROUP_SHIFT = SSM_GROUP.bit_length() - 1
STATE_SHIFT = SSM_STATE.bit_length() - 1

assert SSM_BLOCK * SSM_TILE_CHANNELS == MXU_TILE and SSM_TILE_CHANNELS == SSM_STATE


def _s5_prep_kernel(lr_ref, li_ref, ldt_ref, br_ref, bi_ref, cr_ref, ci_ref,
                    ar_ref, ai_ref, winr_ref, wini_ref, woutr_ref, wouti_ref, kdir_ref):
    lr = lr_ref[...]
    li = li_ref[...]
    dt = jnp.exp(ldt_ref[...])
    mag = jnp.exp(lr * dt)
    ab_re = mag * jnp.cos(li * dt)
    ab_im = mag * jnp.sin(li * dt)
    den = lr * lr + li * li
    f_re = ((ab_re - 1.0) * lr + ab_im * li) / den
    f_im = (ab_im * lr - (ab_re - 1.0) * li) / den
    bb_re = f_re * br_ref[...] - f_im * bi_ref[...]
    bb_im = f_re * bi_ref[...] + f_im * br_ref[...]
    cr = cr_ref[...]
    ci = ci_ref[...]
    pows = [(jnp.ones_like(ab_re), jnp.zeros_like(ab_re))]
    for _ in range(SSM_BLOCK):
        pr, pi = pows[-1]
        pows.append((pr * ab_re - pi * ab_im, pr * ab_im + pi * ab_re))
    ar_ref[...], ai_ref[...] = pows[SSM_BLOCK]

    R, T, TG = SSM_BLOCK, SSM_TILES, SSM_TILE_GROUPS
    rows = SSM_GROUPS * SSM_GROUP
    side = SSM_TILE_CHANNELS
    exact = dict(precision=lax.Precision.HIGHEST, preferred_element_type=F32)
    iota = lambda shape, d: lax.broadcasted_iota(jnp.int32, shape, d)
    rep4 = lambda a, axis: jnp.concatenate([a] * TG, axis=axis)
    tiles = lambda a: a.reshape(T, side, a.shape[-1])
    group_of = lambda i: (i & (side - 1)) >> GROUP_SHIFT

    own_in = group_of(iota((rows, MXU_TILE), 0)) == iota((rows, MXU_TILE), 1) >> STATE_SHIFT
    for s in range(R):
        pr, pi = pows[R - 1 - s]
        for ref, w in ((winr_ref, pr * bb_re - pi * bb_im), (wini_ref, pr * bb_im + pi * bb_re)):
            ref[:, side * s:side * (s + 1), :] = tiles(jnp.where(own_in, rep4(w, 1), 0.0)).astype(BF16)

    own_out = iota((MXU_TILE, rows), 0) >> STATE_SHIFT == group_of(iota((MXU_TILE, rows), 1))
    for ref, real in ((woutr_ref, True), (wouti_ref, False)):
        per_lag = []
        for j in range(R):
            pr, pi = pows[j + 1]
            w = cr * pr - ci * pi if real else -(cr * pi + ci * pr)
            wt = w.T
            per_lag.append(jnp.where(own_out, rep4(wt, 0), 0.0))
        for n in range(T):
            ref[n] = jnp.concatenate([m[:, side * n:side * (n + 1)] for m in per_lag], axis=1).astype(BF16)

    grouped = lambda a: a.reshape(SSM_GROUPS, SSM_GROUP, SSM_STATE)
    contract = lambda a, b: lax.dot_general(a, b, (((2,), (2,)), ((0,), (0,))), **exact)
    own_dir = group_of(iota((rows, side), 0)) == iota((rows, side), 1) >> GROUP_SHIFT
    by_lag = []
    for d in range(R):
        pr, pi = pows[d]
        er = grouped(pr * bb_re - pi * bb_im)
        ei = grouped(pr * bb_im + pi * bb_re)
        kt = (contract(er, grouped(cr)) - contract(ei, grouped(ci))).reshape(rows, SSM_GROUP)
        by_lag.append(tiles(jnp.where(own_dir, rep4(kt, 1), 0.0)))
    nothing = jnp.zeros_like(by_lag[0])
    for s in range(R):
        kdir_ref[:, side * s:side * (s + 1), :] = jnp.concatenate(
            [by_lag[j - s] if j >= s else nothing for j in range(R)], axis=2).astype(BF16)


def _s5_block_maps(lam_re, lam_im, log_dt, b_re, b_im, c_re, c_im):
    rows = SSM_GROUPS * SSM_GROUP
    rep = lambda a: jnp.repeat(a, SSM_GROUP, axis=0)
    ldt = jnp.broadcast_to(log_dt[:, None], (SSM_GROUPS, SSM_STATE))
    bt = lambda b: b.transpose(0, 2, 1).reshape(rows, SSM_STATE)
    ct = lambda c: c.reshape(rows, SSM_STATE)
    one = jax.ShapeDtypeStruct((rows, SSM_STATE), F32)
    tile = jax.ShapeDtypeStruct((SSM_TILES, MXU_TILE, MXU_TILE), BF16)
    ar, ai, winr, wini, woutr, wouti, direct = pl.pallas_call(
        _s5_prep_kernel,
        out_shape=(one, one, tile, tile, tile, tile, tile),
        compiler_params=pltpu.CompilerParams(vmem_limit_bytes=VMEM_LIMIT),
        name="s5_block_maps",
    )(rep(lam_re), rep(lam_im), rep(ldt), bt(b_re), bt(b_im), ct(c_re), ct(c_im))
    a_re = ar[::SSM_GROUP].reshape(1, SSM_LANES)
    a_im = ai[::SSM_GROUP].reshape(1, SSM_LANES)
    return a_re, a_im, winr, wini, woutr, wouti, direct


def _retention_chunk(length):
    return RET_CHUNK if length % RET_CHUNK == 0 else length


def _log_gamma():
    return np.log(1.0 - 2.0 ** (-5.0 - np.arange(RET_HEADS, dtype=np.float64)))


def _rope_tables(length, pos0):
    half = RET_HEAD_DIM // 2
    inv = ROPE_BASE ** (-np.arange(half, dtype=np.float64) / half)
    ang = (pos0 + np.arange(length, dtype=np.float64))[:, None] * inv[None, :]
    return (np.concatenate([np.cos(ang), np.cos(ang)], axis=1),
            np.concatenate([-np.sin(ang), np.sin(ang)], axis=1))


def _zeta_rows(length):
    chunk = _retention_chunk(length)
    idx = np.arange(chunk, dtype=np.float64)
    zeta = np.exp((chunk - 1.0 - idx)[None, :] * _log_gamma()[:, None])
    return np.concatenate([np.broadcast_to(z[:, None], (chunk, RET_HEAD_DIM)) for z in zeta], axis=1)


def _decay_tables(length, padded):
    chunk = _retention_chunk(length)
    cpad = chunk if padded == length else padded
    lg = _log_gamma()
    idx = np.arange(chunk, dtype=np.float64)
    diff = idx[:, None] - idx[None, :]
    mask = np.where(diff[None] >= 0, np.exp(np.maximum(diff, 0.0)[None] * lg[:, None, None]), 0.0)
    xi = np.exp((idx + 1.0)[None, :] * lg[:, None])
    gamma_c = tuple(float(v) for v in np.exp(chunk * lg))
    mask = np.pad(mask, ((0, 0), (0, cpad - chunk), (0, cpad - chunk)))
    xi = np.broadcast_to(np.pad(xi, ((0, 0), (0, cpad - chunk)))[:, :, None], (RET_HEADS, cpad, RET_HEAD_DIM))
    f = lambda a: jnp.asarray(np.ascontiguousarray(a), dtype=F32)
    return cpad, gamma_c, f(mask), f(xi)


def _ret_lanes(part, head):
    lo = part * RET_WIDTH + head * RET_HEAD_DIM
    return slice(lo, lo + RET_HEAD_DIM)


def _proj_kernel(x_ref, g_ref, w_ref, cc_ref, ss_ref, zeta_ref, *rest, n_cast):
    casts_in, rest = rest[:n_cast], rest[n_cast:]
    u_ref, ret_ref, gate_ref = rest[:3]
    w = w_ref[...]
    if n_cast:
        w = w.astype(BF16)
        for src, dst in zip(casts_in, rest[3:]):
            dst[...] = src[...].astype(BF16)
    h = _rmsnorm(x_ref[...], g_ref[...]).astype(BF16)
    proj = _dot(h, w)
    u_ref[...] = proj[:, :SSM_WIDTH]
    ret_ref[:, 3 * RET_WIDTH:] = proj[:, SSM_WIDTH + 2 * RET_WIDTH:SSM_WIDTH + 3 * RET_WIDTH].astype(BF16)
    gate_ref[...] = proj[:, SSM_WIDTH + 3 * RET_WIDTH:]
    cc = cc_ref[...]
    ss = ss_ref[...]
    rope = lambda a: a * cc + pltpu.roll(a, RET_HEAD_DIM // 2, axis=1) * ss
    for hd in range(RET_HEADS):
        cols = slice(hd * RET_HEAD_DIM, (hd + 1) * RET_HEAD_DIM)
        q = rope(proj[:, SSM_WIDTH + hd * RET_HEAD_DIM:SSM_WIDTH + (hd + 1) * RET_HEAD_DIM])
        k = rope(proj[:, SSM_WIDTH + RET_WIDTH + hd * RET_HEAD_DIM:SSM_WIDTH + RET_WIDTH + (hd + 1) * RET_HEAD_DIM])
        ret_ref[:, _ret_lanes(0, hd)] = (q * (RET_HEAD_DIM ** -0.5)).astype(BF16)
        ret_ref[:, _ret_lanes(1, hd)] = k.astype(BF16)
        ret_ref[:, _ret_lanes(2, hd)] = (k * zeta_ref[:, cols]).astype(BF16)


def _project(x, g_mix, w_in, *, length, pos0, casts=()):
    t = x.shape[0]
    tm = min(t, ROWS_LARGE)
    steps = t // tm
    chunk = _retention_chunk(length)
    casts = (w_in, *casts) if casts else ()
    if tm % chunk or (length % tm and tm % length):
        raise ValueError("token tiles must hold whole retention chunks of whole or repeated sequences")
    cc, ss = _rope_tables(length, pos0)
    if length < tm:
        cc, ss = np.tile(cc, (tm // length, 1)), np.tile(ss, (tm // length, 1))
    tab_blocks = cc.shape[0] // tm
    zeta = np.tile(_zeta_rows(length), (tm // chunk, 1))
    const = lambda a: jnp.asarray(np.ascontiguousarray(a), dtype=F32)
    row = lambda w: pl.BlockSpec((tm, w), lambda i: (i, 0))
    tab = pl.BlockSpec((tm, RET_HEAD_DIM), lambda i: (i % tab_blocks, 0))
    f = jax.ShapeDtypeStruct((t, SSM_WIDTH), F32)
    h = jax.ShapeDtypeStruct((t, 4 * RET_WIDTH), BF16)
    in_specs = [row(D_MODEL), _full((1, D_MODEL)), _full((D_MODEL, PROJ_WIDTH)), tab, tab, _full((tm, RET_WIDTH))]
    out_specs = [row(SSM_WIDTH), row(4 * RET_WIDTH), row(RET_WIDTH)]
    out_shape = [f, h, f]
    if casts:
        for c in casts:
            rows, rem = divmod(c.shape[0], steps)
            if rem or rows % BF16_ROWS:
                raise ValueError("side-cast weights must split into whole bf16 tiles per grid step")
            blk = pl.BlockSpec((rows, c.shape[1]), lambda i: (i, 0))
            in_specs.append(blk)
            out_specs.append(blk)
            out_shape.append(jax.ShapeDtypeStruct(c.shape, BF16))
    outs = pl.pallas_call(
        functools.partial(_proj_kernel, n_cast=len(casts)),
        grid=(steps,),
        in_specs=in_specs,
        out_specs=out_specs,
        out_shape=out_shape,
        compiler_params=_params("arbitrary" if casts else "parallel", fuse=_fuse(len(in_specs), 0)),
        name="in_proj",
    )(x, g_mix, w_in, const(cc), const(ss), const(zeta), *casts)
    return outs[:3], outs[3:]


def _s5_block_kernel(u_ref, h0r_ref, h0i_ref, ar_ref, ai_ref, winr_ref, wini_ref, woutr_ref, wouti_ref,
                     kdir_ref, d_ref, wglu_ref, out_ref, hr_ref, hi_ref, vr_scr, vi_scr, xr_scr, xi_scr,
                     *, batch, steps, pair, width):
    R, T = SSM_BLOCK, SSM_TILES
    chan = SSM_TILE_CHANNELS
    blocks = steps // R
    rows, brow = steps * batch, blocks * batch

    @pl.when(pl.program_id(0) == 0)
    def _():
        hr_ref[...] = h0r_ref[...]
        hi_ref[...] = h0i_ref[...]

    u = jnp.swapaxes(u_ref[...], 0, 1).reshape(rows, SSM_WIDTH)
    u4 = u.reshape(blocks, R, batch, SSM_WIDTH)
    lag = [u4[:, s].reshape(brow, SSM_WIDTH) for s in range(R)]
    direct = []
    for n in range(T):
        un = jnp.concatenate([lag[s][:, chan * n:chan * (n + 1)] for s in range(R)], axis=1).astype(BF16)
        lanes = slice(MXU_TILE * n, MXU_TILE * (n + 1))
        vr_scr[:, lanes] = _dot(un, winr_ref[n])
        vi_scr[:, lanes] = _dot(un, wini_ref[n])
        direct.append(_dot(un, kdir_ref[n]))

    for j in range(SSM_LANES // width):
        glob = slice(j * width, (j + 1) * width)
        ar = jnp.broadcast_to(ar_ref[:, glob], (batch, width))
        ai = jnp.broadcast_to(ai_ref[:, glob], (batch, width))
        xr = hr_ref[:, glob]
        xi = hi_ref[:, glob]
        for i in range(blocks // pair):
            prev_r, prev_i = [], []
            for s in range(pair):
                at = slice((i * pair + s) * batch, (i * pair + s + 1) * batch)
                prev_r.append(xr)
                prev_i.append(xi)
                xr, xi = ar * xr - ai * xi + vr_scr[at, glob], ar * xi + ai * xr + vi_scr[at, glob]
            blk = slice(i * pair * batch, (i + 1) * pair * batch)
            xr_scr[blk, glob] = jnp.concatenate(prev_r, axis=0).astype(BF16)
            xi_scr[blk, glob] = jnp.concatenate(prev_i, axis=0).astype(BF16)
        hr_ref[:, glob] = xr
        hi_ref[:, glob] = xi

    y4 = []
    for n in range(T):
        lanes = slice(MXU_TILE * n, MXU_TILE * (n + 1))
        y4.append(_dot(xr_scr[:, lanes], woutr_ref[n]) + _dot(xi_scr[:, lanes], wouti_ref[n]) + direct[n])
    per_lag = [jnp.concatenate([y4[n][:, chan * j:chan * (j + 1)] for n in range(T)], axis=1)
               .reshape(blocks, batch, SSM_WIDTH) for j in range(R)]
    y = jnp.stack(per_lag, axis=1).reshape(rows, SSM_WIDTH) + d_ref[...] * u
    z = _gelu_tanh(y)
    out = z * _sigmoid(_dot(z.astype(BF16), wglu_ref[...].astype(BF16)))
    out_ref[...] = jnp.swapaxes(out.reshape(steps, batch, SSM_WIDTH), 0, 1).astype(BF16)


def _s5_mixer(u, h0_re, h0_im, maps, d_skip, w_glu, *, batch, length):
    a_re, a_im, winr, wini, woutr, wouti, direct = maps
    steps = min(length, S5_STEPS)
    if steps % SSM_BLOCK or length % steps:
        raise ValueError("sequence length must be a multiple of the S5 block")
    blocks = steps // SSM_BLOCK
    pair = max(1, min(blocks, BF16_ROWS // batch))
    if (pair * batch) % BF16_ROWS or blocks % pair:
        raise ValueError("S5 state stores need whole bf16 tiles")
    width = SSM_LANES // 2 if batch <= 8 else LANES
    brow = blocks * batch
    kern = functools.partial(_s5_block_kernel, batch=batch, steps=steps, pair=pair, width=width)
    st = jax.ShapeDtypeStruct((batch, SSM_LANES), F32)
    tile = _full((SSM_TILES, MXU_TILE, MXU_TILE))
    return pl.pallas_call(
        kern,
        grid=(length // steps,),
        in_specs=[pl.BlockSpec((batch, steps, SSM_WIDTH), lambda i: (0, i, 0)),
                  _full((batch, SSM_LANES)), _full((batch, SSM_LANES)),
                  _full((1, SSM_LANES)), _full((1, SSM_LANES)),
                  tile, tile, tile, tile, tile,
                  _full((1, SSM_WIDTH)), _full((SSM_WIDTH, SSM_WIDTH))],
        out_specs=[pl.BlockSpec((batch, steps, SSM_WIDTH), lambda i: (0, i, 0)),
                   _full((batch, SSM_LANES)), _full((batch, SSM_LANES))],
        out_shape=(jax.ShapeDtypeStruct((batch, length, SSM_WIDTH), BF16), st, st),
        scratch_shapes=[pltpu.VMEM((brow, SSM_LANES), F32), pltpu.VMEM((brow, SSM_LANES), F32),
                        pltpu.VMEM((brow, SSM_LANES), BF16), pltpu.VMEM((brow, SSM_LANES), BF16)],
        compiler_params=_params("arbitrary", fuse=_fuse(12, 11)),
        name="s5_mixer",
    )(u, h0_re, h0_im, a_re, a_im, winr, wini, woutr, wouti, direct, d_skip, w_glu)


def _ret_kernel(p_ref, g_ref, mask_ref, xi_ref, gain_ref, s0_ref, o_ref, s_ref,
                *, bb, gamma_c, unroll, chunk, chunks):
    @pl.when(pl.program_id(1) == 0)
    def _():
        s_ref[...] = s0_ref[...]

    heads = range(RET_HEADS)
    cols = [slice(h * RET_HEAD_DIM, (h + 1) * RET_HEAD_DIM) for h in heads]

    def one_chunk(b, rows):
        qb = [p_ref[b, rows, _ret_lanes(0, h)] for h in heads]
        vb = [p_ref[b, rows, _ret_lanes(3, h)] for h in heads]
        s_prev = [s_ref[b, h] for h in heads]
        scores = [(_dot_nt(qb[h], p_ref[b, rows, _ret_lanes(1, h)]) * mask_ref[h]).astype(BF16) for h in heads]
        cross = [_dot(qb[h], s_prev[h].astype(BF16)) * xi_ref[h] for h in heads]
        kv = [_dot_tn(p_ref[b, rows, _ret_lanes(2, h)], vb[h]) for h in heads]
        inner = [_dot(scores[h], vb[h]) for h in heads]
        outs = []
        for h in heads:
            s_ref[b, h] = s_prev[h] * gamma_c[h] + kv[h]
            o = inner[h] + cross[h]
            mu = jnp.mean(o, axis=-1, keepdims=True)
            oc = o - mu
            var = jnp.mean(oc * oc, axis=-1, keepdims=True)
            on = oc * lax.rsqrt(var + EPS) * gain_ref[:, cols[h]]
            gate = g_ref[b, rows, cols[h]]
            outs.append((gate * _sigmoid(gate) * on).astype(BF16))
        o_ref[b, rows, :] = jnp.concatenate(outs, axis=1)

    def per_batch(b, carry):
        for c in range(chunks):
            one_chunk(b, slice(c * chunk, (c + 1) * chunk))
        return carry

    if unroll == bb:
        for b in range(bb):
            per_batch(b, 0)
    else:
        lax.fori_loop(0, bb, per_batch, 0, unroll=unroll)


def _retention(packed, g, s0, gn_gain):
    batch, length, _ = g.shape
    padded = pl.cdiv(length, BF16_ROWS) * BF16_ROWS
    chunk, gamma_c, mask, xi = _decay_tables(length, padded)
    if padded != length:
        pad = lambda a: jnp.pad(a, ((0, 0), (0, padded - length), (0, 0)))
        packed, g = pad(packed), pad(g)
    bb, unroll = (8, 2) if chunk > BF16_ROWS else (16, 16)
    chunks = 2 if (padded // chunk) % 2 == 0 else 1
    tok = pl.BlockSpec((bb, chunks * chunk, RET_WIDTH), lambda i, c: (i, c, 0))
    state = pl.BlockSpec((bb, RET_HEADS, RET_HEAD_DIM, RET_HEAD_DIM), lambda i, c: (i, 0, 0, 0))
    out, s_new = pl.pallas_call(
        functools.partial(_ret_kernel, bb=bb, gamma_c=gamma_c, unroll=unroll, chunk=chunk, chunks=chunks),
        grid=(batch // bb, padded // (chunks * chunk)),
        in_specs=[pl.BlockSpec((bb, chunks * chunk, 4 * RET_WIDTH), lambda i, c: (i, c, 0)), tok,
                  _full((RET_HEADS, chunk, chunk)), _full((RET_HEADS, chunk, RET_HEAD_DIM)),
                  _full((1, RET_WIDTH)), state],
        out_specs=[tok, state],
        out_shape=(jax.ShapeDtypeStruct((batch, padded, RET_WIDTH), BF16),
                   jax.ShapeDtypeStruct(s0.shape, F32)),
        compiler_params=_params("parallel", "arbitrary", fuse=_fuse(6, 0, 1)),
        name="retention",
    )(packed, g, mask, xi, gn_gain, s0)
    return out[:, :length], s_new


def _mixer_out(x_ref, ssm_ref, ret_ref, wout_ref, rows=slice(None)):
    return (x_ref[rows, :] + _dot(ssm_ref[rows, :], wout_ref[:SSM_WIDTH, :].astype(BF16))
            + _dot(ret_ref[rows, :], wout_ref[SSM_WIDTH:, :].astype(BF16)))


def _query(x1, g_ref, wq_ref):
    return (_dot(_rmsnorm(x1, g_ref[...]).astype(BF16), wq_ref[...].astype(BF16)) * MEM_SCALE).astype(BF16)


def _outq_kernel(x_ref, ssm_ref, ret_ref, wout_ref, g_ref, wq_ref, x1_ref, q_ref):
    x1 = _mixer_out(x_ref, ssm_ref, ret_ref, wout_ref)
    x1_ref[...] = x1
    q_ref[...] = _query(x1, g_ref, wq_ref)


def _out_and_query(x, ssm, ret, w_out, g_xattn, w_mq):
    t = x.shape[0]
    tm = min(t, ROWS_SMALL)
    row = lambda w: pl.BlockSpec((tm, w), lambda i: (i, 0))
    return pl.pallas_call(
        _outq_kernel,
        grid=(t // tm,),
        in_specs=[row(D_MODEL), row(SSM_WIDTH), row(RET_WIDTH), _full((D_MODEL, D_MODEL)),
                  _full((1, D_MODEL)), _full((D_MODEL, D_MODEL))],
        out_specs=[row(D_MODEL), row(D_MODEL)],
        out_shape=(jax.ShapeDtypeStruct((t, D_MODEL), F32), jax.ShapeDtypeStruct((t, D_MODEL), BF16)),
        compiler_params=_params("parallel", fuse=_fuse(6, 0, 3, 5)),
        name="out_proj_query",
    )(x, ssm, ret, w_out, g_xattn, w_mq)


def _cache_row_order(x):
    halves = MEM_HEAD_DIM // LANES
    tiles = [x[:, (h * halves + d) * LANES:(h * halves + d + 1) * LANES]
             for d in range(halves) for h in range(MEM_HEADS)]
    return jnp.swapaxes(jnp.stack(tiles, axis=0), 0, 1).reshape(x.shape[0] * CACHE_SUB, LANES)


MEMKV_SEQS = 2


def _memkv_kernel(m_ref, g_ref, wk_ref, wv_ref, k_ref, v_ref, kb_ref, vb_ref):
    seqs = m_ref.shape[0]
    m = _rmsnorm(m_ref[...].reshape(seqs * MEM_LEN, D_MODEL), g_ref[...]).astype(BF16)
    k = _dot(m, wk_ref[...])
    v = _dot(m, wv_ref[...])
    k_ref[...] = _cache_row_order(k).reshape(k_ref.shape)
    v_ref[...] = _cache_row_order(v).reshape(v_ref.shape)
    kb_ref[...] = k.astype(BF16).reshape(kb_ref.shape)
    vb_ref[...] = v.astype(BF16).reshape(vb_ref.shape)


def _memory_kv(mem, g_mem, w_mk, w_mv):
    batch = mem.shape[0]
    seqs = MEMKV_SEQS if batch % MEMKV_SEQS == 0 else 1
    tok = pl.BlockSpec((seqs, MEM_LEN, D_MODEL), lambda i: (i, 0, 0))
    cache = pl.BlockSpec((seqs, CACHE_ROWS, LANES), lambda i: (i, 0, 0))
    f = jax.ShapeDtypeStruct((batch, CACHE_ROWS, LANES), F32)
    h = jax.ShapeDtypeStruct((batch, MEM_LEN, D_MODEL), BF16)
    return pl.pallas_call(
        _memkv_kernel,
        grid=(batch // seqs,),
        in_specs=[tok, _full((1, D_MODEL)), _full((D_MODEL, D_MODEL)), _full((D_MODEL, D_MODEL))],
        out_specs=[cache, cache, tok, tok],
        out_shape=(f, f, h, h),
        compiler_params=_params("parallel", fuse=_fuse(4, 2, 3)),
        name="memory_kv",
    )(mem, g_mem, w_mk, w_mv)


ATTN_ROWS = 512


def _outattn_kernel(x_ref, ssm_ref, ret_ref, wout_ref, g_ref, wq_ref, k_ref, v_ref, x1_ref, o_ref):
    cols = [slice(h * MEM_HEAD_DIM, (h + 1) * MEM_HEAD_DIM) for h in range(MEM_HEADS)]
    groups = [slice(r, r + ATTN_ROWS) for r in range(0, x_ref.shape[0], ATTN_ROWS)]

    def scores_of(rows):
        x1 = _mixer_out(x_ref, ssm_ref, ret_ref, wout_ref, rows)
        x1_ref[rows, :] = x1
        q = _query(x1, g_ref, wq_ref)
        return [_dot_nt(q[:, c], k_ref[0, :, c]) for c in cols]

    def attend(rows, scores):
        outs = []
        for s, c in zip(scores, cols):
            e = jnp.exp(s - jnp.max(s, axis=-1, keepdims=True))
            p = (e / jnp.sum(e, axis=-1, keepdims=True)).astype(BF16)
            outs.append(_dot(p, v_ref[0, :, c]).astype(BF16))
        o_ref[rows, :] = jnp.concatenate(outs, axis=1)

    pending = None
    for rows in groups:
        scores = scores_of(rows)
        if pending is not None:
            attend(*pending)
        pending = (rows, scores)
    attend(*pending)


def _out_and_attention(x, ssm, ret, w_out, g_xattn, w_mq, mk, mv, *, length):
    t = x.shape[0]
    tm = min(length, ROWS_LARGE)
    per_seq = length // tm
    row = lambda w: pl.BlockSpec((tm, w), lambda i: (i, 0))
    mem = pl.BlockSpec((1, MEM_LEN, D_MODEL), lambda i: (i // per_seq, 0, 0))
    return pl.pallas_call(
        _outattn_kernel,
        grid=(t // tm,),
        in_specs=[row(D_MODEL), row(SSM_WIDTH), row(RET_WIDTH), _full((D_MODEL, D_MODEL)),
                  _full((1, D_MODEL)), _full((D_MODEL, D_MODEL)), mem, mem],
        out_specs=[row(D_MODEL), row(D_MODEL)],
        out_shape=(jax.ShapeDtypeStruct((t, D_MODEL), F32), jax.ShapeDtypeStruct((t, D_MODEL), BF16)),
        compiler_params=_params("parallel", fuse=_fuse(8, 3, 5)),
        name="out_proj_attention",
    )(x, ssm, ret, w_out, g_xattn, w_mq, mk, mv)


def _cached_attention(q_ref, k_ref, v_ref, o_ref, *, bb, length):
    hit = (lax.broadcasted_iota(jnp.int32, (CACHE_SUB, CACHE_ROWS), 0)
           == lax.broadcasted_iota(jnp.int32, (CACHE_SUB, CACHE_ROWS), 1) % CACHE_SUB)
    first_half = lax.broadcasted_iota(jnp.int32, (length, LANES), 1) % CACHE_SUB < MEM_HEADS
    tiles = CACHE_ROWS // LANES

    def class_reduce(x, op):
        shift = CACHE_SUB
        while shift < LANES:
            x = op(x, pltpu.roll(x, shift, axis=1))
            shift *= 2
        return x

    def softmax_rows(r):
        z = jnp.concatenate(
            [jnp.sum(jnp.where(hit, r[CACHE_SUB * t:CACHE_SUB * (t + 1)], 0.0), axis=0, keepdims=True)
             for t in range(length)], axis=0)
        parts = []
        for i in range(tiles):
            zi = z[:, LANES * i:LANES * (i + 1)]
            parts.append(zi + pltpu.roll(zi, LANES - MEM_HEADS, axis=1))
        mx = parts[0]
        for pi in parts[1:]:
            mx = jnp.maximum(mx, pi)
        mx = class_reduce(mx, jnp.maximum)
        es = [jnp.exp(pi - mx) for pi in parts]
        tot = es[0]
        for ei in es[1:]:
            tot = tot + ei
        tot = class_reduce(tot, jnp.add)
        ps = []
        for ei in es:
            pi = ei / tot
            ps.append(jnp.where(first_half, pi, pltpu.roll(pi, MEM_HEADS, axis=1)))
        p = jnp.concatenate(ps, axis=1)
        return jnp.concatenate(
            [jnp.where(hit, jnp.broadcast_to(p[t:t + 1], (CACHE_SUB, CACHE_ROWS)), 0.0)
             for t in range(length)], axis=0).astype(BF16)

    def start():
        return [_dot_nt(q_ref[b], k_ref[b].astype(BF16)) for b in range(bb)]

    def finish(scores):
        probs = [softmax_rows(r) for r in scores]
        for b in range(bb):
            o_ref[b] = _dot(probs[b], v_ref[b].astype(BF16)).astype(BF16)

    return start, finish


def _to_cache_rows(a, lead):
    halves = MEM_HEAD_DIM // LANES
    batch = a.size // (lead * D_MODEL)
    a = a.reshape(batch, lead, MEM_HEADS, halves, LANES).transpose(0, 1, 3, 2, 4)
    return a.reshape(batch, lead * CACHE_SUB, LANES)


def _from_cache_rows(a, lead):
    halves = MEM_HEAD_DIM // LANES
    batch = a.shape[0]
    a = a.reshape(batch, lead, halves, MEM_HEADS, LANES).transpose(0, 1, 3, 2, 4)
    return a.reshape(batch, lead, D_MODEL)


MLP_CHUNK = 1024


def _post_kernel(x1_ref, o_ref, wo_ref, gm_ref, wup_ref, wdn_ref, gf_ref, *rest, side_bb, side_len):
    if side_bb:
        qs_ref, ck_ref, cv_ref, y_ref, os_ref = rest
    else:
        (y_ref,) = rest
    x2 = x1_ref[...] + _dot(o_ref[...], wo_ref[...].astype(BF16))
    if side_bb:
        side_start, side_finish = _cached_attention(qs_ref, ck_ref, cv_ref, os_ref, bb=side_bb, length=side_len)
        side_scores = side_start()
    h = _rmsnorm(x2, gm_ref[...]).astype(BF16)
    chunks = [slice(c, c + MLP_CHUNK) for c in range(0, D_FF, MLP_CHUNK)]
    acc = x2
    up = _dot(h, wup_ref[:, chunks[0]])
    for c, cols in enumerate(chunks):
        nxt = _dot(h, wup_ref[:, chunks[c + 1]]) if c + 1 < len(chunks) else None
        if side_bb and c == len(chunks) - 1:
            side_finish(side_scores)
        a = jnp.maximum(up, 0.0)
        acc = acc + _dot((a * a).astype(BF16), wdn_ref[cols, :])
        up = nxt
    y_ref[...] = _rmsnorm(acc, gf_ref[...])


def _attn_out_mlp(x1, o, w_mo, g_mlp, w_up, w_down, g_final, side=None):
    t = x1.shape[0]
    tm = min(t, ROWS_SMALL)
    steps = t // tm
    row = pl.BlockSpec((tm, D_MODEL), lambda i: (i, 0))
    in_specs = [row, row, _full((D_MODEL, D_MODEL)), _full((1, D_MODEL)),
                _full((D_MODEL, D_FF)), _full((D_FF, D_MODEL)), _full((1, D_MODEL))]
    out_specs = [row]
    out_shape = [jax.ShapeDtypeStruct((t, D_MODEL), F32)]
    args = [x1, o, w_mo, g_mlp, w_up, w_down, g_final]
    side_bb = side_len = 0
    if side is not None:
        q_rows = side[0]
        side_bb, rem = divmod(q_rows.shape[0], steps)
        if rem or not side_bb:
            raise ValueError("side attention sequences must spread evenly over the grid steps")
        side_len = q_rows.shape[1] // CACHE_SUB
        blk = lambda r: pl.BlockSpec((side_bb, r, LANES), lambda i: (i, 0, 0))
        in_specs += [blk(q_rows.shape[1]), blk(CACHE_ROWS), blk(CACHE_ROWS)]
        out_specs.append(blk(q_rows.shape[1]))
        out_shape.append(jax.ShapeDtypeStruct(q_rows.shape, BF16))
        args += list(side)
    outs = pl.pallas_call(
        functools.partial(_post_kernel, side_bb=side_bb, side_len=side_len),
        grid=(steps,),
        in_specs=in_specs,
        out_specs=out_specs,
        out_shape=out_shape,
        compiler_params=_params("parallel", fuse=_fuse(len(in_specs), 1, 2, 7)),
        name="attn_out_mlp",
    )(*args)
    return outs if side is not None else outs[0]


def _post_stream_kernel(x1_ref, o_ref, wo_ref, gm_ref, wup_hbm, wdn_hbm, gf_ref, y_ref, up_buf, dn_buf, sem):
    chunks = D_FF // MLP_CHUNK

    def copies(c):
        cols = pl.ds(c * MLP_CHUNK, MLP_CHUNK)
        return (pltpu.make_async_copy(wup_hbm.at[:, cols], up_buf.at[c], sem.at[0, c]),
                pltpu.make_async_copy(wdn_hbm.at[cols, :], dn_buf.at[c], sem.at[1, c]))

    for c in range(chunks):
        for cp in copies(c):
            cp.start(priority=c % 2)
    x2 = x1_ref[...] + _dot(o_ref[...], wo_ref[...].astype(BF16))
    h = _rmsnorm(x2, gm_ref[...]).astype(BF16)
    acc = x2
    for c in range(chunks):
        up_copy, dn_copy = copies(c)
        up_copy.wait()
        a = jnp.maximum(_dot(h, up_buf[c]), 0.0)
        dn_copy.wait()
        acc = acc + _dot((a * a).astype(BF16), dn_buf[c])
    y_ref[...] = _rmsnorm(acc, gf_ref[...])


def _attn_out_mlp_streamed(x1, o, w_mo, g_mlp, w_up, w_down, g_final):
    t = x1.shape[0]
    chunks = D_FF // MLP_CHUNK
    hbm = pl.BlockSpec(memory_space=pl.ANY)
    return pl.pallas_call(
        _post_stream_kernel,
        grid=(1,),
        in_specs=[_full((t, D_MODEL)), _full((t, D_MODEL)), _full((D_MODEL, D_MODEL)), _full((1, D_MODEL)),
                  hbm, hbm, _full((1, D_MODEL))],
        out_specs=_full((t, D_MODEL)),
        out_shape=jax.ShapeDtypeStruct((t, D_MODEL), F32),
        scratch_shapes=[pltpu.VMEM((chunks, D_MODEL, MLP_CHUNK), BF16),
                        pltpu.VMEM((chunks, MLP_CHUNK, D_MODEL), BF16),
                        pltpu.SemaphoreType.DMA((2, chunks))],
        compiler_params=_params("arbitrary", fuse=_fuse(7, 1, 2)),
        name="attn_out_mlp_streamed",
    )(x1, o, w_mo, g_mlp, w_up, w_down, g_final)


def _mixers(proj, batch, length, s5_re, s5_im, ret_s, p):
    tokens = batch * length
    u, packed, g = proj
    ssm, s5_re_new, s5_im_new = _s5_mixer(
        u.reshape(batch, length, SSM_WIDTH), s5_re.reshape(batch, SSM_LANES), s5_im.reshape(batch, SSM_LANES),
        p["s5_maps"], p["d_skip"], p["w_glu"], batch=batch, length=length)
    tok3 = lambda a: a.reshape(batch, length, RET_WIDTH)
    ret, ret_new = _retention(packed.reshape(batch, length, 4 * RET_WIDTH), tok3(g), ret_s, p["ret_gn"])
    states = (s5_re_new.reshape(1, batch, SSM_GROUPS, SSM_STATE),
              s5_im_new.reshape(1, batch, SSM_GROUPS, SSM_STATE), ret_new[None])
    return ssm.reshape(tokens, SSM_WIDTH), ret.reshape(tokens, RET_WIDTH), states


def kernel(x_prompt, x_sample, mem_prompt, state_s5_re, state_s5_im, state_ret, cache_mem_k, cache_mem_v, g_mix, w_in, lam_re, lam_im, log_dt, b_re, b_im, c_re, c_im, d_skip, w_glu, ret_gn, w_out, g_xattn, g_mem, w_mq, w_mk, w_mv, w_mo, g_mlp, w_up, w_down, g_final):
    if g_mix.shape[0] != 1:
        raise ValueError("this kernel implements the single-layer configuration")
    bp, lp, _ = x_prompt.shape
    bs, ls, _ = x_sample.shape
    vec = lambda a: a.reshape(1, -1).astype(F32)
    xp = x_prompt.reshape(bp * lp, D_MODEL)
    xs = x_sample.reshape(bs * ls, D_MODEL)
    proj_p, (w_in_b, w_up_b, w_down_b) = _project(
        xp, vec(g_mix[0]), w_in[0], length=lp, pos0=0.0, casts=(w_up[0], w_down[0]))
    w_out_b, w_mq_b, w_mo_b, w_mk_b, w_mv_b, w_glu_b = (
        w[0].astype(BF16) for w in (w_out, w_mq, w_mo, w_mk, w_mv, w_glu))
    proj_s, _ = _project(xs, vec(g_mix[0]), w_in_b, length=ls, pos0=float(PAST_LEN))
    p = dict(
        s5_maps=_s5_block_maps(lam_re[0], lam_im[0], log_dt[0], b_re[0], b_im[0], c_re[0], c_im[0]),
        d_skip=vec(d_skip[0]), w_glu=w_glu_b, ret_gn=vec(ret_gn[0]))
    g_x = vec(g_xattn[0])
    mlp = (w_mo_b, vec(g_mlp[0]), w_up_b, w_down_b, vec(g_final))

    ssm_s, ret_s, states_s = _mixers(proj_s, bs, ls, state_s5_re[0], state_s5_im[0], state_ret[0],
                                     dict(p, w_glu=w_glu[0]))
    x1_s, q_s = _out_and_query(xs, ssm_s, ret_s, w_out[0], g_x, w_mq[0])
    if (ls * CACHE_SUB) % BF16_ROWS:
        raise ValueError("cached attention needs whole bf16 tiles of query rows")
    side = (_to_cache_rows(q_s, ls), _to_cache_rows(cache_mem_k, MEM_LEN), _to_cache_rows(cache_mem_v, MEM_LEN))

    mk_rows, mv_rows, mk_p, mv_p = _memory_kv(mem_prompt, vec(g_mem[0]), w_mk_b, w_mv_b)
    zs = jnp.zeros((bp, SSM_GROUPS, SSM_STATE), F32)
    zr = jnp.zeros((bp, RET_HEADS, RET_HEAD_DIM, RET_HEAD_DIM), F32)
    ssm_p, ret_p, states_p = _mixers(proj_p, bp, lp, zs, zs, zr, p)
    x1_p, o_p = _out_and_attention(xp, ssm_p, ret_p, w_out_b, g_x, w_mq_b, mk_p, mv_p, length=lp)
    y_p, o_s = _attn_out_mlp(x1_p, o_p, *mlp, side=side)
    mlp_s = _attn_out_mlp_streamed if bs * ls <= ROWS_SMALL else _attn_out_mlp
    y_s = mlp_s(x1_s, _from_cache_rows(o_s, ls).reshape(bs * ls, D_MODEL), w_mo[0], *mlp[1:])

    kv5 = lambda a: _from_cache_rows(a, MEM_LEN).reshape(1, bp, MEM_LEN, MEM_HEADS, MEM_HEAD_DIM)
    return (y_p.reshape(bp, lp, D_MODEL), y_s.reshape(bs, ls, D_MODEL), *states_p, kv5(mk_rows), kv5(mv_rows),
            *states_s)
```

```python
import functools
import math

import numpy as np
import jax
import jax.numpy as jnp
from jax import lax
from jax.experimental import pallas as pl
from jax.experimental.pallas import tpu as pltpu

F32 = jnp.float32
BF16 = jnp.bfloat16

D_MODEL = 1024
SSM_WIDTH = 512
SSM_GROUP = 16
SSM_GROUPS = 32
SSM_STATE = 64
SSM_LANES = SSM_GROUPS * SSM_STATE
RET_WIDTH = 512
RET_HEADS = 4
RET_HEAD_DIM = 128
RET_CHUNK = 128
ROPE_BASE = 10000.0
MEM_LEN = 256
MEM_HEADS = 4
MEM_HEAD_DIM = 256
MEM_SCALE = MEM_HEAD_DIM ** -0.5
D_FF = 4096
PROJ_WIDTH = SSM_WIDTH + 4 * RET_WIDTH
EPS = 1e-6
PAST_LEN = 16384

LANES = 128
MXU_TILE = 256
BF16_ROWS = 16
VMEM_LIMIT = 56 * 1024 * 1024

ROWS_LARGE = 1024
ROWS_SMALL = 512
S5_STEPS = 256

CACHE_ROWS = MEM_LEN * MEM_HEADS * MEM_HEAD_DIM // LANES
CACHE_SUB = MEM_HEADS * MEM_HEAD_DIM // LANES

assert math.frexp(MEM_SCALE)[0] == 0.5


def _dot(a, b):
    return jnp.dot(a, b, preferred_element_type=F32)


def _dot_nt(a, b):
    return lax.dot_general(a, b, (((1,), (1,)), ((), ())), preferred_element_type=F32)


def _dot_tn(a, b):
    return lax.dot_general(a, b, (((0,), (0,)), ((), ())), preferred_element_type=F32)


def _rmsnorm(x, g):
    return x * lax.rsqrt(jnp.mean(x * x, axis=-1, keepdims=True) + EPS) * g


def _sigmoid(x):
    return 1.0 / (1.0 + jnp.exp(-x))


def _gelu_tanh(x):
    c = math.sqrt(2.0 / math.pi)
    return x * (0.5 * jnp.tanh(x * (c + (c * 0.044715) * (x * x))) + 0.5)


def _params(*sem, fuse=None):
    return pltpu.CompilerParams(dimension_semantics=sem, vmem_limit_bytes=VMEM_LIMIT, allow_input_fusion=fuse)


def _fuse(n_inputs, *which):
    return [i in which for i in range(n_inputs)]


def _full(shape):
    return pl.BlockSpec(shape, lambda *_: (0,) * len(shape))


SSM_BLOCK = 4
SSM_TILE_GROUPS = MXU_TILE // SSM_STATE
SSM_TILES = SSM_GROUPS // SSM_TILE_GROUPS
SSM_TILE_CHANNELS = SSM_TILE_GROUPS * SSM_GROUP
GROUP_SHIFT = SSM_GROUP.bit_length() - 1
STATE_SHIFT = SSM_STATE.bit_length() - 1

assert SSM_BLOCK * SSM_TILE_CHANNELS == MXU_TILE and SSM_TILE_CHANNELS == SSM_STATE


def _s5_prep_kernel(lr_ref, li_ref, ldt_ref, br_ref, bi_ref, cr_ref, ci_ref,
                    ar_ref, ai_ref, winr_ref, wini_ref, woutr_ref, wouti_ref, kdir_ref):
    lr = lr_ref[...]
    li = li_ref[...]
    dt = jnp.exp(ldt_ref[...])
    mag = jnp.exp(lr * dt)
    ab_re = mag * jnp.cos(li * dt)
    ab_im = mag * jnp.sin(li * dt)
    den = lr * lr + li * li
    f_re = ((ab_re - 1.0) * lr + ab_im * li) / den
    f_im = (ab_im * lr - (ab_re - 1.0) * li) / den
    bb_re = f_re * br_ref[...] - f_im * bi_ref[...]
    bb_im = f_re * bi_ref[...] + f_im * br_ref[...]
    cr = cr_ref[...]
    ci = ci_ref[...]
    pows = [(jnp.ones_like(ab_re), jnp.zeros_like(ab_re))]
    for _ in range(SSM_BLOCK):
        pr, pi = pows[-1]
        pows.append((pr * ab_re - pi * ab_im, pr * ab_im + pi * ab_re))
    ar_ref[...], ai_ref[...] = pows[SSM_BLOCK]

    R, T, TG = SSM_BLOCK, SSM_TILES, SSM_TILE_GROUPS
    rows = SSM_GROUPS * SSM_GROUP
    side = SSM_TILE_CHANNELS
    exact = dict(precision=lax.Precision.HIGHEST, preferred_element_type=F32)
    iota = lambda shape, d: lax.broadcasted_iota(jnp.int32, shape, d)
    rep4 = lambda a, axis: jnp.concatenate([a] * TG, axis=axis)
    tiles = lambda a: a.reshape(T, side, a.shape[-1])
    group_of = lambda i: (i & (side - 1)) >> GROUP_SHIFT

    own_in = group_of(iota((rows, MXU_TILE), 0)) == iota((rows, MXU_TILE), 1) >> STATE_SHIFT
    for s in range(R):
        pr, pi = pows[R - 1 - s]
        for ref, w in ((winr_ref, pr * bb_re - pi * bb_im), (wini_ref, pr * bb_im + pi * bb_re)):
            ref[:, side * s:side * (s + 1), :] = tiles(jnp.where(own_in, rep4(w, 1), 0.0)).astype(BF16)

    own_out = iota((MXU_TILE, rows), 0) >> STATE_SHIFT == group_of(iota((MXU_TILE, rows), 1))
    for ref, real in ((woutr_ref, True), (wouti_ref, False)):
        per_lag = []
        for j in range(R):
            pr, pi = pows[j + 1]
            w = cr * pr - ci * pi if real else -(cr * pi + ci * pr)
            wt = w.T
            per_lag.append(jnp.where(own_out, rep4(wt, 0), 0.0))
        for n in range(T):
            ref[n] = jnp.concatenate([m[:, side * n:side * (n + 1)] for m in per_lag], axis=1).astype(BF16)

    grouped = lambda a: a.reshape(SSM_GROUPS, SSM_GROUP, SSM_STATE)
    contract = lambda a, b: lax.dot_general(a, b, (((2,), (2,)), ((0,), (0,))), **exact)
    own_dir = group_of(iota((rows, side), 0)) == iota((rows, side), 1) >> GROUP_SHIFT
    by_lag = []
    for d in range(R):
        pr, pi = pows[d]
        er = grouped(pr * bb_re - pi * bb_im)
        ei = grouped(pr * bb_im + pi * bb_re)
        kt = (contract(er, grouped(cr)) - contract(ei, grouped(ci))).reshape(rows, SSM_GROUP)
        by_lag.append(tiles(jnp.where(own_dir, rep4(kt, 1), 0.0)))
    nothing = jnp.zeros_like(by_lag[0])
    for s in range(R):
        kdir_ref[:, side * s:side * (s + 1), :] = jnp.concatenate(
            [by_lag[j - s] if j >= s else nothing for j in range(R)], axis=2).astype(BF16)


def _s5_block_maps(lam_re, lam_im, log_dt, b_re, b_im, c_re, c_im):
    rows = SSM_GROUPS * SSM_GROUP
    rep = lambda a: jnp.repeat(a, SSM_GROUP, axis=0)
    ldt = jnp.broadcast_to(log_dt[:, None], (SSM_GROUPS, SSM_STATE))
    bt = lambda b: b.transpose(0, 2, 1).reshape(rows, SSM_STATE)
    ct = lambda c: c.reshape(rows, SSM_STATE)
    one = jax.ShapeDtypeStruct((rows, SSM_STATE), F32)
    tile = jax.ShapeDtypeStruct((SSM_TILES, MXU_TILE, MXU_TILE), BF16)
    ar, ai, winr, wini, woutr, wouti, direct = pl.pallas_call(
        _s5_prep_kernel,
        out_shape=(one, one, tile, tile, tile, tile, tile),
        compiler_params=pltpu.CompilerParams(vmem_limit_bytes=VMEM_LIMIT),
        name="s5_block_maps",
    )(rep(lam_re), rep(lam_im), rep(ldt), bt(b_re), bt(b_im), ct(c_re), ct(c_im))
    a_re = ar[::SSM_GROUP].reshape(1, SSM_LANES)
    a_im = ai[::SSM_GROUP].reshape(1, SSM_LANES)
    return a_re, a_im, winr, wini, woutr, wouti, direct


def _retention_chunk(length):
    return RET_CHUNK if length % RET_CHUNK == 0 else length


def _log_gamma():
    return np.log(1.0 - 2.0 ** (-5.0 - np.arange(RET_HEADS, dtype=np.float64)))


def _rope_tables(length, pos0):
    half = RET_HEAD_DIM // 2
    inv = ROPE_BASE ** (-np.arange(half, dtype=np.float64) / half)
    ang = (pos0 + np.arange(length, dtype=np.float64))[:, None] * inv[None, :]
    return (np.concatenate([np.cos(ang), np.cos(ang)], axis=1),
            np.concatenate([-np.sin(ang), np.sin(ang)], axis=1))


def _zeta_rows(length):
    chunk = _retention_chunk(length)
    idx = np.arange(chunk, dtype=np.float64)
    zeta = np.exp((chunk - 1.0 - idx)[None, :] * _log_gamma()[:, None])
    return np.concatenate([np.broadcast_to(z[:, None], (chunk, RET_HEAD_DIM)) for z in zeta], axis=1)


def _decay_tables(length, padded):
    chunk = _retention_chunk(length)
    cpad = chunk if padded == length else padded
    lg = _log_gamma()
    idx = np.arange(chunk, dtype=np.float64)
    diff = idx[:, None] - idx[None, :]
    mask = np.where(diff[None] >= 0, np.exp(np.maximum(diff, 0.0)[None] * lg[:, None, None]), 0.0)
    xi = np.exp((idx + 1.0)[None, :] * lg[:, None])
    gamma_c = tuple(float(v) for v in np.exp(chunk * lg))
    mask = np.pad(mask, ((0, 0), (0, cpad - chunk), (0, cpad - chunk)))
    xi = np.broadcast_to(np.pad(xi, ((0, 0), (0, cpad - chunk)))[:, :, None], (RET_HEADS, cpad, RET_HEAD_DIM))
    f = lambda a: jnp.asarray(np.ascontiguousarray(a), dtype=F32)
    return cpad, gamma_c, f(mask), f(xi)


def _ret_lanes(part, head):
    lo = part * RET_WIDTH + head * RET_HEAD_DIM
    return slice(lo, lo + RET_HEAD_DIM)


def _proj_kernel(x_ref, g_ref, w_ref, cc_ref, ss_ref, zeta_ref, *rest, n_cast):
    casts_in, rest = rest[:n_cast], rest[n_cast:]
    u_ref, ret_ref, gate_ref = rest[:3]
    w = w_ref[...]
    if n_cast:
        w = w.astype(BF16)
        for src, dst in zip(casts_in, rest[3:]):
            dst[...] = src[...].astype(BF16)
    h = _rmsnorm(x_ref[...], g_ref[...]).astype(BF16)
    proj = _dot(h, w)
    u_ref[...] = proj[:, :SSM_WIDTH]
    ret_ref[:, 3 * RET_WIDTH:] = proj[:, SSM_WIDTH + 2 * RET_WIDTH:SSM_WIDTH + 3 * RET_WIDTH].astype(BF16)
    gate_ref[...] = proj[:, SSM_WIDTH + 3 * RET_WIDTH:]
    cc = cc_ref[...]
    ss = ss_ref[...]
    rope = lambda a: a * cc + pltpu.roll(a, RET_HEAD_DIM // 2, axis=1) * ss
    for hd in range(RET_HEADS):
        cols = slice(hd * RET_HEAD_DIM, (hd + 1) * RET_HEAD_DIM)
        q = rope(proj[:, SSM_WIDTH + hd * RET_HEAD_DIM:SSM_WIDTH + (hd + 1) * RET_HEAD_DIM])
        k = rope(proj[:, SSM_WIDTH + RET_WIDTH + hd * RET_HEAD_DIM:SSM_WIDTH + RET_WIDTH + (hd + 1) * RET_HEAD_DIM])
        ret_ref[:, _ret_lanes(0, hd)] = (q * (RET_HEAD_DIM ** -0.5)).astype(BF16)
        ret_ref[:, _ret_lanes(1, hd)] = k.astype(BF16)
        ret_ref[:, _ret_lanes(2, hd)] = (k * zeta_ref[:, cols]).astype(BF16)


def _project(x, g_mix, w_in, *, length, pos0, casts=()):
    t = x.shape[0]
    tm = min(t, ROWS_LARGE)
    steps = t // tm
    chunk = _retention_chunk(length)
    casts = (w_in, *casts) if casts else ()
    if tm % chunk or (length % tm and tm % length):
        raise ValueError("token tiles must hold whole retention chunks of whole or repeated sequences")
    cc, ss = _rope_tables(length, pos0)
    if length < tm:
        cc, ss = np.tile(cc, (tm // length, 1)), np.tile(ss, (tm // length, 1))
    tab_blocks = cc.shape[0] // tm
    zeta = np.tile(_zeta_rows(length), (tm // chunk, 1))
    const = lambda a: jnp.asarray(np.ascontiguousarray(a), dtype=F32)
    row = lambda w: pl.BlockSpec((tm, w), lambda i: (i, 0))
    tab = pl.BlockSpec((tm, RET_HEAD_DIM), lambda i: (i % tab_blocks, 0))
    f = jax.ShapeDtypeStruct((t, SSM_WIDTH), F32)
    h = jax.ShapeDtypeStruct((t, 4 * RET_WIDTH), BF16)
    in_specs = [row(D_MODEL), _full((1, D_MODEL)), _full((D_MODEL, PROJ_WIDTH)), tab, tab, _full((tm, RET_WIDTH))]
    out_specs = [row(SSM_WIDTH), row(4 * RET_WIDTH), row(RET_WIDTH)]
    out_shape = [f, h, f]
    if casts:
        for c in casts:
            rows, rem = divmod(c.shape[0], steps)
            if rem or rows % BF16_ROWS:
                raise ValueError("side-cast weights must split into whole bf16 tiles per grid step")
            blk = pl.BlockSpec((rows, c.shape[1]), lambda i: (i, 0))
            in_specs.append(blk)
            out_specs.append(blk)
            out_shape.append(jax.ShapeDtypeStruct(c.shape, BF16))
    outs = pl.pallas_call(
        functools.partial(_proj_kernel, n_cast=len(casts)),
        grid=(steps,),
        in_specs=in_specs,
        out_specs=out_specs,
        out_shape=out_shape,
        compiler_params=_params("arbitrary" if casts else "parallel", fuse=_fuse(len(in_specs), 0)),
        name="in_proj",
    )(x, g_mix, w_in, const(cc), const(ss), const(zeta), *casts)
    return outs[:3], outs[3:]


def _s5_block_kernel(u_ref, h0r_ref, h0i_ref, ar_ref, ai_ref, winr_ref, wini_ref, woutr_ref, wouti_ref,
                     kdir_ref, d_ref, wglu_ref, out_ref, hr_ref, hi_ref, vr_scr, vi_scr, xr_scr, xi_scr,
                     *, batch, steps, pair, width):
    R, T = SSM_BLOCK, SSM_TILES
    chan = SSM_TILE_CHANNELS
    blocks = steps // R
    rows, brow = steps * batch, blocks * batch

    @pl.when(pl.program_id(0) == 0)
    def _():
        hr_ref[...] = h0r_ref[...]
        hi_ref[...] = h0i_ref[...]

    u = jnp.swapaxes(u_ref[...], 0, 1).reshape(rows, SSM_WIDTH)
    u4 = u.reshape(blocks, R, batch, SSM_WIDTH)
    lag = [u4[:, s].reshape(brow, SSM_WIDTH) for s in range(R)]
    direct = []
    for n in range(T):
        un = jnp.concatenate([lag[s][:, chan * n:chan * (n + 1)] for s in range(R)], axis=1).astype(BF16)
        lanes = slice(MXU_TILE * n, MXU_TILE * (n + 1))
        vr_scr[:, lanes] = _dot(un, winr_ref[n])
        vi_scr[:, lanes] = _dot(un, wini_ref[n])
        direct.append(_dot(un, kdir_ref[n]))

    for j in range(SSM_LANES // width):
        glob = slice(j * width, (j + 1) * width)
        ar = jnp.broadcast_to(ar_ref[:, glob], (batch, width))
        ai = jnp.broadcast_to(ai_ref[:, glob], (batch, width))
        xr = hr_ref[:, glob]
        xi = hi_ref[:, glob]
        for i in range(blocks // pair):
            prev_r, prev_i = [], []
            for s in range(pair):
                at = slice((i * pair + s) * batch, (i * pair + s + 1) * batch)
                prev_r.append(xr)
                prev_i.append(xi)
                xr, xi = ar * xr - ai * xi + vr_scr[at, glob], ar * xi + ai * xr + vi_scr[at, glob]
            blk = slice(i * pair * batch, (i + 1) * pair * batch)
            xr_scr[blk, glob] = jnp.concatenate(prev_r, axis=0).astype(BF16)
            xi_scr[blk, glob] = jnp.concatenate(prev_i, axis=0).astype(BF16)
        hr_ref[:, glob] = xr
        hi_ref[:, glob] = xi

    y4 = []
    for n in range(T):
        lanes = slice(MXU_TILE * n, MXU_TILE * (n + 1))
        y4.append(_dot(xr_scr[:, lanes], woutr_ref[n]) + _dot(xi_scr[:, lanes], wouti_ref[n]) + direct[n])
    per_lag = [jnp.concatenate([y4[n][:, chan * j:chan * (j + 1)] for n in range(T)], axis=1)
               .reshape(blocks, batch, SSM_WIDTH) for j in range(R)]
    y = jnp.stack(per_lag, axis=1).reshape(rows, SSM_WIDTH) + d_ref[...] * u
    z = _gelu_tanh(y)
    out = z * _sigmoid(_dot(z.astype(BF16), wglu_ref[...].astype(BF16)))
    out_ref[...] = jnp.swapaxes(out.reshape(steps, batch, SSM_WIDTH), 0, 1).astype(BF16)


def _s5_mixer(u, h0_re, h0_im, maps, d_skip, w_glu, *, batch, length):
    a_re, a_im, winr, wini, woutr, wouti, direct = maps
    steps = min(length, S5_STEPS)
    if steps % SSM_BLOCK or length % steps:
        raise ValueError("sequence length must be a multiple of the S5 block")
    blocks = steps // SSM_BLOCK
    pair = max(1, min(blocks, BF16_ROWS // batch))
    if (pair * batch) % BF16_ROWS or blocks % pair:
        raise ValueError("S5 state stores need whole bf16 tiles")
    width = SSM_LANES // 2 if batch <= 8 else LANES
    brow = blocks * batch
    kern = functools.partial(_s5_block_kernel, batch=batch, steps=steps, pair=pair, width=width)
    st = jax.ShapeDtypeStruct((batch, SSM_LANES), F32)
    tile = _full((SSM_TILES, MXU_TILE, MXU_TILE))
    return pl.pallas_call(
        kern,
        grid=(length // steps,),
        in_specs=[pl.BlockSpec((batch, steps, SSM_WIDTH), lambda i: (0, i, 0)),
                  _full((batch, SSM_LANES)), _full((batch, SSM_LANES)),
                  _full((1, SSM_LANES)), _full((1, SSM_LANES)),
                  tile, tile, tile, tile, tile,
                  _full((1, SSM_WIDTH)), _full((SSM_WIDTH, SSM_WIDTH))],
        out_specs=[pl.BlockSpec((batch, steps, SSM_WIDTH), lambda i: (0, i, 0)),
                   _full((batch, SSM_LANES)), _full((batch, SSM_LANES))],
        out_shape=(jax.ShapeDtypeStruct((batch, length, SSM_WIDTH), BF16), st, st),
        scratch_shapes=[pltpu.VMEM((brow, SSM_LANES), F32), pltpu.VMEM((brow, SSM_LANES), F32),
                        pltpu.VMEM((brow, SSM_LANES), BF16), pltpu.VMEM((brow, SSM_LANES), BF16)],
        compiler_params=_params("arbitrary", fuse=_fuse(12, 11)),
        name="s5_mixer",
    )(u, h0_re, h0_im, a_re, a_im, winr, wini, woutr, wouti, direct, d_skip, w_glu)


def _ret_kernel(p_ref, g_ref, mask_ref, xi_ref, gain_ref, s0_ref, o_ref, s_ref,
                *, bb, gamma_c, unroll, chunk, chunks):
    @pl.when(pl.program_id(1) == 0)
    def _():
        s_ref[...] = s0_ref[...]

    heads = range(RET_HEADS)
    cols = [slice(h * RET_HEAD_DIM, (h + 1) * RET_HEAD_DIM) for h in heads]

    def one_chunk(b, rows):
        qb = [p_ref[b, rows, _ret_lanes(0, h)] for h in heads]
        vb = [p_ref[b, rows, _ret_lanes(3, h)] for h in heads]
        s_prev = [s_ref[b, h] for h in heads]
        scores = [(_dot_nt(qb[h], p_ref[b, rows, _ret_lanes(1, h)]) * mask_ref[h]).astype(BF16) for h in heads]
        cross = [_dot(qb[h], s_prev[h].astype(BF16)) * xi_ref[h] for h in heads]
        kv = [_dot_tn(p_ref[b, rows, _ret_lanes(2, h)], vb[h]) for h in heads]
        inner = [_dot(scores[h], vb[h]) for h in heads]
        outs = []
        for h in heads:
            s_ref[b, h] = s_prev[h] * gamma_c[h] + kv[h]
            o = inner[h] + cross[h]
            mu = jnp.mean(o, axis=-1, keepdims=True)
            oc = o - mu
            var = jnp.mean(oc * oc, axis=-1, keepdims=True)
            on = oc * lax.rsqrt(var + EPS) * gain_ref[:, cols[h]]
            gate = g_ref[b, rows, cols[h]]
            outs.append((gate * _sigmoid(gate) * on).astype(BF16))
        o_ref[b, rows, :] = jnp.concatenate(outs, axis=1)

    def per_batch(b, carry):
        for c in range(chunks):
            one_chunk(b, slice(c * chunk, (c + 1) * chunk))
        return carry

    if unroll == bb:
        for b in range(bb):
            per_batch(b, 0)
    else:
        lax.fori_loop(0, bb, per_batch, 0, unroll=unroll)


def _retention(packed, g, s0, gn_gain):
    batch, length, _ = g.shape
    padded = pl.cdiv(length, BF16_ROWS) * BF16_ROWS
    chunk, gamma_c, mask, xi = _decay_tables(length, padded)
    if padded != length:
        pad = lambda a: jnp.pad(a, ((0, 0), (0, padded - length), (0, 0)))
        packed, g = pad(packed), pad(g)
    bb, unroll = (8, 2) if chunk > BF16_ROWS else (16, 16)
    chunks = 2 if (padded // chunk) % 2 == 0 else 1
    tok = pl.BlockSpec((bb, chunks * chunk, RET_WIDTH), lambda i, c: (i, c, 0))
    state = pl.BlockSpec((bb, RET_HEADS, RET_HEAD_DIM, RET_HEAD_DIM), lambda i, c: (i, 0, 0, 0))
    out, s_new = pl.pallas_call(
        functools.partial(_ret_kernel, bb=bb, gamma_c=gamma_c, unroll=unroll, chunk=chunk, chunks=chunks),
        grid=(batch // bb, padded // (chunks * chunk)),
        in_specs=[pl.BlockSpec((bb, chunks * chunk, 4 * RET_WIDTH), lambda i, c: (i, c, 0)), tok,
                  _full((RET_HEADS, chunk, chunk)), _full((RET_HEADS, chunk, RET_HEAD_DIM)),
                  _full((1, RET_WIDTH)), state],
        out_specs=[tok, state],
        out_shape=(jax.ShapeDtypeStruct((batch, padded, RET_WIDTH), BF16),
                   jax.ShapeDtypeStruct(s0.shape, F32)),
        compiler_params=_params("parallel", "arbitrary", fuse=_fuse(6, 0, 1)),
        name="retention",
    )(packed, g, mask, xi, gn_gain, s0)
    return out[:, :length], s_new


def _mixer_out(x_ref, ssm_ref, ret_ref, wout_ref, rows=slice(None)):
    return (x_ref[rows, :] + _dot(ssm_ref[rows, :], wout_ref[:SSM_WIDTH, :].astype(BF16))
            + _dot(ret_ref[rows, :], wout_ref[SSM_WIDTH:, :].astype(BF16)))


def _query(x1, g_ref, wq_ref):
    return (_dot(_rmsnorm(x1, g_ref[...]).astype(BF16), wq_ref[...].astype(BF16)) * MEM_SCALE).astype(BF16)


def _outq_kernel(x_ref, ssm_ref, ret_ref, wout_ref, g_ref, wq_ref, x1_ref, q_ref):
    x1 = _mixer_out(x_ref, ssm_ref, ret_ref, wout_ref)
    x1_ref[...] = x1
    q_ref[...] = _query(x1, g_ref, wq_ref)


def _outq_stream_kernel(x_ref, ssm_ref, ret_ref, wout_hbm, g_ref, wq_hbm, x1_ref, q_ref, wout_buf, wq_buf, sem):
    out_copy = pltpu.make_async_copy(wout_hbm, wout_buf, sem.at[0])
    q_copy = pltpu.make_async_copy(wq_hbm, wq_buf, sem.at[1])
    out_copy.start()
    q_copy.start()
    out_copy.wait()
    x1 = _mixer_out(x_ref, ssm_ref, ret_ref, wout_buf)
    x1_ref[...] = x1
    q_copy.wait()
    q_ref[...] = _query(x1, g_ref, wq_buf)


def _out_and_query_streamed(x, ssm, ret, w_out, g_xattn, w_mq):
    t = x.shape[0]
    hbm = pl.BlockSpec(memory_space=pl.ANY)
    return pl.pallas_call(
        _outq_stream_kernel,
        grid=(1,),
        in_specs=[_full((t, D_MODEL)), _full((t, SSM_WIDTH)), _full((t, RET_WIDTH)), hbm,
                  _full((1, D_MODEL)), hbm],
        out_specs=[_full((t, D_MODEL)), _full((t, D_MODEL))],
        out_shape=(jax.ShapeDtypeStruct((t, D_MODEL), F32), jax.ShapeDtypeStruct((t, D_MODEL), BF16)),
        scratch_shapes=[pltpu.VMEM(w_out.shape, w_out.dtype), pltpu.VMEM(w_mq.shape, w_mq.dtype),
                        pltpu.SemaphoreType.DMA((2,))],
        compiler_params=_params("arbitrary"),
        name="out_proj_query_streamed",
    )(x, ssm, ret, w_out, g_xattn, w_mq)


def _out_and_query(x, ssm, ret, w_out, g_xattn, w_mq):
    t = x.shape[0]
    tm = min(t, ROWS_SMALL)
    row = lambda w: pl.BlockSpec((tm, w), lambda i: (i, 0))
    return pl.pallas_call(
        _outq_kernel,
        grid=(t // tm,),
        in_specs=[row(D_MODEL), row(SSM_WIDTH), row(RET_WIDTH), _full((D_MODEL, D_MODEL)),
                  _full((1, D_MODEL)), _full((D_MODEL, D_MODEL))],
        out_specs=[row(D_MODEL), row(D_MODEL)],
        out_shape=(jax.ShapeDtypeStruct((t, D_MODEL), F32), jax.ShapeDtypeStruct((t, D_MODEL), BF16)),
        compiler_params=_params("parallel", fuse=_fuse(6, 0, 3, 5)),
        name="out_proj_query",
    )(x, ssm, ret, w_out, g_xattn, w_mq)


def _cache_row_order(x):
    halves = MEM_HEAD_DIM // LANES
    tiles = [x[:, (h * halves + d) * LANES:(h * halves + d + 1) * LANES]
             for d in range(halves) for h in range(MEM_HEADS)]
    return jnp.swapaxes(jnp.stack(tiles, axis=0), 0, 1).reshape(x.shape[0] * CACHE_SUB, LANES)


MEMKV_SEQS = 2


def _memkv_kernel(m_ref, g_ref, wk_ref, wv_ref, k_ref, v_ref, kb_ref, vb_ref):
    seqs = m_ref.shape[0]
    m = _rmsnorm(m_ref[...].reshape(seqs * MEM_LEN, D_MODEL), g_ref[...]).astype(BF16)
    k = _dot(m, wk_ref[...])
    v = _dot(m, wv_ref[...])
    k_ref[...] = _cache_row_order(k).reshape(k_ref.shape)
    v_ref[...] = _cache_row_order(v).reshape(v_ref.shape)
    kb_ref[...] = k.astype(BF16).reshape(kb_ref.shape)
    vb_ref[...] = v.astype(BF16).reshape(vb_ref.shape)


def _memory_kv(mem, g_mem, w_mk, w_mv):
    batch = mem.shape[0]
    seqs = MEMKV_SEQS if batch % MEMKV_SEQS == 0 else 1
    tok = pl.BlockSpec((seqs, MEM_LEN, D_MODEL), lambda i: (i, 0, 0))
    cache = pl.BlockSpec((seqs, CACHE_ROWS, LANES), lambda i: (i, 0, 0))
    f = jax.ShapeDtypeStruct((batch, CACHE_ROWS, LANES), F32)
    h = jax.ShapeDtypeStruct((batch, MEM_LEN, D_MODEL), BF16)
    return pl.pallas_call(
        _memkv_kernel,
        grid=(batch // seqs,),
        in_specs=[tok, _full((1, D_MODEL)), _full((D_MODEL, D_MODEL)), _full((D_MODEL, D_MODEL))],
        out_specs=[cache, cache, tok, tok],
        out_shape=(f, f, h, h),
        compiler_params=_params("parallel", fuse=_fuse(4, 2, 3)),
        name="memory_kv",
    )(mem, g_mem, w_mk, w_mv)


ATTN_ROWS = 512


def _outattn_kernel(x_ref, ssm_ref, ret_ref, wout_ref, g_ref, wq_ref, k_ref, v_ref, x1_ref, o_ref):
    cols = [slice(h * MEM_HEAD_DIM, (h + 1) * MEM_HEAD_DIM) for h in range(MEM_HEADS)]
    groups = [slice(r, r + ATTN_ROWS) for r in range(0, x_ref.shape[0], ATTN_ROWS)]

    def scores_of(rows):
        x1 = _mixer_out(x_ref, ssm_ref, ret_ref, wout_ref, rows)
        x1_ref[rows, :] = x1
        q = _query(x1, g_ref, wq_ref)
        return [_dot_nt(q[:, c], k_ref[0, :, c]) for c in cols]

    def attend(rows, scores):
        outs = []
        for s, c in zip(scores, cols):
            e = jnp.exp(s - jnp.max(s, axis=-1, keepdims=True))
            p = (e / jnp.sum(e, axis=-1, keepdims=True)).astype(BF16)
            outs.append(_dot(p, v_ref[0, :, c]).astype(BF16))
        o_ref[rows, :] = jnp.concatenate(outs, axis=1)

    pending = None
    for rows in groups:
        scores = scores_of(rows)
        if pending is not None:
            attend(*pending)
        pending = (rows, scores)
    attend(*pending)


def _out_and_attention(x, ssm, ret, w_out, g_xattn, w_mq, mk, mv, *, length):
    t = x.shape[0]
    tm = min(length, ROWS_LARGE)
    per_seq = length // tm
    row = lambda w: pl.BlockSpec((tm, w), lambda i: (i, 0))
    mem = pl.BlockSpec((1, MEM_LEN, D_MODEL), lambda i: (i // per_seq, 0, 0))
    return pl.pallas_call(
        _outattn_kernel,
        grid=(t // tm,),
        in_specs=[row(D_MODEL), row(SSM_WIDTH), row(RET_WIDTH), _full((D_MODEL, D_MODEL)),
                  _full((1, D_MODEL)), _full((D_MODEL, D_MODEL)), mem, mem],
        out_specs=[row(D_MODEL), row(D_MODEL)],
        out_shape=(jax.ShapeDtypeStruct((t, D_MODEL), F32), jax.ShapeDtypeStruct((t, D_MODEL), BF16)),
        compiler_params=_params("parallel", fuse=_fuse(8, 3, 5)),
        name="out_proj_attention",
    )(x, ssm, ret, w_out, g_xattn, w_mq, mk, mv)


def _cached_attention(q_ref, k_ref, v_ref, o_ref, *, bb, length):
    hit = (lax.broadcasted_iota(jnp.int32, (CACHE_SUB, CACHE_ROWS), 0)
           == lax.broadcasted_iota(jnp.int32, (CACHE_SUB, CACHE_ROWS), 1) % CACHE_SUB)
    first_half = lax.broadcasted_iota(jnp.int32, (length, LANES), 1) % CACHE_SUB < MEM_HEADS
    tiles = CACHE_ROWS // LANES

    def class_reduce(x, op):
        shift = CACHE_SUB
        while shift < LANES:
            x = op(x, pltpu.roll(x, shift, axis=1))
            shift *= 2
        return x

    def softmax_rows(r):
        z = jnp.concatenate(
            [jnp.sum(jnp.where(hit, r[CACHE_SUB * t:CACHE_SUB * (t + 1)], 0.0), axis=0, keepdims=True)
             for t in range(length)], axis=0)
        parts = []
        for i in range(tiles):
            zi = z[:, LANES * i:LANES * (i + 1)]
            parts.append(zi + pltpu.roll(zi, LANES - MEM_HEADS, axis=1))
        mx = parts[0]
        for pi in parts[1:]:
            mx = jnp.maximum(mx, pi)
        mx = class_reduce(mx, jnp.maximum)
        es = [jnp.exp(pi - mx) for pi in parts]
        tot = es[0]
        for ei in es[1:]:
            tot = tot + ei
        tot = class_reduce(tot, jnp.add)
        ps = []
        for ei in es:
            pi = ei / tot
            ps.append(jnp.where(first_half, pi, pltpu.roll(pi, MEM_HEADS, axis=1)))
        p = jnp.concatenate(ps, axis=1)
        return jnp.concatenate(
            [jnp.where(hit, jnp.broadcast_to(p[t:t + 1], (CACHE_SUB, CACHE_ROWS)), 0.0)
             for t in range(length)], axis=0).astype(BF16)

    def start():
        return [_dot_nt(q_ref[b], k_ref[b].astype(BF16)) for b in range(bb)]

    def finish(scores):
        probs = [softmax_rows(r) for r in scores]
        for b in range(bb):
            o_ref[b] = _dot(probs[b], v_ref[b].astype(BF16)).astype(BF16)

    return start, finish


def _to_cache_rows(a, lead):
    halves = MEM_HEAD_DIM // LANES
    batch = a.size // (lead * D_MODEL)
    a = a.reshape(batch, lead, MEM_HEADS, halves, LANES).transpose(0, 1, 3, 2, 4)
    return a.reshape(batch, lead * CACHE_SUB, LANES)


def _from_cache_rows(a, lead):
    halves = MEM_HEAD_DIM // LANES
    batch = a.shape[0]
    a = a.reshape(batch, lead, halves, MEM_HEADS, LANES).transpose(0, 1, 3, 2, 4)
    return a.reshape(batch, lead, D_MODEL)


MLP_CHUNK = 1024


def _post_kernel(x1_ref, o_ref, wo_ref, gm_ref, wup_ref, wdn_ref, gf_ref, *rest, side_bb, side_len):
    if side_bb:
        qs_ref, ck_ref, cv_ref, y_ref, os_ref = rest
    else:
        (y_ref,) = rest
    x2 = x1_ref[...] + _dot(o_ref[...], wo_ref[...].astype(BF16))
    if side_bb:
        side_start, side_finish = _cached_attention(qs_ref, ck_ref, cv_ref, os_ref, bb=side_bb, length=side_len)
        side_scores = side_start()
    h = _rmsnorm(x2, gm_ref[...]).astype(BF16)
    chunks = [slice(c, c + MLP_CHUNK) for c in range(0, D_FF, MLP_CHUNK)]
    acc = x2
    up = _dot(h, wup_ref[:, chunks[0]])
    for c, cols in enumerate(chunks):
        nxt = _dot(h, wup_ref[:, chunks[c + 1]]) if c + 1 < len(chunks) else None
        if side_bb and c == len(chunks) - 1:
            side_finish(side_scores)
        a = jnp.maximum(up, 0.0)
        acc = acc + _dot((a * a).astype(BF16), wdn_ref[cols, :])
        up = nxt
    y_ref[...] = _rmsnorm(acc, gf_ref[...])


def _attn_out_mlp(x1, o, w_mo, g_mlp, w_up, w_down, g_final, side=None):
    t = x1.shape[0]
    tm = min(t, ROWS_SMALL)
    steps = t // tm
    row = pl.BlockSpec((tm, D_MODEL), lambda i: (i, 0))
    in_specs = [row, row, _full((D_MODEL, D_MODEL)), _full((1, D_MODEL)),
                _full((D_MODEL, D_FF)), _full((D_FF, D_MODEL)), _full((1, D_MODEL))]
    out_specs = [row]
    out_shape = [jax.ShapeDtypeStruct((t, D_MODEL), F32)]
    args = [x1, o, w_mo, g_mlp, w_up, w_down, g_final]
    side_bb = side_len = 0
    if side is not None:
        q_rows = side[0]
        side_bb, rem = divmod(q_rows.shape[0], steps)
        if rem or not side_bb:
            raise ValueError("side attention sequences must spread evenly over the grid steps")
        side_len = q_rows.shape[1] // CACHE_SUB
        blk = lambda r: pl.BlockSpec((side_bb, r, LANES), lambda i: (i, 0, 0))
        in_specs += [blk(q_rows.shape[1]), blk(CACHE_ROWS), blk(CACHE_ROWS)]
        out_specs.append(blk(q_rows.shape[1]))
        out_shape.append(jax.ShapeDtypeStruct(q_rows.shape, BF16))
        args += list(side)
    outs = pl.pallas_call(
        functools.partial(_post_kernel, side_bb=side_bb, side_len=side_len),
        grid=(steps,),
        in_specs=in_specs,
        out_specs=out_specs,
        out_shape=out_shape,
        compiler_params=_params("parallel", fuse=_fuse(len(in_specs), 1, 2, 7)),
        name="attn_out_mlp",
    )(*args)
    return outs if side is not None else outs[0]


def _post_stream_kernel(x1_ref, o_ref, wo_ref, gm_ref, wup_hbm, wdn_hbm, gf_ref, y_ref, up_buf, dn_buf, sem):
    chunks = D_FF // MLP_CHUNK

    def copies(c):
        cols = pl.ds(c * MLP_CHUNK, MLP_CHUNK)
        return (pltpu.make_async_copy(wup_hbm.at[:, cols], up_buf.at[c], sem.at[0, c]),
                pltpu.make_async_copy(wdn_hbm.at[cols, :], dn_buf.at[c], sem.at[1, c]))

    for c in range(chunks):
        for cp in copies(c):
            cp.start(priority=c % 2)
    x2 = x1_ref[...] + _dot(o_ref[...], wo_ref[...].astype(BF16))
    h = _rmsnorm(x2, gm_ref[...]).astype(BF16)
    acc = x2
    for c in range(chunks):
        up_copy, dn_copy = copies(c)
        up_copy.wait()
        a = jnp.maximum(_dot(h, up_buf[c]), 0.0)
        dn_copy.wait()
        acc = acc + _dot((a * a).astype(BF16), dn_buf[c])
    y_ref[...] = _rmsnorm(acc, gf_ref[...])


def _attn_out_mlp_streamed(x1, o, w_mo, g_mlp, w_up, w_down, g_final):
    t = x1.shape[0]
    chunks = D_FF // MLP_CHUNK
    hbm = pl.BlockSpec(memory_space=pl.ANY)
    return pl.pallas_call(
        _post_stream_kernel,
        grid=(1,),
        in_specs=[_full((t, D_MODEL)), _full((t, D_MODEL)), _full((D_MODEL, D_MODEL)), _full((1, D_MODEL)),
                  hbm, hbm, _full((1, D_MODEL))],
        out_specs=_full((t, D_MODEL)),
        out_shape=jax.ShapeDtypeStruct((t, D_MODEL), F32),
        scratch_shapes=[pltpu.VMEM((chunks, D_MODEL, MLP_CHUNK), BF16),
                        pltpu.VMEM((chunks, MLP_CHUNK, D_MODEL), BF16),
                        pltpu.SemaphoreType.DMA((2, chunks))],
        compiler_params=_params("arbitrary", fuse=_fuse(7, 1, 2)),
        name="attn_out_mlp_streamed",
    )(x1, o, w_mo, g_mlp, w_up, w_down, g_final)


def _mixers(proj, batch, length, s5_re, s5_im, ret_s, p):
    tokens = batch * length
    u, packed, g = proj
    ssm, s5_re_new, s5_im_new = _s5_mixer(
        u.reshape(batch, length, SSM_WIDTH), s5_re.reshape(batch, SSM_LANES), s5_im.reshape(batch, SSM_LANES),
        p["s5_maps"], p["d_skip"], p["w_glu"], batch=batch, length=length)
    tok3 = lambda a: a.reshape(batch, length, RET_WIDTH)
    ret, ret_new = _retention(packed.reshape(batch, length, 4 * RET_WIDTH), tok3(g), ret_s, p["ret_gn"])
    states = (s5_re_new.reshape(1, batch, SSM_GROUPS, SSM_STATE),
              s5_im_new.reshape(1, batch, SSM_GROUPS, SSM_STATE), ret_new[None])
    return ssm.reshape(tokens, SSM_WIDTH), ret.reshape(tokens, RET_WIDTH), states


def kernel(x_prompt, x_sample, mem_prompt, state_s5_re, state_s5_im, state_ret, cache_mem_k, cache_mem_v, g_mix, w_in, lam_re, lam_im, log_dt, b_re, b_im, c_re, c_im, d_skip, w_glu, ret_gn, w_out, g_xattn, g_mem, w_mq, w_mk, w_mv, w_mo, g_mlp, w_up, w_down, g_final):
    if g_mix.shape[0] != 1:
        raise ValueError("this kernel implements the single-layer configuration")
    bp, lp, _ = x_prompt.shape
    bs, ls, _ = x_sample.shape
    vec = lambda a: a.reshape(1, -1).astype(F32)
    xp = x_prompt.reshape(bp * lp, D_MODEL)
    xs = x_sample.reshape(bs * ls, D_MODEL)
    proj_p, (w_in_b, w_up_b, w_down_b) = _project(
        xp, vec(g_mix[0]), w_in[0], length=lp, pos0=0.0, casts=(w_up[0], w_down[0]))
    w_out_b, w_mq_b, w_mo_b, w_mk_b, w_mv_b, w_glu_b = (
        w[0].astype(BF16) for w in (w_out, w_mq, w_mo, w_mk, w_mv, w_glu))
    proj_s, _ = _project(xs, vec(g_mix[0]), w_in_b, length=ls, pos0=float(PAST_LEN))
    p = dict(
        s5_maps=_s5_block_maps(lam_re[0], lam_im[0], log_dt[0], b_re[0], b_im[0], c_re[0], c_im[0]),
        d_skip=vec(d_skip[0]), w_glu=w_glu_b, ret_gn=vec(ret_gn[0]))
    g_x = vec(g_xattn[0])
    mlp = (w_mo_b, vec(g_mlp[0]), w_up_b, w_down_b, vec(g_final))

    ssm_s, ret_s, states_s = _mixers(proj_s, bs, ls, state_s5_re[0], state_s5_im[0], state_ret[0],
                                     dict(p, w_glu=w_glu[0]))
    outq_s = _out_and_query_streamed if bs * ls <= ROWS_SMALL else _out_and_query
    x1_s, q_s = outq_s(xs, ssm_s, ret_s, w_out[0], g_x, w_mq[0])
    if (ls * CACHE_SUB) % BF16_ROWS:
        raise ValueError("cached attention needs whole bf16 tiles of query rows")
    side = (_to_cache_rows(q_s, ls), _to_cache_rows(cache_mem_k, MEM_LEN), _to_cache_rows(cache_mem_v, MEM_LEN))

    mk_rows, mv_rows, mk_p, mv_p = _memory_kv(mem_prompt, vec(g_mem[0]), w_mk_b, w_mv_b)
    zs = jnp.zeros((bp, SSM_GROUPS, SSM_STATE), F32)
    zr = jnp.zeros((bp, RET_HEADS, RET_HEAD_DIM, RET_HEAD_DIM), F32)
    ssm_p, ret_p, states_p = _mixers(proj_p, bp, lp, zs, zs, zr, p)
    x1_p, o_p = _out_and_attention(xp, ssm_p, ret_p, w_out_b, g_x, w_mq_b, mk_p, mv_p, length=lp)
    y_p, o_s = _attn_out_mlp(x1_p, o_p, *mlp, side=side)
    mlp_s = _attn_out_mlp_streamed if bs * ls <= ROWS_SMALL else _attn_out_mlp
    y_s = mlp_s(x1_s, _from_cache_rows(o_s, ls).reshape(bs * ls, D_MODEL), w_mo[0], *mlp[1:])

    kv5 = lambda a: _from_cache_rows(a, MEM_LEN).reshape(1, bp, MEM_LEN, MEM_HEADS, MEM_HEAD_DIM)
    return (y_p.reshape(bp, lp, D_MODEL), y_s.reshape(bs, ls, D_MODEL), *states_p, kv5(mk_rows), kv5(mv_rows),
            *states_s)
```
